```python
import math
import jax, jax.numpy as jnp
from jax import lax
import numpy as np

D_MODEL = 2048
BATCH = 8
SEQ = 4096
DEPTH = 1

N_META = 16
CHUNK = 64
NORM_EPS = 1e-6
DN_HEADS = D_MODEL // 128
DN_DK = 128
DN_DV = 128
DN_W = DN_HEADS * DN_DV
DN_CONV = 4
M2_HEAD_DIM = 64
M2_HEADS = D_MODEL // M2_HEAD_DIM
M2_GROUPS = 4
M2_HPG = M2_HEADS // M2_GROUPS
M2_STATE = 128
M2_W = M2_HEADS * M2_HEAD_DIM
M2_CONV = 4
D_MIX = DN_W + M2_W
D_IN_PROJ = 4 * DN_W + 2 * DN_HEADS + 2 * M2_W + 2 * M2_GROUPS * M2_STATE + M2_HEADS
D_FF = 11 * D_MODEL // 4
FFN_CONV = 3

kernel_name = "hybrid_gdn_mamba2_convffn_meta"


def rms_norm(x, w):
    xf = x.astype(jnp.float32)
    y = xf * lax.rsqrt(jnp.mean(xf * xf, axis=-1, keepdims=True) + NORM_EPS)
    return (y * w.astype(jnp.float32)).astype(x.dtype)


def causal_dwconv(x, w):
    k, c = w.shape
    return lax.conv_general_dilated(x, w[:, None, :].astype(x.dtype), window_strides=(1,),
                                    padding=[(k - 1, 0)],
                                    dimension_numbers=('NWC', 'WIO', 'NWC'),
                                    feature_group_count=c)


def to_chunks(t):
    pad = [(0, 0), (CHUNK - N_META, 0)] + [(0, 0)] * (t.ndim - 2)
    t = jnp.pad(t, pad)
    b, lp = t.shape[:2]
    t = t.reshape((b, lp // CHUNK, CHUNK) + t.shape[2:])
    return jnp.moveaxis(t, 1, 0)


def from_chunks(t):
    t = jnp.moveaxis(t, 0, 1)
    t = t.reshape((t.shape[0], t.shape[1] * t.shape[2]) + t.shape[3:])
    return t[:, CHUNK - N_META:]


def gated_delta_rule(q, k, v, beta, g):
    causal = jnp.tril(jnp.ones((CHUNK, CHUNK), dtype=bool))
    strict = jnp.tril(jnp.ones((CHUNK, CHUNK), dtype=bool), -1)

    def step(state, inp):
        qc, kc, vc, bc, gc = inp
        gcum = jnp.cumsum(gc, axis=1)
        gh = jnp.swapaxes(gcum, 1, 2)
        decay = jnp.exp(jnp.where(causal, gh[..., :, None] - gh[..., None, :], -jnp.inf))
        kk = jnp.einsum('blhd,bshd->bhls', kc, kc)
        bh = jnp.swapaxes(bc, 1, 2)
        a_mat = jnp.where(strict, bh[..., :, None] * kk * decay, 0.0)
        rhs = jnp.concatenate([vc * bc[..., None], kc * (bc * jnp.exp(gcum))[..., None]], axis=-1)
        rhs = jnp.swapaxes(rhs, 1, 2)
        sol = lax.linalg.triangular_solve(a_mat, rhs, left_side=True, lower=True,
                                          unit_diagonal=True)
        u, w = sol[..., :DN_DV], sol[..., DN_DV:]
        v_new = u - jnp.einsum('bhlk,bhkv->bhlv', w, state)
        o_inter = jnp.einsum('blhk,bhkv->bhlv', qc * jnp.exp(gcum)[..., None], state)
        qk = jnp.einsum('blhd,bshd->bhls', qc, kc) * decay
        o = o_inter + jnp.einsum('bhls,bhsv->bhlv', qk, v_new)
        g_last = gcum[:, -1:, :]
        state = (state * jnp.exp(g_last[:, 0])[..., None, None]
                 + jnp.einsum('bshk,bhsv->bhkv', kc * jnp.exp(g_last - gcum)[..., None], v_new))
        return state, jnp.swapaxes(o, 1, 2)

    b = q.shape[0]
    s0 = jnp.zeros((b, DN_HEADS, DN_DK, DN_DV), jnp.float32)
    _, ys = lax.scan(step, s0, (to_chunks(q), to_chunks(k), to_chunks(v),
                                to_chunks(beta), to_chunks(g)))
    return from_chunks(ys)


def ssd_chunked(xs, dt, a, bm, cm):
    causal = jnp.tril(jnp.ones((CHUNK, CHUNK), dtype=bool))[None, :, :, None, None]

    def step(state, inp):
        xc, dtc, ac, bc, cc = inp
        acs = jnp.cumsum(ac, axis=1)
        lmat = jnp.exp(jnp.where(causal, acs[:, :, None] - acs[:, None, :], -jnp.inf))
        xdt = xc * dtc[..., None]
        cb = jnp.einsum('blgn,bsgn->blsg', cc, bc)
        y_diag = jnp.einsum('blsg,blsgr,bsgrp->blgrp', cb, lmat, xdt)
        y_off = jnp.einsum('blgn,bgrpn->blgrp', cc, state) * jnp.exp(acs)[..., None]
        a_last = acs[:, -1]
        state = (state * jnp.exp(a_last)[..., None, None]
                 + jnp.einsum('bsgn,bsgr,bsgrp->bgrpn', bc, jnp.exp(a_last[:, None] - acs), xdt))
        return state, y_diag + y_off

    b = xs.shape[0]
    s0 = jnp.zeros((b, M2_GROUPS, M2_HPG, M2_HEAD_DIM, M2_STATE), jnp.float32)
    _, ys = lax.scan(step, s0, (to_chunks(xs), to_chunks(dt), to_chunks(a),
                                to_chunks(bm), to_chunks(cm)))
    return from_chunks(ys)


def hybrid_mixer(h, w_in, dn_conv_w, dn_a_log, dn_dt_bias, dn_norm_w,
                 m2_conv_w, m2_conv_b, m2_a_log, m2_dt_bias, m2_d, m2_norm_w, w_out):
    b, l, _ = h.shape
    f32 = jnp.float32
    proj = h @ w_in
    sizes = [3 * DN_W, DN_W, DN_HEADS, DN_HEADS, M2_W, M2_W + 2 * M2_GROUPS * M2_STATE]
    dn_qkv, dn_z, dn_b, dn_a, m2_z, m2_xbc, m2_dt = jnp.split(proj, list(np.cumsum(sizes)), axis=-1)

    qkv = jax.nn.silu(causal_dwconv(dn_qkv, dn_conv_w)).astype(f32)
    q, k, v = jnp.split(qkv, 3, axis=-1)
    q = q.reshape(b, l, DN_HEADS, DN_DK)
    k = k.reshape(b, l, DN_HEADS, DN_DK)
    v = v.reshape(b, l, DN_HEADS, DN_DV)
    q = q * lax.rsqrt(jnp.sum(q * q, -1, keepdims=True) + NORM_EPS) * (DN_DK ** -0.5)
    k = k * lax.rsqrt(jnp.sum(k * k, -1, keepdims=True) + NORM_EPS)
    beta = jax.nn.sigmoid(dn_b.astype(f32))
    g = -jnp.exp(dn_a_log.astype(f32)) * jax.nn.softplus(dn_a.astype(f32) + dn_dt_bias.astype(f32))
    o = gated_delta_rule(q, k, v, beta, g)
    z = dn_z.astype(f32).reshape(b, l, DN_HEADS, DN_DV)
    o = (o * lax.rsqrt(jnp.mean(o * o, -1, keepdims=True) + NORM_EPS)
         * dn_norm_w.astype(f32) * jax.nn.silu(z)).reshape(b, l, DN_W)

    xbc = jax.nn.silu(causal_dwconv(m2_xbc, m2_conv_w) + m2_conv_b.astype(h.dtype)).astype(f32)
    xs, bm, cm = jnp.split(xbc, [M2_W, M2_W + M2_GROUPS * M2_STATE], axis=-1)
    xs = xs.reshape(b, l, M2_GROUPS, M2_HPG, M2_HEAD_DIM)
    bm = bm.reshape(b, l, M2_GROUPS, M2_STATE)
    cm = cm.reshape(b, l, M2_GROUPS, M2_STATE)
    dt = jax.nn.softplus(m2_dt.astype(f32) + m2_dt_bias.astype(f32)).reshape(b, l, M2_GROUPS, M2_HPG)
    a_head = (-jnp.exp(m2_a_log.astype(f32))).reshape(M2_GROUPS, M2_HPG)
    y = ssd_chunked(xs, dt, dt * a_head, bm, cm)
    y = y + m2_d.astype(f32).reshape(M2_GROUPS, M2_HPG)[:, :, None] * xs
    y = y.reshape(b, l, M2_W) * jax.nn.silu(m2_z.astype(f32))
    y = y.reshape(b, l, M2_GROUPS, M2_W // M2_GROUPS)
    y = (y * lax.rsqrt(jnp.mean(y * y, -1, keepdims=True) + NORM_EPS)).reshape(b, l, M2_W)
    y = y * m2_norm_w.astype(f32)

    mixed = jnp.concatenate([o, y], axis=-1).astype(h.dtype)
    return mixed @ w_out


def conv_ffn(h, w_up, conv_w, w_down):
    u = causal_dwconv(h @ w_up, conv_w)
    gate, val = jnp.split(u, 2, axis=-1)
    return (jax.nn.silu(gate) * val) @ w_down


def _fwd_setup_inputs(seed: int = 0) -> dict:
    key = jax.random.key(seed)
    ks = jax.random.split(key, 24)
    nrm = jax.random.normal

    def dt_bias(k, n):
        dt = jnp.exp(jax.random.uniform(k, (DEPTH, n), minval=math.log(1e-3), maxval=math.log(1e-1)))
        return dt + jnp.log(-jnp.expm1(-dt))

    def a_log(k, n):
        return jnp.log(jax.random.uniform(k, (DEPTH, n), minval=1.0, maxval=16.0))

    return {
        "x": nrm(ks[0], (BATCH, SEQ, D_MODEL), jnp.float32),
        "meta_tokens": nrm(ks[1], (N_META, D_MODEL), jnp.float32),
        "norm_mix_w": 1.0 + 0.02 * nrm(ks[2], (DEPTH, D_MODEL), jnp.float32),
        "w_in": nrm(ks[3], (DEPTH, D_MODEL, D_IN_PROJ), jnp.float32) * D_MODEL ** -0.5,
        "dn_conv_w": nrm(ks[4], (DEPTH, DN_CONV, 3 * DN_W), jnp.float32) * DN_CONV ** -0.5,
        "dn_a_log": a_log(ks[5], DN_HEADS),
        "dn_dt_bias": dt_bias(ks[6], DN_HEADS),
        "dn_norm_w": 1.0 + 0.02 * nrm(ks[7], (DEPTH, DN_DV), jnp.float32),
        "m2_conv_w": nrm(ks[8], (DEPTH, M2_CONV, M2_W + 2 * M2_GROUPS * M2_STATE), jnp.float32) * M2_CONV ** -0.5,
        "m2_conv_b": 0.02 * nrm(ks[9], (DEPTH, M2_W + 2 * M2_GROUPS * M2_STATE), jnp.float32),
        "m2_a_log": a_log(ks[10], M2_HEADS),
        "m2_dt_bias": dt_bias(ks[11], M2_HEADS),
        "m2_d": 1.0 + 0.1 * nrm(ks[12], (DEPTH, M2_HEADS), jnp.float32),
        "m2_norm_w": 1.0 + 0.02 * nrm(ks[13], (DEPTH, M2_W), jnp.float32),
        "w_out": nrm(ks[14], (DEPTH, D_MIX, D_MODEL), jnp.float32) * D_MIX ** -0.5,
        "norm_ffn_w": 1.0 + 0.02 * nrm(ks[15], (DEPTH, D_MODEL), jnp.float32),
        "ffn_up": nrm(ks[16], (DEPTH, D_MODEL, 2 * D_FF), jnp.float32) * D_MODEL ** -0.5,
        "ffn_conv_w": nrm(ks[17], (DEPTH, FFN_CONV, 2 * D_FF), jnp.float32) * FFN_CONV ** -0.5,
        "ffn_down": nrm(ks[18], (DEPTH, D_FF, D_MODEL), jnp.float32) * D_FF ** -0.5,
        "norm_final_w": 1.0 + 0.02 * nrm(ks[19], (D_MODEL,), jnp.float32),
    }


def _fwd_reference(x, meta_tokens, norm_mix_w, w_in, dn_conv_w, dn_a_log, dn_dt_bias, dn_norm_w,
              m2_conv_w, m2_conv_b, m2_a_log, m2_dt_bias, m2_d, m2_norm_w, w_out,
              norm_ffn_w, ffn_up, ffn_conv_w, ffn_down, norm_final_w):
    b = x.shape[0]
    meta = jnp.broadcast_to(meta_tokens[None].astype(x.dtype), (b, N_META, D_MODEL))
    h = jnp.concatenate([meta, x], axis=1)
    for layer in range(DEPTH):
        h = h + hybrid_mixer(rms_norm(h, norm_mix_w[layer]), w_in[layer], dn_conv_w[layer],
                             dn_a_log[layer], dn_dt_bias[layer], dn_norm_w[layer],
                             m2_conv_w[layer], m2_conv_b[layer], m2_a_log[layer],
                             m2_dt_bias[layer], m2_d[layer], m2_norm_w[layer], w_out[layer])
        h = h + conv_ffn(rms_norm(h, norm_ffn_w[layer]), ffn_up[layer], ffn_conv_w[layer],
                         ffn_down[layer])
    return rms_norm(h, norm_final_w)[:, N_META:]


import jax as _jax
import jax.numpy as _jnp

TWIN_FORMAT = 'train_step'
FWD_PARAMS = ['x', 'meta_tokens', 'norm_mix_w', 'w_in', 'dn_conv_w', 'dn_a_log', 'dn_dt_bias', 'dn_norm_w', 'm2_conv_w', 'm2_conv_b', 'm2_a_log', 'm2_dt_bias', 'm2_d', 'm2_norm_w', 'w_out', 'norm_ffn_w', 'ffn_up', 'ffn_conv_w', 'ffn_down', 'norm_final_w']
TWIN_WEIGHTS = ['meta_tokens', 'norm_mix_w', 'w_in', 'dn_conv_w', 'dn_a_log', 'dn_dt_bias', 'dn_norm_w', 'm2_conv_w', 'm2_conv_b', 'm2_a_log', 'm2_dt_bias', 'm2_d', 'm2_norm_w', 'w_out', 'norm_ffn_w', 'ffn_up', 'ffn_conv_w', 'ffn_down', 'norm_final_w']
TWIN_DIFF_INPUT = 'x'
TWIN_INPUTS = ['x', 'meta_tokens', 'norm_mix_w', 'w_in', 'dn_conv_w', 'dn_a_log', 'dn_dt_bias', 'dn_norm_w', 'm2_conv_w', 'm2_conv_b', 'm2_a_log', 'm2_dt_bias', 'm2_d', 'm2_norm_w', 'w_out', 'norm_ffn_w', 'ffn_up', 'ffn_conv_w', 'ffn_down', 'norm_final_w', 'loss_target', 'm_meta_tokens', 'm_norm_mix_w', 'm_w_in', 'm_dn_conv_w', 'm_dn_a_log', 'm_dn_dt_bias', 'm_dn_norm_w', 'm_m2_conv_w', 'm_m2_conv_b', 'm_m2_a_log', 'm_m2_dt_bias', 'm_m2_d', 'm_m2_norm_w', 'm_w_out', 'm_norm_ffn_w', 'm_ffn_up', 'm_ffn_conv_w', 'm_ffn_down', 'm_norm_final_w', 'v_meta_tokens', 'v_norm_mix_w', 'v_w_in', 'v_dn_conv_w', 'v_dn_a_log', 'v_dn_dt_bias', 'v_dn_norm_w', 'v_m2_conv_w', 'v_m2_conv_b', 'v_m2_a_log', 'v_m2_dt_bias', 'v_m2_d', 'v_m2_norm_w', 'v_w_out', 'v_norm_ffn_w', 'v_ffn_up', 'v_ffn_conv_w', 'v_ffn_down', 'v_norm_final_w']
TWIN_OUTPUTS = ['loss', 'grad_x', 'grad_meta_tokens', 'grad_norm_mix_w', 'grad_w_in', 'grad_dn_conv_w', 'grad_dn_a_log', 'grad_dn_dt_bias', 'grad_dn_norm_w', 'grad_m2_conv_w', 'grad_m2_conv_b', 'grad_m2_a_log', 'grad_m2_dt_bias', 'grad_m2_d', 'grad_m2_norm_w', 'grad_w_out', 'grad_norm_ffn_w', 'grad_ffn_up', 'grad_ffn_conv_w', 'grad_ffn_down', 'grad_norm_final_w', 'delta_meta_tokens', 'delta_norm_mix_w', 'delta_w_in', 'delta_dn_conv_w', 'delta_dn_a_log', 'delta_dn_dt_bias', 'delta_dn_norm_w', 'delta_m2_conv_w', 'delta_m2_conv_b', 'delta_m2_a_log', 'delta_m2_dt_bias', 'delta_m2_d', 'delta_m2_norm_w', 'delta_w_out', 'delta_norm_ffn_w', 'delta_ffn_up', 'delta_ffn_conv_w', 'delta_ffn_down', 'delta_norm_final_w', 'new_m_meta_tokens', 'new_m_norm_mix_w', 'new_m_w_in', 'new_m_dn_conv_w', 'new_m_dn_a_log', 'new_m_dn_dt_bias', 'new_m_dn_norm_w', 'new_m_m2_conv_w', 'new_m_m2_conv_b', 'new_m_m2_a_log', 'new_m_m2_dt_bias', 'new_m_m2_d', 'new_m_m2_norm_w', 'new_m_w_out', 'new_m_norm_ffn_w', 'new_m_ffn_up', 'new_m_ffn_conv_w', 'new_m_ffn_down', 'new_m_norm_final_w', 'new_v_meta_tokens', 'new_v_norm_mix_w', 'new_v_w_in', 'new_v_dn_conv_w', 'new_v_dn_a_log', 'new_v_dn_dt_bias', 'new_v_dn_norm_w', 'new_v_m2_conv_w', 'new_v_m2_conv_b', 'new_v_m2_a_log', 'new_v_m2_dt_bias', 'new_v_m2_d', 'new_v_m2_norm_w', 'new_v_w_out', 'new_v_norm_ffn_w', 'new_v_ffn_up', 'new_v_ffn_conv_w', 'new_v_ffn_down', 'new_v_norm_final_w']
TWIN_LEAF_KINDS = {'loss': 'loss', 'grad_x': 'grad_x', 'grad_meta_tokens': 'grad_w', 'grad_norm_mix_w': 'grad_w', 'grad_w_in': 'grad_w', 'grad_dn_conv_w': 'grad_w', 'grad_dn_a_log': 'grad_w', 'grad_dn_dt_bias': 'grad_w', 'grad_dn_norm_w': 'grad_w', 'grad_m2_conv_w': 'grad_w', 'grad_m2_conv_b': 'grad_w', 'grad_m2_a_log': 'grad_w', 'grad_m2_dt_bias': 'grad_w', 'grad_m2_d': 'grad_w', 'grad_m2_norm_w': 'grad_w', 'grad_w_out': 'grad_w', 'grad_norm_ffn_w': 'grad_w', 'grad_ffn_up': 'grad_w', 'grad_ffn_conv_w': 'grad_w', 'grad_ffn_down': 'grad_w', 'grad_norm_final_w': 'grad_w', 'delta_meta_tokens': 'delta_w', 'delta_norm_mix_w': 'delta_w', 'delta_w_in': 'delta_w', 'delta_dn_conv_w': 'delta_w', 'delta_dn_a_log': 'delta_w', 'delta_dn_dt_bias': 'delta_w', 'delta_dn_norm_w': 'delta_w', 'delta_m2_conv_w': 'delta_w', 'delta_m2_conv_b': 'delta_w', 'delta_m2_a_log': 'delta_w', 'delta_m2_dt_bias': 'delta_w', 'delta_m2_d': 'delta_w', 'delta_m2_norm_w': 'delta_w', 'delta_w_out': 'delta_w', 'delta_norm_ffn_w': 'delta_w', 'delta_ffn_up': 'delta_w', 'delta_ffn_conv_w': 'delta_w', 'delta_ffn_down': 'delta_w', 'delta_norm_final_w': 'delta_w', 'new_m_meta_tokens': 'new_m', 'new_m_norm_mix_w': 'new_m', 'new_m_w_in': 'new_m', 'new_m_dn_conv_w': 'new_m', 'new_m_dn_a_log': 'new_m', 'new_m_dn_dt_bias': 'new_m', 'new_m_dn_norm_w': 'new_m', 'new_m_m2_conv_w': 'new_m', 'new_m_m2_conv_b': 'new_m', 'new_m_m2_a_log': 'new_m', 'new_m_m2_dt_bias': 'new_m', 'new_m_m2_d': 'new_m', 'new_m_m2_norm_w': 'new_m', 'new_m_w_out': 'new_m', 'new_m_norm_ffn_w': 'new_m', 'new_m_ffn_up': 'new_m', 'new_m_ffn_conv_w': 'new_m', 'new_m_ffn_down': 'new_m', 'new_m_norm_final_w': 'new_m', 'new_v_meta_tokens': 'new_v', 'new_v_norm_mix_w': 'new_v', 'new_v_w_in': 'new_v', 'new_v_dn_conv_w': 'new_v', 'new_v_dn_a_log': 'new_v', 'new_v_dn_dt_bias': 'new_v', 'new_v_dn_norm_w': 'new_v', 'new_v_m2_conv_w': 'new_v', 'new_v_m2_conv_b': 'new_v', 'new_v_m2_a_log': 'new_v', 'new_v_m2_dt_bias': 'new_v', 'new_v_m2_d': 'new_v', 'new_v_m2_norm_w': 'new_v', 'new_v_w_out': 'new_v', 'new_v_norm_ffn_w': 'new_v', 'new_v_ffn_up': 'new_v', 'new_v_ffn_conv_w': 'new_v', 'new_v_ffn_down': 'new_v', 'new_v_norm_final_w': 'new_v'}


def _forward(args):
    return _fwd_reference(*[args[k] for k in FWD_PARAMS])


def _output_shape():
    out = _jax.eval_shape(lambda: _forward(_fwd_setup_inputs(0)))
    return out.shape, out.dtype

N_MICROBATCH = 1
ADAM_LR = 0.001
ADAM_B1 = 0.9
ADAM_B2 = 0.999
ADAM_EPS = 1e-08
ADAM_WD = 0.01
ADAM_STEP = 10
PER_EXAMPLE_BATCH_AXIS = {'x': 0, 'loss_target': 0}
SHARED_INPUTS = []
_WEIGHT_DTYPES = {'meta_tokens': _jnp.float32, 'norm_mix_w': _jnp.float32, 'w_in': _jnp.float32, 'dn_conv_w': _jnp.float32, 'dn_a_log': _jnp.float32, 'dn_dt_bias': _jnp.float32, 'dn_norm_w': _jnp.float32, 'm2_conv_w': _jnp.float32, 'm2_conv_b': _jnp.float32, 'm2_a_log': _jnp.float32, 'm2_dt_bias': _jnp.float32, 'm2_d': _jnp.float32, 'm2_norm_w': _jnp.float32, 'w_out': _jnp.float32, 'norm_ffn_w': _jnp.float32, 'ffn_up': _jnp.float32, 'ffn_conv_w': _jnp.float32, 'ffn_down': _jnp.float32, 'norm_final_w': _jnp.float32}
MOMENT_SCALE = {'meta_tokens': 2.157219e-03, 'norm_mix_w': 9.715325e-02, 'w_in': 3.703126e-02, 'dn_conv_w': 2.386173e-02, 'dn_a_log': 1.553616e-01, 'dn_dt_bias': 1.477903e-01, 'dn_norm_w': 1.353520e-01, 'm2_conv_w': 4.733171e-02, 'm2_conv_b': 6.103589e-02, 'm2_a_log': 1.502275e-01, 'm2_dt_bias': 1.012813e-01, 'm2_d': 5.491286e-01, 'm2_norm_w': 5.482815e-02, 'w_out': 6.159232e-02, 'norm_ffn_w': 5.491568e-02, 'ffn_up': 2.321660e-02, 'ffn_conv_w': 2.386427e-02, 'ffn_down': 3.792685e-02, 'norm_final_w': 1.599403e+01}


def _to_microbatches(a, axis):
    t = _jnp.moveaxis(a, axis, 0)
    t = t.reshape((N_MICROBATCH, t.shape[0] // N_MICROBATCH) + t.shape[1:])
    return _jnp.moveaxis(t, 1, axis + 1)


def setup_inputs(seed: int = 0) -> dict:
    inp = _fwd_setup_inputs(seed)
    key = _jax.random.fold_in(_jax.random.key(seed), 7919)
    shape, _ = _output_shape()
    out = dict(inp)
    out["loss_target"] = _jax.random.normal(_jax.random.fold_in(key, 0), shape, _jnp.float32)
    for i, name in enumerate(TWIN_WEIGHTS):
        w = inp[name].astype(_jnp.float32)
        if MOMENT_SCALE is None:
            s = _jnp.sqrt(_jnp.mean(_jnp.square(w)) + 1e-30)
        else:
            s = MOMENT_SCALE[name]
        km, kv = _jax.random.split(_jax.random.fold_in(key, i + 1))
        out[name] = w
        out["m_" + name] = s * _jax.random.normal(km, w.shape, _jnp.float32)
        out["v_" + name] = (s * s) * _jax.random.uniform(kv, w.shape, _jnp.float32, 0.5, 1.5)
    if N_MICROBATCH > 1:
        for name, axis in PER_EXAMPLE_BATCH_AXIS.items():
            out[name] = _to_microbatches(out[name], axis)
    return {'x': out['x'], 'meta_tokens': out['meta_tokens'], 'norm_mix_w': out['norm_mix_w'], 'w_in': out['w_in'], 'dn_conv_w': out['dn_conv_w'], 'dn_a_log': out['dn_a_log'], 'dn_dt_bias': out['dn_dt_bias'], 'dn_norm_w': out['dn_norm_w'], 'm2_conv_w': out['m2_conv_w'], 'm2_conv_b': out['m2_conv_b'], 'm2_a_log': out['m2_a_log'], 'm2_dt_bias': out['m2_dt_bias'], 'm2_d': out['m2_d'], 'm2_norm_w': out['m2_norm_w'], 'w_out': out['w_out'], 'norm_ffn_w': out['norm_ffn_w'], 'ffn_up': out['ffn_up'], 'ffn_conv_w': out['ffn_conv_w'], 'ffn_down': out['ffn_down'], 'norm_final_w': out['norm_final_w'], 'loss_target': out['loss_target'], 'm_meta_tokens': out['m_meta_tokens'], 'm_norm_mix_w': out['m_norm_mix_w'], 'm_w_in': out['m_w_in'], 'm_dn_conv_w': out['m_dn_conv_w'], 'm_dn_a_log': out['m_dn_a_log'], 'm_dn_dt_bias': out['m_dn_dt_bias'], 'm_dn_norm_w': out['m_dn_norm_w'], 'm_m2_conv_w': out['m_m2_conv_w'], 'm_m2_conv_b': out['m_m2_conv_b'], 'm_m2_a_log': out['m_m2_a_log'], 'm_m2_dt_bias': out['m_m2_dt_bias'], 'm_m2_d': out['m_m2_d'], 'm_m2_norm_w': out['m_m2_norm_w'], 'm_w_out': out['m_w_out'], 'm_norm_ffn_w': out['m_norm_ffn_w'], 'm_ffn_up': out['m_ffn_up'], 'm_ffn_conv_w': out['m_ffn_conv_w'], 'm_ffn_down': out['m_ffn_down'], 'm_norm_final_w': out['m_norm_final_w'], 'v_meta_tokens': out['v_meta_tokens'], 'v_norm_mix_w': out['v_norm_mix_w'], 'v_w_in': out['v_w_in'], 'v_dn_conv_w': out['v_dn_conv_w'], 'v_dn_a_log': out['v_dn_a_log'], 'v_dn_dt_bias': out['v_dn_dt_bias'], 'v_dn_norm_w': out['v_dn_norm_w'], 'v_m2_conv_w': out['v_m2_conv_w'], 'v_m2_conv_b': out['v_m2_conv_b'], 'v_m2_a_log': out['v_m2_a_log'], 'v_m2_dt_bias': out['v_m2_dt_bias'], 'v_m2_d': out['v_m2_d'], 'v_m2_norm_w': out['v_m2_norm_w'], 'v_w_out': out['v_w_out'], 'v_norm_ffn_w': out['v_norm_ffn_w'], 'v_ffn_up': out['v_ffn_up'], 'v_ffn_conv_w': out['v_ffn_conv_w'], 'v_ffn_down': out['v_ffn_down'], 'v_norm_final_w': out['v_norm_final_w']}


def _loss(weights, diff, rest, loss_target):
    with _jax.named_scope("forward"):
        args = {**rest, TWIN_DIFF_INPUT: diff, **{k: w.astype(_WEIGHT_DTYPES[k]) for k, w in weights.items()}}
        y = _forward(args)
    with _jax.named_scope("loss_head"):
        err = _jnp.square(y.astype(_jnp.float32) - loss_target)
        return 0.5 * _jnp.sum(_jnp.mean(err, axis=-1)) if err.ndim else 0.5 * err


def _adamw(w, g, m, v):
    m = ADAM_B1 * m + (1.0 - ADAM_B1) * g
    v = ADAM_B2 * v + (1.0 - ADAM_B2) * _jnp.square(g)
    m_hat = m / (1.0 - ADAM_B1 ** ADAM_STEP)
    v_hat = v / (1.0 - ADAM_B2 ** ADAM_STEP)
    delta = -ADAM_LR * (m_hat / (_jnp.sqrt(v_hat) + ADAM_EPS) + ADAM_WD * w)
    return delta, m, v


def reference(x, meta_tokens, norm_mix_w, w_in, dn_conv_w, dn_a_log, dn_dt_bias, dn_norm_w, m2_conv_w, m2_conv_b, m2_a_log, m2_dt_bias, m2_d, m2_norm_w, w_out, norm_ffn_w, ffn_up, ffn_conv_w, ffn_down, norm_final_w, loss_target, m_meta_tokens, m_norm_mix_w, m_w_in, m_dn_conv_w, m_dn_a_log, m_dn_dt_bias, m_dn_norm_w, m_m2_conv_w, m_m2_conv_b, m_m2_a_log, m_m2_dt_bias, m_m2_d, m_m2_norm_w, m_w_out, m_norm_ffn_w, m_ffn_up, m_ffn_conv_w, m_ffn_down, m_norm_final_w, v_meta_tokens, v_norm_mix_w, v_w_in, v_dn_conv_w, v_dn_a_log, v_dn_dt_bias, v_dn_norm_w, v_m2_conv_w, v_m2_conv_b, v_m2_a_log, v_m2_dt_bias, v_m2_d, v_m2_norm_w, v_w_out, v_norm_ffn_w, v_ffn_up, v_ffn_conv_w, v_ffn_down, v_norm_final_w):
    given = dict(x=x, meta_tokens=meta_tokens, norm_mix_w=norm_mix_w, w_in=w_in, dn_conv_w=dn_conv_w, dn_a_log=dn_a_log, dn_dt_bias=dn_dt_bias, dn_norm_w=dn_norm_w, m2_conv_w=m2_conv_w, m2_conv_b=m2_conv_b, m2_a_log=m2_a_log, m2_dt_bias=m2_dt_bias, m2_d=m2_d, m2_norm_w=m2_norm_w, w_out=w_out, norm_ffn_w=norm_ffn_w, ffn_up=ffn_up, ffn_conv_w=ffn_conv_w, ffn_down=ffn_down, norm_final_w=norm_final_w, loss_target=loss_target, m_meta_tokens=m_meta_tokens, m_norm_mix_w=m_norm_mix_w, m_w_in=m_w_in, m_dn_conv_w=m_dn_conv_w, m_dn_a_log=m_dn_a_log, m_dn_dt_bias=m_dn_dt_bias, m_dn_norm_w=m_dn_norm_w, m_m2_conv_w=m_m2_conv_w, m_m2_conv_b=m_m2_conv_b, m_m2_a_log=m_m2_a_log, m_m2_dt_bias=m_m2_dt_bias, m_m2_d=m_m2_d, m_m2_norm_w=m_m2_norm_w, m_w_out=m_w_out, m_norm_ffn_w=m_norm_ffn_w, m_ffn_up=m_ffn_up, m_ffn_conv_w=m_ffn_conv_w, m_ffn_down=m_ffn_down, m_norm_final_w=m_norm_final_w, v_meta_tokens=v_meta_tokens, v_norm_mix_w=v_norm_mix_w, v_w_in=v_w_in, v_dn_conv_w=v_dn_conv_w, v_dn_a_log=v_dn_a_log, v_dn_dt_bias=v_dn_dt_bias, v_dn_norm_w=v_dn_norm_w, v_m2_conv_w=v_m2_conv_w, v_m2_conv_b=v_m2_conv_b, v_m2_a_log=v_m2_a_log, v_m2_dt_bias=v_m2_dt_bias, v_m2_d=v_m2_d, v_m2_norm_w=v_m2_norm_w, v_w_out=v_w_out, v_norm_ffn_w=v_norm_ffn_w, v_ffn_up=v_ffn_up, v_ffn_conv_w=v_ffn_conv_w, v_ffn_down=v_ffn_down, v_norm_final_w=v_norm_final_w)
    weights = {n: given[n] for n in TWIN_WEIGHTS}
    shared = {n: given[n] for n in SHARED_INPUTS}
    per_example = {n: given[n] for n in ['x']}
    grad_fn = _jax.value_and_grad(_loss, argnums=(0, 1))

    def one_microbatch(ex, loss_target):
        ex = dict(ex)
        diff = ex.pop(TWIN_DIFF_INPUT)
        return grad_fn(weights, diff, {**shared, **ex}, loss_target)

    if N_MICROBATCH == 1:
        loss, (grad_w, grad_x) = one_microbatch(per_example, given["loss_target"])
    else:
        def body(carry, xs):
            loss_sum, grad_sum = carry
            l_k, (gw_k, gx_k) = one_microbatch(xs[0], xs[1])
            with _jax.named_scope("update"):
                return (loss_sum + l_k, _jax.tree.map(_jnp.add, grad_sum, gw_k)), gx_k

        init = (_jnp.zeros((), _jnp.float32), _jax.tree.map(_jnp.zeros_like, weights))
        (loss, grad_w), grad_x = _jax.lax.scan(body, init, (per_example, given["loss_target"]))
    with _jax.named_scope("update"):
        delta_w, new_m, new_v = {}, {}, {}
        for n in TWIN_WEIGHTS:
            delta_w[n], new_m[n], new_v[n] = _adamw(weights[n], grad_w[n], given["m_" + n], given["v_" + n])
    return (loss, grad_x, *[grad_w[n] for n in TWIN_WEIGHTS], *[delta_w[n] for n in TWIN_WEIGHTS],
            *[new_m[n] for n in TWIN_WEIGHTS], *[new_v[n] for n in TWIN_WEIGHTS])
```

```python
import functools

import jax
import jax.numpy as jnp
from jax import lax
from jax.experimental import pallas as pl
from jax.experimental.pallas import tpu as pltpu

F32 = jnp.float32
MXU = jnp.bfloat16
WIRE = jnp.bfloat16
HI = lax.Precision.HIGHEST

NDEV = 8
CH = 64
NMETA = 16
PADR = CH - NMETA
EPS = 1e-6
HD = 128
M2P = 64
M2G = 4
NST = 128
LANE = 128

ADAM_LR, ADAM_B1, ADAM_B2, ADAM_EPS, ADAM_WD, ADAM_STEP = 0.001, 0.9, 0.999, 1e-08, 0.01, 10

MESH_AXES = ("x", "y", "c")


def _pick(n, target, mult=16):
    best = None
    for t in range(mult, min(n, target) + 1, mult):
        if n % t == 0:
            best = t
    return best if best is not None else n


def _dg(a, b, ca, cb):
    return lax.dot_general(a.astype(MXU), b.astype(MXU), (((ca,), (cb,)), ((), ())), preferred_element_type=F32)


def _dgh(a, b, ca, cb):
    return lax.dot_general(a, b, (((ca,), (cb,)), ((), ())), precision=HI, preferred_element_type=F32)


def _silu(x):
    return x * jax.nn.sigmoid(x)


def _softplus(x):
    return jnp.maximum(x, 0.0) + jnp.log1p(jnp.exp(-jnp.abs(x)))


def _rms(x, w):
    return x * lax.rsqrt(jnp.mean(x * x, axis=-1, keepdims=True) + EPS) * w


def _cparams(sem, vmem_mb):
    return pltpu.CompilerParams(dimension_semantics=sem, vmem_limit_bytes=vmem_mb << 20)


def _mm(name, a, b, *, ta=False, tb=False, add=None, out_dtype=F32, tm=1024, tn=512, tk=2048):
    m, kdim = (a.shape[1], a.shape[0]) if ta else a.shape
    n = b.shape[0] if tb else b.shape[1]
    tm = _pick(m, tm, 128 if ta else 16)
    tn = _pick(n, tn, 128)
    tk = _pick(kdim, tk, 16 if (ta and not tb) else 128)
    nk = kdim // tk
    ca, cb = (0 if ta else 1), (1 if tb else 0)

    def body(*refs):
        a_ref, b_ref = refs[0], refs[1]
        add_ref = refs[2] if add is not None else None
        o_ref, acc = refs[-2], refs[-1]
        k = pl.program_id(2)

        @pl.when(k == 0)
        def _():
            acc[...] = jnp.zeros_like(acc)

        acc[...] += _dg(a_ref[...], b_ref[...], ca, cb)

        @pl.when(k == nk - 1)
        def _():
            r = acc[...]
            if add_ref is not None:
                r = r + add_ref[...].astype(F32)
            o_ref[...] = r.astype(o_ref.dtype)

    a_spec = pl.BlockSpec((tk, tm), lambda i, j, k: (k, i)) if ta else pl.BlockSpec((tm, tk), lambda i, j, k: (i, k))
    b_spec = pl.BlockSpec((tn, tk), lambda i, j, k: (j, k)) if tb else pl.BlockSpec((tk, tn), lambda i, j, k: (k, j))
    in_specs, ops = [a_spec, b_spec], [a, b]
    if add is not None:
        in_specs.append(pl.BlockSpec((tm, tn), lambda i, j, k: (i, j)))
        ops.append(add)
    return pl.pallas_call(
        body, name=name, grid=(m // tm, n // tn, nk), in_specs=in_specs,
        out_specs=pl.BlockSpec((tm, tn), lambda i, j, k: (i, j)),
        out_shape=jax.ShapeDtypeStruct((m, n), out_dtype),
        scratch_shapes=[pltpu.VMEM((tm, tn), F32)],
        compiler_params=_cparams(("parallel", "parallel", "arbitrary"), 48),
    )(*ops)


def _rw(name, fn, ins, outs, nrows, tm, ncol=1, vmem_mb=48):
    nrow = nrows // tm
    in_specs, ops = [], []
    for spec in ins:
        kind, arr, bw, cj = spec[:4]
        ops.append(arr)
        if kind == "r":
            ri = spec[4] if len(spec) > 4 else (lambda i: i)
            in_specs.append(pl.BlockSpec((tm, bw), lambda j, i, cj=cj, ri=ri: (ri(i), cj(j))))
        else:
            in_specs.append(pl.BlockSpec((arr.shape[0], bw), lambda j, i, cj=cj: (0, cj(j))))
    out_shape, out_specs = [], []
    for o in outs:
        if o[0] == "r":
            _, width, bw, cj, dt = o
            out_shape.append(jax.ShapeDtypeStruct((nrows, width), dt))
            out_specs.append(pl.BlockSpec((tm, bw), lambda j, i, cj=cj: (i, cj(j))))
        else:
            _, rows, width, bw, cj = o
            out_shape.append(jax.ShapeDtypeStruct((rows, width), F32))
            out_specs.append(pl.BlockSpec((rows, bw), lambda j, i, cj=cj: (0, cj(j))))
    n_in = len(ins)

    def body(*refs):
        j, i = pl.program_id(0), pl.program_id(1)
        rows = i * tm + lax.broadcasted_iota(jnp.int32, (tm, 1), 0)
        vals = []
        for spec, ref in zip(ins, refs[:n_in]):
            if spec[0] == "r" or spec[1].shape[0] == 1:
                vals.append(ref[...])
            else:
                vals.append([ref[pl.ds(r, 1), :] for r in range(spec[1].shape[0])])
        res = fn(rows, j, *vals)
        for o, val, ref in zip(outs, res, refs[n_in:]):
            if o[0] == "r":
                ref[...] = val.astype(ref.dtype)
            else:
                @pl.when(i == 0)
                def _(ref=ref):
                    ref[...] = jnp.zeros_like(ref)

                if o[1] == 1:
                    ref[...] += val
                else:
                    for r in range(o[1]):
                        ref[pl.ds(r, 1), :] += val[r]

    return pl.pallas_call(
        body, name=name, grid=(ncol, nrow), in_specs=in_specs, out_specs=out_specs, out_shape=out_shape,
        compiler_params=_cparams(("parallel", "arbitrary"), vmem_mb),
    )(*ops)


def _c0(j):
    return 0


def _cj(j):
    return j


def _shift(x, s):
    if s == 0:
        return x
    return pltpu.roll(x, s % x.shape[0], 0)


def _conv(x, w):
    k = len(w)
    return functools.reduce(lambda a, b: a + b, [w[j] * _shift(x, k - 1 - j) for j in range(k)])


def _conv_t(dy, w):
    k = len(w)
    return functools.reduce(lambda a, b: a + b, [w[j] * _shift(dy, -(k - 1 - j)) for j in range(k)])


def _conv_w(dy, x, k):
    return [jnp.sum(dy * _shift(x, k - 1 - j), axis=0, keepdims=True) for j in range(k)]


def _tri():
    r = lax.broadcasted_iota(jnp.int32, (CH, CH), 0)
    c = lax.broadcasted_iota(jnp.int32, (CH, CH), 1)
    return r, c


def _col(row):
    r, c = _tri()
    return jnp.sum(jnp.where(r == c, row, 0.0), axis=1, keepdims=True)


def _cumsum_rc(g_r):
    r, c = _tri()
    g_c = _col(g_r)
    cs_r = jnp.sum(jnp.where(r <= c, g_c, 0.0), axis=0, keepdims=True)
    cs_c = jnp.sum(jnp.where(c <= r, g_r, 0.0), axis=1, keepdims=True)
    return cs_r, cs_c


def _decay(cs_r, cs_c):
    r, c = _tri()
    return jnp.exp(jnp.where(c <= r, cs_c - cs_r, -jnp.inf))


def _gdn_a(k, beta_r, g_r):
    r, c = _tri()
    cs_r, cs_c = _cumsum_rc(g_r)
    kk = _dg(k, k, 1, 1)
    return jnp.where(c < r, _col(beta_r) * kk * _decay(cs_r, cs_c), 0.0)


def _neumann(a):
    r, c = _tri()
    x = jnp.where(r == c, 1.0, 0.0) - a
    p = a
    n = 2
    while n < CH:
        p = _dgh(p, p, 1, 0)
        x = x + _dgh(x, p, 1, 0)
        n *= 2
    return x


def _gdn_rest(s, q, k, v, beta_r, g_r, t):
    cs_r, cs_c = _cumsum_rc(g_r)
    dm = _decay(cs_r, cs_c)
    beta_c = _col(beta_r)
    u = _dgh(t, v * beta_c, 1, 0)
    w = _dgh(t, k * (beta_c * jnp.exp(cs_c)), 1, 0)
    v_new = u - _dg(w, s, 1, 0)
    o = _dg(q * jnp.exp(cs_c), s, 1, 0) + _dg(_dg(q, k, 1, 1) * dm, v_new, 1, 0)
    g_last = jnp.sum(g_r, axis=1, keepdims=True)
    s_new = s * jnp.exp(g_last) + _dg(k * jnp.exp(g_last - cs_c), v_new, 0, 0)
    return s_new, o


def _gdn_fwd(q, k, v, beta, g, hb):
    t_rows, d = q.shape
    nc, ng, w = t_rows // CH, d // (HD * hb), HD * hb

    def body(q_ref, k_ref, v_ref, b_ref, g_ref, o_ref, ss_ref, ts_ref, s_scr):
        c = pl.program_id(1)

        @pl.when(c == 0)
        def _():
            s_scr[...] = jnp.zeros_like(s_scr)

        for h in range(hb):
            sl = slice(h * HD, (h + 1) * HD)
            kh = k_ref[:, sl]
            br, gr = b_ref[0, 0, pl.ds(h, 1), :], g_ref[0, 0, pl.ds(h, 1), :]
            s0 = s_scr[h]
            ss_ref[0, 0, h] = s0
            tm = _neumann(_gdn_a(kh, br, gr))
            ts_ref[0, 0, h] = tm
            s1, o = _gdn_rest(s0, q_ref[:, sl], kh, v_ref[:, sl], br, gr, tm)
            o_ref[:, sl] = o
            s_scr[h] = s1

    blk = pl.BlockSpec((CH, w), lambda n, c: (c, n))
    row = pl.BlockSpec((1, 1, hb, CH), lambda n, c: (n, c, 0, 0))
    return pl.pallas_call(
        body, name="gdn_fwd", grid=(ng, nc), in_specs=[blk, blk, blk, row, row],
        out_specs=[blk, pl.BlockSpec((1, 1, hb, HD, HD), lambda n, c: (n, c, 0, 0, 0)),
                   pl.BlockSpec((1, 1, hb, CH, CH), lambda n, c: (n, c, 0, 0, 0))],
        out_shape=[jax.ShapeDtypeStruct((t_rows, d), F32), jax.ShapeDtypeStruct((ng, nc, hb, HD, HD), F32),
                   jax.ShapeDtypeStruct((ng, nc, hb, CH, CH), F32)],
        scratch_shapes=[pltpu.VMEM((hb, HD, HD), F32)],
        compiler_params=_cparams(("parallel", "arbitrary"), 32),
    )(q, k, v, beta, g)


def _gdn_bwd(q, k, v, beta, g, ss, ts, do, hb):
    t_rows, d = q.shape
    nc, ng, w = t_rows // CH, d // (HD * hb), HD * hb

    def body(q_ref, k_ref, v_ref, b_ref, g_ref, ss_ref, ts_ref, do_ref, dq_ref, dk_ref, dv_ref, db_ref, dg_ref, ds_scr):
        cr = pl.program_id(1)

        @pl.when(cr == 0)
        def _():
            ds_scr[...] = jnp.zeros_like(ds_scr)

        first = cr == nc - 1
        rowi = lax.broadcasted_iota(jnp.int32, (CH, 1), 0)
        lani = lax.broadcasted_iota(jnp.int32, (1, CH), 1)
        keep_c = jnp.logical_or(jnp.logical_not(first), rowi >= PADR)
        keep_r = jnp.logical_or(jnp.logical_not(first), lani >= PADR)
        for h in range(hb):
            sl = slice(h * HD, (h + 1) * HD)
            qh, kh, vh = q_ref[:, sl], k_ref[:, sl], v_ref[:, sl]
            br, gr = b_ref[0, 0, pl.ds(h, 1), :], g_ref[0, 0, pl.ds(h, 1), :]
            tm = ts_ref[0, 0, h]
            _, vjp_rest = jax.vjp(_gdn_rest, ss_ref[0, 0, h], qh, kh, vh, br, gr, tm)
            ds0, dq, dk, dv, db, dg, dt = vjp_rest((ds_scr[h], do_ref[:, sl]))
            da = -_dgh(tm, _dgh(dt, tm, 1, 1), 0, 0)
            _, vjp_a = jax.vjp(_gdn_a, kh, br, gr)
            dk2, db2, dg2 = vjp_a(da)
            ds_scr[h] = ds0
            dq_ref[:, sl] = jnp.where(keep_c, dq, 0.0)
            dk_ref[:, sl] = jnp.where(keep_c, dk + dk2, 0.0)
            dv_ref[:, sl] = jnp.where(keep_c, dv, 0.0)
            db_ref[0, 0, pl.ds(h, 1), :] = jnp.where(keep_r, db + db2, 0.0)
            dg_ref[0, 0, pl.ds(h, 1), :] = jnp.where(keep_r, dg + dg2, 0.0)

    blk = pl.BlockSpec((CH, w), lambda n, c: (nc - 1 - c, n))
    row = pl.BlockSpec((1, 1, hb, CH), lambda n, c: (n, nc - 1 - c, 0, 0))
    return pl.pallas_call(
        body, name="gdn_bwd", grid=(ng, nc),
        in_specs=[blk, blk, blk, row, row, pl.BlockSpec((1, 1, hb, HD, HD), lambda n, c: (n, nc - 1 - c, 0, 0, 0)),
                  pl.BlockSpec((1, 1, hb, CH, CH), lambda n, c: (n, nc - 1 - c, 0, 0, 0)), blk],
        out_specs=[blk, blk, blk, row, row],
        out_shape=[jax.ShapeDtypeStruct((t_rows, d), F32)] * 3 + [jax.ShapeDtypeStruct((ng, nc, hb, CH), F32)] * 2,
        scratch_shapes=[pltpu.VMEM((hb, HD, HD), F32)],
        compiler_params=_cparams(("parallel", "arbitrary"), 32),
    )(q, k, v, beta, g, ss, ts, do)


def _ssd_step(s, xs, bm, cm, dt_r, a_r):
    first = lax.broadcasted_iota(jnp.int32, (1, 2 * M2P), 1) < M2P

    def pick(v0, v1):
        return jnp.where(first, v0, v1)

    cs = [_cumsum_rc(a) for a in a_r]
    lm = [_decay(r, c) for r, c in cs]
    alast = [jnp.sum(a, axis=1, keepdims=True) for a in a_r]
    xdt = xs * pick(_col(dt_r[0]), _col(dt_r[1]))
    cb = _dg(cm, bm, 1, 1)
    y = pick(_dg(cb * lm[0], xdt, 1, 0), _dg(cb * lm[1], xdt, 1, 0))
    y = y + _dg(cm, s, 1, 0) * pick(jnp.exp(cs[0][1]), jnp.exp(cs[1][1]))
    wts = pick(jnp.exp(alast[0] - cs[0][1]), jnp.exp(alast[1] - cs[1][1]))
    s_new = s * pick(jnp.exp(alast[0]), jnp.exp(alast[1])) + _dg(bm, xdt * wts, 0, 0)
    return s_new, y


def _ssd_specs(nc, d, hpg, rev):
    ppg = hpg // 2
    cc = (lambda c: nc - 1 - c) if rev else (lambda c: c)
    xs = pl.BlockSpec((CH, LANE), lambda p, c: (cc(c), p))
    bm = pl.BlockSpec((CH, NST), lambda p, c: (cc(c), d // LANE + p // ppg))
    cm = pl.BlockSpec((CH, NST), lambda p, c: (cc(c), d // LANE + M2G + p // ppg))
    row = pl.BlockSpec((1, 1, 2, CH), lambda p, c: (p, cc(c), 0, 0))
    st = pl.BlockSpec((1, 1, NST, LANE), lambda p, c: (p, cc(c), 0, 0))
    return xs, bm, cm, row, st


def _ssd_fwd(xbc, dt, a, d):
    t_rows = xbc.shape[0]
    nc, npair = t_rows // CH, d // LANE
    hpg = (d // M2P) // M2G

    def body(xs_ref, b_ref, c_ref, dt_ref, a_ref, y_ref, ss_ref, s_scr):
        c = pl.program_id(1)

        @pl.when(c == 0)
        def _():
            s_scr[...] = jnp.zeros_like(s_scr)

        s0 = s_scr[...]
        ss_ref[0, 0] = s0
        dt_r = [dt_ref[0, 0, pl.ds(h, 1), :] for h in range(2)]
        a_r = [a_ref[0, 0, pl.ds(h, 1), :] for h in range(2)]
        s1, y = _ssd_step(s0, xs_ref[...], b_ref[...], c_ref[...], dt_r, a_r)
        y_ref[...] = y
        s_scr[...] = s1

    xs, bm, cm, row, st = _ssd_specs(nc, d, hpg, False)
    return pl.pallas_call(
        body, name="ssd_fwd", grid=(npair, nc), in_specs=[xs, bm, cm, row, row], out_specs=[xs, st],
        out_shape=[jax.ShapeDtypeStruct((t_rows, d), F32), jax.ShapeDtypeStruct((npair, nc, NST, LANE), F32)],
        scratch_shapes=[pltpu.VMEM((NST, LANE), F32)],
        compiler_params=_cparams(("parallel", "arbitrary"), 32),
    )(xbc, xbc, xbc, dt, a)


def _ssd_bwd(xbc, dt, a, ss, dy, d):
    t_rows = xbc.shape[0]
    nc, npair = t_rows // CH, d // LANE
    hpg = (d // M2P) // M2G

    def body(xs_ref, b_ref, c_ref, dt_ref, a_ref, ss_ref, dy_ref, dxs_ref, db_ref, dc_ref, ddt_ref, da_ref, ds_scr):
        cr = pl.program_id(1)

        @pl.when(cr == 0)
        def _():
            ds_scr[...] = jnp.zeros_like(ds_scr)

        first = cr == nc - 1
        keep_c = jnp.logical_or(jnp.logical_not(first), lax.broadcasted_iota(jnp.int32, (CH, 1), 0) >= PADR)
        keep_r = jnp.logical_or(jnp.logical_not(first), lax.broadcasted_iota(jnp.int32, (1, CH), 1) >= PADR)
        dt_r = [dt_ref[0, 0, pl.ds(h, 1), :] for h in range(2)]
        a_r = [a_ref[0, 0, pl.ds(h, 1), :] for h in range(2)]
        _, vjp = jax.vjp(_ssd_step, ss_ref[0, 0], xs_ref[...], b_ref[...], c_ref[...], dt_r, a_r)
        ds0, dxs, db, dc, ddt, da = vjp((ds_scr[...], dy_ref[...]))
        ds_scr[...] = ds0
        dxs_ref[...] = jnp.where(keep_c, dxs, 0.0)
        db_ref[...] = jnp.where(keep_c, db, 0.0)
        dc_ref[...] = jnp.where(keep_c, dc, 0.0)
        for h in range(2):
            ddt_ref[0, 0, pl.ds(h, 1), :] = jnp.where(keep_r, ddt[h], 0.0)
            da_ref[0, 0, pl.ds(h, 1), :] = jnp.where(keep_r, da[h], 0.0)

    xs, bm, cm, row, st = _ssd_specs(nc, d, hpg, True)
    return pl.pallas_call(
        body, name="ssd_bwd", grid=(npair, nc), in_specs=[xs, bm, cm, row, row, st, xs],
        out_specs=[xs, xs, xs, row, row],
        out_shape=[jax.ShapeDtypeStruct((t_rows, d), F32)] * 3 + [jax.ShapeDtypeStruct((npair, nc, 2, CH), F32)] * 2,
        scratch_shapes=[pltpu.VMEM((NST, LANE), F32)],
        compiler_params=_cparams(("parallel", "arbitrary"), 32),
    )(xbc, xbc, xbc, dt, a, ss, dy)


def _exchange(name, gathers, scatters):
    arrays = list(gathers) + list(scatters)
    n_g, n = len(gathers), len(arrays)

    def body(*refs):
        ins, outs = refs[:n], refs[n:2 * n]
        send_sems, recv_sems, local_sems = refs[2 * n:]
        x, y, c = lax.axis_index("x"), lax.axis_index("y"), lax.axis_index("c")
        me = 4 * x + 2 * y + c

        def src(a, slot):
            return ins[a] if a < n_g else ins[a].at[slot]

        local = [pltpu.make_async_copy(src(a, me), outs[a].at[me], local_sems.at[a]) for a in range(n)]
        for cp in local:
            cp.start()
        copies = []
        for rel in range(1, NDEV):
            px, py, pc = x ^ (rel >> 2), y ^ ((rel >> 1) & 1), c ^ (rel & 1)
            peer = 4 * px + 2 * py + pc
            for a in range(n):
                copies.append(pltpu.make_async_remote_copy(
                    src_ref=src(a, peer), dst_ref=outs[a].at[me], send_sem=send_sems.at[a, rel - 1],
                    recv_sem=recv_sems.at[a, rel - 1], device_id=(px, py, pc), device_id_type=pl.DeviceIdType.MESH))
        for cp in copies:
            cp.start()
        for cp in copies:
            cp.wait_recv()
        for cp in copies:
            cp.wait_send()
        for cp in local:
            cp.wait()

    any_spec = pl.BlockSpec(memory_space=pl.ANY)
    out_shape = [jax.ShapeDtypeStruct((NDEV,) + a.shape, a.dtype) for a in gathers]
    out_shape += [jax.ShapeDtypeStruct(a.shape, a.dtype) for a in scatters]
    return pl.pallas_call(
        body, name=name, in_specs=[any_spec] * n, out_specs=[any_spec] * n, out_shape=out_shape,
        scratch_shapes=[pltpu.SemaphoreType.DMA((n, NDEV - 1)), pltpu.SemaphoreType.DMA((n, NDEV - 1)),
                        pltpu.SemaphoreType.DMA((n,))],
        compiler_params=pltpu.CompilerParams(has_side_effects=True),
    )(*arrays)


def _adamw(name, staged, w, m, v):
    r, c = w.shape
    tr = _pick(r, 256, 8)

    def body(st_ref, w_ref, m_ref, v_ref, g_ref, d_ref, nm_ref, nv_ref):
        g = st_ref[0].astype(F32)
        for k in range(1, NDEV):
            g = g + st_ref[k].astype(F32)
        m_new = ADAM_B1 * m_ref[...] + (1.0 - ADAM_B1) * g
        v_new = ADAM_B2 * v_ref[...] + (1.0 - ADAM_B2) * jnp.square(g)
        m_hat = m_new / (1.0 - ADAM_B1 ** ADAM_STEP)
        v_hat = v_new / (1.0 - ADAM_B2 ** ADAM_STEP)
        g_ref[...] = g
        d_ref[...] = -ADAM_LR * (m_hat / (jnp.sqrt(v_hat) + ADAM_EPS) + ADAM_WD * w_ref[...])
        nm_ref[...] = m_new
        nv_ref[...] = v_new

    blk = pl.BlockSpec((tr, c), lambda i: (i, 0))
    return pl.pallas_call(
        body, name=name, grid=(r // tr,), in_specs=[pl.BlockSpec((NDEV, tr, c), lambda i: (0, i, 0)), blk, blk, blk],
        out_specs=[blk] * 4, out_shape=[jax.ShapeDtypeStruct((r, c), F32)] * 4,
        compiler_params=_cparams(("parallel",), 48),
    )(staged, w, m, v)


def _pack(parts):
    flat = jnp.concatenate([p.reshape(-1).astype(F32) for p in parts])
    pad = (-flat.shape[0]) % (8 * LANE)
    return jnp.pad(flat, (0, pad)).reshape(-1, LANE)


def _unpack(slab, shapes):
    flat, out, off = slab.reshape(-1), [], 0
    for s in shapes:
        n = 1
        for dim in s:
            n *= dim
        out.append(flat[off:off + n].reshape(s))
        off += n
    return out


def _to_shards(full, axis):
    shp = full.shape
    t = full.reshape(shp[:axis] + (NDEV, shp[axis] // NDEV) + shp[axis + 1:])
    return jnp.moveaxis(t, axis, 0)


def _from_shards(g, axis):
    t = jnp.moveaxis(g, 0, axis)
    shp = t.shape
    return t.reshape(shp[:axis] + (shp[axis] * shp[axis + 1],) + shp[axis + 2:])


def kernel(x, meta_tokens, norm_mix_w, w_in, dn_conv_w, dn_a_log, dn_dt_bias, dn_norm_w, m2_conv_w, m2_conv_b, m2_a_log, m2_dt_bias, m2_d, m2_norm_w, w_out, norm_ffn_w, ffn_up, ffn_conv_w, ffn_down, norm_final_w, loss_target, m_meta_tokens, m_norm_mix_w, m_w_in, m_dn_conv_w, m_dn_a_log, m_dn_dt_bias, m_dn_norm_w, m_m2_conv_w, m_m2_conv_b, m_m2_a_log, m_m2_dt_bias, m_m2_d, m_m2_norm_w, m_w_out, m_norm_ffn_w, m_ffn_up, m_ffn_conv_w, m_ffn_down, m_norm_final_w, v_meta_tokens, v_norm_mix_w, v_w_in, v_dn_conv_w, v_dn_a_log, v_dn_dt_bias, v_dn_norm_w, v_m2_conv_w, v_m2_conv_b, v_m2_a_log, v_m2_dt_bias, v_m2_d, v_m2_norm_w, v_w_out, v_norm_ffn_w, v_ffn_up, v_ffn_conv_w, v_ffn_down, v_norm_final_w):
    seq, d = x.shape[1], x.shape[2]
    t_rows = seq + CH
    nc = t_rows // CH
    dnh, m2h = d // HD, d // M2P
    dff = ffn_down.shape[1] * NDEV
    xbc_w = d + 2 * M2G * NST
    assert seq % CH == 0 and d % (2 * M2P * M2G) == 0 and 2 * dnh + m2h <= LANE
    hb = 2 if dnh % 2 == 0 else 1
    tm_rw = _pick(t_rows, 208, 16)

    small_sharded = [meta_tokens, dn_conv_w[0], m2_conv_w[0], ffn_conv_w[0]]
    small_shapes = [p.shape for p in small_sharded]
    g_win, g_wout, g_wup, g_wdown, g_small = _exchange(
        "gather_weights",
        [w_in[0].astype(WIRE), w_out[0].astype(WIRE), ffn_up[0].astype(WIRE), ffn_down[0].astype(WIRE), _pack(small_sharded)], [])
    win = _from_shards(g_win, 1)
    wout = _from_shards(g_wout, 0)
    wup = _from_shards(g_wup, 1)
    wdown = _from_shards(g_wdown, 0)
    small_full = [_unpack(g_small[k], small_shapes) for k in range(NDEV)]
    meta_f, dnconv_f, m2conv_f, ffnconv_f = [jnp.concatenate([small_full[k][i] for k in range(NDEV)], axis=-1) for i in range(4)]

    o_z, o_b, o_a = 3 * d, 4 * d, 4 * d + dnh
    o_m2z = 4 * d + 2 * dnh
    o_xbc, o_dt = o_m2z + d, o_m2z + d + xbc_w
    w_small = jnp.concatenate([win[:, o_b:o_m2z], win[:, o_dt:], jnp.zeros((d, LANE - 2 * dnh - m2h), WIRE)], axis=1)
    w_seg = {"q": win[:, :d], "k": win[:, d:2 * d], "v": win[:, 2 * d:3 * d], "z": win[:, o_z:o_b],
             "m2z": win[:, o_m2z:o_xbc], "xbc": win[:, o_xbc:o_dt], "sm": w_small}
    wup_g, wup_v = wup[:, :dff], wup[:, dff:]

    h0 = jnp.concatenate([jnp.zeros((PADR, d), F32), meta_f, x[0]], axis=0)
    valid = lambda rows: rows >= PADR

    def norm_fwd(name, h, w):
        return _rw(name, lambda rows, j, hv, wv: (_rms(hv, wv),), [("r", h, d, _c0), ("p", w, d, _c0)],
                   [("r", d, d, _c0, MXU)], t_rows, tm_rw)[0]

    hn1 = norm_fwd("norm_mix", h0, norm_mix_w)
    proj = {s: _mm("proj_" + s, hn1, w_seg[s]) for s in w_seg}

    def dn_post(sec, cv):
        s = _silu(cv)
        if sec < 2:
            s = s * lax.rsqrt(jnp.sum(s * s, axis=-1, keepdims=True) + EPS)
        if sec == 0:
            s = s * (HD ** -0.5)
        return s

    def dn_prep(sec, name):
        def fn(rows, j, p, w):
            return (jnp.where(valid(rows), dn_post(sec, _conv(p, w)), 0.0),)
        wc = dnconv_f[:, sec * d:(sec + 1) * d]
        return _rw("dn_prep_" + name, fn, [("r", proj[name], HD, _cj), ("p", wc, HD, _cj)], [("r", d, HD, _cj, F32)],
                   t_rows, t_rows, ncol=dnh)[0]

    q_act, k_act, v_act = dn_prep(0, "q"), dn_prep(1, "k"), dn_prep(2, "v")

    lane = lambda: lax.broadcasted_iota(jnp.int32, (1, LANE), 1)

    def lanes_of(vec, off):
        return jnp.pad(vec.astype(F32), ((0, 0), (off, LANE - off - vec.shape[1])))

    gate_params = [lanes_of(dn_a_log, dnh), lanes_of(dn_dt_bias, dnh), lanes_of(m2_a_log, 2 * dnh), lanes_of(m2_dt_bias, 2 * dnh)]

    def gates(rows, sm, p_alog, p_dtb, p_malog, p_mdtb):
        ln = lane()
        is_b, is_g = ln < dnh, jnp.logical_and(ln >= dnh, ln < 2 * dnh)
        is_d = jnp.logical_and(ln >= 2 * dnh, ln < 2 * dnh + m2h)
        beta = jax.nn.sigmoid(sm)
        gdec = -jnp.exp(p_alog) * _softplus(sm + p_dtb)
        dt = _softplus(sm + p_mdtb)
        am = dt * (-jnp.exp(p_malog))
        ok = valid(rows)
        g1 = jnp.where(ok, jnp.where(is_b, beta, jnp.where(is_g, gdec, jnp.where(is_d, dt, 0.0))), 0.0)
        g2 = jnp.where(jnp.logical_and(ok, is_d), am, 0.0)
        return g1, g2

    gate_ins = [("r", proj["sm"], LANE, _c0)] + [("p", p, LANE, _c0) for p in gate_params]
    g1, g2 = _rw("gates", lambda rows, j, *a: gates(rows, *a), gate_ins,
                 [("r", LANE, LANE, _c0, F32), ("r", LANE, LANE, _c0, F32)], t_rows, tm_rw)

    def head_rows(cols, per):
        n = cols.shape[1]
        return cols.reshape(nc, CH, n // per, per).transpose(2, 0, 3, 1)

    def head_cols(rows_):
        ngrp, _, per, _ = rows_.shape
        return rows_.transpose(1, 3, 0, 2).reshape(t_rows, ngrp * per)

    beta_r, gdec_r = head_rows(g1[:, :dnh], hb), head_rows(g1[:, dnh:2 * dnh], hb)
    dt_r, am_r = head_rows(g1[:, 2 * dnh:2 * dnh + m2h], 2), head_rows(g2[:, 2 * dnh:2 * dnh + m2h], 2)

    o_dn, dn_states, dn_tinv = _gdn_fwd(q_act, k_act, v_act, beta_r, gdec_r, hb)

    def dn_out(o, z, w):
        outs = []
        for h in range(dnh):
            sl = slice(h * HD, (h + 1) * HD)
            outs.append(_rms(o[:, sl], w) * _silu(z[:, sl]))
        return jnp.concatenate(outs, axis=1)

    mixed_dn = _rw("dn_out", lambda rows, j, o, z, w: (dn_out(o, z, w),),
                   [("r", o_dn, d, _c0), ("r", proj["z"], d, _c0), ("p", dn_norm_w, HD, _c0)], [("r", d, d, _c0, MXU)],
                   t_rows, tm_rw)[0]

    def m2_prep(rows, j, p, w, b):
        return (jnp.where(valid(rows), _silu(_conv(p, w) + b), 0.0),)

    xbc_act = _rw("m2_prep", m2_prep, [("r", proj["xbc"], LANE, _cj), ("p", m2conv_f, LANE, _cj), ("p", m2_conv_b, LANE, _cj)],
                  [("r", xbc_w, LANE, _cj, F32)], t_rows, t_rows, ncol=xbc_w // LANE)[0]
    y_ssd, m2_states = _ssd_fwd(xbc_act, dt_r, am_r, d)

    d_lanes = jnp.repeat(m2_d.astype(F32), M2P, axis=1)
    gw = d // M2G

    def m2_out(ys, xs, z, dl, nw):
        yv = (ys + dl * xs) * _silu(z)
        outs = []
        for gi in range(M2G):
            sl = slice(gi * gw, (gi + 1) * gw)
            outs.append(_rms(yv[:, sl], nw[:, sl]))
        return jnp.concatenate(outs, axis=1)

    m2_out_ins = [("r", y_ssd, d, _c0), ("r", xbc_act, d, _c0), ("r", proj["m2z"], d, _c0), ("p", d_lanes, d, _c0),
                  ("p", m2_norm_w, d, _c0)]
    mixed_m2 = _rw("m2_out", lambda rows, j, *a: (m2_out(*a),), m2_out_ins, [("r", d, d, _c0, MXU)], t_rows, tm_rw)[0]

    mixed = jnp.concatenate([mixed_dn, mixed_m2], axis=1)
    h1 = _mm("out_proj", mixed, wout, add=h0)
    hn2 = norm_fwd("norm_ffn", h1, norm_ffn_w)
    u_g, u_v = _mm("ffn_up_g", hn2, wup_g), _mm("ffn_up_v", hn2, wup_v)
    fc_g, fc_v = ffnconv_f[:, :dff], ffnconv_f[:, dff:]

    def ffn_act(rows, j, ug, uv, wg, wv):
        return (jnp.where(valid(rows), _silu(_conv(ug, wg)) * _conv(uv, wv), 0.0),)

    act = _rw("ffn_act", ffn_act, [("r", u_g, LANE, _cj), ("r", u_v, LANE, _cj), ("p", fc_g, LANE, _cj), ("p", fc_v, LANE, _cj)],
              [("r", dff, LANE, _cj, MXU)], t_rows, t_rows, ncol=dff // LANE)[0]
    h2 = _mm("ffn_down", act, wdown, add=h1, tk=1408)

    def loss_fn(hv, wf, tgt, rows):
        err = jnp.where(rows >= CH, _rms(hv, wf) - tgt, 0.0)
        return 0.5 * jnp.sum(jnp.mean(err * err, axis=-1, keepdims=True), axis=0, keepdims=True)

    def final(rows, j, hv, wf, tgt):
        loss, vjp = jax.vjp(lambda a, b: loss_fn(a, b, tgt, rows), hv, wf)
        dh, dw = vjp(jnp.ones((1, 1), F32))
        return dh, dh, dw, jnp.broadcast_to(loss, (1, LANE))

    wf2 = norm_final_w.reshape(1, d)
    dh2, dh2_m, d_wf, loss_part = _rw(
        "loss_head", final, [("r", h2, d, _c0), ("p", wf2, d, _c0), ("r", loss_target[0], d, _c0, lambda i: jnp.maximum(i - 1, 0))],
        [("r", d, d, _c0, F32), ("r", d, d, _c0, MXU), ("p", 1, d, d, _c0), ("p", 1, LANE, LANE, _c0)], t_rows, CH)
    loss = lax.psum(loss_part[0, 0], MESH_AXES)

    d_act = _mm("d_act", dh2_m, wdown, tb=True)
    gw_down = _mm("gw_down", act, dh2_m, ta=True, tm=1408, tn=1024, tk=1040)

    def ffn_act_bwd(rows, j, ug, uv, wg, wv, da):
        cg, cv = _conv(ug, wg), _conv(uv, wv)
        _, vjp = jax.vjp(lambda a, b: _silu(a) * b, cg, cv)
        dcg, dcv = vjp(jnp.where(valid(rows), da, 0.0))
        return _conv_t(dcg, wg), _conv_t(dcv, wv), _conv_w(dcg, ug, len(wg)), _conv_w(dcv, uv, len(wv))

    kf = fc_g.shape[0]
    du_g, du_v, g_fc_g, g_fc_v = _rw(
        "ffn_act_bwd", ffn_act_bwd,
        [("r", u_g, LANE, _cj), ("r", u_v, LANE, _cj), ("p", fc_g, LANE, _cj), ("p", fc_v, LANE, _cj), ("r", d_act, LANE, _cj)],
        [("r", dff, LANE, _cj, MXU), ("r", dff, LANE, _cj, MXU), ("p", kf, dff, LANE, _cj), ("p", kf, dff, LANE, _cj)],
        t_rows, t_rows, ncol=dff // LANE)
    d_hn2 = _mm("d_hn2_v", du_v, wup_v, tb=True, tk=1408, add=_mm("d_hn2_g", du_g, wup_g, tb=True, tk=1408))
    gw_up_g = _mm("gw_up_g", hn2, du_g, ta=True, tm=1024, tn=1408, tk=1040)
    gw_up_v = _mm("gw_up_v", hn2, du_v, ta=True, tm=1024, tn=1408, tk=1040)

    def norm_bwd(name, h, w, dy, dres):
        def fn(rows, j, hv, wv, dyv, dr):
            _, vjp = jax.vjp(_rms, hv, wv)
            dh, dw = vjp(dyv)
            dh = dh + dr
            return dh, dh, dw
        return _rw(name, fn, [("r", h, d, _c0), ("p", w, d, _c0), ("r", dy, d, _c0), ("r", dres, d, _c0)],
                   [("r", d, d, _c0, F32), ("r", d, d, _c0, MXU), ("p", 1, d, d, _c0)], t_rows, tm_rw)

    dh1, dh1_m, g_norm_ffn = norm_bwd("norm_ffn_bwd", h1, norm_ffn_w, d_hn2, dh2)

    d_mixed = _mm("d_mixed", dh1_m, wout, tb=True)
    gw_out = _mm("gw_out", mixed, dh1_m, ta=True, tm=1024, tn=1024, tk=1040)

    def fold_heads(vec):
        r = lax.broadcasted_iota(jnp.int32, (d, LANE), 0)
        c = lax.broadcasted_iota(jnp.int32, (d, LANE), 1)
        return _dgh(vec, jnp.where(jnp.logical_and(r >= c * M2P, r < (c + 1) * M2P), 1.0, 0.0), 1, 0)

    def m2_out_bwd(rows, j, ys, xs, z, dl, nw, dy):
        _, vjp = jax.vjp(m2_out, ys, xs, z, dl, nw)
        dys, dxs, dz, ddl, dnw = vjp(dy)
        return dys, dxs, dz, fold_heads(ddl), dnw

    dy_ssd, dxs_skip, d_m2z, g_m2_d, g_m2_norm = _rw(
        "m2_out_bwd", m2_out_bwd, m2_out_ins + [("r", d_mixed, d, lambda j: 1)],
        [("r", d, d, _c0, F32), ("r", d, d, _c0, F32), ("r", d, d, _c0, MXU), ("p", 1, LANE, LANE, _c0), ("p", 1, d, d, _c0)],
        t_rows, _pick(t_rows, 208, 16))

    dxs, db_parts, dc_parts, ddt_r, dam_r = _ssd_bwd(xbc_act, dt_r, am_r, m2_states, dy_ssd, d)

    ppg = (m2h // M2G) // 2

    def m2_prep_bwd(n_d):
        def fn(rows, j, p, w, b, *ds):
            pre = _conv(p, w) + b
            _, vjp = jax.vjp(_silu, pre)
            dpre, = vjp(jnp.where(valid(rows), functools.reduce(lambda a_, b_: a_ + b_, ds), 0.0))
            return _conv_t(dpre, w), _conv_w(dpre, p, len(w)), jnp.sum(dpre, axis=0, keepdims=True)
        return fn

    def m2_prep_bwd_call(name, off, width, d_ins):
        blk0 = off // LANE
        ins = [("r", proj["xbc"], LANE, lambda j: blk0 + j), ("p", m2conv_f, LANE, lambda j: blk0 + j),
               ("p", m2_conv_b, LANE, lambda j: blk0 + j)] + d_ins
        km = m2conv_f.shape[0]
        return _rw(name, m2_prep_bwd(len(d_ins)), ins,
                   [("r", width, LANE, _cj, MXU), ("p", km, width, LANE, _cj), ("p", 1, width, LANE, _cj)],
                   t_rows, t_rows, ncol=width // LANE)

    dp_xs, gcw_xs, gcb_xs = m2_prep_bwd_call("m2_prep_bwd_x", 0, d, [("r", dxs, LANE, _cj), ("r", dxs_skip, LANE, _cj)])
    grp = lambda parts: [("r", parts, LANE, lambda j, i_=i_: j * ppg + i_) for i_ in range(ppg)]
    dp_b, gcw_b, gcb_b = m2_prep_bwd_call("m2_prep_bwd_b", d, M2G * NST, grp(db_parts))
    dp_c, gcw_c, gcb_c = m2_prep_bwd_call("m2_prep_bwd_c", d + M2G * NST, M2G * NST, grp(dc_parts))
    d_pxbc = jnp.concatenate([dp_xs, dp_b, dp_c], axis=1)
    g_m2_conv = jnp.concatenate([gcw_xs, gcw_b, gcw_c], axis=1)
    g_m2_conv_b = jnp.concatenate([gcb_xs, gcb_b, gcb_c], axis=1)

    def dn_out_bwd(rows, j, o, z, w, dy):
        _, vjp = jax.vjp(dn_out, o, z, w)
        return vjp(dy)

    d_o, d_z, g_dn_norm = _rw(
        "dn_out_bwd", dn_out_bwd,
        [("r", o_dn, d, _c0), ("r", proj["z"], d, _c0), ("p", dn_norm_w, HD, _c0), ("r", d_mixed, d, _c0)],
        [("r", d, d, _c0, F32), ("r", d, d, _c0, MXU), ("p", 1, HD, HD, _c0)], t_rows, _pick(t_rows, 208, 16))

    dq, dk, dv, dbeta_r, dgdec_r = _gdn_bwd(q_act, k_act, v_act, beta_r, gdec_r, dn_states, dn_tinv, d_o, hb)

    def dn_prep_bwd(sec, name, dact):
        def fn(rows, j, p, w, da):
            cv = _conv(p, w)
            _, vjp = jax.vjp(functools.partial(dn_post, sec), cv)
            dcv, = vjp(jnp.where(valid(rows), da, 0.0))
            return _conv_t(dcv, w), _conv_w(dcv, p, len(w))
        wc = dnconv_f[:, sec * d:(sec + 1) * d]
        return _rw("dn_prep_bwd_" + name, fn, [("r", proj[name], HD, _cj), ("p", wc, HD, _cj), ("r", dact, HD, _cj)],
                   [("r", d, HD, _cj, MXU), ("p", wc.shape[0], d, HD, _cj)], t_rows, t_rows, ncol=dnh)

    (dp_q, gcw_q), (dp_k, gcw_k), (dp_v, gcw_v) = dn_prep_bwd(0, "q", dq), dn_prep_bwd(1, "k", dk), dn_prep_bwd(2, "v", dv)
    g_dn_conv = jnp.concatenate([gcw_q, gcw_k, gcw_v], axis=1)

    zpad = jnp.zeros((t_rows, LANE - 2 * dnh - m2h), F32)
    dg1 = jnp.concatenate([head_cols(dbeta_r), head_cols(dgdec_r), head_cols(ddt_r), zpad], axis=1)
    dg2 = jnp.concatenate([jnp.zeros((t_rows, 2 * dnh), F32), head_cols(dam_r), zpad], axis=1)

    def gates_bwd(rows, j, sm, pa, pb, pc, pd, d1, d2):
        _, vjp = jax.vjp(lambda *a: gates(rows, *a), sm, pa, pb, pc, pd)
        return vjp((d1, d2))

    dp_sm, g_pa, g_pb, g_pc, g_pd = _rw(
        "gates_bwd", gates_bwd, gate_ins + [("r", dg1, LANE, _c0), ("r", dg2, LANE, _c0)],
        [("r", LANE, LANE, _c0, MXU)] + [("p", 1, LANE, LANE, _c0)] * 4, t_rows, tm_rw)

    dseg = {"q": dp_q, "k": dp_k, "v": dp_v, "z": d_z, "m2z": d_m2z, "xbc": d_pxbc, "sm": dp_sm}
    d_hn1 = None
    for s in dseg:
        d_hn1 = _mm("d_hn1_" + s, dseg[s], w_seg[s], tb=True, tk=2048, add=d_hn1)
    gw_seg = {s: _mm("gw_in_" + s, hn1, dseg[s], ta=True, tm=1024, tn=1024, tk=1040) for s in dseg}
    dh0, _, g_norm_mix = norm_bwd("norm_mix_bwd", h0, norm_mix_w, d_hn1, dh1)

    gsm = gw_seg["sm"]
    gw_in_full = jnp.concatenate([gw_seg["q"], gw_seg["k"], gw_seg["v"], gw_seg["z"], gsm[:, :2 * dnh], gw_seg["m2z"],
                                  gw_seg["xbc"], gsm[:, 2 * dnh:2 * dnh + m2h]], axis=1)
    gw_up_full = jnp.concatenate([gw_up_g, gw_up_v], axis=1)
    g_ffn_conv = jnp.concatenate([g_fc_g, g_fc_v], axis=1)
    small_parts = [_to_shards(dh0[PADR:CH], 1), _to_shards(g_dn_conv, 1), _to_shards(g_m2_conv, 1), _to_shards(g_ffn_conv, 1)]
    small_scatter = jnp.stack([_pack([p[k] for p in small_parts]) for k in range(NDEV)])

    rep_names = ["norm_mix_w", "dn_a_log", "dn_dt_bias", "dn_norm_w", "m2_conv_b", "m2_a_log", "m2_dt_bias", "m2_d",
                 "m2_norm_w", "norm_ffn_w", "norm_final_w"]
    rep_grads = [g_norm_mix, g_pa[:, dnh:2 * dnh], g_pb[:, dnh:2 * dnh], g_dn_norm, g_m2_conv_b, g_pc[:, 2 * dnh:2 * dnh + m2h],
                 g_pd[:, 2 * dnh:2 * dnh + m2h], g_m2_d[:, :m2h], g_m2_norm, g_norm_ffn, d_wf.reshape(d)]
    st_rep, st_win, st_wout, st_wup, st_wdown, st_small = _exchange(
        "exchange_grads", [_pack(rep_grads)],
        [_to_shards(gw_in_full, 1).astype(WIRE), _to_shards(gw_out, 0).astype(WIRE), _to_shards(gw_up_full, 1).astype(WIRE),
         _to_shards(gw_down, 0).astype(WIRE), small_scatter])

    weights = dict(meta_tokens=meta_tokens, norm_mix_w=norm_mix_w, w_in=w_in, dn_conv_w=dn_conv_w, dn_a_log=dn_a_log,
                   dn_dt_bias=dn_dt_bias, dn_norm_w=dn_norm_w, m2_conv_w=m2_conv_w, m2_conv_b=m2_conv_b, m2_a_log=m2_a_log,
                   m2_dt_bias=m2_dt_bias, m2_d=m2_d, m2_norm_w=m2_norm_w, w_out=w_out, norm_ffn_w=norm_ffn_w, ffn_up=ffn_up,
                   ffn_conv_w=ffn_conv_w, ffn_down=ffn_down, norm_final_w=norm_final_w)
    mom1 = dict(meta_tokens=m_meta_tokens, norm_mix_w=m_norm_mix_w, w_in=m_w_in, dn_conv_w=m_dn_conv_w, dn_a_log=m_dn_a_log,
                dn_dt_bias=m_dn_dt_bias, dn_norm_w=m_dn_norm_w, m2_conv_w=m_m2_conv_w, m2_conv_b=m_m2_conv_b,
                m2_a_log=m_m2_a_log, m2_dt_bias=m_m2_dt_bias, m2_d=m_m2_d, m2_norm_w=m_m2_norm_w, w_out=m_w_out,
                norm_ffn_w=m_norm_ffn_w, ffn_up=m_ffn_up, ffn_conv_w=m_ffn_conv_w, ffn_down=m_ffn_down,
                norm_final_w=m_norm_final_w)
    mom2 = dict(meta_tokens=v_meta_tokens, norm_mix_w=v_norm_mix_w, w_in=v_w_in, dn_conv_w=v_dn_conv_w, dn_a_log=v_dn_a_log,
                dn_dt_bias=v_dn_dt_bias, dn_norm_w=v_dn_norm_w, m2_conv_w=v_m2_conv_w, m2_conv_b=v_m2_conv_b,
                m2_a_log=v_m2_a_log, m2_dt_bias=v_m2_dt_bias, m2_d=v_m2_d, m2_norm_w=v_m2_norm_w, w_out=v_w_out,
                norm_ffn_w=v_norm_ffn_w, ffn_up=v_ffn_up, ffn_conv_w=v_ffn_conv_w, ffn_down=v_ffn_down,
                norm_final_w=v_norm_final_w)
    res = {}
    for name, st in (("w_in", st_win), ("w_out", st_wout), ("ffn_up", st_wup), ("ffn_down", st_wdown)):
        outs = _adamw("adamw_" + name, st, weights[name][0], mom1[name][0], mom2[name][0])
        res[name] = tuple(o[None] for o in outs)

    def adam_packed(label, staged, names):
        shapes = [weights[nm].shape for nm in names]
        outs = _adamw(label, staged, *[_pack([src[nm] for nm in names]) for src in (weights, mom1, mom2)])
        unpacked = [_unpack(o, shapes) for o in outs]
        for i, nm in enumerate(names):
            res[nm] = tuple(u[i] for u in unpacked)

    adam_packed("adamw_small_sharded", st_small, ["meta_tokens", "dn_conv_w", "m2_conv_w", "ffn_conv_w"])
    adam_packed("adamw_replicated", st_rep, rep_names)

    order = list(weights)
    grad_x = dh0[CH:][None]
    return (loss, grad_x, *[res[nm][0] for nm in order], *[res[nm][1] for nm in order], *[res[nm][2] for nm in order],
            *[res[nm][3] for nm in order])
```

```python
import functools

import jax
import jax.numpy as jnp
from jax import lax
from jax.experimental import pallas as pl
from jax.experimental.pallas import tpu as pltpu

F32 = jnp.float32
MXU = jnp.bfloat16
WIRE = jnp.bfloat16
HI = lax.Precision.HIGH

NDEV = 8
CH = 64
NMETA = 16
PADR = CH - NMETA
EPS = 1e-6
HD = 128
M2P = 64
M2G = 4
NST = 128
LANE = 128

ADAM_LR, ADAM_B1, ADAM_B2, ADAM_EPS, ADAM_WD, ADAM_STEP = 0.001, 0.9, 0.999, 1e-08, 0.01, 10

MESH_AXES = ("x", "y", "c")


def _pick(n, target, mult=16):
    best = None
    for t in range(mult, min(n, target) + 1, mult):
        if n % t == 0:
            best = t
    return best if best is not None else n


def _dg(a, b, ca, cb):
    return lax.dot_general(a.astype(MXU), b.astype(MXU), (((ca,), (cb,)), ((), ())), preferred_element_type=F32)


def _dgh(a, b, ca, cb):
    return lax.dot_general(a, b, (((ca,), (cb,)), ((), ())), precision=HI, preferred_element_type=F32)


def _silu(x):
    return x * jax.nn.sigmoid(x)


def _softplus(x):
    return jnp.maximum(x, 0.0) + jnp.log1p(jnp.exp(-jnp.abs(x)))


def _rms(x, w):
    return x * lax.rsqrt(jnp.mean(x * x, axis=-1, keepdims=True) + EPS) * w


def _cparams(sem, vmem_mb):
    return pltpu.CompilerParams(dimension_semantics=sem, vmem_limit_bytes=vmem_mb << 20)


def _mm(name, a, b, *, ta=False, tb=False, add=None, out_dtype=F32, tm=1024, tn=512, tk=2048):
    m, kdim = (a.shape[1], a.shape[0]) if ta else a.shape
    n = b.shape[0] if tb else b.shape[1]
    tm = _pick(m, tm, 128 if ta else 16)
    tn = _pick(n, tn, 128)
    tk = _pick(kdim, tk, 16 if (ta and not tb) else 128)
    nk = kdim // tk
    ca, cb = (0 if ta else 1), (1 if tb else 0)

    def body(*refs):
        a_ref, b_ref = refs[0], refs[1]
        add_ref = refs[2] if add is not None else None
        o_ref, acc = refs[-2], refs[-1]
        k = pl.program_id(2)

        @pl.when(k == 0)
        def _():
            acc[...] = jnp.zeros_like(acc)

        acc[...] += _dg(a_ref[...], b_ref[...], ca, cb)

        @pl.when(k == nk - 1)
        def _():
            r = acc[...]
            if add_ref is not None:
                r = r + add_ref[...].astype(F32)
            o_ref[...] = r.astype(o_ref.dtype)

    a_spec = pl.BlockSpec((tk, tm), lambda i, j, k: (k, i)) if ta else pl.BlockSpec((tm, tk), lambda i, j, k: (i, k))
    b_spec = pl.BlockSpec((tn, tk), lambda i, j, k: (j, k)) if tb else pl.BlockSpec((tk, tn), lambda i, j, k: (k, j))
    in_specs, ops = [a_spec, b_spec], [a, b]
    if add is not None:
        in_specs.append(pl.BlockSpec((tm, tn), lambda i, j, k: (i, j)))
        ops.append(add)
    return pl.pallas_call(
        body, name=name, grid=(m // tm, n // tn, nk), in_specs=in_specs,
        out_specs=pl.BlockSpec((tm, tn), lambda i, j, k: (i, j)),
        out_shape=jax.ShapeDtypeStruct((m, n), out_dtype),
        scratch_shapes=[pltpu.VMEM((tm, tn), F32)],
        compiler_params=_cparams(("parallel", "parallel", "arbitrary"), 48),
    )(*ops)


def _rw(name, fn, ins, outs, nrows, tm, ncol=1, vmem_mb=48):
    nrow = nrows // tm
    in_specs, ops = [], []
    for spec in ins:
        kind, arr, bw, cj = spec[:4]
        ops.append(arr)
        if kind == "r":
            ri = spec[4] if len(spec) > 4 else (lambda i: i)
            in_specs.append(pl.BlockSpec((tm, bw), lambda j, i, cj=cj, ri=ri: (ri(i), cj(j))))
        else:
            in_specs.append(pl.BlockSpec((arr.shape[0], bw), lambda j, i, cj=cj: (0, cj(j))))
    out_shape, out_specs = [], []
    for o in outs:
        if o[0] == "r":
            _, width, bw, cj, dt = o
            out_shape.append(jax.ShapeDtypeStruct((nrows, width), dt))
            out_specs.append(pl.BlockSpec((tm, bw), lambda j, i, cj=cj: (i, cj(j))))
        else:
            _, rows, width, bw, cj = o
            out_shape.append(jax.ShapeDtypeStruct((rows, width), F32))
            out_specs.append(pl.BlockSpec((rows, bw), lambda j, i, cj=cj: (0, cj(j))))
    n_in = len(ins)

    def body(*refs):
        j, i = pl.program_id(0), pl.program_id(1)
        rows = i * tm + lax.broadcasted_iota(jnp.int32, (tm, 1), 0)
        vals = []
        for spec, ref in zip(ins, refs[:n_in]):
            if spec[0] == "r" or spec[1].shape[0] == 1:
                vals.append(ref[...])
            else:
                vals.append([ref[pl.ds(r, 1), :] for r in range(spec[1].shape[0])])
        res = fn(rows, j, *vals)
        for o, val, ref in zip(outs, res, refs[n_in:]):
            if o[0] == "r":
                ref[...] = val.astype(ref.dtype)
            else:
                @pl.when(i == 0)
                def _(ref=ref):
                    ref[...] = jnp.zeros_like(ref)

                if o[1] == 1:
                    ref[...] += val
                else:
                    for r in range(o[1]):
                        ref[pl.ds(r, 1), :] += val[r]

    return pl.pallas_call(
        body, name=name, grid=(ncol, nrow), in_specs=in_specs, out_specs=out_specs, out_shape=out_shape,
        compiler_params=_cparams(("parallel", "arbitrary"), vmem_mb),
    )(*ops)


def _c0(j):
    return 0


def _cj(j):
    return j


def _shift(x, s):
    if s == 0:
        return x
    return pltpu.roll(x, s % x.shape[0], 0)


def _conv(x, w):
    k = len(w)
    return functools.reduce(lambda a, b: a + b, [w[j] * _shift(x, k - 1 - j) for j in range(k)])


def _conv_t(dy, w):
    k = len(w)
    return functools.reduce(lambda a, b: a + b, [w[j] * _shift(dy, -(k - 1 - j)) for j in range(k)])


def _conv_w(dy, x, k):
    return [jnp.sum(dy * _shift(x, k - 1 - j), axis=0, keepdims=True) for j in range(k)]


def _tri():
    r = lax.broadcasted_iota(jnp.int32, (CH, CH), 0)
    c = lax.broadcasted_iota(jnp.int32, (CH, CH), 1)
    return r, c


def _col(row):
    r, c = _tri()
    return jnp.sum(jnp.where(r == c, row, 0.0), axis=1, keepdims=True)


def _cumsum_rc(g_r):
    r, c = _tri()
    g_c = _col(g_r)
    cs_r = jnp.sum(jnp.where(r <= c, g_c, 0.0), axis=0, keepdims=True)
    cs_c = jnp.sum(jnp.where(c <= r, g_r, 0.0), axis=1, keepdims=True)
    return cs_r, cs_c


def _decay(cs_r, cs_c):
    r, c = _tri()
    return jnp.exp(jnp.where(c <= r, cs_c - cs_r, -jnp.inf))


def _gdn_a(ks, betas, gs):
    r, c = _tri()
    cs = [_cumsum_rc(g) for g in gs]
    kk = [_dg(k, k, 1, 1) for k in ks]
    return [jnp.where(c < r, _col(b) * kki * _decay(*csi), 0.0) for b, kki, csi in zip(betas, kk, cs)]


def _neumann(a_list):
    r, c = _tri()
    xs = [jnp.where(r == c, 1.0, 0.0) - a for a in a_list]
    ps = list(a_list)
    n = 2
    while n < CH:
        ps = [_dgh(p, p, 1, 0) for p in ps]
        xs = [x + _dgh(x, p, 1, 0) for x, p in zip(xs, ps)]
        n *= 2
    return xs


def _gdn_rest(ss, qs, ks, vs, betas, gs, ts):
    n = range(len(ss))
    cs = [_cumsum_rc(g) for g in gs]
    dm = [_decay(*csi) for csi in cs]
    ecs = [jnp.exp(csi[1]) for csi in cs]
    bc = [_col(b) for b in betas]
    u = [_dgh(ts[i], vs[i] * bc[i], 1, 0) for i in n]
    w = [_dgh(ts[i], ks[i] * (bc[i] * ecs[i]), 1, 0) for i in n]
    ws = [_dg(w[i], ss[i], 1, 0) for i in n]
    v_new = [u[i] - ws[i] for i in n]
    qk = [_dg(qs[i], ks[i], 1, 1) * dm[i] for i in n]
    o_in = [_dg(qs[i] * ecs[i], ss[i], 1, 0) for i in n]
    o = [o_in[i] + _dg(qk[i], v_new[i], 1, 0) for i in n]
    g_last = [jnp.sum(g, axis=1, keepdims=True) for g in gs]
    s_new = [ss[i] * jnp.exp(g_last[i]) + _dg(ks[i] * jnp.exp(g_last[i] - cs[i][1]), v_new[i], 0, 0) for i in n]
    return s_new, o


def _gdn_fwd(q, k, v, beta, g, hb):
    t_rows, d = q.shape
    nc, ng, w = t_rows // CH, d // (HD * hb), HD * hb
    sls = [slice(h * HD, (h + 1) * HD) for h in range(hb)]

    def body(q_ref, k_ref, v_ref, b_ref, g_ref, o_ref, ss_ref, ts_ref, s_scr):
        c = pl.program_id(1)

        @pl.when(c == 0)
        def _():
            s_scr[...] = jnp.zeros_like(s_scr)

        qs, ks, vs = ([ref[:, sl] for sl in sls] for ref in (q_ref, k_ref, v_ref))
        br = [b_ref[0, 0, pl.ds(h, 1), :] for h in range(hb)]
        gr = [g_ref[0, 0, pl.ds(h, 1), :] for h in range(hb)]
        s0 = [s_scr[h] for h in range(hb)]
        tm = _neumann(_gdn_a(ks, br, gr))
        s1, o = _gdn_rest(s0, qs, ks, vs, br, gr, tm)
        for h in range(hb):
            ss_ref[0, 0, h] = s0[h]
            ts_ref[0, 0, h] = tm[h]
            o_ref[:, sls[h]] = o[h]
            s_scr[h] = s1[h]

    blk = pl.BlockSpec((CH, w), lambda n, c: (c, n))
    row = pl.BlockSpec((1, 1, hb, CH), lambda n, c: (n, c, 0, 0))
    return pl.pallas_call(
        body, name="gdn_fwd", grid=(ng, nc), in_specs=[blk, blk, blk, row, row],
        out_specs=[blk, pl.BlockSpec((1, 1, hb, HD, HD), lambda n, c: (n, c, 0, 0, 0)),
                   pl.BlockSpec((1, 1, hb, CH, CH), lambda n, c: (n, c, 0, 0, 0))],
        out_shape=[jax.ShapeDtypeStruct((t_rows, d), F32), jax.ShapeDtypeStruct((ng, nc, hb, HD, HD), F32),
                   jax.ShapeDtypeStruct((ng, nc, hb, CH, CH), F32)],
        scratch_shapes=[pltpu.VMEM((hb, HD, HD), F32)],
        compiler_params=_cparams(("parallel", "arbitrary"), 32),
    )(q, k, v, beta, g)


def _gdn_bwd(q, k, v, beta, g, ss, ts, do, hb):
    t_rows, d = q.shape
    nc, ng, w = t_rows // CH, d // (HD * hb), HD * hb
    sls = [slice(h * HD, (h + 1) * HD) for h in range(hb)]

    def body(q_ref, k_ref, v_ref, b_ref, g_ref, ss_ref, ts_ref, do_ref, dq_ref, dk_ref, dv_ref, db_ref, dg_ref, ds_scr):
        cr = pl.program_id(1)

        @pl.when(cr == 0)
        def _():
            ds_scr[...] = jnp.zeros_like(ds_scr)

        first = cr == nc - 1
        rowi = lax.broadcasted_iota(jnp.int32, (CH, 1), 0)
        lani = lax.broadcasted_iota(jnp.int32, (1, CH), 1)
        keep_c = jnp.logical_or(jnp.logical_not(first), rowi >= PADR)
        keep_r = jnp.logical_or(jnp.logical_not(first), lani >= PADR)
        hs = range(hb)
        qs, ks, vs, dos = ([ref[:, sl] for sl in sls] for ref in (q_ref, k_ref, v_ref, do_ref))
        br = [b_ref[0, 0, pl.ds(h, 1), :] for h in hs]
        gr = [g_ref[0, 0, pl.ds(h, 1), :] for h in hs]
        tm = [ts_ref[0, 0, h] for h in hs]
        _, vjp_rest = jax.vjp(_gdn_rest, [ss_ref[0, 0, h] for h in hs], qs, ks, vs, br, gr, tm)
        ds0, dq, dk, dv, db, dg, dt = vjp_rest(([ds_scr[h] for h in hs], dos))
        dtt = [_dgh(dt[h], tm[h], 1, 1) for h in hs]
        da = [-_dgh(tm[h], dtt[h], 0, 0) for h in hs]
        _, vjp_a = jax.vjp(_gdn_a, ks, br, gr)
        dk2, db2, dg2 = vjp_a(da)
        for h in hs:
            ds_scr[h] = ds0[h]
            dq_ref[:, sls[h]] = jnp.where(keep_c, dq[h], 0.0)
            dk_ref[:, sls[h]] = jnp.where(keep_c, dk[h] + dk2[h], 0.0)
            dv_ref[:, sls[h]] = jnp.where(keep_c, dv[h], 0.0)
            db_ref[0, 0, pl.ds(h, 1), :] = jnp.where(keep_r, db[h] + db2[h], 0.0)
            dg_ref[0, 0, pl.ds(h, 1), :] = jnp.where(keep_r, dg[h] + dg2[h], 0.0)

    blk = pl.BlockSpec((CH, w), lambda n, c: (nc - 1 - c, n))
    row = pl.BlockSpec((1, 1, hb, CH), lambda n, c: (n, nc - 1 - c, 0, 0))
    return pl.pallas_call(
        body, name="gdn_bwd", grid=(ng, nc),
        in_specs=[blk, blk, blk, row, row, pl.BlockSpec((1, 1, hb, HD, HD), lambda n, c: (n, nc - 1 - c, 0, 0, 0)),
                  pl.BlockSpec((1, 1, hb, CH, CH), lambda n, c: (n, nc - 1 - c, 0, 0, 0)), blk],
        out_specs=[blk, blk, blk, row, row],
        out_shape=[jax.ShapeDtypeStruct((t_rows, d), F32)] * 3 + [jax.ShapeDtypeStruct((ng, nc, hb, CH), F32)] * 2,
        scratch_shapes=[pltpu.VMEM((hb, HD, HD), F32)],
        compiler_params=_cparams(("parallel", "arbitrary"), 32),
    )(q, k, v, beta, g, ss, ts, do)


def _ssd_group(s, xs, bm, cm, dt_r, a_r):
    prs = range(len(s))
    first = lax.broadcasted_iota(jnp.int32, (1, 2 * M2P), 1) < M2P

    def pick(vals, p):
        return jnp.where(first, vals[2 * p], vals[2 * p + 1])

    cs = [_cumsum_rc(a) for a in a_r]
    lm = [_decay(*csi) for csi in cs]
    ecs = [jnp.exp(csi[1]) for csi in cs]
    alast = [jnp.sum(a, axis=1, keepdims=True) for a in a_r]
    ealast = [jnp.exp(al) for al in alast]
    wt = [jnp.exp(al - csi[1]) for al, csi in zip(alast, cs)]
    dtc = [_col(t) for t in dt_r]
    xdt = [xs[:, p * LANE:(p + 1) * LANE] * pick(dtc, p) for p in prs]
    cb = _dg(cm, bm, 1, 1)
    y0 = [_dg(cb * lm[2 * p], xdt[p], 1, 0) for p in prs]
    y1 = [_dg(cb * lm[2 * p + 1], xdt[p], 1, 0) for p in prs]
    yo = [_dg(cm, s[p], 1, 0) for p in prs]
    y = [jnp.where(first, y0[p], y1[p]) + yo[p] * pick(ecs, p) for p in prs]
    s_new = [s[p] * pick(ealast, p) + _dg(bm, xdt[p] * pick(wt, p), 0, 0) for p in prs]
    return s_new, jnp.concatenate(y, axis=1)


def _ssd_specs(nc, d, rev):
    hpg = (d // M2P) // M2G
    gw = hpg * M2P
    cc = (lambda c: nc - 1 - c) if rev else (lambda c: c)
    xs = pl.BlockSpec((CH, gw), lambda g, c: (cc(c), g))
    bm = pl.BlockSpec((CH, NST), lambda g, c: (cc(c), d // LANE + g))
    cm = pl.BlockSpec((CH, NST), lambda g, c: (cc(c), d // LANE + M2G + g))
    row = pl.BlockSpec((1, 1, hpg, CH), lambda g, c: (g, cc(c), 0, 0))
    st = pl.BlockSpec((1, 1, hpg // 2, NST, LANE), lambda g, c: (g, cc(c), 0, 0, 0))
    return xs, bm, cm, row, st, hpg


def _ssd_fwd(xbc, dt, a, d):
    t_rows = xbc.shape[0]
    nc = t_rows // CH
    xs, bm, cm, row, st, hpg = _ssd_specs(nc, d, False)
    ppg = hpg // 2

    def body(xs_ref, b_ref, c_ref, dt_ref, a_ref, y_ref, ss_ref, s_scr):
        c = pl.program_id(1)

        @pl.when(c == 0)
        def _():
            s_scr[...] = jnp.zeros_like(s_scr)

        s0 = [s_scr[p] for p in range(ppg)]
        for p in range(ppg):
            ss_ref[0, 0, p] = s0[p]
        dt_r = [dt_ref[0, 0, pl.ds(h, 1), :] for h in range(hpg)]
        a_r = [a_ref[0, 0, pl.ds(h, 1), :] for h in range(hpg)]
        s1, y = _ssd_group(s0, xs_ref[...], b_ref[...], c_ref[...], dt_r, a_r)
        y_ref[...] = y
        for p in range(ppg):
            s_scr[p] = s1[p]

    return pl.pallas_call(
        body, name="ssd_fwd", grid=(M2G, nc), in_specs=[xs, bm, cm, row, row], out_specs=[xs, st],
        out_shape=[jax.ShapeDtypeStruct((t_rows, d), F32), jax.ShapeDtypeStruct((M2G, nc, ppg, NST, LANE), F32)],
        scratch_shapes=[pltpu.VMEM((ppg, NST, LANE), F32)],
        compiler_params=_cparams(("parallel", "arbitrary"), 32),
    )(xbc, xbc, xbc, dt, a)


def _ssd_bwd(xbc, dt, a, ss, dy, d):
    t_rows = xbc.shape[0]
    nc = t_rows // CH
    xs, bm, cm, row, st, hpg = _ssd_specs(nc, d, True)
    ppg = hpg // 2

    def body(xs_ref, b_ref, c_ref, dt_ref, a_ref, ss_ref, dy_ref, dxs_ref, db_ref, dc_ref, ddt_ref, da_ref, ds_scr):
        cr = pl.program_id(1)

        @pl.when(cr == 0)
        def _():
            ds_scr[...] = jnp.zeros_like(ds_scr)

        first = cr == nc - 1
        keep_c = jnp.logical_or(jnp.logical_not(first), lax.broadcasted_iota(jnp.int32, (CH, 1), 0) >= PADR)
        keep_r = jnp.logical_or(jnp.logical_not(first), lax.broadcasted_iota(jnp.int32, (1, CH), 1) >= PADR)
        dt_r = [dt_ref[0, 0, pl.ds(h, 1), :] for h in range(hpg)]
        a_r = [a_ref[0, 0, pl.ds(h, 1), :] for h in range(hpg)]
        s0 = [ss_ref[0, 0, p] for p in range(ppg)]
        _, vjp = jax.vjp(_ssd_group, s0, xs_ref[...], b_ref[...], c_ref[...], dt_r, a_r)
        ds0, dxs, db, dc, ddt, da = vjp(([ds_scr[p] for p in range(ppg)], dy_ref[...]))
        for p in range(ppg):
            ds_scr[p] = ds0[p]
        dxs_ref[...] = jnp.where(keep_c, dxs, 0.0)
        db_ref[...] = jnp.where(keep_c, db, 0.0)
        dc_ref[...] = jnp.where(keep_c, dc, 0.0)
        for h in range(hpg):
            ddt_ref[0, 0, pl.ds(h, 1), :] = jnp.where(keep_r, ddt[h], 0.0)
            da_ref[0, 0, pl.ds(h, 1), :] = jnp.where(keep_r, da[h], 0.0)

    grp = pl.BlockSpec((CH, NST), lambda g, c: (nc - 1 - c, g))
    return pl.pallas_call(
        body, name="ssd_bwd", grid=(M2G, nc), in_specs=[xs, bm, cm, row, row, st, xs],
        out_specs=[xs, grp, grp, row, row],
        out_shape=[jax.ShapeDtypeStruct((t_rows, d), F32)] + [jax.ShapeDtypeStruct((t_rows, M2G * NST), F32)] * 2
        + [jax.ShapeDtypeStruct((M2G, nc, hpg, CH), F32)] * 2,
        scratch_shapes=[pltpu.VMEM((ppg, NST, LANE), F32)],
        compiler_params=_cparams(("parallel", "arbitrary"), 32),
    )(xbc, xbc, xbc, dt, a, ss, dy)


def _exchange(name, gathers, scatters):
    arrays = list(gathers) + list(scatters)
    n_g, n = len(gathers), len(arrays)

    def body(*refs):
        ins, outs = refs[:n], refs[n:2 * n]
        send_sems, recv_sems, local_sems = refs[2 * n:]
        x, y, c = lax.axis_index("x"), lax.axis_index("y"), lax.axis_index("c")
        me = 4 * x + 2 * y + c

        def src(a, slot):
            return ins[a] if a < n_g else ins[a].at[slot]

        local = [pltpu.make_async_copy(src(a, me), outs[a].at[me], local_sems.at[a]) for a in range(n)]
        for cp in local:
            cp.start()
        copies = []
        for rel in range(1, NDEV):
            px, py, pc = x ^ (rel >> 2), y ^ ((rel >> 1) & 1), c ^ (rel & 1)
            peer = 4 * px + 2 * py + pc
            for a in range(n):
                copies.append(pltpu.make_async_remote_copy(
                    src_ref=src(a, peer), dst_ref=outs[a].at[me], send_sem=send_sems.at[a, rel - 1],
                    recv_sem=recv_sems.at[a, rel - 1], device_id=(px, py, pc), device_id_type=pl.DeviceIdType.MESH))
        for cp in copies:
            cp.start()
        for cp in copies:
            cp.wait_recv()
        for cp in copies:
            cp.wait_send()
        for cp in local:
            cp.wait()

    any_spec = pl.BlockSpec(memory_space=pl.ANY)
    out_shape = [jax.ShapeDtypeStruct((NDEV,) + a.shape, a.dtype) for a in gathers]
    out_shape += [jax.ShapeDtypeStruct(a.shape, a.dtype) for a in scatters]
    return pl.pallas_call(
        body, name=name, in_specs=[any_spec] * n, out_specs=[any_spec] * n, out_shape=out_shape,
        scratch_shapes=[pltpu.SemaphoreType.DMA((n, NDEV - 1)), pltpu.SemaphoreType.DMA((n, NDEV - 1)),
                        pltpu.SemaphoreType.DMA((n,))],
        compiler_params=pltpu.CompilerParams(has_side_effects=True),
    )(*arrays)


def _adamw(name, staged, w, m, v):
    r, c = w.shape
    tr = _pick(r, 256, 8)

    def body(st_ref, w_ref, m_ref, v_ref, g_ref, d_ref, nm_ref, nv_ref):
        g = st_ref[0].astype(F32)
        for k in range(1, NDEV):
            g = g + st_ref[k].astype(F32)
        m_new = ADAM_B1 * m_ref[...] + (1.0 - ADAM_B1) * g
        v_new = ADAM_B2 * v_ref[...] + (1.0 - ADAM_B2) * jnp.square(g)
        m_hat = m_new / (1.0 - ADAM_B1 ** ADAM_STEP)
        v_hat = v_new / (1.0 - ADAM_B2 ** ADAM_STEP)
        g_ref[...] = g
        d_ref[...] = -ADAM_LR * (m_hat / (jnp.sqrt(v_hat) + ADAM_EPS) + ADAM_WD * w_ref[...])
        nm_ref[...] = m_new
        nv_ref[...] = v_new

    blk = pl.BlockSpec((tr, c), lambda i: (i, 0))
    return pl.pallas_call(
        body, name=name, grid=(r // tr,), in_specs=[pl.BlockSpec((NDEV, tr, c), lambda i: (0, i, 0)), blk, blk, blk],
        out_specs=[blk] * 4, out_shape=[jax.ShapeDtypeStruct((r, c), F32)] * 4,
        compiler_params=_cparams(("parallel",), 48),
    )(staged, w, m, v)


def _pack(parts):
    flat = jnp.concatenate([p.reshape(-1).astype(F32) for p in parts])
    pad = (-flat.shape[0]) % (8 * LANE)
    return jnp.pad(flat, (0, pad)).reshape(-1, LANE)


def _unpack(slab, shapes):
    flat, out, off = slab.reshape(-1), [], 0
    for s in shapes:
        n = 1
        for dim in s:
            n *= dim
        out.append(flat[off:off + n].reshape(s))
        off += n
    return out


def _to_shards(full, axis):
    shp = full.shape
    t = full.reshape(shp[:axis] + (NDEV, shp[axis] // NDEV) + shp[axis + 1:])
    return jnp.moveaxis(t, axis, 0)


def _from_shards(g, axis):
    t = jnp.moveaxis(g, 0, axis)
    shp = t.shape
    return t.reshape(shp[:axis] + (shp[axis] * shp[axis + 1],) + shp[axis + 2:])


def kernel(x, meta_tokens, norm_mix_w, w_in, dn_conv_w, dn_a_log, dn_dt_bias, dn_norm_w, m2_conv_w, m2_conv_b, m2_a_log, m2_dt_bias, m2_d, m2_norm_w, w_out, norm_ffn_w, ffn_up, ffn_conv_w, ffn_down, norm_final_w, loss_target, m_meta_tokens, m_norm_mix_w, m_w_in, m_dn_conv_w, m_dn_a_log, m_dn_dt_bias, m_dn_norm_w, m_m2_conv_w, m_m2_conv_b, m_m2_a_log, m_m2_dt_bias, m_m2_d, m_m2_norm_w, m_w_out, m_norm_ffn_w, m_ffn_up, m_ffn_conv_w, m_ffn_down, m_norm_final_w, v_meta_tokens, v_norm_mix_w, v_w_in, v_dn_conv_w, v_dn_a_log, v_dn_dt_bias, v_dn_norm_w, v_m2_conv_w, v_m2_conv_b, v_m2_a_log, v_m2_dt_bias, v_m2_d, v_m2_norm_w, v_w_out, v_norm_ffn_w, v_ffn_up, v_ffn_conv_w, v_ffn_down, v_norm_final_w):
    seq, d = x.shape[1], x.shape[2]
    t_rows = seq + CH
    nc = t_rows // CH
    dnh, m2h = d // HD, d // M2P
    dff = ffn_down.shape[1] * NDEV
    xbc_w = d + 2 * M2G * NST
    assert seq % CH == 0 and d % (2 * M2P * M2G) == 0 and 2 * dnh + m2h <= LANE
    hb = max(h for h in (8, 4, 2, 1) if dnh % h == 0)
    tm_rw = _pick(t_rows, 208, 16)

    small_sharded = [meta_tokens, dn_conv_w[0], m2_conv_w[0], ffn_conv_w[0]]
    small_shapes = [p.shape for p in small_sharded]
    g_win, g_wout, g_wup, g_wdown, g_small = _exchange(
        "gather_weights",
        [w_in[0].astype(WIRE), w_out[0].astype(WIRE), ffn_up[0].astype(WIRE), ffn_down[0].astype(WIRE), _pack(small_sharded)], [])
    win = _from_shards(g_win, 1)
    wout = _from_shards(g_wout, 0)
    wup = _from_shards(g_wup, 1)
    wdown = _from_shards(g_wdown, 0)
    small_full = [_unpack(g_small[k], small_shapes) for k in range(NDEV)]
    meta_f, dnconv_f, m2conv_f, ffnconv_f = [jnp.concatenate([small_full[k][i] for k in range(NDEV)], axis=-1) for i in range(4)]

    o_z, o_b, o_a = 3 * d, 4 * d, 4 * d + dnh
    o_m2z = 4 * d + 2 * dnh
    o_xbc, o_dt = o_m2z + d, o_m2z + d + xbc_w
    w_small = jnp.concatenate([win[:, o_b:o_m2z], win[:, o_dt:], jnp.zeros((d, LANE - 2 * dnh - m2h), WIRE)], axis=1)
    w_seg = {"q": win[:, :d], "k": win[:, d:2 * d], "v": win[:, 2 * d:3 * d], "z": win[:, o_z:o_b],
             "m2z": win[:, o_m2z:o_xbc], "xbc": win[:, o_xbc:o_dt], "sm": w_small}
    wup_g, wup_v = wup[:, :dff], wup[:, dff:]

    h0 = jnp.concatenate([jnp.zeros((PADR, d), F32), meta_f, x[0]], axis=0)
    valid = lambda rows: rows >= PADR

    def norm_fwd(name, h, w):
        return _rw(name, lambda rows, j, hv, wv: (_rms(hv, wv),), [("r", h, d, _c0), ("p", w, d, _c0)],
                   [("r", d, d, _c0, MXU)], t_rows, tm_rw)[0]

    hn1 = norm_fwd("norm_mix", h0, norm_mix_w)
    proj = {s: _mm("proj_" + s, hn1, w_seg[s]) for s in w_seg}

    def dn_post(sec, cv):
        s = _silu(cv)
        if sec < 2:
            s = s * lax.rsqrt(jnp.sum(s * s, axis=-1, keepdims=True) + EPS)
        if sec == 0:
            s = s * (HD ** -0.5)
        return s

    def dn_prep(sec, name):
        def fn(rows, j, p, w):
            return (jnp.where(valid(rows), dn_post(sec, _conv(p, w)), 0.0),)
        wc = dnconv_f[:, sec * d:(sec + 1) * d]
        return _rw("dn_prep_" + name, fn, [("r", proj[name], HD, _cj), ("p", wc, HD, _cj)], [("r", d, HD, _cj, F32)],
                   t_rows, t_rows, ncol=dnh)[0]

    q_act, k_act, v_act = dn_prep(0, "q"), dn_prep(1, "k"), dn_prep(2, "v")

    lane = lambda: lax.broadcasted_iota(jnp.int32, (1, LANE), 1)

    def lanes_of(vec, off):
        return jnp.pad(vec.astype(F32), ((0, 0), (off, LANE - off - vec.shape[1])))

    gate_params = [lanes_of(dn_a_log, dnh), lanes_of(dn_dt_bias, dnh), lanes_of(m2_a_log, 2 * dnh), lanes_of(m2_dt_bias, 2 * dnh)]

    def gates(rows, sm, p_alog, p_dtb, p_malog, p_mdtb):
        ln = lane()
        is_b, is_g = ln < dnh, jnp.logical_and(ln >= dnh, ln < 2 * dnh)
        is_d = jnp.logical_and(ln >= 2 * dnh, ln < 2 * dnh + m2h)
        beta = jax.nn.sigmoid(sm)
        gdec = -jnp.exp(p_alog) * _softplus(sm + p_dtb)
        dt = _softplus(sm + p_mdtb)
        am = dt * (-jnp.exp(p_malog))
        ok = valid(rows)
        g1 = jnp.where(ok, jnp.where(is_b, beta, jnp.where(is_g, gdec, jnp.where(is_d, dt, 0.0))), 0.0)
        g2 = jnp.where(jnp.logical_and(ok, is_d), am, 0.0)
        return g1, g2

    gate_ins = [("r", proj["sm"], LANE, _c0)] + [("p", p, LANE, _c0) for p in gate_params]
    g1, g2 = _rw("gates", lambda rows, j, *a: gates(rows, *a), gate_ins,
                 [("r", LANE, LANE, _c0, F32), ("r", LANE, LANE, _c0, F32)], t_rows, tm_rw)

    def head_rows(cols, per):
        n = cols.shape[1]
        return cols.reshape(nc, CH, n // per, per).transpose(2, 0, 3, 1)

    def head_cols(rows_):
        ngrp, _, per, _ = rows_.shape
        return rows_.transpose(1, 3, 0, 2).reshape(t_rows, ngrp * per)

    beta_r, gdec_r = head_rows(g1[:, :dnh], hb), head_rows(g1[:, dnh:2 * dnh], hb)
    hpg = m2h // M2G
    dt_r, am_r = head_rows(g1[:, 2 * dnh:2 * dnh + m2h], hpg), head_rows(g2[:, 2 * dnh:2 * dnh + m2h], hpg)

    o_dn, dn_states, dn_tinv = _gdn_fwd(q_act, k_act, v_act, beta_r, gdec_r, hb)

    def dn_out(o, z, w):
        outs = []
        for h in range(dnh):
            sl = slice(h * HD, (h + 1) * HD)
            outs.append(_rms(o[:, sl], w) * _silu(z[:, sl]))
        return jnp.concatenate(outs, axis=1)

    mixed_dn = _rw("dn_out", lambda rows, j, o, z, w: (dn_out(o, z, w),),
                   [("r", o_dn, d, _c0), ("r", proj["z"], d, _c0), ("p", dn_norm_w, HD, _c0)], [("r", d, d, _c0, MXU)],
                   t_rows, tm_rw)[0]

    def m2_prep(rows, j, p, w, b):
        return (jnp.where(valid(rows), _silu(_conv(p, w) + b), 0.0),)

    xbc_act = _rw("m2_prep", m2_prep, [("r", proj["xbc"], LANE, _cj), ("p", m2conv_f, LANE, _cj), ("p", m2_conv_b, LANE, _cj)],
                  [("r", xbc_w, LANE, _cj, F32)], t_rows, t_rows, ncol=xbc_w // LANE)[0]
    y_ssd, m2_states = _ssd_fwd(xbc_act, dt_r, am_r, d)

    d_lanes = jnp.repeat(m2_d.astype(F32), M2P, axis=1)
    gw = d // M2G

    def m2_out(ys, xs, z, dl, nw):
        yv = (ys + dl * xs) * _silu(z)
        outs = []
        for gi in range(M2G):
            sl = slice(gi * gw, (gi + 1) * gw)
            outs.append(_rms(yv[:, sl], nw[:, sl]))
        return jnp.concatenate(outs, axis=1)

    m2_out_ins = [("r", y_ssd, d, _c0), ("r", xbc_act, d, _c0), ("r", proj["m2z"], d, _c0), ("p", d_lanes, d, _c0),
                  ("p", m2_norm_w, d, _c0)]
    mixed_m2 = _rw("m2_out", lambda rows, j, *a: (m2_out(*a),), m2_out_ins, [("r", d, d, _c0, MXU)], t_rows, tm_rw)[0]

    mixed = jnp.concatenate([mixed_dn, mixed_m2], axis=1)
    h1 = _mm("out_proj", mixed, wout, add=h0)
    hn2 = norm_fwd("norm_ffn", h1, norm_ffn_w)
    u_g, u_v = _mm("ffn_up_g", hn2, wup_g), _mm("ffn_up_v", hn2, wup_v)
    fc_g, fc_v = ffnconv_f[:, :dff], ffnconv_f[:, dff:]

    def ffn_act(rows, j, ug, uv, wg, wv):
        return (jnp.where(valid(rows), _silu(_conv(ug, wg)) * _conv(uv, wv), 0.0),)

    act = _rw("ffn_act", ffn_act, [("r", u_g, LANE, _cj), ("r", u_v, LANE, _cj), ("p", fc_g, LANE, _cj), ("p", fc_v, LANE, _cj)],
              [("r", dff, LANE, _cj, MXU)], t_rows, t_rows, ncol=dff // LANE)[0]
    h2 = _mm("ffn_down", act, wdown, add=h1, tk=1408)

    def loss_fn(hv, wf, tgt, rows):
        err = jnp.where(rows >= CH, _rms(hv, wf) - tgt, 0.0)
        return 0.5 * jnp.sum(jnp.mean(err * err, axis=-1, keepdims=True), axis=0, keepdims=True)

    def final(rows, j, hv, wf, tgt):
        loss, vjp = jax.vjp(lambda a, b: loss_fn(a, b, tgt, rows), hv, wf)
        dh, dw = vjp(jnp.ones((1, 1), F32))
        return dh, dh, dw, jnp.broadcast_to(loss, (1, LANE))

    wf2 = norm_final_w.reshape(1, d)
    dh2, dh2_m, d_wf, loss_part = _rw(
        "loss_head", final, [("r", h2, d, _c0), ("p", wf2, d, _c0), ("r", loss_target[0], d, _c0, lambda i: jnp.maximum(i - 1, 0))],
        [("r", d, d, _c0, F32), ("r", d, d, _c0, MXU), ("p", 1, d, d, _c0), ("p", 1, LANE, LANE, _c0)], t_rows, CH)
    loss = lax.psum(loss_part[0, 0], MESH_AXES)

    d_act = _mm("d_act", dh2_m, wdown, tb=True)
    gw_down = _mm("gw_down", act, dh2_m, ta=True, tm=1408, tn=1024, tk=1040)

    def ffn_act_bwd(rows, j, ug, uv, wg, wv, da):
        cg, cv = _conv(ug, wg), _conv(uv, wv)
        _, vjp = jax.vjp(lambda a, b: _silu(a) * b, cg, cv)
        dcg, dcv = vjp(jnp.where(valid(rows), da, 0.0))
        return _conv_t(dcg, wg), _conv_t(dcv, wv), _conv_w(dcg, ug, len(wg)), _conv_w(dcv, uv, len(wv))

    kf = fc_g.shape[0]
    du_g, du_v, g_fc_g, g_fc_v = _rw(
        "ffn_act_bwd", ffn_act_bwd,
        [("r", u_g, LANE, _cj), ("r", u_v, LANE, _cj), ("p", fc_g, LANE, _cj), ("p", fc_v, LANE, _cj), ("r", d_act, LANE, _cj)],
        [("r", dff, LANE, _cj, MXU), ("r", dff, LANE, _cj, MXU), ("p", kf, dff, LANE, _cj), ("p", kf, dff, LANE, _cj)],
        t_rows, t_rows, ncol=dff // LANE)
    d_hn2 = _mm("d_hn2_v", du_v, wup_v, tb=True, tk=1408, add=_mm("d_hn2_g", du_g, wup_g, tb=True, tk=1408))
    gw_up_g = _mm("gw_up_g", hn2, du_g, ta=True, tm=1024, tn=1408, tk=1040)
    gw_up_v = _mm("gw_up_v", hn2, du_v, ta=True, tm=1024, tn=1408, tk=1040)

    def norm_bwd(name, h, w, dy, dres):
        def fn(rows, j, hv, wv, dyv, dr):
            _, vjp = jax.vjp(_rms, hv, wv)
            dh, dw = vjp(dyv)
            dh = dh + dr
            return dh, dh, dw
        return _rw(name, fn, [("r", h, d, _c0), ("p", w, d, _c0), ("r", dy, d, _c0), ("r", dres, d, _c0)],
                   [("r", d, d, _c0, F32), ("r", d, d, _c0, MXU), ("p", 1, d, d, _c0)], t_rows, tm_rw)

    dh1, dh1_m, g_norm_ffn = norm_bwd("norm_ffn_bwd", h1, norm_ffn_w, d_hn2, dh2)

    d_mixed = _mm("d_mixed", dh1_m, wout, tb=True)
    gw_out = _mm("gw_out", mixed, dh1_m, ta=True, tm=1024, tn=1024, tk=1040)

    def fold_heads(vec):
        r = lax.broadcasted_iota(jnp.int32, (d, LANE), 0)
        c = lax.broadcasted_iota(jnp.int32, (d, LANE), 1)
        return _dgh(vec, jnp.where(jnp.logical_and(r >= c * M2P, r < (c + 1) * M2P), 1.0, 0.0), 1, 0)

    def m2_out_bwd(rows, j, ys, xs, z, dl, nw, dy):
        _, vjp = jax.vjp(m2_out, ys, xs, z, dl, nw)
        dys, dxs, dz, ddl, dnw = vjp(dy)
        return dys, dxs, dz, fold_heads(ddl), dnw

    dy_ssd, dxs_skip, d_m2z, g_m2_d, g_m2_norm = _rw(
        "m2_out_bwd", m2_out_bwd, m2_out_ins + [("r", d_mixed, d, lambda j: 1)],
        [("r", d, d, _c0, F32), ("r", d, d, _c0, F32), ("r", d, d, _c0, MXU), ("p", 1, LANE, LANE, _c0), ("p", 1, d, d, _c0)],
        t_rows, _pick(t_rows, 208, 16))

    dxs, db_ssd, dc_ssd, ddt_r, dam_r = _ssd_bwd(xbc_act, dt_r, am_r, m2_states, dy_ssd, d)


    def m2_prep_bwd(n_d):
        def fn(rows, j, p, w, b, *ds):
            pre = _conv(p, w) + b
            _, vjp = jax.vjp(_silu, pre)
            dpre, = vjp(jnp.where(valid(rows), functools.reduce(lambda a_, b_: a_ + b_, ds), 0.0))
            return _conv_t(dpre, w), _conv_w(dpre, p, len(w)), jnp.sum(dpre, axis=0, keepdims=True)
        return fn

    def m2_prep_bwd_call(name, off, width, d_ins):
        blk0 = off // LANE
        ins = [("r", proj["xbc"], LANE, lambda j: blk0 + j), ("p", m2conv_f, LANE, lambda j: blk0 + j),
               ("p", m2_conv_b, LANE, lambda j: blk0 + j)] + d_ins
        km = m2conv_f.shape[0]
        return _rw(name, m2_prep_bwd(len(d_ins)), ins,
                   [("r", width, LANE, _cj, MXU), ("p", km, width, LANE, _cj), ("p", 1, width, LANE, _cj)],
                   t_rows, t_rows, ncol=width // LANE)

    dp_xs, gcw_xs, gcb_xs = m2_prep_bwd_call("m2_prep_bwd_x", 0, d, [("r", dxs, LANE, _cj), ("r", dxs_skip, LANE, _cj)])
    dp_b, gcw_b, gcb_b = m2_prep_bwd_call("m2_prep_bwd_b", d, M2G * NST, [("r", db_ssd, LANE, _cj)])
    dp_c, gcw_c, gcb_c = m2_prep_bwd_call("m2_prep_bwd_c", d + M2G * NST, M2G * NST, [("r", dc_ssd, LANE, _cj)])
    d_pxbc = jnp.concatenate([dp_xs, dp_b, dp_c], axis=1)
    g_m2_conv = jnp.concatenate([gcw_xs, gcw_b, gcw_c], axis=1)
    g_m2_conv_b = jnp.concatenate([gcb_xs, gcb_b, gcb_c], axis=1)

    def dn_out_bwd(rows, j, o, z, w, dy):
        _, vjp = jax.vjp(dn_out, o, z, w)
        return vjp(dy)

    d_o, d_z, g_dn_norm = _rw(
        "dn_out_bwd", dn_out_bwd,
        [("r", o_dn, d, _c0), ("r", proj["z"], d, _c0), ("p", dn_norm_w, HD, _c0), ("r", d_mixed, d, _c0)],
        [("r", d, d, _c0, F32), ("r", d, d, _c0, MXU), ("p", 1, HD, HD, _c0)], t_rows, _pick(t_rows, 208, 16))

    dq, dk, dv, dbeta_r, dgdec_r = _gdn_bwd(q_act, k_act, v_act, beta_r, gdec_r, dn_states, dn_tinv, d_o, hb)

    def dn_prep_bwd(sec, name, dact):
        def fn(rows, j, p, w, da):
            cv = _conv(p, w)
            _, vjp = jax.vjp(functools.partial(dn_post, sec), cv)
            dcv, = vjp(jnp.where(valid(rows), da, 0.0))
            return _conv_t(dcv, w), _conv_w(dcv, p, len(w))
        wc = dnconv_f[:, sec * d:(sec + 1) * d]
        return _rw("dn_prep_bwd_" + name, fn, [("r", proj[name], HD, _cj), ("p", wc, HD, _cj), ("r", dact, HD, _cj)],
                   [("r", d, HD, _cj, MXU), ("p", wc.shape[0], d, HD, _cj)], t_rows, t_rows, ncol=dnh)

    (dp_q, gcw_q), (dp_k, gcw_k), (dp_v, gcw_v) = dn_prep_bwd(0, "q", dq), dn_prep_bwd(1, "k", dk), dn_prep_bwd(2, "v", dv)
    g_dn_conv = jnp.concatenate([gcw_q, gcw_k, gcw_v], axis=1)

    zpad = jnp.zeros((t_rows, LANE - 2 * dnh - m2h), F32)
    dg1 = jnp.concatenate([head_cols(dbeta_r), head_cols(dgdec_r), head_cols(ddt_r), zpad], axis=1)
    dg2 = jnp.concatenate([jnp.zeros((t_rows, 2 * dnh), F32), head_cols(dam_r), zpad], axis=1)

    def gates_bwd(rows, j, sm, pa, pb, pc, pd, d1, d2):
        _, vjp = jax.vjp(lambda *a: gates(rows, *a), sm, pa, pb, pc, pd)
        return vjp((d1, d2))

    dp_sm, g_pa, g_pb, g_pc, g_pd = _rw(
        "gates_bwd", gates_bwd, gate_ins + [("r", dg1, LANE, _c0), ("r", dg2, LANE, _c0)],
        [("r", LANE, LANE, _c0, MXU)] + [("p", 1, LANE, LANE, _c0)] * 4, t_rows, tm_rw)

    dseg = {"q": dp_q, "k": dp_k, "v": dp_v, "z": d_z, "m2z": d_m2z, "xbc": d_pxbc, "sm": dp_sm}
    d_hn1 = None
    for s in dseg:
        d_hn1 = _mm("d_hn1_" + s, dseg[s], w_seg[s], tb=True, tk=2048, add=d_hn1)
    gw_seg = {s: _mm("gw_in_" + s, hn1, dseg[s], ta=True, tm=1024, tn=1024, tk=1040) for s in dseg}
    dh0, _, g_norm_mix = norm_bwd("norm_mix_bwd", h0, norm_mix_w, d_hn1, dh1)

    gsm = gw_seg["sm"]
    gw_in_full = jnp.concatenate([gw_seg["q"], gw_seg["k"], gw_seg["v"], gw_seg["z"], gsm[:, :2 * dnh], gw_seg["m2z"],
                                  gw_seg["xbc"], gsm[:, 2 * dnh:2 * dnh + m2h]], axis=1)
    gw_up_full = jnp.concatenate([gw_up_g, gw_up_v], axis=1)
    g_ffn_conv = jnp.concatenate([g_fc_g, g_fc_v], axis=1)
    small_parts = [_to_shards(dh0[PADR:CH], 1), _to_shards(g_dn_conv, 1), _to_shards(g_m2_conv, 1), _to_shards(g_ffn_conv, 1)]
    small_scatter = jnp.stack([_pack([p[k] for p in small_parts]) for k in range(NDEV)])

    rep_names = ["norm_mix_w", "dn_a_log", "dn_dt_bias", "dn_norm_w", "m2_conv_b", "m2_a_log", "m2_dt_bias", "m2_d",
                 "m2_norm_w", "norm_ffn_w", "norm_final_w"]
    rep_grads = [g_norm_mix, g_pa[:, dnh:2 * dnh], g_pb[:, dnh:2 * dnh], g_dn_norm, g_m2_conv_b, g_pc[:, 2 * dnh:2 * dnh + m2h],
                 g_pd[:, 2 * dnh:2 * dnh + m2h], g_m2_d[:, :m2h], g_m2_norm, g_norm_ffn, d_wf.reshape(d)]
    st_rep, st_win, st_wout, st_wup, st_wdown, st_small = _exchange(
        "exchange_grads", [_pack(rep_grads)],
        [_to_shards(gw_in_full, 1).astype(WIRE), _to_shards(gw_out, 0).astype(WIRE), _to_shards(gw_up_full, 1).astype(WIRE),
         _to_shards(gw_down, 0).astype(WIRE), small_scatter])

    weights = dict(meta_tokens=meta_tokens, norm_mix_w=norm_mix_w, w_in=w_in, dn_conv_w=dn_conv_w, dn_a_log=dn_a_log,
                   dn_dt_bias=dn_dt_bias, dn_norm_w=dn_norm_w, m2_conv_w=m2_conv_w, m2_conv_b=m2_conv_b, m2_a_log=m2_a_log,
                   m2_dt_bias=m2_dt_bias, m2_d=m2_d, m2_norm_w=m2_norm_w, w_out=w_out, norm_ffn_w=norm_ffn_w, ffn_up=ffn_up,
                   ffn_conv_w=ffn_conv_w, ffn_down=ffn_down, norm_final_w=norm_final_w)
    mom1 = dict(meta_tokens=m_meta_tokens, norm_mix_w=m_norm_mix_w, w_in=m_w_in, dn_conv_w=m_dn_conv_w, dn_a_log=m_dn_a_log,
                dn_dt_bias=m_dn_dt_bias, dn_norm_w=m_dn_norm_w, m2_conv_w=m_m2_conv_w, m2_conv_b=m_m2_conv_b,
                m2_a_log=m_m2_a_log, m2_dt_bias=m_m2_dt_bias, m2_d=m_m2_d, m2_norm_w=m_m2_norm_w, w_out=m_w_out,
                norm_ffn_w=m_norm_ffn_w, ffn_up=m_ffn_up, ffn_conv_w=m_ffn_conv_w, ffn_down=m_ffn_down,
                norm_final_w=m_norm_final_w)
    mom2 = dict(meta_tokens=v_meta_tokens, norm_mix_w=v_norm_mix_w, w_in=v_w_in, dn_conv_w=v_dn_conv_w, dn_a_log=v_dn_a_log,
                dn_dt_bias=v_dn_dt_bias, dn_norm_w=v_dn_norm_w, m2_conv_w=v_m2_conv_w, m2_conv_b=v_m2_conv_b,
                m2_a_log=v_m2_a_log, m2_dt_bias=v_m2_dt_bias, m2_d=v_m2_d, m2_norm_w=v_m2_norm_w, w_out=v_w_out,
                norm_ffn_w=v_norm_ffn_w, ffn_up=v_ffn_up, ffn_conv_w=v_ffn_conv_w, ffn_down=v_ffn_down,
                norm_final_w=v_norm_final_w)
    res = {}
    for name, st in (("w_in", st_win), ("w_out", st_wout), ("ffn_up", st_wup), ("ffn_down", st_wdown)):
        outs = _adamw("adamw_" + name, st, weights[name][0], mom1[name][0], mom2[name][0])
        res[name] = tuple(o[None] for o in outs)

    def adam_packed(label, staged, names):
        shapes = [weights[nm].shape for nm in names]
        outs = _adamw(label, staged, *[_pack([src[nm] for nm in names]) for src in (weights, mom1, mom2)])
        unpacked = [_unpack(o, shapes) for o in outs]
        for i, nm in enumerate(names):
            res[nm] = tuple(u[i] for u in unpacked)

    adam_packed("adamw_small_sharded", st_small, ["meta_tokens", "dn_conv_w", "m2_conv_w", "ffn_conv_w"])
    adam_packed("adamw_replicated", st_rep, rep_names)

    order = list(weights)
    grad_x = dh0[CH:][None]
    return (loss, grad_x, *[res[nm][0] for nm in order], *[res[nm][1] for nm in order], *[res[nm][2] for nm in order],
            *[res[nm][3] for nm in order])
```

```python
import functools

import jax
import jax.numpy as jnp
from jax import lax
from jax.experimental import pallas as pl
from jax.experimental.pallas import tpu as pltpu

F32 = jnp.float32
MXU = jnp.bfloat16
WIRE = jnp.bfloat16
HI = lax.Precision.HIGH

NDEV = 8
CH = 64
NMETA = 16
PADR = CH - NMETA
EPS = 1e-6
HD = 128
M2P = 64
M2G = 4
NST = 128
LANE = 128

ADAM_LR, ADAM_B1, ADAM_B2, ADAM_EPS, ADAM_WD, ADAM_STEP = 0.001, 0.9, 0.999, 1e-08, 0.01, 10

MESH_AXES = ("x", "y", "c")


def _pick(n, target, mult=16):
    best = None
    for t in range(mult, min(n, target) + 1, mult):
        if n % t == 0:
            best = t
    return best if best is not None else n


def _dg(a, b, ca, cb):
    return lax.dot_general(a.astype(MXU), b.astype(MXU), (((ca,), (cb,)), ((), ())), preferred_element_type=F32)


def _dgh(a, b, ca, cb):
    return lax.dot_general(a, b, (((ca,), (cb,)), ((), ())), precision=HI, preferred_element_type=F32)


def _silu(x):
    return x * jax.nn.sigmoid(x)


def _softplus(x):
    return jnp.maximum(x, 0.0) + jnp.log1p(jnp.exp(-jnp.abs(x)))


def _rms(x, w):
    return x * lax.rsqrt(jnp.mean(x * x, axis=-1, keepdims=True) + EPS) * w


def _cparams(sem, vmem_mb):
    return pltpu.CompilerParams(dimension_semantics=sem, vmem_limit_bytes=vmem_mb << 20)


def _mm(name, a, b, *, ta=False, tb=False, add=None, out_dtype=F32, tm=1024, tn=512, tk=2048, dep=None):
    m, kdim = (a.shape[1], a.shape[0]) if ta else a.shape
    n = b.shape[0] if tb else b.shape[1]
    tm = _pick(m, tm, 128 if ta else 16)
    tn = _pick(n, tn, 128)
    tk = _pick(kdim, tk, 16 if (ta and not tb) else 128)
    nk = kdim // tk
    ca, cb = (0 if ta else 1), (1 if tb else 0)

    def body(*refs):
        a_ref, b_ref = refs[0], refs[1]
        add_ref = refs[2] if add is not None else None
        o_ref, acc = refs[-2], refs[-1]
        k = pl.program_id(2)

        @pl.when(k == 0)
        def _():
            acc[...] = jnp.zeros_like(acc)

        acc[...] += _dg(a_ref[...], b_ref[...], ca, cb)

        @pl.when(k == nk - 1)
        def _():
            r = acc[...]
            if add_ref is not None:
                r = r + add_ref[...].astype(F32)
            o_ref[...] = r.astype(o_ref.dtype)

    a_spec = pl.BlockSpec((tk, tm), lambda i, j, k: (k, i)) if ta else pl.BlockSpec((tm, tk), lambda i, j, k: (i, k))
    b_spec = pl.BlockSpec((tn, tk), lambda i, j, k: (j, k)) if tb else pl.BlockSpec((tk, tn), lambda i, j, k: (k, j))
    in_specs, ops = [a_spec, b_spec], [a, b]
    if add is not None:
        in_specs.append(pl.BlockSpec((tm, tn), lambda i, j, k: (i, j)))
        ops.append(add)
    if dep is not None:
        in_specs.append(pl.BlockSpec((8, LANE), lambda i, j, k: (0, 0)))
        ops.append(dep)
    return pl.pallas_call(
        body, name=name, grid=(m // tm, n // tn, nk), in_specs=in_specs,
        out_specs=pl.BlockSpec((tm, tn), lambda i, j, k: (i, j)),
        out_shape=jax.ShapeDtypeStruct((m, n), out_dtype),
        scratch_shapes=[pltpu.VMEM((tm, tn), F32)],
        compiler_params=_cparams(("parallel", "parallel", "arbitrary"), 48),
    )(*ops)


def _rw(name, fn, ins, outs, nrows, tm, ncol=1, vmem_mb=48):
    nrow = nrows // tm
    in_specs, ops = [], []
    for spec in ins:
        kind, arr, bw, cj = spec[:4]
        ops.append(arr)
        if kind == "r":
            ri = spec[4] if len(spec) > 4 else (lambda i: i)
            in_specs.append(pl.BlockSpec((tm, bw), lambda j, i, cj=cj, ri=ri: (ri(i), cj(j))))
        else:
            in_specs.append(pl.BlockSpec((arr.shape[0], bw), lambda j, i, cj=cj: (0, cj(j))))
    out_shape, out_specs = [], []
    for o in outs:
        if o[0] == "r":
            _, width, bw, cj, dt = o
            out_shape.append(jax.ShapeDtypeStruct((nrows, width), dt))
            out_specs.append(pl.BlockSpec((tm, bw), lambda j, i, cj=cj: (i, cj(j))))
        else:
            _, rows, width, bw, cj = o
            out_shape.append(jax.ShapeDtypeStruct((rows, width), F32))
            out_specs.append(pl.BlockSpec((rows, bw), lambda j, i, cj=cj: (0, cj(j))))
    n_in = len(ins)

    def body(*refs):
        j, i = pl.program_id(0), pl.program_id(1)
        rows = i * tm + lax.broadcasted_iota(jnp.int32, (tm, 1), 0)
        vals = []
        for spec, ref in zip(ins, refs[:n_in]):
            if spec[0] == "r" or spec[1].shape[0] == 1:
                vals.append(ref[...])
            else:
                vals.append([ref[pl.ds(r, 1), :] for r in range(spec[1].shape[0])])
        res = fn(rows, j, *vals)
        for o, val, ref in zip(outs, res, refs[n_in:]):
            if o[0] == "r":
                ref[...] = val.astype(ref.dtype)
            else:
                @pl.when(i == 0)
                def _(ref=ref):
                    ref[...] = jnp.zeros_like(ref)

                if o[1] == 1:
                    ref[...] += val
                else:
                    for r in range(o[1]):
                        ref[pl.ds(r, 1), :] += val[r]

    return pl.pallas_call(
        body, name=name, grid=(ncol, nrow), in_specs=in_specs, out_specs=out_specs, out_shape=out_shape,
        compiler_params=_cparams(("parallel", "arbitrary"), vmem_mb),
    )(*ops)


def _c0(j):
    return 0


def _cj(j):
    return j


def _shift(x, s):
    if s == 0:
        return x
    return pltpu.roll(x, s % x.shape[0], 0)


def _conv(x, w):
    k = len(w)
    return functools.reduce(lambda a, b: a + b, [w[j] * _shift(x, k - 1 - j) for j in range(k)])


def _conv_t(dy, w):
    k = len(w)
    return functools.reduce(lambda a, b: a + b, [w[j] * _shift(dy, -(k - 1 - j)) for j in range(k)])


def _conv_w(dy, x, k):
    return [jnp.sum(dy * _shift(x, k - 1 - j), axis=0, keepdims=True) for j in range(k)]


def _tri():
    r = lax.broadcasted_iota(jnp.int32, (CH, CH), 0)
    c = lax.broadcasted_iota(jnp.int32, (CH, CH), 1)
    return r, c


def _col(row):
    r, c = _tri()
    return jnp.sum(jnp.where(r == c, row, 0.0), axis=1, keepdims=True)


def _cumsum_rc(g_r):
    r, c = _tri()
    g_c = _col(g_r)
    cs_r = jnp.sum(jnp.where(r <= c, g_c, 0.0), axis=0, keepdims=True)
    cs_c = jnp.sum(jnp.where(c <= r, g_r, 0.0), axis=1, keepdims=True)
    return cs_r, cs_c


def _decay(cs_r, cs_c):
    r, c = _tri()
    return jnp.exp(jnp.where(c <= r, cs_c - cs_r, -jnp.inf))


def _gdn_a(ks, betas, gs):
    r, c = _tri()
    cs = [_cumsum_rc(g) for g in gs]
    kk = [_dg(k, k, 1, 1) for k in ks]
    return [jnp.where(c < r, _col(b) * kki * _decay(*csi), 0.0) for b, kki, csi in zip(betas, kk, cs)]


def _neumann(a_list):
    r, c = _tri()
    xs = [jnp.where(r == c, 1.0, 0.0) - a for a in a_list]
    ps = list(a_list)
    n = 2
    while n < CH:
        ps = [_dgh(p, p, 1, 0) for p in ps]
        xs = [x + _dgh(x, p, 1, 0) for x, p in zip(xs, ps)]
        n *= 2
    return xs


def _gdn_rest(ss, qs, ks, vs, betas, gs, ts):
    n = range(len(ss))
    cs = [_cumsum_rc(g) for g in gs]
    dm = [_decay(*csi) for csi in cs]
    ecs = [jnp.exp(csi[1]) for csi in cs]
    bc = [_col(b) for b in betas]
    u = [_dgh(ts[i], vs[i] * bc[i], 1, 0) for i in n]
    w = [_dgh(ts[i], ks[i] * (bc[i] * ecs[i]), 1, 0) for i in n]
    ws = [_dg(w[i], ss[i], 1, 0) for i in n]
    v_new = [u[i] - ws[i] for i in n]
    qk = [_dg(qs[i], ks[i], 1, 1) * dm[i] for i in n]
    o_in = [_dg(qs[i] * ecs[i], ss[i], 1, 0) for i in n]
    o = [o_in[i] + _dg(qk[i], v_new[i], 1, 0) for i in n]
    g_last = [jnp.sum(g, axis=1, keepdims=True) for g in gs]
    s_new = [ss[i] * jnp.exp(g_last[i]) + _dg(ks[i] * jnp.exp(g_last[i] - cs[i][1]), v_new[i], 0, 0) for i in n]
    return s_new, o


def _gdn_fwd(q, k, v, beta, g, hb):
    t_rows, d = q.shape
    nc, ng, w = t_rows // CH, d // (HD * hb), HD * hb
    sls = [slice(h * HD, (h + 1) * HD) for h in range(hb)]

    def body(q_ref, k_ref, v_ref, b_ref, g_ref, o_ref, ss_ref, ts_ref, s_scr):
        c = pl.program_id(1)

        @pl.when(c == 0)
        def _():
            s_scr[...] = jnp.zeros_like(s_scr)

        qs, ks, vs = ([ref[:, sl] for sl in sls] for ref in (q_ref, k_ref, v_ref))
        br = [b_ref[0, 0, pl.ds(h, 1), :] for h in range(hb)]
        gr = [g_ref[0, 0, pl.ds(h, 1), :] for h in range(hb)]
        s0 = [s_scr[h] for h in range(hb)]
        tm = _neumann(_gdn_a(ks, br, gr))
        s1, o = _gdn_rest(s0, qs, ks, vs, br, gr, tm)
        for h in range(hb):
            ss_ref[0, 0, h] = s0[h]
            ts_ref[0, 0, h] = tm[h]
            o_ref[:, sls[h]] = o[h]
            s_scr[h] = s1[h]

    blk = pl.BlockSpec((CH, w), lambda n, c: (c, n))
    row = pl.BlockSpec((1, 1, hb, CH), lambda n, c: (n, c, 0, 0))
    return pl.pallas_call(
        body, name="gdn_fwd", grid=(ng, nc), in_specs=[blk, blk, blk, row, row],
        out_specs=[blk, pl.BlockSpec((1, 1, hb, HD, HD), lambda n, c: (n, c, 0, 0, 0)),
                   pl.BlockSpec((1, 1, hb, CH, CH), lambda n, c: (n, c, 0, 0, 0))],
        out_shape=[jax.ShapeDtypeStruct((t_rows, d), F32), jax.ShapeDtypeStruct((ng, nc, hb, HD, HD), F32),
                   jax.ShapeDtypeStruct((ng, nc, hb, CH, CH), F32)],
        scratch_shapes=[pltpu.VMEM((hb, HD, HD), F32)],
        compiler_params=_cparams(("parallel", "arbitrary"), 32),
    )(q, k, v, beta, g)


def _gdn_bwd(q, k, v, beta, g, ss, ts, do, hb):
    t_rows, d = q.shape
    nc, ng, w = t_rows // CH, d // (HD * hb), HD * hb
    sls = [slice(h * HD, (h + 1) * HD) for h in range(hb)]

    def body(q_ref, k_ref, v_ref, b_ref, g_ref, ss_ref, ts_ref, do_ref, dq_ref, dk_ref, dv_ref, db_ref, dg_ref, ds_scr):
        cr = pl.program_id(1)

        @pl.when(cr == 0)
        def _():
            ds_scr[...] = jnp.zeros_like(ds_scr)

        first = cr == nc - 1
        rowi = lax.broadcasted_iota(jnp.int32, (CH, 1), 0)
        lani = lax.broadcasted_iota(jnp.int32, (1, CH), 1)
        keep_c = jnp.logical_or(jnp.logical_not(first), rowi >= PADR)
        keep_r = jnp.logical_or(jnp.logical_not(first), lani >= PADR)
        hs = range(hb)
        qs, ks, vs, dos = ([ref[:, sl] for sl in sls] for ref in (q_ref, k_ref, v_ref, do_ref))
        br = [b_ref[0, 0, pl.ds(h, 1), :] for h in hs]
        gr = [g_ref[0, 0, pl.ds(h, 1), :] for h in hs]
        tm = [ts_ref[0, 0, h] for h in hs]
        _, vjp_rest = jax.vjp(_gdn_rest, [ss_ref[0, 0, h] for h in hs], qs, ks, vs, br, gr, tm)
        ds0, dq, dk, dv, db, dg, dt = vjp_rest(([ds_scr[h] for h in hs], dos))
        dtt = [_dgh(dt[h], tm[h], 1, 1) for h in hs]
        da = [-_dgh(tm[h], dtt[h], 0, 0) for h in hs]
        _, vjp_a = jax.vjp(_gdn_a, ks, br, gr)
        dk2, db2, dg2 = vjp_a(da)
        for h in hs:
            ds_scr[h] = ds0[h]
            dq_ref[:, sls[h]] = jnp.where(keep_c, dq[h], 0.0)
            dk_ref[:, sls[h]] = jnp.where(keep_c, dk[h] + dk2[h], 0.0)
            dv_ref[:, sls[h]] = jnp.where(keep_c, dv[h], 0.0)
            db_ref[0, 0, pl.ds(h, 1), :] = jnp.where(keep_r, db[h] + db2[h], 0.0)
            dg_ref[0, 0, pl.ds(h, 1), :] = jnp.where(keep_r, dg[h] + dg2[h], 0.0)

    blk = pl.BlockSpec((CH, w), lambda n, c: (nc - 1 - c, n))
    row = pl.BlockSpec((1, 1, hb, CH), lambda n, c: (n, nc - 1 - c, 0, 0))
    return pl.pallas_call(
        body, name="gdn_bwd", grid=(ng, nc),
        in_specs=[blk, blk, blk, row, row, pl.BlockSpec((1, 1, hb, HD, HD), lambda n, c: (n, nc - 1 - c, 0, 0, 0)),
                  pl.BlockSpec((1, 1, hb, CH, CH), lambda n, c: (n, nc - 1 - c, 0, 0, 0)), blk],
        out_specs=[blk, blk, blk, row, row],
        out_shape=[jax.ShapeDtypeStruct((t_rows, d), F32)] * 3 + [jax.ShapeDtypeStruct((ng, nc, hb, CH), F32)] * 2,
        scratch_shapes=[pltpu.VMEM((hb, HD, HD), F32)],
        compiler_params=_cparams(("parallel", "arbitrary"), 32),
    )(q, k, v, beta, g, ss, ts, do)


def _ssd_group(s, xs, bm, cm, dt_r, a_r):
    prs = range(len(s))
    first = lax.broadcasted_iota(jnp.int32, (1, 2 * M2P), 1) < M2P

    def pick(vals, p):
        return jnp.where(first, vals[2 * p], vals[2 * p + 1])

    cs = [_cumsum_rc(a) for a in a_r]
    lm = [_decay(*csi) for csi in cs]
    ecs = [jnp.exp(csi[1]) for csi in cs]
    alast = [jnp.sum(a, axis=1, keepdims=True) for a in a_r]
    ealast = [jnp.exp(al) for al in alast]
    wt = [jnp.exp(al - csi[1]) for al, csi in zip(alast, cs)]
    dtc = [_col(t) for t in dt_r]
    xdt = [xs[:, p * LANE:(p + 1) * LANE] * pick(dtc, p) for p in prs]
    cb = _dg(cm, bm, 1, 1)
    y0 = [_dg(cb * lm[2 * p], xdt[p], 1, 0) for p in prs]
    y1 = [_dg(cb * lm[2 * p + 1], xdt[p], 1, 0) for p in prs]
    yo = [_dg(cm, s[p], 1, 0) for p in prs]
    y = [jnp.where(first, y0[p], y1[p]) + yo[p] * pick(ecs, p) for p in prs]
    s_new = [s[p] * pick(ealast, p) + _dg(bm, xdt[p] * pick(wt, p), 0, 0) for p in prs]
    return s_new, jnp.concatenate(y, axis=1)


def _ssd_specs(nc, d, rev):
    hpg = (d // M2P) // M2G
    gw = hpg * M2P
    cc = (lambda c: nc - 1 - c) if rev else (lambda c: c)
    xs = pl.BlockSpec((CH, gw), lambda g, c: (cc(c), g))
    bm = pl.BlockSpec((CH, NST), lambda g, c: (cc(c), d // LANE + g))
    cm = pl.BlockSpec((CH, NST), lambda g, c: (cc(c), d // LANE + M2G + g))
    row = pl.BlockSpec((1, 1, hpg, CH), lambda g, c: (g, cc(c), 0, 0))
    st = pl.BlockSpec((1, 1, hpg // 2, NST, LANE), lambda g, c: (g, cc(c), 0, 0, 0))
    return xs, bm, cm, row, st, hpg


def _ssd_fwd(xbc, dt, a, d):
    t_rows = xbc.shape[0]
    nc = t_rows // CH
    xs, bm, cm, row, st, hpg = _ssd_specs(nc, d, False)
    ppg = hpg // 2

    def body(xs_ref, b_ref, c_ref, dt_ref, a_ref, y_ref, ss_ref, s_scr):
        c = pl.program_id(1)

        @pl.when(c == 0)
        def _():
            s_scr[...] = jnp.zeros_like(s_scr)

        s0 = [s_scr[p] for p in range(ppg)]
        for p in range(ppg):
            ss_ref[0, 0, p] = s0[p]
        dt_r = [dt_ref[0, 0, pl.ds(h, 1), :] for h in range(hpg)]
        a_r = [a_ref[0, 0, pl.ds(h, 1), :] for h in range(hpg)]
        s1, y = _ssd_group(s0, xs_ref[...], b_ref[...], c_ref[...], dt_r, a_r)
        y_ref[...] = y
        for p in range(ppg):
            s_scr[p] = s1[p]

    return pl.pallas_call(
        body, name="ssd_fwd", grid=(M2G, nc), in_specs=[xs, bm, cm, row, row], out_specs=[xs, st],
        out_shape=[jax.ShapeDtypeStruct((t_rows, d), F32), jax.ShapeDtypeStruct((M2G, nc, ppg, NST, LANE), F32)],
        scratch_shapes=[pltpu.VMEM((ppg, NST, LANE), F32)],
        compiler_params=_cparams(("parallel", "arbitrary"), 32),
    )(xbc, xbc, xbc, dt, a)


def _ssd_bwd(xbc, dt, a, ss, dy, d):
    t_rows = xbc.shape[0]
    nc = t_rows // CH
    xs, bm, cm, row, st, hpg = _ssd_specs(nc, d, True)
    ppg = hpg // 2

    def body(xs_ref, b_ref, c_ref, dt_ref, a_ref, ss_ref, dy_ref, dxs_ref, db_ref, dc_ref, ddt_ref, da_ref, ds_scr):
        cr = pl.program_id(1)

        @pl.when(cr == 0)
        def _():
            ds_scr[...] = jnp.zeros_like(ds_scr)

        first = cr == nc - 1
        keep_c = jnp.logical_or(jnp.logical_not(first), lax.broadcasted_iota(jnp.int32, (CH, 1), 0) >= PADR)
        keep_r = jnp.logical_or(jnp.logical_not(first), lax.broadcasted_iota(jnp.int32, (1, CH), 1) >= PADR)
        dt_r = [dt_ref[0, 0, pl.ds(h, 1), :] for h in range(hpg)]
        a_r = [a_ref[0, 0, pl.ds(h, 1), :] for h in range(hpg)]
        s0 = [ss_ref[0, 0, p] for p in range(ppg)]
        _, vjp = jax.vjp(_ssd_group, s0, xs_ref[...], b_ref[...], c_ref[...], dt_r, a_r)
        ds0, dxs, db, dc, ddt, da = vjp(([ds_scr[p] for p in range(ppg)], dy_ref[...]))
        for p in range(ppg):
            ds_scr[p] = ds0[p]
        dxs_ref[...] = jnp.where(keep_c, dxs, 0.0)
        db_ref[...] = jnp.where(keep_c, db, 0.0)
        dc_ref[...] = jnp.where(keep_c, dc, 0.0)
        for h in range(hpg):
            ddt_ref[0, 0, pl.ds(h, 1), :] = jnp.where(keep_r, ddt[h], 0.0)
            da_ref[0, 0, pl.ds(h, 1), :] = jnp.where(keep_r, da[h], 0.0)

    grp = pl.BlockSpec((CH, NST), lambda g, c: (nc - 1 - c, g))
    return pl.pallas_call(
        body, name="ssd_bwd", grid=(M2G, nc), in_specs=[xs, bm, cm, row, row, st, xs],
        out_specs=[xs, grp, grp, row, row],
        out_shape=[jax.ShapeDtypeStruct((t_rows, d), F32)] + [jax.ShapeDtypeStruct((t_rows, M2G * NST), F32)] * 2
        + [jax.ShapeDtypeStruct((M2G, nc, hpg, CH), F32)] * 2,
        scratch_shapes=[pltpu.VMEM((ppg, NST, LANE), F32)],
        compiler_params=_cparams(("parallel", "arbitrary"), 32),
    )(xbc, xbc, xbc, dt, a, ss, dy)


def _exchange(name, gathers, scatters):
    arrays = list(gathers) + list(scatters)
    n_g, n = len(gathers), len(arrays)

    def body(*refs):
        ins, outs = refs[:n], refs[n:2 * n]
        send_sems, recv_sems, local_sems = refs[2 * n:]
        x, y, c = lax.axis_index("x"), lax.axis_index("y"), lax.axis_index("c")
        me = 4 * x + 2 * y + c

        def src(a, slot):
            return ins[a] if a < n_g else ins[a].at[slot]

        local = [pltpu.make_async_copy(src(a, me), outs[a].at[me], local_sems.at[a]) for a in range(n)]
        for cp in local:
            cp.start()
        copies = []
        for rel in range(1, NDEV):
            px, py, pc = x ^ (rel >> 2), y ^ ((rel >> 1) & 1), c ^ (rel & 1)
            peer = 4 * px + 2 * py + pc
            for a in range(n):
                copies.append(pltpu.make_async_remote_copy(
                    src_ref=src(a, peer), dst_ref=outs[a].at[me], send_sem=send_sems.at[a, rel - 1],
                    recv_sem=recv_sems.at[a, rel - 1], device_id=(px, py, pc), device_id_type=pl.DeviceIdType.MESH))
        for cp in copies:
            cp.start()
        for cp in copies:
            cp.wait_recv()
        for cp in copies:
            cp.wait_send()
        for cp in local:
            cp.wait()

    any_spec = pl.BlockSpec(memory_space=pl.ANY)
    out_shape = [jax.ShapeDtypeStruct((NDEV,) + a.shape, a.dtype) for a in gathers]
    out_shape += [jax.ShapeDtypeStruct(a.shape, a.dtype) for a in scatters]
    return pl.pallas_call(
        body, name=name, in_specs=[any_spec] * n, out_specs=[any_spec] * n, out_shape=out_shape,
        scratch_shapes=[pltpu.SemaphoreType.DMA((n, NDEV - 1)), pltpu.SemaphoreType.DMA((n, NDEV - 1)),
                        pltpu.SemaphoreType.DMA((n,))],
        compiler_params=pltpu.CompilerParams(has_side_effects=True),
    )(*arrays)


_HBM = pl.BlockSpec(memory_space=pltpu.HBM)
_SEM = pl.BlockSpec(memory_space=pltpu.SEMAPHORE)
_EFFECT = pltpu.SideEffectType.DATAFLOW_SIDE_EFFECTING


def _split_copies(srcs, lands, send_sems, recv_sems, n_g):
    x, y, c = lax.axis_index("x"), lax.axis_index("y"), lax.axis_index("c")
    me = 4 * x + 2 * y + c
    copies = []
    for rel in range(1, NDEV):
        px, py, pc = x ^ (rel >> 2), y ^ ((rel >> 1) & 1), c ^ (rel & 1)
        peer = 4 * px + 2 * py + pc
        for a in range(len(srcs)):
            copies.append(pltpu.make_async_remote_copy(
                src_ref=srcs[a] if a < n_g else srcs[a].at[peer], dst_ref=lands[a].at[me],
                send_sem=send_sems.at[a * (NDEV - 1) + rel - 1], recv_sem=recv_sems.at[a * (NDEV - 1) + rel - 1],
                device_id=(px, py, pc), device_id_type=pl.DeviceIdType.MESH))
    return copies


def _exchange_start(name, gathers, scatters, after):
    arrays = list(gathers) + list(scatters)
    n_g, n = len(gathers), len(arrays)
    lands = [lax.empty((NDEV,) + a.shape, a.dtype) for a in gathers] + [lax.empty(a.shape, a.dtype) for a in scatters]

    def body(*refs):
        send_sems, recv_sems = refs[2 * n + 1], refs[2 * n + 2]
        for cp in _split_copies(refs[:n], refs[n:2 * n], send_sems, recv_sems, n_g):
            cp.start()
        refs[-1][...] = jnp.zeros_like(refs[-1])

    sems = pltpu.SemaphoreType.DMA((n * (NDEV - 1),))
    out = pl.pallas_call(
        body, name=name, in_specs=[_HBM] * (2 * n) + [pl.BlockSpec(memory_space=pl.ANY)],
        out_specs=(_SEM, _SEM, *[_HBM] * (2 * n), pl.BlockSpec(memory_space=pltpu.VMEM)),
        out_shape=(sems, sems, *[pltpu.HBM(a.shape, a.dtype) for a in arrays + lands], jax.ShapeDtypeStruct((8, LANE), F32)),
        input_output_aliases={i: 2 + i for i in range(2 * n)},
        compiler_params=pltpu.CompilerParams(has_side_effects=_EFFECT),
    )(*[pltpu.with_memory_space_constraint(a, pltpu.HBM) for a in arrays + lands], after)
    return out[0], out[1], list(out[2:2 + 2 * n]), out[-1], n_g


def _exchange_wait(name, started, after):
    send_sems, recv_sems, thru, _, n_g = started
    n = len(thru) // 2

    def body(*refs):
        for cp in _split_copies(refs[:n], refs[n:2 * n], refs[2 * n], refs[2 * n + 1], n_g):
            cp.wait_send()
            cp.wait_recv()

    out = pl.pallas_call(
        body, name=name, in_specs=[_HBM] * (2 * n) + [_SEM, _SEM, pl.BlockSpec(memory_space=pl.ANY)],
        out_specs=[_HBM] * (2 * n), out_shape=[pltpu.HBM(a.shape, a.dtype) for a in thru],
        input_output_aliases={i: i for i in range(2 * n)},
        compiler_params=pltpu.CompilerParams(has_side_effects=_EFFECT),
    )(*thru, send_sems, recv_sems, after)
    me = 4 * lax.axis_index("x") + 2 * lax.axis_index("y") + lax.axis_index("c")
    full = []
    for a in range(n):
        own = out[a][None] if a < n_g else lax.dynamic_index_in_dim(out[a], me, 0, keepdims=True)
        full.append(lax.dynamic_update_index_in_dim(out[n + a], own, me, 0))
    return full


def _adamw(name, staged, w, m, v):
    r, c = w.shape
    tr = _pick(r, 256, 8)

    def body(st_ref, w_ref, m_ref, v_ref, g_ref, d_ref, nm_ref, nv_ref):
        g = st_ref[0].astype(F32)
        for k in range(1, NDEV):
            g = g + st_ref[k].astype(F32)
        m_new = ADAM_B1 * m_ref[...] + (1.0 - ADAM_B1) * g
        v_new = ADAM_B2 * v_ref[...] + (1.0 - ADAM_B2) * jnp.square(g)
        m_hat = m_new / (1.0 - ADAM_B1 ** ADAM_STEP)
        v_hat = v_new / (1.0 - ADAM_B2 ** ADAM_STEP)
        g_ref[...] = g
        d_ref[...] = -ADAM_LR * (m_hat / (jnp.sqrt(v_hat) + ADAM_EPS) + ADAM_WD * w_ref[...])
        nm_ref[...] = m_new
        nv_ref[...] = v_new

    blk = pl.BlockSpec((tr, c), lambda i: (i, 0))
    return pl.pallas_call(
        body, name=name, grid=(r // tr,), in_specs=[pl.BlockSpec((NDEV, tr, c), lambda i: (0, i, 0)), blk, blk, blk],
        out_specs=[blk] * 4, out_shape=[jax.ShapeDtypeStruct((r, c), F32)] * 4,
        compiler_params=_cparams(("parallel",), 48),
    )(staged, w, m, v)


def _pack(parts):
    flat = jnp.concatenate([p.reshape(-1).astype(F32) for p in parts])
    pad = (-flat.shape[0]) % (8 * LANE)
    return jnp.pad(flat, (0, pad)).reshape(-1, LANE)


def _unpack(slab, shapes):
    flat, out, off = slab.reshape(-1), [], 0
    for s in shapes:
        n = 1
        for dim in s:
            n *= dim
        out.append(flat[off:off + n].reshape(s))
        off += n
    return out


def _to_shards(full, axis):
    shp = full.shape
    t = full.reshape(shp[:axis] + (NDEV, shp[axis] // NDEV) + shp[axis + 1:])
    return jnp.moveaxis(t, axis, 0)


def _from_shards(g, axis):
    t = jnp.moveaxis(g, 0, axis)
    shp = t.shape
    return t.reshape(shp[:axis] + (shp[axis] * shp[axis + 1],) + shp[axis + 2:])


def kernel(x, meta_tokens, norm_mix_w, w_in, dn_conv_w, dn_a_log, dn_dt_bias, dn_norm_w, m2_conv_w, m2_conv_b, m2_a_log, m2_dt_bias, m2_d, m2_norm_w, w_out, norm_ffn_w, ffn_up, ffn_conv_w, ffn_down, norm_final_w, loss_target, m_meta_tokens, m_norm_mix_w, m_w_in, m_dn_conv_w, m_dn_a_log, m_dn_dt_bias, m_dn_norm_w, m_m2_conv_w, m_m2_conv_b, m_m2_a_log, m_m2_dt_bias, m_m2_d, m_m2_norm_w, m_w_out, m_norm_ffn_w, m_ffn_up, m_ffn_conv_w, m_ffn_down, m_norm_final_w, v_meta_tokens, v_norm_mix_w, v_w_in, v_dn_conv_w, v_dn_a_log, v_dn_dt_bias, v_dn_norm_w, v_m2_conv_w, v_m2_conv_b, v_m2_a_log, v_m2_dt_bias, v_m2_d, v_m2_norm_w, v_w_out, v_norm_ffn_w, v_ffn_up, v_ffn_conv_w, v_ffn_down, v_norm_final_w):
    seq, d = x.shape[1], x.shape[2]
    t_rows = seq + CH
    nc = t_rows // CH
    dnh, m2h = d // HD, d // M2P
    dff = ffn_down.shape[1] * NDEV
    xbc_w = d + 2 * M2G * NST
    assert seq % CH == 0 and d % (2 * M2P * M2G) == 0 and 2 * dnh + m2h <= LANE
    hb = max(h for h in (8, 4, 2, 1) if dnh % h == 0)
    tm_rw = _pick(t_rows, 208, 16)

    small_sharded = [meta_tokens, dn_conv_w[0], m2_conv_w[0], ffn_conv_w[0]]
    small_shapes = [p.shape for p in small_sharded]
    g_win, g_small = _exchange("gather_w_in", [w_in[0].astype(WIRE), _pack(small_sharded)], [])
    rest = _exchange_start("gather_rest_start", [w_out[0].astype(WIRE), ffn_up[0].astype(WIRE), ffn_down[0].astype(WIRE)], [],
                           g_small)
    win = _from_shards(g_win, 1)
    small_full = [_unpack(g_small[k], small_shapes) for k in range(NDEV)]
    meta_f, dnconv_f, m2conv_f, ffnconv_f = [jnp.concatenate([small_full[k][i] for k in range(NDEV)], axis=-1) for i in range(4)]

    o_z, o_b, o_a = 3 * d, 4 * d, 4 * d + dnh
    o_m2z = 4 * d + 2 * dnh
    o_xbc, o_dt = o_m2z + d, o_m2z + d + xbc_w
    w_small = jnp.concatenate([win[:, o_b:o_m2z], win[:, o_dt:], jnp.zeros((d, LANE - 2 * dnh - m2h), WIRE)], axis=1)
    w_seg = {"q": win[:, :d], "k": win[:, d:2 * d], "v": win[:, 2 * d:3 * d], "z": win[:, o_z:o_b],
             "m2z": win[:, o_m2z:o_xbc], "xbc": win[:, o_xbc:o_dt], "sm": w_small}

    h0 = jnp.concatenate([jnp.zeros((PADR, d), F32), meta_f, x[0]], axis=0)
    valid = lambda rows: rows >= PADR

    def norm_fwd(name, h, w):
        return _rw(name, lambda rows, j, hv, wv: (_rms(hv, wv),), [("r", h, d, _c0), ("p", w, d, _c0)],
                   [("r", d, d, _c0, MXU)], t_rows, tm_rw)[0]

    hn1 = norm_fwd("norm_mix", h0, norm_mix_w)
    proj = {s: _mm("proj_" + s, hn1, w_seg[s], dep=rest[3]) for s in w_seg}

    def dn_post(sec, cv):
        s = _silu(cv)
        if sec < 2:
            s = s * lax.rsqrt(jnp.sum(s * s, axis=-1, keepdims=True) + EPS)
        if sec == 0:
            s = s * (HD ** -0.5)
        return s

    def dn_prep(sec, name):
        def fn(rows, j, p, w):
            return (jnp.where(valid(rows), dn_post(sec, _conv(p, w)), 0.0),)
        wc = dnconv_f[:, sec * d:(sec + 1) * d]
        return _rw("dn_prep_" + name, fn, [("r", proj[name], HD, _cj), ("p", wc, HD, _cj)], [("r", d, HD, _cj, F32)],
                   t_rows, t_rows, ncol=dnh)[0]

    q_act, k_act, v_act = dn_prep(0, "q"), dn_prep(1, "k"), dn_prep(2, "v")

    lane = lambda: lax.broadcasted_iota(jnp.int32, (1, LANE), 1)

    def lanes_of(vec, off):
        return jnp.pad(vec.astype(F32), ((0, 0), (off, LANE - off - vec.shape[1])))

    gate_params = [lanes_of(dn_a_log, dnh), lanes_of(dn_dt_bias, dnh), lanes_of(m2_a_log, 2 * dnh), lanes_of(m2_dt_bias, 2 * dnh)]

    def gates(rows, sm, p_alog, p_dtb, p_malog, p_mdtb):
        ln = lane()
        is_b, is_g = ln < dnh, jnp.logical_and(ln >= dnh, ln < 2 * dnh)
        is_d = jnp.logical_and(ln >= 2 * dnh, ln < 2 * dnh + m2h)
        beta = jax.nn.sigmoid(sm)
        gdec = -jnp.exp(p_alog) * _softplus(sm + p_dtb)
        dt = _softplus(sm + p_mdtb)
        am = dt * (-jnp.exp(p_malog))
        ok = valid(rows)
        g1 = jnp.where(ok, jnp.where(is_b, beta, jnp.where(is_g, gdec, jnp.where(is_d, dt, 0.0))), 0.0)
        g2 = jnp.where(jnp.logical_and(ok, is_d), am, 0.0)
        return g1, g2

    gate_ins = [("r", proj["sm"], LANE, _c0)] + [("p", p, LANE, _c0) for p in gate_params]
    g1, g2 = _rw("gates", lambda rows, j, *a: gates(rows, *a), gate_ins,
                 [("r", LANE, LANE, _c0, F32), ("r", LANE, LANE, _c0, F32)], t_rows, tm_rw)

    def head_rows(cols, per):
        n = cols.shape[1]
        return cols.reshape(nc, CH, n // per, per).transpose(2, 0, 3, 1)

    def head_cols(rows_):
        ngrp, _, per, _ = rows_.shape
        return rows_.transpose(1, 3, 0, 2).reshape(t_rows, ngrp * per)

    beta_r, gdec_r = head_rows(g1[:, :dnh], hb), head_rows(g1[:, dnh:2 * dnh], hb)
    hpg = m2h // M2G
    dt_r, am_r = head_rows(g1[:, 2 * dnh:2 * dnh + m2h], hpg), head_rows(g2[:, 2 * dnh:2 * dnh + m2h], hpg)

    o_dn, dn_states, dn_tinv = _gdn_fwd(q_act, k_act, v_act, beta_r, gdec_r, hb)

    def dn_out(o, z, w):
        outs = []
        for h in range(dnh):
            sl = slice(h * HD, (h + 1) * HD)
            outs.append(_rms(o[:, sl], w) * _silu(z[:, sl]))
        return jnp.concatenate(outs, axis=1)

    mixed_dn = _rw("dn_out", lambda rows, j, o, z, w: (dn_out(o, z, w),),
                   [("r", o_dn, d, _c0), ("r", proj["z"], d, _c0), ("p", dn_norm_w, HD, _c0)], [("r", d, d, _c0, MXU)],
                   t_rows, tm_rw)[0]

    def m2_prep(rows, j, p, w, b):
        return (jnp.where(valid(rows), _silu(_conv(p, w) + b), 0.0),)

    xbc_act = _rw("m2_prep", m2_prep, [("r", proj["xbc"], LANE, _cj), ("p", m2conv_f, LANE, _cj), ("p", m2_conv_b, LANE, _cj)],
                  [("r", xbc_w, LANE, _cj, F32)], t_rows, t_rows, ncol=xbc_w // LANE)[0]
    y_ssd, m2_states = _ssd_fwd(xbc_act, dt_r, am_r, d)

    d_lanes = jnp.repeat(m2_d.astype(F32), M2P, axis=1)
    gw = d // M2G

    def m2_out(ys, xs, z, dl, nw):
        yv = (ys + dl * xs) * _silu(z)
        outs = []
        for gi in range(M2G):
            sl = slice(gi * gw, (gi + 1) * gw)
            outs.append(_rms(yv[:, sl], nw[:, sl]))
        return jnp.concatenate(outs, axis=1)

    m2_out_ins = [("r", y_ssd, d, _c0), ("r", xbc_act, d, _c0), ("r", proj["m2z"], d, _c0), ("p", d_lanes, d, _c0),
                  ("p", m2_norm_w, d, _c0)]
    mixed_m2 = _rw("m2_out", lambda rows, j, *a: (m2_out(*a),), m2_out_ins, [("r", d, d, _c0, MXU)], t_rows, tm_rw)[0]

    mixed = jnp.concatenate([mixed_dn, mixed_m2], axis=1)
    g_wout, g_wup, g_wdown = _exchange_wait("gather_rest_wait", rest, mixed)
    wout = _from_shards(g_wout, 0)
    wup = _from_shards(g_wup, 1)
    wdown = _from_shards(g_wdown, 0)
    wup_g, wup_v = wup[:, :dff], wup[:, dff:]
    h1 = _mm("out_proj", mixed, wout, add=h0)
    hn2 = norm_fwd("norm_ffn", h1, norm_ffn_w)
    u_g, u_v = _mm("ffn_up_g", hn2, wup_g), _mm("ffn_up_v", hn2, wup_v)
    fc_g, fc_v = ffnconv_f[:, :dff], ffnconv_f[:, dff:]

    def ffn_act(rows, j, ug, uv, wg, wv):
        return (jnp.where(valid(rows), _silu(_conv(ug, wg)) * _conv(uv, wv), 0.0),)

    act = _rw("ffn_act", ffn_act, [("r", u_g, LANE, _cj), ("r", u_v, LANE, _cj), ("p", fc_g, LANE, _cj), ("p", fc_v, LANE, _cj)],
              [("r", dff, LANE, _cj, MXU)], t_rows, t_rows, ncol=dff // LANE)[0]
    h2 = _mm("ffn_down", act, wdown, add=h1, tk=1408)

    def loss_fn(hv, wf, tgt, rows):
        err = jnp.where(rows >= CH, _rms(hv, wf) - tgt, 0.0)
        return 0.5 * jnp.sum(jnp.mean(err * err, axis=-1, keepdims=True), axis=0, keepdims=True)

    def final(rows, j, hv, wf, tgt):
        loss, vjp = jax.vjp(lambda a, b: loss_fn(a, b, tgt, rows), hv, wf)
        dh, dw = vjp(jnp.ones((1, 1), F32))
        return dh, dh, dw, jnp.broadcast_to(loss, (1, LANE))

    wf2 = norm_final_w.reshape(1, d)
    dh2, dh2_m, d_wf, loss_part = _rw(
        "loss_head", final, [("r", h2, d, _c0), ("p", wf2, d, _c0), ("r", loss_target[0], d, _c0, lambda i: jnp.maximum(i - 1, 0))],
        [("r", d, d, _c0, F32), ("r", d, d, _c0, MXU), ("p", 1, d, d, _c0), ("p", 1, LANE, LANE, _c0)], t_rows, CH)
    loss = lax.psum(loss_part[0, 0], MESH_AXES)

    d_act = _mm("d_act", dh2_m, wdown, tb=True)
    gw_down = _mm("gw_down", act, dh2_m, ta=True, tm=1408, tn=1024, tk=1040)
    x_down = _exchange_start("grad_down_start", [], [_to_shards(gw_down, 0).astype(WIRE)], gw_down)

    def ffn_act_bwd(rows, j, ug, uv, wg, wv, da):
        cg, cv = _conv(ug, wg), _conv(uv, wv)
        _, vjp = jax.vjp(lambda a, b: _silu(a) * b, cg, cv)
        dcg, dcv = vjp(jnp.where(valid(rows), da, 0.0))
        return _conv_t(dcg, wg), _conv_t(dcv, wv), _conv_w(dcg, ug, len(wg)), _conv_w(dcv, uv, len(wv))

    kf = fc_g.shape[0]
    du_g, du_v, g_fc_g, g_fc_v = _rw(
        "ffn_act_bwd", ffn_act_bwd,
        [("r", u_g, LANE, _cj), ("r", u_v, LANE, _cj), ("p", fc_g, LANE, _cj), ("p", fc_v, LANE, _cj), ("r", d_act, LANE, _cj)],
        [("r", dff, LANE, _cj, MXU), ("r", dff, LANE, _cj, MXU), ("p", kf, dff, LANE, _cj), ("p", kf, dff, LANE, _cj)],
        t_rows, t_rows, ncol=dff // LANE)
    gw_up_g = _mm("gw_up_g", hn2, du_g, ta=True, tm=1024, tn=1408, tk=1040, dep=x_down[3])
    gw_up_v = _mm("gw_up_v", hn2, du_v, ta=True, tm=1024, tn=1408, tk=1040)
    gw_up_full = jnp.concatenate([gw_up_g, gw_up_v], axis=1)
    x_up = _exchange_start("grad_up_start", [], [_to_shards(gw_up_full, 1).astype(WIRE)], gw_up_full)
    d_hn2 = _mm("d_hn2_v", du_v, wup_v, tb=True, tk=1408, dep=x_up[3],
                add=_mm("d_hn2_g", du_g, wup_g, tb=True, tk=1408, dep=x_up[3]))

    def norm_bwd(name, h, w, dy, dres):
        def fn(rows, j, hv, wv, dyv, dr):
            _, vjp = jax.vjp(_rms, hv, wv)
            dh, dw = vjp(dyv)
            dh = dh + dr
            return dh, dh, dw
        return _rw(name, fn, [("r", h, d, _c0), ("p", w, d, _c0), ("r", dy, d, _c0), ("r", dres, d, _c0)],
                   [("r", d, d, _c0, F32), ("r", d, d, _c0, MXU), ("p", 1, d, d, _c0)], t_rows, tm_rw)

    dh1, dh1_m, g_norm_ffn = norm_bwd("norm_ffn_bwd", h1, norm_ffn_w, d_hn2, dh2)

    gw_out = _mm("gw_out", mixed, dh1_m, ta=True, tm=1024, tn=1024, tk=1040)
    x_out = _exchange_start("grad_out_start", [], [_to_shards(gw_out, 0).astype(WIRE)], gw_out)
    d_mixed = _mm("d_mixed", dh1_m, wout, tb=True, dep=x_out[3])

    def fold_heads(vec):
        r = lax.broadcasted_iota(jnp.int32, (d, LANE), 0)
        c = lax.broadcasted_iota(jnp.int32, (d, LANE), 1)
        return _dgh(vec, jnp.where(jnp.logical_and(r >= c * M2P, r < (c + 1) * M2P), 1.0, 0.0), 1, 0)

    def m2_out_bwd(rows, j, ys, xs, z, dl, nw, dy):
        _, vjp = jax.vjp(m2_out, ys, xs, z, dl, nw)
        dys, dxs, dz, ddl, dnw = vjp(dy)
        return dys, dxs, dz, fold_heads(ddl), dnw

    dy_ssd, dxs_skip, d_m2z, g_m2_d, g_m2_norm = _rw(
        "m2_out_bwd", m2_out_bwd, m2_out_ins + [("r", d_mixed, d, lambda j: 1)],
        [("r", d, d, _c0, F32), ("r", d, d, _c0, F32), ("r", d, d, _c0, MXU), ("p", 1, LANE, LANE, _c0), ("p", 1, d, d, _c0)],
        t_rows, _pick(t_rows, 208, 16))

    dxs, db_ssd, dc_ssd, ddt_r, dam_r = _ssd_bwd(xbc_act, dt_r, am_r, m2_states, dy_ssd, d)


    def m2_prep_bwd(n_d):
        def fn(rows, j, p, w, b, *ds):
            pre = _conv(p, w) + b
            _, vjp = jax.vjp(_silu, pre)
            dpre, = vjp(jnp.where(valid(rows), functools.reduce(lambda a_, b_: a_ + b_, ds), 0.0))
            return _conv_t(dpre, w), _conv_w(dpre, p, len(w)), jnp.sum(dpre, axis=0, keepdims=True)
        return fn

    def m2_prep_bwd_call(name, off, width, d_ins):
        blk0 = off // LANE
        ins = [("r", proj["xbc"], LANE, lambda j: blk0 + j), ("p", m2conv_f, LANE, lambda j: blk0 + j),
               ("p", m2_conv_b, LANE, lambda j: blk0 + j)] + d_ins
        km = m2conv_f.shape[0]
        return _rw(name, m2_prep_bwd(len(d_ins)), ins,
                   [("r", width, LANE, _cj, MXU), ("p", km, width, LANE, _cj), ("p", 1, width, LANE, _cj)],
                   t_rows, t_rows, ncol=width // LANE)

    dp_xs, gcw_xs, gcb_xs = m2_prep_bwd_call("m2_prep_bwd_x", 0, d, [("r", dxs, LANE, _cj), ("r", dxs_skip, LANE, _cj)])
    dp_b, gcw_b, gcb_b = m2_prep_bwd_call("m2_prep_bwd_b", d, M2G * NST, [("r", db_ssd, LANE, _cj)])
    dp_c, gcw_c, gcb_c = m2_prep_bwd_call("m2_prep_bwd_c", d + M2G * NST, M2G * NST, [("r", dc_ssd, LANE, _cj)])
    d_pxbc = jnp.concatenate([dp_xs, dp_b, dp_c], axis=1)
    g_m2_conv = jnp.concatenate([gcw_xs, gcw_b, gcw_c], axis=1)
    g_m2_conv_b = jnp.concatenate([gcb_xs, gcb_b, gcb_c], axis=1)

    def dn_out_bwd(rows, j, o, z, w, dy):
        _, vjp = jax.vjp(dn_out, o, z, w)
        return vjp(dy)

    d_o, d_z, g_dn_norm = _rw(
        "dn_out_bwd", dn_out_bwd,
        [("r", o_dn, d, _c0), ("r", proj["z"], d, _c0), ("p", dn_norm_w, HD, _c0), ("r", d_mixed, d, _c0)],
        [("r", d, d, _c0, F32), ("r", d, d, _c0, MXU), ("p", 1, HD, HD, _c0)], t_rows, _pick(t_rows, 208, 16))

    dq, dk, dv, dbeta_r, dgdec_r = _gdn_bwd(q_act, k_act, v_act, beta_r, gdec_r, dn_states, dn_tinv, d_o, hb)

    def dn_prep_bwd(sec, name, dact):
        def fn(rows, j, p, w, da):
            cv = _conv(p, w)
            _, vjp = jax.vjp(functools.partial(dn_post, sec), cv)
            dcv, = vjp(jnp.where(valid(rows), da, 0.0))
            return _conv_t(dcv, w), _conv_w(dcv, p, len(w))
        wc = dnconv_f[:, sec * d:(sec + 1) * d]
        return _rw("dn_prep_bwd_" + name, fn, [("r", proj[name], HD, _cj), ("p", wc, HD, _cj), ("r", dact, HD, _cj)],
                   [("r", d, HD, _cj, MXU), ("p", wc.shape[0], d, HD, _cj)], t_rows, t_rows, ncol=dnh)

    (dp_q, gcw_q), (dp_k, gcw_k), (dp_v, gcw_v) = dn_prep_bwd(0, "q", dq), dn_prep_bwd(1, "k", dk), dn_prep_bwd(2, "v", dv)
    g_dn_conv = jnp.concatenate([gcw_q, gcw_k, gcw_v], axis=1)

    zpad = jnp.zeros((t_rows, LANE - 2 * dnh - m2h), F32)
    dg1 = jnp.concatenate([head_cols(dbeta_r), head_cols(dgdec_r), head_cols(ddt_r), zpad], axis=1)
    dg2 = jnp.concatenate([jnp.zeros((t_rows, 2 * dnh), F32), head_cols(dam_r), zpad], axis=1)

    def gates_bwd(rows, j, sm, pa, pb, pc, pd, d1, d2):
        _, vjp = jax.vjp(lambda *a: gates(rows, *a), sm, pa, pb, pc, pd)
        return vjp((d1, d2))

    dp_sm, g_pa, g_pb, g_pc, g_pd = _rw(
        "gates_bwd", gates_bwd, gate_ins + [("r", dg1, LANE, _c0), ("r", dg2, LANE, _c0)],
        [("r", LANE, LANE, _c0, MXU)] + [("p", 1, LANE, LANE, _c0)] * 4, t_rows, tm_rw)

    dseg = {"q": dp_q, "k": dp_k, "v": dp_v, "z": d_z, "m2z": d_m2z, "xbc": d_pxbc, "sm": dp_sm}
    gw_seg = {s: _mm("gw_in_" + s, hn1, dseg[s], ta=True, tm=1024, tn=1024, tk=1040) for s in dseg}
    gsm = gw_seg["sm"]
    gw_in_full = jnp.concatenate([gw_seg["q"], gw_seg["k"], gw_seg["v"], gw_seg["z"], gsm[:, :2 * dnh], gw_seg["m2z"],
                                  gw_seg["xbc"], gsm[:, 2 * dnh:2 * dnh + m2h]], axis=1)
    x_in = _exchange_start("grad_in_start", [], [_to_shards(gw_in_full, 1).astype(WIRE)], gw_in_full)
    d_hn1 = None
    for s in dseg:
        d_hn1 = _mm("d_hn1_" + s, dseg[s], w_seg[s], tb=True, tk=2048, add=d_hn1, dep=x_in[3])
    dh0, _, g_norm_mix = norm_bwd("norm_mix_bwd", h0, norm_mix_w, d_hn1, dh1)

    g_ffn_conv = jnp.concatenate([g_fc_g, g_fc_v], axis=1)
    small_parts = [_to_shards(dh0[PADR:CH], 1), _to_shards(g_dn_conv, 1), _to_shards(g_m2_conv, 1), _to_shards(g_ffn_conv, 1)]
    small_scatter = jnp.stack([_pack([p[k] for p in small_parts]) for k in range(NDEV)])

    rep_names = ["norm_mix_w", "dn_a_log", "dn_dt_bias", "dn_norm_w", "m2_conv_b", "m2_a_log", "m2_dt_bias", "m2_d",
                 "m2_norm_w", "norm_ffn_w", "norm_final_w"]
    rep_grads = [g_norm_mix, g_pa[:, dnh:2 * dnh], g_pb[:, dnh:2 * dnh], g_dn_norm, g_m2_conv_b, g_pc[:, 2 * dnh:2 * dnh + m2h],
                 g_pd[:, 2 * dnh:2 * dnh + m2h], g_m2_d[:, :m2h], g_m2_norm, g_norm_ffn, d_wf.reshape(d)]
    st_rep, st_small = _exchange("exchange_small_grads", [_pack(rep_grads)], [small_scatter])
    st_wdown, = _exchange_wait("grad_down_wait", x_down, st_small)
    st_wup, = _exchange_wait("grad_up_wait", x_up, st_small)
    st_wout, = _exchange_wait("grad_out_wait", x_out, st_small)
    st_win, = _exchange_wait("grad_in_wait", x_in, st_small)

    weights = dict(meta_tokens=meta_tokens, norm_mix_w=norm_mix_w, w_in=w_in, dn_conv_w=dn_conv_w, dn_a_log=dn_a_log,
                   dn_dt_bias=dn_dt_bias, dn_norm_w=dn_norm_w, m2_conv_w=m2_conv_w, m2_conv_b=m2_conv_b, m2_a_log=m2_a_log,
                   m2_dt_bias=m2_dt_bias, m2_d=m2_d, m2_norm_w=m2_norm_w, w_out=w_out, norm_ffn_w=norm_ffn_w, ffn_up=ffn_up,
                   ffn_conv_w=ffn_conv_w, ffn_down=ffn_down, norm_final_w=norm_final_w)
    mom1 = dict(meta_tokens=m_meta_tokens, norm_mix_w=m_norm_mix_w, w_in=m_w_in, dn_conv_w=m_dn_conv_w, dn_a_log=m_dn_a_log,
                dn_dt_bias=m_dn_dt_bias, dn_norm_w=m_dn_norm_w, m2_conv_w=m_m2_conv_w, m2_conv_b=m_m2_conv_b,
                m2_a_log=m_m2_a_log, m2_dt_bias=m_m2_dt_bias, m2_d=m_m2_d, m2_norm_w=m_m2_norm_w, w_out=m_w_out,
                norm_ffn_w=m_norm_ffn_w, ffn_up=m_ffn_up, ffn_conv_w=m_ffn_conv_w, ffn_down=m_ffn_down,
                norm_final_w=m_norm_final_w)
    mom2 = dict(meta_tokens=v_meta_tokens, norm_mix_w=v_norm_mix_w, w_in=v_w_in, dn_conv_w=v_dn_conv_w, dn_a_log=v_dn_a_log,
                dn_dt_bias=v_dn_dt_bias, dn_norm_w=v_dn_norm_w, m2_conv_w=v_m2_conv_w, m2_conv_b=v_m2_conv_b,
                m2_a_log=v_m2_a_log, m2_dt_bias=v_m2_dt_bias, m2_d=v_m2_d, m2_norm_w=v_m2_norm_w, w_out=v_w_out,
                norm_ffn_w=v_norm_ffn_w, ffn_up=v_ffn_up, ffn_conv_w=v_ffn_conv_w, ffn_down=v_ffn_down,
                norm_final_w=v_norm_final_w)
    res = {}
    for name, st in (("w_in", st_win), ("w_out", st_wout), ("ffn_up", st_wup), ("ffn_down", st_wdown)):
        outs = _adamw("adamw_" + name, st, weights[name][0], mom1[name][0], mom2[name][0])
        res[name] = tuple(o[None] for o in outs)

    def adam_packed(label, staged, names):
        shapes = [weights[nm].shape for nm in names]
        outs = _adamw(label, staged, *[_pack([src[nm] for nm in names]) for src in (weights, mom1, mom2)])
        unpacked = [_unpack(o, shapes) for o in outs]
        for i, nm in enumerate(names):
            res[nm] = tuple(u[i] for u in unpacked)

    adam_packed("adamw_small_sharded", st_small, ["meta_tokens", "dn_conv_w", "m2_conv_w", "ffn_conv_w"])
    adam_packed("adamw_replicated", st_rep, rep_names)

    order = list(weights)
    grad_x = dh0[CH:][None]
    return (loss, grad_x, *[res[nm][0] for nm in order], *[res[nm][1] for nm in order], *[res[nm][2] for nm in order],
            *[res[nm][3] for nm in order])
```

```python
import functools

import jax
import jax.numpy as jnp
from jax import lax
from jax.experimental import pallas as pl
from jax.experimental.pallas import tpu as pltpu

F32 = jnp.float32
MXU = jnp.bfloat16
WIRE = jnp.bfloat16
HI = lax.Precision.HIGH

NDEV = 8
CH = 64
NMETA = 16
PADR = CH - NMETA
EPS = 1e-6
HD = 128
M2P = 64
M2G = 4
NST = 128
LANE = 128

ADAM_LR, ADAM_B1, ADAM_B2, ADAM_EPS, ADAM_WD, ADAM_STEP = 0.001, 0.9, 0.999, 1e-08, 0.01, 10

MESH_AXES = ("x", "y", "c")


def _pick(n, target, mult=16):
    best = None
    for t in range(mult, min(n, target) + 1, mult):
        if n % t == 0:
            best = t
    return best if best is not None else n


def _dg(a, b, ca, cb):
    return lax.dot_general(a.astype(MXU), b.astype(MXU), (((ca,), (cb,)), ((), ())), preferred_element_type=F32)


def _dgh(a, b, ca, cb):
    return lax.dot_general(a, b, (((ca,), (cb,)), ((), ())), precision=HI, preferred_element_type=F32)


def _silu(x):
    return x * jax.nn.sigmoid(x)


def _softplus(x):
    return jnp.maximum(x, 0.0) + jnp.log1p(jnp.exp(-jnp.abs(x)))


def _rms(x, w):
    return x * lax.rsqrt(jnp.mean(x * x, axis=-1, keepdims=True) + EPS) * w


def _cparams(sem, vmem_mb):
    return pltpu.CompilerParams(dimension_semantics=sem, vmem_limit_bytes=vmem_mb << 20)


def _mm(name, a, b, *, ta=False, tb=False, add=None, out_dtype=F32, tm=1024, tn=512, tk=2048, dep=None):
    m, kdim = (a.shape[1], a.shape[0]) if ta else a.shape
    n = b.shape[0] if tb else b.shape[1]
    tm = _pick(m, tm, 128 if ta else 16)
    tn = _pick(n, tn, 128)
    tk = _pick(kdim, tk, 16 if (ta and not tb) else 128)
    nk = kdim // tk
    ca, cb = (0 if ta else 1), (1 if tb else 0)

    def body(*refs):
        a_ref, b_ref = refs[0], refs[1]
        add_ref = refs[2] if add is not None else None
        o_ref, acc = refs[-2], refs[-1]
        k = pl.program_id(2)

        @pl.when(k == 0)
        def _():
            acc[...] = jnp.zeros_like(acc)

        acc[...] += _dg(a_ref[...], b_ref[...], ca, cb)

        @pl.when(k == nk - 1)
        def _():
            r = acc[...]
            if add_ref is not None:
                r = r + add_ref[...].astype(F32)
            o_ref[...] = r.astype(o_ref.dtype)

    a_spec = pl.BlockSpec((tk, tm), lambda i, j, k: (k, i)) if ta else pl.BlockSpec((tm, tk), lambda i, j, k: (i, k))
    b_spec = pl.BlockSpec((tn, tk), lambda i, j, k: (j, k)) if tb else pl.BlockSpec((tk, tn), lambda i, j, k: (k, j))
    in_specs, ops = [a_spec, b_spec], [a, b]
    if add is not None:
        in_specs.append(pl.BlockSpec((tm, tn), lambda i, j, k: (i, j)))
        ops.append(add)
    if dep is not None:
        in_specs.append(pl.BlockSpec((8, LANE), lambda i, j, k: (0, 0)))
        ops.append(dep)
    return pl.pallas_call(
        body, name=name, grid=(m // tm, n // tn, nk), in_specs=in_specs,
        out_specs=pl.BlockSpec((tm, tn), lambda i, j, k: (i, j)),
        out_shape=jax.ShapeDtypeStruct((m, n), out_dtype),
        scratch_shapes=[pltpu.VMEM((tm, tn), F32)],
        compiler_params=_cparams(("parallel", "parallel", "arbitrary"), 48),
    )(*ops)


def _rw(name, fn, ins, outs, nrows, tm, ncol=1, vmem_mb=48):
    nrow = nrows // tm
    in_specs, ops = [], []
    for spec in ins:
        kind, arr, bw, cj = spec[:4]
        ops.append(arr)
        if kind == "r":
            ri = spec[4] if len(spec) > 4 else (lambda i: i)
            in_specs.append(pl.BlockSpec((tm, bw), lambda j, i, cj=cj, ri=ri: (ri(i), cj(j))))
        else:
            in_specs.append(pl.BlockSpec((arr.shape[0], bw), lambda j, i, cj=cj: (0, cj(j))))
    out_shape, out_specs = [], []
    for o in outs:
        if o[0] == "r":
            _, width, bw, cj, dt = o
            out_shape.append(jax.ShapeDtypeStruct((nrows, width), dt))
            out_specs.append(pl.BlockSpec((tm, bw), lambda j, i, cj=cj: (i, cj(j))))
        else:
            _, rows, width, bw, cj = o
            out_shape.append(jax.ShapeDtypeStruct((rows, width), F32))
            out_specs.append(pl.BlockSpec((rows, bw), lambda j, i, cj=cj: (0, cj(j))))
    n_in = len(ins)

    def body(*refs):
        j, i = pl.program_id(0), pl.program_id(1)
        rows = i * tm + lax.broadcasted_iota(jnp.int32, (tm, 1), 0)
        vals = []
        for spec, ref in zip(ins, refs[:n_in]):
            if spec[0] == "r" or spec[1].shape[0] == 1:
                vals.append(ref[...])
            else:
                vals.append([ref[pl.ds(r, 1), :] for r in range(spec[1].shape[0])])
        res = fn(rows, j, *vals)
        for o, val, ref in zip(outs, res, refs[n_in:]):
            if o[0] == "r":
                ref[...] = val.astype(ref.dtype)
            else:
                @pl.when(i == 0)
                def _(ref=ref):
                    ref[...] = jnp.zeros_like(ref)

                if o[1] == 1:
                    ref[...] += val
                else:
                    for r in range(o[1]):
                        ref[pl.ds(r, 1), :] += val[r]

    return pl.pallas_call(
        body, name=name, grid=(ncol, nrow), in_specs=in_specs, out_specs=out_specs, out_shape=out_shape,
        compiler_params=_cparams(("parallel", "arbitrary"), vmem_mb),
    )(*ops)


def _c0(j):
    return 0


def _cj(j):
    return j


def _shift(x, s):
    if s == 0:
        return x
    return pltpu.roll(x, s % x.shape[0], 0)


def _conv(x, w):
    k = len(w)
    return functools.reduce(lambda a, b: a + b, [w[j] * _shift(x, k - 1 - j) for j in range(k)])


def _conv_t(dy, w):
    k = len(w)
    return functools.reduce(lambda a, b: a + b, [w[j] * _shift(dy, -(k - 1 - j)) for j in range(k)])


def _conv_w(dy, x, k):
    return [jnp.sum(dy * _shift(x, k - 1 - j), axis=0, keepdims=True) for j in range(k)]


def _tri():
    r = lax.broadcasted_iota(jnp.int32, (CH, CH), 0)
    c = lax.broadcasted_iota(jnp.int32, (CH, CH), 1)
    return r, c


def _col(row):
    r, c = _tri()
    return jnp.sum(jnp.where(r == c, row, 0.0), axis=1, keepdims=True)


def _cumsum_rc(g_r):
    r, c = _tri()
    g_c = _col(g_r)
    cs_r = jnp.sum(jnp.where(r <= c, g_c, 0.0), axis=0, keepdims=True)
    cs_c = jnp.sum(jnp.where(c <= r, g_r, 0.0), axis=1, keepdims=True)
    return cs_r, cs_c


def _decay(cs_r, cs_c):
    r, c = _tri()
    return jnp.exp(jnp.where(c <= r, cs_c - cs_r, -jnp.inf))


def _gdn_a(ks, betas, gs):
    r, c = _tri()
    cs = [_cumsum_rc(g) for g in gs]
    kk = [_dg(k, k, 1, 1) for k in ks]
    return [jnp.where(c < r, _col(b) * kki * _decay(*csi), 0.0) for b, kki, csi in zip(betas, kk, cs)]


def _neumann(a_list):
    r, c = _tri()
    xs = [jnp.where(r == c, 1.0, 0.0) - a for a in a_list]
    ps = list(a_list)
    n = 2
    while n < CH:
        ps = [_dgh(p, p, 1, 0) for p in ps]
        xs = [x + _dgh(x, p, 1, 0) for x, p in zip(xs, ps)]
        n *= 2
    return xs


def _gdn_rest(ss, qs, ks, vs, betas, gs, ts):
    n = range(len(ss))
    cs = [_cumsum_rc(g) for g in gs]
    dm = [_decay(*csi) for csi in cs]
    ecs = [jnp.exp(csi[1]) for csi in cs]
    bc = [_col(b) for b in betas]
    u = [_dgh(ts[i], vs[i] * bc[i], 1, 0) for i in n]
    w = [_dgh(ts[i], ks[i] * (bc[i] * ecs[i]), 1, 0) for i in n]
    ws = [_dg(w[i], ss[i], 1, 0) for i in n]
    v_new = [u[i] - ws[i] for i in n]
    qk = [_dg(qs[i], ks[i], 1, 1) * dm[i] for i in n]
    o_in = [_dg(qs[i] * ecs[i], ss[i], 1, 0) for i in n]
    o = [o_in[i] + _dg(qk[i], v_new[i], 1, 0) for i in n]
    g_last = [jnp.sum(g, axis=1, keepdims=True) for g in gs]
    s_new = [ss[i] * jnp.exp(g_last[i]) + _dg(ks[i] * jnp.exp(g_last[i] - cs[i][1]), v_new[i], 0, 0) for i in n]
    return s_new, o


def _gdn_fwd(q, k, v, beta, g, hb):
    t_rows, d = q.shape
    nc, ng, w = t_rows // CH, d // (HD * hb), HD * hb
    sls = [slice(h * HD, (h + 1) * HD) for h in range(hb)]

    def body(q_ref, k_ref, v_ref, b_ref, g_ref, o_ref, ss_ref, ts_ref, s_scr):
        c = pl.program_id(1)

        @pl.when(c == 0)
        def _():
            s_scr[...] = jnp.zeros_like(s_scr)

        qs, ks, vs = ([ref[:, sl] for sl in sls] for ref in (q_ref, k_ref, v_ref))
        br = [b_ref[0, 0, pl.ds(h, 1), :] for h in range(hb)]
        gr = [g_ref[0, 0, pl.ds(h, 1), :] for h in range(hb)]
        s0 = [s_scr[h] for h in range(hb)]
        tm = _neumann(_gdn_a(ks, br, gr))
        s1, o = _gdn_rest(s0, qs, ks, vs, br, gr, tm)
        for h in range(hb):
            ss_ref[0, 0, h] = s0[h]
            ts_ref[0, 0, h] = tm[h]
            o_ref[:, sls[h]] = o[h]
            s_scr[h] = s1[h]

    blk = pl.BlockSpec((CH, w), lambda n, c: (c, n))
    row = pl.BlockSpec((1, 1, hb, CH), lambda n, c: (n, c, 0, 0))
    return pl.pallas_call(
        body, name="gdn_fwd", grid=(ng, nc), in_specs=[blk, blk, blk, row, row],
        out_specs=[blk, pl.BlockSpec((1, 1, hb, HD, HD), lambda n, c: (n, c, 0, 0, 0)),
                   pl.BlockSpec((1, 1, hb, CH, CH), lambda n, c: (n, c, 0, 0, 0))],
        out_shape=[jax.ShapeDtypeStruct((t_rows, d), F32), jax.ShapeDtypeStruct((ng, nc, hb, HD, HD), F32),
                   jax.ShapeDtypeStruct((ng, nc, hb, CH, CH), F32)],
        scratch_shapes=[pltpu.VMEM((hb, HD, HD), F32)],
        compiler_params=_cparams(("parallel", "arbitrary"), 32),
    )(q, k, v, beta, g)


def _gdn_bwd(q, k, v, beta, g, ss, ts, do, hb):
    t_rows, d = q.shape
    nc, ng, w = t_rows // CH, d // (HD * hb), HD * hb
    sls = [slice(h * HD, (h + 1) * HD) for h in range(hb)]

    def body(q_ref, k_ref, v_ref, b_ref, g_ref, ss_ref, ts_ref, do_ref, dq_ref, dk_ref, dv_ref, db_ref, dg_ref, ds_scr):
        cr = pl.program_id(1)

        @pl.when(cr == 0)
        def _():
            ds_scr[...] = jnp.zeros_like(ds_scr)

        first = cr == nc - 1
        rowi = lax.broadcasted_iota(jnp.int32, (CH, 1), 0)
        lani = lax.broadcasted_iota(jnp.int32, (1, CH), 1)
        keep_c = jnp.logical_or(jnp.logical_not(first), rowi >= PADR)
        keep_r = jnp.logical_or(jnp.logical_not(first), lani >= PADR)
        hs = range(hb)
        qs, ks, vs, dos = ([ref[:, sl] for sl in sls] for ref in (q_ref, k_ref, v_ref, do_ref))
        br = [b_ref[0, 0, pl.ds(h, 1), :] for h in hs]
        gr = [g_ref[0, 0, pl.ds(h, 1), :] for h in hs]
        tm = [ts_ref[0, 0, h] for h in hs]
        _, vjp_rest = jax.vjp(_gdn_rest, [ss_ref[0, 0, h] for h in hs], qs, ks, vs, br, gr, tm)
        ds0, dq, dk, dv, db, dg, dt = vjp_rest(([ds_scr[h] for h in hs], dos))
        dtt = [_dgh(dt[h], tm[h], 1, 1) for h in hs]
        da = [-_dgh(tm[h], dtt[h], 0, 0) for h in hs]
        _, vjp_a = jax.vjp(_gdn_a, ks, br, gr)
        dk2, db2, dg2 = vjp_a(da)
        for h in hs:
            ds_scr[h] = ds0[h]
            dq_ref[:, sls[h]] = jnp.where(keep_c, dq[h], 0.0)
            dk_ref[:, sls[h]] = jnp.where(keep_c, dk[h] + dk2[h], 0.0)
            dv_ref[:, sls[h]] = jnp.where(keep_c, dv[h], 0.0)
            db_ref[0, 0, pl.ds(h, 1), :] = jnp.where(keep_r, db[h] + db2[h], 0.0)
            dg_ref[0, 0, pl.ds(h, 1), :] = jnp.where(keep_r, dg[h] + dg2[h], 0.0)

    blk = pl.BlockSpec((CH, w), lambda n, c: (nc - 1 - c, n))
    row = pl.BlockSpec((1, 1, hb, CH), lambda n, c: (n, nc - 1 - c, 0, 0))
    return pl.pallas_call(
        body, name="gdn_bwd", grid=(ng, nc),
        in_specs=[blk, blk, blk, row, row, pl.BlockSpec((1, 1, hb, HD, HD), lambda n, c: (n, nc - 1 - c, 0, 0, 0)),
                  pl.BlockSpec((1, 1, hb, CH, CH), lambda n, c: (n, nc - 1 - c, 0, 0, 0)), blk],
        out_specs=[blk, blk, blk, row, row],
        out_shape=[jax.ShapeDtypeStruct((t_rows, d), F32)] * 3 + [jax.ShapeDtypeStruct((ng, nc, hb, CH), F32)] * 2,
        scratch_shapes=[pltpu.VMEM((hb, HD, HD), F32)],
        compiler_params=_cparams(("parallel", "arbitrary"), 32),
    )(q, k, v, beta, g, ss, ts, do)


def _ssd_group(s, xs, bm, cm, dt_r, a_r):
    prs = range(len(s))
    first = lax.broadcasted_iota(jnp.int32, (1, 2 * M2P), 1) < M2P

    def pick(vals, p):
        return jnp.where(first, vals[2 * p], vals[2 * p + 1])

    cs = [_cumsum_rc(a) for a in a_r]
    lm = [_decay(*csi) for csi in cs]
    ecs = [jnp.exp(csi[1]) for csi in cs]
    alast = [jnp.sum(a, axis=1, keepdims=True) for a in a_r]
    ealast = [jnp.exp(al) for al in alast]
    wt = [jnp.exp(al - csi[1]) for al, csi in zip(alast, cs)]
    dtc = [_col(t) for t in dt_r]
    xdt = [xs[:, p * LANE:(p + 1) * LANE] * pick(dtc, p) for p in prs]
    cb = _dg(cm, bm, 1, 1)
    y0 = [_dg(cb * lm[2 * p], xdt[p], 1, 0) for p in prs]
    y1 = [_dg(cb * lm[2 * p + 1], xdt[p], 1, 0) for p in prs]
    yo = [_dg(cm, s[p], 1, 0) for p in prs]
    y = [jnp.where(first, y0[p], y1[p]) + yo[p] * pick(ecs, p) for p in prs]
    s_new = [s[p] * pick(ealast, p) + _dg(bm, xdt[p] * pick(wt, p), 0, 0) for p in prs]
    return s_new, jnp.concatenate(y, axis=1)


def _ssd_specs(nc, d, rev):
    hpg = (d // M2P) // M2G
    gw = hpg * M2P
    cc = (lambda c: nc - 1 - c) if rev else (lambda c: c)
    xs = pl.BlockSpec((CH, gw), lambda g, c: (cc(c), g))
    bm = pl.BlockSpec((CH, NST), lambda g, c: (cc(c), d // LANE + g))
    cm = pl.BlockSpec((CH, NST), lambda g, c: (cc(c), d // LANE + M2G + g))
    row = pl.BlockSpec((1, 1, hpg, CH), lambda g, c: (g, cc(c), 0, 0))
    st = pl.BlockSpec((1, 1, hpg // 2, NST, LANE), lambda g, c: (g, cc(c), 0, 0, 0))
    return xs, bm, cm, row, st, hpg


def _ssd_fwd(xbc, dt, a, d):
    t_rows = xbc.shape[0]
    nc = t_rows // CH
    xs, bm, cm, row, st, hpg = _ssd_specs(nc, d, False)
    ppg = hpg // 2

    def body(xs_ref, b_ref, c_ref, dt_ref, a_ref, y_ref, ss_ref, s_scr):
        c = pl.program_id(1)

        @pl.when(c == 0)
        def _():
            s_scr[...] = jnp.zeros_like(s_scr)

        s0 = [s_scr[p] for p in range(ppg)]
        for p in range(ppg):
            ss_ref[0, 0, p] = s0[p]
        dt_r = [dt_ref[0, 0, pl.ds(h, 1), :] for h in range(hpg)]
        a_r = [a_ref[0, 0, pl.ds(h, 1), :] for h in range(hpg)]
        s1, y = _ssd_group(s0, xs_ref[...], b_ref[...], c_ref[...], dt_r, a_r)
        y_ref[...] = y
        for p in range(ppg):
            s_scr[p] = s1[p]

    return pl.pallas_call(
        body, name="ssd_fwd", grid=(M2G, nc), in_specs=[xs, bm, cm, row, row], out_specs=[xs, st],
        out_shape=[jax.ShapeDtypeStruct((t_rows, d), F32), jax.ShapeDtypeStruct((M2G, nc, ppg, NST, LANE), F32)],
        scratch_shapes=[pltpu.VMEM((ppg, NST, LANE), F32)],
        compiler_params=_cparams(("parallel", "arbitrary"), 32),
    )(xbc, xbc, xbc, dt, a)


def _ssd_bwd(xbc, dt, a, ss, dy, d):
    t_rows = xbc.shape[0]
    nc = t_rows // CH
    xs, bm, cm, row, st, hpg = _ssd_specs(nc, d, True)
    ppg = hpg // 2

    def body(xs_ref, b_ref, c_ref, dt_ref, a_ref, ss_ref, dy_ref, dxs_ref, db_ref, dc_ref, ddt_ref, da_ref, ds_scr):
        cr = pl.program_id(1)

        @pl.when(cr == 0)
        def _():
            ds_scr[...] = jnp.zeros_like(ds_scr)

        first = cr == nc - 1
        keep_c = jnp.logical_or(jnp.logical_not(first), lax.broadcasted_iota(jnp.int32, (CH, 1), 0) >= PADR)
        keep_r = jnp.logical_or(jnp.logical_not(first), lax.broadcasted_iota(jnp.int32, (1, CH), 1) >= PADR)
        dt_r = [dt_ref[0, 0, pl.ds(h, 1), :] for h in range(hpg)]
        a_r = [a_ref[0, 0, pl.ds(h, 1), :] for h in range(hpg)]
        s0 = [ss_ref[0, 0, p] for p in range(ppg)]
        _, vjp = jax.vjp(_ssd_group, s0, xs_ref[...], b_ref[...], c_ref[...], dt_r, a_r)
        ds0, dxs, db, dc, ddt, da = vjp(([ds_scr[p] for p in range(ppg)], dy_ref[...]))
        for p in range(ppg):
            ds_scr[p] = ds0[p]
        dxs_ref[...] = jnp.where(keep_c, dxs, 0.0)
        db_ref[...] = jnp.where(keep_c, db, 0.0)
        dc_ref[...] = jnp.where(keep_c, dc, 0.0)
        for h in range(hpg):
            ddt_ref[0, 0, pl.ds(h, 1), :] = jnp.where(keep_r, ddt[h], 0.0)
            da_ref[0, 0, pl.ds(h, 1), :] = jnp.where(keep_r, da[h], 0.0)

    grp = pl.BlockSpec((CH, NST), lambda g, c: (nc - 1 - c, g))
    return pl.pallas_call(
        body, name="ssd_bwd", grid=(M2G, nc), in_specs=[xs, bm, cm, row, row, st, xs],
        out_specs=[xs, grp, grp, row, row],
        out_shape=[jax.ShapeDtypeStruct((t_rows, d), F32)] + [jax.ShapeDtypeStruct((t_rows, M2G * NST), F32)] * 2
        + [jax.ShapeDtypeStruct((M2G, nc, hpg, CH), F32)] * 2,
        scratch_shapes=[pltpu.VMEM((ppg, NST, LANE), F32)],
        compiler_params=_cparams(("parallel", "arbitrary"), 32),
    )(xbc, xbc, xbc, dt, a, ss, dy)


def _exchange(name, gathers, scatters):
    arrays = list(gathers) + list(scatters)
    n_g, n = len(gathers), len(arrays)

    def body(*refs):
        ins, outs = refs[:n], refs[n:2 * n]
        send_sems, recv_sems, local_sems = refs[2 * n:]
        x, y, c = lax.axis_index("x"), lax.axis_index("y"), lax.axis_index("c")
        me = 4 * x + 2 * y + c

        def src(a, slot):
            return ins[a] if a < n_g else ins[a].at[slot]

        local = [pltpu.make_async_copy(src(a, me), outs[a].at[me], local_sems.at[a]) for a in range(n)]
        for cp in local:
            cp.start()
        copies = []
        for rel in range(1, NDEV):
            px, py, pc = x ^ (rel >> 2), y ^ ((rel >> 1) & 1), c ^ (rel & 1)
            peer = 4 * px + 2 * py + pc
            for a in range(n):
                copies.append(pltpu.make_async_remote_copy(
                    src_ref=src(a, peer), dst_ref=outs[a].at[me], send_sem=send_sems.at[a, rel - 1],
                    recv_sem=recv_sems.at[a, rel - 1], device_id=(px, py, pc), device_id_type=pl.DeviceIdType.MESH))
        for cp in copies:
            cp.start()
        for cp in copies:
            cp.wait_recv()
        for cp in copies:
            cp.wait_send()
        for cp in local:
            cp.wait()

    any_spec = pl.BlockSpec(memory_space=pl.ANY)
    out_shape = [jax.ShapeDtypeStruct((NDEV,) + a.shape, a.dtype) for a in gathers]
    out_shape += [jax.ShapeDtypeStruct(a.shape, a.dtype) for a in scatters]
    return pl.pallas_call(
        body, name=name, in_specs=[any_spec] * n, out_specs=[any_spec] * n, out_shape=out_shape,
        scratch_shapes=[pltpu.SemaphoreType.DMA((n, NDEV - 1)), pltpu.SemaphoreType.DMA((n, NDEV - 1)),
                        pltpu.SemaphoreType.DMA((n,))],
        compiler_params=pltpu.CompilerParams(has_side_effects=True),
    )(*arrays)


def _gather_two_level(name, arrays):
    n = len(arrays)

    def body(*refs):
        ins, outs = refs[:n], refs[n:2 * n]
        send_sems, recv_sems, local_sems = refs[2 * n:]
        x, y, c = lax.axis_index("x"), lax.axis_index("y"), lax.axis_index("c")
        me, sibling = (x, y, c), (x, y, 1 - c)
        chips = [(1 - x, y), (x, 1 - y), (1 - x, 1 - y)]

        def copy(a, k, block, to, src=None):
            dst = outs[a].at[4 * block[0] + 2 * block[1] + block[2]]
            return pltpu.make_async_remote_copy(
                src_ref=dst if src is None else src, dst_ref=dst, send_sem=send_sems.at[a, k], recv_sem=recv_sems.at[a, k],
                device_id=to, device_id_type=pl.DeviceIdType.MESH)

        mine = [pltpu.make_async_copy(ins[a], outs[a].at[4 * x + 2 * y + c], local_sems.at[a]) for a in range(n)]
        for cp in mine:
            cp.start()
        first = []
        for a in range(n):
            first.append(copy(a, 0, me, sibling, src=ins[a]))
            first += [copy(a, 1 + j, me, (*chip, c), src=ins[a]) for j, chip in enumerate(chips)]
        for cp in first:
            cp.start()
        passed = [[copy(a, 4 + j, (*chip, c), sibling) for j, chip in enumerate(chips)] for a in range(n)]
        for j, chip in enumerate(chips):
            for a in range(n):
                copy(a, 1 + j, (*chip, c), me).wait_recv()
                passed[a][j].start()
        for a in range(n):
            copy(a, 0, sibling, me).wait_recv()
            for j, chip in enumerate(chips):
                copy(a, 4 + j, (*chip, 1 - c), me).wait_recv()
        for cp in first + [cp for row in passed for cp in row]:
            cp.wait_send()
        for cp in mine:
            cp.wait()

    any_spec = pl.BlockSpec(memory_space=pl.ANY)
    return pl.pallas_call(
        body, name=name, in_specs=[any_spec] * n, out_specs=[any_spec] * n,
        out_shape=[jax.ShapeDtypeStruct((NDEV,) + a.shape, a.dtype) for a in arrays],
        scratch_shapes=[pltpu.SemaphoreType.DMA((n, NDEV - 1)), pltpu.SemaphoreType.DMA((n, NDEV - 1)),
                        pltpu.SemaphoreType.DMA((n,))],
        compiler_params=pltpu.CompilerParams(has_side_effects=True),
    )(*arrays)


_HBM = pl.BlockSpec(memory_space=pltpu.HBM)
_SEM = pl.BlockSpec(memory_space=pltpu.SEMAPHORE)
_EFFECT = pltpu.SideEffectType.DATAFLOW_SIDE_EFFECTING


def _split_copies(srcs, lands, send_sems, recv_sems, n_g):
    x, y, c = lax.axis_index("x"), lax.axis_index("y"), lax.axis_index("c")
    me = 4 * x + 2 * y + c
    copies = []
    for rel in range(1, NDEV):
        px, py, pc = x ^ (rel >> 2), y ^ ((rel >> 1) & 1), c ^ (rel & 1)
        peer = 4 * px + 2 * py + pc
        for a in range(len(srcs)):
            copies.append(pltpu.make_async_remote_copy(
                src_ref=srcs[a] if a < n_g else srcs[a].at[peer], dst_ref=lands[a].at[me],
                send_sem=send_sems.at[a * (NDEV - 1) + rel - 1], recv_sem=recv_sems.at[a * (NDEV - 1) + rel - 1],
                device_id=(px, py, pc), device_id_type=pl.DeviceIdType.MESH))
    return copies


def _exchange_start(name, gathers, scatters, after):
    arrays = list(gathers) + list(scatters)
    n_g, n = len(gathers), len(arrays)
    lands = [lax.empty((NDEV,) + a.shape, a.dtype) for a in gathers] + [lax.empty(a.shape, a.dtype) for a in scatters]

    def body(*refs):
        send_sems, recv_sems = refs[2 * n + 1], refs[2 * n + 2]
        for cp in _split_copies(refs[:n], refs[n:2 * n], send_sems, recv_sems, n_g):
            cp.start()
        refs[-1][...] = jnp.zeros_like(refs[-1])

    sems = pltpu.SemaphoreType.DMA((n * (NDEV - 1),))
    out = pl.pallas_call(
        body, name=name, in_specs=[_HBM] * (2 * n) + [pl.BlockSpec(memory_space=pl.ANY)],
        out_specs=(_SEM, _SEM, *[_HBM] * (2 * n), pl.BlockSpec(memory_space=pltpu.VMEM)),
        out_shape=(sems, sems, *[pltpu.HBM(a.shape, a.dtype) for a in arrays + lands], jax.ShapeDtypeStruct((8, LANE), F32)),
        input_output_aliases={i: 2 + i for i in range(2 * n)},
        compiler_params=pltpu.CompilerParams(has_side_effects=_EFFECT),
    )(*[pltpu.with_memory_space_constraint(a, pltpu.HBM) for a in arrays + lands], after)
    return out[0], out[1], list(out[2:2 + 2 * n]), out[-1], n_g


def _exchange_wait(name, started, after):
    send_sems, recv_sems, thru, _, n_g = started
    n = len(thru) // 2

    def body(*refs):
        for cp in _split_copies(refs[:n], refs[n:2 * n], refs[2 * n], refs[2 * n + 1], n_g):
            cp.wait_send()
            cp.wait_recv()

    out = pl.pallas_call(
        body, name=name, in_specs=[_HBM] * (2 * n) + [_SEM, _SEM, pl.BlockSpec(memory_space=pl.ANY)],
        out_specs=[_HBM] * (2 * n), out_shape=[pltpu.HBM(a.shape, a.dtype) for a in thru],
        input_output_aliases={i: i for i in range(2 * n)},
        compiler_params=pltpu.CompilerParams(has_side_effects=_EFFECT),
    )(*thru, send_sems, recv_sems, after)
    me = 4 * lax.axis_index("x") + 2 * lax.axis_index("y") + lax.axis_index("c")
    full = []
    for a in range(n):
        own = out[a][None] if a < n_g else lax.dynamic_index_in_dim(out[a], me, 0, keepdims=True)
        full.append(lax.dynamic_update_index_in_dim(out[n + a], own, me, 0))
    return full


def _adamw(name, staged, w, m, v):
    r, c = w.shape
    tr = _pick(r, 256, 8)

    def body(st_ref, w_ref, m_ref, v_ref, g_ref, d_ref, nm_ref, nv_ref):
        g = st_ref[0].astype(F32)
        for k in range(1, NDEV):
            g = g + st_ref[k].astype(F32)
        m_new = ADAM_B1 * m_ref[...] + (1.0 - ADAM_B1) * g
        v_new = ADAM_B2 * v_ref[...] + (1.0 - ADAM_B2) * jnp.square(g)
        m_hat = m_new / (1.0 - ADAM_B1 ** ADAM_STEP)
        v_hat = v_new / (1.0 - ADAM_B2 ** ADAM_STEP)
        g_ref[...] = g
        d_ref[...] = -ADAM_LR * (m_hat / (jnp.sqrt(v_hat) + ADAM_EPS) + ADAM_WD * w_ref[...])
        nm_ref[...] = m_new
        nv_ref[...] = v_new

    blk = pl.BlockSpec((tr, c), lambda i: (i, 0))
    return pl.pallas_call(
        body, name=name, grid=(r // tr,), in_specs=[pl.BlockSpec((NDEV, tr, c), lambda i: (0, i, 0)), blk, blk, blk],
        out_specs=[blk] * 4, out_shape=[jax.ShapeDtypeStruct((r, c), F32)] * 4,
        compiler_params=_cparams(("parallel",), 48),
    )(staged, w, m, v)


def _pack(parts):
    flat = jnp.concatenate([p.reshape(-1).astype(F32) for p in parts])
    pad = (-flat.shape[0]) % (8 * LANE)
    return jnp.pad(flat, (0, pad)).reshape(-1, LANE)


def _unpack(slab, shapes):
    flat, out, off = slab.reshape(-1), [], 0
    for s in shapes:
        n = 1
        for dim in s:
            n *= dim
        out.append(flat[off:off + n].reshape(s))
        off += n
    return out


def _to_shards(full, axis):
    shp = full.shape
    t = full.reshape(shp[:axis] + (NDEV, shp[axis] // NDEV) + shp[axis + 1:])
    return jnp.moveaxis(t, axis, 0)


def _from_shards(g, axis):
    t = jnp.moveaxis(g, 0, axis)
    shp = t.shape
    return t.reshape(shp[:axis] + (shp[axis] * shp[axis + 1],) + shp[axis + 2:])


def kernel(x, meta_tokens, norm_mix_w, w_in, dn_conv_w, dn_a_log, dn_dt_bias, dn_norm_w, m2_conv_w, m2_conv_b, m2_a_log, m2_dt_bias, m2_d, m2_norm_w, w_out, norm_ffn_w, ffn_up, ffn_conv_w, ffn_down, norm_final_w, loss_target, m_meta_tokens, m_norm_mix_w, m_w_in, m_dn_conv_w, m_dn_a_log, m_dn_dt_bias, m_dn_norm_w, m_m2_conv_w, m_m2_conv_b, m_m2_a_log, m_m2_dt_bias, m_m2_d, m_m2_norm_w, m_w_out, m_norm_ffn_w, m_ffn_up, m_ffn_conv_w, m_ffn_down, m_norm_final_w, v_meta_tokens, v_norm_mix_w, v_w_in, v_dn_conv_w, v_dn_a_log, v_dn_dt_bias, v_dn_norm_w, v_m2_conv_w, v_m2_conv_b, v_m2_a_log, v_m2_dt_bias, v_m2_d, v_m2_norm_w, v_w_out, v_norm_ffn_w, v_ffn_up, v_ffn_conv_w, v_ffn_down, v_norm_final_w):
    seq, d = x.shape[1], x.shape[2]
    t_rows = seq + CH
    nc = t_rows // CH
    dnh, m2h = d // HD, d // M2P
    dff = ffn_down.shape[1] * NDEV
    xbc_w = d + 2 * M2G * NST
    assert seq % CH == 0 and d % (2 * M2P * M2G) == 0 and 2 * dnh + m2h <= LANE
    hb = max(h for h in (8, 4, 2, 1) if dnh % h == 0)
    tm_rw = _pick(t_rows, 208, 16)

    small_sharded = [meta_tokens, dn_conv_w[0], m2_conv_w[0], ffn_conv_w[0]]
    small_shapes = [p.shape for p in small_sharded]
    g_win, g_small = _gather_two_level("gather_w_in", [w_in[0].astype(WIRE), _pack(small_sharded)])
    rest = _exchange_start("gather_rest_start", [w_out[0].astype(WIRE), ffn_up[0].astype(WIRE), ffn_down[0].astype(WIRE)], [],
                           g_small)
    win = _from_shards(g_win, 1)
    small_full = [_unpack(g_small[k], small_shapes) for k in range(NDEV)]
    meta_f, dnconv_f, m2conv_f, ffnconv_f = [jnp.concatenate([small_full[k][i] for k in range(NDEV)], axis=-1) for i in range(4)]

    o_z, o_b, o_a = 3 * d, 4 * d, 4 * d + dnh
    o_m2z = 4 * d + 2 * dnh
    o_xbc, o_dt = o_m2z + d, o_m2z + d + xbc_w
    w_small = jnp.concatenate([win[:, o_b:o_m2z], win[:, o_dt:], jnp.zeros((d, LANE - 2 * dnh - m2h), WIRE)], axis=1)
    w_seg = {"q": win[:, :d], "k": win[:, d:2 * d], "v": win[:, 2 * d:3 * d], "z": win[:, o_z:o_b],
             "m2z": win[:, o_m2z:o_xbc], "xbc": win[:, o_xbc:o_dt], "sm": w_small}

    h0 = jnp.concatenate([jnp.zeros((PADR, d), F32), meta_f, x[0]], axis=0)
    valid = lambda rows: rows >= PADR

    def norm_fwd(name, h, w):
        return _rw(name, lambda rows, j, hv, wv: (_rms(hv, wv),), [("r", h, d, _c0), ("p", w, d, _c0)],
                   [("r", d, d, _c0, MXU)], t_rows, tm_rw)[0]

    hn1 = norm_fwd("norm_mix", h0, norm_mix_w)
    proj = {s: _mm("proj_" + s, hn1, w_seg[s], dep=rest[3]) for s in w_seg}

    def dn_post(sec, cv):
        s = _silu(cv)
        if sec < 2:
            s = s * lax.rsqrt(jnp.sum(s * s, axis=-1, keepdims=True) + EPS)
        if sec == 0:
            s = s * (HD ** -0.5)
        return s

    def dn_prep(sec, name):
        def fn(rows, j, p, w):
            return (jnp.where(valid(rows), dn_post(sec, _conv(p, w)), 0.0),)
        wc = dnconv_f[:, sec * d:(sec + 1) * d]
        return _rw("dn_prep_" + name, fn, [("r", proj[name], HD, _cj), ("p", wc, HD, _cj)], [("r", d, HD, _cj, F32)],
                   t_rows, t_rows, ncol=dnh)[0]

    q_act, k_act, v_act = dn_prep(0, "q"), dn_prep(1, "k"), dn_prep(2, "v")

    lane = lambda: lax.broadcasted_iota(jnp.int32, (1, LANE), 1)

    def lanes_of(vec, off):
        return jnp.pad(vec.astype(F32), ((0, 0), (off, LANE - off - vec.shape[1])))

    gate_params = [lanes_of(dn_a_log, dnh), lanes_of(dn_dt_bias, dnh), lanes_of(m2_a_log, 2 * dnh), lanes_of(m2_dt_bias, 2 * dnh)]

    def gates(rows, sm, p_alog, p_dtb, p_malog, p_mdtb):
        ln = lane()
        is_b, is_g = ln < dnh, jnp.logical_and(ln >= dnh, ln < 2 * dnh)
        is_d = jnp.logical_and(ln >= 2 * dnh, ln < 2 * dnh + m2h)
        beta = jax.nn.sigmoid(sm)
        gdec = -jnp.exp(p_alog) * _softplus(sm + p_dtb)
        dt = _softplus(sm + p_mdtb)
        am = dt * (-jnp.exp(p_malog))
        ok = valid(rows)
        g1 = jnp.where(ok, jnp.where(is_b, beta, jnp.where(is_g, gdec, jnp.where(is_d, dt, 0.0))), 0.0)
        g2 = jnp.where(jnp.logical_and(ok, is_d), am, 0.0)
        return g1, g2

    gate_ins = [("r", proj["sm"], LANE, _c0)] + [("p", p, LANE, _c0) for p in gate_params]
    g1, g2 = _rw("gates", lambda rows, j, *a: gates(rows, *a), gate_ins,
                 [("r", LANE, LANE, _c0, F32), ("r", LANE, LANE, _c0, F32)], t_rows, tm_rw)

    def head_rows(cols, per):
        n = cols.shape[1]
        return cols.reshape(nc, CH, n // per, per).transpose(2, 0, 3, 1)

    def head_cols(rows_):
        ngrp, _, per, _ = rows_.shape
        return rows_.transpose(1, 3, 0, 2).reshape(t_rows, ngrp * per)

    beta_r, gdec_r = head_rows(g1[:, :dnh], hb), head_rows(g1[:, dnh:2 * dnh], hb)
    hpg = m2h // M2G
    dt_r, am_r = head_rows(g1[:, 2 * dnh:2 * dnh + m2h], hpg), head_rows(g2[:, 2 * dnh:2 * dnh + m2h], hpg)

    o_dn, dn_states, dn_tinv = _gdn_fwd(q_act, k_act, v_act, beta_r, gdec_r, hb)

    def dn_out(o, z, w):
        outs = []
        for h in range(dnh):
            sl = slice(h * HD, (h + 1) * HD)
            outs.append(_rms(o[:, sl], w) * _silu(z[:, sl]))
        return jnp.concatenate(outs, axis=1)

    mixed_dn = _rw("dn_out", lambda rows, j, o, z, w: (dn_out(o, z, w),),
                   [("r", o_dn, d, _c0), ("r", proj["z"], d, _c0), ("p", dn_norm_w, HD, _c0)], [("r", d, d, _c0, MXU)],
                   t_rows, tm_rw)[0]

    def m2_prep(rows, j, p, w, b):
        return (jnp.where(valid(rows), _silu(_conv(p, w) + b), 0.0),)

    xbc_act = _rw("m2_prep", m2_prep, [("r", proj["xbc"], LANE, _cj), ("p", m2conv_f, LANE, _cj), ("p", m2_conv_b, LANE, _cj)],
                  [("r", xbc_w, LANE, _cj, F32)], t_rows, t_rows, ncol=xbc_w // LANE)[0]
    y_ssd, m2_states = _ssd_fwd(xbc_act, dt_r, am_r, d)

    d_lanes = jnp.repeat(m2_d.astype(F32), M2P, axis=1)
    gw = d // M2G

    def m2_out(ys, xs, z, dl, nw):
        yv = (ys + dl * xs) * _silu(z)
        outs = []
        for gi in range(M2G):
            sl = slice(gi * gw, (gi + 1) * gw)
            outs.append(_rms(yv[:, sl], nw[:, sl]))
        return jnp.concatenate(outs, axis=1)

    m2_out_ins = [("r", y_ssd, d, _c0), ("r", xbc_act, d, _c0), ("r", proj["m2z"], d, _c0), ("p", d_lanes, d, _c0),
                  ("p", m2_norm_w, d, _c0)]
    mixed_m2 = _rw("m2_out", lambda rows, j, *a: (m2_out(*a),), m2_out_ins, [("r", d, d, _c0, MXU)], t_rows, tm_rw)[0]

    mixed = jnp.concatenate([mixed_dn, mixed_m2], axis=1)
    g_wout, g_wup, g_wdown = _exchange_wait("gather_rest_wait", rest, mixed)
    wout = _from_shards(g_wout, 0)
    wup = _from_shards(g_wup, 1)
    wdown = _from_shards(g_wdown, 0)
    wup_g, wup_v = wup[:, :dff], wup[:, dff:]
    h1 = _mm("out_proj", mixed, wout, add=h0)
    hn2 = norm_fwd("norm_ffn", h1, norm_ffn_w)
    u_g, u_v = _mm("ffn_up_g", hn2, wup_g), _mm("ffn_up_v", hn2, wup_v)
    fc_g, fc_v = ffnconv_f[:, :dff], ffnconv_f[:, dff:]

    def ffn_act(rows, j, ug, uv, wg, wv):
        return (jnp.where(valid(rows), _silu(_conv(ug, wg)) * _conv(uv, wv), 0.0),)

    act = _rw("ffn_act", ffn_act, [("r", u_g, LANE, _cj), ("r", u_v, LANE, _cj), ("p", fc_g, LANE, _cj), ("p", fc_v, LANE, _cj)],
              [("r", dff, LANE, _cj, MXU)], t_rows, t_rows, ncol=dff // LANE)[0]
    h2 = _mm("ffn_down", act, wdown, add=h1, tk=1408)

    def loss_fn(hv, wf, tgt, rows):
        err = jnp.where(rows >= CH, _rms(hv, wf) - tgt, 0.0)
        return 0.5 * jnp.sum(jnp.mean(err * err, axis=-1, keepdims=True), axis=0, keepdims=True)

    def final(rows, j, hv, wf, tgt):
        loss, vjp = jax.vjp(lambda a, b: loss_fn(a, b, tgt, rows), hv, wf)
        dh, dw = vjp(jnp.ones((1, 1), F32))
        return dh, dh, dw, jnp.broadcast_to(loss, (1, LANE))

    wf2 = norm_final_w.reshape(1, d)
    dh2, dh2_m, d_wf, loss_part = _rw(
        "loss_head", final, [("r", h2, d, _c0), ("p", wf2, d, _c0), ("r", loss_target[0], d, _c0, lambda i: jnp.maximum(i - 1, 0))],
        [("r", d, d, _c0, F32), ("r", d, d, _c0, MXU), ("p", 1, d, d, _c0), ("p", 1, LANE, LANE, _c0)], t_rows, CH)
    loss = lax.psum(loss_part[0, 0], MESH_AXES)

    d_act = _mm("d_act", dh2_m, wdown, tb=True)
    gw_down = _mm("gw_down", act, dh2_m, ta=True, tm=1408, tn=1024, tk=1040, out_dtype=WIRE)
    x_down = _exchange_start("grad_down_start", [], [_to_shards(gw_down, 0).astype(WIRE)], gw_down)

    def ffn_act_bwd(rows, j, ug, uv, wg, wv, da):
        cg, cv = _conv(ug, wg), _conv(uv, wv)
        _, vjp = jax.vjp(lambda a, b: _silu(a) * b, cg, cv)
        dcg, dcv = vjp(jnp.where(valid(rows), da, 0.0))
        return _conv_t(dcg, wg), _conv_t(dcv, wv), _conv_w(dcg, ug, len(wg)), _conv_w(dcv, uv, len(wv))

    kf = fc_g.shape[0]
    du_g, du_v, g_fc_g, g_fc_v = _rw(
        "ffn_act_bwd", ffn_act_bwd,
        [("r", u_g, LANE, _cj), ("r", u_v, LANE, _cj), ("p", fc_g, LANE, _cj), ("p", fc_v, LANE, _cj), ("r", d_act, LANE, _cj)],
        [("r", dff, LANE, _cj, MXU), ("r", dff, LANE, _cj, MXU), ("p", kf, dff, LANE, _cj), ("p", kf, dff, LANE, _cj)],
        t_rows, t_rows, ncol=dff // LANE)
    gw_up_g = _mm("gw_up_g", hn2, du_g, ta=True, tm=1024, tn=1408, tk=1040, out_dtype=WIRE, dep=x_down[3])
    gw_up_v = _mm("gw_up_v", hn2, du_v, ta=True, tm=1024, tn=1408, tk=1040, out_dtype=WIRE)
    gw_up_full = jnp.concatenate([gw_up_g, gw_up_v], axis=1)
    x_up = _exchange_start("grad_up_start", [], [_to_shards(gw_up_full, 1).astype(WIRE)], gw_up_full)
    d_hn2 = _mm("d_hn2_v", du_v, wup_v, tb=True, tk=1408, dep=x_up[3],
                add=_mm("d_hn2_g", du_g, wup_g, tb=True, tk=1408, dep=x_up[3]))

    def norm_bwd(name, h, w, dy, dres):
        def fn(rows, j, hv, wv, dyv, dr):
            _, vjp = jax.vjp(_rms, hv, wv)
            dh, dw = vjp(dyv)
            dh = dh + dr
            return dh, dh, dw
        return _rw(name, fn, [("r", h, d, _c0), ("p", w, d, _c0), ("r", dy, d, _c0), ("r", dres, d, _c0)],
                   [("r", d, d, _c0, F32), ("r", d, d, _c0, MXU), ("p", 1, d, d, _c0)], t_rows, tm_rw)

    dh1, dh1_m, g_norm_ffn = norm_bwd("norm_ffn_bwd", h1, norm_ffn_w, d_hn2, dh2)

    gw_out = _mm("gw_out", mixed, dh1_m, ta=True, tm=1024, tn=1024, tk=1040, out_dtype=WIRE)
    x_out = _exchange_start("grad_out_start", [], [_to_shards(gw_out, 0).astype(WIRE)], gw_out)
    d_mixed = _mm("d_mixed", dh1_m, wout, tb=True, dep=x_out[3])

    def fold_heads(vec):
        r = lax.broadcasted_iota(jnp.int32, (d, LANE), 0)
        c = lax.broadcasted_iota(jnp.int32, (d, LANE), 1)
        return _dgh(vec, jnp.where(jnp.logical_and(r >= c * M2P, r < (c + 1) * M2P), 1.0, 0.0), 1, 0)

    def m2_out_bwd(rows, j, ys, xs, z, dl, nw, dy):
        _, vjp = jax.vjp(m2_out, ys, xs, z, dl, nw)
        dys, dxs, dz, ddl, dnw = vjp(dy)
        return dys, dxs, dz, fold_heads(ddl), dnw

    dy_ssd, dxs_skip, d_m2z, g_m2_d, g_m2_norm = _rw(
        "m2_out_bwd", m2_out_bwd, m2_out_ins + [("r", d_mixed, d, lambda j: 1)],
        [("r", d, d, _c0, F32), ("r", d, d, _c0, F32), ("r", d, d, _c0, MXU), ("p", 1, LANE, LANE, _c0), ("p", 1, d, d, _c0)],
        t_rows, _pick(t_rows, 208, 16))

    dxs, db_ssd, dc_ssd, ddt_r, dam_r = _ssd_bwd(xbc_act, dt_r, am_r, m2_states, dy_ssd, d)


    def m2_prep_bwd(n_d):
        def fn(rows, j, p, w, b, *ds):
            pre = _conv(p, w) + b
            _, vjp = jax.vjp(_silu, pre)
            dpre, = vjp(jnp.where(valid(rows), functools.reduce(lambda a_, b_: a_ + b_, ds), 0.0))
            return _conv_t(dpre, w), _conv_w(dpre, p, len(w)), jnp.sum(dpre, axis=0, keepdims=True)
        return fn

    def m2_prep_bwd_call(name, off, width, d_ins):
        blk0 = off // LANE
        ins = [("r", proj["xbc"], LANE, lambda j: blk0 + j), ("p", m2conv_f, LANE, lambda j: blk0 + j),
               ("p", m2_conv_b, LANE, lambda j: blk0 + j)] + d_ins
        km = m2conv_f.shape[0]
        return _rw(name, m2_prep_bwd(len(d_ins)), ins,
                   [("r", width, LANE, _cj, MXU), ("p", km, width, LANE, _cj), ("p", 1, width, LANE, _cj)],
                   t_rows, t_rows, ncol=width // LANE)

    dp_xs, gcw_xs, gcb_xs = m2_prep_bwd_call("m2_prep_bwd_x", 0, d, [("r", dxs, LANE, _cj), ("r", dxs_skip, LANE, _cj)])
    dp_b, gcw_b, gcb_b = m2_prep_bwd_call("m2_prep_bwd_b", d, M2G * NST, [("r", db_ssd, LANE, _cj)])
    dp_c, gcw_c, gcb_c = m2_prep_bwd_call("m2_prep_bwd_c", d + M2G * NST, M2G * NST, [("r", dc_ssd, LANE, _cj)])
    d_pxbc = jnp.concatenate([dp_xs, dp_b, dp_c], axis=1)
    g_m2_conv = jnp.concatenate([gcw_xs, gcw_b, gcw_c], axis=1)
    g_m2_conv_b = jnp.concatenate([gcb_xs, gcb_b, gcb_c], axis=1)

    def dn_out_bwd(rows, j, o, z, w, dy):
        _, vjp = jax.vjp(dn_out, o, z, w)
        return vjp(dy)

    d_o, d_z, g_dn_norm = _rw(
        "dn_out_bwd", dn_out_bwd,
        [("r", o_dn, d, _c0), ("r", proj["z"], d, _c0), ("p", dn_norm_w, HD, _c0), ("r", d_mixed, d, _c0)],
        [("r", d, d, _c0, F32), ("r", d, d, _c0, MXU), ("p", 1, HD, HD, _c0)], t_rows, _pick(t_rows, 208, 16))

    dq, dk, dv, dbeta_r, dgdec_r = _gdn_bwd(q_act, k_act, v_act, beta_r, gdec_r, dn_states, dn_tinv, d_o, hb)

    def dn_prep_bwd(sec, name, dact):
        def fn(rows, j, p, w, da):
            cv = _conv(p, w)
            _, vjp = jax.vjp(functools.partial(dn_post, sec), cv)
            dcv, = vjp(jnp.where(valid(rows), da, 0.0))
            return _conv_t(dcv, w), _conv_w(dcv, p, len(w))
        wc = dnconv_f[:, sec * d:(sec + 1) * d]
        return _rw("dn_prep_bwd_" + name, fn, [("r", proj[name], HD, _cj), ("p", wc, HD, _cj), ("r", dact, HD, _cj)],
                   [("r", d, HD, _cj, MXU), ("p", wc.shape[0], d, HD, _cj)], t_rows, t_rows, ncol=dnh)

    (dp_q, gcw_q), (dp_k, gcw_k), (dp_v, gcw_v) = dn_prep_bwd(0, "q", dq), dn_prep_bwd(1, "k", dk), dn_prep_bwd(2, "v", dv)
    g_dn_conv = jnp.concatenate([gcw_q, gcw_k, gcw_v], axis=1)

    zpad = jnp.zeros((t_rows, LANE - 2 * dnh - m2h), F32)
    dg1 = jnp.concatenate([head_cols(dbeta_r), head_cols(dgdec_r), head_cols(ddt_r), zpad], axis=1)
    dg2 = jnp.concatenate([jnp.zeros((t_rows, 2 * dnh), F32), head_cols(dam_r), zpad], axis=1)

    def gates_bwd(rows, j, sm, pa, pb, pc, pd, d1, d2):
        _, vjp = jax.vjp(lambda *a: gates(rows, *a), sm, pa, pb, pc, pd)
        return vjp((d1, d2))

    dp_sm, g_pa, g_pb, g_pc, g_pd = _rw(
        "gates_bwd", gates_bwd, gate_ins + [("r", dg1, LANE, _c0), ("r", dg2, LANE, _c0)],
        [("r", LANE, LANE, _c0, MXU)] + [("p", 1, LANE, LANE, _c0)] * 4, t_rows, tm_rw)

    dseg = {"q": dp_q, "k": dp_k, "v": dp_v, "z": d_z, "m2z": d_m2z, "xbc": d_pxbc, "sm": dp_sm}
    gw_seg = {s: _mm("gw_in_" + s, hn1, dseg[s], ta=True, tm=1024, tn=1024, tk=1040, out_dtype=WIRE) for s in dseg}
    gsm = gw_seg["sm"]
    gw_in_full = jnp.concatenate([gw_seg["q"], gw_seg["k"], gw_seg["v"], gw_seg["z"], gsm[:, :2 * dnh], gw_seg["m2z"],
                                  gw_seg["xbc"], gsm[:, 2 * dnh:2 * dnh + m2h]], axis=1)
    x_in = _exchange_start("grad_in_start", [], [_to_shards(gw_in_full, 1).astype(WIRE)], gw_in_full)
    d_hn1 = None
    for s in dseg:
        d_hn1 = _mm("d_hn1_" + s, dseg[s], w_seg[s], tb=True, tk=2048, add=d_hn1, dep=x_in[3])
    dh0, _, g_norm_mix = norm_bwd("norm_mix_bwd", h0, norm_mix_w, d_hn1, dh1)

    g_ffn_conv = jnp.concatenate([g_fc_g, g_fc_v], axis=1)
    small_parts = [_to_shards(dh0[PADR:CH], 1), _to_shards(g_dn_conv, 1), _to_shards(g_m2_conv, 1), _to_shards(g_ffn_conv, 1)]
    small_scatter = jnp.stack([_pack([p[k] for p in small_parts]) for k in range(NDEV)])

    rep_names = ["norm_mix_w", "dn_a_log", "dn_dt_bias", "dn_norm_w", "m2_conv_b", "m2_a_log", "m2_dt_bias", "m2_d",
                 "m2_norm_w", "norm_ffn_w", "norm_final_w"]
    rep_grads = [g_norm_mix, g_pa[:, dnh:2 * dnh], g_pb[:, dnh:2 * dnh], g_dn_norm, g_m2_conv_b, g_pc[:, 2 * dnh:2 * dnh + m2h],
                 g_pd[:, 2 * dnh:2 * dnh + m2h], g_m2_d[:, :m2h], g_m2_norm, g_norm_ffn, d_wf.reshape(d)]
    st_rep, st_small = _exchange("exchange_small_grads", [_pack(rep_grads)], [small_scatter])
    st_wdown, = _exchange_wait("grad_down_wait", x_down, st_small)
    st_wup, = _exchange_wait("grad_up_wait", x_up, st_small)
    st_wout, = _exchange_wait("grad_out_wait", x_out, st_small)
    st_win, = _exchange_wait("grad_in_wait", x_in, st_small)

    weights = dict(meta_tokens=meta_tokens, norm_mix_w=norm_mix_w, w_in=w_in, dn_conv_w=dn_conv_w, dn_a_log=dn_a_log,
                   dn_dt_bias=dn_dt_bias, dn_norm_w=dn_norm_w, m2_conv_w=m2_conv_w, m2_conv_b=m2_conv_b, m2_a_log=m2_a_log,
                   m2_dt_bias=m2_dt_bias, m2_d=m2_d, m2_norm_w=m2_norm_w, w_out=w_out, norm_ffn_w=norm_ffn_w, ffn_up=ffn_up,
                   ffn_conv_w=ffn_conv_w, ffn_down=ffn_down, norm_final_w=norm_final_w)
    mom1 = dict(meta_tokens=m_meta_tokens, norm_mix_w=m_norm_mix_w, w_in=m_w_in, dn_conv_w=m_dn_conv_w, dn_a_log=m_dn_a_log,
                dn_dt_bias=m_dn_dt_bias, dn_norm_w=m_dn_norm_w, m2_conv_w=m_m2_conv_w, m2_conv_b=m_m2_conv_b,
                m2_a_log=m_m2_a_log, m2_dt_bias=m_m2_dt_bias, m2_d=m_m2_d, m2_norm_w=m_m2_norm_w, w_out=m_w_out,
                norm_ffn_w=m_norm_ffn_w, ffn_up=m_ffn_up, ffn_conv_w=m_ffn_conv_w, ffn_down=m_ffn_down,
                norm_final_w=m_norm_final_w)
    mom2 = dict(meta_tokens=v_meta_tokens, norm_mix_w=v_norm_mix_w, w_in=v_w_in, dn_conv_w=v_dn_conv_w, dn_a_log=v_dn_a_log,
                dn_dt_bias=v_dn_dt_bias, dn_norm_w=v_dn_norm_w, m2_conv_w=v_m2_conv_w, m2_conv_b=v_m2_conv_b,
                m2_a_log=v_m2_a_log, m2_dt_bias=v_m2_dt_bias, m2_d=v_m2_d, m2_norm_w=v_m2_norm_w, w_out=v_w_out,
                norm_ffn_w=v_norm_ffn_w, ffn_up=v_ffn_up, ffn_conv_w=v_ffn_conv_w, ffn_down=v_ffn_down,
                norm_final_w=v_norm_final_w)
    res = {}
    for name, st in (("w_in", st_win), ("w_out", st_wout), ("ffn_up", st_wup), ("ffn_down", st_wdown)):
        outs = _adamw("adamw_" + name, st, weights[name][0], mom1[name][0], mom2[name][0])
        res[name] = tuple(o[None] for o in outs)

    def adam_packed(label, staged, names):
        shapes = [weights[nm].shape for nm in names]
        outs = _adamw(label, staged, *[_pack([src[nm] for nm in names]) for src in (weights, mom1, mom2)])
        unpacked = [_unpack(o, shapes) for o in outs]
        for i, nm in enumerate(names):
            res[nm] = tuple(u[i] for u in unpacked)

    adam_packed("adamw_small_sharded", st_small, ["meta_tokens", "dn_conv_w", "m2_conv_w", "ffn_conv_w"])
    adam_packed("adamw_replicated", st_rep, rep_names)

    order = list(weights)
    grad_x = dh0[CH:][None]
    return (loss, grad_x, *[res[nm][0] for nm in order], *[res[nm][1] for nm in order], *[res[nm][2] for nm in order],
            *[res[nm][3] for nm in order])
```

```python
import functools

import jax
import jax.numpy as jnp
from jax import lax
from jax.experimental import pallas as pl
from jax.experimental.pallas import tpu as pltpu

F32 = jnp.float32
MXU = jnp.bfloat16
WIRE = jnp.bfloat16
HI = lax.Precision.HIGH

NDEV = 8
CH = 64
NMETA = 16
PADR = CH - NMETA
EPS = 1e-6
HD = 128
M2P = 64
M2G = 4
NST = 128
LANE = 128

ADAM_LR, ADAM_B1, ADAM_B2, ADAM_EPS, ADAM_WD, ADAM_STEP = 0.001, 0.9, 0.999, 1e-08, 0.01, 10

MESH_AXES = ("x", "y", "c")


def _pick(n, target, mult=16):
    best = None
    for t in range(mult, min(n, target) + 1, mult):
        if n % t == 0:
            best = t
    return best if best is not None else n


def _dg(a, b, ca, cb):
    return lax.dot_general(a.astype(MXU), b.astype(MXU), (((ca,), (cb,)), ((), ())), preferred_element_type=F32)


def _dgh(a, b, ca, cb):
    return lax.dot_general(a, b, (((ca,), (cb,)), ((), ())), precision=HI, preferred_element_type=F32)


def _silu(x):
    return x * jax.nn.sigmoid(x)


def _softplus(x):
    return jnp.maximum(x, 0.0) + jnp.log1p(jnp.exp(-jnp.abs(x)))


def _rms(x, w):
    return x * lax.rsqrt(jnp.mean(x * x, axis=-1, keepdims=True) + EPS) * w


def _cparams(sem, vmem_mb):
    return pltpu.CompilerParams(dimension_semantics=sem, vmem_limit_bytes=vmem_mb << 20)


def _mm(name, a, b, *, ta=False, tb=False, add=None, out_dtype=F32, tm=1024, tn=512, tk=2048, dep=None):
    m, kdim = (a.shape[1], a.shape[0]) if ta else a.shape
    n = b.shape[0] if tb else b.shape[1]
    tm = _pick(m, tm, 128 if ta else 16)
    tn = _pick(n, tn, 128)
    tk = _pick(kdim, tk, 16 if (ta and not tb) else 128)
    nk = kdim // tk
    ca, cb = (0 if ta else 1), (1 if tb else 0)

    def body(*refs):
        a_ref, b_ref = refs[0], refs[1]
        add_ref = refs[2] if add is not None else None
        o_ref, acc = refs[-2], refs[-1]
        k = pl.program_id(2)

        @pl.when(k == 0)
        def _():
            acc[...] = jnp.zeros_like(acc)

        acc[...] += _dg(a_ref[...], b_ref[...], ca, cb)

        @pl.when(k == nk - 1)
        def _():
            r = acc[...]
            if add_ref is not None:
                r = r + add_ref[...].astype(F32)
            o_ref[...] = r.astype(o_ref.dtype)

    a_spec = pl.BlockSpec((tk, tm), lambda i, j, k: (k, i)) if ta else pl.BlockSpec((tm, tk), lambda i, j, k: (i, k))
    b_spec = pl.BlockSpec((tn, tk), lambda i, j, k: (j, k)) if tb else pl.BlockSpec((tk, tn), lambda i, j, k: (k, j))
    in_specs, ops = [a_spec, b_spec], [a, b]
    if add is not None:
        in_specs.append(pl.BlockSpec((tm, tn), lambda i, j, k: (i, j)))
        ops.append(add)
    if dep is not None:
        in_specs.append(pl.BlockSpec((8, LANE), lambda i, j, k: (0, 0)))
        ops.append(dep)
    return pl.pallas_call(
        body, name=name, grid=(m // tm, n // tn, nk), in_specs=in_specs,
        out_specs=pl.BlockSpec((tm, tn), lambda i, j, k: (i, j)),
        out_shape=jax.ShapeDtypeStruct((m, n), out_dtype),
        scratch_shapes=[pltpu.VMEM((tm, tn), F32)],
        compiler_params=_cparams(("parallel", "parallel", "arbitrary"), 48),
    )(*ops)


def _rw(name, fn, ins, outs, nrows, tm, ncol=1, vmem_mb=48):
    nrow = nrows // tm
    sub = _pick(tm, 16, 16)
    in_specs, ops = [], []
    for spec in ins:
        kind, arr, bw, cj = spec[:4]
        ops.append(arr)
        if kind == "r":
            ri = spec[4] if len(spec) > 4 else (lambda i: i)
            in_specs.append(pl.BlockSpec((tm, bw), lambda j, i, cj=cj, ri=ri: (ri(i), cj(j))))
        else:
            in_specs.append(pl.BlockSpec((arr.shape[0], bw), lambda j, i, cj=cj: (0, cj(j))))
    out_shape, out_specs = [], []
    for o in outs:
        if o[0] == "r":
            _, width, bw, cj, dt = o
            out_shape.append(jax.ShapeDtypeStruct((nrows, width), dt))
            out_specs.append(pl.BlockSpec((tm, bw), lambda j, i, cj=cj: (i, cj(j))))
        else:
            _, rows, width, bw, cj = o
            out_shape.append(jax.ShapeDtypeStruct((rows, width), F32))
            out_specs.append(pl.BlockSpec((rows, bw), lambda j, i, cj=cj: (0, cj(j))))
    n_in = len(ins)

    def body(*refs):
        j, i = pl.program_id(0), pl.program_id(1)
        in_refs, out_refs = refs[:n_in], refs[n_in:]
        pars = [ref[...] if spec[0] == "p" else None for spec, ref in zip(ins, in_refs)]

        def one(r0, nr):
            rows = i * tm + r0 + lax.broadcasted_iota(jnp.int32, (nr, 1), 0)
            vals = [par if spec[0] == "p" else ref[pl.ds(r0, nr), :] for spec, ref, par in zip(ins, in_refs, pars)]
            parts = []
            for o, val, ref in zip(outs, fn(rows, j, *vals), out_refs):
                if o[0] == "r":
                    ref[pl.ds(r0, nr), :] = val.astype(ref.dtype)
                else:
                    parts.append(val)
            return parts

        if sub >= tm:
            parts = one(0, tm)
        else:
            zero = [jnp.zeros((1, o[3]), F32) for o in outs if o[0] == "p"]
            parts = lax.fori_loop(
                0, tm // sub, lambda s, acc: [a + b for a, b in zip(acc, one(pl.multiple_of(s * sub, sub), sub))], zero)
        for ref, val in zip([r for o, r in zip(outs, out_refs) if o[0] == "p"], parts):
            @pl.when(i == 0)
            def _(ref=ref):
                ref[...] = jnp.zeros_like(ref)

            ref[...] += val

    return pl.pallas_call(
        body, name=name, grid=(ncol, nrow), in_specs=in_specs, out_specs=out_specs, out_shape=out_shape,
        compiler_params=_cparams(("parallel", "arbitrary"), vmem_mb),
    )(*ops)


def _c0(j):
    return 0


def _cj(j):
    return j


def _shift(x, s):
    if s == 0:
        return x
    return pltpu.roll(x, s % x.shape[0], 0)


def _conv(x, w):
    k = len(w)
    return functools.reduce(lambda a, b: a + b, [w[j] * _shift(x, k - 1 - j) for j in range(k)])


def _conv_t(dy, w, rows, t_end):
    k = len(w)
    terms = []
    for j in range(k):
        s = k - 1 - j
        v = _shift(dy, -s)
        if t_end is not None and s > 0:
            v = jnp.where(rows + s < t_end, v, 0.0)
        terms.append(w[j] * v)
    return functools.reduce(lambda a, b: a + b, terms)


def _conv_w(dy, x, k):
    return [jnp.sum(dy * _shift(x, k - 1 - j), axis=0, keepdims=True) for j in range(k)]


HALO = 8


def _cv(name, fn, row_ins, par_ins, row_outs, par_outs, nrows, ncol, chunk=CH):
    n_chunks = nrows // chunk
    assert nrows % chunk == 0 and n_chunks >= 3
    n_ri, n_pi, n_ro = len(row_ins), len(par_ins), len(row_outs)

    def body(*refs):
        rin, pin = refs[:n_ri], refs[n_ri:n_ri + n_pi]
        rout, pout = refs[n_ri + n_pi:n_ri + n_pi + n_ro], refs[n_ri + n_pi + n_ro:]
        pars = [[p[pl.ds(r, 1), :] for r in range(p.shape[0])] for p in pin]

        def run(r0, top, bot, last):
            wlen = top + chunk + bot
            w0 = r0 - top if isinstance(r0, int) else pl.multiple_of(r0 - top, HALO)
            local = lax.broadcasted_iota(jnp.int32, (wlen, 1), 0)
            own = jnp.logical_and(local >= top, local < top + chunk)
            outs, parts = fn(w0 + local, own, last, [ref[pl.ds(w0, wlen), :] for ref in rin], pars)
            for ref, val in zip(rout, outs):
                ref[pl.ds(r0, chunk), :] = val[top:top + chunk].astype(ref.dtype)
            return parts

        def add(acc, parts):
            return [[a + b for a, b in zip(ra, rb)] for ra, rb in zip(acc, parts)]

        acc = run(0, 0, HALO, False)
        acc = lax.fori_loop(1, n_chunks - 1, lambda i, a: add(a, run(pl.multiple_of(i * chunk, chunk), HALO, HALO, False)), acc)
        acc = add(acc, run(nrows - chunk, HALO, 0, True))
        for ref, prow in zip(pout, acc):
            for r, v in enumerate(prow):
                ref[pl.ds(r, 1), :] = v

    in_specs = [pl.BlockSpec((nrows, LANE), lambda j, cj=cj: (0, cj(j))) for _, cj in row_ins]
    in_specs += [pl.BlockSpec((a.shape[0], LANE), lambda j, cj=cj: (0, cj(j))) for a, cj in par_ins]
    out_specs = [pl.BlockSpec((nrows, LANE), lambda j, cj=cj: (0, cj(j))) for _, cj, _ in row_outs]
    out_specs += [pl.BlockSpec((k, LANE), lambda j, cj=cj: (0, cj(j))) for k, _, cj in par_outs]
    out_shape = [jax.ShapeDtypeStruct((nrows, width), dt) for width, _, dt in row_outs]
    out_shape += [jax.ShapeDtypeStruct((k, width), F32) for k, width, _ in par_outs]
    return pl.pallas_call(
        body, name=name, grid=(ncol,), in_specs=in_specs, out_specs=out_specs, out_shape=out_shape,
        compiler_params=_cparams(("parallel",), 48),
    )(*[a for a, _ in row_ins], *[a for a, _ in par_ins])


def _tri():
    r = lax.broadcasted_iota(jnp.int32, (CH, CH), 0)
    c = lax.broadcasted_iota(jnp.int32, (CH, CH), 1)
    return r, c


def _col(row):
    r, c = _tri()
    return jnp.sum(jnp.where(r == c, row, 0.0), axis=1, keepdims=True)


def _cumsum_rc(g_r):
    r, c = _tri()
    g_c = _col(g_r)
    cs_r = jnp.sum(jnp.where(r <= c, g_c, 0.0), axis=0, keepdims=True)
    cs_c = jnp.sum(jnp.where(c <= r, g_r, 0.0), axis=1, keepdims=True)
    return cs_r, cs_c


def _decay(cs_r, cs_c):
    r, c = _tri()
    return jnp.exp(jnp.where(c <= r, cs_c - cs_r, -jnp.inf))


def _gdn_a(ks, betas, gs):
    r, c = _tri()
    cs = [_cumsum_rc(g) for g in gs]
    kk = [_dg(k, k, 1, 1) for k in ks]
    return [jnp.where(c < r, _col(b) * kki * _decay(*csi), 0.0) for b, kki, csi in zip(betas, kk, cs)]


def _neumann(a_list):
    r, c = _tri()
    xs = [jnp.where(r == c, 1.0, 0.0) - a for a in a_list]
    ps = list(a_list)
    n = 2
    while n < CH:
        ps = [_dgh(p, p, 1, 0) for p in ps]
        xs = [x + _dgh(x, p, 1, 0) for x, p in zip(xs, ps)]
        n *= 2
    return xs


def _gdn_rest(ss, qs, ks, vs, betas, gs, ts):
    n = range(len(ss))
    cs = [_cumsum_rc(g) for g in gs]
    dm = [_decay(*csi) for csi in cs]
    ecs = [jnp.exp(csi[1]) for csi in cs]
    bc = [_col(b) for b in betas]
    u = [_dgh(ts[i], vs[i] * bc[i], 1, 0) for i in n]
    w = [_dgh(ts[i], ks[i] * (bc[i] * ecs[i]), 1, 0) for i in n]
    ws = [_dg(w[i], ss[i], 1, 0) for i in n]
    v_new = [u[i] - ws[i] for i in n]
    qk = [_dg(qs[i], ks[i], 1, 1) * dm[i] for i in n]
    o_in = [_dg(qs[i] * ecs[i], ss[i], 1, 0) for i in n]
    o = [o_in[i] + _dg(qk[i], v_new[i], 1, 0) for i in n]
    g_last = [jnp.sum(g, axis=1, keepdims=True) for g in gs]
    s_new = [ss[i] * jnp.exp(g_last[i]) + _dg(ks[i] * jnp.exp(g_last[i] - cs[i][1]), v_new[i], 0, 0) for i in n]
    return s_new, o


def _gdn_fwd(q, k, v, beta, g, hb):
    t_rows, d = q.shape
    nc, ng, w = t_rows // CH, d // (HD * hb), HD * hb
    sls = [slice(h * HD, (h + 1) * HD) for h in range(hb)]

    def body(q_ref, k_ref, v_ref, b_ref, g_ref, o_ref, ss_ref, ts_ref, s_scr):
        c = pl.program_id(1)

        @pl.when(c == 0)
        def _():
            s_scr[...] = jnp.zeros_like(s_scr)

        qs, ks, vs = ([ref[:, sl] for sl in sls] for ref in (q_ref, k_ref, v_ref))
        br = [b_ref[0, 0, pl.ds(h, 1), :] for h in range(hb)]
        gr = [g_ref[0, 0, pl.ds(h, 1), :] for h in range(hb)]
        s0 = [s_scr[h] for h in range(hb)]
        tm = _neumann(_gdn_a(ks, br, gr))
        s1, o = _gdn_rest(s0, qs, ks, vs, br, gr, tm)
        for h in range(hb):
            ss_ref[0, 0, h] = s0[h]
            ts_ref[0, 0, h] = tm[h]
            o_ref[:, sls[h]] = o[h]
            s_scr[h] = s1[h]

    blk = pl.BlockSpec((CH, w), lambda n, c: (c, n))
    row = pl.BlockSpec((1, 1, hb, CH), lambda n, c: (n, c, 0, 0))
    return pl.pallas_call(
        body, name="gdn_fwd", grid=(ng, nc), in_specs=[blk, blk, blk, row, row],
        out_specs=[blk, pl.BlockSpec((1, 1, hb, HD, HD), lambda n, c: (n, c, 0, 0, 0)),
                   pl.BlockSpec((1, 1, hb, CH, CH), lambda n, c: (n, c, 0, 0, 0))],
        out_shape=[jax.ShapeDtypeStruct((t_rows, d), F32), jax.ShapeDtypeStruct((ng, nc, hb, HD, HD), F32),
                   jax.ShapeDtypeStruct((ng, nc, hb, CH, CH), F32)],
        scratch_shapes=[pltpu.VMEM((hb, HD, HD), F32)],
        compiler_params=_cparams(("parallel", "arbitrary"), 32),
    )(q, k, v, beta, g)


def _gdn_bwd(q, k, v, beta, g, ss, ts, do, hb):
    t_rows, d = q.shape
    nc, ng, w = t_rows // CH, d // (HD * hb), HD * hb
    sls = [slice(h * HD, (h + 1) * HD) for h in range(hb)]

    def body(q_ref, k_ref, v_ref, b_ref, g_ref, ss_ref, ts_ref, do_ref, dq_ref, dk_ref, dv_ref, db_ref, dg_ref, ds_scr):
        cr = pl.program_id(1)

        @pl.when(cr == 0)
        def _():
            ds_scr[...] = jnp.zeros_like(ds_scr)

        first = cr == nc - 1
        rowi = lax.broadcasted_iota(jnp.int32, (CH, 1), 0)
        lani = lax.broadcasted_iota(jnp.int32, (1, CH), 1)
        keep_c = jnp.logical_or(jnp.logical_not(first), rowi >= PADR)
        keep_r = jnp.logical_or(jnp.logical_not(first), lani >= PADR)
        hs = range(hb)
        qs, ks, vs, dos = ([ref[:, sl] for sl in sls] for ref in (q_ref, k_ref, v_ref, do_ref))
        br = [b_ref[0, 0, pl.ds(h, 1), :] for h in hs]
        gr = [g_ref[0, 0, pl.ds(h, 1), :] for h in hs]
        tm = [ts_ref[0, 0, h] for h in hs]
        _, vjp_rest = jax.vjp(_gdn_rest, [ss_ref[0, 0, h] for h in hs], qs, ks, vs, br, gr, tm)
        ds0, dq, dk, dv, db, dg, dt = vjp_rest(([ds_scr[h] for h in hs], dos))
        dtt = [_dgh(dt[h], tm[h], 1, 1) for h in hs]
        da = [-_dgh(tm[h], dtt[h], 0, 0) for h in hs]
        _, vjp_a = jax.vjp(_gdn_a, ks, br, gr)
        dk2, db2, dg2 = vjp_a(da)
        for h in hs:
            ds_scr[h] = ds0[h]
            dq_ref[:, sls[h]] = jnp.where(keep_c, dq[h], 0.0)
            dk_ref[:, sls[h]] = jnp.where(keep_c, dk[h] + dk2[h], 0.0)
            dv_ref[:, sls[h]] = jnp.where(keep_c, dv[h], 0.0)
            db_ref[0, 0, pl.ds(h, 1), :] = jnp.where(keep_r, db[h] + db2[h], 0.0)
            dg_ref[0, 0, pl.ds(h, 1), :] = jnp.where(keep_r, dg[h] + dg2[h], 0.0)

    blk = pl.BlockSpec((CH, w), lambda n, c: (nc - 1 - c, n))
    row = pl.BlockSpec((1, 1, hb, CH), lambda n, c: (n, nc - 1 - c, 0, 0))
    return pl.pallas_call(
        body, name="gdn_bwd", grid=(ng, nc),
        in_specs=[blk, blk, blk, row, row, pl.BlockSpec((1, 1, hb, HD, HD), lambda n, c: (n, nc - 1 - c, 0, 0, 0)),
                  pl.BlockSpec((1, 1, hb, CH, CH), lambda n, c: (n, nc - 1 - c, 0, 0, 0)), blk],
        out_specs=[blk, blk, blk, row, row],
        out_shape=[jax.ShapeDtypeStruct((t_rows, d), F32)] * 3 + [jax.ShapeDtypeStruct((ng, nc, hb, CH), F32)] * 2,
        scratch_shapes=[pltpu.VMEM((hb, HD, HD), F32)],
        compiler_params=_cparams(("parallel", "arbitrary"), 32),
    )(q, k, v, beta, g, ss, ts, do)


def _ssd_group(s, xs, bm, cm, dt_r, a_r):
    prs = range(len(s))
    first = lax.broadcasted_iota(jnp.int32, (1, 2 * M2P), 1) < M2P

    def pick(vals, p):
        return jnp.where(first, vals[2 * p], vals[2 * p + 1])

    cs = [_cumsum_rc(a) for a in a_r]
    lm = [_decay(*csi) for csi in cs]
    ecs = [jnp.exp(csi[1]) for csi in cs]
    alast = [jnp.sum(a, axis=1, keepdims=True) for a in a_r]
    ealast = [jnp.exp(al) for al in alast]
    wt = [jnp.exp(al - csi[1]) for al, csi in zip(alast, cs)]
    dtc = [_col(t) for t in dt_r]
    xdt = [xs[:, p * LANE:(p + 1) * LANE] * pick(dtc, p) for p in prs]
    cb = _dg(cm, bm, 1, 1)
    y0 = [_dg(cb * lm[2 * p], xdt[p], 1, 0) for p in prs]
    y1 = [_dg(cb * lm[2 * p + 1], xdt[p], 1, 0) for p in prs]
    yo = [_dg(cm, s[p], 1, 0) for p in prs]
    y = [jnp.where(first, y0[p], y1[p]) + yo[p] * pick(ecs, p) for p in prs]
    s_new = [s[p] * pick(ealast, p) + _dg(bm, xdt[p] * pick(wt, p), 0, 0) for p in prs]
    return s_new, jnp.concatenate(y, axis=1)


def _ssd_specs(nc, d, rev):
    hpg = (d // M2P) // M2G
    gw = hpg * M2P
    cc = (lambda c: nc - 1 - c) if rev else (lambda c: c)
    xs = pl.BlockSpec((CH, gw), lambda g, c: (cc(c), g))
    bm = pl.BlockSpec((CH, NST), lambda g, c: (cc(c), d // LANE + g))
    cm = pl.BlockSpec((CH, NST), lambda g, c: (cc(c), d // LANE + M2G + g))
    row = pl.BlockSpec((1, 1, hpg, CH), lambda g, c: (g, cc(c), 0, 0))
    st = pl.BlockSpec((1, 1, hpg // 2, NST, LANE), lambda g, c: (g, cc(c), 0, 0, 0))
    return xs, bm, cm, row, st, hpg


def _ssd_fwd(xbc, dt, a, d):
    t_rows = xbc.shape[0]
    nc = t_rows // CH
    xs, bm, cm, row, st, hpg = _ssd_specs(nc, d, False)
    ppg = hpg // 2

    def body(xs_ref, b_ref, c_ref, dt_ref, a_ref, y_ref, ss_ref, s_scr):
        c = pl.program_id(1)

        @pl.when(c == 0)
        def _():
            s_scr[...] = jnp.zeros_like(s_scr)

        s0 = [s_scr[p] for p in range(ppg)]
        for p in range(ppg):
            ss_ref[0, 0, p] = s0[p]
        dt_r = [dt_ref[0, 0, pl.ds(h, 1), :] for h in range(hpg)]
        a_r = [a_ref[0, 0, pl.ds(h, 1), :] for h in range(hpg)]
        s1, y = _ssd_group(s0, xs_ref[...], b_ref[...], c_ref[...], dt_r, a_r)
        y_ref[...] = y
        for p in range(ppg):
            s_scr[p] = s1[p]

    return pl.pallas_call(
        body, name="ssd_fwd", grid=(M2G, nc), in_specs=[xs, bm, cm, row, row], out_specs=[xs, st],
        out_shape=[jax.ShapeDtypeStruct((t_rows, d), F32), jax.ShapeDtypeStruct((M2G, nc, ppg, NST, LANE), F32)],
        scratch_shapes=[pltpu.VMEM((ppg, NST, LANE), F32)],
        compiler_params=_cparams(("parallel", "arbitrary"), 32),
    )(xbc, xbc, xbc, dt, a)


def _ssd_bwd(xbc, dt, a, ss, dy, d):
    t_rows = xbc.shape[0]
    nc = t_rows // CH
    xs, bm, cm, row, st, hpg = _ssd_specs(nc, d, True)
    ppg = hpg // 2

    def body(xs_ref, b_ref, c_ref, dt_ref, a_ref, ss_ref, dy_ref, dxs_ref, db_ref, dc_ref, ddt_ref, da_ref, ds_scr):
        cr = pl.program_id(1)

        @pl.when(cr == 0)
        def _():
            ds_scr[...] = jnp.zeros_like(ds_scr)

        first = cr == nc - 1
        keep_c = jnp.logical_or(jnp.logical_not(first), lax.broadcasted_iota(jnp.int32, (CH, 1), 0) >= PADR)
        keep_r = jnp.logical_or(jnp.logical_not(first), lax.broadcasted_iota(jnp.int32, (1, CH), 1) >= PADR)
        dt_r = [dt_ref[0, 0, pl.ds(h, 1), :] for h in range(hpg)]
        a_r = [a_ref[0, 0, pl.ds(h, 1), :] for h in range(hpg)]
        s0 = [ss_ref[0, 0, p] for p in range(ppg)]
        _, vjp = jax.vjp(_ssd_group, s0, xs_ref[...], b_ref[...], c_ref[...], dt_r, a_r)
        ds0, dxs, db, dc, ddt, da = vjp(([ds_scr[p] for p in range(ppg)], dy_ref[...]))
        for p in range(ppg):
            ds_scr[p] = ds0[p]
        dxs_ref[...] = jnp.where(keep_c, dxs, 0.0)
        db_ref[...] = jnp.where(keep_c, db, 0.0)
        dc_ref[...] = jnp.where(keep_c, dc, 0.0)
        for h in range(hpg):
            ddt_ref[0, 0, pl.ds(h, 1), :] = jnp.where(keep_r, ddt[h], 0.0)
            da_ref[0, 0, pl.ds(h, 1), :] = jnp.where(keep_r, da[h], 0.0)

    grp = pl.BlockSpec((CH, NST), lambda g, c: (nc - 1 - c, g))
    return pl.pallas_call(
        body, name="ssd_bwd", grid=(M2G, nc), in_specs=[xs, bm, cm, row, row, st, xs],
        out_specs=[xs, grp, grp, row, row],
        out_shape=[jax.ShapeDtypeStruct((t_rows, d), F32)] + [jax.ShapeDtypeStruct((t_rows, M2G * NST), F32)] * 2
        + [jax.ShapeDtypeStruct((M2G, nc, hpg, CH), F32)] * 2,
        scratch_shapes=[pltpu.VMEM((ppg, NST, LANE), F32)],
        compiler_params=_cparams(("parallel", "arbitrary"), 32),
    )(xbc, xbc, xbc, dt, a, ss, dy)


def _exchange(name, gathers, scatters, after):
    arrays = list(gathers) + list(scatters)
    n_g, n = len(gathers), len(arrays)

    def body(*refs):
        ins, outs = refs[:n], refs[n + 1:2 * n + 1]
        send_sems, recv_sems, local_sems = refs[2 * n + 1:]
        x, y, c = lax.axis_index("x"), lax.axis_index("y"), lax.axis_index("c")
        me = 4 * x + 2 * y + c

        def src(a, slot):
            return ins[a] if a < n_g else ins[a].at[slot]

        local = [pltpu.make_async_copy(src(a, me), outs[a].at[me], local_sems.at[a]) for a in range(n)]
        for cp in local:
            cp.start()
        copies = []
        for rel in range(1, NDEV):
            px, py, pc = x ^ (rel >> 2), y ^ ((rel >> 1) & 1), c ^ (rel & 1)
            peer = 4 * px + 2 * py + pc
            for a in range(n):
                copies.append(pltpu.make_async_remote_copy(
                    src_ref=src(a, peer), dst_ref=outs[a].at[me], send_sem=send_sems.at[a, rel - 1],
                    recv_sem=recv_sems.at[a, rel - 1], device_id=(px, py, pc), device_id_type=pl.DeviceIdType.MESH))
        for cp in copies:
            cp.start()
        for cp in copies:
            cp.wait_recv()
        for cp in copies:
            cp.wait_send()
        for cp in local:
            cp.wait()

    any_spec = pl.BlockSpec(memory_space=pl.ANY)
    out_shape = [jax.ShapeDtypeStruct((NDEV,) + a.shape, a.dtype) for a in gathers]
    out_shape += [jax.ShapeDtypeStruct(a.shape, a.dtype) for a in scatters]
    return pl.pallas_call(
        body, name=name, in_specs=[any_spec] * (n + 1), out_specs=[any_spec] * n, out_shape=out_shape,
        scratch_shapes=[pltpu.SemaphoreType.DMA((n, NDEV - 1)), pltpu.SemaphoreType.DMA((n, NDEV - 1)),
                        pltpu.SemaphoreType.DMA((n,))],
        compiler_params=pltpu.CompilerParams(has_side_effects=True),
    )(*arrays, after)


def _gather_two_level(name, arrays):
    n = len(arrays)

    def body(*refs):
        ins, outs = refs[:n], refs[n:2 * n]
        send_sems, recv_sems, local_sems = refs[2 * n:]
        x, y, c = lax.axis_index("x"), lax.axis_index("y"), lax.axis_index("c")
        me, sibling = (x, y, c), (x, y, 1 - c)
        chips = [(1 - x, y), (x, 1 - y), (1 - x, 1 - y)]

        def copy(a, k, block, to, src=None):
            dst = outs[a].at[4 * block[0] + 2 * block[1] + block[2]]
            return pltpu.make_async_remote_copy(
                src_ref=dst if src is None else src, dst_ref=dst, send_sem=send_sems.at[a, k], recv_sem=recv_sems.at[a, k],
                device_id=to, device_id_type=pl.DeviceIdType.MESH)

        mine = [pltpu.make_async_copy(ins[a], outs[a].at[4 * x + 2 * y + c], local_sems.at[a]) for a in range(n)]
        for cp in mine:
            cp.start()
        first = []
        for a in range(n):
            first.append(copy(a, 0, me, sibling, src=ins[a]))
            first += [copy(a, 1 + j, me, (*chip, c), src=ins[a]) for j, chip in enumerate(chips)]
        for cp in first:
            cp.start()
        passed = [[copy(a, 4 + j, (*chip, c), sibling) for j, chip in enumerate(chips)] for a in range(n)]
        for j, chip in enumerate(chips):
            for a in range(n):
                copy(a, 1 + j, (*chip, c), me).wait_recv()
                passed[a][j].start()
        for a in range(n):
            copy(a, 0, sibling, me).wait_recv()
            for j, chip in enumerate(chips):
                copy(a, 4 + j, (*chip, 1 - c), me).wait_recv()
        for cp in first + [cp for row in passed for cp in row]:
            cp.wait_send()
        for cp in mine:
            cp.wait()

    any_spec = pl.BlockSpec(memory_space=pl.ANY)
    return pl.pallas_call(
        body, name=name, in_specs=[any_spec] * n, out_specs=[any_spec] * n,
        out_shape=[jax.ShapeDtypeStruct((NDEV,) + a.shape, a.dtype) for a in arrays],
        scratch_shapes=[pltpu.SemaphoreType.DMA((n, NDEV - 1)), pltpu.SemaphoreType.DMA((n, NDEV - 1)),
                        pltpu.SemaphoreType.DMA((n,))],
        compiler_params=pltpu.CompilerParams(has_side_effects=True),
    )(*arrays)


_HBM = pl.BlockSpec(memory_space=pltpu.HBM)
_SEM = pl.BlockSpec(memory_space=pltpu.SEMAPHORE)
_EFFECT = pltpu.SideEffectType.DATAFLOW_SIDE_EFFECTING


def _split_copies(srcs, lands, send_sems, recv_sems, n_g):
    x, y, c = lax.axis_index("x"), lax.axis_index("y"), lax.axis_index("c")
    me = 4 * x + 2 * y + c
    copies = []
    for rel in range(1, NDEV):
        px, py, pc = x ^ (rel >> 2), y ^ ((rel >> 1) & 1), c ^ (rel & 1)
        peer = 4 * px + 2 * py + pc
        for a in range(len(srcs)):
            copies.append(pltpu.make_async_remote_copy(
                src_ref=srcs[a] if a < n_g else srcs[a].at[peer], dst_ref=lands[a].at[me],
                send_sem=send_sems.at[a * (NDEV - 1) + rel - 1], recv_sem=recv_sems.at[a * (NDEV - 1) + rel - 1],
                device_id=(px, py, pc), device_id_type=pl.DeviceIdType.MESH))
    return copies


def _exchange_start(name, gathers, scatters, after):
    arrays = list(gathers) + list(scatters)
    n_g, n = len(gathers), len(arrays)
    lands = [lax.empty((NDEV,) + a.shape, a.dtype) for a in gathers] + [lax.empty(a.shape, a.dtype) for a in scatters]

    def body(*refs):
        send_sems, recv_sems = refs[2 * n + 1], refs[2 * n + 2]
        for cp in _split_copies(refs[:n], refs[n:2 * n], send_sems, recv_sems, n_g):
            cp.start()
        refs[-1][...] = jnp.zeros_like(refs[-1])

    sems = pltpu.SemaphoreType.DMA((n * (NDEV - 1),))
    out = pl.pallas_call(
        body, name=name, in_specs=[_HBM] * (2 * n) + [pl.BlockSpec(memory_space=pl.ANY)],
        out_specs=(_SEM, _SEM, *[_HBM] * (2 * n), pl.BlockSpec(memory_space=pltpu.VMEM)),
        out_shape=(sems, sems, *[pltpu.HBM(a.shape, a.dtype) for a in arrays + lands], jax.ShapeDtypeStruct((8, LANE), F32)),
        input_output_aliases={i: 2 + i for i in range(2 * n)},
        compiler_params=pltpu.CompilerParams(has_side_effects=_EFFECT),
    )(*[pltpu.with_memory_space_constraint(a, pltpu.HBM) for a in arrays + lands], after)
    return out[0], out[1], list(out[2:2 + 2 * n]), out[-1], n_g


def _exchange_wait(name, started, after):
    send_sems, recv_sems, thru, _, n_g = started
    n = len(thru) // 2

    def body(*refs):
        for cp in _split_copies(refs[:n], refs[n:2 * n], refs[2 * n], refs[2 * n + 1], n_g):
            cp.wait_send()
            cp.wait_recv()

    out = pl.pallas_call(
        body, name=name, in_specs=[_HBM] * (2 * n) + [_SEM, _SEM, pl.BlockSpec(memory_space=pl.ANY)],
        out_specs=[_HBM] * (2 * n), out_shape=[pltpu.HBM(a.shape, a.dtype) for a in thru],
        input_output_aliases={i: i for i in range(2 * n)},
        compiler_params=pltpu.CompilerParams(has_side_effects=_EFFECT),
    )(*thru, send_sems, recv_sems, after)
    me = 4 * lax.axis_index("x") + 2 * lax.axis_index("y") + lax.axis_index("c")
    full = []
    for a in range(n):
        own = out[a][None] if a < n_g else lax.dynamic_index_in_dim(out[a], me, 0, keepdims=True)
        full.append(lax.dynamic_update_index_in_dim(out[n + a], own, me, 0))
    return full


def _adamw(name, staged, w, m, v):
    r, c = w.shape
    tr = _pick(r, 256, 8)

    def body(st_ref, w_ref, m_ref, v_ref, g_ref, d_ref, nm_ref, nv_ref):
        g = st_ref[0].astype(F32)
        for k in range(1, NDEV):
            g = g + st_ref[k].astype(F32)
        m_new = ADAM_B1 * m_ref[...] + (1.0 - ADAM_B1) * g
        v_new = ADAM_B2 * v_ref[...] + (1.0 - ADAM_B2) * jnp.square(g)
        m_hat = m_new / (1.0 - ADAM_B1 ** ADAM_STEP)
        v_hat = v_new / (1.0 - ADAM_B2 ** ADAM_STEP)
        g_ref[...] = g
        d_ref[...] = -ADAM_LR * (m_hat / (jnp.sqrt(v_hat) + ADAM_EPS) + ADAM_WD * w_ref[...])
        nm_ref[...] = m_new
        nv_ref[...] = v_new

    blk = pl.BlockSpec((tr, c), lambda i: (i, 0))
    return pl.pallas_call(
        body, name=name, grid=(r // tr,), in_specs=[pl.BlockSpec((NDEV, tr, c), lambda i: (0, i, 0)), blk, blk, blk],
        out_specs=[blk] * 4, out_shape=[jax.ShapeDtypeStruct((r, c), F32)] * 4,
        compiler_params=_cparams(("parallel",), 48),
    )(staged, w, m, v)


def _pack(parts):
    flat = jnp.concatenate([p.reshape(-1).astype(F32) for p in parts])
    pad = (-flat.shape[0]) % (8 * LANE)
    return jnp.pad(flat, (0, pad)).reshape(-1, LANE)


def _unpack(slab, shapes):
    flat, out, off = slab.reshape(-1), [], 0
    for s in shapes:
        n = 1
        for dim in s:
            n *= dim
        out.append(flat[off:off + n].reshape(s))
        off += n
    return out


def _to_shards(full, axis):
    shp = full.shape
    t = full.reshape(shp[:axis] + (NDEV, shp[axis] // NDEV) + shp[axis + 1:])
    return jnp.moveaxis(t, axis, 0)


def _from_shards(g, axis):
    t = jnp.moveaxis(g, 0, axis)
    shp = t.shape
    return t.reshape(shp[:axis] + (shp[axis] * shp[axis + 1],) + shp[axis + 2:])


def kernel(x, meta_tokens, norm_mix_w, w_in, dn_conv_w, dn_a_log, dn_dt_bias, dn_norm_w, m2_conv_w, m2_conv_b, m2_a_log, m2_dt_bias, m2_d, m2_norm_w, w_out, norm_ffn_w, ffn_up, ffn_conv_w, ffn_down, norm_final_w, loss_target, m_meta_tokens, m_norm_mix_w, m_w_in, m_dn_conv_w, m_dn_a_log, m_dn_dt_bias, m_dn_norm_w, m_m2_conv_w, m_m2_conv_b, m_m2_a_log, m_m2_dt_bias, m_m2_d, m_m2_norm_w, m_w_out, m_norm_ffn_w, m_ffn_up, m_ffn_conv_w, m_ffn_down, m_norm_final_w, v_meta_tokens, v_norm_mix_w, v_w_in, v_dn_conv_w, v_dn_a_log, v_dn_dt_bias, v_dn_norm_w, v_m2_conv_w, v_m2_conv_b, v_m2_a_log, v_m2_dt_bias, v_m2_d, v_m2_norm_w, v_w_out, v_norm_ffn_w, v_ffn_up, v_ffn_conv_w, v_ffn_down, v_norm_final_w):
    seq, d = x.shape[1], x.shape[2]
    t_rows = seq + CH
    nc = t_rows // CH
    dnh, m2h = d // HD, d // M2P
    dff = ffn_down.shape[1] * NDEV
    xbc_w = d + 2 * M2G * NST
    assert seq % CH == 0 and d % (2 * M2P * M2G) == 0 and 2 * dnh + m2h <= LANE
    hb = max(h for h in (8, 4, 2, 1) if dnh % h == 0)
    tm_rw = _pick(t_rows, 208, 16)

    small_sharded = [meta_tokens, dn_conv_w[0], m2_conv_w[0], ffn_conv_w[0]]
    small_shapes = [p.shape for p in small_sharded]
    g_win, g_small = _gather_two_level("gather_w_in", [w_in[0].astype(WIRE), _pack(small_sharded)])
    rest = _exchange_start("gather_rest_start", [w_out[0].astype(WIRE), ffn_up[0].astype(WIRE), ffn_down[0].astype(WIRE)], [],
                           g_small)
    win = _from_shards(g_win, 1)
    small_full = [_unpack(g_small[k], small_shapes) for k in range(NDEV)]
    meta_f, dnconv_f, m2conv_f, ffnconv_f = [jnp.concatenate([small_full[k][i] for k in range(NDEV)], axis=-1) for i in range(4)]

    o_z, o_b, o_a = 3 * d, 4 * d, 4 * d + dnh
    o_m2z = 4 * d + 2 * dnh
    o_xbc, o_dt = o_m2z + d, o_m2z + d + xbc_w
    w_small = jnp.concatenate([win[:, o_b:o_m2z], win[:, o_dt:], jnp.zeros((d, LANE - 2 * dnh - m2h), WIRE)], axis=1)
    w_seg = {"q": win[:, :d], "k": win[:, d:2 * d], "v": win[:, 2 * d:3 * d], "z": win[:, o_z:o_b],
             "m2z": win[:, o_m2z:o_xbc], "xbc": win[:, o_xbc:o_dt], "sm": w_small}

    h0 = jnp.concatenate([jnp.zeros((PADR, d), F32), meta_f, x[0]], axis=0)
    valid = lambda rows: rows >= PADR

    def norm_fwd(name, h, w):
        return _rw(name, lambda rows, j, hv, wv: (_rms(hv, wv),), [("r", h, d, _c0), ("p", w, d, _c0)],
                   [("r", d, d, _c0, MXU)], t_rows, tm_rw)[0]

    hn1 = norm_fwd("norm_mix", h0, norm_mix_w)
    proj = {s: _mm("proj_" + s, hn1, w_seg[s], dep=rest[3]) for s in w_seg}

    def dn_post(sec, cv):
        s = _silu(cv)
        if sec < 2:
            s = s * lax.rsqrt(jnp.sum(s * s, axis=-1, keepdims=True) + EPS)
        if sec == 0:
            s = s * (HD ** -0.5)
        return s

    def dn_prep(sec, name):
        def fn(rows, own, last, wins, pars):
            return [jnp.where(valid(rows), dn_post(sec, _conv(wins[0], pars[0])), 0.0)], []
        wc = dnconv_f[:, sec * d:(sec + 1) * d]
        return _cv("dn_prep_" + name, fn, [(proj[name], _cj)], [(wc, _cj)], [(d, _cj, F32)], [], t_rows, dnh)[0]

    q_act, k_act, v_act = dn_prep(0, "q"), dn_prep(1, "k"), dn_prep(2, "v")

    lane = lambda: lax.broadcasted_iota(jnp.int32, (1, LANE), 1)

    def lanes_of(vec, off):
        return jnp.pad(vec.astype(F32), ((0, 0), (off, LANE - off - vec.shape[1])))

    gate_params = [lanes_of(dn_a_log, dnh), lanes_of(dn_dt_bias, dnh), lanes_of(m2_a_log, 2 * dnh), lanes_of(m2_dt_bias, 2 * dnh)]

    def gates(rows, sm, p_alog, p_dtb, p_malog, p_mdtb):
        ln = lane()
        is_b, is_g = ln < dnh, jnp.logical_and(ln >= dnh, ln < 2 * dnh)
        is_d = jnp.logical_and(ln >= 2 * dnh, ln < 2 * dnh + m2h)
        beta = jax.nn.sigmoid(sm)
        gdec = -jnp.exp(p_alog) * _softplus(sm + p_dtb)
        dt = _softplus(sm + p_mdtb)
        am = dt * (-jnp.exp(p_malog))
        ok = valid(rows)
        g1 = jnp.where(ok, jnp.where(is_b, beta, jnp.where(is_g, gdec, jnp.where(is_d, dt, 0.0))), 0.0)
        g2 = jnp.where(jnp.logical_and(ok, is_d), am, 0.0)
        return g1, g2

    gate_ins = [("r", proj["sm"], LANE, _c0)] + [("p", p, LANE, _c0) for p in gate_params]
    g1, g2 = _rw("gates", lambda rows, j, *a: gates(rows, *a), gate_ins,
                 [("r", LANE, LANE, _c0, F32), ("r", LANE, LANE, _c0, F32)], t_rows, tm_rw)

    def head_rows(cols, per):
        n = cols.shape[1]
        return cols.reshape(nc, CH, n // per, per).transpose(2, 0, 3, 1)

    def head_cols(rows_):
        ngrp, _, per, _ = rows_.shape
        return rows_.transpose(1, 3, 0, 2).reshape(t_rows, ngrp * per)

    beta_r, gdec_r = head_rows(g1[:, :dnh], hb), head_rows(g1[:, dnh:2 * dnh], hb)
    hpg = m2h // M2G
    dt_r, am_r = head_rows(g1[:, 2 * dnh:2 * dnh + m2h], hpg), head_rows(g2[:, 2 * dnh:2 * dnh + m2h], hpg)

    o_dn, dn_states, dn_tinv = _gdn_fwd(q_act, k_act, v_act, beta_r, gdec_r, hb)

    def dn_out(o, z, w):
        outs = []
        for h in range(dnh):
            sl = slice(h * HD, (h + 1) * HD)
            outs.append(_rms(o[:, sl], w) * _silu(z[:, sl]))
        return jnp.concatenate(outs, axis=1)

    mixed_dn = _rw("dn_out", lambda rows, j, o, z, w: (dn_out(o, z, w),),
                   [("r", o_dn, d, _c0), ("r", proj["z"], d, _c0), ("p", dn_norm_w, HD, _c0)], [("r", d, d, _c0, MXU)],
                   t_rows, tm_rw)[0]

    def m2_prep(rows, own, last, wins, pars):
        return [jnp.where(valid(rows), _silu(_conv(wins[0], pars[0]) + pars[1][0]), 0.0)], []

    xbc_act = _cv("m2_prep", m2_prep, [(proj["xbc"], _cj)], [(m2conv_f, _cj), (m2_conv_b, _cj)], [(xbc_w, _cj, F32)], [],
                  t_rows, xbc_w // LANE)[0]
    y_ssd, m2_states = _ssd_fwd(xbc_act, dt_r, am_r, d)

    d_lanes = jnp.repeat(m2_d.astype(F32), M2P, axis=1)
    gw = d // M2G

    def m2_out(ys, xs, z, dl, nw):
        yv = (ys + dl * xs) * _silu(z)
        outs = []
        for gi in range(M2G):
            sl = slice(gi * gw, (gi + 1) * gw)
            outs.append(_rms(yv[:, sl], nw[:, sl]))
        return jnp.concatenate(outs, axis=1)

    m2_out_ins = [("r", y_ssd, d, _c0), ("r", xbc_act, d, _c0), ("r", proj["m2z"], d, _c0), ("p", d_lanes, d, _c0),
                  ("p", m2_norm_w, d, _c0)]
    mixed_m2 = _rw("m2_out", lambda rows, j, *a: (m2_out(*a),), m2_out_ins, [("r", d, d, _c0, MXU)], t_rows, tm_rw)[0]

    mixed = jnp.concatenate([mixed_dn, mixed_m2], axis=1)
    g_wout, g_wup, g_wdown = _exchange_wait("gather_rest_wait", rest, mixed)
    wout = _from_shards(g_wout, 0)
    wup = _from_shards(g_wup, 1)
    wdown = _from_shards(g_wdown, 0)
    wup_g, wup_v = wup[:, :dff], wup[:, dff:]
    h1 = _mm("out_proj", mixed, wout, add=h0)
    hn2 = norm_fwd("norm_ffn", h1, norm_ffn_w)
    u_g, u_v = _mm("ffn_up_g", hn2, wup_g), _mm("ffn_up_v", hn2, wup_v)
    fc_g, fc_v = ffnconv_f[:, :dff], ffnconv_f[:, dff:]

    def ffn_act(rows, own, last, wins, pars):
        return [jnp.where(valid(rows), _silu(_conv(wins[0], pars[0])) * _conv(wins[1], pars[1]), 0.0)], []

    act = _cv("ffn_act", ffn_act, [(u_g, _cj), (u_v, _cj)], [(fc_g, _cj), (fc_v, _cj)], [(dff, _cj, MXU)], [],
              t_rows, dff // LANE)[0]
    h2 = _mm("ffn_down", act, wdown, add=h1, tk=1408)

    def loss_fn(hv, wf, tgt, rows):
        err = jnp.where(rows >= CH, _rms(hv, wf) - tgt, 0.0)
        return 0.5 * jnp.sum(jnp.mean(err * err, axis=-1, keepdims=True), axis=0, keepdims=True)

    def final(rows, j, hv, wf, tgt):
        loss, vjp = jax.vjp(lambda a, b: loss_fn(a, b, tgt, rows), hv, wf)
        dh, dw = vjp(jnp.ones((1, 1), F32))
        return dh, dh, dw, jnp.broadcast_to(loss, (1, LANE))

    wf2 = norm_final_w.reshape(1, d)
    dh2, dh2_m, d_wf, loss_part = _rw(
        "loss_head", final, [("r", h2, d, _c0), ("p", wf2, d, _c0), ("r", loss_target[0], d, _c0, lambda i: jnp.maximum(i - 1, 0))],
        [("r", d, d, _c0, F32), ("r", d, d, _c0, MXU), ("p", 1, d, d, _c0), ("p", 1, LANE, LANE, _c0)], t_rows, CH)
    loss = lax.psum(loss_part[0, 0], MESH_AXES)

    d_act = _mm("d_act", dh2_m, wdown, tb=True)
    gw_down = _mm("gw_down", act, dh2_m, ta=True, tm=1408, tn=1024, tk=1040, out_dtype=WIRE)
    x_down = _exchange_start("grad_down_start", [], [_to_shards(gw_down, 0).astype(WIRE)], gw_down)

    def t_end(last):
        return t_rows if last else None

    def ffn_act_bwd(rows, own, last, wins, pars):
        (ug, uv, da), (wg, wv) = wins, pars
        cg, cv = _conv(ug, wg), _conv(uv, wv)
        _, vjp = jax.vjp(lambda a, b: _silu(a) * b, cg, cv)
        dcg, dcv = vjp(jnp.where(valid(rows), da, 0.0))
        return ([_conv_t(dcg, wg, rows, t_end(last)), _conv_t(dcv, wv, rows, t_end(last))],
                [_conv_w(jnp.where(own, dcg, 0.0), ug, len(wg)), _conv_w(jnp.where(own, dcv, 0.0), uv, len(wv))])

    kf = fc_g.shape[0]
    du_g, du_v, g_fc_g, g_fc_v = _cv(
        "ffn_act_bwd", ffn_act_bwd, [(u_g, _cj), (u_v, _cj), (d_act, _cj)], [(fc_g, _cj), (fc_v, _cj)],
        [(dff, _cj, MXU), (dff, _cj, MXU)], [(kf, dff, _cj), (kf, dff, _cj)], t_rows, dff // LANE)
    gw_up_g = _mm("gw_up_g", hn2, du_g, ta=True, tm=1024, tn=1408, tk=1040, out_dtype=WIRE, dep=x_down[3])
    gw_up_v = _mm("gw_up_v", hn2, du_v, ta=True, tm=1024, tn=1408, tk=1040, out_dtype=WIRE)
    gw_up_full = jnp.concatenate([gw_up_g, gw_up_v], axis=1)
    x_up = _exchange_start("grad_up_start", [], [_to_shards(gw_up_full, 1).astype(WIRE)], gw_up_full)
    d_hn2 = _mm("d_hn2_v", du_v, wup_v, tb=True, tk=1408, dep=x_up[3],
                add=_mm("d_hn2_g", du_g, wup_g, tb=True, tk=1408, dep=x_up[3]))

    def norm_bwd(name, h, w, dy, dres):
        def fn(rows, j, hv, wv, dyv, dr):
            _, vjp = jax.vjp(_rms, hv, wv)
            dh, dw = vjp(dyv)
            dh = dh + dr
            return dh, dh, dw
        return _rw(name, fn, [("r", h, d, _c0), ("p", w, d, _c0), ("r", dy, d, _c0), ("r", dres, d, _c0)],
                   [("r", d, d, _c0, F32), ("r", d, d, _c0, MXU), ("p", 1, d, d, _c0)], t_rows, tm_rw)

    dh1, dh1_m, g_norm_ffn = norm_bwd("norm_ffn_bwd", h1, norm_ffn_w, d_hn2, dh2)

    gw_out = _mm("gw_out", mixed, dh1_m, ta=True, tm=1024, tn=1024, tk=1040, out_dtype=WIRE)
    x_out = _exchange_start("grad_out_start", [], [_to_shards(gw_out, 0).astype(WIRE)], gw_out)
    d_mixed = _mm("d_mixed", dh1_m, wout, tb=True, dep=x_out[3])

    def m2_out_bwd(rows, j, ys, xs, z, dl, nw, dy):
        _, vjp = jax.vjp(m2_out, ys, xs, z, dl, nw)
        return vjp(dy)

    dy_ssd, dxs_skip, d_m2z, g_d_lanes, g_m2_norm = _rw(
        "m2_out_bwd", m2_out_bwd, m2_out_ins + [("r", d_mixed, d, lambda j: 1)],
        [("r", d, d, _c0, F32), ("r", d, d, _c0, F32), ("r", d, d, _c0, MXU), ("p", 1, d, d, _c0), ("p", 1, d, d, _c0)],
        t_rows, _pick(t_rows, 208, 16))

    def fold_heads(vec_ref, out_ref):
        r = lax.broadcasted_iota(jnp.int32, (d, LANE), 0)
        c = lax.broadcasted_iota(jnp.int32, (d, LANE), 1)
        out_ref[...] = _dgh(vec_ref[...], jnp.where(jnp.logical_and(r >= c * M2P, r < (c + 1) * M2P), 1.0, 0.0), 1, 0)

    g_m2_d = pl.pallas_call(fold_heads, name="fold_m2_d", out_shape=jax.ShapeDtypeStruct((1, LANE), F32))(g_d_lanes)

    dxs, db_ssd, dc_ssd, ddt_r, dam_r = _ssd_bwd(xbc_act, dt_r, am_r, m2_states, dy_ssd, d)


    def m2_prep_bwd(rows, own, last, wins, pars):
        (p, *ds), (w, b) = wins, pars
        _, vjp = jax.vjp(_silu, _conv(p, w) + b[0])
        dpre, = vjp(jnp.where(valid(rows), functools.reduce(lambda a_, b_: a_ + b_, ds), 0.0))
        dpre_own = jnp.where(own, dpre, 0.0)
        return [_conv_t(dpre, w, rows, t_end(last))], [_conv_w(dpre_own, p, len(w)), [jnp.sum(dpre_own, axis=0, keepdims=True)]]

    def m2_prep_bwd_call(name, off, width, d_arrs):
        at = lambda j, blk0=off // LANE: blk0 + j
        return _cv(name, m2_prep_bwd, [(proj["xbc"], at)] + [(a, _cj) for a in d_arrs], [(m2conv_f, at), (m2_conv_b, at)],
                   [(width, _cj, MXU)], [(m2conv_f.shape[0], width, _cj), (1, width, _cj)], t_rows, width // LANE)

    dp_xs, gcw_xs, gcb_xs = m2_prep_bwd_call("m2_prep_bwd_x", 0, d, [dxs, dxs_skip])
    dp_b, gcw_b, gcb_b = m2_prep_bwd_call("m2_prep_bwd_b", d, M2G * NST, [db_ssd])
    dp_c, gcw_c, gcb_c = m2_prep_bwd_call("m2_prep_bwd_c", d + M2G * NST, M2G * NST, [dc_ssd])
    d_pxbc = jnp.concatenate([dp_xs, dp_b, dp_c], axis=1)
    g_m2_conv = jnp.concatenate([gcw_xs, gcw_b, gcw_c], axis=1)
    g_m2_conv_b = jnp.concatenate([gcb_xs, gcb_b, gcb_c], axis=1)

    def dn_out_bwd(rows, j, o, z, w, dy):
        _, vjp = jax.vjp(dn_out, o, z, w)
        return vjp(dy)

    d_o, d_z, g_dn_norm = _rw(
        "dn_out_bwd", dn_out_bwd,
        [("r", o_dn, d, _c0), ("r", proj["z"], d, _c0), ("p", dn_norm_w, HD, _c0), ("r", d_mixed, d, _c0)],
        [("r", d, d, _c0, F32), ("r", d, d, _c0, MXU), ("p", 1, HD, HD, _c0)], t_rows, _pick(t_rows, 208, 16))

    dq, dk, dv, dbeta_r, dgdec_r = _gdn_bwd(q_act, k_act, v_act, beta_r, gdec_r, dn_states, dn_tinv, d_o, hb)

    def dn_prep_bwd(sec, name, dact):
        def fn(rows, own, last, wins, pars):
            (p, da), (w,) = wins, pars
            _, vjp = jax.vjp(functools.partial(dn_post, sec), _conv(p, w))
            dcv, = vjp(jnp.where(valid(rows), da, 0.0))
            return [_conv_t(dcv, w, rows, t_end(last))], [_conv_w(jnp.where(own, dcv, 0.0), p, len(w))]
        wc = dnconv_f[:, sec * d:(sec + 1) * d]
        return _cv("dn_prep_bwd_" + name, fn, [(proj[name], _cj), (dact, _cj)], [(wc, _cj)], [(d, _cj, MXU)],
                   [(wc.shape[0], d, _cj)], t_rows, dnh)

    (dp_q, gcw_q), (dp_k, gcw_k), (dp_v, gcw_v) = dn_prep_bwd(0, "q", dq), dn_prep_bwd(1, "k", dk), dn_prep_bwd(2, "v", dv)
    g_dn_conv = jnp.concatenate([gcw_q, gcw_k, gcw_v], axis=1)

    zpad = jnp.zeros((t_rows, LANE - 2 * dnh - m2h), F32)
    dg1 = jnp.concatenate([head_cols(dbeta_r), head_cols(dgdec_r), head_cols(ddt_r), zpad], axis=1)
    dg2 = jnp.concatenate([jnp.zeros((t_rows, 2 * dnh), F32), head_cols(dam_r), zpad], axis=1)

    def gates_bwd(rows, j, sm, pa, pb, pc, pd, d1, d2):
        _, vjp = jax.vjp(lambda *a: gates(rows, *a), sm, pa, pb, pc, pd)
        return vjp((d1, d2))

    dp_sm, g_pa, g_pb, g_pc, g_pd = _rw(
        "gates_bwd", gates_bwd, gate_ins + [("r", dg1, LANE, _c0), ("r", dg2, LANE, _c0)],
        [("r", LANE, LANE, _c0, MXU)] + [("p", 1, LANE, LANE, _c0)] * 4, t_rows, tm_rw)

    dseg = {"q": dp_q, "k": dp_k, "v": dp_v, "z": d_z, "m2z": d_m2z, "xbc": d_pxbc, "sm": dp_sm}
    gw_seg = {s: _mm("gw_in_" + s, hn1, dseg[s], ta=True, tm=1024, tn=1024, tk=1040, out_dtype=WIRE) for s in dseg}
    gsm = gw_seg["sm"]
    gw_in_full = jnp.concatenate([gw_seg["q"], gw_seg["k"], gw_seg["v"], gw_seg["z"], gsm[:, :2 * dnh], gw_seg["m2z"],
                                  gw_seg["xbc"], gsm[:, 2 * dnh:2 * dnh + m2h]], axis=1)
    x_in = _exchange_start("grad_in_start", [], [_to_shards(gw_in_full, 1).astype(WIRE)], gw_in_full)
    d_hn1 = None
    for s in dseg:
        d_hn1 = _mm("d_hn1_" + s, dseg[s], w_seg[s], tb=True, tk=2048, add=d_hn1, dep=x_in[3])
    dh0, _, g_norm_mix = norm_bwd("norm_mix_bwd", h0, norm_mix_w, d_hn1, dh1)

    g_ffn_conv = jnp.concatenate([g_fc_g, g_fc_v], axis=1)
    small_parts = [_to_shards(dh0[PADR:CH], 1), _to_shards(g_dn_conv, 1), _to_shards(g_m2_conv, 1), _to_shards(g_ffn_conv, 1)]
    small_scatter = jnp.stack([_pack([p[k] for p in small_parts]) for k in range(NDEV)])

    rep_names = ["norm_mix_w", "dn_a_log", "dn_dt_bias", "dn_norm_w", "m2_conv_b", "m2_a_log", "m2_dt_bias", "m2_d",
                 "m2_norm_w", "norm_ffn_w", "norm_final_w"]
    rep_grads = [g_norm_mix, g_pa[:, dnh:2 * dnh], g_pb[:, dnh:2 * dnh], g_dn_norm, g_m2_conv_b, g_pc[:, 2 * dnh:2 * dnh + m2h],
                 g_pd[:, 2 * dnh:2 * dnh + m2h], g_m2_d[:, :m2h], g_m2_norm, g_norm_ffn, d_wf.reshape(d)]

    weights = dict(meta_tokens=meta_tokens, norm_mix_w=norm_mix_w, w_in=w_in, dn_conv_w=dn_conv_w, dn_a_log=dn_a_log,
                   dn_dt_bias=dn_dt_bias, dn_norm_w=dn_norm_w, m2_conv_w=m2_conv_w, m2_conv_b=m2_conv_b, m2_a_log=m2_a_log,
                   m2_dt_bias=m2_dt_bias, m2_d=m2_d, m2_norm_w=m2_norm_w, w_out=w_out, norm_ffn_w=norm_ffn_w, ffn_up=ffn_up,
                   ffn_conv_w=ffn_conv_w, ffn_down=ffn_down, norm_final_w=norm_final_w)
    mom1 = dict(meta_tokens=m_meta_tokens, norm_mix_w=m_norm_mix_w, w_in=m_w_in, dn_conv_w=m_dn_conv_w, dn_a_log=m_dn_a_log,
                dn_dt_bias=m_dn_dt_bias, dn_norm_w=m_dn_norm_w, m2_conv_w=m_m2_conv_w, m2_conv_b=m_m2_conv_b,
                m2_a_log=m_m2_a_log, m2_dt_bias=m_m2_dt_bias, m2_d=m_m2_d, m2_norm_w=m_m2_norm_w, w_out=m_w_out,
                norm_ffn_w=m_norm_ffn_w, ffn_up=m_ffn_up, ffn_conv_w=m_ffn_conv_w, ffn_down=m_ffn_down,
                norm_final_w=m_norm_final_w)
    mom2 = dict(meta_tokens=v_meta_tokens, norm_mix_w=v_norm_mix_w, w_in=v_w_in, dn_conv_w=v_dn_conv_w, dn_a_log=v_dn_a_log,
                dn_dt_bias=v_dn_dt_bias, dn_norm_w=v_dn_norm_w, m2_conv_w=v_m2_conv_w, m2_conv_b=v_m2_conv_b,
                m2_a_log=v_m2_a_log, m2_dt_bias=v_m2_dt_bias, m2_d=v_m2_d, m2_norm_w=v_m2_norm_w, w_out=v_w_out,
                norm_ffn_w=v_norm_ffn_w, ffn_up=v_ffn_up, ffn_conv_w=v_ffn_conv_w, ffn_down=v_ffn_down,
                norm_final_w=v_norm_final_w)
    res = {}

    def adam_big(name, started, after):
        staged, = _exchange_wait("grad_" + name + "_wait", started, after)
        outs = _adamw("adamw_" + name, staged, weights[name][0], mom1[name][0], mom2[name][0])
        res[name] = tuple(o[None] for o in outs)
        return outs[1]

    done = adam_big("ffn_down", x_down, dh0)
    done = adam_big("ffn_up", x_up, done)
    done = adam_big("w_out", x_out, done)
    st_rep, st_small = _exchange("exchange_small_grads", [_pack(rep_grads)], [small_scatter], done)
    adam_big("w_in", x_in, st_small)

    def adam_packed(label, staged, names):
        shapes = [weights[nm].shape for nm in names]
        outs = _adamw(label, staged, *[_pack([src[nm] for nm in names]) for src in (weights, mom1, mom2)])
        unpacked = [_unpack(o, shapes) for o in outs]
        for i, nm in enumerate(names):
            res[nm] = tuple(u[i] for u in unpacked)

    adam_packed("adamw_small_sharded", st_small, ["meta_tokens", "dn_conv_w", "m2_conv_w", "ffn_conv_w"])
    adam_packed("adamw_replicated", st_rep, rep_names)

    order = list(weights)
    grad_x = dh0[CH:][None]
    return (loss, grad_x, *[res[nm][0] for nm in order], *[res[nm][1] for nm in order], *[res[nm][2] for nm in order],
            *[res[nm][3] for nm in order])
```

```python
import functools

import jax
import jax.numpy as jnp
from jax import lax
from jax.experimental import pallas as pl
from jax.experimental.pallas import tpu as pltpu

F32 = jnp.float32
MXU = jnp.bfloat16
WIRE = jnp.bfloat16
HI = lax.Precision.HIGH

NDEV = 8
CH = 64
NMETA = 16
PADR = CH - NMETA
EPS = 1e-6
HD = 128
M2P = 64
M2G = 4
NST = 128
LANE = 128

ADAM_LR, ADAM_B1, ADAM_B2, ADAM_EPS, ADAM_WD, ADAM_STEP = 0.001, 0.9, 0.999, 1e-08, 0.01, 10

MESH_AXES = ("x", "y", "c")


def _pick(n, target, mult=16):
    best = None
    for t in range(mult, min(n, target) + 1, mult):
        if n % t == 0:
            best = t
    return best if best is not None else n


def _dg(a, b, ca, cb):
    return lax.dot_general(a.astype(MXU), b.astype(MXU), (((ca,), (cb,)), ((), ())), preferred_element_type=F32)


def _dgh(a, b, ca, cb):
    return lax.dot_general(a, b, (((ca,), (cb,)), ((), ())), precision=HI, preferred_element_type=F32)


def _silu(x):
    return x * jax.nn.sigmoid(x)


def _softplus(x):
    return jnp.maximum(x, 0.0) + jnp.log1p(jnp.exp(-jnp.abs(x)))


def _rms(x, w):
    return x * lax.rsqrt(jnp.mean(x * x, axis=-1, keepdims=True) + EPS) * w


def _cparams(sem, vmem_mb):
    return pltpu.CompilerParams(dimension_semantics=sem, vmem_limit_bytes=vmem_mb << 20)


def _mm(name, a, b, *, ta=False, tb=False, add=None, out_dtype=F32, tm=1024, tn=1024, tk=2048, dep=None):
    m, kdim = (a.shape[1], a.shape[0]) if ta else a.shape
    n = b.shape[0] if tb else b.shape[1]
    tm = _pick(m, tm, 128 if ta else 16)
    tn = _pick(n, tn, 128)
    tk = _pick(kdim, tk, 16 if (ta and not tb) else 128)
    nk = kdim // tk
    ca, cb = (0 if ta else 1), (1 if tb else 0)

    def body(*refs):
        a_ref, b_ref = refs[0], refs[1]
        add_ref = refs[2] if add is not None else None
        if nk == 1:
            r = _dg(a_ref[...], b_ref[...], ca, cb)
            if add_ref is not None:
                r = r + add_ref[...].astype(F32)
            refs[-1][...] = r.astype(refs[-1].dtype)
            return
        o_ref, acc = refs[-2], refs[-1]
        k = pl.program_id(2)

        @pl.when(k == 0)
        def _():
            acc[...] = jnp.zeros_like(acc)

        acc[...] += _dg(a_ref[...], b_ref[...], ca, cb)

        @pl.when(k == nk - 1)
        def _():
            r = acc[...]
            if add_ref is not None:
                r = r + add_ref[...].astype(F32)
            o_ref[...] = r.astype(o_ref.dtype)

    a_spec = pl.BlockSpec((tk, tm), lambda i, j, k: (k, i)) if ta else pl.BlockSpec((tm, tk), lambda i, j, k: (i, k))
    b_spec = pl.BlockSpec((tn, tk), lambda i, j, k: (j, k)) if tb else pl.BlockSpec((tk, tn), lambda i, j, k: (k, j))
    in_specs, ops = [a_spec, b_spec], [a, b]
    if add is not None:
        in_specs.append(pl.BlockSpec((tm, tn), lambda i, j, k: (i, j)))
        ops.append(add)
    if dep is not None:
        in_specs.append(pl.BlockSpec((8, LANE), lambda i, j, k: (0, 0)))
        ops.append(dep)
    return pl.pallas_call(
        body, name=name, grid=(m // tm, n // tn, nk), in_specs=in_specs,
        out_specs=pl.BlockSpec((tm, tn), lambda i, j, k: (i, j)),
        out_shape=jax.ShapeDtypeStruct((m, n), out_dtype),
        scratch_shapes=[pltpu.VMEM((tm, tn), F32)] if nk > 1 else [],
        compiler_params=_cparams(("parallel", "parallel", "arbitrary"), 48),
    )(*ops)


def _rw(name, fn, ins, outs, nrows, tm, ncol=1, vmem_mb=48):
    nrow = nrows // tm
    sub = tm
    in_specs, ops = [], []
    for spec in ins:
        kind, arr, bw, cj = spec[:4]
        ops.append(arr)
        if kind == "r":
            ri = spec[4] if len(spec) > 4 else (lambda i: i)
            in_specs.append(pl.BlockSpec((tm, bw), lambda j, i, cj=cj, ri=ri: (ri(i), cj(j))))
        else:
            in_specs.append(pl.BlockSpec((arr.shape[0], bw), lambda j, i, cj=cj: (0, cj(j))))
    out_shape, out_specs = [], []
    for o in outs:
        if o[0] == "r":
            _, width, bw, cj, dt = o
            out_shape.append(jax.ShapeDtypeStruct((nrows, width), dt))
            out_specs.append(pl.BlockSpec((tm, bw), lambda j, i, cj=cj: (i, cj(j))))
        else:
            _, rows, width, bw, cj = o
            out_shape.append(jax.ShapeDtypeStruct((rows, width), F32))
            out_specs.append(pl.BlockSpec((rows, bw), lambda j, i, cj=cj: (0, cj(j))))
    n_in = len(ins)

    def body(*refs):
        j, i = pl.program_id(0), pl.program_id(1)
        in_refs, out_refs = refs[:n_in], refs[n_in:]
        pars = [ref[...] if spec[0] == "p" else None for spec, ref in zip(ins, in_refs)]

        def one(r0, nr):
            rows = i * tm + r0 + lax.broadcasted_iota(jnp.int32, (nr, 1), 0)
            vals = [par if spec[0] == "p" else ref[pl.ds(r0, nr), :] for spec, ref, par in zip(ins, in_refs, pars)]
            parts = []
            for o, val, ref in zip(outs, fn(rows, j, *vals), out_refs):
                if o[0] == "r":
                    ref[pl.ds(r0, nr), :] = val.astype(ref.dtype)
                else:
                    parts.append(val)
            return parts

        if sub >= tm:
            parts = one(0, tm)
        else:
            zero = [jnp.zeros((1, o[3]), F32) for o in outs if o[0] == "p"]
            parts = lax.fori_loop(
                0, tm // sub, lambda s, acc: [a + b for a, b in zip(acc, one(pl.multiple_of(s * sub, sub), sub))], zero)
        for ref, val in zip([r for o, r in zip(outs, out_refs) if o[0] == "p"], parts):
            @pl.when(i == 0)
            def _(ref=ref):
                ref[...] = jnp.zeros_like(ref)

            ref[...] += val

    return pl.pallas_call(
        body, name=name, grid=(ncol, nrow), in_specs=in_specs, out_specs=out_specs, out_shape=out_shape,
        compiler_params=_cparams(("parallel", "arbitrary"), vmem_mb),
    )(*ops)


def _c0(j):
    return 0


def _cj(j):
    return j


def _shift(x, s):
    if s == 0:
        return x
    return pltpu.roll(x, s % x.shape[0], 0)


def _conv(x, w):
    k = len(w)
    return functools.reduce(lambda a, b: a + b, [w[j] * _shift(x, k - 1 - j) for j in range(k)])


def _conv_t(dy, w, rows, t_end):
    k = len(w)
    terms = []
    for j in range(k):
        s = k - 1 - j
        v = _shift(dy, -s)
        if t_end is not None and s > 0:
            v = jnp.where(rows + s < t_end, v, 0.0)
        terms.append(w[j] * v)
    return functools.reduce(lambda a, b: a + b, terms)


def _conv_w(dy, x, k):
    return [jnp.sum(dy * _shift(x, k - 1 - j), axis=0, keepdims=True) for j in range(k)]


HALO = 8


def _cv(name, fn, row_ins, par_ins, row_outs, par_outs, nrows, ncol, chunk=None):
    whole = chunk is None
    chunk = nrows if whole else chunk
    n_chunks = nrows // chunk
    assert nrows % chunk == 0 and (whole or n_chunks >= 3)
    n_ri, n_pi, n_ro = len(row_ins), len(par_ins), len(row_outs)

    def body(*refs):
        rin, pin = refs[:n_ri], refs[n_ri:n_ri + n_pi]
        rout, pout = refs[n_ri + n_pi:n_ri + n_pi + n_ro], refs[n_ri + n_pi + n_ro:]
        pars = [[p[pl.ds(r, 1), :] for r in range(p.shape[0])] for p in pin]

        def run(r0, top, bot, last):
            wlen = top + chunk + bot
            w0 = r0 - top if isinstance(r0, int) else pl.multiple_of(r0 - top, HALO)
            local = lax.broadcasted_iota(jnp.int32, (wlen, 1), 0)
            own = jnp.logical_and(local >= top, local < top + chunk)
            outs, parts = fn(w0 + local, own, last, [ref[pl.ds(w0, wlen), :] for ref in rin], pars)
            for ref, val in zip(rout, outs):
                ref[pl.ds(r0, chunk), :] = val[top:top + chunk].astype(ref.dtype)
            return parts

        def add(acc, parts):
            return [[a + b for a, b in zip(ra, rb)] for ra, rb in zip(acc, parts)]

        if whole:
            acc = run(0, 0, 0, True)
        else:
            acc = run(0, 0, HALO, False)
            acc = lax.fori_loop(1, n_chunks - 1,
                                lambda i, a: add(a, run(pl.multiple_of(i * chunk, chunk), HALO, HALO, False)), acc)
            acc = add(acc, run(nrows - chunk, HALO, 0, True))
        for ref, prow in zip(pout, acc):
            for r, v in enumerate(prow):
                ref[pl.ds(r, 1), :] = v

    in_specs = [pl.BlockSpec((nrows, LANE), lambda j, cj=cj: (0, cj(j))) for _, cj in row_ins]
    in_specs += [pl.BlockSpec((a.shape[0], LANE), lambda j, cj=cj: (0, cj(j))) for a, cj in par_ins]
    out_specs = [pl.BlockSpec((nrows, LANE), lambda j, cj=cj: (0, cj(j))) for _, cj, _ in row_outs]
    out_specs += [pl.BlockSpec((k, LANE), lambda j, cj=cj: (0, cj(j))) for k, _, cj in par_outs]
    out_shape = [jax.ShapeDtypeStruct((nrows, width), dt) for width, _, dt in row_outs]
    out_shape += [jax.ShapeDtypeStruct((k, width), F32) for k, width, _ in par_outs]
    return pl.pallas_call(
        body, name=name, grid=(ncol,), in_specs=in_specs, out_specs=out_specs, out_shape=out_shape,
        compiler_params=_cparams(("parallel",), 48),
    )(*[a for a, _ in row_ins], *[a for a, _ in par_ins])


def _tri():
    r = lax.broadcasted_iota(jnp.int32, (CH, CH), 0)
    c = lax.broadcasted_iota(jnp.int32, (CH, CH), 1)
    return r, c


def _col(row):
    r, c = _tri()
    return jnp.sum(jnp.where(r == c, row, 0.0), axis=1, keepdims=True)


def _cumsum_rc(g_r):
    r, c = _tri()
    g_c = _col(g_r)
    cs_r = jnp.sum(jnp.where(r <= c, g_c, 0.0), axis=0, keepdims=True)
    cs_c = jnp.sum(jnp.where(c <= r, g_r, 0.0), axis=1, keepdims=True)
    return cs_r, cs_c


def _decay(cs_r, cs_c):
    r, c = _tri()
    return jnp.exp(jnp.where(c <= r, cs_c - cs_r, -jnp.inf))


def _gdn_a(ks, betas, gs):
    r, c = _tri()
    cs = [_cumsum_rc(g) for g in gs]
    kk = [_dg(k, k, 1, 1) for k in ks]
    return [jnp.where(c < r, _col(b) * kki * _decay(*csi), 0.0) for b, kki, csi in zip(betas, kk, cs)]


def _neumann(a_list):
    r, c = _tri()
    xs = [jnp.where(r == c, 1.0, 0.0) - a for a in a_list]
    ps = list(a_list)
    n = 2
    while n < CH:
        ps = [_dgh(p, p, 1, 0) for p in ps]
        xs = [x + _dgh(x, p, 1, 0) for x, p in zip(xs, ps)]
        n *= 2
    return xs


def _gdn_rest(ss, qs, ks, vs, betas, gs, ts):
    n = range(len(ss))
    cs = [_cumsum_rc(g) for g in gs]
    dm = [_decay(*csi) for csi in cs]
    ecs = [jnp.exp(csi[1]) for csi in cs]
    bc = [_col(b) for b in betas]
    u = [_dgh(ts[i], vs[i] * bc[i], 1, 0) for i in n]
    w = [_dgh(ts[i], ks[i] * (bc[i] * ecs[i]), 1, 0) for i in n]
    ws = [_dg(w[i], ss[i], 1, 0) for i in n]
    v_new = [u[i] - ws[i] for i in n]
    qk = [_dg(qs[i], ks[i], 1, 1) * dm[i] for i in n]
    o_in = [_dg(qs[i] * ecs[i], ss[i], 1, 0) for i in n]
    o = [o_in[i] + _dg(qk[i], v_new[i], 1, 0) for i in n]
    g_last = [jnp.sum(g, axis=1, keepdims=True) for g in gs]
    s_new = [ss[i] * jnp.exp(g_last[i]) + _dg(ks[i] * jnp.exp(g_last[i] - cs[i][1]), v_new[i], 0, 0) for i in n]
    return s_new, o


def _gdn_fwd(q, k, v, beta, g, hb):
    t_rows, d = q.shape
    nc, ng, w = t_rows // CH, d // (HD * hb), HD * hb
    sls = [slice(h * HD, (h + 1) * HD) for h in range(hb)]

    def body(q_ref, k_ref, v_ref, b_ref, g_ref, o_ref, ss_ref, ts_ref, s_scr):
        c = pl.program_id(1)

        @pl.when(c == 0)
        def _():
            s_scr[...] = jnp.zeros_like(s_scr)

        qs, ks, vs = ([ref[:, sl] for sl in sls] for ref in (q_ref, k_ref, v_ref))
        br = [b_ref[0, 0, pl.ds(h, 1), :] for h in range(hb)]
        gr = [g_ref[0, 0, pl.ds(h, 1), :] for h in range(hb)]
        s0 = [s_scr[h] for h in range(hb)]
        tm = _neumann(_gdn_a(ks, br, gr))
        s1, o = _gdn_rest(s0, qs, ks, vs, br, gr, tm)
        for h in range(hb):
            ss_ref[0, 0, h] = s0[h]
            ts_ref[0, 0, h] = tm[h]
            o_ref[:, sls[h]] = o[h]
            s_scr[h] = s1[h]

    blk = pl.BlockSpec((CH, w), lambda n, c: (c, n))
    row = pl.BlockSpec((1, 1, hb, CH), lambda n, c: (n, c, 0, 0))
    return pl.pallas_call(
        body, name="gdn_fwd", grid=(ng, nc), in_specs=[blk, blk, blk, row, row],
        out_specs=[blk, pl.BlockSpec((1, 1, hb, HD, HD), lambda n, c: (n, c, 0, 0, 0)),
                   pl.BlockSpec((1, 1, hb, CH, CH), lambda n, c: (n, c, 0, 0, 0))],
        out_shape=[jax.ShapeDtypeStruct((t_rows, d), F32), jax.ShapeDtypeStruct((ng, nc, hb, HD, HD), F32),
                   jax.ShapeDtypeStruct((ng, nc, hb, CH, CH), F32)],
        scratch_shapes=[pltpu.VMEM((hb, HD, HD), F32)],
        compiler_params=_cparams(("parallel", "arbitrary"), 32),
    )(q, k, v, beta, g)


def _gdn_bwd(q, k, v, beta, g, ss, ts, do, hb):
    t_rows, d = q.shape
    nc, ng, w = t_rows // CH, d // (HD * hb), HD * hb
    sls = [slice(h * HD, (h + 1) * HD) for h in range(hb)]

    def body(q_ref, k_ref, v_ref, b_ref, g_ref, ss_ref, ts_ref, do_ref, dq_ref, dk_ref, dv_ref, db_ref, dg_ref, ds_scr):
        cr = pl.program_id(1)

        @pl.when(cr == 0)
        def _():
            ds_scr[...] = jnp.zeros_like(ds_scr)

        first = cr == nc - 1
        rowi = lax.broadcasted_iota(jnp.int32, (CH, 1), 0)
        lani = lax.broadcasted_iota(jnp.int32, (1, CH), 1)
        keep_c = jnp.logical_or(jnp.logical_not(first), rowi >= PADR)
        keep_r = jnp.logical_or(jnp.logical_not(first), lani >= PADR)
        hs = range(hb)
        qs, ks, vs, dos = ([ref[:, sl] for sl in sls] for ref in (q_ref, k_ref, v_ref, do_ref))
        br = [b_ref[0, 0, pl.ds(h, 1), :] for h in hs]
        gr = [g_ref[0, 0, pl.ds(h, 1), :] for h in hs]
        tm = [ts_ref[0, 0, h] for h in hs]
        _, vjp_rest = jax.vjp(_gdn_rest, [ss_ref[0, 0, h] for h in hs], qs, ks, vs, br, gr, tm)
        ds0, dq, dk, dv, db, dg, dt = vjp_rest(([ds_scr[h] for h in hs], dos))
        dtt = [_dgh(dt[h], tm[h], 1, 1) for h in hs]
        da = [-_dgh(tm[h], dtt[h], 0, 0) for h in hs]
        _, vjp_a = jax.vjp(_gdn_a, ks, br, gr)
        dk2, db2, dg2 = vjp_a(da)
        for h in hs:
            ds_scr[h] = ds0[h]
            dq_ref[:, sls[h]] = jnp.where(keep_c, dq[h], 0.0)
            dk_ref[:, sls[h]] = jnp.where(keep_c, dk[h] + dk2[h], 0.0)
            dv_ref[:, sls[h]] = jnp.where(keep_c, dv[h], 0.0)
            db_ref[0, 0, pl.ds(h, 1), :] = jnp.where(keep_r, db[h] + db2[h], 0.0)
            dg_ref[0, 0, pl.ds(h, 1), :] = jnp.where(keep_r, dg[h] + dg2[h], 0.0)

    blk = pl.BlockSpec((CH, w), lambda n, c: (nc - 1 - c, n))
    row = pl.BlockSpec((1, 1, hb, CH), lambda n, c: (n, nc - 1 - c, 0, 0))
    return pl.pallas_call(
        body, name="gdn_bwd", grid=(ng, nc),
        in_specs=[blk, blk, blk, row, row, pl.BlockSpec((1, 1, hb, HD, HD), lambda n, c: (n, nc - 1 - c, 0, 0, 0)),
                  pl.BlockSpec((1, 1, hb, CH, CH), lambda n, c: (n, nc - 1 - c, 0, 0, 0)), blk],
        out_specs=[blk, blk, blk, row, row],
        out_shape=[jax.ShapeDtypeStruct((t_rows, d), F32)] * 3 + [jax.ShapeDtypeStruct((ng, nc, hb, CH), F32)] * 2,
        scratch_shapes=[pltpu.VMEM((hb, HD, HD), F32)],
        compiler_params=_cparams(("parallel", "arbitrary"), 32),
    )(q, k, v, beta, g, ss, ts, do)


def _ssd_group(s, xs, bm, cm, dt_r, a_r):
    prs = range(len(s))
    first = lax.broadcasted_iota(jnp.int32, (1, 2 * M2P), 1) < M2P

    def pick(vals, p):
        return jnp.where(first, vals[2 * p], vals[2 * p + 1])

    cs = [_cumsum_rc(a) for a in a_r]
    lm = [_decay(*csi) for csi in cs]
    ecs = [jnp.exp(csi[1]) for csi in cs]
    alast = [jnp.sum(a, axis=1, keepdims=True) for a in a_r]
    ealast = [jnp.exp(al) for al in alast]
    wt = [jnp.exp(al - csi[1]) for al, csi in zip(alast, cs)]
    dtc = [_col(t) for t in dt_r]
    xdt = [xs[:, p * LANE:(p + 1) * LANE] * pick(dtc, p) for p in prs]
    cb = _dg(cm, bm, 1, 1)
    y0 = [_dg(cb * lm[2 * p], xdt[p], 1, 0) for p in prs]
    y1 = [_dg(cb * lm[2 * p + 1], xdt[p], 1, 0) for p in prs]
    yo = [_dg(cm, s[p], 1, 0) for p in prs]
    y = [jnp.where(first, y0[p], y1[p]) + yo[p] * pick(ecs, p) for p in prs]
    s_new = [s[p] * pick(ealast, p) + _dg(bm, xdt[p] * pick(wt, p), 0, 0) for p in prs]
    return s_new, jnp.concatenate(y, axis=1)


def _ssd_specs(nc, d, rev):
    hpg = (d // M2P) // M2G
    gw = hpg * M2P
    cc = (lambda c: nc - 1 - c) if rev else (lambda c: c)
    xs = pl.BlockSpec((CH, gw), lambda g, c: (cc(c), g))
    bm = pl.BlockSpec((CH, NST), lambda g, c: (cc(c), d // LANE + g))
    cm = pl.BlockSpec((CH, NST), lambda g, c: (cc(c), d // LANE + M2G + g))
    row = pl.BlockSpec((1, 1, hpg, CH), lambda g, c: (g, cc(c), 0, 0))
    st = pl.BlockSpec((1, 1, hpg // 2, NST, LANE), lambda g, c: (g, cc(c), 0, 0, 0))
    return xs, bm, cm, row, st, hpg


def _ssd_fwd(xbc, dt, a, d):
    t_rows = xbc.shape[0]
    nc = t_rows // CH
    xs, bm, cm, row, st, hpg = _ssd_specs(nc, d, False)
    ppg = hpg // 2

    def body(xs_ref, b_ref, c_ref, dt_ref, a_ref, y_ref, ss_ref, s_scr):
        c = pl.program_id(1)

        @pl.when(c == 0)
        def _():
            s_scr[...] = jnp.zeros_like(s_scr)

        s0 = [s_scr[p] for p in range(ppg)]
        for p in range(ppg):
            ss_ref[0, 0, p] = s0[p]
        dt_r = [dt_ref[0, 0, pl.ds(h, 1), :] for h in range(hpg)]
        a_r = [a_ref[0, 0, pl.ds(h, 1), :] for h in range(hpg)]
        s1, y = _ssd_group(s0, xs_ref[...], b_ref[...], c_ref[...], dt_r, a_r)
        y_ref[...] = y
        for p in range(ppg):
            s_scr[p] = s1[p]

    return pl.pallas_call(
        body, name="ssd_fwd", grid=(M2G, nc), in_specs=[xs, bm, cm, row, row], out_specs=[xs, st],
        out_shape=[jax.ShapeDtypeStruct((t_rows, d), F32), jax.ShapeDtypeStruct((M2G, nc, ppg, NST, LANE), F32)],
        scratch_shapes=[pltpu.VMEM((ppg, NST, LANE), F32)],
        compiler_params=_cparams(("parallel", "arbitrary"), 32),
    )(xbc, xbc, xbc, dt, a)


def _ssd_bwd(xbc, dt, a, ss, dy, d):
    t_rows = xbc.shape[0]
    nc = t_rows // CH
    xs, bm, cm, row, st, hpg = _ssd_specs(nc, d, True)
    ppg = hpg // 2

    def body(xs_ref, b_ref, c_ref, dt_ref, a_ref, ss_ref, dy_ref, dxs_ref, db_ref, dc_ref, ddt_ref, da_ref, ds_scr):
        cr = pl.program_id(1)

        @pl.when(cr == 0)
        def _():
            ds_scr[...] = jnp.zeros_like(ds_scr)

        first = cr == nc - 1
        keep_c = jnp.logical_or(jnp.logical_not(first), lax.broadcasted_iota(jnp.int32, (CH, 1), 0) >= PADR)
        keep_r = jnp.logical_or(jnp.logical_not(first), lax.broadcasted_iota(jnp.int32, (1, CH), 1) >= PADR)
        dt_r = [dt_ref[0, 0, pl.ds(h, 1), :] for h in range(hpg)]
        a_r = [a_ref[0, 0, pl.ds(h, 1), :] for h in range(hpg)]
        s0 = [ss_ref[0, 0, p] for p in range(ppg)]
        _, vjp = jax.vjp(_ssd_group, s0, xs_ref[...], b_ref[...], c_ref[...], dt_r, a_r)
        ds0, dxs, db, dc, ddt, da = vjp(([ds_scr[p] for p in range(ppg)], dy_ref[...]))
        for p in range(ppg):
            ds_scr[p] = ds0[p]
        dxs_ref[...] = jnp.where(keep_c, dxs, 0.0)
        db_ref[...] = jnp.where(keep_c, db, 0.0)
        dc_ref[...] = jnp.where(keep_c, dc, 0.0)
        for h in range(hpg):
            ddt_ref[0, 0, pl.ds(h, 1), :] = jnp.where(keep_r, ddt[h], 0.0)
            da_ref[0, 0, pl.ds(h, 1), :] = jnp.where(keep_r, da[h], 0.0)

    grp = pl.BlockSpec((CH, NST), lambda g, c: (nc - 1 - c, g))
    return pl.pallas_call(
        body, name="ssd_bwd", grid=(M2G, nc), in_specs=[xs, bm, cm, row, row, st, xs],
        out_specs=[xs, grp, grp, row, row],
        out_shape=[jax.ShapeDtypeStruct((t_rows, d), F32)] + [jax.ShapeDtypeStruct((t_rows, M2G * NST), F32)] * 2
        + [jax.ShapeDtypeStruct((M2G, nc, hpg, CH), F32)] * 2,
        scratch_shapes=[pltpu.VMEM((ppg, NST, LANE), F32)],
        compiler_params=_cparams(("parallel", "arbitrary"), 32),
    )(xbc, xbc, xbc, dt, a, ss, dy)


def _exchange(name, gathers, scatters, after):
    arrays = list(gathers) + list(scatters)
    n_g, n = len(gathers), len(arrays)

    def body(*refs):
        ins, outs = refs[:n], refs[n + 1:2 * n + 1]
        send_sems, recv_sems, local_sems = refs[2 * n + 1:]
        x, y, c = lax.axis_index("x"), lax.axis_index("y"), lax.axis_index("c")
        me = 4 * x + 2 * y + c

        def src(a, slot):
            return ins[a] if a < n_g else ins[a].at[slot]

        local = [pltpu.make_async_copy(src(a, me), outs[a].at[me], local_sems.at[a]) for a in range(n)]
        for cp in local:
            cp.start()
        copies = []
        for rel in range(1, NDEV):
            px, py, pc = x ^ (rel >> 2), y ^ ((rel >> 1) & 1), c ^ (rel & 1)
            peer = 4 * px + 2 * py + pc
            for a in range(n):
                copies.append(pltpu.make_async_remote_copy(
                    src_ref=src(a, peer), dst_ref=outs[a].at[me], send_sem=send_sems.at[a, rel - 1],
                    recv_sem=recv_sems.at[a, rel - 1], device_id=(px, py, pc), device_id_type=pl.DeviceIdType.MESH))
        for cp in copies:
            cp.start()
        for cp in copies:
            cp.wait_recv()
        for cp in copies:
            cp.wait_send()
        for cp in local:
            cp.wait()

    any_spec = pl.BlockSpec(memory_space=pl.ANY)
    out_shape = [jax.ShapeDtypeStruct((NDEV,) + a.shape, a.dtype) for a in gathers]
    out_shape += [jax.ShapeDtypeStruct(a.shape, a.dtype) for a in scatters]
    return pl.pallas_call(
        body, name=name, in_specs=[any_spec] * (n + 1), out_specs=[any_spec] * n, out_shape=out_shape,
        scratch_shapes=[pltpu.SemaphoreType.DMA((n, NDEV - 1)), pltpu.SemaphoreType.DMA((n, NDEV - 1)),
                        pltpu.SemaphoreType.DMA((n,))],
        compiler_params=pltpu.CompilerParams(has_side_effects=True),
    )(*arrays, after)


def _gather_two_level(name, arrays):
    n = len(arrays)

    def body(*refs):
        ins, outs = refs[:n], refs[n:2 * n]
        send_sems, recv_sems, local_sems = refs[2 * n:]
        x, y, c = lax.axis_index("x"), lax.axis_index("y"), lax.axis_index("c")
        me, sibling = (x, y, c), (x, y, 1 - c)
        chips = [(1 - x, y), (x, 1 - y), (1 - x, 1 - y)]

        def copy(a, k, block, to, src=None):
            dst = outs[a].at[4 * block[0] + 2 * block[1] + block[2]]
            return pltpu.make_async_remote_copy(
                src_ref=dst if src is None else src, dst_ref=dst, send_sem=send_sems.at[a, k], recv_sem=recv_sems.at[a, k],
                device_id=to, device_id_type=pl.DeviceIdType.MESH)

        mine = [pltpu.make_async_copy(ins[a], outs[a].at[4 * x + 2 * y + c], local_sems.at[a]) for a in range(n)]
        for cp in mine:
            cp.start()
        first = []
        for a in range(n):
            first.append(copy(a, 0, me, sibling, src=ins[a]))
            first += [copy(a, 1 + j, me, (*chip, c), src=ins[a]) for j, chip in enumerate(chips)]
        for cp in first:
            cp.start()
        passed = [[copy(a, 4 + j, (*chip, c), sibling) for j, chip in enumerate(chips)] for a in range(n)]
        for j, chip in enumerate(chips):
            for a in range(n):
                copy(a, 1 + j, (*chip, c), me).wait_recv()
                passed[a][j].start()
        for a in range(n):
            copy(a, 0, sibling, me).wait_recv()
            for j, chip in enumerate(chips):
                copy(a, 4 + j, (*chip, 1 - c), me).wait_recv()
        for cp in first + [cp for row in passed for cp in row]:
            cp.wait_send()
        for cp in mine:
            cp.wait()

    any_spec = pl.BlockSpec(memory_space=pl.ANY)
    return pl.pallas_call(
        body, name=name, in_specs=[any_spec] * n, out_specs=[any_spec] * n,
        out_shape=[jax.ShapeDtypeStruct((NDEV,) + a.shape, a.dtype) for a in arrays],
        scratch_shapes=[pltpu.SemaphoreType.DMA((n, NDEV - 1)), pltpu.SemaphoreType.DMA((n, NDEV - 1)),
                        pltpu.SemaphoreType.DMA((n,))],
        compiler_params=pltpu.CompilerParams(has_side_effects=True),
    )(*arrays)


_HBM = pl.BlockSpec(memory_space=pltpu.HBM)
_SEM = pl.BlockSpec(memory_space=pltpu.SEMAPHORE)
_EFFECT = pltpu.SideEffectType.DATAFLOW_SIDE_EFFECTING


def _split_copies(srcs, lands, send_sems, recv_sems, n_g):
    x, y, c = lax.axis_index("x"), lax.axis_index("y"), lax.axis_index("c")
    me = 4 * x + 2 * y + c
    copies = []
    for rel in range(1, NDEV):
        px, py, pc = x ^ (rel >> 2), y ^ ((rel >> 1) & 1), c ^ (rel & 1)
        peer = 4 * px + 2 * py + pc
        for a in range(len(srcs)):
            copies.append(pltpu.make_async_remote_copy(
                src_ref=srcs[a] if a < n_g else srcs[a].at[peer], dst_ref=lands[a].at[me],
                send_sem=send_sems.at[a * (NDEV - 1) + rel - 1], recv_sem=recv_sems.at[a * (NDEV - 1) + rel - 1],
                device_id=(px, py, pc), device_id_type=pl.DeviceIdType.MESH))
    return copies


def _exchange_start(name, gathers, scatters, after):
    arrays = list(gathers) + list(scatters)
    n_g, n = len(gathers), len(arrays)
    lands = [lax.empty((NDEV,) + a.shape, a.dtype) for a in gathers] + [lax.empty(a.shape, a.dtype) for a in scatters]

    def body(*refs):
        send_sems, recv_sems = refs[2 * n + 1], refs[2 * n + 2]
        for cp in _split_copies(refs[:n], refs[n:2 * n], send_sems, recv_sems, n_g):
            cp.start()
        refs[-1][...] = jnp.zeros_like(refs[-1])

    sems = pltpu.SemaphoreType.DMA((n * (NDEV - 1),))
    out = pl.pallas_call(
        body, name=name, in_specs=[_HBM] * (2 * n) + [pl.BlockSpec(memory_space=pl.ANY)],
        out_specs=(_SEM, _SEM, *[_HBM] * (2 * n), pl.BlockSpec(memory_space=pltpu.VMEM)),
        out_shape=(sems, sems, *[pltpu.HBM(a.shape, a.dtype) for a in arrays + lands], jax.ShapeDtypeStruct((8, LANE), F32)),
        input_output_aliases={i: 2 + i for i in range(2 * n)},
        compiler_params=pltpu.CompilerParams(has_side_effects=_EFFECT),
    )(*[pltpu.with_memory_space_constraint(a, pltpu.HBM) for a in arrays + lands], after)
    return out[0], out[1], list(out[2:2 + 2 * n]), out[-1], n_g


def _exchange_wait(name, started, after):
    send_sems, recv_sems, thru, _, n_g = started
    n = len(thru) // 2

    def body(*refs):
        for cp in _split_copies(refs[:n], refs[n:2 * n], refs[2 * n], refs[2 * n + 1], n_g):
            cp.wait_send()
            cp.wait_recv()

    out = pl.pallas_call(
        body, name=name, in_specs=[_HBM] * (2 * n) + [_SEM, _SEM, pl.BlockSpec(memory_space=pl.ANY)],
        out_specs=[_HBM] * (2 * n), out_shape=[pltpu.HBM(a.shape, a.dtype) for a in thru],
        input_output_aliases={i: i for i in range(2 * n)},
        compiler_params=pltpu.CompilerParams(has_side_effects=_EFFECT),
    )(*thru, send_sems, recv_sems, after)
    me = 4 * lax.axis_index("x") + 2 * lax.axis_index("y") + lax.axis_index("c")
    full = []
    for a in range(n):
        own = out[a][None] if a < n_g else lax.dynamic_index_in_dim(out[a], me, 0, keepdims=True)
        full.append(lax.dynamic_update_index_in_dim(out[n + a], own, me, 0))
    return full


def _adamw(name, staged, w, m, v):
    r, c = w.shape
    tr = _pick(r, 256, 8)

    def body(st_ref, w_ref, m_ref, v_ref, g_ref, d_ref, nm_ref, nv_ref):
        g = st_ref[0].astype(F32)
        for k in range(1, NDEV):
            g = g + st_ref[k].astype(F32)
        m_new = ADAM_B1 * m_ref[...] + (1.0 - ADAM_B1) * g
        v_new = ADAM_B2 * v_ref[...] + (1.0 - ADAM_B2) * jnp.square(g)
        m_hat = m_new / (1.0 - ADAM_B1 ** ADAM_STEP)
        v_hat = v_new / (1.0 - ADAM_B2 ** ADAM_STEP)
        g_ref[...] = g
        d_ref[...] = -ADAM_LR * (m_hat / (jnp.sqrt(v_hat) + ADAM_EPS) + ADAM_WD * w_ref[...])
        nm_ref[...] = m_new
        nv_ref[...] = v_new

    blk = pl.BlockSpec((tr, c), lambda i: (i, 0))
    return pl.pallas_call(
        body, name=name, grid=(r // tr,), in_specs=[pl.BlockSpec((NDEV, tr, c), lambda i: (0, i, 0)), blk, blk, blk],
        out_specs=[blk] * 4, out_shape=[jax.ShapeDtypeStruct((r, c), F32)] * 4,
        compiler_params=_cparams(("parallel",), 48),
    )(staged, w, m, v)


def _pack(parts):
    flat = jnp.concatenate([p.reshape(-1).astype(F32) for p in parts])
    pad = (-flat.shape[0]) % (8 * LANE)
    return jnp.pad(flat, (0, pad)).reshape(-1, LANE)


def _unpack(slab, shapes):
    flat, out, off = slab.reshape(-1), [], 0
    for s in shapes:
        n = 1
        for dim in s:
            n *= dim
        out.append(flat[off:off + n].reshape(s))
        off += n
    return out


def _to_shards(full, axis):
    shp = full.shape
    t = full.reshape(shp[:axis] + (NDEV, shp[axis] // NDEV) + shp[axis + 1:])
    return jnp.moveaxis(t, axis, 0)


def _from_shards(g, axis):
    t = jnp.moveaxis(g, 0, axis)
    shp = t.shape
    return t.reshape(shp[:axis] + (shp[axis] * shp[axis + 1],) + shp[axis + 2:])


def kernel(x, meta_tokens, norm_mix_w, w_in, dn_conv_w, dn_a_log, dn_dt_bias, dn_norm_w, m2_conv_w, m2_conv_b, m2_a_log, m2_dt_bias, m2_d, m2_norm_w, w_out, norm_ffn_w, ffn_up, ffn_conv_w, ffn_down, norm_final_w, loss_target, m_meta_tokens, m_norm_mix_w, m_w_in, m_dn_conv_w, m_dn_a_log, m_dn_dt_bias, m_dn_norm_w, m_m2_conv_w, m_m2_conv_b, m_m2_a_log, m_m2_dt_bias, m_m2_d, m_m2_norm_w, m_w_out, m_norm_ffn_w, m_ffn_up, m_ffn_conv_w, m_ffn_down, m_norm_final_w, v_meta_tokens, v_norm_mix_w, v_w_in, v_dn_conv_w, v_dn_a_log, v_dn_dt_bias, v_dn_norm_w, v_m2_conv_w, v_m2_conv_b, v_m2_a_log, v_m2_dt_bias, v_m2_d, v_m2_norm_w, v_w_out, v_norm_ffn_w, v_ffn_up, v_ffn_conv_w, v_ffn_down, v_norm_final_w):
    seq, d = x.shape[1], x.shape[2]
    t_rows = seq + CH
    nc = t_rows // CH
    dnh, m2h = d // HD, d // M2P
    dff = ffn_down.shape[1] * NDEV
    xbc_w = d + 2 * M2G * NST
    assert seq % CH == 0 and d % (2 * M2P * M2G) == 0 and 2 * dnh + m2h <= LANE
    hb = max(h for h in (8, 4, 2, 1) if dnh % h == 0)
    tm_rw = _pick(t_rows, 208, 16)

    small_sharded = [meta_tokens, dn_conv_w[0], m2_conv_w[0], ffn_conv_w[0]]
    small_shapes = [p.shape for p in small_sharded]
    g_win, g_small = _gather_two_level("gather_w_in", [w_in[0].astype(WIRE), _pack(small_sharded)])
    rest = _exchange_start("gather_rest_start", [w_out[0].astype(WIRE), ffn_up[0].astype(WIRE), ffn_down[0].astype(WIRE)], [],
                           g_small)
    win = _from_shards(g_win, 1)
    small_full = [_unpack(g_small[k], small_shapes) for k in range(NDEV)]
    meta_f, dnconv_f, m2conv_f, ffnconv_f = [jnp.concatenate([small_full[k][i] for k in range(NDEV)], axis=-1) for i in range(4)]

    o_z, o_b, o_a = 3 * d, 4 * d, 4 * d + dnh
    o_m2z = 4 * d + 2 * dnh
    o_xbc, o_dt = o_m2z + d, o_m2z + d + xbc_w
    w_small = jnp.concatenate([win[:, o_b:o_m2z], win[:, o_dt:], jnp.zeros((d, LANE - 2 * dnh - m2h), WIRE)], axis=1)
    w_seg = {"q": win[:, :d], "k": win[:, d:2 * d], "v": win[:, 2 * d:3 * d], "z": win[:, o_z:o_b],
             "m2z": win[:, o_m2z:o_xbc], "xbc": win[:, o_xbc:o_dt], "sm": w_small}

    h0 = jnp.concatenate([jnp.zeros((PADR, d), F32), meta_f, x[0]], axis=0)
    valid = lambda rows: rows >= PADR

    def norm_fwd(name, h, w):
        return _rw(name, lambda rows, j, hv, wv: (_rms(hv, wv),), [("r", h, d, _c0), ("p", w, d, _c0)],
                   [("r", d, d, _c0, MXU)], t_rows, tm_rw)[0]

    hn1 = norm_fwd("norm_mix", h0, norm_mix_w)
    proj = {s: _mm("proj_" + s, hn1, w_seg[s], dep=rest[3]) for s in w_seg}

    def dn_post(sec, cv):
        s = _silu(cv)
        if sec < 2:
            s = s * lax.rsqrt(jnp.sum(s * s, axis=-1, keepdims=True) + EPS)
        if sec == 0:
            s = s * (HD ** -0.5)
        return s

    def dn_prep(sec, name):
        def fn(rows, own, last, wins, pars):
            return [jnp.where(valid(rows), dn_post(sec, _conv(wins[0], pars[0])), 0.0)], []
        wc = dnconv_f[:, sec * d:(sec + 1) * d]
        return _cv("dn_prep_" + name, fn, [(proj[name], _cj)], [(wc, _cj)], [(d, _cj, F32)], [], t_rows, dnh)[0]

    q_act, k_act, v_act = dn_prep(0, "q"), dn_prep(1, "k"), dn_prep(2, "v")

    lane = lambda: lax.broadcasted_iota(jnp.int32, (1, LANE), 1)

    def lanes_of(vec, off):
        return jnp.pad(vec.astype(F32), ((0, 0), (off, LANE - off - vec.shape[1])))

    gate_params = [lanes_of(dn_a_log, dnh), lanes_of(dn_dt_bias, dnh), lanes_of(m2_a_log, 2 * dnh), lanes_of(m2_dt_bias, 2 * dnh)]

    def gates(rows, sm, p_alog, p_dtb, p_malog, p_mdtb):
        ln = lane()
        is_b, is_g = ln < dnh, jnp.logical_and(ln >= dnh, ln < 2 * dnh)
        is_d = jnp.logical_and(ln >= 2 * dnh, ln < 2 * dnh + m2h)
        beta = jax.nn.sigmoid(sm)
        gdec = -jnp.exp(p_alog) * _softplus(sm + p_dtb)
        dt = _softplus(sm + p_mdtb)
        am = dt * (-jnp.exp(p_malog))
        ok = valid(rows)
        g1 = jnp.where(ok, jnp.where(is_b, beta, jnp.where(is_g, gdec, jnp.where(is_d, dt, 0.0))), 0.0)
        g2 = jnp.where(jnp.logical_and(ok, is_d), am, 0.0)
        return g1, g2

    gate_ins = [("r", proj["sm"], LANE, _c0)] + [("p", p, LANE, _c0) for p in gate_params]
    g1, g2 = _rw("gates", lambda rows, j, *a: gates(rows, *a), gate_ins,
                 [("r", LANE, LANE, _c0, F32), ("r", LANE, LANE, _c0, F32)], t_rows, tm_rw)

    def head_rows(cols, per):
        n = cols.shape[1]
        return cols.reshape(nc, CH, n // per, per).transpose(2, 0, 3, 1)

    def head_cols(rows_):
        ngrp, _, per, _ = rows_.shape
        return rows_.transpose(1, 3, 0, 2).reshape(t_rows, ngrp * per)

    beta_r, gdec_r = head_rows(g1[:, :dnh], hb), head_rows(g1[:, dnh:2 * dnh], hb)
    hpg = m2h // M2G
    dt_r, am_r = head_rows(g1[:, 2 * dnh:2 * dnh + m2h], hpg), head_rows(g2[:, 2 * dnh:2 * dnh + m2h], hpg)

    o_dn, dn_states, dn_tinv = _gdn_fwd(q_act, k_act, v_act, beta_r, gdec_r, hb)

    def dn_out(o, z, w):
        outs = []
        for h in range(dnh):
            sl = slice(h * HD, (h + 1) * HD)
            outs.append(_rms(o[:, sl], w) * _silu(z[:, sl]))
        return jnp.concatenate(outs, axis=1)

    mixed_dn = _rw("dn_out", lambda rows, j, o, z, w: (dn_out(o, z, w),),
                   [("r", o_dn, d, _c0), ("r", proj["z"], d, _c0), ("p", dn_norm_w, HD, _c0)], [("r", d, d, _c0, MXU)],
                   t_rows, tm_rw)[0]

    def m2_prep(rows, own, last, wins, pars):
        return [jnp.where(valid(rows), _silu(_conv(wins[0], pars[0]) + pars[1][0]), 0.0)], []

    xbc_act = _cv("m2_prep", m2_prep, [(proj["xbc"], _cj)], [(m2conv_f, _cj), (m2_conv_b, _cj)], [(xbc_w, _cj, F32)], [],
                  t_rows, xbc_w // LANE)[0]
    y_ssd, m2_states = _ssd_fwd(xbc_act, dt_r, am_r, d)

    d_lanes = jnp.repeat(m2_d.astype(F32), M2P, axis=1)
    gw = d // M2G

    def m2_out(ys, xs, z, dl, nw):
        yv = (ys + dl * xs) * _silu(z)
        outs = []
        for gi in range(M2G):
            sl = slice(gi * gw, (gi + 1) * gw)
            outs.append(_rms(yv[:, sl], nw[:, sl]))
        return jnp.concatenate(outs, axis=1)

    m2_out_ins = [("r", y_ssd, d, _c0), ("r", xbc_act, d, _c0), ("r", proj["m2z"], d, _c0), ("p", d_lanes, d, _c0),
                  ("p", m2_norm_w, d, _c0)]
    mixed_m2 = _rw("m2_out", lambda rows, j, *a: (m2_out(*a),), m2_out_ins, [("r", d, d, _c0, MXU)], t_rows, tm_rw)[0]

    mixed = jnp.concatenate([mixed_dn, mixed_m2], axis=1)
    g_wout, g_wup, g_wdown = _exchange_wait("gather_rest_wait", rest, mixed)
    wout = _from_shards(g_wout, 0)
    wup = _from_shards(g_wup, 1)
    wdown = _from_shards(g_wdown, 0)
    wup_g, wup_v = wup[:, :dff], wup[:, dff:]
    h1 = _mm("out_proj", mixed, wout, add=h0)
    hn2 = norm_fwd("norm_ffn", h1, norm_ffn_w)
    u_g, u_v = _mm("ffn_up_g", hn2, wup_g), _mm("ffn_up_v", hn2, wup_v)
    fc_g, fc_v = ffnconv_f[:, :dff], ffnconv_f[:, dff:]

    def ffn_act(rows, own, last, wins, pars):
        return [jnp.where(valid(rows), _silu(_conv(wins[0], pars[0])) * _conv(wins[1], pars[1]), 0.0)], []

    act = _cv("ffn_act", ffn_act, [(u_g, _cj), (u_v, _cj)], [(fc_g, _cj), (fc_v, _cj)], [(dff, _cj, MXU)], [],
              t_rows, dff // LANE)[0]
    h2 = _mm("ffn_down", act, wdown, add=h1, tk=1408)

    def loss_fn(hv, wf, tgt, rows):
        err = jnp.where(rows >= CH, _rms(hv, wf) - tgt, 0.0)
        return 0.5 * jnp.sum(jnp.mean(err * err, axis=-1, keepdims=True), axis=0, keepdims=True)

    def final(rows, j, hv, wf, tgt):
        loss, vjp = jax.vjp(lambda a, b: loss_fn(a, b, tgt, rows), hv, wf)
        dh, dw = vjp(jnp.ones((1, 1), F32))
        return dh, dh, dw, jnp.broadcast_to(loss, (1, LANE))

    wf2 = norm_final_w.reshape(1, d)
    dh2, dh2_m, d_wf, loss_part = _rw(
        "loss_head", final, [("r", h2, d, _c0), ("p", wf2, d, _c0), ("r", loss_target[0], d, _c0, lambda i: jnp.maximum(i - 1, 0))],
        [("r", d, d, _c0, F32), ("r", d, d, _c0, MXU), ("p", 1, d, d, _c0), ("p", 1, LANE, LANE, _c0)], t_rows, CH)
    loss = lax.psum(loss_part[0, 0], MESH_AXES)

    d_act = _mm("d_act", dh2_m, wdown, tb=True)
    gw_down = _mm("gw_down", act, dh2_m, ta=True, tm=1408, tn=1024, tk=1040, out_dtype=WIRE)
    x_down = _exchange_start("grad_down_start", [], [_to_shards(gw_down, 0).astype(WIRE)], gw_down)

    def t_end(last):
        return t_rows if last else None

    def ffn_act_bwd(rows, own, last, wins, pars):
        (ug, uv, da), (wg, wv) = wins, pars
        cg, cv = _conv(ug, wg), _conv(uv, wv)
        _, vjp = jax.vjp(lambda a, b: _silu(a) * b, cg, cv)
        dcg, dcv = vjp(jnp.where(valid(rows), da, 0.0))
        return ([_conv_t(dcg, wg, rows, t_end(last)), _conv_t(dcv, wv, rows, t_end(last))],
                [_conv_w(jnp.where(own, dcg, 0.0), ug, len(wg)), _conv_w(jnp.where(own, dcv, 0.0), uv, len(wv))])

    kf = fc_g.shape[0]
    du_g, du_v, g_fc_g, g_fc_v = _cv(
        "ffn_act_bwd", ffn_act_bwd, [(u_g, _cj), (u_v, _cj), (d_act, _cj)], [(fc_g, _cj), (fc_v, _cj)],
        [(dff, _cj, MXU), (dff, _cj, MXU)], [(kf, dff, _cj), (kf, dff, _cj)], t_rows, dff // LANE, chunk=CH)
    gw_up_g = _mm("gw_up_g", hn2, du_g, ta=True, tm=1024, tn=1408, tk=1040, out_dtype=WIRE, dep=x_down[3])
    gw_up_v = _mm("gw_up_v", hn2, du_v, ta=True, tm=1024, tn=1408, tk=1040, out_dtype=WIRE)
    gw_up_full = jnp.concatenate([gw_up_g, gw_up_v], axis=1)
    x_up = _exchange_start("grad_up_start", [], [_to_shards(gw_up_full, 1).astype(WIRE)], gw_up_full)
    d_hn2 = _mm("d_hn2_v", du_v, wup_v, tb=True, tk=1408, dep=x_up[3],
                add=_mm("d_hn2_g", du_g, wup_g, tb=True, tk=1408, tn=2048, dep=x_up[3]))

    def norm_bwd(name, h, w, dy, dres):
        def fn(rows, j, hv, wv, dyv, dr):
            _, vjp = jax.vjp(_rms, hv, wv)
            dh, dw = vjp(dyv)
            dh = dh + dr
            return dh, dh, dw
        return _rw(name, fn, [("r", h, d, _c0), ("p", w, d, _c0), ("r", dy, d, _c0), ("r", dres, d, _c0)],
                   [("r", d, d, _c0, F32), ("r", d, d, _c0, MXU), ("p", 1, d, d, _c0)], t_rows, tm_rw)

    dh1, dh1_m, g_norm_ffn = norm_bwd("norm_ffn_bwd", h1, norm_ffn_w, d_hn2, dh2)

    gw_out = _mm("gw_out", mixed, dh1_m, ta=True, tm=1024, tn=1024, tk=1040, out_dtype=WIRE)
    x_out = _exchange_start("grad_out_start", [], [_to_shards(gw_out, 0).astype(WIRE)], gw_out)
    d_mixed = _mm("d_mixed", dh1_m, wout, tb=True, dep=x_out[3])

    def m2_out_bwd(rows, j, ys, xs, z, dl, nw, dy):
        _, vjp = jax.vjp(m2_out, ys, xs, z, dl, nw)
        return vjp(dy)

    dy_ssd, dxs_skip, d_m2z, g_d_lanes, g_m2_norm = _rw(
        "m2_out_bwd", m2_out_bwd, m2_out_ins + [("r", d_mixed, d, lambda j: 1)],
        [("r", d, d, _c0, F32), ("r", d, d, _c0, F32), ("r", d, d, _c0, MXU), ("p", 1, d, d, _c0), ("p", 1, d, d, _c0)],
        t_rows, _pick(t_rows, 208, 16))

    def fold_heads(vec_ref, out_ref):
        r = lax.broadcasted_iota(jnp.int32, (d, LANE), 0)
        c = lax.broadcasted_iota(jnp.int32, (d, LANE), 1)
        out_ref[...] = _dgh(vec_ref[...], jnp.where(jnp.logical_and(r >= c * M2P, r < (c + 1) * M2P), 1.0, 0.0), 1, 0)

    g_m2_d = pl.pallas_call(fold_heads, name="fold_m2_d", out_shape=jax.ShapeDtypeStruct((1, LANE), F32))(g_d_lanes)

    dxs, db_ssd, dc_ssd, ddt_r, dam_r = _ssd_bwd(xbc_act, dt_r, am_r, m2_states, dy_ssd, d)


    def m2_prep_bwd(rows, own, last, wins, pars):
        (p, *ds), (w, b) = wins, pars
        _, vjp = jax.vjp(_silu, _conv(p, w) + b[0])
        dpre, = vjp(jnp.where(valid(rows), functools.reduce(lambda a_, b_: a_ + b_, ds), 0.0))
        dpre_own = jnp.where(own, dpre, 0.0)
        return [_conv_t(dpre, w, rows, t_end(last))], [_conv_w(dpre_own, p, len(w)), [jnp.sum(dpre_own, axis=0, keepdims=True)]]

    def m2_prep_bwd_call(name, off, width, d_arrs):
        at = lambda j, blk0=off // LANE: blk0 + j
        return _cv(name, m2_prep_bwd, [(proj["xbc"], at)] + [(a, _cj) for a in d_arrs], [(m2conv_f, at), (m2_conv_b, at)],
                   [(width, _cj, MXU)], [(m2conv_f.shape[0], width, _cj), (1, width, _cj)], t_rows, width // LANE, chunk=CH)

    dp_xs, gcw_xs, gcb_xs = m2_prep_bwd_call("m2_prep_bwd_x", 0, d, [dxs, dxs_skip])
    dp_b, gcw_b, gcb_b = m2_prep_bwd_call("m2_prep_bwd_b", d, M2G * NST, [db_ssd])
    dp_c, gcw_c, gcb_c = m2_prep_bwd_call("m2_prep_bwd_c", d + M2G * NST, M2G * NST, [dc_ssd])
    d_pxbc = jnp.concatenate([dp_xs, dp_b, dp_c], axis=1)
    g_m2_conv = jnp.concatenate([gcw_xs, gcw_b, gcw_c], axis=1)
    g_m2_conv_b = jnp.concatenate([gcb_xs, gcb_b, gcb_c], axis=1)

    def dn_out_bwd(rows, j, o, z, w, dy):
        _, vjp = jax.vjp(dn_out, o, z, w)
        return vjp(dy)

    d_o, d_z, g_dn_norm = _rw(
        "dn_out_bwd", dn_out_bwd,
        [("r", o_dn, d, _c0), ("r", proj["z"], d, _c0), ("p", dn_norm_w, HD, _c0), ("r", d_mixed, d, _c0)],
        [("r", d, d, _c0, F32), ("r", d, d, _c0, MXU), ("p", 1, HD, HD, _c0)], t_rows, _pick(t_rows, 208, 16))

    dq, dk, dv, dbeta_r, dgdec_r = _gdn_bwd(q_act, k_act, v_act, beta_r, gdec_r, dn_states, dn_tinv, d_o, hb)

    def dn_prep_bwd(sec, name, dact):
        def fn(rows, own, last, wins, pars):
            (p, da), (w,) = wins, pars
            _, vjp = jax.vjp(functools.partial(dn_post, sec), _conv(p, w))
            dcv, = vjp(jnp.where(valid(rows), da, 0.0))
            return [_conv_t(dcv, w, rows, t_end(last))], [_conv_w(jnp.where(own, dcv, 0.0), p, len(w))]
        wc = dnconv_f[:, sec * d:(sec + 1) * d]
        return _cv("dn_prep_bwd_" + name, fn, [(proj[name], _cj), (dact, _cj)], [(wc, _cj)], [(d, _cj, MXU)],
                   [(wc.shape[0], d, _cj)], t_rows, dnh, chunk=CH if sec == 2 else None)

    (dp_q, gcw_q), (dp_k, gcw_k), (dp_v, gcw_v) = dn_prep_bwd(0, "q", dq), dn_prep_bwd(1, "k", dk), dn_prep_bwd(2, "v", dv)
    g_dn_conv = jnp.concatenate([gcw_q, gcw_k, gcw_v], axis=1)

    zpad = jnp.zeros((t_rows, LANE - 2 * dnh - m2h), F32)
    dg1 = jnp.concatenate([head_cols(dbeta_r), head_cols(dgdec_r), head_cols(ddt_r), zpad], axis=1)
    dg2 = jnp.concatenate([jnp.zeros((t_rows, 2 * dnh), F32), head_cols(dam_r), zpad], axis=1)

    def gates_bwd(rows, j, sm, pa, pb, pc, pd, d1, d2):
        _, vjp = jax.vjp(lambda *a: gates(rows, *a), sm, pa, pb, pc, pd)
        return vjp((d1, d2))

    dp_sm, g_pa, g_pb, g_pc, g_pd = _rw(
        "gates_bwd", gates_bwd, gate_ins + [("r", dg1, LANE, _c0), ("r", dg2, LANE, _c0)],
        [("r", LANE, LANE, _c0, MXU)] + [("p", 1, LANE, LANE, _c0)] * 4, t_rows, tm_rw)

    dseg = {"q": dp_q, "k": dp_k, "v": dp_v, "z": d_z, "m2z": d_m2z, "xbc": d_pxbc, "sm": dp_sm}
    gw_seg = {s: _mm("gw_in_" + s, hn1, dseg[s], ta=True, tm=1024, tn=1024, tk=1040, out_dtype=WIRE) for s in dseg}
    gsm = gw_seg["sm"]
    gw_in_full = jnp.concatenate([gw_seg["q"], gw_seg["k"], gw_seg["v"], gw_seg["z"], gsm[:, :2 * dnh], gw_seg["m2z"],
                                  gw_seg["xbc"], gsm[:, 2 * dnh:2 * dnh + m2h]], axis=1)
    x_in = _exchange_start("grad_in_start", [], [_to_shards(gw_in_full, 1).astype(WIRE)], gw_in_full)
    d_hn1 = None
    for s in dseg:
        d_hn1 = _mm("d_hn1_" + s, dseg[s], w_seg[s], tb=True, tk=2048, add=d_hn1, dep=x_in[3])
    dh0, _, g_norm_mix = norm_bwd("norm_mix_bwd", h0, norm_mix_w, d_hn1, dh1)

    g_ffn_conv = jnp.concatenate([g_fc_g, g_fc_v], axis=1)
    small_parts = [_to_shards(dh0[PADR:CH], 1), _to_shards(g_dn_conv, 1), _to_shards(g_m2_conv, 1), _to_shards(g_ffn_conv, 1)]
    small_scatter = jnp.stack([_pack([p[k] for p in small_parts]) for k in range(NDEV)])

    rep_names = ["norm_mix_w", "dn_a_log", "dn_dt_bias", "dn_norm_w", "m2_conv_b", "m2_a_log", "m2_dt_bias", "m2_d",
                 "m2_norm_w", "norm_ffn_w", "norm_final_w"]
    rep_grads = [g_norm_mix, g_pa[:, dnh:2 * dnh], g_pb[:, dnh:2 * dnh], g_dn_norm, g_m2_conv_b, g_pc[:, 2 * dnh:2 * dnh + m2h],
                 g_pd[:, 2 * dnh:2 * dnh + m2h], g_m2_d[:, :m2h], g_m2_norm, g_norm_ffn, d_wf.reshape(d)]

    weights = dict(meta_tokens=meta_tokens, norm_mix_w=norm_mix_w, w_in=w_in, dn_conv_w=dn_conv_w, dn_a_log=dn_a_log,
                   dn_dt_bias=dn_dt_bias, dn_norm_w=dn_norm_w, m2_conv_w=m2_conv_w, m2_conv_b=m2_conv_b, m2_a_log=m2_a_log,
                   m2_dt_bias=m2_dt_bias, m2_d=m2_d, m2_norm_w=m2_norm_w, w_out=w_out, norm_ffn_w=norm_ffn_w, ffn_up=ffn_up,
                   ffn_conv_w=ffn_conv_w, ffn_down=ffn_down, norm_final_w=norm_final_w)
    mom1 = dict(meta_tokens=m_meta_tokens, norm_mix_w=m_norm_mix_w, w_in=m_w_in, dn_conv_w=m_dn_conv_w, dn_a_log=m_dn_a_log,
                dn_dt_bias=m_dn_dt_bias, dn_norm_w=m_dn_norm_w, m2_conv_w=m_m2_conv_w, m2_conv_b=m_m2_conv_b,
                m2_a_log=m_m2_a_log, m2_dt_bias=m_m2_dt_bias, m2_d=m_m2_d, m2_norm_w=m_m2_norm_w, w_out=m_w_out,
                norm_ffn_w=m_norm_ffn_w, ffn_up=m_ffn_up, ffn_conv_w=m_ffn_conv_w, ffn_down=m_ffn_down,
                norm_final_w=m_norm_final_w)
    mom2 = dict(meta_tokens=v_meta_tokens, norm_mix_w=v_norm_mix_w, w_in=v_w_in, dn_conv_w=v_dn_conv_w, dn_a_log=v_dn_a_log,
                dn_dt_bias=v_dn_dt_bias, dn_norm_w=v_dn_norm_w, m2_conv_w=v_m2_conv_w, m2_conv_b=v_m2_conv_b,
                m2_a_log=v_m2_a_log, m2_dt_bias=v_m2_dt_bias, m2_d=v_m2_d, m2_norm_w=v_m2_norm_w, w_out=v_w_out,
                norm_ffn_w=v_norm_ffn_w, ffn_up=v_ffn_up, ffn_conv_w=v_ffn_conv_w, ffn_down=v_ffn_down,
                norm_final_w=v_norm_final_w)
    res = {}

    def adam_big(name, started, after):
        staged, = _exchange_wait("grad_" + name + "_wait", started, after)
        outs = _adamw("adamw_" + name, staged, weights[name][0], mom1[name][0], mom2[name][0])
        res[name] = tuple(o[None] for o in outs)
        return outs[1]

    done = adam_big("ffn_down", x_down, dh0)
    done = adam_big("ffn_up", x_up, done)
    done = adam_big("w_out", x_out, done)
    st_rep, st_small = _exchange("exchange_small_grads", [_pack(rep_grads)], [small_scatter], done)
    adam_big("w_in", x_in, st_small)

    def adam_packed(label, staged, names):
        shapes = [weights[nm].shape for nm in names]
        outs = _adamw(label, staged, *[_pack([src[nm] for nm in names]) for src in (weights, mom1, mom2)])
        unpacked = [_unpack(o, shapes) for o in outs]
        for i, nm in enumerate(names):
            res[nm] = tuple(u[i] for u in unpacked)

    adam_packed("adamw_small_sharded", st_small, ["meta_tokens", "dn_conv_w", "m2_conv_w", "ffn_conv_w"])
    adam_packed("adamw_replicated", st_rep, rep_names)

    order = list(weights)
    grad_x = dh0[CH:][None]
    return (loss, grad_x, *[res[nm][0] for nm in order], *[res[nm][1] for nm in order], *[res[nm][2] for nm in order],
            *[res[nm][3] for nm in order])
```

```python
import functools
import math

import jax
import jax.numpy as jnp
from jax import lax
from jax.experimental import pallas as pl
from jax.experimental.pallas import tpu as pltpu

F32 = jnp.float32
MXU = jnp.bfloat16
WIRE = jnp.bfloat16
HI = lax.Precision.HIGH

NDEV = 8
CH = 64
NMETA = 16
PADR = CH - NMETA
EPS = 1e-6
HD = 128
M2P = 64
M2G = 4
NST = 128
LANE = 128

ADAM_LR, ADAM_B1, ADAM_B2, ADAM_EPS, ADAM_WD, ADAM_STEP = 0.001, 0.9, 0.999, 1e-08, 0.01, 10

MESH_AXES = ("x", "y", "c")


def _pick(n, target, mult=16):
    best = None
    for t in range(mult, min(n, target) + 1, mult):
        if n % t == 0:
            best = t
    return best if best is not None else n


def _dg(a, b, ca, cb):
    return lax.dot_general(a.astype(MXU), b.astype(MXU), (((ca,), (cb,)), ((), ())), preferred_element_type=F32)


def _dgh(a, b, ca, cb):
    return lax.dot_general(a, b, (((ca,), (cb,)), ((), ())), precision=HI, preferred_element_type=F32)


def _silu(x):
    return x * jax.nn.sigmoid(x)


def _softplus(x):
    return jnp.maximum(x, 0.0) + jnp.log1p(jnp.exp(-jnp.abs(x)))


def _rms(x, w):
    return x * lax.rsqrt(jnp.mean(x * x, axis=-1, keepdims=True) + EPS) * w


def _cparams(sem, vmem_mb):
    return pltpu.CompilerParams(dimension_semantics=sem, vmem_limit_bytes=vmem_mb << 20)


def _mm(name, a, b, *, ta=False, tb=False, add=None, out_dtype=F32, tm=1040, tn=1024, tk=2048, dep=None,
        b_cols=None):
    m, kdim = (a.shape[1], a.shape[0]) if ta else a.shape
    b_start, b_width = b_cols if b_cols is not None else (0, b.shape[1])
    n = b.shape[0] if tb else b_width
    if tb:
        kdim = b_width
    tm = _pick(m, tm, 128 if ta else 16)
    tn = _pick(n if tb else math.gcd(b_width, b_start), tn, 128)
    tk = _pick(math.gcd(b_width, b_start) if tb else kdim, tk, 16 if (ta and not tb) else 128)
    nk = kdim // tk
    bj0, bk0 = (0, b_start // tk) if tb else (b_start // tn, 0)
    ca, cb = (0 if ta else 1), (1 if tb else 0)

    def body(*refs):
        a_ref, b_ref = refs[0], refs[1]
        add_ref = refs[2] if add is not None else None
        if nk == 1:
            r = _dg(a_ref[...], b_ref[...], ca, cb)
            if add_ref is not None:
                r = r + add_ref[...].astype(F32)
            refs[-1][...] = r.astype(refs[-1].dtype)
            return
        o_ref, acc = refs[-2], refs[-1]
        k = pl.program_id(2)

        @pl.when(k == 0)
        def _():
            acc[...] = jnp.zeros_like(acc)

        acc[...] += _dg(a_ref[...], b_ref[...], ca, cb)

        @pl.when(k == nk - 1)
        def _():
            r = acc[...]
            if add_ref is not None:
                r = r + add_ref[...].astype(F32)
            o_ref[...] = r.astype(o_ref.dtype)

    a_spec = pl.BlockSpec((tk, tm), lambda i, j, k: (k, i)) if ta else pl.BlockSpec((tm, tk), lambda i, j, k: (i, k))
    b_spec = (pl.BlockSpec((tn, tk), lambda i, j, k: (j, k + bk0)) if tb
              else pl.BlockSpec((tk, tn), lambda i, j, k: (k, j + bj0)))
    in_specs, ops = [a_spec, b_spec], [a, b]
    if add is not None:
        in_specs.append(pl.BlockSpec((tm, tn), lambda i, j, k: (i, j)))
        ops.append(add)
    if dep is not None:
        in_specs.append(pl.BlockSpec((8, LANE), lambda i, j, k: (0, 0)))
        ops.append(dep)
    return pl.pallas_call(
        body, name=name, grid=(m // tm, n // tn, nk), in_specs=in_specs,
        out_specs=pl.BlockSpec((tm, tn), lambda i, j, k: (i, j)),
        out_shape=jax.ShapeDtypeStruct((m, n), out_dtype),
        scratch_shapes=[pltpu.VMEM((tm, tn), F32)] if nk > 1 else [],
        compiler_params=_cparams(("parallel", "parallel", "arbitrary"), 48),
    )(*ops)


def _rw(name, fn, ins, outs, nrows, tm, ncol=1, vmem_mb=48):
    nrow = nrows // tm
    sub = tm
    in_specs, ops = [], []
    for spec in ins:
        kind, arr, bw, cj = spec[:4]
        ops.append(arr)
        if kind == "r":
            ri = spec[4] if len(spec) > 4 else (lambda i: i)
            in_specs.append(pl.BlockSpec((tm, bw), lambda j, i, cj=cj, ri=ri: (ri(i), cj(j))))
        else:
            in_specs.append(pl.BlockSpec((arr.shape[0], bw), lambda j, i, cj=cj: (0, cj(j))))
    out_shape, out_specs = [], []
    for o in outs:
        if o[0] == "r":
            _, width, bw, cj, dt = o
            out_shape.append(jax.ShapeDtypeStruct((nrows, width), dt))
            out_specs.append(pl.BlockSpec((tm, bw), lambda j, i, cj=cj: (i, cj(j))))
        else:
            _, rows, width, bw, cj = o
            out_shape.append(jax.ShapeDtypeStruct((rows, width), F32))
            out_specs.append(pl.BlockSpec((rows, bw), lambda j, i, cj=cj: (0, cj(j))))
    n_in = len(ins)

    def body(*refs):
        j, i = pl.program_id(0), pl.program_id(1)
        in_refs, out_refs = refs[:n_in], refs[n_in:]
        pars = [ref[...] if spec[0] == "p" else None for spec, ref in zip(ins, in_refs)]

        def one(r0, nr):
            rows = i * tm + r0 + lax.broadcasted_iota(jnp.int32, (nr, 1), 0)
            vals = [par if spec[0] == "p" else ref[pl.ds(r0, nr), :] for spec, ref, par in zip(ins, in_refs, pars)]
            parts = []
            for o, val, ref in zip(outs, fn(rows, j, *vals), out_refs):
                if o[0] == "r":
                    ref[pl.ds(r0, nr), :] = val.astype(ref.dtype)
                else:
                    parts.append(val)
            return parts

        if sub >= tm:
            parts = one(0, tm)
        else:
            zero = [jnp.zeros((1, o[3]), F32) for o in outs if o[0] == "p"]
            parts = lax.fori_loop(
                0, tm // sub, lambda s, acc: [a + b for a, b in zip(acc, one(pl.multiple_of(s * sub, sub), sub))], zero)
        for ref, val in zip([r for o, r in zip(outs, out_refs) if o[0] == "p"], parts):
            @pl.when(i == 0)
            def _(ref=ref):
                ref[...] = jnp.zeros_like(ref)

            ref[...] += val

    return pl.pallas_call(
        body, name=name, grid=(ncol, nrow), in_specs=in_specs, out_specs=out_specs, out_shape=out_shape,
        compiler_params=_cparams(("parallel", "arbitrary"), vmem_mb),
    )(*ops)


def _c0(j):
    return 0


def _cj(j):
    return j


def _shift(x, s):
    if s == 0:
        return x
    return pltpu.roll(x, s % x.shape[0], 0)


def _conv(x, w):
    k = len(w)
    return functools.reduce(lambda a, b: a + b, [w[j] * _shift(x, k - 1 - j) for j in range(k)])


def _conv_t(dy, w, rows, t_end):
    k = len(w)
    terms = []
    for j in range(k):
        s = k - 1 - j
        v = _shift(dy, -s)
        if t_end is not None and s > 0:
            v = jnp.where(rows + s < t_end, v, 0.0)
        terms.append(w[j] * v)
    return functools.reduce(lambda a, b: a + b, terms)


def _conv_w(dy, x, k):
    return [jnp.sum(dy * _shift(x, k - 1 - j), axis=0, keepdims=True) for j in range(k)]


HALO = 8


def _cv(name, fn, row_ins, par_ins, row_outs, par_outs, nrows, ncol, chunk=None):
    whole = chunk is None
    chunk = nrows if whole else chunk
    n_chunks = nrows // chunk
    assert nrows % chunk == 0 and (whole or n_chunks >= 3)
    n_ri, n_pi, n_ro = len(row_ins), len(par_ins), len(row_outs)

    def body(*refs):
        rin, pin = refs[:n_ri], refs[n_ri:n_ri + n_pi]
        rout, pout = refs[n_ri + n_pi:n_ri + n_pi + n_ro], refs[n_ri + n_pi + n_ro:]
        pars = [[p[pl.ds(r, 1), :] for r in range(p.shape[0])] for p in pin]

        def run(r0, top, bot, last):
            wlen = top + chunk + bot
            w0 = r0 - top if isinstance(r0, int) else pl.multiple_of(r0 - top, HALO)
            local = lax.broadcasted_iota(jnp.int32, (wlen, 1), 0)
            own = jnp.logical_and(local >= top, local < top + chunk)
            outs, parts = fn(w0 + local, own, last, [ref[pl.ds(w0, wlen), :] for ref in rin], pars)
            for ref, val in zip(rout, outs):
                ref[pl.ds(r0, chunk), :] = val[top:top + chunk].astype(ref.dtype)
            return parts

        def add(acc, parts):
            return [[a + b for a, b in zip(ra, rb)] for ra, rb in zip(acc, parts)]

        if whole:
            acc = run(0, 0, 0, True)
        else:
            acc = run(0, 0, HALO, False)
            acc = lax.fori_loop(1, n_chunks - 1,
                                lambda i, a: add(a, run(pl.multiple_of(i * chunk, chunk), HALO, HALO, False)), acc)
            acc = add(acc, run(nrows - chunk, HALO, 0, True))
        for ref, prow in zip(pout, acc):
            for r, v in enumerate(prow):
                ref[pl.ds(r, 1), :] = v

    in_specs = [pl.BlockSpec((nrows, LANE), lambda j, cj=cj: (0, cj(j))) for _, cj in row_ins]
    in_specs += [pl.BlockSpec((a.shape[0], LANE), lambda j, cj=cj: (0, cj(j))) for a, cj in par_ins]
    out_specs = [pl.BlockSpec((nrows, LANE), lambda j, cj=cj: (0, cj(j))) for _, cj, _ in row_outs]
    out_specs += [pl.BlockSpec((k, LANE), lambda j, cj=cj: (0, cj(j))) for k, _, cj in par_outs]
    out_shape = [jax.ShapeDtypeStruct((nrows, width), dt) for width, _, dt in row_outs]
    out_shape += [jax.ShapeDtypeStruct((k, width), F32) for k, width, _ in par_outs]
    return pl.pallas_call(
        body, name=name, grid=(ncol,), in_specs=in_specs, out_specs=out_specs, out_shape=out_shape,
        compiler_params=_cparams(("parallel",), 48),
    )(*[a for a, _ in row_ins], *[a for a, _ in par_ins])


def _tri():
    r = lax.broadcasted_iota(jnp.int32, (CH, CH), 0)
    c = lax.broadcasted_iota(jnp.int32, (CH, CH), 1)
    return r, c


def _col(row):
    r, c = _tri()
    return jnp.sum(jnp.where(r == c, row, 0.0), axis=1, keepdims=True)


def _cumsum_rc(g_r):
    r, c = _tri()
    g_c = _col(g_r)
    cs_r = jnp.sum(jnp.where(r <= c, g_c, 0.0), axis=0, keepdims=True)
    cs_c = jnp.sum(jnp.where(c <= r, g_r, 0.0), axis=1, keepdims=True)
    return cs_r, cs_c


def _decay(cs_r, cs_c):
    r, c = _tri()
    return jnp.exp(jnp.where(c <= r, cs_c - cs_r, -jnp.inf))


def _gdn_a(ks, betas, gs):
    r, c = _tri()
    cs = [_cumsum_rc(g) for g in gs]
    kk = [_dg(k, k, 1, 1) for k in ks]
    return [jnp.where(c < r, _col(b) * kki * _decay(*csi), 0.0) for b, kki, csi in zip(betas, kk, cs)]


def _neumann(a_list):
    r, c = _tri()
    xs = [jnp.where(r == c, 1.0, 0.0) - a for a in a_list]
    ps = list(a_list)
    n = 2
    while n < CH:
        ps = [_dgh(p, p, 1, 0) for p in ps]
        xs = [x + _dgh(x, p, 1, 0) for x, p in zip(xs, ps)]
        n *= 2
    return xs


def _gdn_rest(ss, qs, ks, vs, betas, gs, ts):
    n = range(len(ss))
    cs = [_cumsum_rc(g) for g in gs]
    dm = [_decay(*csi) for csi in cs]
    ecs = [jnp.exp(csi[1]) for csi in cs]
    bc = [_col(b) for b in betas]
    u = [_dgh(ts[i], vs[i] * bc[i], 1, 0) for i in n]
    w = [_dgh(ts[i], ks[i] * (bc[i] * ecs[i]), 1, 0) for i in n]
    ws = [_dg(w[i], ss[i], 1, 0) for i in n]
    v_new = [u[i] - ws[i] for i in n]
    qk = [_dg(qs[i], ks[i], 1, 1) * dm[i] for i in n]
    o_in = [_dg(qs[i] * ecs[i], ss[i], 1, 0) for i in n]
    o = [o_in[i] + _dg(qk[i], v_new[i], 1, 0) for i in n]
    g_last = [jnp.sum(g, axis=1, keepdims=True) for g in gs]
    s_new = [ss[i] * jnp.exp(g_last[i]) + _dg(ks[i] * jnp.exp(g_last[i] - cs[i][1]), v_new[i], 0, 0) for i in n]
    return s_new, o


def _gdn_fwd(q, k, v, beta, g, hb):
    t_rows, d = q.shape
    nc, ng, w = t_rows // CH, d // (HD * hb), HD * hb
    sls = [slice(h * HD, (h + 1) * HD) for h in range(hb)]

    def body(q_ref, k_ref, v_ref, b_ref, g_ref, o_ref, ss_ref, ts_ref, s_scr):
        c = pl.program_id(1)

        @pl.when(c == 0)
        def _():
            s_scr[...] = jnp.zeros_like(s_scr)

        qs, ks, vs = ([ref[:, sl] for sl in sls] for ref in (q_ref, k_ref, v_ref))
        br = [b_ref[0, 0, pl.ds(h, 1), :] for h in range(hb)]
        gr = [g_ref[0, 0, pl.ds(h, 1), :] for h in range(hb)]
        s0 = [s_scr[h] for h in range(hb)]
        tm = _neumann(_gdn_a(ks, br, gr))
        s1, o = _gdn_rest(s0, qs, ks, vs, br, gr, tm)
        for h in range(hb):
            ss_ref[0, 0, h] = s0[h]
            ts_ref[0, 0, h] = tm[h]
            o_ref[:, sls[h]] = o[h]
            s_scr[h] = s1[h]

    blk = pl.BlockSpec((CH, w), lambda n, c: (c, n))
    row = pl.BlockSpec((1, 1, hb, CH), lambda n, c: (n, c, 0, 0))
    return pl.pallas_call(
        body, name="gdn_fwd", grid=(ng, nc), in_specs=[blk, blk, blk, row, row],
        out_specs=[blk, pl.BlockSpec((1, 1, hb, HD, HD), lambda n, c: (n, c, 0, 0, 0)),
                   pl.BlockSpec((1, 1, hb, CH, CH), lambda n, c: (n, c, 0, 0, 0))],
        out_shape=[jax.ShapeDtypeStruct((t_rows, d), F32), jax.ShapeDtypeStruct((ng, nc, hb, HD, HD), F32),
                   jax.ShapeDtypeStruct((ng, nc, hb, CH, CH), F32)],
        scratch_shapes=[pltpu.VMEM((hb, HD, HD), F32)],
        compiler_params=_cparams(("parallel", "arbitrary"), 32),
    )(q, k, v, beta, g)


def _gdn_bwd(q, k, v, beta, g, ss, ts, do, hb):
    t_rows, d = q.shape
    nc, ng, w = t_rows // CH, d // (HD * hb), HD * hb
    sls = [slice(h * HD, (h + 1) * HD) for h in range(hb)]

    def body(q_ref, k_ref, v_ref, b_ref, g_ref, ss_ref, ts_ref, do_ref, dq_ref, dk_ref, dv_ref, db_ref, dg_ref, ds_scr):
        cr = pl.program_id(1)

        @pl.when(cr == 0)
        def _():
            ds_scr[...] = jnp.zeros_like(ds_scr)

        first = cr == nc - 1
        rowi = lax.broadcasted_iota(jnp.int32, (CH, 1), 0)
        lani = lax.broadcasted_iota(jnp.int32, (1, CH), 1)
        keep_c = jnp.logical_or(jnp.logical_not(first), rowi >= PADR)
        keep_r = jnp.logical_or(jnp.logical_not(first), lani >= PADR)
        hs = range(hb)
        qs, ks, vs, dos = ([ref[:, sl] for sl in sls] for ref in (q_ref, k_ref, v_ref, do_ref))
        br = [b_ref[0, 0, pl.ds(h, 1), :] for h in hs]
        gr = [g_ref[0, 0, pl.ds(h, 1), :] for h in hs]
        tm = [ts_ref[0, 0, h] for h in hs]
        _, vjp_rest = jax.vjp(_gdn_rest, [ss_ref[0, 0, h] for h in hs], qs, ks, vs, br, gr, tm)
        ds0, dq, dk, dv, db, dg, dt = vjp_rest(([ds_scr[h] for h in hs], dos))
        dtt = [_dgh(dt[h], tm[h], 1, 1) for h in hs]
        da = [-_dgh(tm[h], dtt[h], 0, 0) for h in hs]
        _, vjp_a = jax.vjp(_gdn_a, ks, br, gr)
        dk2, db2, dg2 = vjp_a(da)
        for h in hs:
            ds_scr[h] = ds0[h]
            dq_ref[:, sls[h]] = jnp.where(keep_c, dq[h], 0.0)
            dk_ref[:, sls[h]] = jnp.where(keep_c, dk[h] + dk2[h], 0.0)
            dv_ref[:, sls[h]] = jnp.where(keep_c, dv[h], 0.0)
            db_ref[0, 0, pl.ds(h, 1), :] = jnp.where(keep_r, db[h] + db2[h], 0.0)
            dg_ref[0, 0, pl.ds(h, 1), :] = jnp.where(keep_r, dg[h] + dg2[h], 0.0)

    blk = pl.BlockSpec((CH, w), lambda n, c: (nc - 1 - c, n))
    row = pl.BlockSpec((1, 1, hb, CH), lambda n, c: (n, nc - 1 - c, 0, 0))
    return pl.pallas_call(
        body, name="gdn_bwd", grid=(ng, nc),
        in_specs=[blk, blk, blk, row, row, pl.BlockSpec((1, 1, hb, HD, HD), lambda n, c: (n, nc - 1 - c, 0, 0, 0)),
                  pl.BlockSpec((1, 1, hb, CH, CH), lambda n, c: (n, nc - 1 - c, 0, 0, 0)), blk],
        out_specs=[blk, blk, blk, row, row],
        out_shape=[jax.ShapeDtypeStruct((t_rows, d), F32)] * 3 + [jax.ShapeDtypeStruct((ng, nc, hb, CH), F32)] * 2,
        scratch_shapes=[pltpu.VMEM((hb, HD, HD), F32)],
        compiler_params=_cparams(("parallel", "arbitrary"), 32),
    )(q, k, v, beta, g, ss, ts, do)


def _ssd_group(s, xs, bm, cm, dt_r, a_r):
    prs = range(len(s))
    first = lax.broadcasted_iota(jnp.int32, (1, 2 * M2P), 1) < M2P

    def pick(vals, p):
        return jnp.where(first, vals[2 * p], vals[2 * p + 1])

    cs = [_cumsum_rc(a) for a in a_r]
    lm = [_decay(*csi) for csi in cs]
    ecs = [jnp.exp(csi[1]) for csi in cs]
    alast = [jnp.sum(a, axis=1, keepdims=True) for a in a_r]
    ealast = [jnp.exp(al) for al in alast]
    wt = [jnp.exp(al - csi[1]) for al, csi in zip(alast, cs)]
    dtc = [_col(t) for t in dt_r]
    xdt = [xs[:, p * LANE:(p + 1) * LANE] * pick(dtc, p) for p in prs]
    cb = _dg(cm, bm, 1, 1)
    y0 = [_dg(cb * lm[2 * p], xdt[p], 1, 0) for p in prs]
    y1 = [_dg(cb * lm[2 * p + 1], xdt[p], 1, 0) for p in prs]
    yo = [_dg(cm, s[p], 1, 0) for p in prs]
    y = [jnp.where(first, y0[p], y1[p]) + yo[p] * pick(ecs, p) for p in prs]
    s_new = [s[p] * pick(ealast, p) + _dg(bm, xdt[p] * pick(wt, p), 0, 0) for p in prs]
    return s_new, jnp.concatenate(y, axis=1)


def _ssd_specs(nc, d, rev):
    hpg = (d // M2P) // M2G
    gw = hpg * M2P
    cc = (lambda c: nc - 1 - c) if rev else (lambda c: c)
    xs = pl.BlockSpec((CH, gw), lambda g, c: (cc(c), g))
    bm = pl.BlockSpec((CH, NST), lambda g, c: (cc(c), d // LANE + g))
    cm = pl.BlockSpec((CH, NST), lambda g, c: (cc(c), d // LANE + M2G + g))
    row = pl.BlockSpec((1, 1, hpg, CH), lambda g, c: (g, cc(c), 0, 0))
    st = pl.BlockSpec((1, 1, hpg // 2, NST, LANE), lambda g, c: (g, cc(c), 0, 0, 0))
    return xs, bm, cm, row, st, hpg


def _ssd_fwd(xbc, dt, a, d):
    t_rows = xbc.shape[0]
    nc = t_rows // CH
    xs, bm, cm, row, st, hpg = _ssd_specs(nc, d, False)
    ppg = hpg // 2

    def body(xs_ref, b_ref, c_ref, dt_ref, a_ref, y_ref, ss_ref, s_scr):
        c = pl.program_id(1)

        @pl.when(c == 0)
        def _():
            s_scr[...] = jnp.zeros_like(s_scr)

        s0 = [s_scr[p] for p in range(ppg)]
        for p in range(ppg):
            ss_ref[0, 0, p] = s0[p]
        dt_r = [dt_ref[0, 0, pl.ds(h, 1), :] for h in range(hpg)]
        a_r = [a_ref[0, 0, pl.ds(h, 1), :] for h in range(hpg)]
        s1, y = _ssd_group(s0, xs_ref[...], b_ref[...], c_ref[...], dt_r, a_r)
        y_ref[...] = y
        for p in range(ppg):
            s_scr[p] = s1[p]

    return pl.pallas_call(
        body, name="ssd_fwd", grid=(M2G, nc), in_specs=[xs, bm, cm, row, row], out_specs=[xs, st],
        out_shape=[jax.ShapeDtypeStruct((t_rows, d), F32), jax.ShapeDtypeStruct((M2G, nc, ppg, NST, LANE), F32)],
        scratch_shapes=[pltpu.VMEM((ppg, NST, LANE), F32)],
        compiler_params=_cparams(("parallel", "arbitrary"), 32),
    )(xbc, xbc, xbc, dt, a)


def _ssd_bwd(xbc, dt, a, ss, dy, d):
    t_rows = xbc.shape[0]
    nc = t_rows // CH
    xs, bm, cm, row, st, hpg = _ssd_specs(nc, d, True)
    ppg = hpg // 2

    def body(xs_ref, b_ref, c_ref, dt_ref, a_ref, ss_ref, dy_ref, dxs_ref, db_ref, dc_ref, ddt_ref, da_ref, ds_scr):
        cr = pl.program_id(1)

        @pl.when(cr == 0)
        def _():
            ds_scr[...] = jnp.zeros_like(ds_scr)

        first = cr == nc - 1
        keep_c = jnp.logical_or(jnp.logical_not(first), lax.broadcasted_iota(jnp.int32, (CH, 1), 0) >= PADR)
        keep_r = jnp.logical_or(jnp.logical_not(first), lax.broadcasted_iota(jnp.int32, (1, CH), 1) >= PADR)
        dt_r = [dt_ref[0, 0, pl.ds(h, 1), :] for h in range(hpg)]
        a_r = [a_ref[0, 0, pl.ds(h, 1), :] for h in range(hpg)]
        s0 = [ss_ref[0, 0, p] for p in range(ppg)]
        _, vjp = jax.vjp(_ssd_group, s0, xs_ref[...], b_ref[...], c_ref[...], dt_r, a_r)
        ds0, dxs, db, dc, ddt, da = vjp(([ds_scr[p] for p in range(ppg)], dy_ref[...]))
        for p in range(ppg):
            ds_scr[p] = ds0[p]
        dxs_ref[...] = jnp.where(keep_c, dxs, 0.0)
        db_ref[...] = jnp.where(keep_c, db, 0.0)
        dc_ref[...] = jnp.where(keep_c, dc, 0.0)
        for h in range(hpg):
            ddt_ref[0, 0, pl.ds(h, 1), :] = jnp.where(keep_r, ddt[h], 0.0)
            da_ref[0, 0, pl.ds(h, 1), :] = jnp.where(keep_r, da[h], 0.0)

    grp = pl.BlockSpec((CH, NST), lambda g, c: (nc - 1 - c, g))
    return pl.pallas_call(
        body, name="ssd_bwd", grid=(M2G, nc), in_specs=[xs, bm, cm, row, row, st, xs],
        out_specs=[xs, grp, grp, row, row],
        out_shape=[jax.ShapeDtypeStruct((t_rows, d), F32)] + [jax.ShapeDtypeStruct((t_rows, M2G * NST), F32)] * 2
        + [jax.ShapeDtypeStruct((M2G, nc, hpg, CH), F32)] * 2,
        scratch_shapes=[pltpu.VMEM((ppg, NST, LANE), F32)],
        compiler_params=_cparams(("parallel", "arbitrary"), 32),
    )(xbc, xbc, xbc, dt, a, ss, dy)


def _exchange(name, gathers, scatters, after):
    arrays = list(gathers) + list(scatters)
    n_g, n = len(gathers), len(arrays)

    def body(*refs):
        ins, outs = refs[:n], refs[n + 1:2 * n + 1]
        send_sems, recv_sems, local_sems = refs[2 * n + 1:]
        x, y, c = lax.axis_index("x"), lax.axis_index("y"), lax.axis_index("c")
        me = 4 * x + 2 * y + c

        def src(a, slot):
            return ins[a] if a < n_g else ins[a].at[slot]

        local = [pltpu.make_async_copy(src(a, me), outs[a].at[me], local_sems.at[a]) for a in range(n)]
        for cp in local:
            cp.start()
        copies = []
        for rel in range(1, NDEV):
            px, py, pc = x ^ (rel >> 2), y ^ ((rel >> 1) & 1), c ^ (rel & 1)
            peer = 4 * px + 2 * py + pc
            for a in range(n):
                copies.append(pltpu.make_async_remote_copy(
                    src_ref=src(a, peer), dst_ref=outs[a].at[me], send_sem=send_sems.at[a, rel - 1],
                    recv_sem=recv_sems.at[a, rel - 1], device_id=(px, py, pc), device_id_type=pl.DeviceIdType.MESH))
        for cp in copies:
            cp.start()
        for cp in copies:
            cp.wait_recv()
        for cp in copies:
            cp.wait_send()
        for cp in local:
            cp.wait()

    any_spec = pl.BlockSpec(memory_space=pl.ANY)
    out_shape = [jax.ShapeDtypeStruct((NDEV,) + a.shape, a.dtype) for a in gathers]
    out_shape += [jax.ShapeDtypeStruct(a.shape, a.dtype) for a in scatters]
    return pl.pallas_call(
        body, name=name, in_specs=[any_spec] * (n + 1), out_specs=[any_spec] * n, out_shape=out_shape,
        scratch_shapes=[pltpu.SemaphoreType.DMA((n, NDEV - 1)), pltpu.SemaphoreType.DMA((n, NDEV - 1)),
                        pltpu.SemaphoreType.DMA((n,))],
        compiler_params=pltpu.CompilerParams(has_side_effects=True),
    )(*arrays, after)


def _gather_two_level(name, arrays):
    n = len(arrays)

    def body(*refs):
        ins, outs = refs[:n], refs[n:2 * n]
        send_sems, recv_sems, local_sems = refs[2 * n:]
        x, y, c = lax.axis_index("x"), lax.axis_index("y"), lax.axis_index("c")
        me, sibling = (x, y, c), (x, y, 1 - c)
        chips = [(1 - x, y), (x, 1 - y), (1 - x, 1 - y)]

        def copy(a, k, block, to, src=None):
            dst = outs[a].at[4 * block[0] + 2 * block[1] + block[2]]
            return pltpu.make_async_remote_copy(
                src_ref=dst if src is None else src, dst_ref=dst, send_sem=send_sems.at[a, k], recv_sem=recv_sems.at[a, k],
                device_id=to, device_id_type=pl.DeviceIdType.MESH)

        mine = [pltpu.make_async_copy(ins[a], outs[a].at[4 * x + 2 * y + c], local_sems.at[a]) for a in range(n)]
        for cp in mine:
            cp.start()
        first = []
        for a in range(n):
            first.append(copy(a, 0, me, sibling, src=ins[a]))
            first += [copy(a, 1 + j, me, (*chip, c), src=ins[a]) for j, chip in enumerate(chips)]
        for cp in first:
            cp.start()
        passed = [[copy(a, 4 + j, (*chip, c), sibling) for j, chip in enumerate(chips)] for a in range(n)]
        for j, chip in enumerate(chips):
            for a in range(n):
                copy(a, 1 + j, (*chip, c), me).wait_recv()
                passed[a][j].start()
        for a in range(n):
            copy(a, 0, sibling, me).wait_recv()
            for j, chip in enumerate(chips):
                copy(a, 4 + j, (*chip, 1 - c), me).wait_recv()
        for cp in first + [cp for row in passed for cp in row]:
            cp.wait_send()
        for cp in mine:
            cp.wait()

    any_spec = pl.BlockSpec(memory_space=pl.ANY)
    return pl.pallas_call(
        body, name=name, in_specs=[any_spec] * n, out_specs=[any_spec] * n,
        out_shape=[jax.ShapeDtypeStruct((NDEV,) + a.shape, a.dtype) for a in arrays],
        scratch_shapes=[pltpu.SemaphoreType.DMA((n, NDEV - 1)), pltpu.SemaphoreType.DMA((n, NDEV - 1)),
                        pltpu.SemaphoreType.DMA((n,))],
        compiler_params=pltpu.CompilerParams(has_side_effects=True),
    )(*arrays)


_HBM = pl.BlockSpec(memory_space=pltpu.HBM)
_SEM = pl.BlockSpec(memory_space=pltpu.SEMAPHORE)
_EFFECT = pltpu.SideEffectType.DATAFLOW_SIDE_EFFECTING


def _split_copies(srcs, lands, send_sems, recv_sems, n_g):
    x, y, c = lax.axis_index("x"), lax.axis_index("y"), lax.axis_index("c")
    me = 4 * x + 2 * y + c
    copies = []
    for rel in range(1, NDEV):
        px, py, pc = x ^ (rel >> 2), y ^ ((rel >> 1) & 1), c ^ (rel & 1)
        peer = 4 * px + 2 * py + pc
        for a in range(len(srcs)):
            copies.append(pltpu.make_async_remote_copy(
                src_ref=srcs[a] if a < n_g else srcs[a].at[peer], dst_ref=lands[a].at[me],
                send_sem=send_sems.at[a * (NDEV - 1) + rel - 1], recv_sem=recv_sems.at[a * (NDEV - 1) + rel - 1],
                device_id=(px, py, pc), device_id_type=pl.DeviceIdType.MESH))
    return copies


def _exchange_start(name, gathers, scatters, after):
    arrays = list(gathers) + list(scatters)
    n_g, n = len(gathers), len(arrays)
    lands = [lax.empty((NDEV,) + a.shape, a.dtype) for a in gathers] + [lax.empty(a.shape, a.dtype) for a in scatters]

    def body(*refs):
        send_sems, recv_sems = refs[2 * n + 1], refs[2 * n + 2]
        for cp in _split_copies(refs[:n], refs[n:2 * n], send_sems, recv_sems, n_g):
            cp.start()
        refs[-1][...] = jnp.zeros_like(refs[-1])

    sems = pltpu.SemaphoreType.DMA((n * (NDEV - 1),))
    out = pl.pallas_call(
        body, name=name, in_specs=[_HBM] * (2 * n) + [pl.BlockSpec(memory_space=pl.ANY)],
        out_specs=(_SEM, _SEM, *[_HBM] * (2 * n), pl.BlockSpec(memory_space=pltpu.VMEM)),
        out_shape=(sems, sems, *[pltpu.HBM(a.shape, a.dtype) for a in arrays + lands], jax.ShapeDtypeStruct((8, LANE), F32)),
        input_output_aliases={i: 2 + i for i in range(2 * n)},
        compiler_params=pltpu.CompilerParams(has_side_effects=_EFFECT),
    )(*[pltpu.with_memory_space_constraint(a, pltpu.HBM) for a in arrays + lands], after)
    return out[0], out[1], list(out[2:2 + 2 * n]), out[-1], n_g


def _exchange_wait(name, started, after):
    send_sems, recv_sems, thru, _, n_g = started
    n = len(thru) // 2

    def body(*refs):
        for cp in _split_copies(refs[:n], refs[n:2 * n], refs[2 * n], refs[2 * n + 1], n_g):
            cp.wait_send()
            cp.wait_recv()

    out = pl.pallas_call(
        body, name=name, in_specs=[_HBM] * (2 * n) + [_SEM, _SEM, pl.BlockSpec(memory_space=pl.ANY)],
        out_specs=[_HBM] * (2 * n), out_shape=[pltpu.HBM(a.shape, a.dtype) for a in thru],
        input_output_aliases={i: i for i in range(2 * n)},
        compiler_params=pltpu.CompilerParams(has_side_effects=_EFFECT),
    )(*thru, send_sems, recv_sems, after)
    me = 4 * lax.axis_index("x") + 2 * lax.axis_index("y") + lax.axis_index("c")
    full = []
    for a in range(n):
        own = out[a][None] if a < n_g else lax.dynamic_index_in_dim(out[a], me, 0, keepdims=True)
        full.append(lax.dynamic_update_index_in_dim(out[n + a], own, me, 0))
    return full


def _adamw(name, staged, w, m, v):
    r, c = w.shape
    tr = _pick(r, 256, 8)

    def body(st_ref, w_ref, m_ref, v_ref, g_ref, d_ref, nm_ref, nv_ref):
        g = st_ref[0].astype(F32)
        for k in range(1, NDEV):
            g = g + st_ref[k].astype(F32)
        m_new = ADAM_B1 * m_ref[...] + (1.0 - ADAM_B1) * g
        v_new = ADAM_B2 * v_ref[...] + (1.0 - ADAM_B2) * jnp.square(g)
        m_hat = m_new / (1.0 - ADAM_B1 ** ADAM_STEP)
        v_hat = v_new / (1.0 - ADAM_B2 ** ADAM_STEP)
        g_ref[...] = g
        d_ref[...] = -ADAM_LR * (m_hat / (jnp.sqrt(v_hat) + ADAM_EPS) + ADAM_WD * w_ref[...])
        nm_ref[...] = m_new
        nv_ref[...] = v_new

    blk = pl.BlockSpec((tr, c), lambda i: (i, 0))
    return pl.pallas_call(
        body, name=name, grid=(r // tr,), in_specs=[pl.BlockSpec((NDEV, tr, c), lambda i: (0, i, 0)), blk, blk, blk],
        out_specs=[blk] * 4, out_shape=[jax.ShapeDtypeStruct((r, c), F32)] * 4,
        compiler_params=_cparams(("parallel",), 48),
    )(staged, w, m, v)


def _pack(parts):
    flat = jnp.concatenate([p.reshape(-1).astype(F32) for p in parts])
    pad = (-flat.shape[0]) % (8 * LANE)
    return jnp.pad(flat, (0, pad)).reshape(-1, LANE)


def _unpack(slab, shapes):
    flat, out, off = slab.reshape(-1), [], 0
    for s in shapes:
        n = 1
        for dim in s:
            n *= dim
        out.append(flat[off:off + n].reshape(s))
        off += n
    return out


def _to_shards(full, axis):
    shp = full.shape
    t = full.reshape(shp[:axis] + (NDEV, shp[axis] // NDEV) + shp[axis + 1:])
    return jnp.moveaxis(t, axis, 0)


def _from_shards(g, axis):
    t = jnp.moveaxis(g, 0, axis)
    shp = t.shape
    return t.reshape(shp[:axis] + (shp[axis] * shp[axis + 1],) + shp[axis + 2:])


def kernel(x, meta_tokens, norm_mix_w, w_in, dn_conv_w, dn_a_log, dn_dt_bias, dn_norm_w, m2_conv_w, m2_conv_b, m2_a_log, m2_dt_bias, m2_d, m2_norm_w, w_out, norm_ffn_w, ffn_up, ffn_conv_w, ffn_down, norm_final_w, loss_target, m_meta_tokens, m_norm_mix_w, m_w_in, m_dn_conv_w, m_dn_a_log, m_dn_dt_bias, m_dn_norm_w, m_m2_conv_w, m_m2_conv_b, m_m2_a_log, m_m2_dt_bias, m_m2_d, m_m2_norm_w, m_w_out, m_norm_ffn_w, m_ffn_up, m_ffn_conv_w, m_ffn_down, m_norm_final_w, v_meta_tokens, v_norm_mix_w, v_w_in, v_dn_conv_w, v_dn_a_log, v_dn_dt_bias, v_dn_norm_w, v_m2_conv_w, v_m2_conv_b, v_m2_a_log, v_m2_dt_bias, v_m2_d, v_m2_norm_w, v_w_out, v_norm_ffn_w, v_ffn_up, v_ffn_conv_w, v_ffn_down, v_norm_final_w):
    seq, d = x.shape[1], x.shape[2]
    t_rows = seq + CH
    nc = t_rows // CH
    dnh, m2h = d // HD, d // M2P
    dff = ffn_down.shape[1] * NDEV
    xbc_w = d + 2 * M2G * NST
    assert seq % CH == 0 and d % (2 * M2P * M2G) == 0 and 2 * dnh + m2h <= LANE
    hb = max(h for h in (8, 4, 2, 1) if dnh % h == 0)
    tm_rw = _pick(t_rows, 208, 16)

    small_sharded = [meta_tokens, dn_conv_w[0], m2_conv_w[0], ffn_conv_w[0]]
    small_shapes = [p.shape for p in small_sharded]
    g_win, g_small = _gather_two_level("gather_w_in", [w_in[0].astype(WIRE), _pack(small_sharded)])
    rest = _exchange_start("gather_rest_start", [w_out[0].astype(WIRE), ffn_up[0].astype(WIRE), ffn_down[0].astype(WIRE)], [],
                           g_small)
    win = _from_shards(g_win, 1)
    small_full = [_unpack(g_small[k], small_shapes) for k in range(NDEV)]
    meta_f, dnconv_f, m2conv_f, ffnconv_f = [jnp.concatenate([small_full[k][i] for k in range(NDEV)], axis=-1) for i in range(4)]

    o_z, o_b, o_a = 3 * d, 4 * d, 4 * d + dnh
    o_m2z = 4 * d + 2 * dnh
    o_xbc, o_dt = o_m2z + d, o_m2z + d + xbc_w
    w_all = jnp.concatenate([win[:, :o_b], win[:, o_m2z:o_dt], win[:, o_b:o_m2z], win[:, o_dt:],
                             jnp.zeros((d, LANE - 2 * dnh - m2h), WIRE)], axis=1)
    seg_cols = {"q": (0, d), "k": (d, d), "v": (2 * d, d), "z": (3 * d, d), "m2z": (4 * d, d), "xbc": (5 * d, xbc_w),
                "sm": (5 * d + xbc_w, LANE)}

    h0 = jnp.concatenate([jnp.zeros((PADR, d), F32), meta_f, x[0]], axis=0)
    valid = lambda rows: rows >= PADR

    def norm_fwd(name, h, w):
        return _rw(name, lambda rows, j, hv, wv: (_rms(hv, wv),), [("r", h, d, _c0), ("p", w, d, _c0)],
                   [("r", d, d, _c0, MXU)], t_rows, tm_rw)[0]

    hn1 = norm_fwd("norm_mix", h0, norm_mix_w)
    proj = {s: _mm("proj_" + s, hn1, w_all, b_cols=seg_cols[s], dep=rest[3]) for s in seg_cols}

    def dn_post(sec, cv):
        s = _silu(cv)
        if sec < 2:
            s = s * lax.rsqrt(jnp.sum(s * s, axis=-1, keepdims=True) + EPS)
        if sec == 0:
            s = s * (HD ** -0.5)
        return s

    def dn_prep(sec, name):
        def fn(rows, own, last, wins, pars):
            return [jnp.where(valid(rows), dn_post(sec, _conv(wins[0], pars[0])), 0.0)], []
        wc = dnconv_f[:, sec * d:(sec + 1) * d]
        return _cv("dn_prep_" + name, fn, [(proj[name], _cj)], [(wc, _cj)], [(d, _cj, F32)], [], t_rows, dnh)[0]

    q_act, k_act, v_act = dn_prep(0, "q"), dn_prep(1, "k"), dn_prep(2, "v")

    lane = lambda: lax.broadcasted_iota(jnp.int32, (1, LANE), 1)

    def lanes_of(vec, off):
        return jnp.pad(vec.astype(F32), ((0, 0), (off, LANE - off - vec.shape[1])))

    gate_params = [lanes_of(dn_a_log, dnh), lanes_of(dn_dt_bias, dnh), lanes_of(m2_a_log, 2 * dnh), lanes_of(m2_dt_bias, 2 * dnh)]

    def gates(rows, sm, p_alog, p_dtb, p_malog, p_mdtb):
        ln = lane()
        is_b, is_g = ln < dnh, jnp.logical_and(ln >= dnh, ln < 2 * dnh)
        is_d = jnp.logical_and(ln >= 2 * dnh, ln < 2 * dnh + m2h)
        beta = jax.nn.sigmoid(sm)
        gdec = -jnp.exp(p_alog) * _softplus(sm + p_dtb)
        dt = _softplus(sm + p_mdtb)
        am = dt * (-jnp.exp(p_malog))
        ok = valid(rows)
        g1 = jnp.where(ok, jnp.where(is_b, beta, jnp.where(is_g, gdec, jnp.where(is_d, dt, 0.0))), 0.0)
        g2 = jnp.where(jnp.logical_and(ok, is_d), am, 0.0)
        return g1, g2

    gate_ins = [("r", proj["sm"], LANE, _c0)] + [("p", p, LANE, _c0) for p in gate_params]
    g1, g2 = _rw("gates", lambda rows, j, *a: gates(rows, *a), gate_ins,
                 [("r", LANE, LANE, _c0, F32), ("r", LANE, LANE, _c0, F32)], t_rows, tm_rw)

    def head_rows(cols, per):
        n = cols.shape[1]
        return cols.reshape(nc, CH, n // per, per).transpose(2, 0, 3, 1)

    def head_cols(rows_):
        ngrp, _, per, _ = rows_.shape
        return rows_.transpose(1, 3, 0, 2).reshape(t_rows, ngrp * per)

    beta_r, gdec_r = head_rows(g1[:, :dnh], hb), head_rows(g1[:, dnh:2 * dnh], hb)
    hpg = m2h // M2G
    dt_r, am_r = head_rows(g1[:, 2 * dnh:2 * dnh + m2h], hpg), head_rows(g2[:, 2 * dnh:2 * dnh + m2h], hpg)

    o_dn, dn_states, dn_tinv = _gdn_fwd(q_act, k_act, v_act, beta_r, gdec_r, hb)

    def dn_out(o, z, w):
        outs = []
        for h in range(dnh):
            sl = slice(h * HD, (h + 1) * HD)
            outs.append(_rms(o[:, sl], w) * _silu(z[:, sl]))
        return jnp.concatenate(outs, axis=1)

    mixed_dn = _rw("dn_out", lambda rows, j, o, z, w: (dn_out(o, z, w),),
                   [("r", o_dn, d, _c0), ("r", proj["z"], d, _c0), ("p", dn_norm_w, HD, _c0)], [("r", d, d, _c0, MXU)],
                   t_rows, tm_rw)[0]

    def m2_prep(rows, own, last, wins, pars):
        return [jnp.where(valid(rows), _silu(_conv(wins[0], pars[0]) + pars[1][0]), 0.0)], []

    xbc_act = _cv("m2_prep", m2_prep, [(proj["xbc"], _cj)], [(m2conv_f, _cj), (m2_conv_b, _cj)], [(xbc_w, _cj, F32)], [],
                  t_rows, xbc_w // LANE)[0]
    y_ssd, m2_states = _ssd_fwd(xbc_act, dt_r, am_r, d)

    d_lanes = jnp.repeat(m2_d.astype(F32), M2P, axis=1)
    gw = d // M2G

    def m2_out(ys, xs, z, dl, nw):
        yv = (ys + dl * xs) * _silu(z)
        outs = []
        for gi in range(M2G):
            sl = slice(gi * gw, (gi + 1) * gw)
            outs.append(_rms(yv[:, sl], nw[:, sl]))
        return jnp.concatenate(outs, axis=1)

    m2_out_ins = [("r", y_ssd, d, _c0), ("r", xbc_act, d, _c0), ("r", proj["m2z"], d, _c0), ("p", d_lanes, d, _c0),
                  ("p", m2_norm_w, d, _c0)]
    mixed_m2 = _rw("m2_out", lambda rows, j, *a: (m2_out(*a),), m2_out_ins, [("r", d, d, _c0, MXU)], t_rows, tm_rw)[0]

    mixed = jnp.concatenate([mixed_dn, mixed_m2], axis=1)
    g_wout, g_wup, g_wdown = _exchange_wait("gather_rest_wait", rest, mixed)
    wout = _from_shards(g_wout, 0)
    wup = _from_shards(g_wup, 1)
    wdown = _from_shards(g_wdown, 0)
    up_g, up_v = (0, dff), (dff, dff)
    h1 = _mm("out_proj", mixed, wout, add=h0)
    hn2 = norm_fwd("norm_ffn", h1, norm_ffn_w)
    u_g, u_v = _mm("ffn_up_g", hn2, wup, b_cols=up_g), _mm("ffn_up_v", hn2, wup, b_cols=up_v)
    fc_g, fc_v = ffnconv_f[:, :dff], ffnconv_f[:, dff:]

    def ffn_act(rows, own, last, wins, pars):
        return [jnp.where(valid(rows), _silu(_conv(wins[0], pars[0])) * _conv(wins[1], pars[1]), 0.0)], []

    act = _cv("ffn_act", ffn_act, [(u_g, _cj), (u_v, _cj)], [(fc_g, _cj), (fc_v, _cj)], [(dff, _cj, MXU)], [],
              t_rows, dff // LANE)[0]
    h2 = _mm("ffn_down", act, wdown, add=h1, tk=1408)

    def loss_fn(hv, wf, tgt, rows):
        err = jnp.where(rows >= CH, _rms(hv, wf) - tgt, 0.0)
        return 0.5 * jnp.sum(jnp.mean(err * err, axis=-1, keepdims=True), axis=0, keepdims=True)

    def final(rows, j, hv, wf, tgt):
        loss, vjp = jax.vjp(lambda a, b: loss_fn(a, b, tgt, rows), hv, wf)
        dh, dw = vjp(jnp.ones((1, 1), F32))
        return dh, dh, dw, jnp.broadcast_to(loss, (1, LANE))

    wf2 = norm_final_w.reshape(1, d)
    dh2, dh2_m, d_wf, loss_part = _rw(
        "loss_head", final, [("r", h2, d, _c0), ("p", wf2, d, _c0), ("r", loss_target[0], d, _c0, lambda i: jnp.maximum(i - 1, 0))],
        [("r", d, d, _c0, F32), ("r", d, d, _c0, MXU), ("p", 1, d, d, _c0), ("p", 1, LANE, LANE, _c0)], t_rows, CH)
    loss = lax.psum(loss_part[0, 0], MESH_AXES)

    d_act = _mm("d_act", dh2_m, wdown, tb=True)
    gw_down = _mm("gw_down", act, dh2_m, ta=True, tm=1408, tn=1024, tk=2080, out_dtype=WIRE)
    x_down = _exchange_start("grad_down_start", [], [_to_shards(gw_down, 0).astype(WIRE)], gw_down)

    def t_end(last):
        return t_rows if last else None

    def ffn_act_bwd(rows, own, last, wins, pars):
        (ug, uv, da), (wg, wv) = wins, pars
        cg, cv = _conv(ug, wg), _conv(uv, wv)
        _, vjp = jax.vjp(lambda a, b: _silu(a) * b, cg, cv)
        dcg, dcv = vjp(jnp.where(valid(rows), da, 0.0))
        return ([_conv_t(dcg, wg, rows, t_end(last)), _conv_t(dcv, wv, rows, t_end(last))],
                [_conv_w(jnp.where(own, dcg, 0.0), ug, len(wg)), _conv_w(jnp.where(own, dcv, 0.0), uv, len(wv))])

    kf = fc_g.shape[0]
    du_g, du_v, g_fc_g, g_fc_v = _cv(
        "ffn_act_bwd", ffn_act_bwd, [(u_g, _cj), (u_v, _cj), (d_act, _cj)], [(fc_g, _cj), (fc_v, _cj)],
        [(dff, _cj, MXU), (dff, _cj, MXU)], [(kf, dff, _cj), (kf, dff, _cj)], t_rows, dff // LANE, chunk=CH)
    gw_up_g = _mm("gw_up_g", hn2, du_g, ta=True, tm=1024, tn=1408, tk=2080, out_dtype=WIRE, dep=x_down[3])
    gw_up_v = _mm("gw_up_v", hn2, du_v, ta=True, tm=1024, tn=1408, tk=2080, out_dtype=WIRE)
    gw_up_full = jnp.concatenate([gw_up_g, gw_up_v], axis=1)
    x_up = _exchange_start("grad_up_start", [], [_to_shards(gw_up_full, 1).astype(WIRE)], gw_up_full)
    d_hn2 = _mm("d_hn2_v", du_v, wup, b_cols=up_v, tb=True, tk=1408, dep=x_up[3],
                add=_mm("d_hn2_g", du_g, wup, b_cols=up_g, tb=True, tk=1408, tn=2048, dep=x_up[3]))

    def norm_bwd(name, h, w, dy, dres):
        def fn(rows, j, hv, wv, dyv, dr):
            _, vjp = jax.vjp(_rms, hv, wv)
            dh, dw = vjp(dyv)
            dh = dh + dr
            return dh, dh, dw
        return _rw(name, fn, [("r", h, d, _c0), ("p", w, d, _c0), ("r", dy, d, _c0), ("r", dres, d, _c0)],
                   [("r", d, d, _c0, F32), ("r", d, d, _c0, MXU), ("p", 1, d, d, _c0)], t_rows, tm_rw)

    dh1, dh1_m, g_norm_ffn = norm_bwd("norm_ffn_bwd", h1, norm_ffn_w, d_hn2, dh2)

    gw_out = _mm("gw_out", mixed, dh1_m, ta=True, tm=1024, tn=1024, tk=2080, out_dtype=WIRE)
    x_out = _exchange_start("grad_out_start", [], [_to_shards(gw_out, 0).astype(WIRE)], gw_out)
    d_mixed = _mm("d_mixed", dh1_m, wout, tb=True, dep=x_out[3])

    def m2_out_bwd(rows, j, ys, xs, z, dl, nw, dy):
        _, vjp = jax.vjp(m2_out, ys, xs, z, dl, nw)
        return vjp(dy)

    dy_ssd, dxs_skip, d_m2z, g_d_lanes, g_m2_norm = _rw(
        "m2_out_bwd", m2_out_bwd, m2_out_ins + [("r", d_mixed, d, lambda j: 1)],
        [("r", d, d, _c0, F32), ("r", d, d, _c0, F32), ("r", d, d, _c0, MXU), ("p", 1, d, d, _c0), ("p", 1, d, d, _c0)],
        t_rows, _pick(t_rows, 208, 16))

    def fold_heads(vec_ref, out_ref):
        r = lax.broadcasted_iota(jnp.int32, (d, LANE), 0)
        c = lax.broadcasted_iota(jnp.int32, (d, LANE), 1)
        out_ref[...] = _dgh(vec_ref[...], jnp.where(jnp.logical_and(r >= c * M2P, r < (c + 1) * M2P), 1.0, 0.0), 1, 0)

    g_m2_d = pl.pallas_call(fold_heads, name="fold_m2_d", out_shape=jax.ShapeDtypeStruct((1, LANE), F32))(g_d_lanes)

    dxs, db_ssd, dc_ssd, ddt_r, dam_r = _ssd_bwd(xbc_act, dt_r, am_r, m2_states, dy_ssd, d)


    def m2_prep_bwd(rows, own, last, wins, pars):
        (p, *ds), (w, b) = wins, pars
        _, vjp = jax.vjp(_silu, _conv(p, w) + b[0])
        dpre, = vjp(jnp.where(valid(rows), functools.reduce(lambda a_, b_: a_ + b_, ds), 0.0))
        dpre_own = jnp.where(own, dpre, 0.0)
        return [_conv_t(dpre, w, rows, t_end(last))], [_conv_w(dpre_own, p, len(w)), [jnp.sum(dpre_own, axis=0, keepdims=True)]]

    def m2_prep_bwd_call(name, off, width, d_arrs):
        at = lambda j, blk0=off // LANE: blk0 + j
        return _cv(name, m2_prep_bwd, [(proj["xbc"], at)] + [(a, _cj) for a in d_arrs], [(m2conv_f, at), (m2_conv_b, at)],
                   [(width, _cj, MXU)], [(m2conv_f.shape[0], width, _cj), (1, width, _cj)], t_rows, width // LANE, chunk=CH)

    dp_xs, gcw_xs, gcb_xs = m2_prep_bwd_call("m2_prep_bwd_x", 0, d, [dxs, dxs_skip])
    dp_b, gcw_b, gcb_b = m2_prep_bwd_call("m2_prep_bwd_b", d, M2G * NST, [db_ssd])
    dp_c, gcw_c, gcb_c = m2_prep_bwd_call("m2_prep_bwd_c", d + M2G * NST, M2G * NST, [dc_ssd])
    d_pxbc = jnp.concatenate([dp_xs, dp_b, dp_c], axis=1)
    g_m2_conv = jnp.concatenate([gcw_xs, gcw_b, gcw_c], axis=1)
    g_m2_conv_b = jnp.concatenate([gcb_xs, gcb_b, gcb_c], axis=1)

    def dn_out_bwd(rows, j, o, z, w, dy):
        _, vjp = jax.vjp(dn_out, o, z, w)
        return vjp(dy)

    d_o, d_z, g_dn_norm = _rw(
        "dn_out_bwd", dn_out_bwd,
        [("r", o_dn, d, _c0), ("r", proj["z"], d, _c0), ("p", dn_norm_w, HD, _c0), ("r", d_mixed, d, _c0)],
        [("r", d, d, _c0, F32), ("r", d, d, _c0, MXU), ("p", 1, HD, HD, _c0)], t_rows, _pick(t_rows, 208, 16))

    dq, dk, dv, dbeta_r, dgdec_r = _gdn_bwd(q_act, k_act, v_act, beta_r, gdec_r, dn_states, dn_tinv, d_o, hb)

    def dn_prep_bwd(sec, name, dact):
        def fn(rows, own, last, wins, pars):
            (p, da), (w,) = wins, pars
            _, vjp = jax.vjp(functools.partial(dn_post, sec), _conv(p, w))
            dcv, = vjp(jnp.where(valid(rows), da, 0.0))
            return [_conv_t(dcv, w, rows, t_end(last))], [_conv_w(jnp.where(own, dcv, 0.0), p, len(w))]
        wc = dnconv_f[:, sec * d:(sec + 1) * d]
        return _cv("dn_prep_bwd_" + name, fn, [(proj[name], _cj), (dact, _cj)], [(wc, _cj)], [(d, _cj, MXU)],
                   [(wc.shape[0], d, _cj)], t_rows, dnh, chunk=CH if sec == 2 else None)

    (dp_q, gcw_q), (dp_k, gcw_k), (dp_v, gcw_v) = dn_prep_bwd(0, "q", dq), dn_prep_bwd(1, "k", dk), dn_prep_bwd(2, "v", dv)
    g_dn_conv = jnp.concatenate([gcw_q, gcw_k, gcw_v], axis=1)

    zpad = jnp.zeros((t_rows, LANE - 2 * dnh - m2h), F32)
    dg1 = jnp.concatenate([head_cols(dbeta_r), head_cols(dgdec_r), head_cols(ddt_r), zpad], axis=1)
    dg2 = jnp.concatenate([jnp.zeros((t_rows, 2 * dnh), F32), head_cols(dam_r), zpad], axis=1)

    def gates_bwd(rows, j, sm, pa, pb, pc, pd, d1, d2):
        _, vjp = jax.vjp(lambda *a: gates(rows, *a), sm, pa, pb, pc, pd)
        return vjp((d1, d2))

    dp_sm, g_pa, g_pb, g_pc, g_pd = _rw(
        "gates_bwd", gates_bwd, gate_ins + [("r", dg1, LANE, _c0), ("r", dg2, LANE, _c0)],
        [("r", LANE, LANE, _c0, MXU)] + [("p", 1, LANE, LANE, _c0)] * 4, t_rows, tm_rw)

    dseg = {"q": dp_q, "k": dp_k, "v": dp_v, "z": d_z, "m2z": d_m2z, "xbc": d_pxbc, "sm": dp_sm}
    gw_seg = {s: _mm("gw_in_" + s, hn1, dseg[s], ta=True, tm=1024, tn=1024, tk=2080, out_dtype=WIRE) for s in dseg}
    gsm = gw_seg["sm"]
    gw_in_full = jnp.concatenate([gw_seg["q"], gw_seg["k"], gw_seg["v"], gw_seg["z"], gsm[:, :2 * dnh], gw_seg["m2z"],
                                  gw_seg["xbc"], gsm[:, 2 * dnh:2 * dnh + m2h]], axis=1)
    x_in = _exchange_start("grad_in_start", [], [_to_shards(gw_in_full, 1).astype(WIRE)], gw_in_full)
    d_hn1 = None
    for s in dseg:
        d_hn1 = _mm("d_hn1_" + s, dseg[s], w_all, b_cols=seg_cols[s], tb=True, tk=2048, add=d_hn1, dep=x_in[3])
    dh0, _, g_norm_mix = norm_bwd("norm_mix_bwd", h0, norm_mix_w, d_hn1, dh1)

    g_ffn_conv = jnp.concatenate([g_fc_g, g_fc_v], axis=1)
    small_parts = [_to_shards(dh0[PADR:CH], 1), _to_shards(g_dn_conv, 1), _to_shards(g_m2_conv, 1), _to_shards(g_ffn_conv, 1)]
    small_scatter = jnp.stack([_pack([p[k] for p in small_parts]) for k in range(NDEV)])

    rep_names = ["norm_mix_w", "dn_a_log", "dn_dt_bias", "dn_norm_w", "m2_conv_b", "m2_a_log", "m2_dt_bias", "m2_d",
                 "m2_norm_w", "norm_ffn_w", "norm_final_w"]
    rep_grads = [g_norm_mix, g_pa[:, dnh:2 * dnh], g_pb[:, dnh:2 * dnh], g_dn_norm, g_m2_conv_b, g_pc[:, 2 * dnh:2 * dnh + m2h],
                 g_pd[:, 2 * dnh:2 * dnh + m2h], g_m2_d[:, :m2h], g_m2_norm, g_norm_ffn, d_wf.reshape(d)]

    weights = dict(meta_tokens=meta_tokens, norm_mix_w=norm_mix_w, w_in=w_in, dn_conv_w=dn_conv_w, dn_a_log=dn_a_log,
                   dn_dt_bias=dn_dt_bias, dn_norm_w=dn_norm_w, m2_conv_w=m2_conv_w, m2_conv_b=m2_conv_b, m2_a_log=m2_a_log,
                   m2_dt_bias=m2_dt_bias, m2_d=m2_d, m2_norm_w=m2_norm_w, w_out=w_out, norm_ffn_w=norm_ffn_w, ffn_up=ffn_up,
                   ffn_conv_w=ffn_conv_w, ffn_down=ffn_down, norm_final_w=norm_final_w)
    mom1 = dict(meta_tokens=m_meta_tokens, norm_mix_w=m_norm_mix_w, w_in=m_w_in, dn_conv_w=m_dn_conv_w, dn_a_log=m_dn_a_log,
                dn_dt_bias=m_dn_dt_bias, dn_norm_w=m_dn_norm_w, m2_conv_w=m_m2_conv_w, m2_conv_b=m_m2_conv_b,
                m2_a_log=m_m2_a_log, m2_dt_bias=m_m2_dt_bias, m2_d=m_m2_d, m2_norm_w=m_m2_norm_w, w_out=m_w_out,
                norm_ffn_w=m_norm_ffn_w, ffn_up=m_ffn_up, ffn_conv_w=m_ffn_conv_w, ffn_down=m_ffn_down,
                norm_final_w=m_norm_final_w)
    mom2 = dict(meta_tokens=v_meta_tokens, norm_mix_w=v_norm_mix_w, w_in=v_w_in, dn_conv_w=v_dn_conv_w, dn_a_log=v_dn_a_log,
                dn_dt_bias=v_dn_dt_bias, dn_norm_w=v_dn_norm_w, m2_conv_w=v_m2_conv_w, m2_conv_b=v_m2_conv_b,
                m2_a_log=v_m2_a_log, m2_dt_bias=v_m2_dt_bias, m2_d=v_m2_d, m2_norm_w=v_m2_norm_w, w_out=v_w_out,
                norm_ffn_w=v_norm_ffn_w, ffn_up=v_ffn_up, ffn_conv_w=v_ffn_conv_w, ffn_down=v_ffn_down,
                norm_final_w=v_norm_final_w)
    res = {}

    def adam_big(name, started, after):
        staged, = _exchange_wait("grad_" + name + "_wait", started, after)
        outs = _adamw("adamw_" + name, staged, weights[name][0], mom1[name][0], mom2[name][0])
        res[name] = tuple(o[None] for o in outs)
        return outs[1]

    done = adam_big("ffn_down", x_down, dh0)
    done = adam_big("ffn_up", x_up, done)
    done = adam_big("w_out", x_out, done)
    st_rep, st_small = _exchange("exchange_small_grads", [_pack(rep_grads)], [small_scatter], done)
    adam_big("w_in", x_in, st_small)

    def adam_packed(label, staged, names):
        shapes = [weights[nm].shape for nm in names]
        outs = _adamw(label, staged, *[_pack([src[nm] for nm in names]) for src in (weights, mom1, mom2)])
        unpacked = [_unpack(o, shapes) for o in outs]
        for i, nm in enumerate(names):
            res[nm] = tuple(u[i] for u in unpacked)

    adam_packed("adamw_small_sharded", st_small, ["meta_tokens", "dn_conv_w", "m2_conv_w", "ffn_conv_w"])
    adam_packed("adamw_replicated", st_rep, rep_names)

    order = list(weights)
    grad_x = dh0[CH:][None]
    return (loss, grad_x, *[res[nm][0] for nm in order], *[res[nm][1] for nm in order], *[res[nm][2] for nm in order],
            *[res[nm][3] for nm in order])
```

```python
import functools
import math

import jax
import jax.numpy as jnp
from jax import lax
from jax.experimental import pallas as pl
from jax.experimental.pallas import tpu as pltpu

F32 = jnp.float32
MXU = jnp.bfloat16
WIRE = jnp.bfloat16
HI = lax.Precision.HIGH

NDEV = 8
CH = 64
NMETA = 16
PADR = CH - NMETA
EPS = 1e-6
HD = 128
M2P = 64
M2G = 4
NST = 128
LANE = 128

ADAM_LR, ADAM_B1, ADAM_B2, ADAM_EPS, ADAM_WD, ADAM_STEP = 0.001, 0.9, 0.999, 1e-08, 0.01, 10

MESH_AXES = ("x", "y", "c")


def _pick(n, target, mult=16):
    best = None
    for t in range(mult, min(n, target) + 1, mult):
        if n % t == 0:
            best = t
    return best if best is not None else n


def _dg(a, b, ca, cb):
    return lax.dot_general(a.astype(MXU), b.astype(MXU), (((ca,), (cb,)), ((), ())), preferred_element_type=F32)


def _dgh(a, b, ca, cb):
    return lax.dot_general(a, b, (((ca,), (cb,)), ((), ())), precision=HI, preferred_element_type=F32)


def _silu(x):
    return x * jax.nn.sigmoid(x)


def _softplus(x):
    return jnp.maximum(x, 0.0) + jnp.log1p(jnp.exp(-jnp.abs(x)))


def _rms(x, w):
    return x * lax.rsqrt(jnp.mean(x * x, axis=-1, keepdims=True) + EPS) * w


def _cparams(sem, vmem_mb):
    return pltpu.CompilerParams(dimension_semantics=sem, vmem_limit_bytes=vmem_mb << 20)


def _mm(name, a, b, *, ta=False, tb=False, add=None, out_dtype=F32, tm=1040, tn=1024, tk=2048, dep=None,
        b_cols=None):
    m, kdim = (a.shape[1], a.shape[0]) if ta else a.shape
    b_start, b_width = b_cols if b_cols is not None else (0, b.shape[1])
    n = b.shape[0] if tb else b_width
    if tb:
        kdim = b_width
    tm = _pick(m, tm, 128 if ta else 16)
    tn = _pick(n if tb else math.gcd(b_width, b_start), tn, 128)
    tk = _pick(math.gcd(b_width, b_start) if tb else kdim, tk, 16 if (ta and not tb) else 128)
    nk = kdim // tk
    bj0, bk0 = (0, b_start // tk) if tb else (b_start // tn, 0)
    ca, cb = (0 if ta else 1), (1 if tb else 0)

    def body(*refs):
        a_ref, b_ref = refs[0], refs[1]
        add_ref = refs[2] if add is not None else None
        if nk == 1:
            r = _dg(a_ref[...], b_ref[...], ca, cb)
            if add_ref is not None:
                r = r + add_ref[...].astype(F32)
            refs[-1][...] = r.astype(refs[-1].dtype)
            return
        o_ref, acc = refs[-2], refs[-1]
        k = pl.program_id(2)

        @pl.when(k == 0)
        def _():
            acc[...] = jnp.zeros_like(acc)

        acc[...] += _dg(a_ref[...], b_ref[...], ca, cb)

        @pl.when(k == nk - 1)
        def _():
            r = acc[...]
            if add_ref is not None:
                r = r + add_ref[...].astype(F32)
            o_ref[...] = r.astype(o_ref.dtype)

    a_spec = pl.BlockSpec((tk, tm), lambda i, j, k: (k, i)) if ta else pl.BlockSpec((tm, tk), lambda i, j, k: (i, k))
    b_spec = (pl.BlockSpec((tn, tk), lambda i, j, k: (j, k + bk0)) if tb
              else pl.BlockSpec((tk, tn), lambda i, j, k: (k, j + bj0)))
    in_specs, ops = [a_spec, b_spec], [a, b]
    if add is not None:
        in_specs.append(pl.BlockSpec((tm, tn), lambda i, j, k: (i, j)))
        ops.append(add)
    if dep is not None:
        in_specs.append(pl.BlockSpec((8, LANE), lambda i, j, k: (0, 0)))
        ops.append(dep)
    return pl.pallas_call(
        body, name=name, grid=(m // tm, n // tn, nk), in_specs=in_specs,
        out_specs=pl.BlockSpec((tm, tn), lambda i, j, k: (i, j)),
        out_shape=jax.ShapeDtypeStruct((m, n), out_dtype),
        scratch_shapes=[pltpu.VMEM((tm, tn), F32)] if nk > 1 else [],
        compiler_params=_cparams(("parallel", "parallel", "arbitrary"), 48),
    )(*ops)


def _rw(name, fn, ins, outs, nrows, tm, ncol=1, vmem_mb=48):
    nrow = nrows // tm
    sub = tm
    in_specs, ops = [], []
    for spec in ins:
        kind, arr, bw, cj = spec[:4]
        ops.append(arr)
        if kind == "r":
            ri = spec[4] if len(spec) > 4 else (lambda i: i)
            in_specs.append(pl.BlockSpec((tm, bw), lambda j, i, cj=cj, ri=ri: (ri(i), cj(j))))
        else:
            in_specs.append(pl.BlockSpec((arr.shape[0], bw), lambda j, i, cj=cj: (0, cj(j))))
    out_shape, out_specs = [], []
    for o in outs:
        if o[0] == "r":
            _, width, bw, cj, dt = o
            out_shape.append(jax.ShapeDtypeStruct((nrows, width), dt))
            out_specs.append(pl.BlockSpec((tm, bw), lambda j, i, cj=cj: (i, cj(j))))
        else:
            _, rows, width, bw, cj = o
            out_shape.append(jax.ShapeDtypeStruct((rows, width), F32))
            out_specs.append(pl.BlockSpec((rows, bw), lambda j, i, cj=cj: (0, cj(j))))
    n_in = len(ins)

    def body(*refs):
        j, i = pl.program_id(0), pl.program_id(1)
        in_refs, out_refs = refs[:n_in], refs[n_in:]
        pars = [ref[...] if spec[0] == "p" else None for spec, ref in zip(ins, in_refs)]

        def one(r0, nr):
            rows = i * tm + r0 + lax.broadcasted_iota(jnp.int32, (nr, 1), 0)
            vals = [par if spec[0] == "p" else ref[pl.ds(r0, nr), :] for spec, ref, par in zip(ins, in_refs, pars)]
            parts = []
            for o, val, ref in zip(outs, fn(rows, j, *vals), out_refs):
                if o[0] == "r":
                    ref[pl.ds(r0, nr), :] = val.astype(ref.dtype)
                else:
                    parts.append(val)
            return parts

        if sub >= tm:
            parts = one(0, tm)
        else:
            zero = [jnp.zeros((1, o[3]), F32) for o in outs if o[0] == "p"]
            parts = lax.fori_loop(
                0, tm // sub, lambda s, acc: [a + b for a, b in zip(acc, one(pl.multiple_of(s * sub, sub), sub))], zero)
        for ref, val in zip([r for o, r in zip(outs, out_refs) if o[0] == "p"], parts):
            @pl.when(i == 0)
            def _(ref=ref):
                ref[...] = jnp.zeros_like(ref)

            ref[...] += val

    return pl.pallas_call(
        body, name=name, grid=(ncol, nrow), in_specs=in_specs, out_specs=out_specs, out_shape=out_shape,
        compiler_params=_cparams(("parallel", "arbitrary"), vmem_mb),
    )(*ops)


def _c0(j):
    return 0


def _cj(j):
    return j


def _shift(x, s):
    if s == 0:
        return x
    return pltpu.roll(x, s % x.shape[0], 0)


def _conv(x, w):
    k = len(w)
    return functools.reduce(lambda a, b: a + b, [w[j] * _shift(x, k - 1 - j) for j in range(k)])


def _conv_t(dy, w, rows, t_end):
    k = len(w)
    terms = []
    for j in range(k):
        s = k - 1 - j
        v = _shift(dy, -s)
        if t_end is not None and s > 0:
            v = jnp.where(rows + s < t_end, v, 0.0)
        terms.append(w[j] * v)
    return functools.reduce(lambda a, b: a + b, terms)


def _conv_w(dy, x, k):
    return [jnp.sum(dy * _shift(x, k - 1 - j), axis=0, keepdims=True) for j in range(k)]


HALO = 8


def _cv(name, fn, row_ins, par_ins, row_outs, par_outs, nrows, ncol, chunk=None):
    whole = chunk is None
    chunk = nrows if whole else chunk
    n_chunks = nrows // chunk
    assert nrows % chunk == 0 and (whole or n_chunks >= 3)
    n_ri, n_pi, n_ro = len(row_ins), len(par_ins), len(row_outs)

    def body(*refs):
        rin, pin = refs[:n_ri], refs[n_ri:n_ri + n_pi]
        rout, pout = refs[n_ri + n_pi:n_ri + n_pi + n_ro], refs[n_ri + n_pi + n_ro:]
        pars = [[p[pl.ds(r, 1), :] for r in range(p.shape[0])] for p in pin]

        def run(r0, top, bot, last):
            wlen = top + chunk + bot
            w0 = r0 - top if isinstance(r0, int) else pl.multiple_of(r0 - top, HALO)
            local = lax.broadcasted_iota(jnp.int32, (wlen, 1), 0)
            own = jnp.logical_and(local >= top, local < top + chunk)
            outs, parts = fn(w0 + local, own, last, [ref[pl.ds(w0, wlen), :] for ref in rin], pars)
            for ref, val in zip(rout, outs):
                ref[pl.ds(r0, chunk), :] = val[top:top + chunk].astype(ref.dtype)
            return parts

        def add(acc, parts):
            return [[a + b for a, b in zip(ra, rb)] for ra, rb in zip(acc, parts)]

        if whole:
            acc = run(0, 0, 0, True)
        else:
            acc = run(0, 0, HALO, False)
            acc = lax.fori_loop(1, n_chunks - 1,
                                lambda i, a: add(a, run(pl.multiple_of(i * chunk, chunk), HALO, HALO, False)), acc)
            acc = add(acc, run(nrows - chunk, HALO, 0, True))
        for ref, prow in zip(pout, acc):
            for r, v in enumerate(prow):
                ref[pl.ds(r, 1), :] = v

    in_specs = [pl.BlockSpec((nrows, LANE), lambda j, cj=cj: (0, cj(j))) for _, cj in row_ins]
    in_specs += [pl.BlockSpec((a.shape[0], LANE), lambda j, cj=cj: (0, cj(j))) for a, cj in par_ins]
    out_specs = [pl.BlockSpec((nrows, LANE), lambda j, cj=cj: (0, cj(j))) for _, cj, _ in row_outs]
    out_specs += [pl.BlockSpec((k, LANE), lambda j, cj=cj: (0, cj(j))) for k, _, cj in par_outs]
    out_shape = [jax.ShapeDtypeStruct((nrows, width), dt) for width, _, dt in row_outs]
    out_shape += [jax.ShapeDtypeStruct((k, width), F32) for k, width, _ in par_outs]
    return pl.pallas_call(
        body, name=name, grid=(ncol,), in_specs=in_specs, out_specs=out_specs, out_shape=out_shape,
        compiler_params=_cparams(("parallel",), 48),
    )(*[a for a, _ in row_ins], *[a for a, _ in par_ins])


def _tri():
    r = lax.broadcasted_iota(jnp.int32, (CH, CH), 0)
    c = lax.broadcasted_iota(jnp.int32, (CH, CH), 1)
    return r, c


def _col(row):
    r, c = _tri()
    return jnp.sum(jnp.where(r == c, row, 0.0), axis=1, keepdims=True)


def _cumsum_rc(g_r):
    r, c = _tri()
    g_c = _col(g_r)
    cs_r = jnp.sum(jnp.where(r <= c, g_c, 0.0), axis=0, keepdims=True)
    cs_c = jnp.sum(jnp.where(c <= r, g_r, 0.0), axis=1, keepdims=True)
    return cs_r, cs_c


def _decay(cs_r, cs_c):
    r, c = _tri()
    return jnp.exp(jnp.where(c <= r, cs_c - cs_r, -jnp.inf))


def _gdn_a(ks, betas, gs):
    r, c = _tri()
    cs = [_cumsum_rc(g) for g in gs]
    kk = [_dg(k, k, 1, 1) for k in ks]
    return [jnp.where(c < r, _col(b) * kki * _decay(*csi), 0.0) for b, kki, csi in zip(betas, kk, cs)]


def _neumann(a_list):
    r, c = _tri()
    cs = [-a for a in a_list]
    ps = [_dgh(c_, c_, 1, 0) for c_ in cs]
    ss = [jnp.where(r == c, 1.0, 0.0) + c_ for c_ in cs]
    n = 2
    while n < CH:
        if 2 * n < CH:
            ys = [_dgh(p, jnp.concatenate([p, s], axis=1), 1, 0) for p, s in zip(ps, ss)]
            ps = [y[:, :CH] for y in ys]
            ss = [s + y[:, CH:] for s, y in zip(ss, ys)]
        else:
            ss = [s + _dgh(p, s, 1, 0) for p, s in zip(ps, ss)]
        n *= 2
    return ss


def _gdn_rest(ss, qs, ks, vs, betas, gs, ts):
    n = range(len(ss))
    cs = [_cumsum_rc(g) for g in gs]
    dm = [_decay(*csi) for csi in cs]
    ecs = [jnp.exp(csi[1]) for csi in cs]
    bc = [_col(b) for b in betas]
    u = [_dg(ts[i], vs[i] * bc[i], 1, 0) for i in n]
    w = [_dg(ts[i], ks[i] * (bc[i] * ecs[i]), 1, 0) for i in n]
    ws = [_dg(w[i], ss[i], 1, 0) for i in n]
    v_new = [u[i] - ws[i] for i in n]
    qk = [_dg(qs[i], ks[i], 1, 1) * dm[i] for i in n]
    o_in = [_dg(qs[i] * ecs[i], ss[i], 1, 0) for i in n]
    o = [o_in[i] + _dg(qk[i], v_new[i], 1, 0) for i in n]
    g_last = [jnp.sum(g, axis=1, keepdims=True) for g in gs]
    s_new = [ss[i] * jnp.exp(g_last[i]) + _dg(ks[i] * jnp.exp(g_last[i] - cs[i][1]), v_new[i], 0, 0) for i in n]
    return s_new, o


def _gdn_fwd(q, k, v, beta, g, hb):
    t_rows, d = q.shape
    nc, ng, w = t_rows // CH, d // (HD * hb), HD * hb
    sls = [slice(h * HD, (h + 1) * HD) for h in range(hb)]

    def body(q_ref, k_ref, v_ref, b_ref, g_ref, o_ref, ss_ref, ts_ref, s_scr):
        c = pl.program_id(1)

        @pl.when(c == 0)
        def _():
            s_scr[...] = jnp.zeros_like(s_scr)

        qs, ks, vs = ([ref[:, sl] for sl in sls] for ref in (q_ref, k_ref, v_ref))
        br = [b_ref[0, 0, pl.ds(h, 1), :] for h in range(hb)]
        gr = [g_ref[0, 0, pl.ds(h, 1), :] for h in range(hb)]
        s0 = [s_scr[h] for h in range(hb)]
        tm = _neumann(_gdn_a(ks, br, gr))
        s1, o = _gdn_rest(s0, qs, ks, vs, br, gr, tm)
        for h in range(hb):
            ss_ref[0, 0, h] = s0[h]
            ts_ref[0, 0, h] = tm[h]
            o_ref[:, sls[h]] = o[h]
            s_scr[h] = s1[h]

    blk = pl.BlockSpec((CH, w), lambda n, c: (c, n))
    row = pl.BlockSpec((1, 1, hb, CH), lambda n, c: (n, c, 0, 0))
    return pl.pallas_call(
        body, name="gdn_fwd", grid=(ng, nc), in_specs=[blk, blk, blk, row, row],
        out_specs=[blk, pl.BlockSpec((1, 1, hb, HD, HD), lambda n, c: (n, c, 0, 0, 0)),
                   pl.BlockSpec((1, 1, hb, CH, CH), lambda n, c: (n, c, 0, 0, 0))],
        out_shape=[jax.ShapeDtypeStruct((t_rows, d), F32), jax.ShapeDtypeStruct((ng, nc, hb, HD, HD), F32),
                   jax.ShapeDtypeStruct((ng, nc, hb, CH, CH), F32)],
        scratch_shapes=[pltpu.VMEM((hb, HD, HD), F32)],
        compiler_params=_cparams(("parallel", "arbitrary"), 32),
    )(q, k, v, beta, g)


def _gdn_bwd(q, k, v, beta, g, ss, ts, do, hb):
    t_rows, d = q.shape
    nc, ng, w = t_rows // CH, d // (HD * hb), HD * hb
    sls = [slice(h * HD, (h + 1) * HD) for h in range(hb)]

    def body(q_ref, k_ref, v_ref, b_ref, g_ref, ss_ref, ts_ref, do_ref, dq_ref, dk_ref, dv_ref, db_ref, dg_ref, ds_scr):
        cr = pl.program_id(1)

        @pl.when(cr == 0)
        def _():
            ds_scr[...] = jnp.zeros_like(ds_scr)

        first = cr == nc - 1
        rowi = lax.broadcasted_iota(jnp.int32, (CH, 1), 0)
        lani = lax.broadcasted_iota(jnp.int32, (1, CH), 1)
        keep_c = jnp.logical_or(jnp.logical_not(first), rowi >= PADR)
        keep_r = jnp.logical_or(jnp.logical_not(first), lani >= PADR)
        hs = range(hb)
        qs, ks, vs, dos = ([ref[:, sl] for sl in sls] for ref in (q_ref, k_ref, v_ref, do_ref))
        br = [b_ref[0, 0, pl.ds(h, 1), :] for h in hs]
        gr = [g_ref[0, 0, pl.ds(h, 1), :] for h in hs]
        tm = [ts_ref[0, 0, h] for h in hs]
        _, vjp_rest = jax.vjp(_gdn_rest, [ss_ref[0, 0, h] for h in hs], qs, ks, vs, br, gr, tm)
        ds0, dq, dk, dv, db, dg, dt = vjp_rest(([ds_scr[h] for h in hs], dos))
        dtt = [_dgh(dt[h], tm[h], 1, 1) for h in hs]
        da = [-_dgh(tm[h], dtt[h], 0, 0) for h in hs]
        _, vjp_a = jax.vjp(_gdn_a, ks, br, gr)
        dk2, db2, dg2 = vjp_a(da)
        for h in hs:
            ds_scr[h] = ds0[h]
            dq_ref[:, sls[h]] = jnp.where(keep_c, dq[h], 0.0)
            dk_ref[:, sls[h]] = jnp.where(keep_c, dk[h] + dk2[h], 0.0)
            dv_ref[:, sls[h]] = jnp.where(keep_c, dv[h], 0.0)
            db_ref[0, 0, pl.ds(h, 1), :] = jnp.where(keep_r, db[h] + db2[h], 0.0)
            dg_ref[0, 0, pl.ds(h, 1), :] = jnp.where(keep_r, dg[h] + dg2[h], 0.0)

    blk = pl.BlockSpec((CH, w), lambda n, c: (nc - 1 - c, n))
    row = pl.BlockSpec((1, 1, hb, CH), lambda n, c: (n, nc - 1 - c, 0, 0))
    return pl.pallas_call(
        body, name="gdn_bwd", grid=(ng, nc),
        in_specs=[blk, blk, blk, row, row, pl.BlockSpec((1, 1, hb, HD, HD), lambda n, c: (n, nc - 1 - c, 0, 0, 0)),
                  pl.BlockSpec((1, 1, hb, CH, CH), lambda n, c: (n, nc - 1 - c, 0, 0, 0)), blk],
        out_specs=[blk, blk, blk, row, row],
        out_shape=[jax.ShapeDtypeStruct((t_rows, d), F32)] * 3 + [jax.ShapeDtypeStruct((ng, nc, hb, CH), F32)] * 2,
        scratch_shapes=[pltpu.VMEM((hb, HD, HD), F32)],
        compiler_params=_cparams(("parallel", "arbitrary"), 32),
    )(q, k, v, beta, g, ss, ts, do)


def _ssd_group(s, xs, bm, cm, dt_r, a_r):
    prs = range(len(s))
    first = lax.broadcasted_iota(jnp.int32, (1, 2 * M2P), 1) < M2P

    def pick(vals, p):
        return jnp.where(first, vals[2 * p], vals[2 * p + 1])

    cs = [_cumsum_rc(a) for a in a_r]
    lm = [_decay(*csi) for csi in cs]
    ecs = [jnp.exp(csi[1]) for csi in cs]
    alast = [jnp.sum(a, axis=1, keepdims=True) for a in a_r]
    ealast = [jnp.exp(al) for al in alast]
    wt = [jnp.exp(al - csi[1]) for al, csi in zip(alast, cs)]
    dtc = [_col(t) for t in dt_r]
    xdt = [xs[:, p * LANE:(p + 1) * LANE] * pick(dtc, p) for p in prs]
    cb = _dg(cm, bm, 1, 1)
    y0 = [_dg(cb * lm[2 * p], xdt[p], 1, 0) for p in prs]
    y1 = [_dg(cb * lm[2 * p + 1], xdt[p], 1, 0) for p in prs]
    yo = [_dg(cm, s[p], 1, 0) for p in prs]
    y = [jnp.where(first, y0[p], y1[p]) + yo[p] * pick(ecs, p) for p in prs]
    s_new = [s[p] * pick(ealast, p) + _dg(bm, xdt[p] * pick(wt, p), 0, 0) for p in prs]
    return s_new, jnp.concatenate(y, axis=1)


def _ssd_specs(nc, d, rev):
    hpg = (d // M2P) // M2G
    gw = hpg * M2P
    cc = (lambda c: nc - 1 - c) if rev else (lambda c: c)
    xs = pl.BlockSpec((CH, gw), lambda g, c: (cc(c), g))
    bm = pl.BlockSpec((CH, NST), lambda g, c: (cc(c), d // LANE + g))
    cm = pl.BlockSpec((CH, NST), lambda g, c: (cc(c), d // LANE + M2G + g))
    row = pl.BlockSpec((1, 1, hpg, CH), lambda g, c: (g, cc(c), 0, 0))
    st = pl.BlockSpec((1, 1, hpg // 2, NST, LANE), lambda g, c: (g, cc(c), 0, 0, 0))
    return xs, bm, cm, row, st, hpg


def _ssd_fwd(xbc, dt, a, d):
    t_rows = xbc.shape[0]
    nc = t_rows // CH
    xs, bm, cm, row, st, hpg = _ssd_specs(nc, d, False)
    ppg = hpg // 2

    def body(xs_ref, b_ref, c_ref, dt_ref, a_ref, y_ref, ss_ref, s_scr):
        c = pl.program_id(1)

        @pl.when(c == 0)
        def _():
            s_scr[...] = jnp.zeros_like(s_scr)

        s0 = [s_scr[p] for p in range(ppg)]
        for p in range(ppg):
            ss_ref[0, 0, p] = s0[p]
        dt_r = [dt_ref[0, 0, pl.ds(h, 1), :] for h in range(hpg)]
        a_r = [a_ref[0, 0, pl.ds(h, 1), :] for h in range(hpg)]
        s1, y = _ssd_group(s0, xs_ref[...], b_ref[...], c_ref[...], dt_r, a_r)
        y_ref[...] = y
        for p in range(ppg):
            s_scr[p] = s1[p]

    return pl.pallas_call(
        body, name="ssd_fwd", grid=(M2G, nc), in_specs=[xs, bm, cm, row, row], out_specs=[xs, st],
        out_shape=[jax.ShapeDtypeStruct((t_rows, d), F32), jax.ShapeDtypeStruct((M2G, nc, ppg, NST, LANE), F32)],
        scratch_shapes=[pltpu.VMEM((ppg, NST, LANE), F32)],
        compiler_params=_cparams(("parallel", "arbitrary"), 32),
    )(xbc, xbc, xbc, dt, a)


def _ssd_bwd(xbc, dt, a, ss, dy, d):
    t_rows = xbc.shape[0]
    nc = t_rows // CH
    xs, bm, cm, row, st, hpg = _ssd_specs(nc, d, True)
    ppg = hpg // 2

    def body(xs_ref, b_ref, c_ref, dt_ref, a_ref, ss_ref, dy_ref, dxs_ref, db_ref, dc_ref, ddt_ref, da_ref, ds_scr):
        cr = pl.program_id(1)

        @pl.when(cr == 0)
        def _():
            ds_scr[...] = jnp.zeros_like(ds_scr)

        first = cr == nc - 1
        keep_c = jnp.logical_or(jnp.logical_not(first), lax.broadcasted_iota(jnp.int32, (CH, 1), 0) >= PADR)
        keep_r = jnp.logical_or(jnp.logical_not(first), lax.broadcasted_iota(jnp.int32, (1, CH), 1) >= PADR)
        dt_r = [dt_ref[0, 0, pl.ds(h, 1), :] for h in range(hpg)]
        a_r = [a_ref[0, 0, pl.ds(h, 1), :] for h in range(hpg)]
        s0 = [ss_ref[0, 0, p] for p in range(ppg)]
        _, vjp = jax.vjp(_ssd_group, s0, xs_ref[...], b_ref[...], c_ref[...], dt_r, a_r)
        ds0, dxs, db, dc, ddt, da = vjp(([ds_scr[p] for p in range(ppg)], dy_ref[...]))
        for p in range(ppg):
            ds_scr[p] = ds0[p]
        dxs_ref[...] = jnp.where(keep_c, dxs, 0.0)
        db_ref[...] = jnp.where(keep_c, db, 0.0)
        dc_ref[...] = jnp.where(keep_c, dc, 0.0)
        for h in range(hpg):
            ddt_ref[0, 0, pl.ds(h, 1), :] = jnp.where(keep_r, ddt[h], 0.0)
            da_ref[0, 0, pl.ds(h, 1), :] = jnp.where(keep_r, da[h], 0.0)

    grp = pl.BlockSpec((CH, NST), lambda g, c: (nc - 1 - c, g))
    return pl.pallas_call(
        body, name="ssd_bwd", grid=(M2G, nc), in_specs=[xs, bm, cm, row, row, st, xs],
        out_specs=[xs, grp, grp, row, row],
        out_shape=[jax.ShapeDtypeStruct((t_rows, d), F32)] + [jax.ShapeDtypeStruct((t_rows, M2G * NST), F32)] * 2
        + [jax.ShapeDtypeStruct((M2G, nc, hpg, CH), F32)] * 2,
        scratch_shapes=[pltpu.VMEM((ppg, NST, LANE), F32)],
        compiler_params=_cparams(("parallel", "arbitrary"), 32),
    )(xbc, xbc, xbc, dt, a, ss, dy)


def _exchange(name, gathers, scatters, after):
    arrays = list(gathers) + list(scatters)
    n_g, n = len(gathers), len(arrays)

    def body(*refs):
        ins, outs = refs[:n], refs[n + 1:2 * n + 1]
        send_sems, recv_sems, local_sems = refs[2 * n + 1:]
        x, y, c = lax.axis_index("x"), lax.axis_index("y"), lax.axis_index("c")
        me = 4 * x + 2 * y + c

        def src(a, slot):
            return ins[a] if a < n_g else ins[a].at[slot]

        local = [pltpu.make_async_copy(src(a, me), outs[a].at[me], local_sems.at[a]) for a in range(n)]
        for cp in local:
            cp.start()
        copies = []
        for rel in range(1, NDEV):
            px, py, pc = x ^ (rel >> 2), y ^ ((rel >> 1) & 1), c ^ (rel & 1)
            peer = 4 * px + 2 * py + pc
            for a in range(n):
                copies.append(pltpu.make_async_remote_copy(
                    src_ref=src(a, peer), dst_ref=outs[a].at[me], send_sem=send_sems.at[a, rel - 1],
                    recv_sem=recv_sems.at[a, rel - 1], device_id=(px, py, pc), device_id_type=pl.DeviceIdType.MESH))
        for cp in copies:
            cp.start()
        for cp in copies:
            cp.wait_recv()
        for cp in copies:
            cp.wait_send()
        for cp in local:
            cp.wait()

    any_spec = pl.BlockSpec(memory_space=pl.ANY)
    out_shape = [jax.ShapeDtypeStruct((NDEV,) + a.shape, a.dtype) for a in gathers]
    out_shape += [jax.ShapeDtypeStruct(a.shape, a.dtype) for a in scatters]
    return pl.pallas_call(
        body, name=name, in_specs=[any_spec] * (n + 1), out_specs=[any_spec] * n, out_shape=out_shape,
        scratch_shapes=[pltpu.SemaphoreType.DMA((n, NDEV - 1)), pltpu.SemaphoreType.DMA((n, NDEV - 1)),
                        pltpu.SemaphoreType.DMA((n,))],
        compiler_params=pltpu.CompilerParams(has_side_effects=True),
    )(*arrays, after)


def _gather_two_level(name, arrays):
    n = len(arrays)

    def body(*refs):
        ins, outs = refs[:n], refs[n:2 * n]
        send_sems, recv_sems, local_sems = refs[2 * n:]
        x, y, c = lax.axis_index("x"), lax.axis_index("y"), lax.axis_index("c")
        me, sibling = (x, y, c), (x, y, 1 - c)
        chips = [(1 - x, y), (x, 1 - y), (1 - x, 1 - y)]

        def copy(a, k, block, to, src=None):
            dst = outs[a].at[4 * block[0] + 2 * block[1] + block[2]]
            return pltpu.make_async_remote_copy(
                src_ref=dst if src is None else src, dst_ref=dst, send_sem=send_sems.at[a, k], recv_sem=recv_sems.at[a, k],
                device_id=to, device_id_type=pl.DeviceIdType.MESH)

        mine = [pltpu.make_async_copy(ins[a], outs[a].at[4 * x + 2 * y + c], local_sems.at[a]) for a in range(n)]
        for cp in mine:
            cp.start()
        first = []
        for a in range(n):
            first.append(copy(a, 0, me, sibling, src=ins[a]))
            first += [copy(a, 1 + j, me, (*chip, c), src=ins[a]) for j, chip in enumerate(chips)]
        for cp in first:
            cp.start()
        passed = [[copy(a, 4 + j, (*chip, c), sibling) for j, chip in enumerate(chips)] for a in range(n)]
        for j, chip in enumerate(chips):
            for a in range(n):
                copy(a, 1 + j, (*chip, c), me).wait_recv()
                passed[a][j].start()
        for a in range(n):
            copy(a, 0, sibling, me).wait_recv()
            for j, chip in enumerate(chips):
                copy(a, 4 + j, (*chip, 1 - c), me).wait_recv()
        for cp in first + [cp for row in passed for cp in row]:
            cp.wait_send()
        for cp in mine:
            cp.wait()

    any_spec = pl.BlockSpec(memory_space=pl.ANY)
    return pl.pallas_call(
        body, name=name, in_specs=[any_spec] * n, out_specs=[any_spec] * n,
        out_shape=[jax.ShapeDtypeStruct((NDEV,) + a.shape, a.dtype) for a in arrays],
        scratch_shapes=[pltpu.SemaphoreType.DMA((n, NDEV - 1)), pltpu.SemaphoreType.DMA((n, NDEV - 1)),
                        pltpu.SemaphoreType.DMA((n,))],
        compiler_params=pltpu.CompilerParams(has_side_effects=True),
    )(*arrays)


_HBM = pl.BlockSpec(memory_space=pltpu.HBM)
_SEM = pl.BlockSpec(memory_space=pltpu.SEMAPHORE)
_EFFECT = pltpu.SideEffectType.DATAFLOW_SIDE_EFFECTING


def _split_copies(srcs, lands, send_sems, recv_sems, n_g):
    x, y, c = lax.axis_index("x"), lax.axis_index("y"), lax.axis_index("c")
    me = 4 * x + 2 * y + c
    copies = []
    for rel in range(1, NDEV):
        px, py, pc = x ^ (rel >> 2), y ^ ((rel >> 1) & 1), c ^ (rel & 1)
        peer = 4 * px + 2 * py + pc
        for a in range(len(srcs)):
            copies.append(pltpu.make_async_remote_copy(
                src_ref=srcs[a] if a < n_g else srcs[a].at[peer], dst_ref=lands[a].at[me],
                send_sem=send_sems.at[a * (NDEV - 1) + rel - 1], recv_sem=recv_sems.at[a * (NDEV - 1) + rel - 1],
                device_id=(px, py, pc), device_id_type=pl.DeviceIdType.MESH))
    return copies


def _exchange_start(name, gathers, scatters, after):
    arrays = list(gathers) + list(scatters)
    n_g, n = len(gathers), len(arrays)
    lands = [lax.empty((NDEV,) + a.shape, a.dtype) for a in gathers] + [lax.empty(a.shape, a.dtype) for a in scatters]

    def body(*refs):
        send_sems, recv_sems = refs[2 * n + 1], refs[2 * n + 2]
        for cp in _split_copies(refs[:n], refs[n:2 * n], send_sems, recv_sems, n_g):
            cp.start()
        refs[-1][...] = jnp.zeros_like(refs[-1])

    sems = pltpu.SemaphoreType.DMA((n * (NDEV - 1),))
    out = pl.pallas_call(
        body, name=name, in_specs=[_HBM] * (2 * n) + [pl.BlockSpec(memory_space=pl.ANY)],
        out_specs=(_SEM, _SEM, *[_HBM] * (2 * n), pl.BlockSpec(memory_space=pltpu.VMEM)),
        out_shape=(sems, sems, *[pltpu.HBM(a.shape, a.dtype) for a in arrays + lands], jax.ShapeDtypeStruct((8, LANE), F32)),
        input_output_aliases={i: 2 + i for i in range(2 * n)},
        compiler_params=pltpu.CompilerParams(has_side_effects=_EFFECT),
    )(*[pltpu.with_memory_space_constraint(a, pltpu.HBM) for a in arrays + lands], after)
    return out[0], out[1], list(out[2:2 + 2 * n]), out[-1], n_g


def _exchange_wait(name, started, after):
    send_sems, recv_sems, thru, _, n_g = started
    n = len(thru) // 2

    def body(*refs):
        for cp in _split_copies(refs[:n], refs[n:2 * n], refs[2 * n], refs[2 * n + 1], n_g):
            cp.wait_send()
            cp.wait_recv()

    out = pl.pallas_call(
        body, name=name, in_specs=[_HBM] * (2 * n) + [_SEM, _SEM, pl.BlockSpec(memory_space=pl.ANY)],
        out_specs=[_HBM] * (2 * n), out_shape=[pltpu.HBM(a.shape, a.dtype) for a in thru],
        input_output_aliases={i: i for i in range(2 * n)},
        compiler_params=pltpu.CompilerParams(has_side_effects=_EFFECT),
    )(*thru, send_sems, recv_sems, after)
    me = 4 * lax.axis_index("x") + 2 * lax.axis_index("y") + lax.axis_index("c")
    full = []
    for a in range(n):
        own = out[a][None] if a < n_g else lax.dynamic_index_in_dim(out[a], me, 0, keepdims=True)
        full.append(lax.dynamic_update_index_in_dim(out[n + a], own, me, 0))
    return full


def _adamw(name, staged, w, m, v):
    r, c = w.shape
    tr = _pick(r, 256, 8)

    def body(st_ref, w_ref, m_ref, v_ref, g_ref, d_ref, nm_ref, nv_ref):
        g = st_ref[0].astype(F32)
        for k in range(1, NDEV):
            g = g + st_ref[k].astype(F32)
        m_new = ADAM_B1 * m_ref[...] + (1.0 - ADAM_B1) * g
        v_new = ADAM_B2 * v_ref[...] + (1.0 - ADAM_B2) * jnp.square(g)
        m_hat = m_new / (1.0 - ADAM_B1 ** ADAM_STEP)
        v_hat = v_new / (1.0 - ADAM_B2 ** ADAM_STEP)
        g_ref[...] = g
        d_ref[...] = -ADAM_LR * (m_hat / (jnp.sqrt(v_hat) + ADAM_EPS) + ADAM_WD * w_ref[...])
        nm_ref[...] = m_new
        nv_ref[...] = v_new

    blk = pl.BlockSpec((tr, c), lambda i: (i, 0))
    return pl.pallas_call(
        body, name=name, grid=(r // tr,), in_specs=[pl.BlockSpec((NDEV, tr, c), lambda i: (0, i, 0)), blk, blk, blk],
        out_specs=[blk] * 4, out_shape=[jax.ShapeDtypeStruct((r, c), F32)] * 4,
        compiler_params=_cparams(("parallel",), 48),
    )(staged, w, m, v)


def _pack(parts):
    flat = jnp.concatenate([p.reshape(-1).astype(F32) for p in parts])
    pad = (-flat.shape[0]) % (8 * LANE)
    return jnp.pad(flat, (0, pad)).reshape(-1, LANE)


def _unpack(slab, shapes):
    flat, out, off = slab.reshape(-1), [], 0
    for s in shapes:
        n = 1
        for dim in s:
            n *= dim
        out.append(flat[off:off + n].reshape(s))
        off += n
    return out


def _to_shards(full, axis):
    shp = full.shape
    t = full.reshape(shp[:axis] + (NDEV, shp[axis] // NDEV) + shp[axis + 1:])
    return jnp.moveaxis(t, axis, 0)


def _from_shards(g, axis):
    t = jnp.moveaxis(g, 0, axis)
    shp = t.shape
    return t.reshape(shp[:axis] + (shp[axis] * shp[axis + 1],) + shp[axis + 2:])


def kernel(x, meta_tokens, norm_mix_w, w_in, dn_conv_w, dn_a_log, dn_dt_bias, dn_norm_w, m2_conv_w, m2_conv_b, m2_a_log, m2_dt_bias, m2_d, m2_norm_w, w_out, norm_ffn_w, ffn_up, ffn_conv_w, ffn_down, norm_final_w, loss_target, m_meta_tokens, m_norm_mix_w, m_w_in, m_dn_conv_w, m_dn_a_log, m_dn_dt_bias, m_dn_norm_w, m_m2_conv_w, m_m2_conv_b, m_m2_a_log, m_m2_dt_bias, m_m2_d, m_m2_norm_w, m_w_out, m_norm_ffn_w, m_ffn_up, m_ffn_conv_w, m_ffn_down, m_norm_final_w, v_meta_tokens, v_norm_mix_w, v_w_in, v_dn_conv_w, v_dn_a_log, v_dn_dt_bias, v_dn_norm_w, v_m2_conv_w, v_m2_conv_b, v_m2_a_log, v_m2_dt_bias, v_m2_d, v_m2_norm_w, v_w_out, v_norm_ffn_w, v_ffn_up, v_ffn_conv_w, v_ffn_down, v_norm_final_w):
    seq, d = x.shape[1], x.shape[2]
    t_rows = seq + CH
    nc = t_rows // CH
    dnh, m2h = d // HD, d // M2P
    dff = ffn_down.shape[1] * NDEV
    xbc_w = d + 2 * M2G * NST
    assert seq % CH == 0 and d % (2 * M2P * M2G) == 0 and 2 * dnh + m2h <= LANE
    hb = max(h for h in (8, 4, 2, 1) if dnh % h == 0)
    tm_rw = _pick(t_rows, 208, 16)

    small_sharded = [meta_tokens, dn_conv_w[0], m2_conv_w[0], ffn_conv_w[0]]
    small_shapes = [p.shape for p in small_sharded]
    g_win, g_small = _gather_two_level("gather_w_in", [w_in[0].astype(WIRE), _pack(small_sharded)])
    rest = _exchange_start("gather_rest_start", [w_out[0].astype(WIRE), ffn_up[0].astype(WIRE), ffn_down[0].astype(WIRE)], [],
                           g_small)
    win = _from_shards(g_win, 1)
    small_full = [_unpack(g_small[k], small_shapes) for k in range(NDEV)]
    meta_f, dnconv_f, m2conv_f, ffnconv_f = [jnp.concatenate([small_full[k][i] for k in range(NDEV)], axis=-1) for i in range(4)]

    o_z, o_b, o_a = 3 * d, 4 * d, 4 * d + dnh
    o_m2z = 4 * d + 2 * dnh
    o_xbc, o_dt = o_m2z + d, o_m2z + d + xbc_w
    w_all = jnp.concatenate([win[:, :o_b], win[:, o_m2z:o_dt], win[:, o_b:o_m2z], win[:, o_dt:],
                             jnp.zeros((d, LANE - 2 * dnh - m2h), WIRE)], axis=1)
    seg_cols = {"q": (0, d), "k": (d, d), "v": (2 * d, d), "z": (3 * d, d), "m2z": (4 * d, d), "xbc": (5 * d, xbc_w),
                "sm": (5 * d + xbc_w, LANE)}

    h0 = jnp.concatenate([jnp.zeros((PADR, d), F32), meta_f, x[0]], axis=0)
    valid = lambda rows: rows >= PADR

    def norm_fwd(name, h, w):
        return _rw(name, lambda rows, j, hv, wv: (_rms(hv, wv),), [("r", h, d, _c0), ("p", w, d, _c0)],
                   [("r", d, d, _c0, MXU)], t_rows, tm_rw)[0]

    hn1 = norm_fwd("norm_mix", h0, norm_mix_w)
    proj = {s: _mm("proj_" + s, hn1, w_all, b_cols=seg_cols[s], dep=rest[3]) for s in seg_cols}

    def dn_post(sec, cv):
        s = _silu(cv)
        if sec < 2:
            s = s * lax.rsqrt(jnp.sum(s * s, axis=-1, keepdims=True) + EPS)
        if sec == 0:
            s = s * (HD ** -0.5)
        return s

    def dn_prep(sec, name):
        def fn(rows, own, last, wins, pars):
            return [jnp.where(valid(rows), dn_post(sec, _conv(wins[0], pars[0])), 0.0)], []
        wc = dnconv_f[:, sec * d:(sec + 1) * d]
        return _cv("dn_prep_" + name, fn, [(proj[name], _cj)], [(wc, _cj)], [(d, _cj, F32)], [], t_rows, dnh)[0]

    q_act, k_act, v_act = dn_prep(0, "q"), dn_prep(1, "k"), dn_prep(2, "v")

    lane = lambda: lax.broadcasted_iota(jnp.int32, (1, LANE), 1)

    def lanes_of(vec, off):
        return jnp.pad(vec.astype(F32), ((0, 0), (off, LANE - off - vec.shape[1])))

    gate_params = [lanes_of(dn_a_log, dnh), lanes_of(dn_dt_bias, dnh), lanes_of(m2_a_log, 2 * dnh), lanes_of(m2_dt_bias, 2 * dnh)]

    def gates(rows, sm, p_alog, p_dtb, p_malog, p_mdtb):
        ln = lane()
        is_b, is_g = ln < dnh, jnp.logical_and(ln >= dnh, ln < 2 * dnh)
        is_d = jnp.logical_and(ln >= 2 * dnh, ln < 2 * dnh + m2h)
        beta = jax.nn.sigmoid(sm)
        gdec = -jnp.exp(p_alog) * _softplus(sm + p_dtb)
        dt = _softplus(sm + p_mdtb)
        am = dt * (-jnp.exp(p_malog))
        ok = valid(rows)
        g1 = jnp.where(ok, jnp.where(is_b, beta, jnp.where(is_g, gdec, jnp.where(is_d, dt, 0.0))), 0.0)
        g2 = jnp.where(jnp.logical_and(ok, is_d), am, 0.0)
        return g1, g2

    gate_ins = [("r", proj["sm"], LANE, _c0)] + [("p", p, LANE, _c0) for p in gate_params]
    g1, g2 = _rw("gates", lambda rows, j, *a: gates(rows, *a), gate_ins,
                 [("r", LANE, LANE, _c0, F32), ("r", LANE, LANE, _c0, F32)], t_rows, tm_rw)

    def head_rows(cols, per):
        n = cols.shape[1]
        return cols.reshape(nc, CH, n // per, per).transpose(2, 0, 3, 1)

    def head_cols(rows_):
        ngrp, _, per, _ = rows_.shape
        return rows_.transpose(1, 3, 0, 2).reshape(t_rows, ngrp * per)

    beta_r, gdec_r = head_rows(g1[:, :dnh], hb), head_rows(g1[:, dnh:2 * dnh], hb)
    hpg = m2h // M2G
    dt_r, am_r = head_rows(g1[:, 2 * dnh:2 * dnh + m2h], hpg), head_rows(g2[:, 2 * dnh:2 * dnh + m2h], hpg)

    o_dn, dn_states, dn_tinv = _gdn_fwd(q_act, k_act, v_act, beta_r, gdec_r, hb)

    def dn_out(o, z, w):
        outs = []
        for h in range(dnh):
            sl = slice(h * HD, (h + 1) * HD)
            outs.append(_rms(o[:, sl], w) * _silu(z[:, sl]))
        return jnp.concatenate(outs, axis=1)

    mixed_dn = _rw("dn_out", lambda rows, j, o, z, w: (dn_out(o, z, w),),
                   [("r", o_dn, d, _c0), ("r", proj["z"], d, _c0), ("p", dn_norm_w, HD, _c0)], [("r", d, d, _c0, MXU)],
                   t_rows, tm_rw)[0]

    def m2_prep(rows, own, last, wins, pars):
        return [jnp.where(valid(rows), _silu(_conv(wins[0], pars[0]) + pars[1][0]), 0.0)], []

    xbc_act = _cv("m2_prep", m2_prep, [(proj["xbc"], _cj)], [(m2conv_f, _cj), (m2_conv_b, _cj)], [(xbc_w, _cj, F32)], [],
                  t_rows, xbc_w // LANE)[0]
    y_ssd, m2_states = _ssd_fwd(xbc_act, dt_r, am_r, d)

    d_lanes = jnp.repeat(m2_d.astype(F32), M2P, axis=1)
    gw = d // M2G

    def m2_out(ys, xs, z, dl, nw):
        yv = (ys + dl * xs) * _silu(z)
        outs = []
        for gi in range(M2G):
            sl = slice(gi * gw, (gi + 1) * gw)
            outs.append(_rms(yv[:, sl], nw[:, sl]))
        return jnp.concatenate(outs, axis=1)

    m2_out_ins = [("r", y_ssd, d, _c0), ("r", xbc_act, d, _c0), ("r", proj["m2z"], d, _c0), ("p", d_lanes, d, _c0),
                  ("p", m2_norm_w, d, _c0)]
    mixed_m2 = _rw("m2_out", lambda rows, j, *a: (m2_out(*a),), m2_out_ins, [("r", d, d, _c0, MXU)], t_rows, tm_rw)[0]

    mixed = jnp.concatenate([mixed_dn, mixed_m2], axis=1)
    g_wout, g_wup, g_wdown = _exchange_wait("gather_rest_wait", rest, mixed)
    wout = _from_shards(g_wout, 0)
    wup = _from_shards(g_wup, 1)
    wdown = _from_shards(g_wdown, 0)
    up_g, up_v = (0, dff), (dff, dff)
    h1 = _mm("out_proj", mixed, wout, add=h0)
    hn2 = norm_fwd("norm_ffn", h1, norm_ffn_w)
    u_g, u_v = _mm("ffn_up_g", hn2, wup, b_cols=up_g), _mm("ffn_up_v", hn2, wup, b_cols=up_v)
    fc_g, fc_v = ffnconv_f[:, :dff], ffnconv_f[:, dff:]

    def ffn_act(rows, own, last, wins, pars):
        return [jnp.where(valid(rows), _silu(_conv(wins[0], pars[0])) * _conv(wins[1], pars[1]), 0.0)], []

    act = _cv("ffn_act", ffn_act, [(u_g, _cj), (u_v, _cj)], [(fc_g, _cj), (fc_v, _cj)], [(dff, _cj, MXU)], [],
              t_rows, dff // LANE)[0]
    h2 = _mm("ffn_down", act, wdown, add=h1, tk=1408)

    def loss_fn(hv, wf, tgt, rows):
        err = jnp.where(rows >= CH, _rms(hv, wf) - tgt, 0.0)
        return 0.5 * jnp.sum(jnp.mean(err * err, axis=-1, keepdims=True), axis=0, keepdims=True)

    def final(rows, j, hv, wf, tgt):
        loss, vjp = jax.vjp(lambda a, b: loss_fn(a, b, tgt, rows), hv, wf)
        dh, dw = vjp(jnp.ones((1, 1), F32))
        return dh, dh, dw, jnp.broadcast_to(loss, (1, LANE))

    wf2 = norm_final_w.reshape(1, d)
    dh2, dh2_m, d_wf, loss_part = _rw(
        "loss_head", final, [("r", h2, d, _c0), ("p", wf2, d, _c0), ("r", loss_target[0], d, _c0, lambda i: jnp.maximum(i - 1, 0))],
        [("r", d, d, _c0, F32), ("r", d, d, _c0, MXU), ("p", 1, d, d, _c0), ("p", 1, LANE, LANE, _c0)], t_rows, CH)
    loss = lax.psum(loss_part[0, 0], MESH_AXES)

    d_act = _mm("d_act", dh2_m, wdown, tb=True)
    gw_down = _mm("gw_down", act, dh2_m, ta=True, tm=1408, tn=1024, tk=2080, out_dtype=WIRE)
    x_down = _exchange_start("grad_down_start", [], [_to_shards(gw_down, 0).astype(WIRE)], gw_down)

    def t_end(last):
        return t_rows if last else None

    def ffn_act_bwd(rows, own, last, wins, pars):
        (ug, uv, da), (wg, wv) = wins, pars
        cg, cv = _conv(ug, wg), _conv(uv, wv)
        _, vjp = jax.vjp(lambda a, b: _silu(a) * b, cg, cv)
        dcg, dcv = vjp(jnp.where(valid(rows), da, 0.0))
        return ([_conv_t(dcg, wg, rows, t_end(last)), _conv_t(dcv, wv, rows, t_end(last))],
                [_conv_w(jnp.where(own, dcg, 0.0), ug, len(wg)), _conv_w(jnp.where(own, dcv, 0.0), uv, len(wv))])

    kf = fc_g.shape[0]
    du_g, du_v, g_fc_g, g_fc_v = _cv(
        "ffn_act_bwd", ffn_act_bwd, [(u_g, _cj), (u_v, _cj), (d_act, _cj)], [(fc_g, _cj), (fc_v, _cj)],
        [(dff, _cj, MXU), (dff, _cj, MXU)], [(kf, dff, _cj), (kf, dff, _cj)], t_rows, dff // LANE, chunk=CH)
    gw_up_g = _mm("gw_up_g", hn2, du_g, ta=True, tm=1024, tn=1408, tk=2080, out_dtype=WIRE, dep=x_down[3])
    gw_up_v = _mm("gw_up_v", hn2, du_v, ta=True, tm=1024, tn=1408, tk=2080, out_dtype=WIRE)
    gw_up_full = jnp.concatenate([gw_up_g, gw_up_v], axis=1)
    x_up = _exchange_start("grad_up_start", [], [_to_shards(gw_up_full, 1).astype(WIRE)], gw_up_full)
    d_hn2 = _mm("d_hn2_v", du_v, wup, b_cols=up_v, tb=True, tk=1408, dep=x_up[3],
                add=_mm("d_hn2_g", du_g, wup, b_cols=up_g, tb=True, tk=1408, tn=2048, dep=x_up[3]))

    def norm_bwd(name, h, w, dy, dres):
        def fn(rows, j, hv, wv, dyv, dr):
            _, vjp = jax.vjp(_rms, hv, wv)
            dh, dw = vjp(dyv)
            dh = dh + dr
            return dh, dh, dw
        return _rw(name, fn, [("r", h, d, _c0), ("p", w, d, _c0), ("r", dy, d, _c0), ("r", dres, d, _c0)],
                   [("r", d, d, _c0, F32), ("r", d, d, _c0, MXU), ("p", 1, d, d, _c0)], t_rows, tm_rw)

    dh1, dh1_m, g_norm_ffn = norm_bwd("norm_ffn_bwd", h1, norm_ffn_w, d_hn2, dh2)

    gw_out = _mm("gw_out", mixed, dh1_m, ta=True, tm=1024, tn=1024, tk=2080, out_dtype=WIRE)
    x_out = _exchange_start("grad_out_start", [], [_to_shards(gw_out, 0).astype(WIRE)], gw_out)
    d_mixed = _mm("d_mixed", dh1_m, wout, tb=True, dep=x_out[3])

    gw_seg = {}

    def gw_in(seg, dseg_arr):
        gw_seg[seg] = _mm("gw_in_" + seg, hn1, dseg_arr, ta=True, tm=1024, tn=1024, tk=2080, out_dtype=WIRE)

    def m2_out_bwd(rows, j, ys, xs, z, dl, nw, dy):
        _, vjp = jax.vjp(m2_out, ys, xs, z, dl, nw)
        return vjp(dy)

    dy_ssd, dxs_skip, d_m2z, g_d_lanes, g_m2_norm = _rw(
        "m2_out_bwd", m2_out_bwd, m2_out_ins + [("r", d_mixed, d, lambda j: 1)],
        [("r", d, d, _c0, F32), ("r", d, d, _c0, F32), ("r", d, d, _c0, MXU), ("p", 1, d, d, _c0), ("p", 1, d, d, _c0)],
        t_rows, _pick(t_rows, 208, 16))
    gw_in("m2z", d_m2z)

    def fold_heads(vec_ref, out_ref):
        r = lax.broadcasted_iota(jnp.int32, (d, LANE), 0)
        c = lax.broadcasted_iota(jnp.int32, (d, LANE), 1)
        out_ref[...] = _dgh(vec_ref[...], jnp.where(jnp.logical_and(r >= c * M2P, r < (c + 1) * M2P), 1.0, 0.0), 1, 0)

    g_m2_d = pl.pallas_call(fold_heads, name="fold_m2_d", out_shape=jax.ShapeDtypeStruct((1, LANE), F32))(g_d_lanes)

    dxs, db_ssd, dc_ssd, ddt_r, dam_r = _ssd_bwd(xbc_act, dt_r, am_r, m2_states, dy_ssd, d)


    def m2_prep_bwd(rows, own, last, wins, pars):
        (p, *ds), (w, b) = wins, pars
        _, vjp = jax.vjp(_silu, _conv(p, w) + b[0])
        dpre, = vjp(jnp.where(valid(rows), functools.reduce(lambda a_, b_: a_ + b_, ds), 0.0))
        dpre_own = jnp.where(own, dpre, 0.0)
        return [_conv_t(dpre, w, rows, t_end(last))], [_conv_w(dpre_own, p, len(w)), [jnp.sum(dpre_own, axis=0, keepdims=True)]]

    def m2_prep_bwd_call(name, off, width, d_arrs):
        at = lambda j, blk0=off // LANE: blk0 + j
        return _cv(name, m2_prep_bwd, [(proj["xbc"], at)] + [(a, _cj) for a in d_arrs], [(m2conv_f, at), (m2_conv_b, at)],
                   [(width, _cj, MXU)], [(m2conv_f.shape[0], width, _cj), (1, width, _cj)], t_rows, width // LANE, chunk=CH)

    dp_xs, gcw_xs, gcb_xs = m2_prep_bwd_call("m2_prep_bwd_x", 0, d, [dxs, dxs_skip])
    dp_b, gcw_b, gcb_b = m2_prep_bwd_call("m2_prep_bwd_b", d, M2G * NST, [db_ssd])
    dp_c, gcw_c, gcb_c = m2_prep_bwd_call("m2_prep_bwd_c", d + M2G * NST, M2G * NST, [dc_ssd])
    d_pxbc = jnp.concatenate([dp_xs, dp_b, dp_c], axis=1)
    gw_in("xbc", d_pxbc)
    g_m2_conv = jnp.concatenate([gcw_xs, gcw_b, gcw_c], axis=1)
    g_m2_conv_b = jnp.concatenate([gcb_xs, gcb_b, gcb_c], axis=1)

    def dn_out_bwd(rows, j, o, z, w, dy):
        _, vjp = jax.vjp(dn_out, o, z, w)
        return vjp(dy)

    d_o, d_z, g_dn_norm = _rw(
        "dn_out_bwd", dn_out_bwd,
        [("r", o_dn, d, _c0), ("r", proj["z"], d, _c0), ("p", dn_norm_w, HD, _c0), ("r", d_mixed, d, _c0)],
        [("r", d, d, _c0, F32), ("r", d, d, _c0, MXU), ("p", 1, HD, HD, _c0)], t_rows, _pick(t_rows, 208, 16))
    gw_in("z", d_z)

    dq, dk, dv, dbeta_r, dgdec_r = _gdn_bwd(q_act, k_act, v_act, beta_r, gdec_r, dn_states, dn_tinv, d_o, hb)

    def dn_prep_bwd(sec, name, dact):
        def fn(rows, own, last, wins, pars):
            (p, da), (w,) = wins, pars
            _, vjp = jax.vjp(functools.partial(dn_post, sec), _conv(p, w))
            dcv, = vjp(jnp.where(valid(rows), da, 0.0))
            return [_conv_t(dcv, w, rows, t_end(last))], [_conv_w(jnp.where(own, dcv, 0.0), p, len(w))]
        wc = dnconv_f[:, sec * d:(sec + 1) * d]
        return _cv("dn_prep_bwd_" + name, fn, [(proj[name], _cj), (dact, _cj)], [(wc, _cj)], [(d, _cj, MXU)],
                   [(wc.shape[0], d, _cj)], t_rows, dnh, chunk=CH if sec == 2 else None)

    (dp_q, gcw_q), (dp_k, gcw_k), (dp_v, gcw_v) = dn_prep_bwd(0, "q", dq), dn_prep_bwd(1, "k", dk), dn_prep_bwd(2, "v", dv)
    gw_in("q", dp_q), gw_in("k", dp_k), gw_in("v", dp_v)
    g_dn_conv = jnp.concatenate([gcw_q, gcw_k, gcw_v], axis=1)

    zpad = jnp.zeros((t_rows, LANE - 2 * dnh - m2h), F32)
    dg1 = jnp.concatenate([head_cols(dbeta_r), head_cols(dgdec_r), head_cols(ddt_r), zpad], axis=1)
    dg2 = jnp.concatenate([jnp.zeros((t_rows, 2 * dnh), F32), head_cols(dam_r), zpad], axis=1)

    def gates_bwd(rows, j, sm, pa, pb, pc, pd, d1, d2):
        _, vjp = jax.vjp(lambda *a: gates(rows, *a), sm, pa, pb, pc, pd)
        return vjp((d1, d2))

    dp_sm, g_pa, g_pb, g_pc, g_pd = _rw(
        "gates_bwd", gates_bwd, gate_ins + [("r", dg1, LANE, _c0), ("r", dg2, LANE, _c0)],
        [("r", LANE, LANE, _c0, MXU)] + [("p", 1, LANE, LANE, _c0)] * 4, t_rows, tm_rw)

    dseg = {"q": dp_q, "k": dp_k, "v": dp_v, "z": d_z, "m2z": d_m2z, "xbc": d_pxbc, "sm": dp_sm}
    gw_in("sm", dp_sm)
    gsm = gw_seg["sm"]
    gw_in_full = jnp.concatenate([gw_seg["q"], gw_seg["k"], gw_seg["v"], gw_seg["z"], gsm[:, :2 * dnh], gw_seg["m2z"],
                                  gw_seg["xbc"], gsm[:, 2 * dnh:2 * dnh + m2h]], axis=1)
    x_in = _exchange_start("grad_in_start", [], [_to_shards(gw_in_full, 1).astype(WIRE)], gw_in_full)
    d_hn1 = None
    for s in dseg:
        d_hn1 = _mm("d_hn1_" + s, dseg[s], w_all, b_cols=seg_cols[s], tb=True, tk=2048, add=d_hn1, dep=x_in[3])
    dh0, _, g_norm_mix = norm_bwd("norm_mix_bwd", h0, norm_mix_w, d_hn1, dh1)

    g_ffn_conv = jnp.concatenate([g_fc_g, g_fc_v], axis=1)
    small_parts = [_to_shards(dh0[PADR:CH], 1), _to_shards(g_dn_conv, 1), _to_shards(g_m2_conv, 1), _to_shards(g_ffn_conv, 1)]
    small_scatter = jnp.stack([_pack([p[k] for p in small_parts]) for k in range(NDEV)])

    rep_names = ["norm_mix_w", "dn_a_log", "dn_dt_bias", "dn_norm_w", "m2_conv_b", "m2_a_log", "m2_dt_bias", "m2_d",
                 "m2_norm_w", "norm_ffn_w", "norm_final_w"]
    rep_grads = [g_norm_mix, g_pa[:, dnh:2 * dnh], g_pb[:, dnh:2 * dnh], g_dn_norm, g_m2_conv_b, g_pc[:, 2 * dnh:2 * dnh + m2h],
                 g_pd[:, 2 * dnh:2 * dnh + m2h], g_m2_d[:, :m2h], g_m2_norm, g_norm_ffn, d_wf.reshape(d)]

    weights = dict(meta_tokens=meta_tokens, norm_mix_w=norm_mix_w, w_in=w_in, dn_conv_w=dn_conv_w, dn_a_log=dn_a_log,
                   dn_dt_bias=dn_dt_bias, dn_norm_w=dn_norm_w, m2_conv_w=m2_conv_w, m2_conv_b=m2_conv_b, m2_a_log=m2_a_log,
                   m2_dt_bias=m2_dt_bias, m2_d=m2_d, m2_norm_w=m2_norm_w, w_out=w_out, norm_ffn_w=norm_ffn_w, ffn_up=ffn_up,
                   ffn_conv_w=ffn_conv_w, ffn_down=ffn_down, norm_final_w=norm_final_w)
    mom1 = dict(meta_tokens=m_meta_tokens, norm_mix_w=m_norm_mix_w, w_in=m_w_in, dn_conv_w=m_dn_conv_w, dn_a_log=m_dn_a_log,
                dn_dt_bias=m_dn_dt_bias, dn_norm_w=m_dn_norm_w, m2_conv_w=m_m2_conv_w, m2_conv_b=m_m2_conv_b,
                m2_a_log=m_m2_a_log, m2_dt_bias=m_m2_dt_bias, m2_d=m_m2_d, m2_norm_w=m_m2_norm_w, w_out=m_w_out,
                norm_ffn_w=m_norm_ffn_w, ffn_up=m_ffn_up, ffn_conv_w=m_ffn_conv_w, ffn_down=m_ffn_down,
                norm_final_w=m_norm_final_w)
    mom2 = dict(meta_tokens=v_meta_tokens, norm_mix_w=v_norm_mix_w, w_in=v_w_in, dn_conv_w=v_dn_conv_w, dn_a_log=v_dn_a_log,
                dn_dt_bias=v_dn_dt_bias, dn_norm_w=v_dn_norm_w, m2_conv_w=v_m2_conv_w, m2_conv_b=v_m2_conv_b,
                m2_a_log=v_m2_a_log, m2_dt_bias=v_m2_dt_bias, m2_d=v_m2_d, m2_norm_w=v_m2_norm_w, w_out=v_w_out,
                norm_ffn_w=v_norm_ffn_w, ffn_up=v_ffn_up, ffn_conv_w=v_ffn_conv_w, ffn_down=v_ffn_down,
                norm_final_w=v_norm_final_w)
    res = {}

    def adam_big(name, started, after):
        staged, = _exchange_wait("grad_" + name + "_wait", started, after)
        outs = _adamw("adamw_" + name, staged, weights[name][0], mom1[name][0], mom2[name][0])
        res[name] = tuple(o[None] for o in outs)
        return outs[1]

    done = adam_big("ffn_down", x_down, dh0)
    done = adam_big("ffn_up", x_up, done)
    done = adam_big("w_out", x_out, done)
    st_rep, st_small = _exchange("exchange_small_grads", [_pack(rep_grads)], [small_scatter], done)
    adam_big("w_in", x_in, st_small)

    def adam_packed(label, staged, names):
        shapes = [weights[nm].shape for nm in names]
        outs = _adamw(label, staged, *[_pack([src[nm] for nm in names]) for src in (weights, mom1, mom2)])
        unpacked = [_unpack(o, shapes) for o in outs]
        for i, nm in enumerate(names):
            res[nm] = tuple(u[i] for u in unpacked)

    adam_packed("adamw_small_sharded", st_small, ["meta_tokens", "dn_conv_w", "m2_conv_w", "ffn_conv_w"])
    adam_packed("adamw_replicated", st_rep, rep_names)

    order = list(weights)
    grad_x = dh0[CH:][None]
    return (loss, grad_x, *[res[nm][0] for nm in order], *[res[nm][1] for nm in order], *[res[nm][2] for nm in order],
            *[res[nm][3] for nm in order])
```

```python
import functools
import math

import jax
import jax.numpy as jnp
from jax import lax
from jax.experimental import pallas as pl
from jax.experimental.pallas import tpu as pltpu

F32 = jnp.float32
MXU = jnp.bfloat16
WIRE = jnp.bfloat16
HI = lax.Precision.HIGH

NDEV = 8
CH = 64
NMETA = 16
PADR = CH - NMETA
EPS = 1e-6
HD = 128
M2P = 64
M2G = 4
NST = 128
LANE = 128

ADAM_LR, ADAM_B1, ADAM_B2, ADAM_EPS, ADAM_WD, ADAM_STEP = 0.001, 0.9, 0.999, 1e-08, 0.01, 10

MESH_AXES = ("x", "y", "c")


def _pick(n, target, mult=16):
    best = None
    for t in range(mult, min(n, target) + 1, mult):
        if n % t == 0:
            best = t
    return best if best is not None else n


def _dg(a, b, ca, cb):
    return lax.dot_general(a.astype(MXU), b.astype(MXU), (((ca,), (cb,)), ((), ())), preferred_element_type=F32)


def _dgh(a, b, ca, cb):
    return lax.dot_general(a, b, (((ca,), (cb,)), ((), ())), precision=HI, preferred_element_type=F32)


def _silu(x):
    return x * jax.nn.sigmoid(x)


def _softplus(x):
    return jnp.maximum(x, 0.0) + jnp.log1p(jnp.exp(-jnp.abs(x)))


def _rms(x, w):
    return x * lax.rsqrt(jnp.mean(x * x, axis=-1, keepdims=True) + EPS) * w


def _cparams(sem, vmem_mb):
    return pltpu.CompilerParams(dimension_semantics=sem, vmem_limit_bytes=vmem_mb << 20)


def _mm(name, a, b, *, ta=False, tb=False, add=None, out_dtype=F32, tm=1040, tn=1024, tk=2048, dep=None,
        b_cols=None):
    m, kdim = (a.shape[1], a.shape[0]) if ta else a.shape
    b_start, b_width = b_cols if b_cols is not None else (0, b.shape[1])
    n = b.shape[0] if tb else b_width
    if tb:
        kdim = b_width
    tm = _pick(m, tm, 128 if ta else 16)
    tn = _pick(n if tb else math.gcd(b_width, b_start), tn, 128)
    tk = _pick(math.gcd(b_width, b_start) if tb else kdim, tk, 16 if (ta and not tb) else 128)
    nk = kdim // tk
    bj0, bk0 = (0, b_start // tk) if tb else (b_start // tn, 0)
    ca, cb = (0 if ta else 1), (1 if tb else 0)

    def body(*refs):
        a_ref, b_ref = refs[0], refs[1]
        add_ref = refs[2] if add is not None else None
        if nk == 1:
            r = _dg(a_ref[...], b_ref[...], ca, cb)
            if add_ref is not None:
                r = r + add_ref[...].astype(F32)
            refs[-1][...] = r.astype(refs[-1].dtype)
            return
        o_ref, acc = refs[-2], refs[-1]
        k = pl.program_id(2)

        @pl.when(k == 0)
        def _():
            acc[...] = jnp.zeros_like(acc)

        acc[...] += _dg(a_ref[...], b_ref[...], ca, cb)

        @pl.when(k == nk - 1)
        def _():
            r = acc[...]
            if add_ref is not None:
                r = r + add_ref[...].astype(F32)
            o_ref[...] = r.astype(o_ref.dtype)

    a_spec = pl.BlockSpec((tk, tm), lambda i, j, k: (k, i)) if ta else pl.BlockSpec((tm, tk), lambda i, j, k: (i, k))
    b_spec = (pl.BlockSpec((tn, tk), lambda i, j, k: (j, k + bk0)) if tb
              else pl.BlockSpec((tk, tn), lambda i, j, k: (k, j + bj0)))
    in_specs, ops = [a_spec, b_spec], [a, b]
    if add is not None:
        in_specs.append(pl.BlockSpec((tm, tn), lambda i, j, k: (i, j)))
        ops.append(add)
    if dep is not None:
        in_specs.append(pl.BlockSpec((8, LANE), lambda i, j, k: (0, 0)))
        ops.append(dep)
    return pl.pallas_call(
        body, name=name, grid=(m // tm, n // tn, nk), in_specs=in_specs,
        out_specs=pl.BlockSpec((tm, tn), lambda i, j, k: (i, j)),
        out_shape=jax.ShapeDtypeStruct((m, n), out_dtype),
        scratch_shapes=[pltpu.VMEM((tm, tn), F32)] if nk > 1 else [],
        compiler_params=_cparams(("parallel", "parallel", "arbitrary"), 48),
    )(*ops)


def _rw(name, fn, ins, outs, nrows, tm, ncol=1, vmem_mb=48):
    nrow = nrows // tm
    sub = tm
    in_specs, ops = [], []
    for spec in ins:
        kind, arr, bw, cj = spec[:4]
        ops.append(arr)
        if kind == "r":
            ri = spec[4] if len(spec) > 4 else (lambda i: i)
            in_specs.append(pl.BlockSpec((tm, bw), lambda j, i, cj=cj, ri=ri: (ri(i), cj(j))))
        else:
            in_specs.append(pl.BlockSpec((arr.shape[0], bw), lambda j, i, cj=cj: (0, cj(j))))
    out_shape, out_specs = [], []
    for o in outs:
        if o[0] == "r":
            _, width, bw, cj, dt = o
            out_shape.append(jax.ShapeDtypeStruct((nrows, width), dt))
            out_specs.append(pl.BlockSpec((tm, bw), lambda j, i, cj=cj: (i, cj(j))))
        else:
            _, rows, width, bw, cj = o
            out_shape.append(jax.ShapeDtypeStruct((rows, width), F32))
            out_specs.append(pl.BlockSpec((rows, bw), lambda j, i, cj=cj: (0, cj(j))))
    n_in = len(ins)

    def body(*refs):
        j, i = pl.program_id(0), pl.program_id(1)
        in_refs, out_refs = refs[:n_in], refs[n_in:]
        pars = [ref[...] if spec[0] == "p" else None for spec, ref in zip(ins, in_refs)]

        def one(r0, nr):
            rows = i * tm + r0 + lax.broadcasted_iota(jnp.int32, (nr, 1), 0)
            vals = [par if spec[0] == "p" else ref[pl.ds(r0, nr), :] for spec, ref, par in zip(ins, in_refs, pars)]
            parts = []
            for o, val, ref in zip(outs, fn(rows, j, *vals), out_refs):
                if o[0] == "r":
                    ref[pl.ds(r0, nr), :] = val.astype(ref.dtype)
                else:
                    parts.append(val)
            return parts

        if sub >= tm:
            parts = one(0, tm)
        else:
            zero = [jnp.zeros((1, o[3]), F32) for o in outs if o[0] == "p"]
            parts = lax.fori_loop(
                0, tm // sub, lambda s, acc: [a + b for a, b in zip(acc, one(pl.multiple_of(s * sub, sub), sub))], zero)
        for ref, val in zip([r for o, r in zip(outs, out_refs) if o[0] == "p"], parts):
            @pl.when(i == 0)
            def _(ref=ref):
                ref[...] = jnp.zeros_like(ref)

            ref[...] += val

    return pl.pallas_call(
        body, name=name, grid=(ncol, nrow), in_specs=in_specs, out_specs=out_specs, out_shape=out_shape,
        compiler_params=_cparams(("parallel", "arbitrary"), vmem_mb),
    )(*ops)


def _c0(j):
    return 0


def _cj(j):
    return j


def _shift(x, s):
    if s == 0:
        return x
    return pltpu.roll(x, s % x.shape[0], 0)


def _conv(x, w):
    k = len(w)
    return functools.reduce(lambda a, b: a + b, [w[j] * _shift(x, k - 1 - j) for j in range(k)])


def _conv_t(dy, w, rows, t_end):
    k = len(w)
    terms = []
    for j in range(k):
        s = k - 1 - j
        v = _shift(dy, -s)
        if t_end is not None and s > 0:
            v = jnp.where(rows + s < t_end, v, 0.0)
        terms.append(w[j] * v)
    return functools.reduce(lambda a, b: a + b, terms)


def _conv_w(dy, x, k):
    return [jnp.sum(dy * _shift(x, k - 1 - j), axis=0, keepdims=True) for j in range(k)]


CONV_CHUNK = 320
HALO = 8


def _cv(name, fn, row_ins, par_ins, row_outs, par_outs, nrows, ncol, chunk=None):
    whole = chunk is None
    chunk = nrows if whole else chunk
    n_chunks = nrows // chunk
    assert nrows % chunk == 0 and (whole or n_chunks >= 3)
    n_ri, n_pi, n_ro = len(row_ins), len(par_ins), len(row_outs)

    def body(*refs):
        rin, pin = refs[:n_ri], refs[n_ri:n_ri + n_pi]
        rout, pout = refs[n_ri + n_pi:n_ri + n_pi + n_ro], refs[n_ri + n_pi + n_ro:]
        pars = [[p[pl.ds(r, 1), :] for r in range(p.shape[0])] for p in pin]

        def run(r0, top, bot, last):
            wlen = top + chunk + bot
            w0 = r0 - top if isinstance(r0, int) else pl.multiple_of(r0 - top, HALO)
            local = lax.broadcasted_iota(jnp.int32, (wlen, 1), 0)
            own = jnp.logical_and(local >= top, local < top + chunk)
            outs, parts = fn(w0 + local, own, last, [ref[pl.ds(w0, wlen), :] for ref in rin], pars)
            for ref, val in zip(rout, outs):
                ref[pl.ds(r0, chunk), :] = val[top:top + chunk].astype(ref.dtype)
            return parts

        def add(acc, parts):
            return [[a + b for a, b in zip(ra, rb)] for ra, rb in zip(acc, parts)]

        if whole:
            acc = run(0, 0, 0, True)
        else:
            acc = run(0, 0, HALO, False)
            acc = lax.fori_loop(1, n_chunks - 1,
                                lambda i, a: add(a, run(pl.multiple_of(i * chunk, chunk), HALO, HALO, False)), acc)
            acc = add(acc, run(nrows - chunk, HALO, 0, True))
        for ref, prow in zip(pout, acc):
            for r, v in enumerate(prow):
                ref[pl.ds(r, 1), :] = v

    in_specs = [pl.BlockSpec((nrows, LANE), lambda j, cj=cj: (0, cj(j))) for _, cj in row_ins]
    in_specs += [pl.BlockSpec((a.shape[0], LANE), lambda j, cj=cj: (0, cj(j))) for a, cj in par_ins]
    out_specs = [pl.BlockSpec((nrows, LANE), lambda j, cj=cj: (0, cj(j))) for _, cj, _ in row_outs]
    out_specs += [pl.BlockSpec((k, LANE), lambda j, cj=cj: (0, cj(j))) for k, _, cj in par_outs]
    out_shape = [jax.ShapeDtypeStruct((nrows, width), dt) for width, _, dt in row_outs]
    out_shape += [jax.ShapeDtypeStruct((k, width), F32) for k, width, _ in par_outs]
    return pl.pallas_call(
        body, name=name, grid=(ncol,), in_specs=in_specs, out_specs=out_specs, out_shape=out_shape,
        compiler_params=_cparams(("parallel",), 48),
    )(*[a for a, _ in row_ins], *[a for a, _ in par_ins])


def _tri():
    r = lax.broadcasted_iota(jnp.int32, (CH, CH), 0)
    c = lax.broadcasted_iota(jnp.int32, (CH, CH), 1)
    return r, c


def _col(row):
    r, c = _tri()
    return jnp.sum(jnp.where(r == c, row, 0.0), axis=1, keepdims=True)


def _cumsum_rc(g_r):
    r, c = _tri()
    g_c = _col(g_r)
    cs_r = jnp.sum(jnp.where(r <= c, g_c, 0.0), axis=0, keepdims=True)
    cs_c = jnp.sum(jnp.where(c <= r, g_r, 0.0), axis=1, keepdims=True)
    return cs_r, cs_c


def _decay(cs_r, cs_c):
    r, c = _tri()
    return jnp.exp(jnp.where(c <= r, cs_c - cs_r, -jnp.inf))


def _gdn_a(ks, betas, gs):
    r, c = _tri()
    cs = [_cumsum_rc(g) for g in gs]
    kk = [_dg(k, k, 1, 1) for k in ks]
    return [jnp.where(c < r, _col(b) * kki * _decay(*csi), 0.0) for b, kki, csi in zip(betas, kk, cs)]


def _neumann(a_list):
    r, c = _tri()
    xs = [jnp.where(r == c, 1.0, 0.0) - a for a in a_list]
    ps = list(a_list)
    n = 2
    while n < CH:
        ps = [_dgh(p, p, 1, 0) for p in ps]
        xs = [x + _dgh(x, p, 1, 0) for x, p in zip(xs, ps)]
        n *= 2
    return xs


def _gdn_rest(ss, qs, ks, vs, betas, gs, ts):
    n = range(len(ss))
    cs = [_cumsum_rc(g) for g in gs]
    dm = [_decay(*csi) for csi in cs]
    ecs = [jnp.exp(csi[1]) for csi in cs]
    bc = [_col(b) for b in betas]
    u = [_dgh(ts[i], vs[i] * bc[i], 1, 0) for i in n]
    w = [_dgh(ts[i], ks[i] * (bc[i] * ecs[i]), 1, 0) for i in n]
    ws = [_dg(w[i], ss[i], 1, 0) for i in n]
    v_new = [u[i] - ws[i] for i in n]
    qk = [_dg(qs[i], ks[i], 1, 1) * dm[i] for i in n]
    o_in = [_dg(qs[i] * ecs[i], ss[i], 1, 0) for i in n]
    o = [o_in[i] + _dg(qk[i], v_new[i], 1, 0) for i in n]
    g_last = [jnp.sum(g, axis=1, keepdims=True) for g in gs]
    s_new = [ss[i] * jnp.exp(g_last[i]) + _dg(ks[i] * jnp.exp(g_last[i] - cs[i][1]), v_new[i], 0, 0) for i in n]
    return s_new, o


def _gdn_fwd(q, k, v, beta, g, hb):
    t_rows, d = q.shape
    nc, ng, w = t_rows // CH, d // (HD * hb), HD * hb
    sls = [slice(h * HD, (h + 1) * HD) for h in range(hb)]

    def body(q_ref, k_ref, v_ref, b_ref, g_ref, o_ref, ss_ref, ts_ref, s_scr):
        c = pl.program_id(1)

        @pl.when(c == 0)
        def _():
            s_scr[...] = jnp.zeros_like(s_scr)

        qs, ks, vs = ([ref[:, sl] for sl in sls] for ref in (q_ref, k_ref, v_ref))
        br = [b_ref[0, 0, pl.ds(h, 1), :] for h in range(hb)]
        gr = [g_ref[0, 0, pl.ds(h, 1), :] for h in range(hb)]
        s0 = [s_scr[h] for h in range(hb)]
        tm = _neumann(_gdn_a(ks, br, gr))
        s1, o = _gdn_rest(s0, qs, ks, vs, br, gr, tm)
        for h in range(hb):
            ss_ref[0, 0, h] = s0[h]
            ts_ref[0, 0, h] = tm[h]
            o_ref[:, sls[h]] = o[h]
            s_scr[h] = s1[h]

    blk = pl.BlockSpec((CH, w), lambda n, c: (c, n))
    row = pl.BlockSpec((1, 1, hb, CH), lambda n, c: (n, c, 0, 0))
    return pl.pallas_call(
        body, name="gdn_fwd", grid=(ng, nc), in_specs=[blk, blk, blk, row, row],
        out_specs=[blk, pl.BlockSpec((1, 1, hb, HD, HD), lambda n, c: (n, c, 0, 0, 0)),
                   pl.BlockSpec((1, 1, hb, CH, CH), lambda n, c: (n, c, 0, 0, 0))],
        out_shape=[jax.ShapeDtypeStruct((t_rows, d), F32), jax.ShapeDtypeStruct((ng, nc, hb, HD, HD), F32),
                   jax.ShapeDtypeStruct((ng, nc, hb, CH, CH), F32)],
        scratch_shapes=[pltpu.VMEM((hb, HD, HD), F32)],
        compiler_params=_cparams(("parallel", "arbitrary"), 32),
    )(q, k, v, beta, g)


def _gdn_bwd(q, k, v, beta, g, ss, ts, do, hb):
    t_rows, d = q.shape
    nc, ng, w = t_rows // CH, d // (HD * hb), HD * hb
    sls = [slice(h * HD, (h + 1) * HD) for h in range(hb)]

    def body(q_ref, k_ref, v_ref, b_ref, g_ref, ss_ref, ts_ref, do_ref, dq_ref, dk_ref, dv_ref, db_ref, dg_ref, ds_scr):
        cr = pl.program_id(1)

        @pl.when(cr == 0)
        def _():
            ds_scr[...] = jnp.zeros_like(ds_scr)

        first = cr == nc - 1
        rowi = lax.broadcasted_iota(jnp.int32, (CH, 1), 0)
        lani = lax.broadcasted_iota(jnp.int32, (1, CH), 1)
        keep_c = jnp.logical_or(jnp.logical_not(first), rowi >= PADR)
        keep_r = jnp.logical_or(jnp.logical_not(first), lani >= PADR)
        hs = range(hb)
        qs, ks, vs, dos = ([ref[:, sl] for sl in sls] for ref in (q_ref, k_ref, v_ref, do_ref))
        br = [b_ref[0, 0, pl.ds(h, 1), :] for h in hs]
        gr = [g_ref[0, 0, pl.ds(h, 1), :] for h in hs]
        tm = [ts_ref[0, 0, h] for h in hs]
        _, vjp_rest = jax.vjp(_gdn_rest, [ss_ref[0, 0, h] for h in hs], qs, ks, vs, br, gr, tm)
        ds0, dq, dk, dv, db, dg, dt = vjp_rest(([ds_scr[h] for h in hs], dos))
        dtt = [_dgh(dt[h], tm[h], 1, 1) for h in hs]
        da = [-_dgh(tm[h], dtt[h], 0, 0) for h in hs]
        _, vjp_a = jax.vjp(_gdn_a, ks, br, gr)
        dk2, db2, dg2 = vjp_a(da)
        for h in hs:
            ds_scr[h] = ds0[h]
            dq_ref[:, sls[h]] = jnp.where(keep_c, dq[h], 0.0)
            dk_ref[:, sls[h]] = jnp.where(keep_c, dk[h] + dk2[h], 0.0)
            dv_ref[:, sls[h]] = jnp.where(keep_c, dv[h], 0.0)
            db_ref[0, 0, pl.ds(h, 1), :] = jnp.where(keep_r, db[h] + db2[h], 0.0)
            dg_ref[0, 0, pl.ds(h, 1), :] = jnp.where(keep_r, dg[h] + dg2[h], 0.0)

    blk = pl.BlockSpec((CH, w), lambda n, c: (nc - 1 - c, n))
    row = pl.BlockSpec((1, 1, hb, CH), lambda n, c: (n, nc - 1 - c, 0, 0))
    return pl.pallas_call(
        body, name="gdn_bwd", grid=(ng, nc),
        in_specs=[blk, blk, blk, row, row, pl.BlockSpec((1, 1, hb, HD, HD), lambda n, c: (n, nc - 1 - c, 0, 0, 0)),
                  pl.BlockSpec((1, 1, hb, CH, CH), lambda n, c: (n, nc - 1 - c, 0, 0, 0)), blk],
        out_specs=[blk, blk, blk, row, row],
        out_shape=[jax.ShapeDtypeStruct((t_rows, d), F32)] * 3 + [jax.ShapeDtypeStruct((ng, nc, hb, CH), F32)] * 2,
        scratch_shapes=[pltpu.VMEM((hb, HD, HD), F32)],
        compiler_params=_cparams(("parallel", "arbitrary"), 32),
    )(q, k, v, beta, g, ss, ts, do)


def _ssd_group(s, xs, bm, cm, dt_r, a_r):
    prs = range(len(s))
    first = lax.broadcasted_iota(jnp.int32, (1, 2 * M2P), 1) < M2P

    def pick(vals, p):
        return jnp.where(first, vals[2 * p], vals[2 * p + 1])

    cs = [_cumsum_rc(a) for a in a_r]
    lm = [_decay(*csi) for csi in cs]
    ecs = [jnp.exp(csi[1]) for csi in cs]
    alast = [jnp.sum(a, axis=1, keepdims=True) for a in a_r]
    ealast = [jnp.exp(al) for al in alast]
    wt = [jnp.exp(al - csi[1]) for al, csi in zip(alast, cs)]
    dtc = [_col(t) for t in dt_r]
    xdt = [xs[:, p * LANE:(p + 1) * LANE] * pick(dtc, p) for p in prs]
    cb = _dg(cm, bm, 1, 1)
    y0 = [_dg(cb * lm[2 * p], xdt[p], 1, 0) for p in prs]
    y1 = [_dg(cb * lm[2 * p + 1], xdt[p], 1, 0) for p in prs]
    yo = [_dg(cm, s[p], 1, 0) for p in prs]
    y = [jnp.where(first, y0[p], y1[p]) + yo[p] * pick(ecs, p) for p in prs]
    s_new = [s[p] * pick(ealast, p) + _dg(bm, xdt[p] * pick(wt, p), 0, 0) for p in prs]
    return s_new, jnp.concatenate(y, axis=1)


def _ssd_specs(nc, d, rev):
    hpg = (d // M2P) // M2G
    gw = hpg * M2P
    cc = (lambda c: nc - 1 - c) if rev else (lambda c: c)
    xs = pl.BlockSpec((CH, gw), lambda g, c: (cc(c), g))
    bm = pl.BlockSpec((CH, NST), lambda g, c: (cc(c), d // LANE + g))
    cm = pl.BlockSpec((CH, NST), lambda g, c: (cc(c), d // LANE + M2G + g))
    row = pl.BlockSpec((1, 1, hpg, CH), lambda g, c: (g, cc(c), 0, 0))
    st = pl.BlockSpec((1, 1, hpg // 2, NST, LANE), lambda g, c: (g, cc(c), 0, 0, 0))
    return xs, bm, cm, row, st, hpg


def _ssd_fwd(xbc, dt, a, d):
    t_rows = xbc.shape[0]
    nc = t_rows // CH
    xs, bm, cm, row, st, hpg = _ssd_specs(nc, d, False)
    ppg = hpg // 2

    def body(xs_ref, b_ref, c_ref, dt_ref, a_ref, y_ref, ss_ref, s_scr):
        c = pl.program_id(1)

        @pl.when(c == 0)
        def _():
            s_scr[...] = jnp.zeros_like(s_scr)

        s0 = [s_scr[p] for p in range(ppg)]
        for p in range(ppg):
            ss_ref[0, 0, p] = s0[p]
        dt_r = [dt_ref[0, 0, pl.ds(h, 1), :] for h in range(hpg)]
        a_r = [a_ref[0, 0, pl.ds(h, 1), :] for h in range(hpg)]
        s1, y = _ssd_group(s0, xs_ref[...], b_ref[...], c_ref[...], dt_r, a_r)
        y_ref[...] = y
        for p in range(ppg):
            s_scr[p] = s1[p]

    return pl.pallas_call(
        body, name="ssd_fwd", grid=(M2G, nc), in_specs=[xs, bm, cm, row, row], out_specs=[xs, st],
        out_shape=[jax.ShapeDtypeStruct((t_rows, d), F32), jax.ShapeDtypeStruct((M2G, nc, ppg, NST, LANE), F32)],
        scratch_shapes=[pltpu.VMEM((ppg, NST, LANE), F32)],
        compiler_params=_cparams(("parallel", "arbitrary"), 32),
    )(xbc, xbc, xbc, dt, a)


def _ssd_bwd(xbc, dt, a, ss, dy, d):
    t_rows = xbc.shape[0]
    nc = t_rows // CH
    xs, bm, cm, row, st, hpg = _ssd_specs(nc, d, True)
    ppg = hpg // 2

    def body(xs_ref, b_ref, c_ref, dt_ref, a_ref, ss_ref, dy_ref, dxs_ref, db_ref, dc_ref, ddt_ref, da_ref, ds_scr):
        cr = pl.program_id(1)

        @pl.when(cr == 0)
        def _():
            ds_scr[...] = jnp.zeros_like(ds_scr)

        first = cr == nc - 1
        keep_c = jnp.logical_or(jnp.logical_not(first), lax.broadcasted_iota(jnp.int32, (CH, 1), 0) >= PADR)
        keep_r = jnp.logical_or(jnp.logical_not(first), lax.broadcasted_iota(jnp.int32, (1, CH), 1) >= PADR)
        dt_r = [dt_ref[0, 0, pl.ds(h, 1), :] for h in range(hpg)]
        a_r = [a_ref[0, 0, pl.ds(h, 1), :] for h in range(hpg)]
        s0 = [ss_ref[0, 0, p] for p in range(ppg)]
        _, vjp = jax.vjp(_ssd_group, s0, xs_ref[...], b_ref[...], c_ref[...], dt_r, a_r)
        ds0, dxs, db, dc, ddt, da = vjp(([ds_scr[p] for p in range(ppg)], dy_ref[...]))
        for p in range(ppg):
            ds_scr[p] = ds0[p]
        dxs_ref[...] = jnp.where(keep_c, dxs, 0.0)
        db_ref[...] = jnp.where(keep_c, db, 0.0)
        dc_ref[...] = jnp.where(keep_c, dc, 0.0)
        for h in range(hpg):
            ddt_ref[0, 0, pl.ds(h, 1), :] = jnp.where(keep_r, ddt[h], 0.0)
            da_ref[0, 0, pl.ds(h, 1), :] = jnp.where(keep_r, da[h], 0.0)

    grp = pl.BlockSpec((CH, NST), lambda g, c: (nc - 1 - c, g))
    return pl.pallas_call(
        body, name="ssd_bwd", grid=(M2G, nc), in_specs=[xs, bm, cm, row, row, st, xs],
        out_specs=[xs, grp, grp, row, row],
        out_shape=[jax.ShapeDtypeStruct((t_rows, d), F32)] + [jax.ShapeDtypeStruct((t_rows, M2G * NST), F32)] * 2
        + [jax.ShapeDtypeStruct((M2G, nc, hpg, CH), F32)] * 2,
        scratch_shapes=[pltpu.VMEM((ppg, NST, LANE), F32)],
        compiler_params=_cparams(("parallel", "arbitrary"), 32),
    )(xbc, xbc, xbc, dt, a, ss, dy)


def _exchange(name, gathers, scatters, after):
    arrays = list(gathers) + list(scatters)
    n_g, n = len(gathers), len(arrays)

    def body(*refs):
        ins, outs = refs[:n], refs[n + 1:2 * n + 1]
        send_sems, recv_sems, local_sems = refs[2 * n + 1:]
        x, y, c = lax.axis_index("x"), lax.axis_index("y"), lax.axis_index("c")
        me = 4 * x + 2 * y + c

        def src(a, slot):
            return ins[a] if a < n_g else ins[a].at[slot]

        local = [pltpu.make_async_copy(src(a, me), outs[a].at[me], local_sems.at[a]) for a in range(n)]
        for cp in local:
            cp.start()
        copies = []
        for rel in range(1, NDEV):
            px, py, pc = x ^ (rel >> 2), y ^ ((rel >> 1) & 1), c ^ (rel & 1)
            peer = 4 * px + 2 * py + pc
            for a in range(n):
                copies.append(pltpu.make_async_remote_copy(
                    src_ref=src(a, peer), dst_ref=outs[a].at[me], send_sem=send_sems.at[a, rel - 1],
                    recv_sem=recv_sems.at[a, rel - 1], device_id=(px, py, pc), device_id_type=pl.DeviceIdType.MESH))
        for cp in copies:
            cp.start()
        for cp in copies:
            cp.wait_recv()
        for cp in copies:
            cp.wait_send()
        for cp in local:
            cp.wait()

    any_spec = pl.BlockSpec(memory_space=pl.ANY)
    out_shape = [jax.ShapeDtypeStruct((NDEV,) + a.shape, a.dtype) for a in gathers]
    out_shape += [jax.ShapeDtypeStruct(a.shape, a.dtype) for a in scatters]
    return pl.pallas_call(
        body, name=name, in_specs=[any_spec] * (n + 1), out_specs=[any_spec] * n, out_shape=out_shape,
        scratch_shapes=[pltpu.SemaphoreType.DMA((n, NDEV - 1)), pltpu.SemaphoreType.DMA((n, NDEV - 1)),
                        pltpu.SemaphoreType.DMA((n,))],
        compiler_params=pltpu.CompilerParams(has_side_effects=True),
    )(*arrays, after)


def _gather_two_level(name, arrays):
    n = len(arrays)

    def body(*refs):
        ins, outs = refs[:n], refs[n:2 * n]
        send_sems, recv_sems, local_sems = refs[2 * n:]
        x, y, c = lax.axis_index("x"), lax.axis_index("y"), lax.axis_index("c")
        me, sibling = (x, y, c), (x, y, 1 - c)
        chips = [(1 - x, y), (x, 1 - y), (1 - x, 1 - y)]

        def copy(a, k, block, to, src=None):
            dst = outs[a].at[4 * block[0] + 2 * block[1] + block[2]]
            return pltpu.make_async_remote_copy(
                src_ref=dst if src is None else src, dst_ref=dst, send_sem=send_sems.at[a, k], recv_sem=recv_sems.at[a, k],
                device_id=to, device_id_type=pl.DeviceIdType.MESH)

        mine = [pltpu.make_async_copy(ins[a], outs[a].at[4 * x + 2 * y + c], local_sems.at[a]) for a in range(n)]
        for cp in mine:
            cp.start()
        first = []
        for a in range(n):
            first.append(copy(a, 0, me, sibling, src=ins[a]))
            first += [copy(a, 1 + j, me, (*chip, c), src=ins[a]) for j, chip in enumerate(chips)]
        for cp in first:
            cp.start()
        passed = [[copy(a, 4 + j, (*chip, c), sibling) for j, chip in enumerate(chips)] for a in range(n)]
        for j, chip in enumerate(chips):
            for a in range(n):
                copy(a, 1 + j, (*chip, c), me).wait_recv()
                passed[a][j].start()
        for a in range(n):
            copy(a, 0, sibling, me).wait_recv()
            for j, chip in enumerate(chips):
                copy(a, 4 + j, (*chip, 1 - c), me).wait_recv()
        for cp in first + [cp for row in passed for cp in row]:
            cp.wait_send()
        for cp in mine:
            cp.wait()

    any_spec = pl.BlockSpec(memory_space=pl.ANY)
    return pl.pallas_call(
        body, name=name, in_specs=[any_spec] * n, out_specs=[any_spec] * n,
        out_shape=[jax.ShapeDtypeStruct((NDEV,) + a.shape, a.dtype) for a in arrays],
        scratch_shapes=[pltpu.SemaphoreType.DMA((n, NDEV - 1)), pltpu.SemaphoreType.DMA((n, NDEV - 1)),
                        pltpu.SemaphoreType.DMA((n,))],
        compiler_params=pltpu.CompilerParams(has_side_effects=True),
    )(*arrays)


_HBM = pl.BlockSpec(memory_space=pltpu.HBM)
_SEM = pl.BlockSpec(memory_space=pltpu.SEMAPHORE)
_EFFECT = pltpu.SideEffectType.DATAFLOW_SIDE_EFFECTING


def _split_copies(srcs, lands, send_sems, recv_sems, n_g):
    x, y, c = lax.axis_index("x"), lax.axis_index("y"), lax.axis_index("c")
    me = 4 * x + 2 * y + c
    copies = []
    for rel in range(1, NDEV):
        px, py, pc = x ^ (rel >> 2), y ^ ((rel >> 1) & 1), c ^ (rel & 1)
        peer = 4 * px + 2 * py + pc
        for a in range(len(srcs)):
            copies.append(pltpu.make_async_remote_copy(
                src_ref=srcs[a] if a < n_g else srcs[a].at[peer], dst_ref=lands[a].at[me],
                send_sem=send_sems.at[a * (NDEV - 1) + rel - 1], recv_sem=recv_sems.at[a * (NDEV - 1) + rel - 1],
                device_id=(px, py, pc), device_id_type=pl.DeviceIdType.MESH))
    return copies


def _exchange_start(name, gathers, scatters, after):
    arrays = list(gathers) + list(scatters)
    n_g, n = len(gathers), len(arrays)
    lands = [lax.empty((NDEV,) + a.shape, a.dtype) for a in gathers] + [lax.empty(a.shape, a.dtype) for a in scatters]

    def body(*refs):
        send_sems, recv_sems = refs[2 * n + 1], refs[2 * n + 2]
        for cp in _split_copies(refs[:n], refs[n:2 * n], send_sems, recv_sems, n_g):
            cp.start()
        refs[-1][...] = jnp.zeros_like(refs[-1])

    sems = pltpu.SemaphoreType.DMA((n * (NDEV - 1),))
    out = pl.pallas_call(
        body, name=name, in_specs=[_HBM] * (2 * n) + [pl.BlockSpec(memory_space=pl.ANY)],
        out_specs=(_SEM, _SEM, *[_HBM] * (2 * n), pl.BlockSpec(memory_space=pltpu.VMEM)),
        out_shape=(sems, sems, *[pltpu.HBM(a.shape, a.dtype) for a in arrays + lands], jax.ShapeDtypeStruct((8, LANE), F32)),
        input_output_aliases={i: 2 + i for i in range(2 * n)},
        compiler_params=pltpu.CompilerParams(has_side_effects=_EFFECT),
    )(*[pltpu.with_memory_space_constraint(a, pltpu.HBM) for a in arrays + lands], after)
    return out[0], out[1], list(out[2:2 + 2 * n]), out[-1], n_g


def _exchange_wait(name, started, after):
    send_sems, recv_sems, thru, _, n_g = started
    n = len(thru) // 2

    def body(*refs):
        for cp in _split_copies(refs[:n], refs[n:2 * n], refs[2 * n], refs[2 * n + 1], n_g):
            cp.wait_send()
            cp.wait_recv()

    out = pl.pallas_call(
        body, name=name, in_specs=[_HBM] * (2 * n) + [_SEM, _SEM, pl.BlockSpec(memory_space=pl.ANY)],
        out_specs=[_HBM] * (2 * n), out_shape=[pltpu.HBM(a.shape, a.dtype) for a in thru],
        input_output_aliases={i: i for i in range(2 * n)},
        compiler_params=pltpu.CompilerParams(has_side_effects=_EFFECT),
    )(*thru, send_sems, recv_sems, after)
    me = 4 * lax.axis_index("x") + 2 * lax.axis_index("y") + lax.axis_index("c")
    full = []
    for a in range(n):
        own = out[a][None] if a < n_g else lax.dynamic_index_in_dim(out[a], me, 0, keepdims=True)
        full.append(lax.dynamic_update_index_in_dim(out[n + a], own, me, 0))
    return full


def _adamw(name, staged, w, m, v):
    r, c = w.shape
    tr = _pick(r, 256, 8)

    def body(st_ref, w_ref, m_ref, v_ref, g_ref, d_ref, nm_ref, nv_ref):
        g = st_ref[0].astype(F32)
        for k in range(1, NDEV):
            g = g + st_ref[k].astype(F32)
        m_new = ADAM_B1 * m_ref[...] + (1.0 - ADAM_B1) * g
        v_new = ADAM_B2 * v_ref[...] + (1.0 - ADAM_B2) * jnp.square(g)
        m_hat = m_new / (1.0 - ADAM_B1 ** ADAM_STEP)
        v_hat = v_new / (1.0 - ADAM_B2 ** ADAM_STEP)
        g_ref[...] = g
        d_ref[...] = -ADAM_LR * (m_hat / (jnp.sqrt(v_hat) + ADAM_EPS) + ADAM_WD * w_ref[...])
        nm_ref[...] = m_new
        nv_ref[...] = v_new

    blk = pl.BlockSpec((tr, c), lambda i: (i, 0))
    return pl.pallas_call(
        body, name=name, grid=(r // tr,), in_specs=[pl.BlockSpec((NDEV, tr, c), lambda i: (0, i, 0)), blk, blk, blk],
        out_specs=[blk] * 4, out_shape=[jax.ShapeDtypeStruct((r, c), F32)] * 4,
        compiler_params=_cparams(("parallel",), 48),
    )(staged, w, m, v)


def _pack(parts):
    flat = jnp.concatenate([p.reshape(-1).astype(F32) for p in parts])
    pad = (-flat.shape[0]) % (8 * LANE)
    return jnp.pad(flat, (0, pad)).reshape(-1, LANE)


def _unpack(slab, shapes):
    flat, out, off = slab.reshape(-1), [], 0
    for s in shapes:
        n = 1
        for dim in s:
            n *= dim
        out.append(flat[off:off + n].reshape(s))
        off += n
    return out


def _to_shards(full, axis):
    shp = full.shape
    t = full.reshape(shp[:axis] + (NDEV, shp[axis] // NDEV) + shp[axis + 1:])
    return jnp.moveaxis(t, axis, 0)


def _from_shards(g, axis):
    t = jnp.moveaxis(g, 0, axis)
    shp = t.shape
    return t.reshape(shp[:axis] + (shp[axis] * shp[axis + 1],) + shp[axis + 2:])


def kernel(x, meta_tokens, norm_mix_w, w_in, dn_conv_w, dn_a_log, dn_dt_bias, dn_norm_w, m2_conv_w, m2_conv_b, m2_a_log, m2_dt_bias, m2_d, m2_norm_w, w_out, norm_ffn_w, ffn_up, ffn_conv_w, ffn_down, norm_final_w, loss_target, m_meta_tokens, m_norm_mix_w, m_w_in, m_dn_conv_w, m_dn_a_log, m_dn_dt_bias, m_dn_norm_w, m_m2_conv_w, m_m2_conv_b, m_m2_a_log, m_m2_dt_bias, m_m2_d, m_m2_norm_w, m_w_out, m_norm_ffn_w, m_ffn_up, m_ffn_conv_w, m_ffn_down, m_norm_final_w, v_meta_tokens, v_norm_mix_w, v_w_in, v_dn_conv_w, v_dn_a_log, v_dn_dt_bias, v_dn_norm_w, v_m2_conv_w, v_m2_conv_b, v_m2_a_log, v_m2_dt_bias, v_m2_d, v_m2_norm_w, v_w_out, v_norm_ffn_w, v_ffn_up, v_ffn_conv_w, v_ffn_down, v_norm_final_w):
    seq, d = x.shape[1], x.shape[2]
    t_rows = seq + CH
    nc = t_rows // CH
    dnh, m2h = d // HD, d // M2P
    dff = ffn_down.shape[1] * NDEV
    xbc_w = d + 2 * M2G * NST
    assert seq % CH == 0 and d % (2 * M2P * M2G) == 0 and 2 * dnh + m2h <= LANE
    hb = max(h for h in (8, 4, 2, 1) if dnh % h == 0)
    tm_rw = _pick(t_rows, 208, 16)
    conv_chunk = _pick(t_rows, min(CONV_CHUNK, t_rows // 3), 16)

    small_sharded = [meta_tokens, dn_conv_w[0], m2_conv_w[0], ffn_conv_w[0]]
    small_shapes = [p.shape for p in small_sharded]
    g_win, g_small = _gather_two_level("gather_w_in", [w_in[0].astype(WIRE), _pack(small_sharded)])
    rest = _exchange_start("gather_rest_start", [w_out[0].astype(WIRE), ffn_up[0].astype(WIRE), ffn_down[0].astype(WIRE)], [],
                           g_small)
    win = _from_shards(g_win, 1)
    small_full = [_unpack(g_small[k], small_shapes) for k in range(NDEV)]
    meta_f, dnconv_f, m2conv_f, ffnconv_f = [jnp.concatenate([small_full[k][i] for k in range(NDEV)], axis=-1) for i in range(4)]

    o_z, o_b, o_a = 3 * d, 4 * d, 4 * d + dnh
    o_m2z = 4 * d + 2 * dnh
    o_xbc, o_dt = o_m2z + d, o_m2z + d + xbc_w
    w_all = jnp.concatenate([win[:, :o_b], win[:, o_m2z:o_dt], win[:, o_b:o_m2z], win[:, o_dt:],
                             jnp.zeros((d, LANE - 2 * dnh - m2h), WIRE)], axis=1)
    seg_cols = {"q": (0, d), "k": (d, d), "v": (2 * d, d), "z": (3 * d, d), "m2z": (4 * d, d), "xbc": (5 * d, xbc_w),
                "sm": (5 * d + xbc_w, LANE)}

    h0 = jnp.concatenate([jnp.zeros((PADR, d), F32), meta_f, x[0]], axis=0)
    valid = lambda rows: rows >= PADR

    def norm_fwd(name, h, w):
        return _rw(name, lambda rows, j, hv, wv: (_rms(hv, wv),), [("r", h, d, _c0), ("p", w, d, _c0)],
                   [("r", d, d, _c0, MXU)], t_rows, tm_rw)[0]

    hn1 = norm_fwd("norm_mix", h0, norm_mix_w)
    proj = {s: _mm("proj_" + s, hn1, w_all, b_cols=seg_cols[s], dep=rest[3]) for s in seg_cols}

    def dn_post(sec, cv):
        s = _silu(cv)
        if sec < 2:
            s = s * lax.rsqrt(jnp.sum(s * s, axis=-1, keepdims=True) + EPS)
        if sec == 0:
            s = s * (HD ** -0.5)
        return s

    def dn_prep(sec, name):
        def fn(rows, own, last, wins, pars):
            return [jnp.where(valid(rows), dn_post(sec, _conv(wins[0], pars[0])), 0.0)], []
        wc = dnconv_f[:, sec * d:(sec + 1) * d]
        return _cv("dn_prep_" + name, fn, [(proj[name], _cj)], [(wc, _cj)], [(d, _cj, F32)], [], t_rows, dnh)[0]

    q_act, k_act, v_act = dn_prep(0, "q"), dn_prep(1, "k"), dn_prep(2, "v")

    lane = lambda: lax.broadcasted_iota(jnp.int32, (1, LANE), 1)

    def lanes_of(vec, off):
        return jnp.pad(vec.astype(F32), ((0, 0), (off, LANE - off - vec.shape[1])))

    gate_params = [lanes_of(dn_a_log, dnh), lanes_of(dn_dt_bias, dnh), lanes_of(m2_a_log, 2 * dnh), lanes_of(m2_dt_bias, 2 * dnh)]

    def gates(rows, sm, p_alog, p_dtb, p_malog, p_mdtb):
        ln = lane()
        is_b, is_g = ln < dnh, jnp.logical_and(ln >= dnh, ln < 2 * dnh)
        is_d = jnp.logical_and(ln >= 2 * dnh, ln < 2 * dnh + m2h)
        beta = jax.nn.sigmoid(sm)
        gdec = -jnp.exp(p_alog) * _softplus(sm + p_dtb)
        dt = _softplus(sm + p_mdtb)
        am = dt * (-jnp.exp(p_malog))
        ok = valid(rows)
        g1 = jnp.where(ok, jnp.where(is_b, beta, jnp.where(is_g, gdec, jnp.where(is_d, dt, 0.0))), 0.0)
        g2 = jnp.where(jnp.logical_and(ok, is_d), am, 0.0)
        return g1, g2

    gate_ins = [("r", proj["sm"], LANE, _c0)] + [("p", p, LANE, _c0) for p in gate_params]
    g1, g2 = _rw("gates", lambda rows, j, *a: gates(rows, *a), gate_ins,
                 [("r", LANE, LANE, _c0, F32), ("r", LANE, LANE, _c0, F32)], t_rows, tm_rw)

    def head_rows(cols, per):
        n = cols.shape[1]
        return cols.reshape(nc, CH, n // per, per).transpose(2, 0, 3, 1)

    def head_cols(rows_):
        ngrp, _, per, _ = rows_.shape
        return rows_.transpose(1, 3, 0, 2).reshape(t_rows, ngrp * per)

    beta_r, gdec_r = head_rows(g1[:, :dnh], hb), head_rows(g1[:, dnh:2 * dnh], hb)
    hpg = m2h // M2G
    dt_r, am_r = head_rows(g1[:, 2 * dnh:2 * dnh + m2h], hpg), head_rows(g2[:, 2 * dnh:2 * dnh + m2h], hpg)

    o_dn, dn_states, dn_tinv = _gdn_fwd(q_act, k_act, v_act, beta_r, gdec_r, hb)

    def dn_out(o, z, w):
        outs = []
        for h in range(dnh):
            sl = slice(h * HD, (h + 1) * HD)
            outs.append(_rms(o[:, sl], w) * _silu(z[:, sl]))
        return jnp.concatenate(outs, axis=1)

    mixed_dn = _rw("dn_out", lambda rows, j, o, z, w: (dn_out(o, z, w),),
                   [("r", o_dn, d, _c0), ("r", proj["z"], d, _c0), ("p", dn_norm_w, HD, _c0)], [("r", d, d, _c0, MXU)],
                   t_rows, tm_rw)[0]

    def m2_prep(rows, own, last, wins, pars):
        return [jnp.where(valid(rows), _silu(_conv(wins[0], pars[0]) + pars[1][0]), 0.0)], []

    xbc_act = _cv("m2_prep", m2_prep, [(proj["xbc"], _cj)], [(m2conv_f, _cj), (m2_conv_b, _cj)], [(xbc_w, _cj, F32)], [],
                  t_rows, xbc_w // LANE)[0]
    y_ssd, m2_states = _ssd_fwd(xbc_act, dt_r, am_r, d)

    d_lanes = jnp.repeat(m2_d.astype(F32), M2P, axis=1)
    gw = d // M2G

    def m2_out(ys, xs, z, dl, nw):
        yv = (ys + dl * xs) * _silu(z)
        outs = []
        for gi in range(M2G):
            sl = slice(gi * gw, (gi + 1) * gw)
            outs.append(_rms(yv[:, sl], nw[:, sl]))
        return jnp.concatenate(outs, axis=1)

    m2_out_ins = [("r", y_ssd, d, _c0), ("r", xbc_act, d, _c0), ("r", proj["m2z"], d, _c0), ("p", d_lanes, d, _c0),
                  ("p", m2_norm_w, d, _c0)]
    mixed_m2 = _rw("m2_out", lambda rows, j, *a: (m2_out(*a),), m2_out_ins, [("r", d, d, _c0, MXU)], t_rows, tm_rw)[0]

    mixed = jnp.concatenate([mixed_dn, mixed_m2], axis=1)
    g_wout, g_wup, g_wdown = _exchange_wait("gather_rest_wait", rest, mixed)
    wout = _from_shards(g_wout, 0)
    wup = _from_shards(g_wup, 1)
    wdown = _from_shards(g_wdown, 0)
    up_g, up_v = (0, dff), (dff, dff)
    h1 = _mm("out_proj", mixed, wout, add=h0)
    hn2 = norm_fwd("norm_ffn", h1, norm_ffn_w)
    u_g, u_v = _mm("ffn_up_g", hn2, wup, b_cols=up_g), _mm("ffn_up_v", hn2, wup, b_cols=up_v)
    fc_g, fc_v = ffnconv_f[:, :dff], ffnconv_f[:, dff:]

    def ffn_act(rows, own, last, wins, pars):
        return [jnp.where(valid(rows), _silu(_conv(wins[0], pars[0])) * _conv(wins[1], pars[1]), 0.0)], []

    act = _cv("ffn_act", ffn_act, [(u_g, _cj), (u_v, _cj)], [(fc_g, _cj), (fc_v, _cj)], [(dff, _cj, MXU)], [],
              t_rows, dff // LANE)[0]
    h2 = _mm("ffn_down", act, wdown, add=h1, tk=1408)

    def loss_fn(hv, wf, tgt, rows):
        err = jnp.where(rows >= CH, _rms(hv, wf) - tgt, 0.0)
        return 0.5 * jnp.sum(jnp.mean(err * err, axis=-1, keepdims=True), axis=0, keepdims=True)

    def final(rows, j, hv, wf, tgt):
        loss, vjp = jax.vjp(lambda a, b: loss_fn(a, b, tgt, rows), hv, wf)
        dh, dw = vjp(jnp.ones((1, 1), F32))
        return dh, dh, dw, jnp.broadcast_to(loss, (1, LANE))

    wf2 = norm_final_w.reshape(1, d)
    dh2, dh2_m, d_wf, loss_part = _rw(
        "loss_head", final, [("r", h2, d, _c0), ("p", wf2, d, _c0), ("r", loss_target[0], d, _c0, lambda i: jnp.maximum(i - 1, 0))],
        [("r", d, d, _c0, F32), ("r", d, d, _c0, MXU), ("p", 1, d, d, _c0), ("p", 1, LANE, LANE, _c0)], t_rows, CH)
    loss = lax.psum(loss_part[0, 0], MESH_AXES)

    d_act = _mm("d_act", dh2_m, wdown, tb=True)
    gw_down = _mm("gw_down", act, dh2_m, ta=True, tm=1408, tn=1024, tk=2080, out_dtype=WIRE)
    x_down = _exchange_start("grad_down_start", [], [_to_shards(gw_down, 0).astype(WIRE)], gw_down)

    def t_end(last):
        return t_rows if last else None

    def ffn_act_bwd(rows, own, last, wins, pars):
        (ug, uv, da), (wg, wv) = wins, pars
        cg, cv = _conv(ug, wg), _conv(uv, wv)
        _, vjp = jax.vjp(lambda a, b: _silu(a) * b, cg, cv)
        dcg, dcv = vjp(jnp.where(valid(rows), da, 0.0))
        return ([_conv_t(dcg, wg, rows, t_end(last)), _conv_t(dcv, wv, rows, t_end(last))],
                [_conv_w(jnp.where(own, dcg, 0.0), ug, len(wg)), _conv_w(jnp.where(own, dcv, 0.0), uv, len(wv))])

    kf = fc_g.shape[0]
    du_g, du_v, g_fc_g, g_fc_v = _cv(
        "ffn_act_bwd", ffn_act_bwd, [(u_g, _cj), (u_v, _cj), (d_act, _cj)], [(fc_g, _cj), (fc_v, _cj)],
        [(dff, _cj, MXU), (dff, _cj, MXU)], [(kf, dff, _cj), (kf, dff, _cj)], t_rows, dff // LANE, chunk=conv_chunk)
    gw_up_g = _mm("gw_up_g", hn2, du_g, ta=True, tm=1024, tn=1408, tk=2080, out_dtype=WIRE, dep=x_down[3])
    gw_up_v = _mm("gw_up_v", hn2, du_v, ta=True, tm=1024, tn=1408, tk=2080, out_dtype=WIRE)
    gw_up_full = jnp.concatenate([gw_up_g, gw_up_v], axis=1)
    x_up = _exchange_start("grad_up_start", [], [_to_shards(gw_up_full, 1).astype(WIRE)], gw_up_full)
    d_hn2 = _mm("d_hn2_v", du_v, wup, b_cols=up_v, tb=True, tk=1408, dep=x_up[3],
                add=_mm("d_hn2_g", du_g, wup, b_cols=up_g, tb=True, tk=1408, tn=2048, dep=x_up[3]))

    def norm_bwd(name, h, w, dy, dres):
        def fn(rows, j, hv, wv, dyv, dr):
            _, vjp = jax.vjp(_rms, hv, wv)
            dh, dw = vjp(dyv)
            dh = dh + dr
            return dh, dh, dw
        return _rw(name, fn, [("r", h, d, _c0), ("p", w, d, _c0), ("r", dy, d, _c0), ("r", dres, d, _c0)],
                   [("r", d, d, _c0, F32), ("r", d, d, _c0, MXU), ("p", 1, d, d, _c0)], t_rows, tm_rw)

    dh1, dh1_m, g_norm_ffn = norm_bwd("norm_ffn_bwd", h1, norm_ffn_w, d_hn2, dh2)

    gw_out = _mm("gw_out", mixed, dh1_m, ta=True, tm=1024, tn=1024, tk=2080, out_dtype=WIRE)
    x_out = _exchange_start("grad_out_start", [], [_to_shards(gw_out, 0).astype(WIRE)], gw_out)
    d_mixed = _mm("d_mixed", dh1_m, wout, tb=True, dep=x_out[3])

    gw_seg = {}

    def gw_in(seg, dseg_arr):
        gw_seg[seg] = _mm("gw_in_" + seg, hn1, dseg_arr, ta=True, tm=1024, tn=1024, tk=2080, out_dtype=WIRE)

    def m2_out_bwd(rows, j, ys, xs, z, dl, nw, dy):
        _, vjp = jax.vjp(m2_out, ys, xs, z, dl, nw)
        return vjp(dy)

    dy_ssd, dxs_skip, d_m2z, g_d_lanes, g_m2_norm = _rw(
        "m2_out_bwd", m2_out_bwd, m2_out_ins + [("r", d_mixed, d, lambda j: 1)],
        [("r", d, d, _c0, F32), ("r", d, d, _c0, F32), ("r", d, d, _c0, MXU), ("p", 1, d, d, _c0), ("p", 1, d, d, _c0)],
        t_rows, _pick(t_rows, 208, 16))
    gw_in("m2z", d_m2z)

    def fold_heads(vec_ref, out_ref):
        r = lax.broadcasted_iota(jnp.int32, (d, LANE), 0)
        c = lax.broadcasted_iota(jnp.int32, (d, LANE), 1)
        out_ref[...] = _dgh(vec_ref[...], jnp.where(jnp.logical_and(r >= c * M2P, r < (c + 1) * M2P), 1.0, 0.0), 1, 0)

    g_m2_d = pl.pallas_call(fold_heads, name="fold_m2_d", out_shape=jax.ShapeDtypeStruct((1, LANE), F32))(g_d_lanes)

    dxs, db_ssd, dc_ssd, ddt_r, dam_r = _ssd_bwd(xbc_act, dt_r, am_r, m2_states, dy_ssd, d)


    def m2_prep_bwd(rows, own, last, wins, pars):
        (p, *ds), (w, b) = wins, pars
        _, vjp = jax.vjp(_silu, _conv(p, w) + b[0])
        dpre, = vjp(jnp.where(valid(rows), functools.reduce(lambda a_, b_: a_ + b_, ds), 0.0))
        dpre_own = jnp.where(own, dpre, 0.0)
        return [_conv_t(dpre, w, rows, t_end(last))], [_conv_w(dpre_own, p, len(w)), [jnp.sum(dpre_own, axis=0, keepdims=True)]]

    def m2_prep_bwd_call(name, off, width, d_arrs):
        at = lambda j, blk0=off // LANE: blk0 + j
        return _cv(name, m2_prep_bwd, [(proj["xbc"], at)] + [(a, _cj) for a in d_arrs], [(m2conv_f, at), (m2_conv_b, at)],
                   [(width, _cj, MXU)], [(m2conv_f.shape[0], width, _cj), (1, width, _cj)], t_rows, width // LANE, chunk=conv_chunk)

    dp_xs, gcw_xs, gcb_xs = m2_prep_bwd_call("m2_prep_bwd_x", 0, d, [dxs, dxs_skip])
    dp_b, gcw_b, gcb_b = m2_prep_bwd_call("m2_prep_bwd_b", d, M2G * NST, [db_ssd])
    dp_c, gcw_c, gcb_c = m2_prep_bwd_call("m2_prep_bwd_c", d + M2G * NST, M2G * NST, [dc_ssd])
    d_pxbc = jnp.concatenate([dp_xs, dp_b, dp_c], axis=1)
    gw_in("xbc", d_pxbc)
    g_m2_conv = jnp.concatenate([gcw_xs, gcw_b, gcw_c], axis=1)
    g_m2_conv_b = jnp.concatenate([gcb_xs, gcb_b, gcb_c], axis=1)

    def dn_out_bwd(rows, j, o, z, w, dy):
        _, vjp = jax.vjp(dn_out, o, z, w)
        return vjp(dy)

    d_o, d_z, g_dn_norm = _rw(
        "dn_out_bwd", dn_out_bwd,
        [("r", o_dn, d, _c0), ("r", proj["z"], d, _c0), ("p", dn_norm_w, HD, _c0), ("r", d_mixed, d, _c0)],
        [("r", d, d, _c0, F32), ("r", d, d, _c0, MXU), ("p", 1, HD, HD, _c0)], t_rows, _pick(t_rows, 208, 16))
    gw_in("z", d_z)

    dq, dk, dv, dbeta_r, dgdec_r = _gdn_bwd(q_act, k_act, v_act, beta_r, gdec_r, dn_states, dn_tinv, d_o, hb)

    def dn_prep_bwd(sec, name, dact):
        def fn(rows, own, last, wins, pars):
            (p, da), (w,) = wins, pars
            _, vjp = jax.vjp(functools.partial(dn_post, sec), _conv(p, w))
            dcv, = vjp(jnp.where(valid(rows), da, 0.0))
            return [_conv_t(dcv, w, rows, t_end(last))], [_conv_w(jnp.where(own, dcv, 0.0), p, len(w))]
        wc = dnconv_f[:, sec * d:(sec + 1) * d]
        return _cv("dn_prep_bwd_" + name, fn, [(proj[name], _cj), (dact, _cj)], [(wc, _cj)], [(d, _cj, MXU)],
                   [(wc.shape[0], d, _cj)], t_rows, dnh, chunk=conv_chunk if sec == 2 else None)

    (dp_q, gcw_q), (dp_k, gcw_k), (dp_v, gcw_v) = dn_prep_bwd(0, "q", dq), dn_prep_bwd(1, "k", dk), dn_prep_bwd(2, "v", dv)
    gw_in("q", dp_q), gw_in("k", dp_k), gw_in("v", dp_v)
    g_dn_conv = jnp.concatenate([gcw_q, gcw_k, gcw_v], axis=1)

    zpad = jnp.zeros((t_rows, LANE - 2 * dnh - m2h), F32)
    dg1 = jnp.concatenate([head_cols(dbeta_r), head_cols(dgdec_r), head_cols(ddt_r), zpad], axis=1)
    dg2 = jnp.concatenate([jnp.zeros((t_rows, 2 * dnh), F32), head_cols(dam_r), zpad], axis=1)

    def gates_bwd(rows, j, sm, pa, pb, pc, pd, d1, d2):
        _, vjp = jax.vjp(lambda *a: gates(rows, *a), sm, pa, pb, pc, pd)
        return vjp((d1, d2))

    dp_sm, g_pa, g_pb, g_pc, g_pd = _rw(
        "gates_bwd", gates_bwd, gate_ins + [("r", dg1, LANE, _c0), ("r", dg2, LANE, _c0)],
        [("r", LANE, LANE, _c0, MXU)] + [("p", 1, LANE, LANE, _c0)] * 4, t_rows, tm_rw)

    dseg = {"q": dp_q, "k": dp_k, "v": dp_v, "z": d_z, "m2z": d_m2z, "xbc": d_pxbc, "sm": dp_sm}
    gw_in("sm", dp_sm)
    gsm = gw_seg["sm"]
    gw_in_full = jnp.concatenate([gw_seg["q"], gw_seg["k"], gw_seg["v"], gw_seg["z"], gsm[:, :2 * dnh], gw_seg["m2z"],
                                  gw_seg["xbc"], gsm[:, 2 * dnh:2 * dnh + m2h]], axis=1)
    x_in = _exchange_start("grad_in_start", [], [_to_shards(gw_in_full, 1).astype(WIRE)], gw_in_full)
    d_hn1 = None
    for s in dseg:
        d_hn1 = _mm("d_hn1_" + s, dseg[s], w_all, b_cols=seg_cols[s], tb=True, tk=2048, add=d_hn1, dep=x_in[3])
    dh0, _, g_norm_mix = norm_bwd("norm_mix_bwd", h0, norm_mix_w, d_hn1, dh1)

    g_ffn_conv = jnp.concatenate([g_fc_g, g_fc_v], axis=1)
    small_parts = [_to_shards(dh0[PADR:CH], 1), _to_shards(g_dn_conv, 1), _to_shards(g_m2_conv, 1), _to_shards(g_ffn_conv, 1)]
    small_scatter = jnp.stack([_pack([p[k] for p in small_parts]) for k in range(NDEV)])

    rep_names = ["norm_mix_w", "dn_a_log", "dn_dt_bias", "dn_norm_w", "m2_conv_b", "m2_a_log", "m2_dt_bias", "m2_d",
                 "m2_norm_w", "norm_ffn_w", "norm_final_w"]
    rep_grads = [g_norm_mix, g_pa[:, dnh:2 * dnh], g_pb[:, dnh:2 * dnh], g_dn_norm, g_m2_conv_b, g_pc[:, 2 * dnh:2 * dnh + m2h],
                 g_pd[:, 2 * dnh:2 * dnh + m2h], g_m2_d[:, :m2h], g_m2_norm, g_norm_ffn, d_wf.reshape(d)]

    weights = dict(meta_tokens=meta_tokens, norm_mix_w=norm_mix_w, w_in=w_in, dn_conv_w=dn_conv_w, dn_a_log=dn_a_log,
                   dn_dt_bias=dn_dt_bias, dn_norm_w=dn_norm_w, m2_conv_w=m2_conv_w, m2_conv_b=m2_conv_b, m2_a_log=m2_a_log,
                   m2_dt_bias=m2_dt_bias, m2_d=m2_d, m2_norm_w=m2_norm_w, w_out=w_out, norm_ffn_w=norm_ffn_w, ffn_up=ffn_up,
                   ffn_conv_w=ffn_conv_w, ffn_down=ffn_down, norm_final_w=norm_final_w)
    mom1 = dict(meta_tokens=m_meta_tokens, norm_mix_w=m_norm_mix_w, w_in=m_w_in, dn_conv_w=m_dn_conv_w, dn_a_log=m_dn_a_log,
                dn_dt_bias=m_dn_dt_bias, dn_norm_w=m_dn_norm_w, m2_conv_w=m_m2_conv_w, m2_conv_b=m_m2_conv_b,
                m2_a_log=m_m2_a_log, m2_dt_bias=m_m2_dt_bias, m2_d=m_m2_d, m2_norm_w=m_m2_norm_w, w_out=m_w_out,
                norm_ffn_w=m_norm_ffn_w, ffn_up=m_ffn_up, ffn_conv_w=m_ffn_conv_w, ffn_down=m_ffn_down,
                norm_final_w=m_norm_final_w)
    mom2 = dict(meta_tokens=v_meta_tokens, norm_mix_w=v_norm_mix_w, w_in=v_w_in, dn_conv_w=v_dn_conv_w, dn_a_log=v_dn_a_log,
                dn_dt_bias=v_dn_dt_bias, dn_norm_w=v_dn_norm_w, m2_conv_w=v_m2_conv_w, m2_conv_b=v_m2_conv_b,
                m2_a_log=v_m2_a_log, m2_dt_bias=v_m2_dt_bias, m2_d=v_m2_d, m2_norm_w=v_m2_norm_w, w_out=v_w_out,
                norm_ffn_w=v_norm_ffn_w, ffn_up=v_ffn_up, ffn_conv_w=v_ffn_conv_w, ffn_down=v_ffn_down,
                norm_final_w=v_norm_final_w)
    res = {}

    def adam_big(name, started, after):
        staged, = _exchange_wait("grad_" + name + "_wait", started, after)
        outs = _adamw("adamw_" + name, staged, weights[name][0], mom1[name][0], mom2[name][0])
        res[name] = tuple(o[None] for o in outs)
        return outs[1]

    done = adam_big("ffn_down", x_down, dh0)
    done = adam_big("ffn_up", x_up, done)
    done = adam_big("w_out", x_out, done)
    st_rep, st_small = _exchange("exchange_small_grads", [_pack(rep_grads)], [small_scatter], done)
    adam_big("w_in", x_in, st_small)

    def adam_packed(label, staged, names):
        shapes = [weights[nm].shape for nm in names]
        outs = _adamw(label, staged, *[_pack([src[nm] for nm in names]) for src in (weights, mom1, mom2)])
        unpacked = [_unpack(o, shapes) for o in outs]
        for i, nm in enumerate(names):
            res[nm] = tuple(u[i] for u in unpacked)

    adam_packed("adamw_small_sharded", st_small, ["meta_tokens", "dn_conv_w", "m2_conv_w", "ffn_conv_w"])
    adam_packed("adamw_replicated", st_rep, rep_names)

    order = list(weights)
    grad_x = dh0[CH:][None]
    return (loss, grad_x, *[res[nm][0] for nm in order], *[res[nm][1] for nm in order], *[res[nm][2] for nm in order],
            *[res[nm][3] for nm in order])
```

```python
import functools
import math

import jax
import jax.numpy as jnp
from jax import lax
from jax.experimental import pallas as pl
from jax.experimental.pallas import tpu as pltpu

F32 = jnp.float32
MXU = jnp.bfloat16
WIRE = jnp.bfloat16
HI = lax.Precision.HIGH

NDEV = 8
CH = 64
NMETA = 16
PADR = CH - NMETA
EPS = 1e-6
HD = 128
M2P = 64
M2G = 4
NST = 128
LANE = 128

ADAM_LR, ADAM_B1, ADAM_B2, ADAM_EPS, ADAM_WD, ADAM_STEP = 0.001, 0.9, 0.999, 1e-08, 0.01, 10

MESH_AXES = ("x", "y", "c")


def _pick(n, target, mult=16):
    best = None
    for t in range(mult, min(n, target) + 1, mult):
        if n % t == 0:
            best = t
    return best if best is not None else n


def _dg(a, b, ca, cb):
    return lax.dot_general(a.astype(MXU), b.astype(MXU), (((ca,), (cb,)), ((), ())), preferred_element_type=F32)


def _dgh(a, b, ca, cb):
    return lax.dot_general(a, b, (((ca,), (cb,)), ((), ())), precision=HI, preferred_element_type=F32)


def _silu(x):
    return x * jax.nn.sigmoid(x)


def _softplus(x):
    return jnp.maximum(x, 0.0) + jnp.log1p(jnp.exp(-jnp.abs(x)))


def _rms(x, w):
    return x * lax.rsqrt(jnp.mean(x * x, axis=-1, keepdims=True) + EPS) * w


def _cparams(sem, vmem_mb):
    return pltpu.CompilerParams(dimension_semantics=sem, vmem_limit_bytes=vmem_mb << 20)


def _mm(name, a, b, *, ta=False, tb=False, add=None, out_dtype=F32, tm=1040, tn=1024, tk=2048, dep=None,
        b_cols=None):
    m, kdim = (a.shape[1], a.shape[0]) if ta else a.shape
    b_start, b_width = b_cols if b_cols is not None else (0, b.shape[1])
    n = b.shape[0] if tb else b_width
    if tb:
        kdim = b_width
    tm = _pick(m, tm, 128 if ta else 16)
    tn = _pick(n if tb else math.gcd(b_width, b_start), tn, 128)
    tk = _pick(math.gcd(b_width, b_start) if tb else kdim, tk, 16 if (ta and not tb) else 128)
    nk = kdim // tk
    bj0, bk0 = (0, b_start // tk) if tb else (b_start // tn, 0)
    ca, cb = (0 if ta else 1), (1 if tb else 0)

    def body(*refs):
        a_ref, b_ref = refs[0], refs[1]
        add_ref = refs[2] if add is not None else None
        if nk == 1:
            r = _dg(a_ref[...], b_ref[...], ca, cb)
            if add_ref is not None:
                r = r + add_ref[...].astype(F32)
            refs[-1][...] = r.astype(refs[-1].dtype)
            return
        o_ref, acc = refs[-2], refs[-1]
        k = pl.program_id(2)

        @pl.when(k == 0)
        def _():
            acc[...] = jnp.zeros_like(acc)

        acc[...] += _dg(a_ref[...], b_ref[...], ca, cb)

        @pl.when(k == nk - 1)
        def _():
            r = acc[...]
            if add_ref is not None:
                r = r + add_ref[...].astype(F32)
            o_ref[...] = r.astype(o_ref.dtype)

    a_spec = pl.BlockSpec((tk, tm), lambda i, j, k: (k, i)) if ta else pl.BlockSpec((tm, tk), lambda i, j, k: (i, k))
    b_spec = (pl.BlockSpec((tn, tk), lambda i, j, k: (j, k + bk0)) if tb
              else pl.BlockSpec((tk, tn), lambda i, j, k: (k, j + bj0)))
    in_specs, ops = [a_spec, b_spec], [a, b]
    if add is not None:
        in_specs.append(pl.BlockSpec((tm, tn), lambda i, j, k: (i, j)))
        ops.append(add)
    if dep is not None:
        in_specs.append(pl.BlockSpec((8, LANE), lambda i, j, k: (0, 0)))
        ops.append(dep)
    return pl.pallas_call(
        body, name=name, grid=(m // tm, n // tn, nk), in_specs=in_specs,
        out_specs=pl.BlockSpec((tm, tn), lambda i, j, k: (i, j)),
        out_shape=jax.ShapeDtypeStruct((m, n), out_dtype),
        scratch_shapes=[pltpu.VMEM((tm, tn), F32)] if nk > 1 else [],
        compiler_params=_cparams(("parallel", "parallel", "arbitrary"), 48),
    )(*ops)


def _rw(name, fn, ins, outs, nrows, tm, ncol=1, vmem_mb=48):
    nrow = nrows // tm
    sub = tm
    in_specs, ops = [], []
    for spec in ins:
        kind, arr, bw, cj = spec[:4]
        ops.append(arr)
        if kind == "r":
            ri = spec[4] if len(spec) > 4 else (lambda i: i)
            in_specs.append(pl.BlockSpec((tm, bw), lambda j, i, cj=cj, ri=ri: (ri(i), cj(j))))
        else:
            in_specs.append(pl.BlockSpec((arr.shape[0], bw), lambda j, i, cj=cj: (0, cj(j))))
    out_shape, out_specs = [], []
    for o in outs:
        if o[0] == "r":
            _, width, bw, cj, dt = o
            out_shape.append(jax.ShapeDtypeStruct((nrows, width), dt))
            out_specs.append(pl.BlockSpec((tm, bw), lambda j, i, cj=cj: (i, cj(j))))
        else:
            _, rows, width, bw, cj = o
            out_shape.append(jax.ShapeDtypeStruct((rows, width), F32))
            out_specs.append(pl.BlockSpec((rows, bw), lambda j, i, cj=cj: (0, cj(j))))
    n_in = len(ins)

    def body(*refs):
        j, i = pl.program_id(0), pl.program_id(1)
        in_refs, out_refs = refs[:n_in], refs[n_in:]
        pars = [ref[...] if spec[0] == "p" else None for spec, ref in zip(ins, in_refs)]

        def one(r0, nr):
            rows = i * tm + r0 + lax.broadcasted_iota(jnp.int32, (nr, 1), 0)
            vals = [par if spec[0] == "p" else ref[pl.ds(r0, nr), :] for spec, ref, par in zip(ins, in_refs, pars)]
            parts = []
            for o, val, ref in zip(outs, fn(rows, j, *vals), out_refs):
                if o[0] == "r":
                    ref[pl.ds(r0, nr), :] = val.astype(ref.dtype)
                else:
                    parts.append(val)
            return parts

        if sub >= tm:
            parts = one(0, tm)
        else:
            zero = [jnp.zeros((1, o[3]), F32) for o in outs if o[0] == "p"]
            parts = lax.fori_loop(
                0, tm // sub, lambda s, acc: [a + b for a, b in zip(acc, one(pl.multiple_of(s * sub, sub), sub))], zero)
        for ref, val in zip([r for o, r in zip(outs, out_refs) if o[0] == "p"], parts):
            @pl.when(i == 0)
            def _(ref=ref):
                ref[...] = jnp.zeros_like(ref)

            ref[...] += val

    return pl.pallas_call(
        body, name=name, grid=(ncol, nrow), in_specs=in_specs, out_specs=out_specs, out_shape=out_shape,
        compiler_params=_cparams(("parallel", "arbitrary"), vmem_mb),
    )(*ops)


def _c0(j):
    return 0


def _cj(j):
    return j


def _shift(x, s):
    if s == 0:
        return x
    return pltpu.roll(x, s % x.shape[0], 0)


def _conv(x, w):
    k = len(w)
    return functools.reduce(lambda a, b: a + b, [w[j] * _shift(x, k - 1 - j) for j in range(k)])


def _conv_t(dy, w, rows, t_end):
    k = len(w)
    terms = []
    for j in range(k):
        s = k - 1 - j
        v = _shift(dy, -s)
        if t_end is not None and s > 0:
            v = jnp.where(rows + s < t_end, v, 0.0)
        terms.append(w[j] * v)
    return functools.reduce(lambda a, b: a + b, terms)


def _conv_w(dy, x, k):
    return [jnp.sum(dy * _shift(x, k - 1 - j), axis=0, keepdims=True) for j in range(k)]


CONV_CHUNK = 320
HALO = 8


def _cv(name, fn, row_ins, par_ins, row_outs, par_outs, nrows, ncol, chunk=None):
    whole = chunk is None
    chunk = nrows if whole else chunk
    n_chunks = nrows // chunk
    assert nrows % chunk == 0 and (whole or n_chunks >= 3)
    n_ri, n_pi, n_ro = len(row_ins), len(par_ins), len(row_outs)

    def body(*refs):
        rin, pin = refs[:n_ri], refs[n_ri:n_ri + n_pi]
        rout, pout = refs[n_ri + n_pi:n_ri + n_pi + n_ro], refs[n_ri + n_pi + n_ro:]
        pars = [[p[pl.ds(r, 1), :] for r in range(p.shape[0])] for p in pin]

        def run(r0, top, bot, last):
            wlen = top + chunk + bot
            w0 = r0 - top if isinstance(r0, int) else pl.multiple_of(r0 - top, HALO)
            local = lax.broadcasted_iota(jnp.int32, (wlen, 1), 0)
            own = jnp.logical_and(local >= top, local < top + chunk)
            outs, parts = fn(w0 + local, own, last, [ref[pl.ds(w0, wlen), :] for ref in rin], pars)
            for ref, val in zip(rout, outs):
                ref[pl.ds(r0, chunk), :] = val[top:top + chunk].astype(ref.dtype)
            return parts

        def add(acc, parts):
            return [[a + b for a, b in zip(ra, rb)] for ra, rb in zip(acc, parts)]

        if whole:
            acc = run(0, 0, 0, True)
        else:
            acc = run(0, 0, HALO, False)
            acc = lax.fori_loop(1, n_chunks - 1,
                                lambda i, a: add(a, run(pl.multiple_of(i * chunk, chunk), HALO, HALO, False)), acc)
            acc = add(acc, run(nrows - chunk, HALO, 0, True))
        for ref, prow in zip(pout, acc):
            for r, v in enumerate(prow):
                ref[pl.ds(r, 1), :] = v

    in_specs = [pl.BlockSpec((nrows, LANE), lambda j, cj=cj: (0, cj(j))) for _, cj in row_ins]
    in_specs += [pl.BlockSpec((a.shape[0], LANE), lambda j, cj=cj: (0, cj(j))) for a, cj in par_ins]
    out_specs = [pl.BlockSpec((nrows, LANE), lambda j, cj=cj: (0, cj(j))) for _, cj, _ in row_outs]
    out_specs += [pl.BlockSpec((k, LANE), lambda j, cj=cj: (0, cj(j))) for k, _, cj in par_outs]
    out_shape = [jax.ShapeDtypeStruct((nrows, width), dt) for width, _, dt in row_outs]
    out_shape += [jax.ShapeDtypeStruct((k, width), F32) for k, width, _ in par_outs]
    return pl.pallas_call(
        body, name=name, grid=(ncol,), in_specs=in_specs, out_specs=out_specs, out_shape=out_shape,
        compiler_params=_cparams(("parallel",), 48),
    )(*[a for a, _ in row_ins], *[a for a, _ in par_ins])


def _tri():
    r = lax.broadcasted_iota(jnp.int32, (CH, CH), 0)
    c = lax.broadcasted_iota(jnp.int32, (CH, CH), 1)
    return r, c


def _col(row):
    r, c = _tri()
    return jnp.sum(jnp.where(r == c, row, 0.0), axis=1, keepdims=True)


def _cumsum_rc(g_r):
    r, c = _tri()
    g_c = _col(g_r)
    cs_r = jnp.sum(jnp.where(r <= c, g_c, 0.0), axis=0, keepdims=True)
    cs_c = jnp.sum(jnp.where(c <= r, g_r, 0.0), axis=1, keepdims=True)
    return cs_r, cs_c


def _decay(cs_r, cs_c):
    r, c = _tri()
    return jnp.exp(jnp.where(c <= r, cs_c - cs_r, -jnp.inf))


def _gdn_a(ks, betas, gs):
    r, c = _tri()
    cs = [_cumsum_rc(g) for g in gs]
    kk = [_dg(k, k, 1, 1) for k in ks]
    return [jnp.where(c < r, _col(b) * kki * _decay(*csi), 0.0) for b, kki, csi in zip(betas, kk, cs)]


def _neumann(a_list):
    r, c = _tri()
    xs = [jnp.where(r == c, 1.0, 0.0) - a for a in a_list]
    ps = list(a_list)
    n = 2
    while n < CH:
        ps = [_dgh(p, p, 1, 0) for p in ps]
        xs = [x + _dgh(x, p, 1, 0) for x, p in zip(xs, ps)]
        n *= 2
    return xs


def _gdn_rest(ss, qs, ks, vs, betas, gs, ts):
    n = range(len(ss))
    cs = [_cumsum_rc(g) for g in gs]
    dm = [_decay(*csi) for csi in cs]
    ecs = [jnp.exp(csi[1]) for csi in cs]
    bc = [_col(b) for b in betas]
    u = [_dgh(ts[i], vs[i] * bc[i], 1, 0) for i in n]
    w = [_dgh(ts[i], ks[i] * (bc[i] * ecs[i]), 1, 0) for i in n]
    ws = [_dg(w[i], ss[i], 1, 0) for i in n]
    v_new = [u[i] - ws[i] for i in n]
    qk = [_dg(qs[i], ks[i], 1, 1) * dm[i] for i in n]
    o_in = [_dg(qs[i] * ecs[i], ss[i], 1, 0) for i in n]
    o = [o_in[i] + _dg(qk[i], v_new[i], 1, 0) for i in n]
    g_last = [jnp.sum(g, axis=1, keepdims=True) for g in gs]
    s_new = [ss[i] * jnp.exp(g_last[i]) + _dg(ks[i] * jnp.exp(g_last[i] - cs[i][1]), v_new[i], 0, 0) for i in n]
    return s_new, o


def _gdn_fwd(q, k, v, beta, g, hb):
    t_rows, d = q.shape
    nc, ng, w = t_rows // CH, d // (HD * hb), HD * hb
    sls = [slice(h * HD, (h + 1) * HD) for h in range(hb)]

    def body(q_ref, k_ref, v_ref, b_ref, g_ref, o_ref, ss_ref, ts_ref, s_scr):
        c = pl.program_id(1)

        @pl.when(c == 0)
        def _():
            s_scr[...] = jnp.zeros_like(s_scr)

        qs, ks, vs = ([ref[:, sl] for sl in sls] for ref in (q_ref, k_ref, v_ref))
        br = [b_ref[0, 0, pl.ds(h, 1), :] for h in range(hb)]
        gr = [g_ref[0, 0, pl.ds(h, 1), :] for h in range(hb)]
        s0 = [s_scr[h] for h in range(hb)]
        tm = _neumann(_gdn_a(ks, br, gr))
        s1, o = _gdn_rest(s0, qs, ks, vs, br, gr, tm)
        for h in range(hb):
            ss_ref[0, 0, h] = s0[h]
            ts_ref[0, 0, h] = tm[h]
            o_ref[:, sls[h]] = o[h]
            s_scr[h] = s1[h]

    blk = pl.BlockSpec((CH, w), lambda n, c: (c, n))
    row = pl.BlockSpec((1, 1, hb, CH), lambda n, c: (n, c, 0, 0))
    return pl.pallas_call(
        body, name="gdn_fwd", grid=(ng, nc), in_specs=[blk, blk, blk, row, row],
        out_specs=[blk, pl.BlockSpec((1, 1, hb, HD, HD), lambda n, c: (n, c, 0, 0, 0)),
                   pl.BlockSpec((1, 1, hb, CH, CH), lambda n, c: (n, c, 0, 0, 0))],
        out_shape=[jax.ShapeDtypeStruct((t_rows, d), F32), jax.ShapeDtypeStruct((ng, nc, hb, HD, HD), F32),
                   jax.ShapeDtypeStruct((ng, nc, hb, CH, CH), F32)],
        scratch_shapes=[pltpu.VMEM((hb, HD, HD), F32)],
        compiler_params=_cparams(("parallel", "arbitrary"), 32),
    )(q, k, v, beta, g)


def _gdn_bwd(q, k, v, beta, g, ss, ts, do, hb):
    t_rows, d = q.shape
    nc, ng, w = t_rows // CH, d // (HD * hb), HD * hb
    per_f = ss.shape[2] // hb
    sls = [slice(h * HD, (h + 1) * HD) for h in range(hb)]

    def body(q_ref, k_ref, v_ref, b_ref, g_ref, ss_ref, ts_ref, do_ref, dq_ref, dk_ref, dv_ref, db_ref, dg_ref, ds_scr):
        cr = pl.program_id(1)

        @pl.when(cr == 0)
        def _():
            ds_scr[...] = jnp.zeros_like(ds_scr)

        first = cr == nc - 1
        rowi = lax.broadcasted_iota(jnp.int32, (CH, 1), 0)
        lani = lax.broadcasted_iota(jnp.int32, (1, CH), 1)
        keep_c = jnp.logical_or(jnp.logical_not(first), rowi >= PADR)
        keep_r = jnp.logical_or(jnp.logical_not(first), lani >= PADR)
        hs = range(hb)
        qs, ks, vs, dos = ([ref[:, sl] for sl in sls] for ref in (q_ref, k_ref, v_ref, do_ref))
        br = [b_ref[0, 0, pl.ds(h, 1), :] for h in hs]
        gr = [g_ref[0, 0, pl.ds(h, 1), :] for h in hs]
        tm = [ts_ref[0, 0, h] for h in hs]
        _, vjp_rest = jax.vjp(_gdn_rest, [ss_ref[0, 0, h] for h in hs], qs, ks, vs, br, gr, tm)
        ds0, dq, dk, dv, db, dg, dt = vjp_rest(([ds_scr[h] for h in hs], dos))
        dtt = [_dgh(dt[h], tm[h], 1, 1) for h in hs]
        da = [-_dgh(tm[h], dtt[h], 0, 0) for h in hs]
        _, vjp_a = jax.vjp(_gdn_a, ks, br, gr)
        dk2, db2, dg2 = vjp_a(da)
        for h in hs:
            ds_scr[h] = ds0[h]
            dq_ref[:, sls[h]] = jnp.where(keep_c, dq[h], 0.0)
            dk_ref[:, sls[h]] = jnp.where(keep_c, dk[h] + dk2[h], 0.0)
            dv_ref[:, sls[h]] = jnp.where(keep_c, dv[h], 0.0)
            db_ref[0, 0, pl.ds(h, 1), :] = jnp.where(keep_r, db[h] + db2[h], 0.0)
            dg_ref[0, 0, pl.ds(h, 1), :] = jnp.where(keep_r, dg[h] + dg2[h], 0.0)

    blk = pl.BlockSpec((CH, w), lambda n, c: (nc - 1 - c, n))
    row = pl.BlockSpec((1, 1, hb, CH), lambda n, c: (n, nc - 1 - c, 0, 0))
    return pl.pallas_call(
        body, name="gdn_bwd", grid=(ng, nc),
        in_specs=[blk, blk, blk, row, row,
                  pl.BlockSpec((1, 1, hb, HD, HD), lambda n, c: (n // per_f, nc - 1 - c, n % per_f, 0, 0)),
                  pl.BlockSpec((1, 1, hb, CH, CH), lambda n, c: (n // per_f, nc - 1 - c, n % per_f, 0, 0)), blk],
        out_specs=[blk, blk, blk, row, row],
        out_shape=[jax.ShapeDtypeStruct((t_rows, d), F32)] * 3 + [jax.ShapeDtypeStruct((ng, nc, hb, CH), F32)] * 2,
        scratch_shapes=[pltpu.VMEM((hb, HD, HD), F32)],
        compiler_params=_cparams(("parallel", "arbitrary"), 32),
    )(q, k, v, beta, g, ss, ts, do)


def _ssd_group(s, xs, bm, cm, dt_r, a_r):
    prs = range(len(s))
    first = lax.broadcasted_iota(jnp.int32, (1, 2 * M2P), 1) < M2P

    def pick(vals, p):
        return jnp.where(first, vals[2 * p], vals[2 * p + 1])

    cs = [_cumsum_rc(a) for a in a_r]
    lm = [_decay(*csi) for csi in cs]
    ecs = [jnp.exp(csi[1]) for csi in cs]
    alast = [jnp.sum(a, axis=1, keepdims=True) for a in a_r]
    ealast = [jnp.exp(al) for al in alast]
    wt = [jnp.exp(al - csi[1]) for al, csi in zip(alast, cs)]
    dtc = [_col(t) for t in dt_r]
    xdt = [xs[:, p * LANE:(p + 1) * LANE] * pick(dtc, p) for p in prs]
    cb = _dg(cm, bm, 1, 1)
    y0 = [_dg(cb * lm[2 * p], xdt[p], 1, 0) for p in prs]
    y1 = [_dg(cb * lm[2 * p + 1], xdt[p], 1, 0) for p in prs]
    yo = [_dg(cm, s[p], 1, 0) for p in prs]
    y = [jnp.where(first, y0[p], y1[p]) + yo[p] * pick(ecs, p) for p in prs]
    s_new = [s[p] * pick(ealast, p) + _dg(bm, xdt[p] * pick(wt, p), 0, 0) for p in prs]
    return s_new, jnp.concatenate(y, axis=1)


def _ssd_specs(nc, d, rev):
    hpg = (d // M2P) // M2G
    gw = hpg * M2P
    cc = (lambda c: nc - 1 - c) if rev else (lambda c: c)
    xs = pl.BlockSpec((CH, gw), lambda g, c: (cc(c), g))
    bm = pl.BlockSpec((CH, NST), lambda g, c: (cc(c), d // LANE + g))
    cm = pl.BlockSpec((CH, NST), lambda g, c: (cc(c), d // LANE + M2G + g))
    row = pl.BlockSpec((1, 1, hpg, CH), lambda g, c: (g, cc(c), 0, 0))
    st = pl.BlockSpec((1, 1, hpg // 2, NST, LANE), lambda g, c: (g, cc(c), 0, 0, 0))
    return xs, bm, cm, row, st, hpg


def _ssd_fwd(xbc, dt, a, d):
    t_rows = xbc.shape[0]
    nc = t_rows // CH
    xs, bm, cm, row, st, hpg = _ssd_specs(nc, d, False)
    ppg = hpg // 2

    def body(xs_ref, b_ref, c_ref, dt_ref, a_ref, y_ref, ss_ref, s_scr):
        c = pl.program_id(1)

        @pl.when(c == 0)
        def _():
            s_scr[...] = jnp.zeros_like(s_scr)

        s0 = [s_scr[p] for p in range(ppg)]
        for p in range(ppg):
            ss_ref[0, 0, p] = s0[p]
        dt_r = [dt_ref[0, 0, pl.ds(h, 1), :] for h in range(hpg)]
        a_r = [a_ref[0, 0, pl.ds(h, 1), :] for h in range(hpg)]
        s1, y = _ssd_group(s0, xs_ref[...], b_ref[...], c_ref[...], dt_r, a_r)
        y_ref[...] = y
        for p in range(ppg):
            s_scr[p] = s1[p]

    return pl.pallas_call(
        body, name="ssd_fwd", grid=(M2G, nc), in_specs=[xs, bm, cm, row, row], out_specs=[xs, st],
        out_shape=[jax.ShapeDtypeStruct((t_rows, d), F32), jax.ShapeDtypeStruct((M2G, nc, ppg, NST, LANE), F32)],
        scratch_shapes=[pltpu.VMEM((ppg, NST, LANE), F32)],
        compiler_params=_cparams(("parallel", "arbitrary"), 32),
    )(xbc, xbc, xbc, dt, a)


def _ssd_bwd(xbc, dt, a, ss, dy, d):
    t_rows = xbc.shape[0]
    nc = t_rows // CH
    xs, bm, cm, row, st, hpg = _ssd_specs(nc, d, True)
    ppg = hpg // 2

    def body(xs_ref, b_ref, c_ref, dt_ref, a_ref, ss_ref, dy_ref, dxs_ref, db_ref, dc_ref, ddt_ref, da_ref, ds_scr):
        cr = pl.program_id(1)

        @pl.when(cr == 0)
        def _():
            ds_scr[...] = jnp.zeros_like(ds_scr)

        first = cr == nc - 1
        keep_c = jnp.logical_or(jnp.logical_not(first), lax.broadcasted_iota(jnp.int32, (CH, 1), 0) >= PADR)
        keep_r = jnp.logical_or(jnp.logical_not(first), lax.broadcasted_iota(jnp.int32, (1, CH), 1) >= PADR)
        dt_r = [dt_ref[0, 0, pl.ds(h, 1), :] for h in range(hpg)]
        a_r = [a_ref[0, 0, pl.ds(h, 1), :] for h in range(hpg)]
        s0 = [ss_ref[0, 0, p] for p in range(ppg)]
        _, vjp = jax.vjp(_ssd_group, s0, xs_ref[...], b_ref[...], c_ref[...], dt_r, a_r)
        ds0, dxs, db, dc, ddt, da = vjp(([ds_scr[p] for p in range(ppg)], dy_ref[...]))
        for p in range(ppg):
            ds_scr[p] = ds0[p]
        dxs_ref[...] = jnp.where(keep_c, dxs, 0.0)
        db_ref[...] = jnp.where(keep_c, db, 0.0)
        dc_ref[...] = jnp.where(keep_c, dc, 0.0)
        for h in range(hpg):
            ddt_ref[0, 0, pl.ds(h, 1), :] = jnp.where(keep_r, ddt[h], 0.0)
            da_ref[0, 0, pl.ds(h, 1), :] = jnp.where(keep_r, da[h], 0.0)

    grp = pl.BlockSpec((CH, NST), lambda g, c: (nc - 1 - c, g))
    return pl.pallas_call(
        body, name="ssd_bwd", grid=(M2G, nc), in_specs=[xs, bm, cm, row, row, st, xs],
        out_specs=[xs, grp, grp, row, row],
        out_shape=[jax.ShapeDtypeStruct((t_rows, d), F32)] + [jax.ShapeDtypeStruct((t_rows, M2G * NST), F32)] * 2
        + [jax.ShapeDtypeStruct((M2G, nc, hpg, CH), F32)] * 2,
        scratch_shapes=[pltpu.VMEM((ppg, NST, LANE), F32)],
        compiler_params=_cparams(("parallel", "arbitrary"), 32),
    )(xbc, xbc, xbc, dt, a, ss, dy)


def _exchange(name, gathers, scatters, after):
    arrays = list(gathers) + list(scatters)
    n_g, n = len(gathers), len(arrays)

    def body(*refs):
        ins, outs = refs[:n], refs[n + 1:2 * n + 1]
        send_sems, recv_sems, local_sems = refs[2 * n + 1:]
        x, y, c = lax.axis_index("x"), lax.axis_index("y"), lax.axis_index("c")
        me = 4 * x + 2 * y + c

        def src(a, slot):
            return ins[a] if a < n_g else ins[a].at[slot]

        local = [pltpu.make_async_copy(src(a, me), outs[a].at[me], local_sems.at[a]) for a in range(n)]
        for cp in local:
            cp.start()
        copies = []
        for rel in range(1, NDEV):
            px, py, pc = x ^ (rel >> 2), y ^ ((rel >> 1) & 1), c ^ (rel & 1)
            peer = 4 * px + 2 * py + pc
            for a in range(n):
                copies.append(pltpu.make_async_remote_copy(
                    src_ref=src(a, peer), dst_ref=outs[a].at[me], send_sem=send_sems.at[a, rel - 1],
                    recv_sem=recv_sems.at[a, rel - 1], device_id=(px, py, pc), device_id_type=pl.DeviceIdType.MESH))
        for cp in copies:
            cp.start()
        for cp in copies:
            cp.wait_recv()
        for cp in copies:
            cp.wait_send()
        for cp in local:
            cp.wait()

    any_spec = pl.BlockSpec(memory_space=pl.ANY)
    out_shape = [jax.ShapeDtypeStruct((NDEV,) + a.shape, a.dtype) for a in gathers]
    out_shape += [jax.ShapeDtypeStruct(a.shape, a.dtype) for a in scatters]
    return pl.pallas_call(
        body, name=name, in_specs=[any_spec] * (n + 1), out_specs=[any_spec] * n, out_shape=out_shape,
        scratch_shapes=[pltpu.SemaphoreType.DMA((n, NDEV - 1)), pltpu.SemaphoreType.DMA((n, NDEV - 1)),
                        pltpu.SemaphoreType.DMA((n,))],
        compiler_params=pltpu.CompilerParams(has_side_effects=True),
    )(*arrays, after)


def _gather_two_level(name, arrays):
    n = len(arrays)

    def body(*refs):
        ins, outs = refs[:n], refs[n:2 * n]
        send_sems, recv_sems, local_sems = refs[2 * n:]
        x, y, c = lax.axis_index("x"), lax.axis_index("y"), lax.axis_index("c")
        me, sibling = (x, y, c), (x, y, 1 - c)
        chips = [(1 - x, y), (x, 1 - y), (1 - x, 1 - y)]

        def copy(a, k, block, to, src=None):
            dst = outs[a].at[4 * block[0] + 2 * block[1] + block[2]]
            return pltpu.make_async_remote_copy(
                src_ref=dst if src is None else src, dst_ref=dst, send_sem=send_sems.at[a, k], recv_sem=recv_sems.at[a, k],
                device_id=to, device_id_type=pl.DeviceIdType.MESH)

        mine = [pltpu.make_async_copy(ins[a], outs[a].at[4 * x + 2 * y + c], local_sems.at[a]) for a in range(n)]
        for cp in mine:
            cp.start()
        first = []
        for a in range(n):
            first.append(copy(a, 0, me, sibling, src=ins[a]))
            first += [copy(a, 1 + j, me, (*chip, c), src=ins[a]) for j, chip in enumerate(chips)]
        for cp in first:
            cp.start()
        passed = [[copy(a, 4 + j, (*chip, c), sibling) for j, chip in enumerate(chips)] for a in range(n)]
        for j, chip in enumerate(chips):
            for a in range(n):
                copy(a, 1 + j, (*chip, c), me).wait_recv()
                passed[a][j].start()
        for a in range(n):
            copy(a, 0, sibling, me).wait_recv()
            for j, chip in enumerate(chips):
                copy(a, 4 + j, (*chip, 1 - c), me).wait_recv()
        for cp in first + [cp for row in passed for cp in row]:
            cp.wait_send()
        for cp in mine:
            cp.wait()

    any_spec = pl.BlockSpec(memory_space=pl.ANY)
    return pl.pallas_call(
        body, name=name, in_specs=[any_spec] * n, out_specs=[any_spec] * n,
        out_shape=[jax.ShapeDtypeStruct((NDEV,) + a.shape, a.dtype) for a in arrays],
        scratch_shapes=[pltpu.SemaphoreType.DMA((n, NDEV - 1)), pltpu.SemaphoreType.DMA((n, NDEV - 1)),
                        pltpu.SemaphoreType.DMA((n,))],
        compiler_params=pltpu.CompilerParams(has_side_effects=True),
    )(*arrays)


_HBM = pl.BlockSpec(memory_space=pltpu.HBM)
_SEM = pl.BlockSpec(memory_space=pltpu.SEMAPHORE)
_EFFECT = pltpu.SideEffectType.DATAFLOW_SIDE_EFFECTING


def _split_copies(srcs, lands, send_sems, recv_sems, n_g):
    x, y, c = lax.axis_index("x"), lax.axis_index("y"), lax.axis_index("c")
    me = 4 * x + 2 * y + c
    copies = []
    for rel in range(1, NDEV):
        px, py, pc = x ^ (rel >> 2), y ^ ((rel >> 1) & 1), c ^ (rel & 1)
        peer = 4 * px + 2 * py + pc
        for a in range(len(srcs)):
            copies.append(pltpu.make_async_remote_copy(
                src_ref=srcs[a] if a < n_g else srcs[a].at[peer], dst_ref=lands[a].at[me],
                send_sem=send_sems.at[a * (NDEV - 1) + rel - 1], recv_sem=recv_sems.at[a * (NDEV - 1) + rel - 1],
                device_id=(px, py, pc), device_id_type=pl.DeviceIdType.MESH))
    return copies


def _exchange_start(name, gathers, scatters, after):
    arrays = list(gathers) + list(scatters)
    n_g, n = len(gathers), len(arrays)
    lands = [lax.empty((NDEV,) + a.shape, a.dtype) for a in gathers] + [lax.empty(a.shape, a.dtype) for a in scatters]

    def body(*refs):
        send_sems, recv_sems = refs[2 * n + 1], refs[2 * n + 2]
        for cp in _split_copies(refs[:n], refs[n:2 * n], send_sems, recv_sems, n_g):
            cp.start()
        refs[-1][...] = jnp.zeros_like(refs[-1])

    sems = pltpu.SemaphoreType.DMA((n * (NDEV - 1),))
    out = pl.pallas_call(
        body, name=name, in_specs=[_HBM] * (2 * n) + [pl.BlockSpec(memory_space=pl.ANY)],
        out_specs=(_SEM, _SEM, *[_HBM] * (2 * n), pl.BlockSpec(memory_space=pltpu.VMEM)),
        out_shape=(sems, sems, *[pltpu.HBM(a.shape, a.dtype) for a in arrays + lands], jax.ShapeDtypeStruct((8, LANE), F32)),
        input_output_aliases={i: 2 + i for i in range(2 * n)},
        compiler_params=pltpu.CompilerParams(has_side_effects=_EFFECT),
    )(*[pltpu.with_memory_space_constraint(a, pltpu.HBM) for a in arrays + lands], after)
    return out[0], out[1], list(out[2:2 + 2 * n]), out[-1], n_g


def _exchange_wait(name, started, after):
    send_sems, recv_sems, thru, _, n_g = started
    n = len(thru) // 2

    def body(*refs):
        for cp in _split_copies(refs[:n], refs[n:2 * n], refs[2 * n], refs[2 * n + 1], n_g):
            cp.wait_send()
            cp.wait_recv()

    out = pl.pallas_call(
        body, name=name, in_specs=[_HBM] * (2 * n) + [_SEM, _SEM, pl.BlockSpec(memory_space=pl.ANY)],
        out_specs=[_HBM] * (2 * n), out_shape=[pltpu.HBM(a.shape, a.dtype) for a in thru],
        input_output_aliases={i: i for i in range(2 * n)},
        compiler_params=pltpu.CompilerParams(has_side_effects=_EFFECT),
    )(*thru, send_sems, recv_sems, after)
    me = 4 * lax.axis_index("x") + 2 * lax.axis_index("y") + lax.axis_index("c")
    full = []
    for a in range(n):
        own = out[a][None] if a < n_g else lax.dynamic_index_in_dim(out[a], me, 0, keepdims=True)
        full.append(lax.dynamic_update_index_in_dim(out[n + a], own, me, 0))
    return full


def _adamw(name, staged, w, m, v):
    r, c = w.shape
    tr = _pick(r, 256, 8)

    def body(st_ref, w_ref, m_ref, v_ref, g_ref, d_ref, nm_ref, nv_ref):
        g = st_ref[0].astype(F32)
        for k in range(1, NDEV):
            g = g + st_ref[k].astype(F32)
        m_new = ADAM_B1 * m_ref[...] + (1.0 - ADAM_B1) * g
        v_new = ADAM_B2 * v_ref[...] + (1.0 - ADAM_B2) * jnp.square(g)
        m_hat = m_new / (1.0 - ADAM_B1 ** ADAM_STEP)
        v_hat = v_new / (1.0 - ADAM_B2 ** ADAM_STEP)
        g_ref[...] = g
        d_ref[...] = -ADAM_LR * (m_hat / (jnp.sqrt(v_hat) + ADAM_EPS) + ADAM_WD * w_ref[...])
        nm_ref[...] = m_new
        nv_ref[...] = v_new

    blk = pl.BlockSpec((tr, c), lambda i: (i, 0))
    return pl.pallas_call(
        body, name=name, grid=(r // tr,), in_specs=[pl.BlockSpec((NDEV, tr, c), lambda i: (0, i, 0)), blk, blk, blk],
        out_specs=[blk] * 4, out_shape=[jax.ShapeDtypeStruct((r, c), F32)] * 4,
        compiler_params=_cparams(("parallel",), 48),
    )(staged, w, m, v)


def _pack(parts):
    flat = jnp.concatenate([p.reshape(-1).astype(F32) for p in parts])
    pad = (-flat.shape[0]) % (8 * LANE)
    return jnp.pad(flat, (0, pad)).reshape(-1, LANE)


def _unpack(slab, shapes):
    flat, out, off = slab.reshape(-1), [], 0
    for s in shapes:
        n = 1
        for dim in s:
            n *= dim
        out.append(flat[off:off + n].reshape(s))
        off += n
    return out


def _to_shards(full, axis):
    shp = full.shape
    t = full.reshape(shp[:axis] + (NDEV, shp[axis] // NDEV) + shp[axis + 1:])
    return jnp.moveaxis(t, axis, 0)


def _from_shards(g, axis):
    t = jnp.moveaxis(g, 0, axis)
    shp = t.shape
    return t.reshape(shp[:axis] + (shp[axis] * shp[axis + 1],) + shp[axis + 2:])


def kernel(x, meta_tokens, norm_mix_w, w_in, dn_conv_w, dn_a_log, dn_dt_bias, dn_norm_w, m2_conv_w, m2_conv_b, m2_a_log, m2_dt_bias, m2_d, m2_norm_w, w_out, norm_ffn_w, ffn_up, ffn_conv_w, ffn_down, norm_final_w, loss_target, m_meta_tokens, m_norm_mix_w, m_w_in, m_dn_conv_w, m_dn_a_log, m_dn_dt_bias, m_dn_norm_w, m_m2_conv_w, m_m2_conv_b, m_m2_a_log, m_m2_dt_bias, m_m2_d, m_m2_norm_w, m_w_out, m_norm_ffn_w, m_ffn_up, m_ffn_conv_w, m_ffn_down, m_norm_final_w, v_meta_tokens, v_norm_mix_w, v_w_in, v_dn_conv_w, v_dn_a_log, v_dn_dt_bias, v_dn_norm_w, v_m2_conv_w, v_m2_conv_b, v_m2_a_log, v_m2_dt_bias, v_m2_d, v_m2_norm_w, v_w_out, v_norm_ffn_w, v_ffn_up, v_ffn_conv_w, v_ffn_down, v_norm_final_w):
    seq, d = x.shape[1], x.shape[2]
    t_rows = seq + CH
    nc = t_rows // CH
    dnh, m2h = d // HD, d // M2P
    dff = ffn_down.shape[1] * NDEV
    xbc_w = d + 2 * M2G * NST
    assert seq % CH == 0 and d % (2 * M2P * M2G) == 0 and 2 * dnh + m2h <= LANE
    hb_f = max(h for h in (16, 8, 4, 2, 1) if dnh % h == 0)
    hb_b = max(h for h in (8, 4, 2, 1) if dnh % h == 0)
    tm_rw = _pick(t_rows, 208, 16)
    conv_chunk = _pick(t_rows, min(CONV_CHUNK, t_rows // 3), 16)

    small_sharded = [meta_tokens, dn_conv_w[0], m2_conv_w[0], ffn_conv_w[0]]
    small_shapes = [p.shape for p in small_sharded]
    g_win, g_small = _gather_two_level("gather_w_in", [w_in[0].astype(WIRE), _pack(small_sharded)])
    rest = _exchange_start("gather_rest_start", [w_out[0].astype(WIRE), ffn_up[0].astype(WIRE), ffn_down[0].astype(WIRE)], [],
                           g_small)
    win = _from_shards(g_win, 1)
    small_full = [_unpack(g_small[k], small_shapes) for k in range(NDEV)]
    meta_f, dnconv_f, m2conv_f, ffnconv_f = [jnp.concatenate([small_full[k][i] for k in range(NDEV)], axis=-1) for i in range(4)]

    o_z, o_b, o_a = 3 * d, 4 * d, 4 * d + dnh
    o_m2z = 4 * d + 2 * dnh
    o_xbc, o_dt = o_m2z + d, o_m2z + d + xbc_w
    w_all = jnp.concatenate([win[:, :o_b], win[:, o_m2z:o_dt], win[:, o_b:o_m2z], win[:, o_dt:],
                             jnp.zeros((d, LANE - 2 * dnh - m2h), WIRE)], axis=1)
    seg_cols = {"q": (0, d), "k": (d, d), "v": (2 * d, d), "z": (3 * d, d), "m2z": (4 * d, d), "xbc": (5 * d, xbc_w),
                "sm": (5 * d + xbc_w, LANE)}

    h0 = jnp.concatenate([jnp.zeros((PADR, d), F32), meta_f, x[0]], axis=0)
    valid = lambda rows: rows >= PADR

    def norm_fwd(name, h, w):
        return _rw(name, lambda rows, j, hv, wv: (_rms(hv, wv),), [("r", h, d, _c0), ("p", w, d, _c0)],
                   [("r", d, d, _c0, MXU)], t_rows, tm_rw)[0]

    hn1 = norm_fwd("norm_mix", h0, norm_mix_w)
    proj = {s: _mm("proj_" + s, hn1, w_all, b_cols=seg_cols[s], dep=rest[3]) for s in seg_cols}

    def dn_post(sec, cv):
        s = _silu(cv)
        if sec < 2:
            s = s * lax.rsqrt(jnp.sum(s * s, axis=-1, keepdims=True) + EPS)
        if sec == 0:
            s = s * (HD ** -0.5)
        return s

    def dn_prep(sec, name):
        def fn(rows, own, last, wins, pars):
            return [jnp.where(valid(rows), dn_post(sec, _conv(wins[0], pars[0])), 0.0)], []
        wc = dnconv_f[:, sec * d:(sec + 1) * d]
        return _cv("dn_prep_" + name, fn, [(proj[name], _cj)], [(wc, _cj)], [(d, _cj, F32)], [], t_rows, dnh)[0]

    q_act, k_act, v_act = dn_prep(0, "q"), dn_prep(1, "k"), dn_prep(2, "v")

    lane = lambda: lax.broadcasted_iota(jnp.int32, (1, LANE), 1)

    def lanes_of(vec, off):
        return jnp.pad(vec.astype(F32), ((0, 0), (off, LANE - off - vec.shape[1])))

    gate_params = [lanes_of(dn_a_log, dnh), lanes_of(dn_dt_bias, dnh), lanes_of(m2_a_log, 2 * dnh), lanes_of(m2_dt_bias, 2 * dnh)]

    def gates(rows, sm, p_alog, p_dtb, p_malog, p_mdtb):
        ln = lane()
        is_b, is_g = ln < dnh, jnp.logical_and(ln >= dnh, ln < 2 * dnh)
        is_d = jnp.logical_and(ln >= 2 * dnh, ln < 2 * dnh + m2h)
        beta = jax.nn.sigmoid(sm)
        gdec = -jnp.exp(p_alog) * _softplus(sm + p_dtb)
        dt = _softplus(sm + p_mdtb)
        am = dt * (-jnp.exp(p_malog))
        ok = valid(rows)
        g1 = jnp.where(ok, jnp.where(is_b, beta, jnp.where(is_g, gdec, jnp.where(is_d, dt, 0.0))), 0.0)
        g2 = jnp.where(jnp.logical_and(ok, is_d), am, 0.0)
        return g1, g2

    gate_ins = [("r", proj["sm"], LANE, _c0)] + [("p", p, LANE, _c0) for p in gate_params]
    g1, g2 = _rw("gates", lambda rows, j, *a: gates(rows, *a), gate_ins,
                 [("r", LANE, LANE, _c0, F32), ("r", LANE, LANE, _c0, F32)], t_rows, tm_rw)

    def head_rows(cols, per):
        n = cols.shape[1]
        return cols.reshape(nc, CH, n // per, per).transpose(2, 0, 3, 1)

    def head_cols(rows_):
        ngrp, _, per, _ = rows_.shape
        return rows_.transpose(1, 3, 0, 2).reshape(t_rows, ngrp * per)

    beta_r, gdec_r = head_rows(g1[:, :dnh], hb_f), head_rows(g1[:, dnh:2 * dnh], hb_f)
    hpg = m2h // M2G
    dt_r, am_r = head_rows(g1[:, 2 * dnh:2 * dnh + m2h], hpg), head_rows(g2[:, 2 * dnh:2 * dnh + m2h], hpg)

    o_dn, dn_states, dn_tinv = _gdn_fwd(q_act, k_act, v_act, beta_r, gdec_r, hb_f)

    def dn_out(o, z, w):
        outs = []
        for h in range(dnh):
            sl = slice(h * HD, (h + 1) * HD)
            outs.append(_rms(o[:, sl], w) * _silu(z[:, sl]))
        return jnp.concatenate(outs, axis=1)

    mixed_dn = _rw("dn_out", lambda rows, j, o, z, w: (dn_out(o, z, w),),
                   [("r", o_dn, d, _c0), ("r", proj["z"], d, _c0), ("p", dn_norm_w, HD, _c0)], [("r", d, d, _c0, MXU)],
                   t_rows, tm_rw)[0]

    def m2_prep(rows, own, last, wins, pars):
        return [jnp.where(valid(rows), _silu(_conv(wins[0], pars[0]) + pars[1][0]), 0.0)], []

    xbc_act = _cv("m2_prep", m2_prep, [(proj["xbc"], _cj)], [(m2conv_f, _cj), (m2_conv_b, _cj)], [(xbc_w, _cj, F32)], [],
                  t_rows, xbc_w // LANE)[0]
    y_ssd, m2_states = _ssd_fwd(xbc_act, dt_r, am_r, d)

    d_lanes = jnp.repeat(m2_d.astype(F32), M2P, axis=1)
    gw = d // M2G

    def m2_out(ys, xs, z, dl, nw):
        yv = (ys + dl * xs) * _silu(z)
        outs = []
        for gi in range(M2G):
            sl = slice(gi * gw, (gi + 1) * gw)
            outs.append(_rms(yv[:, sl], nw[:, sl]))
        return jnp.concatenate(outs, axis=1)

    m2_out_ins = [("r", y_ssd, d, _c0), ("r", xbc_act, d, _c0), ("r", proj["m2z"], d, _c0), ("p", d_lanes, d, _c0),
                  ("p", m2_norm_w, d, _c0)]
    mixed_m2 = _rw("m2_out", lambda rows, j, *a: (m2_out(*a),), m2_out_ins, [("r", d, d, _c0, MXU)], t_rows, tm_rw)[0]

    mixed = jnp.concatenate([mixed_dn, mixed_m2], axis=1)
    g_wout, g_wup, g_wdown = _exchange_wait("gather_rest_wait", rest, mixed)
    wout = _from_shards(g_wout, 0)
    wup = _from_shards(g_wup, 1)
    wdown = _from_shards(g_wdown, 0)
    up_g, up_v = (0, dff), (dff, dff)
    h1 = _mm("out_proj", mixed, wout, add=h0)
    hn2 = norm_fwd("norm_ffn", h1, norm_ffn_w)
    u_g, u_v = _mm("ffn_up_g", hn2, wup, b_cols=up_g), _mm("ffn_up_v", hn2, wup, b_cols=up_v)
    fc_g, fc_v = ffnconv_f[:, :dff], ffnconv_f[:, dff:]

    def ffn_act(rows, own, last, wins, pars):
        return [jnp.where(valid(rows), _silu(_conv(wins[0], pars[0])) * _conv(wins[1], pars[1]), 0.0)], []

    act = _cv("ffn_act", ffn_act, [(u_g, _cj), (u_v, _cj)], [(fc_g, _cj), (fc_v, _cj)], [(dff, _cj, MXU)], [],
              t_rows, dff // LANE)[0]
    h2 = _mm("ffn_down", act, wdown, add=h1, tk=1408)

    def loss_fn(hv, wf, tgt, rows):
        err = jnp.where(rows >= CH, _rms(hv, wf) - tgt, 0.0)
        return 0.5 * jnp.sum(jnp.mean(err * err, axis=-1, keepdims=True), axis=0, keepdims=True)

    def final(rows, j, hv, wf, tgt):
        loss, vjp = jax.vjp(lambda a, b: loss_fn(a, b, tgt, rows), hv, wf)
        dh, dw = vjp(jnp.ones((1, 1), F32))
        return dh, dh, dw, jnp.broadcast_to(loss, (1, LANE))

    wf2 = norm_final_w.reshape(1, d)
    dh2, dh2_m, d_wf, loss_part = _rw(
        "loss_head", final, [("r", h2, d, _c0), ("p", wf2, d, _c0), ("r", loss_target[0], d, _c0, lambda i: jnp.maximum(i - 1, 0))],
        [("r", d, d, _c0, F32), ("r", d, d, _c0, MXU), ("p", 1, d, d, _c0), ("p", 1, LANE, LANE, _c0)], t_rows, CH)
    loss = lax.psum(loss_part[0, 0], MESH_AXES)

    d_act = _mm("d_act", dh2_m, wdown, tb=True)
    gw_down = _mm("gw_down", act, dh2_m, ta=True, tm=1408, tn=1024, tk=2080, out_dtype=WIRE)
    x_down = _exchange_start("grad_down_start", [], [_to_shards(gw_down, 0).astype(WIRE)], gw_down)

    def t_end(last):
        return t_rows if last else None

    def ffn_act_bwd(rows, own, last, wins, pars):
        (ug, uv, da), (wg, wv) = wins, pars
        cg, cv = _conv(ug, wg), _conv(uv, wv)
        _, vjp = jax.vjp(lambda a, b: _silu(a) * b, cg, cv)
        dcg, dcv = vjp(jnp.where(valid(rows), da, 0.0))
        return ([_conv_t(dcg, wg, rows, t_end(last)), _conv_t(dcv, wv, rows, t_end(last))],
                [_conv_w(jnp.where(own, dcg, 0.0), ug, len(wg)), _conv_w(jnp.where(own, dcv, 0.0), uv, len(wv))])

    kf = fc_g.shape[0]
    du_g, du_v, g_fc_g, g_fc_v = _cv(
        "ffn_act_bwd", ffn_act_bwd, [(u_g, _cj), (u_v, _cj), (d_act, _cj)], [(fc_g, _cj), (fc_v, _cj)],
        [(dff, _cj, MXU), (dff, _cj, MXU)], [(kf, dff, _cj), (kf, dff, _cj)], t_rows, dff // LANE, chunk=conv_chunk)
    gw_up_g = _mm("gw_up_g", hn2, du_g, ta=True, tm=1024, tn=1408, tk=2080, out_dtype=WIRE, dep=x_down[3])
    gw_up_v = _mm("gw_up_v", hn2, du_v, ta=True, tm=1024, tn=1408, tk=2080, out_dtype=WIRE)
    gw_up_full = jnp.concatenate([gw_up_g, gw_up_v], axis=1)
    x_up = _exchange_start("grad_up_start", [], [_to_shards(gw_up_full, 1).astype(WIRE)], gw_up_full)
    d_hn2 = _mm("d_hn2_v", du_v, wup, b_cols=up_v, tb=True, tk=1408, dep=x_up[3],
                add=_mm("d_hn2_g", du_g, wup, b_cols=up_g, tb=True, tk=1408, tn=2048, dep=x_up[3]))

    def norm_bwd(name, h, w, dy, dres):
        def fn(rows, j, hv, wv, dyv, dr):
            _, vjp = jax.vjp(_rms, hv, wv)
            dh, dw = vjp(dyv)
            dh = dh + dr
            return dh, dh, dw
        return _rw(name, fn, [("r", h, d, _c0), ("p", w, d, _c0), ("r", dy, d, _c0), ("r", dres, d, _c0)],
                   [("r", d, d, _c0, F32), ("r", d, d, _c0, MXU), ("p", 1, d, d, _c0)], t_rows, tm_rw)

    dh1, dh1_m, g_norm_ffn = norm_bwd("norm_ffn_bwd", h1, norm_ffn_w, d_hn2, dh2)

    gw_out = _mm("gw_out", mixed, dh1_m, ta=True, tm=1024, tn=1024, tk=2080, out_dtype=WIRE)
    x_out = _exchange_start("grad_out_start", [], [_to_shards(gw_out, 0).astype(WIRE)], gw_out)
    d_mixed = _mm("d_mixed", dh1_m, wout, tb=True, dep=x_out[3])

    gw_seg = {}

    def gw_in(seg, dseg_arr):
        gw_seg[seg] = _mm("gw_in_" + seg, hn1, dseg_arr, ta=True, tm=1024, tn=1024, tk=2080, out_dtype=WIRE)

    def m2_out_bwd(rows, j, ys, xs, z, dl, nw, dy):
        _, vjp = jax.vjp(m2_out, ys, xs, z, dl, nw)
        return vjp(dy)

    dy_ssd, dxs_skip, d_m2z, g_d_lanes, g_m2_norm = _rw(
        "m2_out_bwd", m2_out_bwd, m2_out_ins + [("r", d_mixed, d, lambda j: 1)],
        [("r", d, d, _c0, F32), ("r", d, d, _c0, F32), ("r", d, d, _c0, MXU), ("p", 1, d, d, _c0), ("p", 1, d, d, _c0)],
        t_rows, _pick(t_rows, 208, 16))
    gw_in("m2z", d_m2z)

    def fold_heads(vec_ref, out_ref):
        r = lax.broadcasted_iota(jnp.int32, (d, LANE), 0)
        c = lax.broadcasted_iota(jnp.int32, (d, LANE), 1)
        out_ref[...] = _dgh(vec_ref[...], jnp.where(jnp.logical_and(r >= c * M2P, r < (c + 1) * M2P), 1.0, 0.0), 1, 0)

    g_m2_d = pl.pallas_call(fold_heads, name="fold_m2_d", out_shape=jax.ShapeDtypeStruct((1, LANE), F32))(g_d_lanes)

    dxs, db_ssd, dc_ssd, ddt_r, dam_r = _ssd_bwd(xbc_act, dt_r, am_r, m2_states, dy_ssd, d)


    def m2_prep_bwd(rows, own, last, wins, pars):
        (p, *ds), (w, b) = wins, pars
        _, vjp = jax.vjp(_silu, _conv(p, w) + b[0])
        dpre, = vjp(jnp.where(valid(rows), functools.reduce(lambda a_, b_: a_ + b_, ds), 0.0))
        dpre_own = jnp.where(own, dpre, 0.0)
        return [_conv_t(dpre, w, rows, t_end(last))], [_conv_w(dpre_own, p, len(w)), [jnp.sum(dpre_own, axis=0, keepdims=True)]]

    def m2_prep_bwd_call(name, off, width, d_arrs):
        at = lambda j, blk0=off // LANE: blk0 + j
        return _cv(name, m2_prep_bwd, [(proj["xbc"], at)] + [(a, _cj) for a in d_arrs], [(m2conv_f, at), (m2_conv_b, at)],
                   [(width, _cj, MXU)], [(m2conv_f.shape[0], width, _cj), (1, width, _cj)], t_rows, width // LANE, chunk=conv_chunk)

    dp_xs, gcw_xs, gcb_xs = m2_prep_bwd_call("m2_prep_bwd_x", 0, d, [dxs, dxs_skip])
    dp_b, gcw_b, gcb_b = m2_prep_bwd_call("m2_prep_bwd_b", d, M2G * NST, [db_ssd])
    dp_c, gcw_c, gcb_c = m2_prep_bwd_call("m2_prep_bwd_c", d + M2G * NST, M2G * NST, [dc_ssd])
    d_pxbc = jnp.concatenate([dp_xs, dp_b, dp_c], axis=1)
    gw_in("xbc", d_pxbc)
    g_m2_conv = jnp.concatenate([gcw_xs, gcw_b, gcw_c], axis=1)
    g_m2_conv_b = jnp.concatenate([gcb_xs, gcb_b, gcb_c], axis=1)

    def dn_out_bwd(rows, j, o, z, w, dy):
        _, vjp = jax.vjp(dn_out, o, z, w)
        return vjp(dy)

    d_o, d_z, g_dn_norm = _rw(
        "dn_out_bwd", dn_out_bwd,
        [("r", o_dn, d, _c0), ("r", proj["z"], d, _c0), ("p", dn_norm_w, HD, _c0), ("r", d_mixed, d, _c0)],
        [("r", d, d, _c0, F32), ("r", d, d, _c0, MXU), ("p", 1, HD, HD, _c0)], t_rows, _pick(t_rows, 208, 16))
    gw_in("z", d_z)

    dq, dk, dv, dbeta_r, dgdec_r = _gdn_bwd(q_act, k_act, v_act, head_rows(g1[:, :dnh], hb_b),
                                            head_rows(g1[:, dnh:2 * dnh], hb_b), dn_states, dn_tinv, d_o, hb_b)

    def dn_prep_bwd(sec, name, dact):
        def fn(rows, own, last, wins, pars):
            (p, da), (w,) = wins, pars
            _, vjp = jax.vjp(functools.partial(dn_post, sec), _conv(p, w))
            dcv, = vjp(jnp.where(valid(rows), da, 0.0))
            return [_conv_t(dcv, w, rows, t_end(last))], [_conv_w(jnp.where(own, dcv, 0.0), p, len(w))]
        wc = dnconv_f[:, sec * d:(sec + 1) * d]
        return _cv("dn_prep_bwd_" + name, fn, [(proj[name], _cj), (dact, _cj)], [(wc, _cj)], [(d, _cj, MXU)],
                   [(wc.shape[0], d, _cj)], t_rows, dnh, chunk=conv_chunk if sec == 2 else None)

    (dp_q, gcw_q), (dp_k, gcw_k), (dp_v, gcw_v) = dn_prep_bwd(0, "q", dq), dn_prep_bwd(1, "k", dk), dn_prep_bwd(2, "v", dv)
    gw_in("q", dp_q), gw_in("k", dp_k), gw_in("v", dp_v)
    g_dn_conv = jnp.concatenate([gcw_q, gcw_k, gcw_v], axis=1)

    zpad = jnp.zeros((t_rows, LANE - 2 * dnh - m2h), F32)
    dg1 = jnp.concatenate([head_cols(dbeta_r), head_cols(dgdec_r), head_cols(ddt_r), zpad], axis=1)
    dg2 = jnp.concatenate([jnp.zeros((t_rows, 2 * dnh), F32), head_cols(dam_r), zpad], axis=1)

    def gates_bwd(rows, j, sm, pa, pb, pc, pd, d1, d2):
        _, vjp = jax.vjp(lambda *a: gates(rows, *a), sm, pa, pb, pc, pd)
        return vjp((d1, d2))

    dp_sm, g_pa, g_pb, g_pc, g_pd = _rw(
        "gates_bwd", gates_bwd, gate_ins + [("r", dg1, LANE, _c0), ("r", dg2, LANE, _c0)],
        [("r", LANE, LANE, _c0, MXU)] + [("p", 1, LANE, LANE, _c0)] * 4, t_rows, tm_rw)

    dseg = {"q": dp_q, "k": dp_k, "v": dp_v, "z": d_z, "m2z": d_m2z, "xbc": d_pxbc, "sm": dp_sm}
    gw_in("sm", dp_sm)
    gsm = gw_seg["sm"]
    gw_in_full = jnp.concatenate([gw_seg["q"], gw_seg["k"], gw_seg["v"], gw_seg["z"], gsm[:, :2 * dnh], gw_seg["m2z"],
                                  gw_seg["xbc"], gsm[:, 2 * dnh:2 * dnh + m2h]], axis=1)
    x_in = _exchange_start("grad_in_start", [], [_to_shards(gw_in_full, 1).astype(WIRE)], gw_in_full)
    d_hn1 = None
    for s in dseg:
        d_hn1 = _mm("d_hn1_" + s, dseg[s], w_all, b_cols=seg_cols[s], tb=True, tk=2048, add=d_hn1, dep=x_in[3])
    dh0, _, g_norm_mix = norm_bwd("norm_mix_bwd", h0, norm_mix_w, d_hn1, dh1)

    g_ffn_conv = jnp.concatenate([g_fc_g, g_fc_v], axis=1)
    small_parts = [_to_shards(dh0[PADR:CH], 1), _to_shards(g_dn_conv, 1), _to_shards(g_m2_conv, 1), _to_shards(g_ffn_conv, 1)]
    small_scatter = jnp.stack([_pack([p[k] for p in small_parts]) for k in range(NDEV)])

    rep_names = ["norm_mix_w", "dn_a_log", "dn_dt_bias", "dn_norm_w", "m2_conv_b", "m2_a_log", "m2_dt_bias", "m2_d",
                 "m2_norm_w", "norm_ffn_w", "norm_final_w"]
    rep_grads = [g_norm_mix, g_pa[:, dnh:2 * dnh], g_pb[:, dnh:2 * dnh], g_dn_norm, g_m2_conv_b, g_pc[:, 2 * dnh:2 * dnh + m2h],
                 g_pd[:, 2 * dnh:2 * dnh + m2h], g_m2_d[:, :m2h], g_m2_norm, g_norm_ffn, d_wf.reshape(d)]

    weights = dict(meta_tokens=meta_tokens, norm_mix_w=norm_mix_w, w_in=w_in, dn_conv_w=dn_conv_w, dn_a_log=dn_a_log,
                   dn_dt_bias=dn_dt_bias, dn_norm_w=dn_norm_w, m2_conv_w=m2_conv_w, m2_conv_b=m2_conv_b, m2_a_log=m2_a_log,
                   m2_dt_bias=m2_dt_bias, m2_d=m2_d, m2_norm_w=m2_norm_w, w_out=w_out, norm_ffn_w=norm_ffn_w, ffn_up=ffn_up,
                   ffn_conv_w=ffn_conv_w, ffn_down=ffn_down, norm_final_w=norm_final_w)
    mom1 = dict(meta_tokens=m_meta_tokens, norm_mix_w=m_norm_mix_w, w_in=m_w_in, dn_conv_w=m_dn_conv_w, dn_a_log=m_dn_a_log,
                dn_dt_bias=m_dn_dt_bias, dn_norm_w=m_dn_norm_w, m2_conv_w=m_m2_conv_w, m2_conv_b=m_m2_conv_b,
                m2_a_log=m_m2_a_log, m2_dt_bias=m_m2_dt_bias, m2_d=m_m2_d, m2_norm_w=m_m2_norm_w, w_out=m_w_out,
                norm_ffn_w=m_norm_ffn_w, ffn_up=m_ffn_up, ffn_conv_w=m_ffn_conv_w, ffn_down=m_ffn_down,
                norm_final_w=m_norm_final_w)
    mom2 = dict(meta_tokens=v_meta_tokens, norm_mix_w=v_norm_mix_w, w_in=v_w_in, dn_conv_w=v_dn_conv_w, dn_a_log=v_dn_a_log,
                dn_dt_bias=v_dn_dt_bias, dn_norm_w=v_dn_norm_w, m2_conv_w=v_m2_conv_w, m2_conv_b=v_m2_conv_b,
                m2_a_log=v_m2_a_log, m2_dt_bias=v_m2_dt_bias, m2_d=v_m2_d, m2_norm_w=v_m2_norm_w, w_out=v_w_out,
                norm_ffn_w=v_norm_ffn_w, ffn_up=v_ffn_up, ffn_conv_w=v_ffn_conv_w, ffn_down=v_ffn_down,
                norm_final_w=v_norm_final_w)
    res = {}

    def adam_big(name, started, after):
        staged, = _exchange_wait("grad_" + name + "_wait", started, after)
        outs = _adamw("adamw_" + name, staged, weights[name][0], mom1[name][0], mom2[name][0])
        res[name] = tuple(o[None] for o in outs)
        return outs[1]

    done = adam_big("ffn_down", x_down, dh0)
    done = adam_big("ffn_up", x_up, done)
    done = adam_big("w_out", x_out, done)
    st_rep, st_small = _exchange("exchange_small_grads", [_pack(rep_grads)], [small_scatter], done)
    adam_big("w_in", x_in, st_small)

    def adam_packed(label, staged, names):
        shapes = [weights[nm].shape for nm in names]
        outs = _adamw(label, staged, *[_pack([src[nm] for nm in names]) for src in (weights, mom1, mom2)])
        unpacked = [_unpack(o, shapes) for o in outs]
        for i, nm in enumerate(names):
            res[nm] = tuple(u[i] for u in unpacked)

    adam_packed("adamw_small_sharded", st_small, ["meta_tokens", "dn_conv_w", "m2_conv_w", "ffn_conv_w"])
    adam_packed("adamw_replicated", st_rep, rep_names)

    order = list(weights)
    grad_x = dh0[CH:][None]
    return (loss, grad_x, *[res[nm][0] for nm in order], *[res[nm][1] for nm in order], *[res[nm][2] for nm in order],
            *[res[nm][3] for nm in order])
```

```python
import functools
import math

import jax
import jax.numpy as jnp
from jax import lax
from jax.experimental import pallas as pl
from jax.experimental.pallas import tpu as pltpu

F32 = jnp.float32
MXU = jnp.bfloat16
WIRE = jnp.bfloat16
HI = lax.Precision.HIGH

NDEV = 8
CH = 64
NMETA = 16
PADR = CH - NMETA
EPS = 1e-6
HD = 128
M2P = 64
M2G = 4
SSD_GPS_F, SSD_GPS_B = 4, 1
NST = 128
LANE = 128

ADAM_LR, ADAM_B1, ADAM_B2, ADAM_EPS, ADAM_WD, ADAM_STEP = 0.001, 0.9, 0.999, 1e-08, 0.01, 10

MESH_AXES = ("x", "y", "c")


def _pick(n, target, mult=16):
    best = None
    for t in range(mult, min(n, target) + 1, mult):
        if n % t == 0:
            best = t
    return best if best is not None else n


def _dg(a, b, ca, cb):
    return lax.dot_general(a.astype(MXU), b.astype(MXU), (((ca,), (cb,)), ((), ())), preferred_element_type=F32)


def _dgh(a, b, ca, cb):
    return lax.dot_general(a, b, (((ca,), (cb,)), ((), ())), precision=HI, preferred_element_type=F32)


def _silu(x):
    return x * jax.nn.sigmoid(x)


def _softplus(x):
    return jnp.maximum(x, 0.0) + jnp.log1p(jnp.exp(-jnp.abs(x)))


def _rms(x, w):
    return x * lax.rsqrt(jnp.mean(x * x, axis=-1, keepdims=True) + EPS) * w


def _cparams(sem, vmem_mb):
    return pltpu.CompilerParams(dimension_semantics=sem, vmem_limit_bytes=vmem_mb << 20)


def _mm(name, a, b, *, ta=False, tb=False, add=None, out_dtype=F32, tm=1040, tn=1024, tk=2048, dep=None,
        b_cols=None):
    m, kdim = (a.shape[1], a.shape[0]) if ta else a.shape
    b_start, b_width = b_cols if b_cols is not None else (0, b.shape[1])
    n = b.shape[0] if tb else b_width
    if tb:
        kdim = b_width
    tm = _pick(m, tm, 128 if ta else 16)
    tn = _pick(n if tb else math.gcd(b_width, b_start), tn, 128)
    tk = _pick(math.gcd(b_width, b_start) if tb else kdim, tk, 16 if (ta and not tb) else 128)
    nk = kdim // tk
    bj0, bk0 = (0, b_start // tk) if tb else (b_start // tn, 0)
    ca, cb = (0 if ta else 1), (1 if tb else 0)

    def body(*refs):
        a_ref, b_ref = refs[0], refs[1]
        add_ref = refs[2] if add is not None else None
        if nk == 1:
            r = _dg(a_ref[...], b_ref[...], ca, cb)
            if add_ref is not None:
                r = r + add_ref[...].astype(F32)
            refs[-1][...] = r.astype(refs[-1].dtype)
            return
        o_ref, acc = refs[-2], refs[-1]
        k = pl.program_id(2)

        @pl.when(k == 0)
        def _():
            acc[...] = jnp.zeros_like(acc)

        acc[...] += _dg(a_ref[...], b_ref[...], ca, cb)

        @pl.when(k == nk - 1)
        def _():
            r = acc[...]
            if add_ref is not None:
                r = r + add_ref[...].astype(F32)
            o_ref[...] = r.astype(o_ref.dtype)

    a_spec = pl.BlockSpec((tk, tm), lambda i, j, k: (k, i)) if ta else pl.BlockSpec((tm, tk), lambda i, j, k: (i, k))
    b_spec = (pl.BlockSpec((tn, tk), lambda i, j, k: (j, k + bk0)) if tb
              else pl.BlockSpec((tk, tn), lambda i, j, k: (k, j + bj0)))
    in_specs, ops = [a_spec, b_spec], [a, b]
    if add is not None:
        in_specs.append(pl.BlockSpec((tm, tn), lambda i, j, k: (i, j)))
        ops.append(add)
    if dep is not None:
        in_specs.append(pl.BlockSpec((8, LANE), lambda i, j, k: (0, 0)))
        ops.append(dep)
    return pl.pallas_call(
        body, name=name, grid=(m // tm, n // tn, nk), in_specs=in_specs,
        out_specs=pl.BlockSpec((tm, tn), lambda i, j, k: (i, j)),
        out_shape=jax.ShapeDtypeStruct((m, n), out_dtype),
        scratch_shapes=[pltpu.VMEM((tm, tn), F32)] if nk > 1 else [],
        compiler_params=_cparams(("parallel", "parallel", "arbitrary"), 48),
    )(*ops)


def _rw(name, fn, ins, outs, nrows, tm, ncol=1, vmem_mb=48):
    nrow = nrows // tm
    sub = tm
    in_specs, ops = [], []
    for spec in ins:
        kind, arr, bw, cj = spec[:4]
        ops.append(arr)
        if kind == "r":
            ri = spec[4] if len(spec) > 4 else (lambda i: i)
            in_specs.append(pl.BlockSpec((tm, bw), lambda j, i, cj=cj, ri=ri: (ri(i), cj(j))))
        else:
            in_specs.append(pl.BlockSpec((arr.shape[0], bw), lambda j, i, cj=cj: (0, cj(j))))
    out_shape, out_specs = [], []
    for o in outs:
        if o[0] == "r":
            _, width, bw, cj, dt = o
            out_shape.append(jax.ShapeDtypeStruct((nrows, width), dt))
            out_specs.append(pl.BlockSpec((tm, bw), lambda j, i, cj=cj: (i, cj(j))))
        else:
            _, rows, width, bw, cj = o
            out_shape.append(jax.ShapeDtypeStruct((rows, width), F32))
            out_specs.append(pl.BlockSpec((rows, bw), lambda j, i, cj=cj: (0, cj(j))))
    n_in = len(ins)

    def body(*refs):
        j, i = pl.program_id(0), pl.program_id(1)
        in_refs, out_refs = refs[:n_in], refs[n_in:]
        pars = [ref[...] if spec[0] == "p" else None for spec, ref in zip(ins, in_refs)]

        def one(r0, nr):
            rows = i * tm + r0 + lax.broadcasted_iota(jnp.int32, (nr, 1), 0)
            vals = [par if spec[0] == "p" else ref[pl.ds(r0, nr), :] for spec, ref, par in zip(ins, in_refs, pars)]
            parts = []
            for o, val, ref in zip(outs, fn(rows, j, *vals), out_refs):
                if o[0] == "r":
                    ref[pl.ds(r0, nr), :] = val.astype(ref.dtype)
                else:
                    parts.append(val)
            return parts

        if sub >= tm:
            parts = one(0, tm)
        else:
            zero = [jnp.zeros((1, o[3]), F32) for o in outs if o[0] == "p"]
            parts = lax.fori_loop(
                0, tm // sub, lambda s, acc: [a + b for a, b in zip(acc, one(pl.multiple_of(s * sub, sub), sub))], zero)
        for ref, val in zip([r for o, r in zip(outs, out_refs) if o[0] == "p"], parts):
            @pl.when(i == 0)
            def _(ref=ref):
                ref[...] = jnp.zeros_like(ref)

            ref[...] += val

    return pl.pallas_call(
        body, name=name, grid=(ncol, nrow), in_specs=in_specs, out_specs=out_specs, out_shape=out_shape,
        compiler_params=_cparams(("parallel", "arbitrary"), vmem_mb),
    )(*ops)


def _c0(j):
    return 0


def _cj(j):
    return j


def _shift(x, s):
    if s == 0:
        return x
    return pltpu.roll(x, s % x.shape[0], 0)


def _conv(x, w):
    k = len(w)
    return functools.reduce(lambda a, b: a + b, [w[j] * _shift(x, k - 1 - j) for j in range(k)])


def _conv_t(dy, w, rows, t_end):
    k = len(w)
    terms = []
    for j in range(k):
        s = k - 1 - j
        v = _shift(dy, -s)
        if t_end is not None and s > 0:
            v = jnp.where(rows + s < t_end, v, 0.0)
        terms.append(w[j] * v)
    return functools.reduce(lambda a, b: a + b, terms)


def _conv_w(dy, x, k):
    return [jnp.sum(dy * _shift(x, k - 1 - j), axis=0, keepdims=True) for j in range(k)]


CONV_CHUNK = 320
HALO = 8


def _cv(name, fn, row_ins, par_ins, row_outs, par_outs, nrows, ncol, chunk=None):
    whole = chunk is None
    chunk = nrows if whole else chunk
    n_chunks = nrows // chunk
    assert nrows % chunk == 0 and (whole or n_chunks >= 3)
    n_ri, n_pi, n_ro = len(row_ins), len(par_ins), len(row_outs)

    def body(*refs):
        rin, pin = refs[:n_ri], refs[n_ri:n_ri + n_pi]
        rout, pout = refs[n_ri + n_pi:n_ri + n_pi + n_ro], refs[n_ri + n_pi + n_ro:]
        pars = [[p[pl.ds(r, 1), :] for r in range(p.shape[0])] for p in pin]

        def run(r0, top, bot, last):
            wlen = top + chunk + bot
            w0 = r0 - top if isinstance(r0, int) else pl.multiple_of(r0 - top, HALO)
            local = lax.broadcasted_iota(jnp.int32, (wlen, 1), 0)
            own = jnp.logical_and(local >= top, local < top + chunk)
            outs, parts = fn(w0 + local, own, last, [ref[pl.ds(w0, wlen), :] for ref in rin], pars)
            for ref, val in zip(rout, outs):
                ref[pl.ds(r0, chunk), :] = val[top:top + chunk].astype(ref.dtype)
            return parts

        def add(acc, parts):
            return [[a + b for a, b in zip(ra, rb)] for ra, rb in zip(acc, parts)]

        if whole:
            acc = run(0, 0, 0, True)
        else:
            acc = run(0, 0, HALO, False)
            acc = lax.fori_loop(1, n_chunks - 1,
                                lambda i, a: add(a, run(pl.multiple_of(i * chunk, chunk), HALO, HALO, False)), acc)
            acc = add(acc, run(nrows - chunk, HALO, 0, True))
        for ref, prow in zip(pout, acc):
            for r, v in enumerate(prow):
                ref[pl.ds(r, 1), :] = v

    in_specs = [pl.BlockSpec((nrows, LANE), lambda j, cj=cj: (0, cj(j))) for _, cj in row_ins]
    in_specs += [pl.BlockSpec((a.shape[0], LANE), lambda j, cj=cj: (0, cj(j))) for a, cj in par_ins]
    out_specs = [pl.BlockSpec((nrows, LANE), lambda j, cj=cj: (0, cj(j))) for _, cj, _ in row_outs]
    out_specs += [pl.BlockSpec((k, LANE), lambda j, cj=cj: (0, cj(j))) for k, _, cj in par_outs]
    out_shape = [jax.ShapeDtypeStruct((nrows, width), dt) for width, _, dt in row_outs]
    out_shape += [jax.ShapeDtypeStruct((k, width), F32) for k, width, _ in par_outs]
    return pl.pallas_call(
        body, name=name, grid=(ncol,), in_specs=in_specs, out_specs=out_specs, out_shape=out_shape,
        compiler_params=_cparams(("parallel",), 48),
    )(*[a for a, _ in row_ins], *[a for a, _ in par_ins])


def _tri():
    r = lax.broadcasted_iota(jnp.int32, (CH, CH), 0)
    c = lax.broadcasted_iota(jnp.int32, (CH, CH), 1)
    return r, c


def _col(row):
    r, c = _tri()
    return jnp.sum(jnp.where(r == c, row, 0.0), axis=1, keepdims=True)


def _cumsum_rc(g_r):
    r, c = _tri()
    g_c = _col(g_r)
    cs_r = jnp.sum(jnp.where(r <= c, g_c, 0.0), axis=0, keepdims=True)
    cs_c = jnp.sum(jnp.where(c <= r, g_r, 0.0), axis=1, keepdims=True)
    return cs_r, cs_c


def _decay(cs_r, cs_c):
    r, c = _tri()
    return jnp.exp(jnp.where(c <= r, cs_c - cs_r, -jnp.inf))


def _gdn_a(ks, betas, gs):
    r, c = _tri()
    cs = [_cumsum_rc(g) for g in gs]
    kk = [_dg(k, k, 1, 1) for k in ks]
    return [jnp.where(c < r, _col(b) * kki * _decay(*csi), 0.0) for b, kki, csi in zip(betas, kk, cs)]


def _neumann(a_list):
    r, c = _tri()
    xs = [jnp.where(r == c, 1.0, 0.0) - a for a in a_list]
    ps = list(a_list)
    n = 2
    while n < CH:
        ps = [_dgh(p, p, 1, 0) for p in ps]
        xs = [x + _dgh(x, p, 1, 0) for x, p in zip(xs, ps)]
        n *= 2
    return xs


def _gdn_rest(ss, qs, ks, vs, betas, gs, ts):
    n = range(len(ss))
    cs = [_cumsum_rc(g) for g in gs]
    dm = [_decay(*csi) for csi in cs]
    ecs = [jnp.exp(csi[1]) for csi in cs]
    bc = [_col(b) for b in betas]
    u = [_dgh(ts[i], vs[i] * bc[i], 1, 0) for i in n]
    w = [_dgh(ts[i], ks[i] * (bc[i] * ecs[i]), 1, 0) for i in n]
    ws = [_dg(w[i], ss[i], 1, 0) for i in n]
    v_new = [u[i] - ws[i] for i in n]
    qk = [_dg(qs[i], ks[i], 1, 1) * dm[i] for i in n]
    o_in = [_dg(qs[i] * ecs[i], ss[i], 1, 0) for i in n]
    o = [o_in[i] + _dg(qk[i], v_new[i], 1, 0) for i in n]
    g_last = [jnp.sum(g, axis=1, keepdims=True) for g in gs]
    s_new = [ss[i] * jnp.exp(g_last[i]) + _dg(ks[i] * jnp.exp(g_last[i] - cs[i][1]), v_new[i], 0, 0) for i in n]
    return s_new, o


def _gdn_fwd(q, k, v, beta, g, hb):
    t_rows, d = q.shape
    nc, ng, w = t_rows // CH, d // (HD * hb), HD * hb
    sls = [slice(h * HD, (h + 1) * HD) for h in range(hb)]

    def body(q_ref, k_ref, v_ref, b_ref, g_ref, o_ref, ss_ref, ts_ref, s_scr):
        c = pl.program_id(1)

        @pl.when(c == 0)
        def _():
            s_scr[...] = jnp.zeros_like(s_scr)

        qs, ks, vs = ([ref[:, sl] for sl in sls] for ref in (q_ref, k_ref, v_ref))
        br = [b_ref[0, 0, pl.ds(h, 1), :] for h in range(hb)]
        gr = [g_ref[0, 0, pl.ds(h, 1), :] for h in range(hb)]
        s0 = [s_scr[h] for h in range(hb)]
        tm = _neumann(_gdn_a(ks, br, gr))
        s1, o = _gdn_rest(s0, qs, ks, vs, br, gr, tm)
        for h in range(hb):
            ss_ref[0, 0, h] = s0[h]
            ts_ref[0, 0, h] = tm[h]
            o_ref[:, sls[h]] = o[h]
            s_scr[h] = s1[h]

    blk = pl.BlockSpec((CH, w), lambda n, c: (c, n))
    row = pl.BlockSpec((1, 1, hb, CH), lambda n, c: (n, c, 0, 0))
    return pl.pallas_call(
        body, name="gdn_fwd", grid=(ng, nc), in_specs=[blk, blk, blk, row, row],
        out_specs=[blk, pl.BlockSpec((1, 1, hb, HD, HD), lambda n, c: (n, c, 0, 0, 0)),
                   pl.BlockSpec((1, 1, hb, CH, CH), lambda n, c: (n, c, 0, 0, 0))],
        out_shape=[jax.ShapeDtypeStruct((t_rows, d), F32), jax.ShapeDtypeStruct((ng, nc, hb, HD, HD), F32),
                   jax.ShapeDtypeStruct((ng, nc, hb, CH, CH), F32)],
        scratch_shapes=[pltpu.VMEM((hb, HD, HD), F32)],
        compiler_params=_cparams(("parallel", "arbitrary"), 32),
    )(q, k, v, beta, g)


def _gdn_bwd(q, k, v, beta, g, ss, ts, do, hb):
    t_rows, d = q.shape
    nc, ng, w = t_rows // CH, d // (HD * hb), HD * hb
    per_f = ss.shape[2] // hb
    sls = [slice(h * HD, (h + 1) * HD) for h in range(hb)]

    def body(q_ref, k_ref, v_ref, b_ref, g_ref, ss_ref, ts_ref, do_ref, dq_ref, dk_ref, dv_ref, db_ref, dg_ref, ds_scr):
        cr = pl.program_id(1)

        @pl.when(cr == 0)
        def _():
            ds_scr[...] = jnp.zeros_like(ds_scr)

        first = cr == nc - 1
        rowi = lax.broadcasted_iota(jnp.int32, (CH, 1), 0)
        lani = lax.broadcasted_iota(jnp.int32, (1, CH), 1)
        keep_c = jnp.logical_or(jnp.logical_not(first), rowi >= PADR)
        keep_r = jnp.logical_or(jnp.logical_not(first), lani >= PADR)
        hs = range(hb)
        qs, ks, vs, dos = ([ref[:, sl] for sl in sls] for ref in (q_ref, k_ref, v_ref, do_ref))
        br = [b_ref[0, 0, pl.ds(h, 1), :] for h in hs]
        gr = [g_ref[0, 0, pl.ds(h, 1), :] for h in hs]
        tm = [ts_ref[0, 0, h] for h in hs]
        _, vjp_rest = jax.vjp(_gdn_rest, [ss_ref[0, 0, h] for h in hs], qs, ks, vs, br, gr, tm)
        ds0, dq, dk, dv, db, dg, dt = vjp_rest(([ds_scr[h] for h in hs], dos))
        dtt = [_dgh(dt[h], tm[h], 1, 1) for h in hs]
        da = [-_dgh(tm[h], dtt[h], 0, 0) for h in hs]
        _, vjp_a = jax.vjp(_gdn_a, ks, br, gr)
        dk2, db2, dg2 = vjp_a(da)
        for h in hs:
            ds_scr[h] = ds0[h]
            dq_ref[:, sls[h]] = jnp.where(keep_c, dq[h], 0.0)
            dk_ref[:, sls[h]] = jnp.where(keep_c, dk[h] + dk2[h], 0.0)
            dv_ref[:, sls[h]] = jnp.where(keep_c, dv[h], 0.0)
            db_ref[0, 0, pl.ds(h, 1), :] = jnp.where(keep_r, db[h] + db2[h], 0.0)
            dg_ref[0, 0, pl.ds(h, 1), :] = jnp.where(keep_r, dg[h] + dg2[h], 0.0)

    blk = pl.BlockSpec((CH, w), lambda n, c: (nc - 1 - c, n))
    row = pl.BlockSpec((1, 1, hb, CH), lambda n, c: (n, nc - 1 - c, 0, 0))
    return pl.pallas_call(
        body, name="gdn_bwd", grid=(ng, nc),
        in_specs=[blk, blk, blk, row, row,
                  pl.BlockSpec((1, 1, hb, HD, HD), lambda n, c: (n // per_f, nc - 1 - c, n % per_f, 0, 0)),
                  pl.BlockSpec((1, 1, hb, CH, CH), lambda n, c: (n // per_f, nc - 1 - c, n % per_f, 0, 0)), blk],
        out_specs=[blk, blk, blk, row, row],
        out_shape=[jax.ShapeDtypeStruct((t_rows, d), F32)] * 3 + [jax.ShapeDtypeStruct((ng, nc, hb, CH), F32)] * 2,
        scratch_shapes=[pltpu.VMEM((hb, HD, HD), F32)],
        compiler_params=_cparams(("parallel", "arbitrary"), 32),
    )(q, k, v, beta, g, ss, ts, do)


def _ssd_group(s, xs, bm, cm, dt_r, a_r):
    prs = range(len(s))
    ngrp = bm.shape[1] // NST
    grp = [p // (len(s) // ngrp) for p in prs]
    bms = [bm[:, g * NST:(g + 1) * NST] for g in range(ngrp)]
    cms = [cm[:, g * NST:(g + 1) * NST] for g in range(ngrp)]
    first = lax.broadcasted_iota(jnp.int32, (1, 2 * M2P), 1) < M2P

    def pick(vals, p):
        return jnp.where(first, vals[2 * p], vals[2 * p + 1])

    cs = [_cumsum_rc(a) for a in a_r]
    lm = [_decay(*csi) for csi in cs]
    ecs = [jnp.exp(csi[1]) for csi in cs]
    alast = [jnp.sum(a, axis=1, keepdims=True) for a in a_r]
    ealast = [jnp.exp(al) for al in alast]
    wt = [jnp.exp(al - csi[1]) for al, csi in zip(alast, cs)]
    dtc = [_col(t) for t in dt_r]
    xdt = [xs[:, p * LANE:(p + 1) * LANE] * pick(dtc, p) for p in prs]
    cb = [_dg(cms[g], bms[g], 1, 1) for g in range(ngrp)]
    y0 = [_dg(cb[grp[p]] * lm[2 * p], xdt[p], 1, 0) for p in prs]
    y1 = [_dg(cb[grp[p]] * lm[2 * p + 1], xdt[p], 1, 0) for p in prs]
    yo = [_dg(cms[grp[p]], s[p], 1, 0) for p in prs]
    y = [jnp.where(first, y0[p], y1[p]) + yo[p] * pick(ecs, p) for p in prs]
    s_new = [s[p] * pick(ealast, p) + _dg(bms[grp[p]], xdt[p] * pick(wt, p), 0, 0) for p in prs]
    return s_new, jnp.concatenate(y, axis=1)


def _ssd_specs(nc, d, rev, gps):
    assert M2G % gps == 0 and (d // LANE) % gps == 0
    hps = (d // M2P) // M2G * gps
    cc = (lambda c: nc - 1 - c) if rev else (lambda c: c)
    xs = pl.BlockSpec((CH, hps * M2P), lambda g, c: (cc(c), g))
    bm = pl.BlockSpec((CH, NST * gps), lambda g, c: (cc(c), (d // LANE) // gps + g))
    cm = pl.BlockSpec((CH, NST * gps), lambda g, c: (cc(c), (d // LANE + M2G) // gps + g))
    row = pl.BlockSpec((1, 1, hps, CH), lambda g, c: (g, cc(c), 0, 0))
    return xs, bm, cm, row, hps


def _ssd_fwd(xbc, dt, a, d, gps):
    t_rows = xbc.shape[0]
    nc = t_rows // CH
    xs, bm, cm, row, hpg = _ssd_specs(nc, d, False, gps)
    ppg = hpg // 2
    st = pl.BlockSpec((1, 1, ppg, NST, LANE), lambda g, c: (g, c, 0, 0, 0))

    def body(xs_ref, b_ref, c_ref, dt_ref, a_ref, y_ref, ss_ref, s_scr):
        c = pl.program_id(1)

        @pl.when(c == 0)
        def _():
            s_scr[...] = jnp.zeros_like(s_scr)

        s0 = [s_scr[p] for p in range(ppg)]
        for p in range(ppg):
            ss_ref[0, 0, p] = s0[p]
        dt_r = [dt_ref[0, 0, pl.ds(h, 1), :] for h in range(hpg)]
        a_r = [a_ref[0, 0, pl.ds(h, 1), :] for h in range(hpg)]
        s1, y = _ssd_group(s0, xs_ref[...], b_ref[...], c_ref[...], dt_r, a_r)
        y_ref[...] = y
        for p in range(ppg):
            s_scr[p] = s1[p]

    return pl.pallas_call(
        body, name="ssd_fwd", grid=(M2G // gps, nc), in_specs=[xs, bm, cm, row, row], out_specs=[xs, st],
        out_shape=[jax.ShapeDtypeStruct((t_rows, d), F32), jax.ShapeDtypeStruct((M2G // gps, nc, ppg, NST, LANE), F32)],
        scratch_shapes=[pltpu.VMEM((ppg, NST, LANE), F32)],
        compiler_params=_cparams(("parallel", "arbitrary"), 32),
    )(xbc, xbc, xbc, dt, a)


def _ssd_bwd(xbc, dt, a, ss, dy, d, gps):
    t_rows = xbc.shape[0]
    nc = t_rows // CH
    xs, bm, cm, row, hpg = _ssd_specs(nc, d, True, gps)
    ppg = hpg // 2
    per_f = ss.shape[2] // ppg
    st = pl.BlockSpec((1, 1, ppg, NST, LANE), lambda g, c: (g // per_f, nc - 1 - c, g % per_f, 0, 0))

    def body(xs_ref, b_ref, c_ref, dt_ref, a_ref, ss_ref, dy_ref, dxs_ref, db_ref, dc_ref, ddt_ref, da_ref, ds_scr):
        cr = pl.program_id(1)

        @pl.when(cr == 0)
        def _():
            ds_scr[...] = jnp.zeros_like(ds_scr)

        first = cr == nc - 1
        keep_c = jnp.logical_or(jnp.logical_not(first), lax.broadcasted_iota(jnp.int32, (CH, 1), 0) >= PADR)
        keep_r = jnp.logical_or(jnp.logical_not(first), lax.broadcasted_iota(jnp.int32, (1, CH), 1) >= PADR)
        dt_r = [dt_ref[0, 0, pl.ds(h, 1), :] for h in range(hpg)]
        a_r = [a_ref[0, 0, pl.ds(h, 1), :] for h in range(hpg)]
        s0 = [ss_ref[0, 0, p] for p in range(ppg)]
        _, vjp = jax.vjp(_ssd_group, s0, xs_ref[...], b_ref[...], c_ref[...], dt_r, a_r)
        ds0, dxs, db, dc, ddt, da = vjp(([ds_scr[p] for p in range(ppg)], dy_ref[...]))
        for p in range(ppg):
            ds_scr[p] = ds0[p]
        dxs_ref[...] = jnp.where(keep_c, dxs, 0.0)
        db_ref[...] = jnp.where(keep_c, db, 0.0)
        dc_ref[...] = jnp.where(keep_c, dc, 0.0)
        for h in range(hpg):
            ddt_ref[0, 0, pl.ds(h, 1), :] = jnp.where(keep_r, ddt[h], 0.0)
            da_ref[0, 0, pl.ds(h, 1), :] = jnp.where(keep_r, da[h], 0.0)

    grp = pl.BlockSpec((CH, NST * gps), lambda g, c: (nc - 1 - c, g))
    return pl.pallas_call(
        body, name="ssd_bwd", grid=(M2G // gps, nc), in_specs=[xs, bm, cm, row, row, st, xs],
        out_specs=[xs, grp, grp, row, row],
        out_shape=[jax.ShapeDtypeStruct((t_rows, d), F32)] + [jax.ShapeDtypeStruct((t_rows, M2G * NST), F32)] * 2
        + [jax.ShapeDtypeStruct((M2G // gps, nc, hpg, CH), F32)] * 2,
        scratch_shapes=[pltpu.VMEM((ppg, NST, LANE), F32)],
        compiler_params=_cparams(("parallel", "arbitrary"), 32),
    )(xbc, xbc, xbc, dt, a, ss, dy)


def _exchange(name, gathers, scatters, after):
    arrays = list(gathers) + list(scatters)
    n_g, n = len(gathers), len(arrays)

    def body(*refs):
        ins, outs = refs[:n], refs[n + 1:2 * n + 1]
        send_sems, recv_sems, local_sems = refs[2 * n + 1:]
        x, y, c = lax.axis_index("x"), lax.axis_index("y"), lax.axis_index("c")
        me = 4 * x + 2 * y + c

        def src(a, slot):
            return ins[a] if a < n_g else ins[a].at[slot]

        local = [pltpu.make_async_copy(src(a, me), outs[a].at[me], local_sems.at[a]) for a in range(n)]
        for cp in local:
            cp.start()
        copies = []
        for rel in range(1, NDEV):
            px, py, pc = x ^ (rel >> 2), y ^ ((rel >> 1) & 1), c ^ (rel & 1)
            peer = 4 * px + 2 * py + pc
            for a in range(n):
                copies.append(pltpu.make_async_remote_copy(
                    src_ref=src(a, peer), dst_ref=outs[a].at[me], send_sem=send_sems.at[a, rel - 1],
                    recv_sem=recv_sems.at[a, rel - 1], device_id=(px, py, pc), device_id_type=pl.DeviceIdType.MESH))
        for cp in copies:
            cp.start()
        for cp in copies:
            cp.wait_recv()
        for cp in copies:
            cp.wait_send()
        for cp in local:
            cp.wait()

    any_spec = pl.BlockSpec(memory_space=pl.ANY)
    out_shape = [jax.ShapeDtypeStruct((NDEV,) + a.shape, a.dtype) for a in gathers]
    out_shape += [jax.ShapeDtypeStruct(a.shape, a.dtype) for a in scatters]
    return pl.pallas_call(
        body, name=name, in_specs=[any_spec] * (n + 1), out_specs=[any_spec] * n, out_shape=out_shape,
        scratch_shapes=[pltpu.SemaphoreType.DMA((n, NDEV - 1)), pltpu.SemaphoreType.DMA((n, NDEV - 1)),
                        pltpu.SemaphoreType.DMA((n,))],
        compiler_params=pltpu.CompilerParams(has_side_effects=True),
    )(*arrays, after)


def _gather_two_level(name, arrays):
    n = len(arrays)

    def body(*refs):
        ins, outs = refs[:n], refs[n:2 * n]
        send_sems, recv_sems, local_sems = refs[2 * n:]
        x, y, c = lax.axis_index("x"), lax.axis_index("y"), lax.axis_index("c")
        me, sibling = (x, y, c), (x, y, 1 - c)
        chips = [(1 - x, y), (x, 1 - y), (1 - x, 1 - y)]

        def copy(a, k, block, to, src=None):
            dst = outs[a].at[4 * block[0] + 2 * block[1] + block[2]]
            return pltpu.make_async_remote_copy(
                src_ref=dst if src is None else src, dst_ref=dst, send_sem=send_sems.at[a, k], recv_sem=recv_sems.at[a, k],
                device_id=to, device_id_type=pl.DeviceIdType.MESH)

        mine = [pltpu.make_async_copy(ins[a], outs[a].at[4 * x + 2 * y + c], local_sems.at[a]) for a in range(n)]
        for cp in mine:
            cp.start()
        first = []
        for a in range(n):
            first.append(copy(a, 0, me, sibling, src=ins[a]))
            first += [copy(a, 1 + j, me, (*chip, c), src=ins[a]) for j, chip in enumerate(chips)]
        for cp in first:
            cp.start()
        passed = [[copy(a, 4 + j, (*chip, c), sibling) for j, chip in enumerate(chips)] for a in range(n)]
        for j, chip in enumerate(chips):
            for a in range(n):
                copy(a, 1 + j, (*chip, c), me).wait_recv()
                passed[a][j].start()
        for a in range(n):
            copy(a, 0, sibling, me).wait_recv()
            for j, chip in enumerate(chips):
                copy(a, 4 + j, (*chip, 1 - c), me).wait_recv()
        for cp in first + [cp for row in passed for cp in row]:
            cp.wait_send()
        for cp in mine:
            cp.wait()

    any_spec = pl.BlockSpec(memory_space=pl.ANY)
    return pl.pallas_call(
        body, name=name, in_specs=[any_spec] * n, out_specs=[any_spec] * n,
        out_shape=[jax.ShapeDtypeStruct((NDEV,) + a.shape, a.dtype) for a in arrays],
        scratch_shapes=[pltpu.SemaphoreType.DMA((n, NDEV - 1)), pltpu.SemaphoreType.DMA((n, NDEV - 1)),
                        pltpu.SemaphoreType.DMA((n,))],
        compiler_params=pltpu.CompilerParams(has_side_effects=True),
    )(*arrays)


_HBM = pl.BlockSpec(memory_space=pltpu.HBM)
_SEM = pl.BlockSpec(memory_space=pltpu.SEMAPHORE)
_EFFECT = pltpu.SideEffectType.DATAFLOW_SIDE_EFFECTING


def _split_copies(srcs, lands, send_sems, recv_sems, n_g):
    x, y, c = lax.axis_index("x"), lax.axis_index("y"), lax.axis_index("c")
    me = 4 * x + 2 * y + c
    copies = []
    for rel in range(1, NDEV):
        px, py, pc = x ^ (rel >> 2), y ^ ((rel >> 1) & 1), c ^ (rel & 1)
        peer = 4 * px + 2 * py + pc
        for a in range(len(srcs)):
            copies.append(pltpu.make_async_remote_copy(
                src_ref=srcs[a] if a < n_g else srcs[a].at[peer], dst_ref=lands[a].at[me],
                send_sem=send_sems.at[a * (NDEV - 1) + rel - 1], recv_sem=recv_sems.at[a * (NDEV - 1) + rel - 1],
                device_id=(px, py, pc), device_id_type=pl.DeviceIdType.MESH))
    return copies


def _exchange_start(name, gathers, scatters, after):
    arrays = list(gathers) + list(scatters)
    n_g, n = len(gathers), len(arrays)
    lands = [lax.empty((NDEV,) + a.shape, a.dtype) for a in gathers] + [lax.empty(a.shape, a.dtype) for a in scatters]

    def body(*refs):
        send_sems, recv_sems = refs[2 * n + 1], refs[2 * n + 2]
        for cp in _split_copies(refs[:n], refs[n:2 * n], send_sems, recv_sems, n_g):
            cp.start()
        refs[-1][...] = jnp.zeros_like(refs[-1])

    sems = pltpu.SemaphoreType.DMA((n * (NDEV - 1),))
    out = pl.pallas_call(
        body, name=name, in_specs=[_HBM] * (2 * n) + [pl.BlockSpec(memory_space=pl.ANY)],
        out_specs=(_SEM, _SEM, *[_HBM] * (2 * n), pl.BlockSpec(memory_space=pltpu.VMEM)),
        out_shape=(sems, sems, *[pltpu.HBM(a.shape, a.dtype) for a in arrays + lands], jax.ShapeDtypeStruct((8, LANE), F32)),
        input_output_aliases={i: 2 + i for i in range(2 * n)},
        compiler_params=pltpu.CompilerParams(has_side_effects=_EFFECT),
    )(*[pltpu.with_memory_space_constraint(a, pltpu.HBM) for a in arrays + lands], after)
    return out[0], out[1], list(out[2:2 + 2 * n]), out[-1], n_g


def _exchange_wait(name, started, after):
    send_sems, recv_sems, thru, _, n_g = started
    n = len(thru) // 2

    def body(*refs):
        for cp in _split_copies(refs[:n], refs[n:2 * n], refs[2 * n], refs[2 * n + 1], n_g):
            cp.wait_send()
            cp.wait_recv()

    out = pl.pallas_call(
        body, name=name, in_specs=[_HBM] * (2 * n) + [_SEM, _SEM, pl.BlockSpec(memory_space=pl.ANY)],
        out_specs=[_HBM] * (2 * n), out_shape=[pltpu.HBM(a.shape, a.dtype) for a in thru],
        input_output_aliases={i: i for i in range(2 * n)},
        compiler_params=pltpu.CompilerParams(has_side_effects=_EFFECT),
    )(*thru, send_sems, recv_sems, after)
    me = 4 * lax.axis_index("x") + 2 * lax.axis_index("y") + lax.axis_index("c")
    full = []
    for a in range(n):
        own = out[a][None] if a < n_g else lax.dynamic_index_in_dim(out[a], me, 0, keepdims=True)
        full.append(lax.dynamic_update_index_in_dim(out[n + a], own, me, 0))
    return full


def _adamw(name, staged, w, m, v):
    r, c = w.shape
    tr = _pick(r, 256, 8)

    def body(st_ref, w_ref, m_ref, v_ref, g_ref, d_ref, nm_ref, nv_ref):
        g = st_ref[0].astype(F32)
        for k in range(1, NDEV):
            g = g + st_ref[k].astype(F32)
        m_new = ADAM_B1 * m_ref[...] + (1.0 - ADAM_B1) * g
        v_new = ADAM_B2 * v_ref[...] + (1.0 - ADAM_B2) * jnp.square(g)
        m_hat = m_new / (1.0 - ADAM_B1 ** ADAM_STEP)
        v_hat = v_new / (1.0 - ADAM_B2 ** ADAM_STEP)
        g_ref[...] = g
        d_ref[...] = -ADAM_LR * (m_hat / (jnp.sqrt(v_hat) + ADAM_EPS) + ADAM_WD * w_ref[...])
        nm_ref[...] = m_new
        nv_ref[...] = v_new

    blk = pl.BlockSpec((tr, c), lambda i: (i, 0))
    return pl.pallas_call(
        body, name=name, grid=(r // tr,), in_specs=[pl.BlockSpec((NDEV, tr, c), lambda i: (0, i, 0)), blk, blk, blk],
        out_specs=[blk] * 4, out_shape=[jax.ShapeDtypeStruct((r, c), F32)] * 4,
        compiler_params=_cparams(("parallel",), 48),
    )(staged, w, m, v)


def _pack(parts):
    flat = jnp.concatenate([p.reshape(-1).astype(F32) for p in parts])
    pad = (-flat.shape[0]) % (8 * LANE)
    return jnp.pad(flat, (0, pad)).reshape(-1, LANE)


def _unpack(slab, shapes):
    flat, out, off = slab.reshape(-1), [], 0
    for s in shapes:
        n = 1
        for dim in s:
            n *= dim
        out.append(flat[off:off + n].reshape(s))
        off += n
    return out


def _to_shards(full, axis):
    shp = full.shape
    t = full.reshape(shp[:axis] + (NDEV, shp[axis] // NDEV) + shp[axis + 1:])
    return jnp.moveaxis(t, axis, 0)


def _from_shards(g, axis):
    t = jnp.moveaxis(g, 0, axis)
    shp = t.shape
    return t.reshape(shp[:axis] + (shp[axis] * shp[axis + 1],) + shp[axis + 2:])


def kernel(x, meta_tokens, norm_mix_w, w_in, dn_conv_w, dn_a_log, dn_dt_bias, dn_norm_w, m2_conv_w, m2_conv_b, m2_a_log, m2_dt_bias, m2_d, m2_norm_w, w_out, norm_ffn_w, ffn_up, ffn_conv_w, ffn_down, norm_final_w, loss_target, m_meta_tokens, m_norm_mix_w, m_w_in, m_dn_conv_w, m_dn_a_log, m_dn_dt_bias, m_dn_norm_w, m_m2_conv_w, m_m2_conv_b, m_m2_a_log, m_m2_dt_bias, m_m2_d, m_m2_norm_w, m_w_out, m_norm_ffn_w, m_ffn_up, m_ffn_conv_w, m_ffn_down, m_norm_final_w, v_meta_tokens, v_norm_mix_w, v_w_in, v_dn_conv_w, v_dn_a_log, v_dn_dt_bias, v_dn_norm_w, v_m2_conv_w, v_m2_conv_b, v_m2_a_log, v_m2_dt_bias, v_m2_d, v_m2_norm_w, v_w_out, v_norm_ffn_w, v_ffn_up, v_ffn_conv_w, v_ffn_down, v_norm_final_w):
    seq, d = x.shape[1], x.shape[2]
    t_rows = seq + CH
    nc = t_rows // CH
    dnh, m2h = d // HD, d // M2P
    dff = ffn_down.shape[1] * NDEV
    xbc_w = d + 2 * M2G * NST
    assert seq % CH == 0 and d % (2 * M2P * M2G) == 0 and 2 * dnh + m2h <= LANE
    hb_f = max(h for h in (16, 8, 4, 2, 1) if dnh % h == 0)
    hb_b = max(h for h in (8, 4, 2, 1) if dnh % h == 0)
    tm_rw = _pick(t_rows, 208, 16)
    conv_chunk = _pick(t_rows, min(CONV_CHUNK, t_rows // 3), 16)

    small_sharded = [meta_tokens, dn_conv_w[0], m2_conv_w[0], ffn_conv_w[0]]
    small_shapes = [p.shape for p in small_sharded]
    g_win, g_small = _gather_two_level("gather_w_in", [w_in[0].astype(WIRE), _pack(small_sharded)])
    rest = _exchange_start("gather_rest_start", [w_out[0].astype(WIRE), ffn_up[0].astype(WIRE), ffn_down[0].astype(WIRE)], [],
                           g_small)
    win = _from_shards(g_win, 1)
    small_full = [_unpack(g_small[k], small_shapes) for k in range(NDEV)]
    meta_f, dnconv_f, m2conv_f, ffnconv_f = [jnp.concatenate([small_full[k][i] for k in range(NDEV)], axis=-1) for i in range(4)]

    o_z, o_b, o_a = 3 * d, 4 * d, 4 * d + dnh
    o_m2z = 4 * d + 2 * dnh
    o_xbc, o_dt = o_m2z + d, o_m2z + d + xbc_w
    w_all = jnp.concatenate([win[:, :o_b], win[:, o_m2z:o_dt], win[:, o_b:o_m2z], win[:, o_dt:],
                             jnp.zeros((d, LANE - 2 * dnh - m2h), WIRE)], axis=1)
    seg_cols = {"q": (0, d), "k": (d, d), "v": (2 * d, d), "z": (3 * d, d), "m2z": (4 * d, d), "xbc": (5 * d, xbc_w),
                "sm": (5 * d + xbc_w, LANE)}

    h0 = jnp.concatenate([jnp.zeros((PADR, d), F32), meta_f, x[0]], axis=0)
    valid = lambda rows: rows >= PADR

    def norm_fwd(name, h, w):
        return _rw(name, lambda rows, j, hv, wv: (_rms(hv, wv),), [("r", h, d, _c0), ("p", w, d, _c0)],
                   [("r", d, d, _c0, MXU)], t_rows, tm_rw)[0]

    hn1 = norm_fwd("norm_mix", h0, norm_mix_w)
    proj = {s: _mm("proj_" + s, hn1, w_all, b_cols=seg_cols[s], dep=rest[3]) for s in seg_cols}

    def dn_post(sec, cv):
        s = _silu(cv)
        if sec < 2:
            s = s * lax.rsqrt(jnp.sum(s * s, axis=-1, keepdims=True) + EPS)
        if sec == 0:
            s = s * (HD ** -0.5)
        return s

    def dn_prep(sec, name):
        def fn(rows, own, last, wins, pars):
            return [jnp.where(valid(rows), dn_post(sec, _conv(wins[0], pars[0])), 0.0)], []
        wc = dnconv_f[:, sec * d:(sec + 1) * d]
        return _cv("dn_prep_" + name, fn, [(proj[name], _cj)], [(wc, _cj)], [(d, _cj, F32)], [], t_rows, dnh)[0]

    q_act, k_act, v_act = dn_prep(0, "q"), dn_prep(1, "k"), dn_prep(2, "v")

    lane = lambda: lax.broadcasted_iota(jnp.int32, (1, LANE), 1)

    def lanes_of(vec, off):
        return jnp.pad(vec.astype(F32), ((0, 0), (off, LANE - off - vec.shape[1])))

    gate_params = [lanes_of(dn_a_log, dnh), lanes_of(dn_dt_bias, dnh), lanes_of(m2_a_log, 2 * dnh), lanes_of(m2_dt_bias, 2 * dnh)]

    def gates(rows, sm, p_alog, p_dtb, p_malog, p_mdtb):
        ln = lane()
        is_b, is_g = ln < dnh, jnp.logical_and(ln >= dnh, ln < 2 * dnh)
        is_d = jnp.logical_and(ln >= 2 * dnh, ln < 2 * dnh + m2h)
        beta = jax.nn.sigmoid(sm)
        gdec = -jnp.exp(p_alog) * _softplus(sm + p_dtb)
        dt = _softplus(sm + p_mdtb)
        am = dt * (-jnp.exp(p_malog))
        ok = valid(rows)
        g1 = jnp.where(ok, jnp.where(is_b, beta, jnp.where(is_g, gdec, jnp.where(is_d, dt, 0.0))), 0.0)
        g2 = jnp.where(jnp.logical_and(ok, is_d), am, 0.0)
        return g1, g2

    gate_ins = [("r", proj["sm"], LANE, _c0)] + [("p", p, LANE, _c0) for p in gate_params]
    g1, g2 = _rw("gates", lambda rows, j, *a: gates(rows, *a), gate_ins,
                 [("r", LANE, LANE, _c0, F32), ("r", LANE, LANE, _c0, F32)], t_rows, tm_rw)

    def head_rows(cols, per):
        n = cols.shape[1]
        return cols.reshape(nc, CH, n // per, per).transpose(2, 0, 3, 1)

    def head_cols(rows_):
        ngrp, _, per, _ = rows_.shape
        return rows_.transpose(1, 3, 0, 2).reshape(t_rows, ngrp * per)

    beta_r, gdec_r = head_rows(g1[:, :dnh], hb_f), head_rows(g1[:, dnh:2 * dnh], hb_f)
    hpg = m2h // M2G
    m2_rows = lambda gps: (head_rows(g1[:, 2 * dnh:2 * dnh + m2h], hpg * gps), head_rows(g2[:, 2 * dnh:2 * dnh + m2h], hpg * gps))

    o_dn, dn_states, dn_tinv = _gdn_fwd(q_act, k_act, v_act, beta_r, gdec_r, hb_f)

    def dn_out(o, z, w):
        outs = []
        for h in range(dnh):
            sl = slice(h * HD, (h + 1) * HD)
            outs.append(_rms(o[:, sl], w) * _silu(z[:, sl]))
        return jnp.concatenate(outs, axis=1)

    mixed_dn = _rw("dn_out", lambda rows, j, o, z, w: (dn_out(o, z, w),),
                   [("r", o_dn, d, _c0), ("r", proj["z"], d, _c0), ("p", dn_norm_w, HD, _c0)], [("r", d, d, _c0, MXU)],
                   t_rows, tm_rw)[0]

    def m2_prep(rows, own, last, wins, pars):
        return [jnp.where(valid(rows), _silu(_conv(wins[0], pars[0]) + pars[1][0]), 0.0)], []

    xbc_act = _cv("m2_prep", m2_prep, [(proj["xbc"], _cj)], [(m2conv_f, _cj), (m2_conv_b, _cj)], [(xbc_w, _cj, F32)], [],
                  t_rows, xbc_w // LANE)[0]
    y_ssd, m2_states = _ssd_fwd(xbc_act, *m2_rows(SSD_GPS_F), d, SSD_GPS_F)

    d_lanes = jnp.repeat(m2_d.astype(F32), M2P, axis=1)
    gw = d // M2G

    def m2_out(ys, xs, z, dl, nw):
        yv = (ys + dl * xs) * _silu(z)
        outs = []
        for gi in range(M2G):
            sl = slice(gi * gw, (gi + 1) * gw)
            outs.append(_rms(yv[:, sl], nw[:, sl]))
        return jnp.concatenate(outs, axis=1)

    m2_out_ins = [("r", y_ssd, d, _c0), ("r", xbc_act, d, _c0), ("r", proj["m2z"], d, _c0), ("p", d_lanes, d, _c0),
                  ("p", m2_norm_w, d, _c0)]
    mixed_m2 = _rw("m2_out", lambda rows, j, *a: (m2_out(*a),), m2_out_ins, [("r", d, d, _c0, MXU)], t_rows, tm_rw)[0]

    mixed = jnp.concatenate([mixed_dn, mixed_m2], axis=1)
    g_wout, g_wup, g_wdown = _exchange_wait("gather_rest_wait", rest, mixed)
    wout = _from_shards(g_wout, 0)
    wup = _from_shards(g_wup, 1)
    wdown = _from_shards(g_wdown, 0)
    up_g, up_v = (0, dff), (dff, dff)
    h1 = _mm("out_proj", mixed, wout, add=h0)
    hn2 = norm_fwd("norm_ffn", h1, norm_ffn_w)
    u_g, u_v = _mm("ffn_up_g", hn2, wup, b_cols=up_g), _mm("ffn_up_v", hn2, wup, b_cols=up_v)
    fc_g, fc_v = ffnconv_f[:, :dff], ffnconv_f[:, dff:]

    def ffn_act(rows, own, last, wins, pars):
        return [jnp.where(valid(rows), _silu(_conv(wins[0], pars[0])) * _conv(wins[1], pars[1]), 0.0)], []

    act = _cv("ffn_act", ffn_act, [(u_g, _cj), (u_v, _cj)], [(fc_g, _cj), (fc_v, _cj)], [(dff, _cj, MXU)], [],
              t_rows, dff // LANE)[0]
    h2 = _mm("ffn_down", act, wdown, add=h1, tk=1408)

    def loss_fn(hv, wf, tgt, rows):
        err = jnp.where(rows >= CH, _rms(hv, wf) - tgt, 0.0)
        return 0.5 * jnp.sum(jnp.mean(err * err, axis=-1, keepdims=True), axis=0, keepdims=True)

    def final(rows, j, hv, wf, tgt):
        loss, vjp = jax.vjp(lambda a, b: loss_fn(a, b, tgt, rows), hv, wf)
        dh, dw = vjp(jnp.ones((1, 1), F32))
        return dh, dh, dw, jnp.broadcast_to(loss, (1, LANE))

    wf2 = norm_final_w.reshape(1, d)
    dh2, dh2_m, d_wf, loss_part = _rw(
        "loss_head", final, [("r", h2, d, _c0), ("p", wf2, d, _c0), ("r", loss_target[0], d, _c0, lambda i: jnp.maximum(i - 1, 0))],
        [("r", d, d, _c0, F32), ("r", d, d, _c0, MXU), ("p", 1, d, d, _c0), ("p", 1, LANE, LANE, _c0)], t_rows, CH)
    loss = lax.psum(loss_part[0, 0], MESH_AXES)

    d_act = _mm("d_act", dh2_m, wdown, tb=True)
    gw_down = _mm("gw_down", act, dh2_m, ta=True, tm=1408, tn=1024, tk=2080, out_dtype=WIRE)
    x_down = _exchange_start("grad_down_start", [], [_to_shards(gw_down, 0).astype(WIRE)], gw_down)

    def t_end(last):
        return t_rows if last else None

    def ffn_act_bwd(rows, own, last, wins, pars):
        (ug, uv, da), (wg, wv) = wins, pars
        cg, cv = _conv(ug, wg), _conv(uv, wv)
        _, vjp = jax.vjp(lambda a, b: _silu(a) * b, cg, cv)
        dcg, dcv = vjp(jnp.where(valid(rows), da, 0.0))
        return ([_conv_t(dcg, wg, rows, t_end(last)), _conv_t(dcv, wv, rows, t_end(last))],
                [_conv_w(jnp.where(own, dcg, 0.0), ug, len(wg)), _conv_w(jnp.where(own, dcv, 0.0), uv, len(wv))])

    kf = fc_g.shape[0]
    du_g, du_v, g_fc_g, g_fc_v = _cv(
        "ffn_act_bwd", ffn_act_bwd, [(u_g, _cj), (u_v, _cj), (d_act, _cj)], [(fc_g, _cj), (fc_v, _cj)],
        [(dff, _cj, MXU), (dff, _cj, MXU)], [(kf, dff, _cj), (kf, dff, _cj)], t_rows, dff // LANE, chunk=conv_chunk)
    gw_up_g = _mm("gw_up_g", hn2, du_g, ta=True, tm=1024, tn=1408, tk=2080, out_dtype=WIRE, dep=x_down[3])
    gw_up_v = _mm("gw_up_v", hn2, du_v, ta=True, tm=1024, tn=1408, tk=2080, out_dtype=WIRE)
    gw_up_full = jnp.concatenate([gw_up_g, gw_up_v], axis=1)
    x_up = _exchange_start("grad_up_start", [], [_to_shards(gw_up_full, 1).astype(WIRE)], gw_up_full)
    d_hn2 = _mm("d_hn2_v", du_v, wup, b_cols=up_v, tb=True, tk=1408, dep=x_up[3],
                add=_mm("d_hn2_g", du_g, wup, b_cols=up_g, tb=True, tk=1408, tn=2048, dep=x_up[3]))

    def norm_bwd(name, h, w, dy, dres):
        def fn(rows, j, hv, wv, dyv, dr):
            _, vjp = jax.vjp(_rms, hv, wv)
            dh, dw = vjp(dyv)
            dh = dh + dr
            return dh, dh, dw
        return _rw(name, fn, [("r", h, d, _c0), ("p", w, d, _c0), ("r", dy, d, _c0), ("r", dres, d, _c0)],
                   [("r", d, d, _c0, F32), ("r", d, d, _c0, MXU), ("p", 1, d, d, _c0)], t_rows, tm_rw)

    dh1, dh1_m, g_norm_ffn = norm_bwd("norm_ffn_bwd", h1, norm_ffn_w, d_hn2, dh2)

    gw_out = _mm("gw_out", mixed, dh1_m, ta=True, tm=1024, tn=1024, tk=2080, out_dtype=WIRE)
    x_out = _exchange_start("grad_out_start", [], [_to_shards(gw_out, 0).astype(WIRE)], gw_out)
    d_mixed = _mm("d_mixed", dh1_m, wout, tb=True, dep=x_out[3])

    gw_seg = {}

    def gw_in(seg, dseg_arr):
        gw_seg[seg] = _mm("gw_in_" + seg, hn1, dseg_arr, ta=True, tm=1024, tn=1024, tk=2080, out_dtype=WIRE)

    def m2_out_bwd(rows, j, ys, xs, z, dl, nw, dy):
        _, vjp = jax.vjp(m2_out, ys, xs, z, dl, nw)
        return vjp(dy)

    dy_ssd, dxs_skip, d_m2z, g_d_lanes, g_m2_norm = _rw(
        "m2_out_bwd", m2_out_bwd, m2_out_ins + [("r", d_mixed, d, lambda j: 1)],
        [("r", d, d, _c0, F32), ("r", d, d, _c0, F32), ("r", d, d, _c0, MXU), ("p", 1, d, d, _c0), ("p", 1, d, d, _c0)],
        t_rows, _pick(t_rows, 208, 16))
    gw_in("m2z", d_m2z)

    def fold_heads(vec_ref, out_ref):
        r = lax.broadcasted_iota(jnp.int32, (d, LANE), 0)
        c = lax.broadcasted_iota(jnp.int32, (d, LANE), 1)
        out_ref[...] = _dgh(vec_ref[...], jnp.where(jnp.logical_and(r >= c * M2P, r < (c + 1) * M2P), 1.0, 0.0), 1, 0)

    g_m2_d = pl.pallas_call(fold_heads, name="fold_m2_d", out_shape=jax.ShapeDtypeStruct((1, LANE), F32))(g_d_lanes)

    dxs, db_ssd, dc_ssd, ddt_r, dam_r = _ssd_bwd(xbc_act, *m2_rows(SSD_GPS_B), m2_states, dy_ssd, d, SSD_GPS_B)


    def m2_prep_bwd(rows, own, last, wins, pars):
        (p, *ds), (w, b) = wins, pars
        _, vjp = jax.vjp(_silu, _conv(p, w) + b[0])
        dpre, = vjp(jnp.where(valid(rows), functools.reduce(lambda a_, b_: a_ + b_, ds), 0.0))
        dpre_own = jnp.where(own, dpre, 0.0)
        return [_conv_t(dpre, w, rows, t_end(last))], [_conv_w(dpre_own, p, len(w)), [jnp.sum(dpre_own, axis=0, keepdims=True)]]

    def m2_prep_bwd_call(name, off, width, d_arrs):
        at = lambda j, blk0=off // LANE: blk0 + j
        return _cv(name, m2_prep_bwd, [(proj["xbc"], at)] + [(a, _cj) for a in d_arrs], [(m2conv_f, at), (m2_conv_b, at)],
                   [(width, _cj, MXU)], [(m2conv_f.shape[0], width, _cj), (1, width, _cj)], t_rows, width // LANE, chunk=conv_chunk)

    dp_xs, gcw_xs, gcb_xs = m2_prep_bwd_call("m2_prep_bwd_x", 0, d, [dxs, dxs_skip])
    dp_b, gcw_b, gcb_b = m2_prep_bwd_call("m2_prep_bwd_b", d, M2G * NST, [db_ssd])
    dp_c, gcw_c, gcb_c = m2_prep_bwd_call("m2_prep_bwd_c", d + M2G * NST, M2G * NST, [dc_ssd])
    d_pxbc = jnp.concatenate([dp_xs, dp_b, dp_c], axis=1)
    gw_in("xbc", d_pxbc)
    g_m2_conv = jnp.concatenate([gcw_xs, gcw_b, gcw_c], axis=1)
    g_m2_conv_b = jnp.concatenate([gcb_xs, gcb_b, gcb_c], axis=1)

    def dn_out_bwd(rows, j, o, z, w, dy):
        _, vjp = jax.vjp(dn_out, o, z, w)
        return vjp(dy)

    d_o, d_z, g_dn_norm = _rw(
        "dn_out_bwd", dn_out_bwd,
        [("r", o_dn, d, _c0), ("r", proj["z"], d, _c0), ("p", dn_norm_w, HD, _c0), ("r", d_mixed, d, _c0)],
        [("r", d, d, _c0, F32), ("r", d, d, _c0, MXU), ("p", 1, HD, HD, _c0)], t_rows, _pick(t_rows, 208, 16))
    gw_in("z", d_z)

    dq, dk, dv, dbeta_r, dgdec_r = _gdn_bwd(q_act, k_act, v_act, head_rows(g1[:, :dnh], hb_b),
                                            head_rows(g1[:, dnh:2 * dnh], hb_b), dn_states, dn_tinv, d_o, hb_b)

    def dn_prep_bwd(sec, name, dact):
        def fn(rows, own, last, wins, pars):
            (p, da), (w,) = wins, pars
            _, vjp = jax.vjp(functools.partial(dn_post, sec), _conv(p, w))
            dcv, = vjp(jnp.where(valid(rows), da, 0.0))
            return [_conv_t(dcv, w, rows, t_end(last))], [_conv_w(jnp.where(own, dcv, 0.0), p, len(w))]
        wc = dnconv_f[:, sec * d:(sec + 1) * d]
        return _cv("dn_prep_bwd_" + name, fn, [(proj[name], _cj), (dact, _cj)], [(wc, _cj)], [(d, _cj, MXU)],
                   [(wc.shape[0], d, _cj)], t_rows, dnh, chunk=conv_chunk if sec == 2 else None)

    (dp_q, gcw_q), (dp_k, gcw_k), (dp_v, gcw_v) = dn_prep_bwd(0, "q", dq), dn_prep_bwd(1, "k", dk), dn_prep_bwd(2, "v", dv)
    gw_in("q", dp_q), gw_in("k", dp_k), gw_in("v", dp_v)
    g_dn_conv = jnp.concatenate([gcw_q, gcw_k, gcw_v], axis=1)

    zpad = jnp.zeros((t_rows, LANE - 2 * dnh - m2h), F32)
    dg1 = jnp.concatenate([head_cols(dbeta_r), head_cols(dgdec_r), head_cols(ddt_r), zpad], axis=1)
    dg2 = jnp.concatenate([jnp.zeros((t_rows, 2 * dnh), F32), head_cols(dam_r), zpad], axis=1)

    def gates_bwd(rows, j, sm, pa, pb, pc, pd, d1, d2):
        _, vjp = jax.vjp(lambda *a: gates(rows, *a), sm, pa, pb, pc, pd)
        return vjp((d1, d2))

    dp_sm, g_pa, g_pb, g_pc, g_pd = _rw(
        "gates_bwd", gates_bwd, gate_ins + [("r", dg1, LANE, _c0), ("r", dg2, LANE, _c0)],
        [("r", LANE, LANE, _c0, MXU)] + [("p", 1, LANE, LANE, _c0)] * 4, t_rows, tm_rw)

    dseg = {"q": dp_q, "k": dp_k, "v": dp_v, "z": d_z, "m2z": d_m2z, "xbc": d_pxbc, "sm": dp_sm}
    gw_in("sm", dp_sm)
    gsm = gw_seg["sm"]
    gw_in_full = jnp.concatenate([gw_seg["q"], gw_seg["k"], gw_seg["v"], gw_seg["z"], gsm[:, :2 * dnh], gw_seg["m2z"],
                                  gw_seg["xbc"], gsm[:, 2 * dnh:2 * dnh + m2h]], axis=1)
    x_in = _exchange_start("grad_in_start", [], [_to_shards(gw_in_full, 1).astype(WIRE)], gw_in_full)
    d_hn1 = None
    for s in dseg:
        d_hn1 = _mm("d_hn1_" + s, dseg[s], w_all, b_cols=seg_cols[s], tb=True, tk=2048, add=d_hn1, dep=x_in[3])
    dh0, _, g_norm_mix = norm_bwd("norm_mix_bwd", h0, norm_mix_w, d_hn1, dh1)

    g_ffn_conv = jnp.concatenate([g_fc_g, g_fc_v], axis=1)
    small_parts = [_to_shards(dh0[PADR:CH], 1), _to_shards(g_dn_conv, 1), _to_shards(g_m2_conv, 1), _to_shards(g_ffn_conv, 1)]
    small_scatter = jnp.stack([_pack([p[k] for p in small_parts]) for k in range(NDEV)])

    rep_names = ["norm_mix_w", "dn_a_log", "dn_dt_bias", "dn_norm_w", "m2_conv_b", "m2_a_log", "m2_dt_bias", "m2_d",
                 "m2_norm_w", "norm_ffn_w", "norm_final_w"]
    rep_grads = [g_norm_mix, g_pa[:, dnh:2 * dnh], g_pb[:, dnh:2 * dnh], g_dn_norm, g_m2_conv_b, g_pc[:, 2 * dnh:2 * dnh + m2h],
                 g_pd[:, 2 * dnh:2 * dnh + m2h], g_m2_d[:, :m2h], g_m2_norm, g_norm_ffn, d_wf.reshape(d)]

    weights = dict(meta_tokens=meta_tokens, norm_mix_w=norm_mix_w, w_in=w_in, dn_conv_w=dn_conv_w, dn_a_log=dn_a_log,
                   dn_dt_bias=dn_dt_bias, dn_norm_w=dn_norm_w, m2_conv_w=m2_conv_w, m2_conv_b=m2_conv_b, m2_a_log=m2_a_log,
                   m2_dt_bias=m2_dt_bias, m2_d=m2_d, m2_norm_w=m2_norm_w, w_out=w_out, norm_ffn_w=norm_ffn_w, ffn_up=ffn_up,
                   ffn_conv_w=ffn_conv_w, ffn_down=ffn_down, norm_final_w=norm_final_w)
    mom1 = dict(meta_tokens=m_meta_tokens, norm_mix_w=m_norm_mix_w, w_in=m_w_in, dn_conv_w=m_dn_conv_w, dn_a_log=m_dn_a_log,
                dn_dt_bias=m_dn_dt_bias, dn_norm_w=m_dn_norm_w, m2_conv_w=m_m2_conv_w, m2_conv_b=m_m2_conv_b,
                m2_a_log=m_m2_a_log, m2_dt_bias=m_m2_dt_bias, m2_d=m_m2_d, m2_norm_w=m_m2_norm_w, w_out=m_w_out,
                norm_ffn_w=m_norm_ffn_w, ffn_up=m_ffn_up, ffn_conv_w=m_ffn_conv_w, ffn_down=m_ffn_down,
                norm_final_w=m_norm_final_w)
    mom2 = dict(meta_tokens=v_meta_tokens, norm_mix_w=v_norm_mix_w, w_in=v_w_in, dn_conv_w=v_dn_conv_w, dn_a_log=v_dn_a_log,
                dn_dt_bias=v_dn_dt_bias, dn_norm_w=v_dn_norm_w, m2_conv_w=v_m2_conv_w, m2_conv_b=v_m2_conv_b,
                m2_a_log=v_m2_a_log, m2_dt_bias=v_m2_dt_bias, m2_d=v_m2_d, m2_norm_w=v_m2_norm_w, w_out=v_w_out,
                norm_ffn_w=v_norm_ffn_w, ffn_up=v_ffn_up, ffn_conv_w=v_ffn_conv_w, ffn_down=v_ffn_down,
                norm_final_w=v_norm_final_w)
    res = {}

    def adam_big(name, started, after):
        staged, = _exchange_wait("grad_" + name + "_wait", started, after)
        outs = _adamw("adamw_" + name, staged, weights[name][0], mom1[name][0], mom2[name][0])
        res[name] = tuple(o[None] for o in outs)
        return outs[1]

    done = adam_big("ffn_down", x_down, dh0)
    done = adam_big("ffn_up", x_up, done)
    done = adam_big("w_out", x_out, done)
    st_rep, st_small = _exchange("exchange_small_grads", [_pack(rep_grads)], [small_scatter], done)
    adam_big("w_in", x_in, st_small)

    def adam_packed(label, staged, names):
        shapes = [weights[nm].shape for nm in names]
        outs = _adamw(label, staged, *[_pack([src[nm] for nm in names]) for src in (weights, mom1, mom2)])
        unpacked = [_unpack(o, shapes) for o in outs]
        for i, nm in enumerate(names):
            res[nm] = tuple(u[i] for u in unpacked)

    adam_packed("adamw_small_sharded", st_small, ["meta_tokens", "dn_conv_w", "m2_conv_w", "ffn_conv_w"])
    adam_packed("adamw_replicated", st_rep, rep_names)

    order = list(weights)
    grad_x = dh0[CH:][None]
    return (loss, grad_x, *[res[nm][0] for nm in order], *[res[nm][1] for nm in order], *[res[nm][2] for nm in order],
            *[res[nm][3] for nm in order])
```

```python
import functools
import math

import jax
import jax.numpy as jnp
from jax import lax
from jax.experimental import pallas as pl
from jax.experimental.pallas import tpu as pltpu

F32 = jnp.float32
MXU = jnp.bfloat16
WIRE = jnp.bfloat16
HI = lax.Precision.HIGH

NDEV = 8
CH = 64
NMETA = 16
PADR = CH - NMETA
EPS = 1e-6
HD = 128
M2P = 64
M2G = 4
SSD_GPS_F, SSD_GPS_B = 4, 1
NST = 128
LANE = 128

ADAM_LR, ADAM_B1, ADAM_B2, ADAM_EPS, ADAM_WD, ADAM_STEP = 0.001, 0.9, 0.999, 1e-08, 0.01, 10

MESH_AXES = ("x", "y", "c")


def _pick(n, target, mult=16):
    best = None
    for t in range(mult, min(n, target) + 1, mult):
        if n % t == 0:
            best = t
    return best if best is not None else n


def _dg(a, b, ca, cb):
    return lax.dot_general(a.astype(MXU), b.astype(MXU), (((ca,), (cb,)), ((), ())), preferred_element_type=F32)


def _dgh(a, b, ca, cb):
    return lax.dot_general(a, b, (((ca,), (cb,)), ((), ())), precision=HI, preferred_element_type=F32)


def _silu(x):
    return x * jax.nn.sigmoid(x)


def _softplus(x):
    return jnp.maximum(x, 0.0) + jnp.log1p(jnp.exp(-jnp.abs(x)))


def _rms(x, w):
    return x * lax.rsqrt(jnp.mean(x * x, axis=-1, keepdims=True) + EPS) * w


def _cparams(sem, vmem_mb):
    return pltpu.CompilerParams(dimension_semantics=sem, vmem_limit_bytes=vmem_mb << 20)


def _mm(name, a, b, *, ta=False, tb=False, add=None, out_dtype=F32, tm=1040, tn=1024, tk=2048, dep=None,
        b_rows=None, b_cols=None):
    m = a.shape[1] if ta else a.shape[0]
    (rs, rw), (cs, cw) = b_rows or (0, b.shape[0]), b_cols or (0, b.shape[1])
    (n, ns), (kdim, ks) = ((rw, rs), (cw, cs)) if tb else ((cw, cs), (rw, rs))
    assert kdim == (a.shape[0] if ta else a.shape[1])
    tm = _pick(m, tm, 128 if ta else 16)
    tn = _pick(math.gcd(n, ns), tn, 128)
    tk = _pick(math.gcd(kdim, ks), tk, 16 if (ta and not tb) else 128)
    nk = kdim // tk
    bj0, bk0 = ns // tn, ks // tk
    ca, cb = (0 if ta else 1), (1 if tb else 0)

    def body(*refs):
        a_ref, b_ref = refs[0], refs[1]
        add_ref = refs[2] if add is not None else None
        if nk == 1:
            r = _dg(a_ref[...], b_ref[...], ca, cb)
            if add_ref is not None:
                r = r + add_ref[...].astype(F32)
            refs[-1][...] = r.astype(refs[-1].dtype)
            return
        o_ref, acc = refs[-2], refs[-1]
        k = pl.program_id(2)

        @pl.when(k == 0)
        def _():
            acc[...] = jnp.zeros_like(acc)

        acc[...] += _dg(a_ref[...], b_ref[...], ca, cb)

        @pl.when(k == nk - 1)
        def _():
            r = acc[...]
            if add_ref is not None:
                r = r + add_ref[...].astype(F32)
            o_ref[...] = r.astype(o_ref.dtype)

    a_spec = pl.BlockSpec((tk, tm), lambda i, j, k: (k, i)) if ta else pl.BlockSpec((tm, tk), lambda i, j, k: (i, k))
    b_spec = (pl.BlockSpec((tn, tk), lambda i, j, k: (j + bj0, k + bk0)) if tb
              else pl.BlockSpec((tk, tn), lambda i, j, k: (k + bk0, j + bj0)))
    in_specs, ops = [a_spec, b_spec], [a, b]
    if add is not None:
        in_specs.append(pl.BlockSpec((tm, tn), lambda i, j, k: (i, j)))
        ops.append(add)
    if dep is not None:
        in_specs.append(pl.BlockSpec((8, LANE), lambda i, j, k: (0, 0)))
        ops.append(dep)
    return pl.pallas_call(
        body, name=name, grid=(m // tm, n // tn, nk), in_specs=in_specs,
        out_specs=pl.BlockSpec((tm, tn), lambda i, j, k: (i, j)),
        out_shape=jax.ShapeDtypeStruct((m, n), out_dtype),
        scratch_shapes=[pltpu.VMEM((tm, tn), F32)] if nk > 1 else [],
        compiler_params=_cparams(("parallel", "parallel", "arbitrary"), 48),
    )(*ops)


def _rw(name, fn, ins, outs, nrows, tm, ncol=1, vmem_mb=48):
    nrow = nrows // tm
    sub = tm
    in_specs, ops = [], []
    for spec in ins:
        kind, arr, bw, cj = spec[:4]
        ops.append(arr)
        if kind == "r":
            ri = spec[4] if len(spec) > 4 else (lambda i: i)
            in_specs.append(pl.BlockSpec((tm, bw), lambda j, i, cj=cj, ri=ri: (ri(i), cj(j))))
        else:
            in_specs.append(pl.BlockSpec((arr.shape[0], bw), lambda j, i, cj=cj: (0, cj(j))))
    out_shape, out_specs = [], []
    for o in outs:
        if o[0] == "r":
            _, width, bw, cj, dt = o
            out_shape.append(jax.ShapeDtypeStruct((nrows, width), dt))
            out_specs.append(pl.BlockSpec((tm, bw), lambda j, i, cj=cj: (i, cj(j))))
        else:
            _, rows, width, bw, cj = o
            out_shape.append(jax.ShapeDtypeStruct((rows, width), F32))
            out_specs.append(pl.BlockSpec((rows, bw), lambda j, i, cj=cj: (0, cj(j))))
    n_in = len(ins)

    def body(*refs):
        j, i = pl.program_id(0), pl.program_id(1)
        in_refs, out_refs = refs[:n_in], refs[n_in:]
        pars = [ref[...] if spec[0] == "p" else None for spec, ref in zip(ins, in_refs)]

        def one(r0, nr):
            rows = i * tm + r0 + lax.broadcasted_iota(jnp.int32, (nr, 1), 0)
            vals = [par if spec[0] == "p" else ref[pl.ds(r0, nr), :] for spec, ref, par in zip(ins, in_refs, pars)]
            parts = []
            for o, val, ref in zip(outs, fn(rows, j, *vals), out_refs):
                if o[0] == "r":
                    ref[pl.ds(r0, nr), :] = val.astype(ref.dtype)
                else:
                    parts.append(val)
            return parts

        if sub >= tm:
            parts = one(0, tm)
        else:
            zero = [jnp.zeros((1, o[3]), F32) for o in outs if o[0] == "p"]
            parts = lax.fori_loop(
                0, tm // sub, lambda s, acc: [a + b for a, b in zip(acc, one(pl.multiple_of(s * sub, sub), sub))], zero)
        for ref, val in zip([r for o, r in zip(outs, out_refs) if o[0] == "p"], parts):
            @pl.when(i == 0)
            def _(ref=ref):
                ref[...] = jnp.zeros_like(ref)

            ref[...] += val

    return pl.pallas_call(
        body, name=name, grid=(ncol, nrow), in_specs=in_specs, out_specs=out_specs, out_shape=out_shape,
        compiler_params=_cparams(("parallel", "arbitrary"), vmem_mb),
    )(*ops)


def _c0(j):
    return 0


def _cj(j):
    return j


def _shift(x, s):
    if s == 0:
        return x
    return pltpu.roll(x, s % x.shape[0], 0)


def _conv(x, w):
    k = len(w)
    return functools.reduce(lambda a, b: a + b, [w[j] * _shift(x, k - 1 - j) for j in range(k)])


def _conv_t(dy, w, rows, t_end):
    k = len(w)
    terms = []
    for j in range(k):
        s = k - 1 - j
        v = _shift(dy, -s)
        if t_end is not None and s > 0:
            v = jnp.where(rows + s < t_end, v, 0.0)
        terms.append(w[j] * v)
    return functools.reduce(lambda a, b: a + b, terms)


def _conv_w(dy, x, k):
    return [jnp.sum(dy * _shift(x, k - 1 - j), axis=0, keepdims=True) for j in range(k)]


CONV_CHUNK = 320
HALO = 8


def _cv(name, fn, row_ins, par_ins, row_outs, par_outs, nrows, ncol, chunk=None):
    whole = chunk is None
    chunk = nrows if whole else chunk
    n_chunks = nrows // chunk
    assert nrows % chunk == 0 and (whole or n_chunks >= 3)
    n_ri, n_pi, n_ro = len(row_ins), len(par_ins), len(row_outs)

    def body(*refs):
        rin, pin = refs[:n_ri], refs[n_ri:n_ri + n_pi]
        rout, pout = refs[n_ri + n_pi:n_ri + n_pi + n_ro], refs[n_ri + n_pi + n_ro:]
        pars = [[p[pl.ds(r, 1), :] for r in range(p.shape[0])] for p in pin]

        def run(r0, top, bot, last):
            wlen = top + chunk + bot
            w0 = r0 - top if isinstance(r0, int) else pl.multiple_of(r0 - top, HALO)
            local = lax.broadcasted_iota(jnp.int32, (wlen, 1), 0)
            own = jnp.logical_and(local >= top, local < top + chunk)
            outs, parts = fn(w0 + local, own, last, [ref[pl.ds(w0, wlen), :] for ref in rin], pars)
            for ref, val in zip(rout, outs):
                ref[pl.ds(r0, chunk), :] = val[top:top + chunk].astype(ref.dtype)
            return parts

        def add(acc, parts):
            return [[a + b for a, b in zip(ra, rb)] for ra, rb in zip(acc, parts)]

        if whole:
            acc = run(0, 0, 0, True)
        else:
            acc = run(0, 0, HALO, False)
            acc = lax.fori_loop(1, n_chunks - 1,
                                lambda i, a: add(a, run(pl.multiple_of(i * chunk, chunk), HALO, HALO, False)), acc)
            acc = add(acc, run(nrows - chunk, HALO, 0, True))
        for ref, prow in zip(pout, acc):
            for r, v in enumerate(prow):
                ref[pl.ds(r, 1), :] = v

    in_specs = [pl.BlockSpec((nrows, LANE), lambda j, cj=cj: (0, cj(j))) for _, cj in row_ins]
    in_specs += [pl.BlockSpec((a.shape[0], LANE), lambda j, cj=cj: (0, cj(j))) for a, cj in par_ins]
    out_specs = [pl.BlockSpec((nrows, LANE), lambda j, cj=cj: (0, cj(j))) for _, cj, _ in row_outs]
    out_specs += [pl.BlockSpec((k, LANE), lambda j, cj=cj: (0, cj(j))) for k, _, cj in par_outs]
    out_shape = [jax.ShapeDtypeStruct((nrows, width), dt) for width, _, dt in row_outs]
    out_shape += [jax.ShapeDtypeStruct((k, width), F32) for k, width, _ in par_outs]
    return pl.pallas_call(
        body, name=name, grid=(ncol,), in_specs=in_specs, out_specs=out_specs, out_shape=out_shape,
        compiler_params=_cparams(("parallel",), 48),
    )(*[a for a, _ in row_ins], *[a for a, _ in par_ins])


def _tri():
    r = lax.broadcasted_iota(jnp.int32, (CH, CH), 0)
    c = lax.broadcasted_iota(jnp.int32, (CH, CH), 1)
    return r, c


def _col(row):
    r, c = _tri()
    return jnp.sum(jnp.where(r == c, row, 0.0), axis=1, keepdims=True)


def _cumsum_rc(g_r):
    r, c = _tri()
    g_c = _col(g_r)
    cs_r = jnp.sum(jnp.where(r <= c, g_c, 0.0), axis=0, keepdims=True)
    cs_c = jnp.sum(jnp.where(c <= r, g_r, 0.0), axis=1, keepdims=True)
    return cs_r, cs_c


def _decay(cs_r, cs_c):
    r, c = _tri()
    return jnp.exp(jnp.where(c <= r, cs_c - cs_r, -jnp.inf))


def _gdn_a(ks, betas, gs):
    r, c = _tri()
    cs = [_cumsum_rc(g) for g in gs]
    kk = [_dg(k, k, 1, 1) for k in ks]
    return [jnp.where(c < r, _col(b) * kki * _decay(*csi), 0.0) for b, kki, csi in zip(betas, kk, cs)]


def _neumann(a_list):
    r, c = _tri()
    xs = [jnp.where(r == c, 1.0, 0.0) - a for a in a_list]
    ps = list(a_list)
    n = 2
    while n < CH:
        ps = [_dgh(p, p, 1, 0) for p in ps]
        xs = [x + _dgh(x, p, 1, 0) for x, p in zip(xs, ps)]
        n *= 2
    return xs


def _gdn_rest(ss, qs, ks, vs, betas, gs, ts):
    n = range(len(ss))
    cs = [_cumsum_rc(g) for g in gs]
    dm = [_decay(*csi) for csi in cs]
    ecs = [jnp.exp(csi[1]) for csi in cs]
    bc = [_col(b) for b in betas]
    u = [_dgh(ts[i], vs[i] * bc[i], 1, 0) for i in n]
    w = [_dgh(ts[i], ks[i] * (bc[i] * ecs[i]), 1, 0) for i in n]
    ws = [_dg(w[i], ss[i], 1, 0) for i in n]
    v_new = [u[i] - ws[i] for i in n]
    qk = [_dg(qs[i], ks[i], 1, 1) * dm[i] for i in n]
    o_in = [_dg(qs[i] * ecs[i], ss[i], 1, 0) for i in n]
    o = [o_in[i] + _dg(qk[i], v_new[i], 1, 0) for i in n]
    g_last = [jnp.sum(g, axis=1, keepdims=True) for g in gs]
    s_new = [ss[i] * jnp.exp(g_last[i]) + _dg(ks[i] * jnp.exp(g_last[i] - cs[i][1]), v_new[i], 0, 0) for i in n]
    return s_new, o


def _gdn_fwd(q, k, v, beta, g, hb):
    t_rows, d = q.shape
    nc, ng, w = t_rows // CH, d // (HD * hb), HD * hb
    sls = [slice(h * HD, (h + 1) * HD) for h in range(hb)]

    def body(q_ref, k_ref, v_ref, b_ref, g_ref, o_ref, ss_ref, ts_ref, s_scr):
        c = pl.program_id(1)

        @pl.when(c == 0)
        def _():
            s_scr[...] = jnp.zeros_like(s_scr)

        qs, ks, vs = ([ref[:, sl] for sl in sls] for ref in (q_ref, k_ref, v_ref))
        br = [b_ref[0, 0, pl.ds(h, 1), :] for h in range(hb)]
        gr = [g_ref[0, 0, pl.ds(h, 1), :] for h in range(hb)]
        s0 = [s_scr[h] for h in range(hb)]
        tm = _neumann(_gdn_a(ks, br, gr))
        s1, o = _gdn_rest(s0, qs, ks, vs, br, gr, tm)
        for h in range(hb):
            ss_ref[0, 0, h] = s0[h]
            ts_ref[0, 0, h] = tm[h]
            o_ref[:, sls[h]] = o[h]
            s_scr[h] = s1[h]

    blk = pl.BlockSpec((CH, w), lambda n, c: (c, n))
    row = pl.BlockSpec((1, 1, hb, CH), lambda n, c: (n, c, 0, 0))
    return pl.pallas_call(
        body, name="gdn_fwd", grid=(ng, nc), in_specs=[blk, blk, blk, row, row],
        out_specs=[blk, pl.BlockSpec((1, 1, hb, HD, HD), lambda n, c: (n, c, 0, 0, 0)),
                   pl.BlockSpec((1, 1, hb, CH, CH), lambda n, c: (n, c, 0, 0, 0))],
        out_shape=[jax.ShapeDtypeStruct((t_rows, d), F32), jax.ShapeDtypeStruct((ng, nc, hb, HD, HD), F32),
                   jax.ShapeDtypeStruct((ng, nc, hb, CH, CH), F32)],
        scratch_shapes=[pltpu.VMEM((hb, HD, HD), F32)],
        compiler_params=_cparams(("parallel", "arbitrary"), 32),
    )(q, k, v, beta, g)


def _gdn_bwd(q, k, v, beta, g, ss, ts, do, hb):
    t_rows, d = q.shape
    nc, ng, w = t_rows // CH, d // (HD * hb), HD * hb
    per_f = ss.shape[2] // hb
    sls = [slice(h * HD, (h + 1) * HD) for h in range(hb)]

    def body(q_ref, k_ref, v_ref, b_ref, g_ref, ss_ref, ts_ref, do_ref, dq_ref, dk_ref, dv_ref, db_ref, dg_ref, ds_scr):
        cr = pl.program_id(1)

        @pl.when(cr == 0)
        def _():
            ds_scr[...] = jnp.zeros_like(ds_scr)

        first = cr == nc - 1
        rowi = lax.broadcasted_iota(jnp.int32, (CH, 1), 0)
        lani = lax.broadcasted_iota(jnp.int32, (1, CH), 1)
        keep_c = jnp.logical_or(jnp.logical_not(first), rowi >= PADR)
        keep_r = jnp.logical_or(jnp.logical_not(first), lani >= PADR)
        hs = range(hb)
        qs, ks, vs, dos = ([ref[:, sl] for sl in sls] for ref in (q_ref, k_ref, v_ref, do_ref))
        br = [b_ref[0, 0, pl.ds(h, 1), :] for h in hs]
        gr = [g_ref[0, 0, pl.ds(h, 1), :] for h in hs]
        tm = [ts_ref[0, 0, h] for h in hs]
        _, vjp_rest = jax.vjp(_gdn_rest, [ss_ref[0, 0, h] for h in hs], qs, ks, vs, br, gr, tm)
        ds0, dq, dk, dv, db, dg, dt = vjp_rest(([ds_scr[h] for h in hs], dos))
        dtt = [_dgh(dt[h], tm[h], 1, 1) for h in hs]
        da = [-_dgh(tm[h], dtt[h], 0, 0) for h in hs]
        _, vjp_a = jax.vjp(_gdn_a, ks, br, gr)
        dk2, db2, dg2 = vjp_a(da)
        for h in hs:
            ds_scr[h] = ds0[h]
            dq_ref[:, sls[h]] = jnp.where(keep_c, dq[h], 0.0)
            dk_ref[:, sls[h]] = jnp.where(keep_c, dk[h] + dk2[h], 0.0)
            dv_ref[:, sls[h]] = jnp.where(keep_c, dv[h], 0.0)
            db_ref[0, 0, pl.ds(h, 1), :] = jnp.where(keep_r, db[h] + db2[h], 0.0)
            dg_ref[0, 0, pl.ds(h, 1), :] = jnp.where(keep_r, dg[h] + dg2[h], 0.0)

    blk = pl.BlockSpec((CH, w), lambda n, c: (nc - 1 - c, n))
    row = pl.BlockSpec((1, 1, hb, CH), lambda n, c: (n, nc - 1 - c, 0, 0))
    return pl.pallas_call(
        body, name="gdn_bwd", grid=(ng, nc),
        in_specs=[blk, blk, blk, row, row,
                  pl.BlockSpec((1, 1, hb, HD, HD), lambda n, c: (n // per_f, nc - 1 - c, n % per_f, 0, 0)),
                  pl.BlockSpec((1, 1, hb, CH, CH), lambda n, c: (n // per_f, nc - 1 - c, n % per_f, 0, 0)), blk],
        out_specs=[blk, blk, blk, row, row],
        out_shape=[jax.ShapeDtypeStruct((t_rows, d), F32)] * 3 + [jax.ShapeDtypeStruct((ng, nc, hb, CH), F32)] * 2,
        scratch_shapes=[pltpu.VMEM((hb, HD, HD), F32)],
        compiler_params=_cparams(("parallel", "arbitrary"), 32),
    )(q, k, v, beta, g, ss, ts, do)


def _ssd_group(s, xs, bm, cm, dt_r, a_r):
    prs = range(len(s))
    ngrp = bm.shape[1] // NST
    grp = [p // (len(s) // ngrp) for p in prs]
    bms = [bm[:, g * NST:(g + 1) * NST] for g in range(ngrp)]
    cms = [cm[:, g * NST:(g + 1) * NST] for g in range(ngrp)]
    first = lax.broadcasted_iota(jnp.int32, (1, 2 * M2P), 1) < M2P

    def pick(vals, p):
        return jnp.where(first, vals[2 * p], vals[2 * p + 1])

    cs = [_cumsum_rc(a) for a in a_r]
    lm = [_decay(*csi) for csi in cs]
    ecs = [jnp.exp(csi[1]) for csi in cs]
    alast = [jnp.sum(a, axis=1, keepdims=True) for a in a_r]
    ealast = [jnp.exp(al) for al in alast]
    wt = [jnp.exp(al - csi[1]) for al, csi in zip(alast, cs)]
    dtc = [_col(t) for t in dt_r]
    xdt = [xs[:, p * LANE:(p + 1) * LANE] * pick(dtc, p) for p in prs]
    cb = [_dg(cms[g], bms[g], 1, 1) for g in range(ngrp)]
    y0 = [_dg(cb[grp[p]] * lm[2 * p], xdt[p], 1, 0) for p in prs]
    y1 = [_dg(cb[grp[p]] * lm[2 * p + 1], xdt[p], 1, 0) for p in prs]
    yo = [_dg(cms[grp[p]], s[p], 1, 0) for p in prs]
    y = [jnp.where(first, y0[p], y1[p]) + yo[p] * pick(ecs, p) for p in prs]
    s_new = [s[p] * pick(ealast, p) + _dg(bms[grp[p]], xdt[p] * pick(wt, p), 0, 0) for p in prs]
    return s_new, jnp.concatenate(y, axis=1)


def _ssd_specs(nc, d, rev, gps):
    assert M2G % gps == 0 and (d // LANE) % gps == 0
    hps = (d // M2P) // M2G * gps
    cc = (lambda c: nc - 1 - c) if rev else (lambda c: c)
    xs = pl.BlockSpec((CH, hps * M2P), lambda g, c: (cc(c), g))
    bm = pl.BlockSpec((CH, NST * gps), lambda g, c: (cc(c), (d // LANE) // gps + g))
    cm = pl.BlockSpec((CH, NST * gps), lambda g, c: (cc(c), (d // LANE + M2G) // gps + g))
    row = pl.BlockSpec((1, 1, hps, CH), lambda g, c: (g, cc(c), 0, 0))
    return xs, bm, cm, row, hps


def _ssd_fwd(xbc, dt, a, d, gps):
    t_rows = xbc.shape[0]
    nc = t_rows // CH
    xs, bm, cm, row, hpg = _ssd_specs(nc, d, False, gps)
    ppg = hpg // 2
    st = pl.BlockSpec((1, 1, ppg, NST, LANE), lambda g, c: (g, c, 0, 0, 0))

    def body(xs_ref, b_ref, c_ref, dt_ref, a_ref, y_ref, ss_ref, s_scr):
        c = pl.program_id(1)

        @pl.when(c == 0)
        def _():
            s_scr[...] = jnp.zeros_like(s_scr)

        s0 = [s_scr[p] for p in range(ppg)]
        for p in range(ppg):
            ss_ref[0, 0, p] = s0[p]
        dt_r = [dt_ref[0, 0, pl.ds(h, 1), :] for h in range(hpg)]
        a_r = [a_ref[0, 0, pl.ds(h, 1), :] for h in range(hpg)]
        s1, y = _ssd_group(s0, xs_ref[...], b_ref[...], c_ref[...], dt_r, a_r)
        y_ref[...] = y
        for p in range(ppg):
            s_scr[p] = s1[p]

    return pl.pallas_call(
        body, name="ssd_fwd", grid=(M2G // gps, nc), in_specs=[xs, bm, cm, row, row], out_specs=[xs, st],
        out_shape=[jax.ShapeDtypeStruct((t_rows, d), F32), jax.ShapeDtypeStruct((M2G // gps, nc, ppg, NST, LANE), F32)],
        scratch_shapes=[pltpu.VMEM((ppg, NST, LANE), F32)],
        compiler_params=_cparams(("parallel", "arbitrary"), 32),
    )(xbc, xbc, xbc, dt, a)


def _ssd_bwd(xbc, dt, a, ss, dy, d, gps):
    t_rows = xbc.shape[0]
    nc = t_rows // CH
    xs, bm, cm, row, hpg = _ssd_specs(nc, d, True, gps)
    ppg = hpg // 2
    per_f = ss.shape[2] // ppg
    st = pl.BlockSpec((1, 1, ppg, NST, LANE), lambda g, c: (g // per_f, nc - 1 - c, g % per_f, 0, 0))

    def body(xs_ref, b_ref, c_ref, dt_ref, a_ref, ss_ref, dy_ref, dxs_ref, db_ref, dc_ref, ddt_ref, da_ref, ds_scr):
        cr = pl.program_id(1)

        @pl.when(cr == 0)
        def _():
            ds_scr[...] = jnp.zeros_like(ds_scr)

        first = cr == nc - 1
        keep_c = jnp.logical_or(jnp.logical_not(first), lax.broadcasted_iota(jnp.int32, (CH, 1), 0) >= PADR)
        keep_r = jnp.logical_or(jnp.logical_not(first), lax.broadcasted_iota(jnp.int32, (1, CH), 1) >= PADR)
        dt_r = [dt_ref[0, 0, pl.ds(h, 1), :] for h in range(hpg)]
        a_r = [a_ref[0, 0, pl.ds(h, 1), :] for h in range(hpg)]
        s0 = [ss_ref[0, 0, p] for p in range(ppg)]
        _, vjp = jax.vjp(_ssd_group, s0, xs_ref[...], b_ref[...], c_ref[...], dt_r, a_r)
        ds0, dxs, db, dc, ddt, da = vjp(([ds_scr[p] for p in range(ppg)], dy_ref[...]))
        for p in range(ppg):
            ds_scr[p] = ds0[p]
        dxs_ref[...] = jnp.where(keep_c, dxs, 0.0)
        db_ref[...] = jnp.where(keep_c, db, 0.0)
        dc_ref[...] = jnp.where(keep_c, dc, 0.0)
        for h in range(hpg):
            ddt_ref[0, 0, pl.ds(h, 1), :] = jnp.where(keep_r, ddt[h], 0.0)
            da_ref[0, 0, pl.ds(h, 1), :] = jnp.where(keep_r, da[h], 0.0)

    grp = pl.BlockSpec((CH, NST * gps), lambda g, c: (nc - 1 - c, g))
    return pl.pallas_call(
        body, name="ssd_bwd", grid=(M2G // gps, nc), in_specs=[xs, bm, cm, row, row, st, xs],
        out_specs=[xs, grp, grp, row, row],
        out_shape=[jax.ShapeDtypeStruct((t_rows, d), F32)] + [jax.ShapeDtypeStruct((t_rows, M2G * NST), F32)] * 2
        + [jax.ShapeDtypeStruct((M2G // gps, nc, hpg, CH), F32)] * 2,
        scratch_shapes=[pltpu.VMEM((ppg, NST, LANE), F32)],
        compiler_params=_cparams(("parallel", "arbitrary"), 32),
    )(xbc, xbc, xbc, dt, a, ss, dy)


def _exchange(name, gathers, scatters, after):
    arrays = list(gathers) + list(scatters)
    n_g, n = len(gathers), len(arrays)

    def body(*refs):
        ins, outs = refs[:n], refs[n + 1:2 * n + 1]
        send_sems, recv_sems, local_sems = refs[2 * n + 1:]
        x, y, c = lax.axis_index("x"), lax.axis_index("y"), lax.axis_index("c")
        me = 4 * x + 2 * y + c

        def src(a, slot):
            return ins[a] if a < n_g else ins[a].at[slot]

        local = [pltpu.make_async_copy(src(a, me), outs[a].at[me], local_sems.at[a]) for a in range(n)]
        for cp in local:
            cp.start()
        copies = []
        for rel in range(1, NDEV):
            px, py, pc = x ^ (rel >> 2), y ^ ((rel >> 1) & 1), c ^ (rel & 1)
            peer = 4 * px + 2 * py + pc
            for a in range(n):
                copies.append(pltpu.make_async_remote_copy(
                    src_ref=src(a, peer), dst_ref=outs[a].at[me], send_sem=send_sems.at[a, rel - 1],
                    recv_sem=recv_sems.at[a, rel - 1], device_id=(px, py, pc), device_id_type=pl.DeviceIdType.MESH))
        for cp in copies:
            cp.start()
        for cp in copies:
            cp.wait_recv()
        for cp in copies:
            cp.wait_send()
        for cp in local:
            cp.wait()

    any_spec = pl.BlockSpec(memory_space=pl.ANY)
    out_shape = [jax.ShapeDtypeStruct((NDEV,) + a.shape, a.dtype) for a in gathers]
    out_shape += [jax.ShapeDtypeStruct(a.shape, a.dtype) for a in scatters]
    return pl.pallas_call(
        body, name=name, in_specs=[any_spec] * (n + 1), out_specs=[any_spec] * n, out_shape=out_shape,
        scratch_shapes=[pltpu.SemaphoreType.DMA((n, NDEV - 1)), pltpu.SemaphoreType.DMA((n, NDEV - 1)),
                        pltpu.SemaphoreType.DMA((n,))],
        compiler_params=pltpu.CompilerParams(has_side_effects=True),
    )(*arrays, after)


def _gather_two_level(name, arrays):
    n = len(arrays)

    def body(*refs):
        ins, outs = refs[:n], refs[n:2 * n]
        send_sems, recv_sems, local_sems = refs[2 * n:]
        x, y, c = lax.axis_index("x"), lax.axis_index("y"), lax.axis_index("c")
        me, sibling = (x, y, c), (x, y, 1 - c)
        chips = [(1 - x, y), (x, 1 - y), (1 - x, 1 - y)]

        def copy(a, k, block, to, src=None):
            dst = outs[a].at[4 * block[0] + 2 * block[1] + block[2]]
            return pltpu.make_async_remote_copy(
                src_ref=dst if src is None else src, dst_ref=dst, send_sem=send_sems.at[a, k], recv_sem=recv_sems.at[a, k],
                device_id=to, device_id_type=pl.DeviceIdType.MESH)

        mine = [pltpu.make_async_copy(ins[a], outs[a].at[4 * x + 2 * y + c], local_sems.at[a]) for a in range(n)]
        for cp in mine:
            cp.start()
        first = []
        for a in range(n):
            first.append(copy(a, 0, me, sibling, src=ins[a]))
            first += [copy(a, 1 + j, me, (*chip, c), src=ins[a]) for j, chip in enumerate(chips)]
        for cp in first:
            cp.start()
        passed = [[copy(a, 4 + j, (*chip, c), sibling) for j, chip in enumerate(chips)] for a in range(n)]
        for j, chip in enumerate(chips):
            for a in range(n):
                copy(a, 1 + j, (*chip, c), me).wait_recv()
                passed[a][j].start()
        for a in range(n):
            copy(a, 0, sibling, me).wait_recv()
            for j, chip in enumerate(chips):
                copy(a, 4 + j, (*chip, 1 - c), me).wait_recv()
        for cp in first + [cp for row in passed for cp in row]:
            cp.wait_send()
        for cp in mine:
            cp.wait()

    any_spec = pl.BlockSpec(memory_space=pl.ANY)
    return pl.pallas_call(
        body, name=name, in_specs=[any_spec] * n, out_specs=[any_spec] * n,
        out_shape=[jax.ShapeDtypeStruct((NDEV,) + a.shape, a.dtype) for a in arrays],
        scratch_shapes=[pltpu.SemaphoreType.DMA((n, NDEV - 1)), pltpu.SemaphoreType.DMA((n, NDEV - 1)),
                        pltpu.SemaphoreType.DMA((n,))],
        compiler_params=pltpu.CompilerParams(has_side_effects=True),
    )(*arrays)


_HBM = pl.BlockSpec(memory_space=pltpu.HBM)
_SEM = pl.BlockSpec(memory_space=pltpu.SEMAPHORE)
_EFFECT = pltpu.SideEffectType.DATAFLOW_SIDE_EFFECTING


def _split_copies(srcs, lands, send_sems, recv_sems, n_g):
    x, y, c = lax.axis_index("x"), lax.axis_index("y"), lax.axis_index("c")
    me = 4 * x + 2 * y + c
    copies = []
    for rel in range(1, NDEV):
        px, py, pc = x ^ (rel >> 2), y ^ ((rel >> 1) & 1), c ^ (rel & 1)
        peer = 4 * px + 2 * py + pc
        for a in range(len(srcs)):
            copies.append(pltpu.make_async_remote_copy(
                src_ref=srcs[a] if a < n_g else srcs[a].at[peer], dst_ref=lands[a].at[me],
                send_sem=send_sems.at[a * (NDEV - 1) + rel - 1], recv_sem=recv_sems.at[a * (NDEV - 1) + rel - 1],
                device_id=(px, py, pc), device_id_type=pl.DeviceIdType.MESH))
    return copies


def _exchange_start(name, gathers, scatters, after):
    arrays = list(gathers) + list(scatters)
    n_g, n = len(gathers), len(arrays)
    lands = [lax.empty((NDEV,) + a.shape, a.dtype) for a in gathers] + [lax.empty(a.shape, a.dtype) for a in scatters]

    def body(*refs):
        send_sems, recv_sems = refs[2 * n + 1], refs[2 * n + 2]
        for cp in _split_copies(refs[:n], refs[n:2 * n], send_sems, recv_sems, n_g):
            cp.start()
        refs[-1][...] = jnp.zeros_like(refs[-1])

    sems = pltpu.SemaphoreType.DMA((n * (NDEV - 1),))
    out = pl.pallas_call(
        body, name=name, in_specs=[_HBM] * (2 * n) + [pl.BlockSpec(memory_space=pl.ANY)],
        out_specs=(_SEM, _SEM, *[_HBM] * (2 * n), pl.BlockSpec(memory_space=pltpu.VMEM)),
        out_shape=(sems, sems, *[pltpu.HBM(a.shape, a.dtype) for a in arrays + lands], jax.ShapeDtypeStruct((8, LANE), F32)),
        input_output_aliases={i: 2 + i for i in range(2 * n)},
        compiler_params=pltpu.CompilerParams(has_side_effects=_EFFECT),
    )(*[pltpu.with_memory_space_constraint(a, pltpu.HBM) for a in arrays + lands], after)
    return out[0], out[1], list(out[2:2 + 2 * n]), out[-1], n_g


def _exchange_wait(name, started, after):
    send_sems, recv_sems, thru, _, n_g = started
    n = len(thru) // 2

    def body(*refs):
        for cp in _split_copies(refs[:n], refs[n:2 * n], refs[2 * n], refs[2 * n + 1], n_g):
            cp.wait_send()
            cp.wait_recv()

    out = pl.pallas_call(
        body, name=name, in_specs=[_HBM] * (2 * n) + [_SEM, _SEM, pl.BlockSpec(memory_space=pl.ANY)],
        out_specs=[_HBM] * (2 * n), out_shape=[pltpu.HBM(a.shape, a.dtype) for a in thru],
        input_output_aliases={i: i for i in range(2 * n)},
        compiler_params=pltpu.CompilerParams(has_side_effects=_EFFECT),
    )(*thru, send_sems, recv_sems, after)
    me = 4 * lax.axis_index("x") + 2 * lax.axis_index("y") + lax.axis_index("c")
    full = []
    for a in range(n):
        own = out[a][None] if a < n_g else lax.dynamic_index_in_dim(out[a], me, 0, keepdims=True)
        full.append(lax.dynamic_update_index_in_dim(out[n + a], own, me, 0))
    return full


def _adamw(name, staged, w, m, v):
    r, c = w.shape
    tr = _pick(r, 256, 8)

    def body(st_ref, w_ref, m_ref, v_ref, g_ref, d_ref, nm_ref, nv_ref):
        g = st_ref[0].astype(F32)
        for k in range(1, NDEV):
            g = g + st_ref[k].astype(F32)
        m_new = ADAM_B1 * m_ref[...] + (1.0 - ADAM_B1) * g
        v_new = ADAM_B2 * v_ref[...] + (1.0 - ADAM_B2) * jnp.square(g)
        m_hat = m_new / (1.0 - ADAM_B1 ** ADAM_STEP)
        v_hat = v_new / (1.0 - ADAM_B2 ** ADAM_STEP)
        g_ref[...] = g
        d_ref[...] = -ADAM_LR * (m_hat / (jnp.sqrt(v_hat) + ADAM_EPS) + ADAM_WD * w_ref[...])
        nm_ref[...] = m_new
        nv_ref[...] = v_new

    blk = pl.BlockSpec((tr, c), lambda i: (i, 0))
    return pl.pallas_call(
        body, name=name, grid=(r // tr,), in_specs=[pl.BlockSpec((NDEV, tr, c), lambda i: (0, i, 0)), blk, blk, blk],
        out_specs=[blk] * 4, out_shape=[jax.ShapeDtypeStruct((r, c), F32)] * 4,
        compiler_params=_cparams(("parallel",), 48),
    )(staged, w, m, v)


def _pack(parts):
    flat = jnp.concatenate([p.reshape(-1).astype(F32) for p in parts])
    pad = (-flat.shape[0]) % (8 * LANE)
    return jnp.pad(flat, (0, pad)).reshape(-1, LANE)


def _unpack(slab, shapes):
    flat, out, off = slab.reshape(-1), [], 0
    for s in shapes:
        n = 1
        for dim in s:
            n *= dim
        out.append(flat[off:off + n].reshape(s))
        off += n
    return out


def _to_shards(full, axis):
    shp = full.shape
    t = full.reshape(shp[:axis] + (NDEV, shp[axis] // NDEV) + shp[axis + 1:])
    return jnp.moveaxis(t, axis, 0)


def _from_shards(g, axis):
    t = jnp.moveaxis(g, 0, axis)
    shp = t.shape
    return t.reshape(shp[:axis] + (shp[axis] * shp[axis + 1],) + shp[axis + 2:])


def kernel(x, meta_tokens, norm_mix_w, w_in, dn_conv_w, dn_a_log, dn_dt_bias, dn_norm_w, m2_conv_w, m2_conv_b, m2_a_log, m2_dt_bias, m2_d, m2_norm_w, w_out, norm_ffn_w, ffn_up, ffn_conv_w, ffn_down, norm_final_w, loss_target, m_meta_tokens, m_norm_mix_w, m_w_in, m_dn_conv_w, m_dn_a_log, m_dn_dt_bias, m_dn_norm_w, m_m2_conv_w, m_m2_conv_b, m_m2_a_log, m_m2_dt_bias, m_m2_d, m_m2_norm_w, m_w_out, m_norm_ffn_w, m_ffn_up, m_ffn_conv_w, m_ffn_down, m_norm_final_w, v_meta_tokens, v_norm_mix_w, v_w_in, v_dn_conv_w, v_dn_a_log, v_dn_dt_bias, v_dn_norm_w, v_m2_conv_w, v_m2_conv_b, v_m2_a_log, v_m2_dt_bias, v_m2_d, v_m2_norm_w, v_w_out, v_norm_ffn_w, v_ffn_up, v_ffn_conv_w, v_ffn_down, v_norm_final_w):
    seq, d = x.shape[1], x.shape[2]
    t_rows = seq + CH
    nc = t_rows // CH
    dnh, m2h = d // HD, d // M2P
    dff = ffn_down.shape[1] * NDEV
    xbc_w = d + 2 * M2G * NST
    assert seq % CH == 0 and d % (2 * M2P * M2G) == 0 and 2 * dnh + m2h <= LANE
    hb_f = max(h for h in (16, 8, 4, 2, 1) if dnh % h == 0)
    hb_b = max(h for h in (8, 4, 2, 1) if dnh % h == 0)
    tm_rw = _pick(t_rows, 208, 16)
    conv_chunk = _pick(t_rows, min(CONV_CHUNK, t_rows // 3), 16)

    small_sharded = [meta_tokens, dn_conv_w[0], m2_conv_w[0], ffn_conv_w[0]]
    small_shapes = [p.shape for p in small_sharded]
    g_win, g_small = _gather_two_level("gather_w_in", [w_in[0].astype(WIRE).T, _pack(small_sharded)])
    rest = _exchange_start("gather_rest_start", [w_out[0].astype(WIRE), ffn_up[0].astype(WIRE), ffn_down[0].astype(WIRE)], [],
                           g_small)
    win_t = g_win.reshape(-1, d)
    small_full = [_unpack(g_small[k], small_shapes) for k in range(NDEV)]
    meta_f, dnconv_f, m2conv_f, ffnconv_f = [jnp.concatenate([small_full[k][i] for k in range(NDEV)], axis=-1) for i in range(4)]

    o_z, o_b, o_a = 3 * d, 4 * d, 4 * d + dnh
    o_m2z = 4 * d + 2 * dnh
    o_xbc, o_dt = o_m2z + d, o_m2z + d + xbc_w
    w_all_t = jnp.concatenate([win_t[:o_b], win_t[o_m2z:o_dt], win_t[o_b:o_m2z], win_t[o_dt:],
                               jnp.zeros((LANE - 2 * dnh - m2h, d), WIRE)], axis=0)
    seg_cols = {"q": (0, d), "k": (d, d), "v": (2 * d, d), "z": (3 * d, d), "m2z": (4 * d, d), "xbc": (5 * d, xbc_w),
                "sm": (5 * d + xbc_w, LANE)}

    h0 = jnp.concatenate([jnp.zeros((PADR, d), F32), meta_f, x[0]], axis=0)
    valid = lambda rows: rows >= PADR

    def norm_fwd(name, h, w):
        return _rw(name, lambda rows, j, hv, wv: (_rms(hv, wv),), [("r", h, d, _c0), ("p", w, d, _c0)],
                   [("r", d, d, _c0, MXU)], t_rows, tm_rw)[0]

    hn1 = norm_fwd("norm_mix", h0, norm_mix_w)
    proj = {s: _mm("proj_" + s, hn1, w_all_t, tb=True, b_rows=seg_cols[s], dep=rest[3]) for s in seg_cols}

    def dn_post(sec, cv):
        s = _silu(cv)
        if sec < 2:
            s = s * lax.rsqrt(jnp.sum(s * s, axis=-1, keepdims=True) + EPS)
        if sec == 0:
            s = s * (HD ** -0.5)
        return s

    def dn_prep(sec, name):
        def fn(rows, own, last, wins, pars):
            return [jnp.where(valid(rows), dn_post(sec, _conv(wins[0], pars[0])), 0.0)], []
        wc = dnconv_f[:, sec * d:(sec + 1) * d]
        return _cv("dn_prep_" + name, fn, [(proj[name], _cj)], [(wc, _cj)], [(d, _cj, F32)], [], t_rows, dnh)[0]

    q_act, k_act, v_act = dn_prep(0, "q"), dn_prep(1, "k"), dn_prep(2, "v")

    lane = lambda: lax.broadcasted_iota(jnp.int32, (1, LANE), 1)

    def lanes_of(vec, off):
        return jnp.pad(vec.astype(F32), ((0, 0), (off, LANE - off - vec.shape[1])))

    gate_params = [lanes_of(dn_a_log, dnh), lanes_of(dn_dt_bias, dnh), lanes_of(m2_a_log, 2 * dnh), lanes_of(m2_dt_bias, 2 * dnh)]

    def gates(rows, sm, p_alog, p_dtb, p_malog, p_mdtb):
        ln = lane()
        is_b, is_g = ln < dnh, jnp.logical_and(ln >= dnh, ln < 2 * dnh)
        is_d = jnp.logical_and(ln >= 2 * dnh, ln < 2 * dnh + m2h)
        beta = jax.nn.sigmoid(sm)
        gdec = -jnp.exp(p_alog) * _softplus(sm + p_dtb)
        dt = _softplus(sm + p_mdtb)
        am = dt * (-jnp.exp(p_malog))
        ok = valid(rows)
        g1 = jnp.where(ok, jnp.where(is_b, beta, jnp.where(is_g, gdec, jnp.where(is_d, dt, 0.0))), 0.0)
        g2 = jnp.where(jnp.logical_and(ok, is_d), am, 0.0)
        return g1, g2

    gate_ins = [("r", proj["sm"], LANE, _c0)] + [("p", p, LANE, _c0) for p in gate_params]
    g1, g2 = _rw("gates", lambda rows, j, *a: gates(rows, *a), gate_ins,
                 [("r", LANE, LANE, _c0, F32), ("r", LANE, LANE, _c0, F32)], t_rows, tm_rw)

    def head_rows(cols, per):
        n = cols.shape[1]
        return cols.reshape(nc, CH, n // per, per).transpose(2, 0, 3, 1)

    def head_cols(rows_):
        ngrp, _, per, _ = rows_.shape
        return rows_.transpose(1, 3, 0, 2).reshape(t_rows, ngrp * per)

    beta_r, gdec_r = head_rows(g1[:, :dnh], hb_f), head_rows(g1[:, dnh:2 * dnh], hb_f)
    hpg = m2h // M2G
    m2_rows = lambda gps: (head_rows(g1[:, 2 * dnh:2 * dnh + m2h], hpg * gps), head_rows(g2[:, 2 * dnh:2 * dnh + m2h], hpg * gps))

    o_dn, dn_states, dn_tinv = _gdn_fwd(q_act, k_act, v_act, beta_r, gdec_r, hb_f)

    def dn_out(o, z, w):
        outs = []
        for h in range(dnh):
            sl = slice(h * HD, (h + 1) * HD)
            outs.append(_rms(o[:, sl], w) * _silu(z[:, sl]))
        return jnp.concatenate(outs, axis=1)

    mixed_dn = _rw("dn_out", lambda rows, j, o, z, w: (dn_out(o, z, w),),
                   [("r", o_dn, d, _c0), ("r", proj["z"], d, _c0), ("p", dn_norm_w, HD, _c0)], [("r", d, d, _c0, MXU)],
                   t_rows, tm_rw)[0]

    def m2_prep(rows, own, last, wins, pars):
        return [jnp.where(valid(rows), _silu(_conv(wins[0], pars[0]) + pars[1][0]), 0.0)], []

    xbc_act = _cv("m2_prep", m2_prep, [(proj["xbc"], _cj)], [(m2conv_f, _cj), (m2_conv_b, _cj)], [(xbc_w, _cj, F32)], [],
                  t_rows, xbc_w // LANE)[0]
    y_ssd, m2_states = _ssd_fwd(xbc_act, *m2_rows(SSD_GPS_F), d, SSD_GPS_F)

    d_lanes = jnp.repeat(m2_d.astype(F32), M2P, axis=1)
    gw = d // M2G

    def m2_out(ys, xs, z, dl, nw):
        yv = (ys + dl * xs) * _silu(z)
        outs = []
        for gi in range(M2G):
            sl = slice(gi * gw, (gi + 1) * gw)
            outs.append(_rms(yv[:, sl], nw[:, sl]))
        return jnp.concatenate(outs, axis=1)

    m2_out_ins = [("r", y_ssd, d, _c0), ("r", xbc_act, d, _c0), ("r", proj["m2z"], d, _c0), ("p", d_lanes, d, _c0),
                  ("p", m2_norm_w, d, _c0)]
    mixed_m2 = _rw("m2_out", lambda rows, j, *a: (m2_out(*a),), m2_out_ins, [("r", d, d, _c0, MXU)], t_rows, tm_rw)[0]

    mixed = jnp.concatenate([mixed_dn, mixed_m2], axis=1)
    g_wout, g_wup, g_wdown = _exchange_wait("gather_rest_wait", rest, mixed)
    wout = _from_shards(g_wout, 0)
    wup = _from_shards(g_wup, 1)
    wdown = _from_shards(g_wdown, 0)
    up_g, up_v = (0, dff), (dff, dff)
    h1 = _mm("out_proj", mixed, wout, add=h0)
    hn2 = norm_fwd("norm_ffn", h1, norm_ffn_w)
    u_g, u_v = _mm("ffn_up_g", hn2, wup, b_cols=up_g), _mm("ffn_up_v", hn2, wup, b_cols=up_v)
    fc_g, fc_v = ffnconv_f[:, :dff], ffnconv_f[:, dff:]

    def ffn_act(rows, own, last, wins, pars):
        return [jnp.where(valid(rows), _silu(_conv(wins[0], pars[0])) * _conv(wins[1], pars[1]), 0.0)], []

    act = _cv("ffn_act", ffn_act, [(u_g, _cj), (u_v, _cj)], [(fc_g, _cj), (fc_v, _cj)], [(dff, _cj, MXU)], [],
              t_rows, dff // LANE)[0]
    h2 = _mm("ffn_down", act, wdown, add=h1, tk=1408)

    def loss_fn(hv, wf, tgt, rows):
        err = jnp.where(rows >= CH, _rms(hv, wf) - tgt, 0.0)
        return 0.5 * jnp.sum(jnp.mean(err * err, axis=-1, keepdims=True), axis=0, keepdims=True)

    def final(rows, j, hv, wf, tgt):
        loss, vjp = jax.vjp(lambda a, b: loss_fn(a, b, tgt, rows), hv, wf)
        dh, dw = vjp(jnp.ones((1, 1), F32))
        return dh, dh, dw, jnp.broadcast_to(loss, (1, LANE))

    wf2 = norm_final_w.reshape(1, d)
    dh2, dh2_m, d_wf, loss_part = _rw(
        "loss_head", final, [("r", h2, d, _c0), ("p", wf2, d, _c0), ("r", loss_target[0], d, _c0, lambda i: jnp.maximum(i - 1, 0))],
        [("r", d, d, _c0, F32), ("r", d, d, _c0, MXU), ("p", 1, d, d, _c0), ("p", 1, LANE, LANE, _c0)], t_rows, CH)
    loss = lax.psum(loss_part[0, 0], MESH_AXES)

    d_act = _mm("d_act", dh2_m, wdown, tb=True)
    gw_down = _mm("gw_down", act, dh2_m, ta=True, tm=1408, tn=1024, tk=2080, out_dtype=WIRE)
    x_down = _exchange_start("grad_down_start", [], [_to_shards(gw_down, 0).astype(WIRE)], gw_down)

    def t_end(last):
        return t_rows if last else None

    def ffn_act_bwd(rows, own, last, wins, pars):
        (ug, uv, da), (wg, wv) = wins, pars
        cg, cv = _conv(ug, wg), _conv(uv, wv)
        _, vjp = jax.vjp(lambda a, b: _silu(a) * b, cg, cv)
        dcg, dcv = vjp(jnp.where(valid(rows), da, 0.0))
        return ([_conv_t(dcg, wg, rows, t_end(last)), _conv_t(dcv, wv, rows, t_end(last))],
                [_conv_w(jnp.where(own, dcg, 0.0), ug, len(wg)), _conv_w(jnp.where(own, dcv, 0.0), uv, len(wv))])

    kf = fc_g.shape[0]
    du_g, du_v, g_fc_g, g_fc_v = _cv(
        "ffn_act_bwd", ffn_act_bwd, [(u_g, _cj), (u_v, _cj), (d_act, _cj)], [(fc_g, _cj), (fc_v, _cj)],
        [(dff, _cj, MXU), (dff, _cj, MXU)], [(kf, dff, _cj), (kf, dff, _cj)], t_rows, dff // LANE, chunk=conv_chunk)
    gw_up_g = _mm("gw_up_g", hn2, du_g, ta=True, tm=1024, tn=1408, tk=2080, out_dtype=WIRE, dep=x_down[3])
    gw_up_v = _mm("gw_up_v", hn2, du_v, ta=True, tm=1024, tn=1408, tk=2080, out_dtype=WIRE)
    gw_up_full = jnp.concatenate([gw_up_g, gw_up_v], axis=1)
    x_up = _exchange_start("grad_up_start", [], [_to_shards(gw_up_full, 1).astype(WIRE)], gw_up_full)
    d_hn2 = _mm("d_hn2_v", du_v, wup, b_cols=up_v, tb=True, tk=1408, dep=x_up[3],
                add=_mm("d_hn2_g", du_g, wup, b_cols=up_g, tb=True, tk=1408, tn=2048, dep=x_up[3]))

    def norm_bwd(name, h, w, dy, dres):
        def fn(rows, j, hv, wv, dyv, dr):
            _, vjp = jax.vjp(_rms, hv, wv)
            dh, dw = vjp(dyv)
            dh = dh + dr
            return dh, dh, dw
        return _rw(name, fn, [("r", h, d, _c0), ("p", w, d, _c0), ("r", dy, d, _c0), ("r", dres, d, _c0)],
                   [("r", d, d, _c0, F32), ("r", d, d, _c0, MXU), ("p", 1, d, d, _c0)], t_rows, tm_rw)

    dh1, dh1_m, g_norm_ffn = norm_bwd("norm_ffn_bwd", h1, norm_ffn_w, d_hn2, dh2)

    gw_out = _mm("gw_out", mixed, dh1_m, ta=True, tm=1024, tn=1024, tk=2080, out_dtype=WIRE)
    x_out = _exchange_start("grad_out_start", [], [_to_shards(gw_out, 0).astype(WIRE)], gw_out)
    d_mixed = _mm("d_mixed", dh1_m, wout, tb=True, dep=x_out[3])

    gw_seg = {}

    def gw_in(seg, dseg_arr):
        gw_seg[seg] = _mm("gw_in_" + seg, hn1, dseg_arr, ta=True, tm=1024, tn=1024, tk=2080, out_dtype=WIRE)

    def m2_out_bwd(rows, j, ys, xs, z, dl, nw, dy):
        _, vjp = jax.vjp(m2_out, ys, xs, z, dl, nw)
        return vjp(dy)

    dy_ssd, dxs_skip, d_m2z, g_d_lanes, g_m2_norm = _rw(
        "m2_out_bwd", m2_out_bwd, m2_out_ins + [("r", d_mixed, d, lambda j: 1)],
        [("r", d, d, _c0, F32), ("r", d, d, _c0, F32), ("r", d, d, _c0, MXU), ("p", 1, d, d, _c0), ("p", 1, d, d, _c0)],
        t_rows, _pick(t_rows, 208, 16))
    gw_in("m2z", d_m2z)

    def fold_heads(vec_ref, out_ref):
        r = lax.broadcasted_iota(jnp.int32, (d, LANE), 0)
        c = lax.broadcasted_iota(jnp.int32, (d, LANE), 1)
        out_ref[...] = _dgh(vec_ref[...], jnp.where(jnp.logical_and(r >= c * M2P, r < (c + 1) * M2P), 1.0, 0.0), 1, 0)

    g_m2_d = pl.pallas_call(fold_heads, name="fold_m2_d", out_shape=jax.ShapeDtypeStruct((1, LANE), F32))(g_d_lanes)

    dxs, db_ssd, dc_ssd, ddt_r, dam_r = _ssd_bwd(xbc_act, *m2_rows(SSD_GPS_B), m2_states, dy_ssd, d, SSD_GPS_B)


    def m2_prep_bwd(rows, own, last, wins, pars):
        (p, *ds), (w, b) = wins, pars
        _, vjp = jax.vjp(_silu, _conv(p, w) + b[0])
        dpre, = vjp(jnp.where(valid(rows), functools.reduce(lambda a_, b_: a_ + b_, ds), 0.0))
        dpre_own = jnp.where(own, dpre, 0.0)
        return [_conv_t(dpre, w, rows, t_end(last))], [_conv_w(dpre_own, p, len(w)), [jnp.sum(dpre_own, axis=0, keepdims=True)]]

    def m2_prep_bwd_call(name, off, width, d_arrs):
        at = lambda j, blk0=off // LANE: blk0 + j
        return _cv(name, m2_prep_bwd, [(proj["xbc"], at)] + [(a, _cj) for a in d_arrs], [(m2conv_f, at), (m2_conv_b, at)],
                   [(width, _cj, MXU)], [(m2conv_f.shape[0], width, _cj), (1, width, _cj)], t_rows, width // LANE, chunk=conv_chunk)

    dp_xs, gcw_xs, gcb_xs = m2_prep_bwd_call("m2_prep_bwd_x", 0, d, [dxs, dxs_skip])
    dp_b, gcw_b, gcb_b = m2_prep_bwd_call("m2_prep_bwd_b", d, M2G * NST, [db_ssd])
    dp_c, gcw_c, gcb_c = m2_prep_bwd_call("m2_prep_bwd_c", d + M2G * NST, M2G * NST, [dc_ssd])
    d_pxbc = jnp.concatenate([dp_xs, dp_b, dp_c], axis=1)
    gw_in("xbc", d_pxbc)
    g_m2_conv = jnp.concatenate([gcw_xs, gcw_b, gcw_c], axis=1)
    g_m2_conv_b = jnp.concatenate([gcb_xs, gcb_b, gcb_c], axis=1)

    def dn_out_bwd(rows, j, o, z, w, dy):
        _, vjp = jax.vjp(dn_out, o, z, w)
        return vjp(dy)

    d_o, d_z, g_dn_norm = _rw(
        "dn_out_bwd", dn_out_bwd,
        [("r", o_dn, d, _c0), ("r", proj["z"], d, _c0), ("p", dn_norm_w, HD, _c0), ("r", d_mixed, d, _c0)],
        [("r", d, d, _c0, F32), ("r", d, d, _c0, MXU), ("p", 1, HD, HD, _c0)], t_rows, _pick(t_rows, 208, 16))
    gw_in("z", d_z)

    dq, dk, dv, dbeta_r, dgdec_r = _gdn_bwd(q_act, k_act, v_act, head_rows(g1[:, :dnh], hb_b),
                                            head_rows(g1[:, dnh:2 * dnh], hb_b), dn_states, dn_tinv, d_o, hb_b)

    def dn_prep_bwd(sec, name, dact):
        def fn(rows, own, last, wins, pars):
            (p, da), (w,) = wins, pars
            _, vjp = jax.vjp(functools.partial(dn_post, sec), _conv(p, w))
            dcv, = vjp(jnp.where(valid(rows), da, 0.0))
            return [_conv_t(dcv, w, rows, t_end(last))], [_conv_w(jnp.where(own, dcv, 0.0), p, len(w))]
        wc = dnconv_f[:, sec * d:(sec + 1) * d]
        return _cv("dn_prep_bwd_" + name, fn, [(proj[name], _cj), (dact, _cj)], [(wc, _cj)], [(d, _cj, MXU)],
                   [(wc.shape[0], d, _cj)], t_rows, dnh, chunk=conv_chunk if sec == 2 else None)

    (dp_q, gcw_q), (dp_k, gcw_k), (dp_v, gcw_v) = dn_prep_bwd(0, "q", dq), dn_prep_bwd(1, "k", dk), dn_prep_bwd(2, "v", dv)
    gw_in("q", dp_q), gw_in("k", dp_k), gw_in("v", dp_v)
    g_dn_conv = jnp.concatenate([gcw_q, gcw_k, gcw_v], axis=1)

    zpad = jnp.zeros((t_rows, LANE - 2 * dnh - m2h), F32)
    dg1 = jnp.concatenate([head_cols(dbeta_r), head_cols(dgdec_r), head_cols(ddt_r), zpad], axis=1)
    dg2 = jnp.concatenate([jnp.zeros((t_rows, 2 * dnh), F32), head_cols(dam_r), zpad], axis=1)

    def gates_bwd(rows, j, sm, pa, pb, pc, pd, d1, d2):
        _, vjp = jax.vjp(lambda *a: gates(rows, *a), sm, pa, pb, pc, pd)
        return vjp((d1, d2))

    dp_sm, g_pa, g_pb, g_pc, g_pd = _rw(
        "gates_bwd", gates_bwd, gate_ins + [("r", dg1, LANE, _c0), ("r", dg2, LANE, _c0)],
        [("r", LANE, LANE, _c0, MXU)] + [("p", 1, LANE, LANE, _c0)] * 4, t_rows, tm_rw)

    dseg = {"q": dp_q, "k": dp_k, "v": dp_v, "z": d_z, "m2z": d_m2z, "xbc": d_pxbc, "sm": dp_sm}
    gw_in("sm", dp_sm)
    gsm = gw_seg["sm"]
    gw_in_full = jnp.concatenate([gw_seg["q"], gw_seg["k"], gw_seg["v"], gw_seg["z"], gsm[:, :2 * dnh], gw_seg["m2z"],
                                  gw_seg["xbc"], gsm[:, 2 * dnh:2 * dnh + m2h]], axis=1)
    x_in = _exchange_start("grad_in_start", [], [_to_shards(gw_in_full, 1).astype(WIRE)], gw_in_full)
    d_hn1 = None
    for s in dseg:
        d_hn1 = _mm("d_hn1_" + s, dseg[s], w_all_t, b_rows=seg_cols[s], tk=2048, add=d_hn1, dep=x_in[3])
    dh0, _, g_norm_mix = norm_bwd("norm_mix_bwd", h0, norm_mix_w, d_hn1, dh1)

    g_ffn_conv = jnp.concatenate([g_fc_g, g_fc_v], axis=1)
    small_parts = [_to_shards(dh0[PADR:CH], 1), _to_shards(g_dn_conv, 1), _to_shards(g_m2_conv, 1), _to_shards(g_ffn_conv, 1)]
    small_scatter = jnp.stack([_pack([p[k] for p in small_parts]) for k in range(NDEV)])

    rep_names = ["norm_mix_w", "dn_a_log", "dn_dt_bias", "dn_norm_w", "m2_conv_b", "m2_a_log", "m2_dt_bias", "m2_d",
                 "m2_norm_w", "norm_ffn_w", "norm_final_w"]
    rep_grads = [g_norm_mix, g_pa[:, dnh:2 * dnh], g_pb[:, dnh:2 * dnh], g_dn_norm, g_m2_conv_b, g_pc[:, 2 * dnh:2 * dnh + m2h],
                 g_pd[:, 2 * dnh:2 * dnh + m2h], g_m2_d[:, :m2h], g_m2_norm, g_norm_ffn, d_wf.reshape(d)]

    weights = dict(meta_tokens=meta_tokens, norm_mix_w=norm_mix_w, w_in=w_in, dn_conv_w=dn_conv_w, dn_a_log=dn_a_log,
                   dn_dt_bias=dn_dt_bias, dn_norm_w=dn_norm_w, m2_conv_w=m2_conv_w, m2_conv_b=m2_conv_b, m2_a_log=m2_a_log,
                   m2_dt_bias=m2_dt_bias, m2_d=m2_d, m2_norm_w=m2_norm_w, w_out=w_out, norm_ffn_w=norm_ffn_w, ffn_up=ffn_up,
                   ffn_conv_w=ffn_conv_w, ffn_down=ffn_down, norm_final_w=norm_final_w)
    mom1 = dict(meta_tokens=m_meta_tokens, norm_mix_w=m_norm_mix_w, w_in=m_w_in, dn_conv_w=m_dn_conv_w, dn_a_log=m_dn_a_log,
                dn_dt_bias=m_dn_dt_bias, dn_norm_w=m_dn_norm_w, m2_conv_w=m_m2_conv_w, m2_conv_b=m_m2_conv_b,
                m2_a_log=m_m2_a_log, m2_dt_bias=m_m2_dt_bias, m2_d=m_m2_d, m2_norm_w=m_m2_norm_w, w_out=m_w_out,
                norm_ffn_w=m_norm_ffn_w, ffn_up=m_ffn_up, ffn_conv_w=m_ffn_conv_w, ffn_down=m_ffn_down,
                norm_final_w=m_norm_final_w)
    mom2 = dict(meta_tokens=v_meta_tokens, norm_mix_w=v_norm_mix_w, w_in=v_w_in, dn_conv_w=v_dn_conv_w, dn_a_log=v_dn_a_log,
                dn_dt_bias=v_dn_dt_bias, dn_norm_w=v_dn_norm_w, m2_conv_w=v_m2_conv_w, m2_conv_b=v_m2_conv_b,
                m2_a_log=v_m2_a_log, m2_dt_bias=v_m2_dt_bias, m2_d=v_m2_d, m2_norm_w=v_m2_norm_w, w_out=v_w_out,
                norm_ffn_w=v_norm_ffn_w, ffn_up=v_ffn_up, ffn_conv_w=v_ffn_conv_w, ffn_down=v_ffn_down,
                norm_final_w=v_norm_final_w)
    res = {}

    def adam_big(name, started, after):
        staged, = _exchange_wait("grad_" + name + "_wait", started, after)
        outs = _adamw("adamw_" + name, staged, weights[name][0], mom1[name][0], mom2[name][0])
        res[name] = tuple(o[None] for o in outs)
        return outs[1]

    done = adam_big("ffn_down", x_down, dh0)
    done = adam_big("ffn_up", x_up, done)
    done = adam_big("w_out", x_out, done)
    st_rep, st_small = _exchange("exchange_small_grads", [_pack(rep_grads)], [small_scatter], done)
    adam_big("w_in", x_in, st_small)

    def adam_packed(label, staged, names):
        shapes = [weights[nm].shape for nm in names]
        outs = _adamw(label, staged, *[_pack([src[nm] for nm in names]) for src in (weights, mom1, mom2)])
        unpacked = [_unpack(o, shapes) for o in outs]
        for i, nm in enumerate(names):
            res[nm] = tuple(u[i] for u in unpacked)

    adam_packed("adamw_small_sharded", st_small, ["meta_tokens", "dn_conv_w", "m2_conv_w", "ffn_conv_w"])
    adam_packed("adamw_replicated", st_rep, rep_names)

    order = list(weights)
    grad_x = dh0[CH:][None]
    return (loss, grad_x, *[res[nm][0] for nm in order], *[res[nm][1] for nm in order], *[res[nm][2] for nm in order],
            *[res[nm][3] for nm in order])
```

```python
import functools
import math

import jax
import jax.numpy as jnp
from jax import lax
from jax.experimental import pallas as pl
from jax.experimental.pallas import tpu as pltpu

F32 = jnp.float32
MXU = jnp.bfloat16
WIRE = jnp.bfloat16
HI = lax.Precision.HIGH

NDEV = 8
CH = 64
NMETA = 16
PADR = CH - NMETA
EPS = 1e-6
HD = 128
M2P = 64
M2G = 4
SSD_GPS_F, SSD_GPS_B = 4, 1
NST = 128
LANE = 128

ADAM_LR, ADAM_B1, ADAM_B2, ADAM_EPS, ADAM_WD, ADAM_STEP = 0.001, 0.9, 0.999, 1e-08, 0.01, 10

MESH_AXES = ("x", "y", "c")


def _pick(n, target, mult=16):
    best = None
    for t in range(mult, min(n, target) + 1, mult):
        if n % t == 0:
            best = t
    return best if best is not None else n


def _dg(a, b, ca, cb):
    return lax.dot_general(a.astype(MXU), b.astype(MXU), (((ca,), (cb,)), ((), ())), preferred_element_type=F32)


def _dgh(a, b, ca, cb):
    return lax.dot_general(a, b, (((ca,), (cb,)), ((), ())), precision=HI, preferred_element_type=F32)


def _silu(x):
    return x * jax.nn.sigmoid(x)


def _softplus(x):
    return jnp.maximum(x, 0.0) + jnp.log1p(jnp.exp(-jnp.abs(x)))


def _rms(x, w):
    return x * lax.rsqrt(jnp.mean(x * x, axis=-1, keepdims=True) + EPS) * w


def _cparams(sem, vmem_mb):
    return pltpu.CompilerParams(dimension_semantics=sem, vmem_limit_bytes=vmem_mb << 20)


def _mm(name, a, b, *, ta=False, tb=False, add=None, out_dtype=F32, tm=1040, tn=1024, tk=2048, dep=None,
        b_rows=None, b_cols=None, out_cols=None, into=None):
    m = a.shape[1] if ta else a.shape[0]
    (rs, rw), (cs, cw) = b_rows or (0, b.shape[0]), b_cols or (0, b.shape[1])
    (n, ns), (kdim, ks) = ((rw, rs), (cw, cs)) if tb else ((cw, cs), (rw, rs))
    assert kdim == (a.shape[0] if ta else a.shape[1])
    tm = _pick(m, tm, 128 if ta else 16)
    tn = _pick(math.gcd(n, ns), tn, 128)
    tk = _pick(math.gcd(kdim, ks), tk, 16 if (ta and not tb) else 128)
    nk = kdim // tk
    bj0, bk0 = ns // tn, ks // tk
    ca, cb = (0 if ta else 1), (1 if tb else 0)

    def body(*refs):
        a_ref, b_ref = refs[0], refs[1]
        add_ref = refs[2] if add is not None else None
        if nk == 1:
            r = _dg(a_ref[...], b_ref[...], ca, cb)
            if add_ref is not None:
                r = r + add_ref[...].astype(F32)
            refs[-1][...] = r.astype(refs[-1].dtype)
            return
        o_ref, acc = refs[-2], refs[-1]
        k = pl.program_id(2)

        @pl.when(k == 0)
        def _():
            acc[...] = jnp.zeros_like(acc)

        acc[...] += _dg(a_ref[...], b_ref[...], ca, cb)

        @pl.when(k == nk - 1)
        def _():
            r = acc[...]
            if add_ref is not None:
                r = r + add_ref[...].astype(F32)
            o_ref[...] = r.astype(o_ref.dtype)

    a_spec = pl.BlockSpec((tk, tm), lambda i, j, k: (k, i)) if ta else pl.BlockSpec((tm, tk), lambda i, j, k: (i, k))
    b_spec = (pl.BlockSpec((tn, tk), lambda i, j, k: (j + bj0, k + bk0)) if tb
              else pl.BlockSpec((tk, tn), lambda i, j, k: (k + bk0, j + bj0)))
    in_specs, ops = [a_spec, b_spec], [a, b]
    if add is not None:
        in_specs.append(pl.BlockSpec((tm, tn), lambda i, j, k: (i, j)))
        ops.append(add)
    if dep is not None:
        in_specs.append(pl.BlockSpec((8, LANE), lambda i, j, k: (0, 0)))
        ops.append(dep)
    oj0, n_out, aliases = 0, n, {}
    if out_cols is not None:
        assert out_cols[0] % tn == 0
        oj0, n_out = out_cols[0] // tn, out_cols[1]
    if into is not None:
        in_specs.append(pl.BlockSpec(memory_space=pl.ANY))
        ops.append(into)
        aliases = {len(ops) - 1: 0}
    return pl.pallas_call(
        body, name=name, grid=(m // tm, n // tn, nk), in_specs=in_specs,
        out_specs=pl.BlockSpec((tm, tn), lambda i, j, k: (i, j + oj0)),
        out_shape=jax.ShapeDtypeStruct((m, n_out), out_dtype),
        scratch_shapes=[pltpu.VMEM((tm, tn), F32)] if nk > 1 else [],
        input_output_aliases=aliases,
        compiler_params=_cparams(("parallel", "parallel", "arbitrary"), 48),
    )(*ops)


def _rw(name, fn, ins, outs, nrows, tm, ncol=1, vmem_mb=48):
    nrow = nrows // tm
    sub = tm
    in_specs, ops = [], []
    for spec in ins:
        kind, arr, bw, cj = spec[:4]
        ops.append(arr)
        if kind == "r":
            ri = spec[4] if len(spec) > 4 else (lambda i: i)
            in_specs.append(pl.BlockSpec((tm, bw), lambda j, i, cj=cj, ri=ri: (ri(i), cj(j))))
        else:
            in_specs.append(pl.BlockSpec((arr.shape[0], bw), lambda j, i, cj=cj: (0, cj(j))))
    out_shape, out_specs = [], []
    for o in outs:
        if o[0] == "r":
            _, width, bw, cj, dt = o
            out_shape.append(jax.ShapeDtypeStruct((nrows, width), dt))
            out_specs.append(pl.BlockSpec((tm, bw), lambda j, i, cj=cj: (i, cj(j))))
        else:
            _, rows, width, bw, cj = o
            out_shape.append(jax.ShapeDtypeStruct((rows, width), F32))
            out_specs.append(pl.BlockSpec((rows, bw), lambda j, i, cj=cj: (0, cj(j))))
    n_in = len(ins)

    def body(*refs):
        j, i = pl.program_id(0), pl.program_id(1)
        in_refs, out_refs = refs[:n_in], refs[n_in:]
        pars = [ref[...] if spec[0] == "p" else None for spec, ref in zip(ins, in_refs)]

        def one(r0, nr):
            rows = i * tm + r0 + lax.broadcasted_iota(jnp.int32, (nr, 1), 0)
            vals = [par if spec[0] == "p" else ref[pl.ds(r0, nr), :] for spec, ref, par in zip(ins, in_refs, pars)]
            parts = []
            for o, val, ref in zip(outs, fn(rows, j, *vals), out_refs):
                if o[0] == "r":
                    ref[pl.ds(r0, nr), :] = val.astype(ref.dtype)
                else:
                    parts.append(val)
            return parts

        if sub >= tm:
            parts = one(0, tm)
        else:
            zero = [jnp.zeros((1, o[3]), F32) for o in outs if o[0] == "p"]
            parts = lax.fori_loop(
                0, tm // sub, lambda s, acc: [a + b for a, b in zip(acc, one(pl.multiple_of(s * sub, sub), sub))], zero)
        for ref, val in zip([r for o, r in zip(outs, out_refs) if o[0] == "p"], parts):
            @pl.when(i == 0)
            def _(ref=ref):
                ref[...] = jnp.zeros_like(ref)

            ref[...] += val

    return pl.pallas_call(
        body, name=name, grid=(ncol, nrow), in_specs=in_specs, out_specs=out_specs, out_shape=out_shape,
        compiler_params=_cparams(("parallel", "arbitrary"), vmem_mb),
    )(*ops)


def _c0(j):
    return 0


def _cj(j):
    return j


def _shift(x, s):
    if s == 0:
        return x
    return pltpu.roll(x, s % x.shape[0], 0)


def _conv(x, w):
    k = len(w)
    return functools.reduce(lambda a, b: a + b, [w[j] * _shift(x, k - 1 - j) for j in range(k)])


def _conv_t(dy, w, rows, t_end):
    k = len(w)
    terms = []
    for j in range(k):
        s = k - 1 - j
        v = _shift(dy, -s)
        if t_end is not None and s > 0:
            v = jnp.where(rows + s < t_end, v, 0.0)
        terms.append(w[j] * v)
    return functools.reduce(lambda a, b: a + b, terms)


def _conv_w(dy, x, k):
    return [jnp.sum(dy * _shift(x, k - 1 - j), axis=0, keepdims=True) for j in range(k)]


CONV_CHUNK = 320
HALO = 8


def _cv(name, fn, row_ins, par_ins, row_outs, par_outs, nrows, ncol, chunk=None):
    whole = chunk is None
    chunk = nrows if whole else chunk
    n_chunks = nrows // chunk
    assert nrows % chunk == 0 and (whole or n_chunks >= 3)
    n_ri, n_pi, n_ro = len(row_ins), len(par_ins), len(row_outs)

    def body(*refs):
        rin, pin = refs[:n_ri], refs[n_ri:n_ri + n_pi]
        rout, pout = refs[n_ri + n_pi:n_ri + n_pi + n_ro], refs[n_ri + n_pi + n_ro:]
        pars = [[p[pl.ds(r, 1), :] for r in range(p.shape[0])] for p in pin]

        def run(r0, top, bot, last):
            wlen = top + chunk + bot
            w0 = r0 - top if isinstance(r0, int) else pl.multiple_of(r0 - top, HALO)
            local = lax.broadcasted_iota(jnp.int32, (wlen, 1), 0)
            own = jnp.logical_and(local >= top, local < top + chunk)
            outs, parts = fn(w0 + local, own, last, [ref[pl.ds(w0, wlen), :] for ref in rin], pars)
            for ref, val in zip(rout, outs):
                ref[pl.ds(r0, chunk), :] = val[top:top + chunk].astype(ref.dtype)
            return parts

        def add(acc, parts):
            return [[a + b for a, b in zip(ra, rb)] for ra, rb in zip(acc, parts)]

        if whole:
            acc = run(0, 0, 0, True)
        else:
            acc = run(0, 0, HALO, False)
            acc = lax.fori_loop(1, n_chunks - 1,
                                lambda i, a: add(a, run(pl.multiple_of(i * chunk, chunk), HALO, HALO, False)), acc)
            acc = add(acc, run(nrows - chunk, HALO, 0, True))
        for ref, prow in zip(pout, acc):
            for r, v in enumerate(prow):
                ref[pl.ds(r, 1), :] = v

    in_specs = [pl.BlockSpec((nrows, LANE), lambda j, cj=cj: (0, cj(j))) for _, cj in row_ins]
    in_specs += [pl.BlockSpec((a.shape[0], LANE), lambda j, cj=cj: (0, cj(j))) for a, cj in par_ins]
    out_specs = [pl.BlockSpec((nrows, LANE), lambda j, cj=cj: (0, cj(j))) for _, cj, _ in row_outs]
    out_specs += [pl.BlockSpec((k, LANE), lambda j, cj=cj: (0, cj(j))) for k, _, cj in par_outs]
    out_shape = [jax.ShapeDtypeStruct((nrows, width), dt) for width, _, dt in row_outs]
    out_shape += [jax.ShapeDtypeStruct((k, width), F32) for k, width, _ in par_outs]
    return pl.pallas_call(
        body, name=name, grid=(ncol,), in_specs=in_specs, out_specs=out_specs, out_shape=out_shape,
        compiler_params=_cparams(("parallel",), 48),
    )(*[a for a, _ in row_ins], *[a for a, _ in par_ins])


def _tri():
    r = lax.broadcasted_iota(jnp.int32, (CH, CH), 0)
    c = lax.broadcasted_iota(jnp.int32, (CH, CH), 1)
    return r, c


def _col(row):
    r, c = _tri()
    return jnp.sum(jnp.where(r == c, row, 0.0), axis=1, keepdims=True)


def _cumsum_rc(g_r):
    r, c = _tri()
    g_c = _col(g_r)
    cs_r = jnp.sum(jnp.where(r <= c, g_c, 0.0), axis=0, keepdims=True)
    cs_c = jnp.sum(jnp.where(c <= r, g_r, 0.0), axis=1, keepdims=True)
    return cs_r, cs_c


def _decay(cs_r, cs_c):
    r, c = _tri()
    return jnp.exp(jnp.where(c <= r, cs_c - cs_r, -jnp.inf))


def _gdn_a(ks, betas, gs):
    r, c = _tri()
    cs = [_cumsum_rc(g) for g in gs]
    kk = [_dg(k, k, 1, 1) for k in ks]
    return [jnp.where(c < r, _col(b) * kki * _decay(*csi), 0.0) for b, kki, csi in zip(betas, kk, cs)]


def _neumann(a_list):
    r, c = _tri()
    xs = [jnp.where(r == c, 1.0, 0.0) - a for a in a_list]
    ps = list(a_list)
    n = 2
    while n < CH:
        ps = [_dgh(p, p, 1, 0) for p in ps]
        xs = [x + _dgh(x, p, 1, 0) for x, p in zip(xs, ps)]
        n *= 2
    return xs


def _gdn_rest(ss, qs, ks, vs, betas, gs, ts):
    n = range(len(ss))
    cs = [_cumsum_rc(g) for g in gs]
    dm = [_decay(*csi) for csi in cs]
    ecs = [jnp.exp(csi[1]) for csi in cs]
    bc = [_col(b) for b in betas]
    u = [_dgh(ts[i], vs[i] * bc[i], 1, 0) for i in n]
    w = [_dgh(ts[i], ks[i] * (bc[i] * ecs[i]), 1, 0) for i in n]
    ws = [_dg(w[i], ss[i], 1, 0) for i in n]
    v_new = [u[i] - ws[i] for i in n]
    qk = [_dg(qs[i], ks[i], 1, 1) * dm[i] for i in n]
    o_in = [_dg(qs[i] * ecs[i], ss[i], 1, 0) for i in n]
    o = [o_in[i] + _dg(qk[i], v_new[i], 1, 0) for i in n]
    g_last = [jnp.sum(g, axis=1, keepdims=True) for g in gs]
    s_new = [ss[i] * jnp.exp(g_last[i]) + _dg(ks[i] * jnp.exp(g_last[i] - cs[i][1]), v_new[i], 0, 0) for i in n]
    return s_new, o


def _gdn_fwd(q, k, v, beta, g, hb):
    t_rows, d = q.shape
    nc, ng, w = t_rows // CH, d // (HD * hb), HD * hb
    sls = [slice(h * HD, (h + 1) * HD) for h in range(hb)]

    def body(q_ref, k_ref, v_ref, b_ref, g_ref, o_ref, ss_ref, ts_ref, s_scr):
        c = pl.program_id(1)

        @pl.when(c == 0)
        def _():
            s_scr[...] = jnp.zeros_like(s_scr)

        qs, ks, vs = ([ref[:, sl] for sl in sls] for ref in (q_ref, k_ref, v_ref))
        br = [b_ref[0, 0, pl.ds(h, 1), :] for h in range(hb)]
        gr = [g_ref[0, 0, pl.ds(h, 1), :] for h in range(hb)]
        s0 = [s_scr[h] for h in range(hb)]
        tm = _neumann(_gdn_a(ks, br, gr))
        s1, o = _gdn_rest(s0, qs, ks, vs, br, gr, tm)
        for h in range(hb):
            ss_ref[0, 0, h] = s0[h]
            ts_ref[0, 0, h] = tm[h]
            o_ref[:, sls[h]] = o[h]
            s_scr[h] = s1[h]

    blk = pl.BlockSpec((CH, w), lambda n, c: (c, n))
    row = pl.BlockSpec((1, 1, hb, CH), lambda n, c: (n, c, 0, 0))
    return pl.pallas_call(
        body, name="gdn_fwd", grid=(ng, nc), in_specs=[blk, blk, blk, row, row],
        out_specs=[blk, pl.BlockSpec((1, 1, hb, HD, HD), lambda n, c: (n, c, 0, 0, 0)),
                   pl.BlockSpec((1, 1, hb, CH, CH), lambda n, c: (n, c, 0, 0, 0))],
        out_shape=[jax.ShapeDtypeStruct((t_rows, d), F32), jax.ShapeDtypeStruct((ng, nc, hb, HD, HD), F32),
                   jax.ShapeDtypeStruct((ng, nc, hb, CH, CH), F32)],
        scratch_shapes=[pltpu.VMEM((hb, HD, HD), F32)],
        compiler_params=_cparams(("parallel", "arbitrary"), 32),
    )(q, k, v, beta, g)


def _gdn_bwd(q, k, v, beta, g, ss, ts, do, hb):
    t_rows, d = q.shape
    nc, ng, w = t_rows // CH, d // (HD * hb), HD * hb
    per_f = ss.shape[2] // hb
    sls = [slice(h * HD, (h + 1) * HD) for h in range(hb)]

    def body(q_ref, k_ref, v_ref, b_ref, g_ref, ss_ref, ts_ref, do_ref, dq_ref, dk_ref, dv_ref, db_ref, dg_ref, ds_scr):
        cr = pl.program_id(1)

        @pl.when(cr == 0)
        def _():
            ds_scr[...] = jnp.zeros_like(ds_scr)

        first = cr == nc - 1
        rowi = lax.broadcasted_iota(jnp.int32, (CH, 1), 0)
        lani = lax.broadcasted_iota(jnp.int32, (1, CH), 1)
        keep_c = jnp.logical_or(jnp.logical_not(first), rowi >= PADR)
        keep_r = jnp.logical_or(jnp.logical_not(first), lani >= PADR)
        hs = range(hb)
        qs, ks, vs, dos = ([ref[:, sl] for sl in sls] for ref in (q_ref, k_ref, v_ref, do_ref))
        br = [b_ref[0, 0, pl.ds(h, 1), :] for h in hs]
        gr = [g_ref[0, 0, pl.ds(h, 1), :] for h in hs]
        tm = [ts_ref[0, 0, h] for h in hs]
        _, vjp_rest = jax.vjp(_gdn_rest, [ss_ref[0, 0, h] for h in hs], qs, ks, vs, br, gr, tm)
        ds0, dq, dk, dv, db, dg, dt = vjp_rest(([ds_scr[h] for h in hs], dos))
        dtt = [_dgh(dt[h], tm[h], 1, 1) for h in hs]
        da = [-_dgh(tm[h], dtt[h], 0, 0) for h in hs]
        _, vjp_a = jax.vjp(_gdn_a, ks, br, gr)
        dk2, db2, dg2 = vjp_a(da)
        for h in hs:
            ds_scr[h] = ds0[h]
            dq_ref[:, sls[h]] = jnp.where(keep_c, dq[h], 0.0)
            dk_ref[:, sls[h]] = jnp.where(keep_c, dk[h] + dk2[h], 0.0)
            dv_ref[:, sls[h]] = jnp.where(keep_c, dv[h], 0.0)
            db_ref[0, 0, pl.ds(h, 1), :] = jnp.where(keep_r, db[h] + db2[h], 0.0)
            dg_ref[0, 0, pl.ds(h, 1), :] = jnp.where(keep_r, dg[h] + dg2[h], 0.0)

    blk = pl.BlockSpec((CH, w), lambda n, c: (nc - 1 - c, n))
    row = pl.BlockSpec((1, 1, hb, CH), lambda n, c: (n, nc - 1 - c, 0, 0))
    return pl.pallas_call(
        body, name="gdn_bwd", grid=(ng, nc),
        in_specs=[blk, blk, blk, row, row,
                  pl.BlockSpec((1, 1, hb, HD, HD), lambda n, c: (n // per_f, nc - 1 - c, n % per_f, 0, 0)),
                  pl.BlockSpec((1, 1, hb, CH, CH), lambda n, c: (n // per_f, nc - 1 - c, n % per_f, 0, 0)), blk],
        out_specs=[blk, blk, blk, row, row],
        out_shape=[jax.ShapeDtypeStruct((t_rows, d), F32)] * 3 + [jax.ShapeDtypeStruct((ng, nc, hb, CH), F32)] * 2,
        scratch_shapes=[pltpu.VMEM((hb, HD, HD), F32)],
        compiler_params=_cparams(("parallel", "arbitrary"), 32),
    )(q, k, v, beta, g, ss, ts, do)


def _ssd_group(s, xs, bm, cm, dt_r, a_r):
    prs = range(len(s))
    ngrp = bm.shape[1] // NST
    grp = [p // (len(s) // ngrp) for p in prs]
    bms = [bm[:, g * NST:(g + 1) * NST] for g in range(ngrp)]
    cms = [cm[:, g * NST:(g + 1) * NST] for g in range(ngrp)]
    first = lax.broadcasted_iota(jnp.int32, (1, 2 * M2P), 1) < M2P

    def pick(vals, p):
        return jnp.where(first, vals[2 * p], vals[2 * p + 1])

    cs = [_cumsum_rc(a) for a in a_r]
    lm = [_decay(*csi) for csi in cs]
    ecs = [jnp.exp(csi[1]) for csi in cs]
    alast = [jnp.sum(a, axis=1, keepdims=True) for a in a_r]
    ealast = [jnp.exp(al) for al in alast]
    wt = [jnp.exp(al - csi[1]) for al, csi in zip(alast, cs)]
    dtc = [_col(t) for t in dt_r]
    xdt = [xs[:, p * LANE:(p + 1) * LANE] * pick(dtc, p) for p in prs]
    cb = [_dg(cms[g], bms[g], 1, 1) for g in range(ngrp)]
    y0 = [_dg(cb[grp[p]] * lm[2 * p], xdt[p], 1, 0) for p in prs]
    y1 = [_dg(cb[grp[p]] * lm[2 * p + 1], xdt[p], 1, 0) for p in prs]
    yo = [_dg(cms[grp[p]], s[p], 1, 0) for p in prs]
    y = [jnp.where(first, y0[p], y1[p]) + yo[p] * pick(ecs, p) for p in prs]
    s_new = [s[p] * pick(ealast, p) + _dg(bms[grp[p]], xdt[p] * pick(wt, p), 0, 0) for p in prs]
    return s_new, jnp.concatenate(y, axis=1)


def _ssd_specs(nc, d, rev, gps):
    assert M2G % gps == 0 and (d // LANE) % gps == 0
    hps = (d // M2P) // M2G * gps
    cc = (lambda c: nc - 1 - c) if rev else (lambda c: c)
    xs = pl.BlockSpec((CH, hps * M2P), lambda g, c: (cc(c), g))
    bm = pl.BlockSpec((CH, NST * gps), lambda g, c: (cc(c), (d // LANE) // gps + g))
    cm = pl.BlockSpec((CH, NST * gps), lambda g, c: (cc(c), (d // LANE + M2G) // gps + g))
    row = pl.BlockSpec((1, 1, hps, CH), lambda g, c: (g, cc(c), 0, 0))
    return xs, bm, cm, row, hps


def _ssd_fwd(xbc, dt, a, d, gps):
    t_rows = xbc.shape[0]
    nc = t_rows // CH
    xs, bm, cm, row, hpg = _ssd_specs(nc, d, False, gps)
    ppg = hpg // 2
    st = pl.BlockSpec((1, 1, ppg, NST, LANE), lambda g, c: (g, c, 0, 0, 0))

    def body(xs_ref, b_ref, c_ref, dt_ref, a_ref, y_ref, ss_ref, s_scr):
        c = pl.program_id(1)

        @pl.when(c == 0)
        def _():
            s_scr[...] = jnp.zeros_like(s_scr)

        s0 = [s_scr[p] for p in range(ppg)]
        for p in range(ppg):
            ss_ref[0, 0, p] = s0[p]
        dt_r = [dt_ref[0, 0, pl.ds(h, 1), :] for h in range(hpg)]
        a_r = [a_ref[0, 0, pl.ds(h, 1), :] for h in range(hpg)]
        s1, y = _ssd_group(s0, xs_ref[...], b_ref[...], c_ref[...], dt_r, a_r)
        y_ref[...] = y
        for p in range(ppg):
            s_scr[p] = s1[p]

    return pl.pallas_call(
        body, name="ssd_fwd", grid=(M2G // gps, nc), in_specs=[xs, bm, cm, row, row], out_specs=[xs, st],
        out_shape=[jax.ShapeDtypeStruct((t_rows, d), F32), jax.ShapeDtypeStruct((M2G // gps, nc, ppg, NST, LANE), F32)],
        scratch_shapes=[pltpu.VMEM((ppg, NST, LANE), F32)],
        compiler_params=_cparams(("parallel", "arbitrary"), 32),
    )(xbc, xbc, xbc, dt, a)


def _ssd_bwd(xbc, dt, a, ss, dy, d, gps):
    t_rows = xbc.shape[0]
    nc = t_rows // CH
    xs, bm, cm, row, hpg = _ssd_specs(nc, d, True, gps)
    ppg = hpg // 2
    per_f = ss.shape[2] // ppg
    st = pl.BlockSpec((1, 1, ppg, NST, LANE), lambda g, c: (g // per_f, nc - 1 - c, g % per_f, 0, 0))

    def body(xs_ref, b_ref, c_ref, dt_ref, a_ref, ss_ref, dy_ref, dxs_ref, db_ref, dc_ref, ddt_ref, da_ref, ds_scr):
        cr = pl.program_id(1)

        @pl.when(cr == 0)
        def _():
            ds_scr[...] = jnp.zeros_like(ds_scr)

        first = cr == nc - 1
        keep_c = jnp.logical_or(jnp.logical_not(first), lax.broadcasted_iota(jnp.int32, (CH, 1), 0) >= PADR)
        keep_r = jnp.logical_or(jnp.logical_not(first), lax.broadcasted_iota(jnp.int32, (1, CH), 1) >= PADR)
        dt_r = [dt_ref[0, 0, pl.ds(h, 1), :] for h in range(hpg)]
        a_r = [a_ref[0, 0, pl.ds(h, 1), :] for h in range(hpg)]
        s0 = [ss_ref[0, 0, p] for p in range(ppg)]
        _, vjp = jax.vjp(_ssd_group, s0, xs_ref[...], b_ref[...], c_ref[...], dt_r, a_r)
        ds0, dxs, db, dc, ddt, da = vjp(([ds_scr[p] for p in range(ppg)], dy_ref[...]))
        for p in range(ppg):
            ds_scr[p] = ds0[p]
        dxs_ref[...] = jnp.where(keep_c, dxs, 0.0)
        db_ref[...] = jnp.where(keep_c, db, 0.0)
        dc_ref[...] = jnp.where(keep_c, dc, 0.0)
        for h in range(hpg):
            ddt_ref[0, 0, pl.ds(h, 1), :] = jnp.where(keep_r, ddt[h], 0.0)
            da_ref[0, 0, pl.ds(h, 1), :] = jnp.where(keep_r, da[h], 0.0)

    grp = pl.BlockSpec((CH, NST * gps), lambda g, c: (nc - 1 - c, g))
    return pl.pallas_call(
        body, name="ssd_bwd", grid=(M2G // gps, nc), in_specs=[xs, bm, cm, row, row, st, xs],
        out_specs=[xs, grp, grp, row, row],
        out_shape=[jax.ShapeDtypeStruct((t_rows, d), F32)] + [jax.ShapeDtypeStruct((t_rows, M2G * NST), F32)] * 2
        + [jax.ShapeDtypeStruct((M2G // gps, nc, hpg, CH), F32)] * 2,
        scratch_shapes=[pltpu.VMEM((ppg, NST, LANE), F32)],
        compiler_params=_cparams(("parallel", "arbitrary"), 32),
    )(xbc, xbc, xbc, dt, a, ss, dy)


def _exchange(name, gathers, scatters, after):
    arrays = list(gathers) + list(scatters)
    n_g, n = len(gathers), len(arrays)

    def body(*refs):
        ins, outs = refs[:n], refs[n + 1:2 * n + 1]
        send_sems, recv_sems, local_sems = refs[2 * n + 1:]
        x, y, c = lax.axis_index("x"), lax.axis_index("y"), lax.axis_index("c")
        me = 4 * x + 2 * y + c

        def src(a, slot):
            return ins[a] if a < n_g else ins[a].at[slot]

        local = [pltpu.make_async_copy(src(a, me), outs[a].at[me], local_sems.at[a]) for a in range(n)]
        for cp in local:
            cp.start()
        copies = []
        for rel in range(1, NDEV):
            px, py, pc = x ^ (rel >> 2), y ^ ((rel >> 1) & 1), c ^ (rel & 1)
            peer = 4 * px + 2 * py + pc
            for a in range(n):
                copies.append(pltpu.make_async_remote_copy(
                    src_ref=src(a, peer), dst_ref=outs[a].at[me], send_sem=send_sems.at[a, rel - 1],
                    recv_sem=recv_sems.at[a, rel - 1], device_id=(px, py, pc), device_id_type=pl.DeviceIdType.MESH))
        for cp in copies:
            cp.start()
        for cp in copies:
            cp.wait_recv()
        for cp in copies:
            cp.wait_send()
        for cp in local:
            cp.wait()

    any_spec = pl.BlockSpec(memory_space=pl.ANY)
    out_shape = [jax.ShapeDtypeStruct((NDEV,) + a.shape, a.dtype) for a in gathers]
    out_shape += [jax.ShapeDtypeStruct(a.shape, a.dtype) for a in scatters]
    return pl.pallas_call(
        body, name=name, in_specs=[any_spec] * (n + 1), out_specs=[any_spec] * n, out_shape=out_shape,
        scratch_shapes=[pltpu.SemaphoreType.DMA((n, NDEV - 1)), pltpu.SemaphoreType.DMA((n, NDEV - 1)),
                        pltpu.SemaphoreType.DMA((n,))],
        compiler_params=pltpu.CompilerParams(has_side_effects=True),
    )(*arrays, after)


def _gather_two_level(name, arrays):
    n = len(arrays)

    def body(*refs):
        ins, outs = refs[:n], refs[n:2 * n]
        send_sems, recv_sems, local_sems = refs[2 * n:]
        x, y, c = lax.axis_index("x"), lax.axis_index("y"), lax.axis_index("c")
        me, sibling = (x, y, c), (x, y, 1 - c)
        chips = [(1 - x, y), (x, 1 - y), (1 - x, 1 - y)]

        def copy(a, k, block, to, src=None):
            dst = outs[a].at[4 * block[0] + 2 * block[1] + block[2]]
            return pltpu.make_async_remote_copy(
                src_ref=dst if src is None else src, dst_ref=dst, send_sem=send_sems.at[a, k], recv_sem=recv_sems.at[a, k],
                device_id=to, device_id_type=pl.DeviceIdType.MESH)

        mine = [pltpu.make_async_copy(ins[a], outs[a].at[4 * x + 2 * y + c], local_sems.at[a]) for a in range(n)]
        for cp in mine:
            cp.start()
        first = []
        for a in range(n):
            first.append(copy(a, 0, me, sibling, src=ins[a]))
            first += [copy(a, 1 + j, me, (*chip, c), src=ins[a]) for j, chip in enumerate(chips)]
        for cp in first:
            cp.start()
        passed = [[copy(a, 4 + j, (*chip, c), sibling) for j, chip in enumerate(chips)] for a in range(n)]
        for j, chip in enumerate(chips):
            for a in range(n):
                copy(a, 1 + j, (*chip, c), me).wait_recv()
                passed[a][j].start()
        for a in range(n):
            copy(a, 0, sibling, me).wait_recv()
            for j, chip in enumerate(chips):
                copy(a, 4 + j, (*chip, 1 - c), me).wait_recv()
        for cp in first + [cp for row in passed for cp in row]:
            cp.wait_send()
        for cp in mine:
            cp.wait()

    any_spec = pl.BlockSpec(memory_space=pl.ANY)
    return pl.pallas_call(
        body, name=name, in_specs=[any_spec] * n, out_specs=[any_spec] * n,
        out_shape=[jax.ShapeDtypeStruct((NDEV,) + a.shape, a.dtype) for a in arrays],
        scratch_shapes=[pltpu.SemaphoreType.DMA((n, NDEV - 1)), pltpu.SemaphoreType.DMA((n, NDEV - 1)),
                        pltpu.SemaphoreType.DMA((n,))],
        compiler_params=pltpu.CompilerParams(has_side_effects=True),
    )(*arrays)


_HBM = pl.BlockSpec(memory_space=pltpu.HBM)
_SEM = pl.BlockSpec(memory_space=pltpu.SEMAPHORE)
_EFFECT = pltpu.SideEffectType.DATAFLOW_SIDE_EFFECTING


def _split_copies(srcs, lands, send_sems, recv_sems, n_g):
    x, y, c = lax.axis_index("x"), lax.axis_index("y"), lax.axis_index("c")
    me = 4 * x + 2 * y + c
    copies = []
    for rel in range(1, NDEV):
        px, py, pc = x ^ (rel >> 2), y ^ ((rel >> 1) & 1), c ^ (rel & 1)
        peer = 4 * px + 2 * py + pc
        for a in range(len(srcs)):
            copies.append(pltpu.make_async_remote_copy(
                src_ref=srcs[a] if a < n_g else srcs[a].at[peer], dst_ref=lands[a].at[me],
                send_sem=send_sems.at[a * (NDEV - 1) + rel - 1], recv_sem=recv_sems.at[a * (NDEV - 1) + rel - 1],
                device_id=(px, py, pc), device_id_type=pl.DeviceIdType.MESH))
    return copies


def _exchange_start(name, gathers, scatters, after):
    arrays = list(gathers) + list(scatters)
    n_g, n = len(gathers), len(arrays)
    lands = [lax.empty((NDEV,) + a.shape, a.dtype) for a in gathers] + [lax.empty(a.shape, a.dtype) for a in scatters]

    def body(*refs):
        send_sems, recv_sems = refs[2 * n + 1], refs[2 * n + 2]
        for cp in _split_copies(refs[:n], refs[n:2 * n], send_sems, recv_sems, n_g):
            cp.start()
        refs[-1][...] = jnp.zeros_like(refs[-1])

    sems = pltpu.SemaphoreType.DMA((n * (NDEV - 1),))
    out = pl.pallas_call(
        body, name=name, in_specs=[_HBM] * (2 * n) + [pl.BlockSpec(memory_space=pl.ANY)],
        out_specs=(_SEM, _SEM, *[_HBM] * (2 * n), pl.BlockSpec(memory_space=pltpu.VMEM)),
        out_shape=(sems, sems, *[pltpu.HBM(a.shape, a.dtype) for a in arrays + lands], jax.ShapeDtypeStruct((8, LANE), F32)),
        input_output_aliases={i: 2 + i for i in range(2 * n)},
        compiler_params=pltpu.CompilerParams(has_side_effects=_EFFECT),
    )(*[pltpu.with_memory_space_constraint(a, pltpu.HBM) for a in arrays + lands], after)
    return out[0], out[1], list(out[2:2 + 2 * n]), out[-1], n_g


def _exchange_wait(name, started, after):
    send_sems, recv_sems, thru, _, n_g = started
    n = len(thru) // 2

    def body(*refs):
        for cp in _split_copies(refs[:n], refs[n:2 * n], refs[2 * n], refs[2 * n + 1], n_g):
            cp.wait_send()
            cp.wait_recv()

    out = pl.pallas_call(
        body, name=name, in_specs=[_HBM] * (2 * n) + [_SEM, _SEM, pl.BlockSpec(memory_space=pl.ANY)],
        out_specs=[_HBM] * (2 * n), out_shape=[pltpu.HBM(a.shape, a.dtype) for a in thru],
        input_output_aliases={i: i for i in range(2 * n)},
        compiler_params=pltpu.CompilerParams(has_side_effects=_EFFECT),
    )(*thru, send_sems, recv_sems, after)
    me = 4 * lax.axis_index("x") + 2 * lax.axis_index("y") + lax.axis_index("c")
    full = []
    for a in range(n):
        own = out[a][None] if a < n_g else lax.dynamic_index_in_dim(out[a], me, 0, keepdims=True)
        full.append(lax.dynamic_update_index_in_dim(out[n + a], own, me, 0))
    return full


def _adamw(name, staged, w, m, v):
    r, c = w.shape
    tr = _pick(r, 256, 8)

    def body(st_ref, w_ref, m_ref, v_ref, g_ref, d_ref, nm_ref, nv_ref):
        g = st_ref[0].astype(F32)
        for k in range(1, NDEV):
            g = g + st_ref[k].astype(F32)
        m_new = ADAM_B1 * m_ref[...] + (1.0 - ADAM_B1) * g
        v_new = ADAM_B2 * v_ref[...] + (1.0 - ADAM_B2) * jnp.square(g)
        m_hat = m_new / (1.0 - ADAM_B1 ** ADAM_STEP)
        v_hat = v_new / (1.0 - ADAM_B2 ** ADAM_STEP)
        g_ref[...] = g
        d_ref[...] = -ADAM_LR * (m_hat / (jnp.sqrt(v_hat) + ADAM_EPS) + ADAM_WD * w_ref[...])
        nm_ref[...] = m_new
        nv_ref[...] = v_new

    blk = pl.BlockSpec((tr, c), lambda i: (i, 0))
    return pl.pallas_call(
        body, name=name, grid=(r // tr,), in_specs=[pl.BlockSpec((NDEV, tr, c), lambda i: (0, i, 0)), blk, blk, blk],
        out_specs=[blk] * 4, out_shape=[jax.ShapeDtypeStruct((r, c), F32)] * 4,
        compiler_params=_cparams(("parallel",), 48),
    )(staged, w, m, v)


def _pack(parts):
    flat = jnp.concatenate([p.reshape(-1).astype(F32) for p in parts])
    pad = (-flat.shape[0]) % (8 * LANE)
    return jnp.pad(flat, (0, pad)).reshape(-1, LANE)


def _unpack(slab, shapes):
    flat, out, off = slab.reshape(-1), [], 0
    for s in shapes:
        n = 1
        for dim in s:
            n *= dim
        out.append(flat[off:off + n].reshape(s))
        off += n
    return out


def _to_shards(full, axis):
    shp = full.shape
    t = full.reshape(shp[:axis] + (NDEV, shp[axis] // NDEV) + shp[axis + 1:])
    return jnp.moveaxis(t, axis, 0)


def _from_shards(g, axis):
    t = jnp.moveaxis(g, 0, axis)
    shp = t.shape
    return t.reshape(shp[:axis] + (shp[axis] * shp[axis + 1],) + shp[axis + 2:])


def kernel(x, meta_tokens, norm_mix_w, w_in, dn_conv_w, dn_a_log, dn_dt_bias, dn_norm_w, m2_conv_w, m2_conv_b, m2_a_log, m2_dt_bias, m2_d, m2_norm_w, w_out, norm_ffn_w, ffn_up, ffn_conv_w, ffn_down, norm_final_w, loss_target, m_meta_tokens, m_norm_mix_w, m_w_in, m_dn_conv_w, m_dn_a_log, m_dn_dt_bias, m_dn_norm_w, m_m2_conv_w, m_m2_conv_b, m_m2_a_log, m_m2_dt_bias, m_m2_d, m_m2_norm_w, m_w_out, m_norm_ffn_w, m_ffn_up, m_ffn_conv_w, m_ffn_down, m_norm_final_w, v_meta_tokens, v_norm_mix_w, v_w_in, v_dn_conv_w, v_dn_a_log, v_dn_dt_bias, v_dn_norm_w, v_m2_conv_w, v_m2_conv_b, v_m2_a_log, v_m2_dt_bias, v_m2_d, v_m2_norm_w, v_w_out, v_norm_ffn_w, v_ffn_up, v_ffn_conv_w, v_ffn_down, v_norm_final_w):
    seq, d = x.shape[1], x.shape[2]
    t_rows = seq + CH
    nc = t_rows // CH
    dnh, m2h = d // HD, d // M2P
    dff = ffn_down.shape[1] * NDEV
    xbc_w = d + 2 * M2G * NST
    assert seq % CH == 0 and d % (2 * M2P * M2G) == 0 and 2 * dnh + m2h <= LANE
    hb_f = max(h for h in (16, 8, 4, 2, 1) if dnh % h == 0)
    hb_b = max(h for h in (8, 4, 2, 1) if dnh % h == 0)
    tm_rw = _pick(t_rows, 208, 16)
    conv_chunk = _pick(t_rows, min(CONV_CHUNK, t_rows // 3), 16)

    small_sharded = [meta_tokens, dn_conv_w[0], m2_conv_w[0], ffn_conv_w[0]]
    small_shapes = [p.shape for p in small_sharded]
    g_win, g_small = _gather_two_level("gather_w_in", [w_in[0].astype(WIRE).T, _pack(small_sharded)])
    rest = _exchange_start("gather_rest_start", [w_out[0].astype(WIRE), ffn_up[0].astype(WIRE), ffn_down[0].astype(WIRE)], [],
                           g_small)
    win_t = g_win.reshape(-1, d)
    small_full = [_unpack(g_small[k], small_shapes) for k in range(NDEV)]
    meta_f, dnconv_f, m2conv_f, ffnconv_f = [jnp.concatenate([small_full[k][i] for k in range(NDEV)], axis=-1) for i in range(4)]

    o_z, o_b, o_a = 3 * d, 4 * d, 4 * d + dnh
    o_m2z = 4 * d + 2 * dnh
    o_xbc, o_dt = o_m2z + d, o_m2z + d + xbc_w
    w_all_t = jnp.concatenate([win_t[:o_b], win_t[o_m2z:o_dt], win_t[o_b:o_m2z], win_t[o_dt:],
                               jnp.zeros((LANE - 2 * dnh - m2h, d), WIRE)], axis=0)
    seg_cols = {"q": (0, d), "k": (d, d), "v": (2 * d, d), "z": (3 * d, d), "m2z": (4 * d, d), "xbc": (5 * d, xbc_w),
                "sm": (5 * d + xbc_w, LANE)}

    h0 = jnp.concatenate([jnp.zeros((PADR, d), F32), meta_f, x[0]], axis=0)
    valid = lambda rows: rows >= PADR

    def norm_fwd(name, h, w):
        return _rw(name, lambda rows, j, hv, wv: (_rms(hv, wv),), [("r", h, d, _c0), ("p", w, d, _c0)],
                   [("r", d, d, _c0, MXU)], t_rows, tm_rw)[0]

    hn1 = norm_fwd("norm_mix", h0, norm_mix_w)
    proj = {s: _mm("proj_" + s, hn1, w_all_t, tb=True, b_rows=seg_cols[s], dep=rest[3]) for s in seg_cols}

    def dn_post(sec, cv):
        s = _silu(cv)
        if sec < 2:
            s = s * lax.rsqrt(jnp.sum(s * s, axis=-1, keepdims=True) + EPS)
        if sec == 0:
            s = s * (HD ** -0.5)
        return s

    def dn_prep(sec, name):
        def fn(rows, own, last, wins, pars):
            return [jnp.where(valid(rows), dn_post(sec, _conv(wins[0], pars[0])), 0.0)], []
        wc = dnconv_f[:, sec * d:(sec + 1) * d]
        return _cv("dn_prep_" + name, fn, [(proj[name], _cj)], [(wc, _cj)], [(d, _cj, F32)], [], t_rows, dnh)[0]

    q_act, k_act, v_act = dn_prep(0, "q"), dn_prep(1, "k"), dn_prep(2, "v")

    lane = lambda: lax.broadcasted_iota(jnp.int32, (1, LANE), 1)

    def lanes_of(vec, off):
        return jnp.pad(vec.astype(F32), ((0, 0), (off, LANE - off - vec.shape[1])))

    gate_params = [lanes_of(dn_a_log, dnh), lanes_of(dn_dt_bias, dnh), lanes_of(m2_a_log, 2 * dnh), lanes_of(m2_dt_bias, 2 * dnh)]

    def gates(rows, sm, p_alog, p_dtb, p_malog, p_mdtb):
        ln = lane()
        is_b, is_g = ln < dnh, jnp.logical_and(ln >= dnh, ln < 2 * dnh)
        is_d = jnp.logical_and(ln >= 2 * dnh, ln < 2 * dnh + m2h)
        beta = jax.nn.sigmoid(sm)
        gdec = -jnp.exp(p_alog) * _softplus(sm + p_dtb)
        dt = _softplus(sm + p_mdtb)
        am = dt * (-jnp.exp(p_malog))
        ok = valid(rows)
        g1 = jnp.where(ok, jnp.where(is_b, beta, jnp.where(is_g, gdec, jnp.where(is_d, dt, 0.0))), 0.0)
        g2 = jnp.where(jnp.logical_and(ok, is_d), am, 0.0)
        return g1, g2

    gate_ins = [("r", proj["sm"], LANE, _c0)] + [("p", p, LANE, _c0) for p in gate_params]
    g1, g2 = _rw("gates", lambda rows, j, *a: gates(rows, *a), gate_ins,
                 [("r", LANE, LANE, _c0, F32), ("r", LANE, LANE, _c0, F32)], t_rows, tm_rw)

    def head_rows(cols, per):
        n = cols.shape[1]
        return cols.reshape(nc, CH, n // per, per).transpose(2, 0, 3, 1)

    def head_cols(rows_):
        ngrp, _, per, _ = rows_.shape
        return rows_.transpose(1, 3, 0, 2).reshape(t_rows, ngrp * per)

    beta_r, gdec_r = head_rows(g1[:, :dnh], hb_f), head_rows(g1[:, dnh:2 * dnh], hb_f)
    hpg = m2h // M2G
    m2_rows = lambda gps: (head_rows(g1[:, 2 * dnh:2 * dnh + m2h], hpg * gps), head_rows(g2[:, 2 * dnh:2 * dnh + m2h], hpg * gps))

    o_dn, dn_states, dn_tinv = _gdn_fwd(q_act, k_act, v_act, beta_r, gdec_r, hb_f)

    def dn_out(o, z, w):
        outs = []
        for h in range(dnh):
            sl = slice(h * HD, (h + 1) * HD)
            outs.append(_rms(o[:, sl], w) * _silu(z[:, sl]))
        return jnp.concatenate(outs, axis=1)

    mixed_dn = _rw("dn_out", lambda rows, j, o, z, w: (dn_out(o, z, w),),
                   [("r", o_dn, d, _c0), ("r", proj["z"], d, _c0), ("p", dn_norm_w, HD, _c0)], [("r", d, d, _c0, MXU)],
                   t_rows, tm_rw)[0]

    def m2_prep(rows, own, last, wins, pars):
        return [jnp.where(valid(rows), _silu(_conv(wins[0], pars[0]) + pars[1][0]), 0.0)], []

    xbc_act = _cv("m2_prep", m2_prep, [(proj["xbc"], _cj)], [(m2conv_f, _cj), (m2_conv_b, _cj)], [(xbc_w, _cj, F32)], [],
                  t_rows, xbc_w // LANE)[0]
    y_ssd, m2_states = _ssd_fwd(xbc_act, *m2_rows(SSD_GPS_F), d, SSD_GPS_F)

    d_lanes = jnp.repeat(m2_d.astype(F32), M2P, axis=1)
    gw = d // M2G

    def m2_out(ys, xs, z, dl, nw):
        yv = (ys + dl * xs) * _silu(z)
        outs = []
        for gi in range(M2G):
            sl = slice(gi * gw, (gi + 1) * gw)
            outs.append(_rms(yv[:, sl], nw[:, sl]))
        return jnp.concatenate(outs, axis=1)

    m2_out_ins = [("r", y_ssd, d, _c0), ("r", xbc_act, d, _c0), ("r", proj["m2z"], d, _c0), ("p", d_lanes, d, _c0),
                  ("p", m2_norm_w, d, _c0)]
    mixed_m2 = _rw("m2_out", lambda rows, j, *a: (m2_out(*a),), m2_out_ins, [("r", d, d, _c0, MXU)], t_rows, tm_rw)[0]

    mixed = jnp.concatenate([mixed_dn, mixed_m2], axis=1)
    g_wout, g_wup, g_wdown = _exchange_wait("gather_rest_wait", rest, mixed)
    wout = _from_shards(g_wout, 0)
    wup = _from_shards(g_wup, 1)
    wdown = _from_shards(g_wdown, 0)
    up_g, up_v = (0, dff), (dff, dff)
    h1 = _mm("out_proj", mixed, wout, add=h0)
    hn2 = norm_fwd("norm_ffn", h1, norm_ffn_w)
    u_g, u_v = _mm("ffn_up_g", hn2, wup, b_cols=up_g), _mm("ffn_up_v", hn2, wup, b_cols=up_v)
    fc_g, fc_v = ffnconv_f[:, :dff], ffnconv_f[:, dff:]

    def ffn_act(rows, own, last, wins, pars):
        return [jnp.where(valid(rows), _silu(_conv(wins[0], pars[0])) * _conv(wins[1], pars[1]), 0.0)], []

    act = _cv("ffn_act", ffn_act, [(u_g, _cj), (u_v, _cj)], [(fc_g, _cj), (fc_v, _cj)], [(dff, _cj, MXU)], [],
              t_rows, dff // LANE)[0]
    h2 = _mm("ffn_down", act, wdown, add=h1, tk=1408)

    def loss_fn(hv, wf, tgt, rows):
        err = jnp.where(rows >= CH, _rms(hv, wf) - tgt, 0.0)
        return 0.5 * jnp.sum(jnp.mean(err * err, axis=-1, keepdims=True), axis=0, keepdims=True)

    def final(rows, j, hv, wf, tgt):
        loss, vjp = jax.vjp(lambda a, b: loss_fn(a, b, tgt, rows), hv, wf)
        dh, dw = vjp(jnp.ones((1, 1), F32))
        return dh, dh, dw, jnp.broadcast_to(loss, (1, LANE))

    wf2 = norm_final_w.reshape(1, d)
    dh2, dh2_m, d_wf, loss_part = _rw(
        "loss_head", final, [("r", h2, d, _c0), ("p", wf2, d, _c0), ("r", loss_target[0], d, _c0, lambda i: jnp.maximum(i - 1, 0))],
        [("r", d, d, _c0, F32), ("r", d, d, _c0, MXU), ("p", 1, d, d, _c0), ("p", 1, LANE, LANE, _c0)], t_rows, CH)
    loss = lax.psum(loss_part[0, 0], MESH_AXES)

    d_act = _mm("d_act", dh2_m, wdown, tb=True)
    gw_down = _mm("gw_down", act, dh2_m, ta=True, tm=1408, tn=1024, tk=2080, out_dtype=WIRE)
    x_down = _exchange_start("grad_down_start", [], [_to_shards(gw_down, 0).astype(WIRE)], gw_down)

    def t_end(last):
        return t_rows if last else None

    def ffn_act_bwd(rows, own, last, wins, pars):
        (ug, uv, da), (wg, wv) = wins, pars
        cg, cv = _conv(ug, wg), _conv(uv, wv)
        _, vjp = jax.vjp(lambda a, b: _silu(a) * b, cg, cv)
        dcg, dcv = vjp(jnp.where(valid(rows), da, 0.0))
        return ([_conv_t(dcg, wg, rows, t_end(last)), _conv_t(dcv, wv, rows, t_end(last))],
                [_conv_w(jnp.where(own, dcg, 0.0), ug, len(wg)), _conv_w(jnp.where(own, dcv, 0.0), uv, len(wv))])

    kf = fc_g.shape[0]
    du_g, du_v, g_fc_g, g_fc_v = _cv(
        "ffn_act_bwd", ffn_act_bwd, [(u_g, _cj), (u_v, _cj), (d_act, _cj)], [(fc_g, _cj), (fc_v, _cj)],
        [(dff, _cj, MXU), (dff, _cj, MXU)], [(kf, dff, _cj), (kf, dff, _cj)], t_rows, dff // LANE, chunk=conv_chunk)
    gw_up_half = _mm("gw_up_g", hn2, du_g, ta=True, tm=1024, tn=1408, tk=2080, out_dtype=WIRE, dep=x_down[3],
                     out_cols=(0, 2 * dff))
    gw_up_full = _mm("gw_up_v", hn2, du_v, ta=True, tm=1024, tn=1408, tk=2080, out_dtype=WIRE, out_cols=(dff, 2 * dff),
                     into=gw_up_half)
    x_up = _exchange_start("grad_up_start", [], [_to_shards(gw_up_full, 1).astype(WIRE)], gw_up_full)
    d_hn2 = _mm("d_hn2_v", du_v, wup, b_cols=up_v, tb=True, tk=1408, dep=x_up[3],
                add=_mm("d_hn2_g", du_g, wup, b_cols=up_g, tb=True, tk=1408, tn=2048, dep=x_up[3]))

    def norm_bwd(name, h, w, dy, dres):
        def fn(rows, j, hv, wv, dyv, dr):
            _, vjp = jax.vjp(_rms, hv, wv)
            dh, dw = vjp(dyv)
            dh = dh + dr
            return dh, dh, dw
        return _rw(name, fn, [("r", h, d, _c0), ("p", w, d, _c0), ("r", dy, d, _c0), ("r", dres, d, _c0)],
                   [("r", d, d, _c0, F32), ("r", d, d, _c0, MXU), ("p", 1, d, d, _c0)], t_rows, tm_rw)

    dh1, dh1_m, g_norm_ffn = norm_bwd("norm_ffn_bwd", h1, norm_ffn_w, d_hn2, dh2)

    gw_out = _mm("gw_out", mixed, dh1_m, ta=True, tm=1024, tn=1024, tk=2080, out_dtype=WIRE)
    x_out = _exchange_start("grad_out_start", [], [_to_shards(gw_out, 0).astype(WIRE)], gw_out)
    d_mixed = _mm("d_mixed", dh1_m, wout, tb=True, dep=x_out[3])

    gw_seg = {}

    def gw_in(seg, dseg_arr):
        gw_seg[seg] = _mm("gw_in_" + seg, hn1, dseg_arr, ta=True, tm=1024, tn=1024, tk=2080, out_dtype=WIRE)

    def m2_out_bwd(rows, j, ys, xs, z, dl, nw, dy):
        _, vjp = jax.vjp(m2_out, ys, xs, z, dl, nw)
        return vjp(dy)

    dy_ssd, dxs_skip, d_m2z, g_d_lanes, g_m2_norm = _rw(
        "m2_out_bwd", m2_out_bwd, m2_out_ins + [("r", d_mixed, d, lambda j: 1)],
        [("r", d, d, _c0, F32), ("r", d, d, _c0, F32), ("r", d, d, _c0, MXU), ("p", 1, d, d, _c0), ("p", 1, d, d, _c0)],
        t_rows, _pick(t_rows, 208, 16))
    gw_in("m2z", d_m2z)

    def fold_heads(vec_ref, out_ref):
        r = lax.broadcasted_iota(jnp.int32, (d, LANE), 0)
        c = lax.broadcasted_iota(jnp.int32, (d, LANE), 1)
        out_ref[...] = _dgh(vec_ref[...], jnp.where(jnp.logical_and(r >= c * M2P, r < (c + 1) * M2P), 1.0, 0.0), 1, 0)

    g_m2_d = pl.pallas_call(fold_heads, name="fold_m2_d", out_shape=jax.ShapeDtypeStruct((1, LANE), F32))(g_d_lanes)

    dxs, db_ssd, dc_ssd, ddt_r, dam_r = _ssd_bwd(xbc_act, *m2_rows(SSD_GPS_B), m2_states, dy_ssd, d, SSD_GPS_B)


    def m2_prep_bwd(rows, own, last, wins, pars):
        (p, *ds), (w, b) = wins, pars
        _, vjp = jax.vjp(_silu, _conv(p, w) + b[0])
        dpre, = vjp(jnp.where(valid(rows), functools.reduce(lambda a_, b_: a_ + b_, ds), 0.0))
        dpre_own = jnp.where(own, dpre, 0.0)
        return [_conv_t(dpre, w, rows, t_end(last))], [_conv_w(dpre_own, p, len(w)), [jnp.sum(dpre_own, axis=0, keepdims=True)]]

    def m2_prep_bwd_call(name, off, width, d_arrs):
        at = lambda j, blk0=off // LANE: blk0 + j
        return _cv(name, m2_prep_bwd, [(proj["xbc"], at)] + [(a, _cj) for a in d_arrs], [(m2conv_f, at), (m2_conv_b, at)],
                   [(width, _cj, MXU)], [(m2conv_f.shape[0], width, _cj), (1, width, _cj)], t_rows, width // LANE, chunk=conv_chunk)

    dp_xs, gcw_xs, gcb_xs = m2_prep_bwd_call("m2_prep_bwd_x", 0, d, [dxs, dxs_skip])
    dp_b, gcw_b, gcb_b = m2_prep_bwd_call("m2_prep_bwd_b", d, M2G * NST, [db_ssd])
    dp_c, gcw_c, gcb_c = m2_prep_bwd_call("m2_prep_bwd_c", d + M2G * NST, M2G * NST, [dc_ssd])
    d_pxbc = jnp.concatenate([dp_xs, dp_b, dp_c], axis=1)
    gw_in("xbc", d_pxbc)
    g_m2_conv = jnp.concatenate([gcw_xs, gcw_b, gcw_c], axis=1)
    g_m2_conv_b = jnp.concatenate([gcb_xs, gcb_b, gcb_c], axis=1)

    def dn_out_bwd(rows, j, o, z, w, dy):
        _, vjp = jax.vjp(dn_out, o, z, w)
        return vjp(dy)

    d_o, d_z, g_dn_norm = _rw(
        "dn_out_bwd", dn_out_bwd,
        [("r", o_dn, d, _c0), ("r", proj["z"], d, _c0), ("p", dn_norm_w, HD, _c0), ("r", d_mixed, d, _c0)],
        [("r", d, d, _c0, F32), ("r", d, d, _c0, MXU), ("p", 1, HD, HD, _c0)], t_rows, _pick(t_rows, 208, 16))
    gw_in("z", d_z)

    dq, dk, dv, dbeta_r, dgdec_r = _gdn_bwd(q_act, k_act, v_act, head_rows(g1[:, :dnh], hb_b),
                                            head_rows(g1[:, dnh:2 * dnh], hb_b), dn_states, dn_tinv, d_o, hb_b)

    def dn_prep_bwd(sec, name, dact):
        def fn(rows, own, last, wins, pars):
            (p, da), (w,) = wins, pars
            _, vjp = jax.vjp(functools.partial(dn_post, sec), _conv(p, w))
            dcv, = vjp(jnp.where(valid(rows), da, 0.0))
            return [_conv_t(dcv, w, rows, t_end(last))], [_conv_w(jnp.where(own, dcv, 0.0), p, len(w))]
        wc = dnconv_f[:, sec * d:(sec + 1) * d]
        return _cv("dn_prep_bwd_" + name, fn, [(proj[name], _cj), (dact, _cj)], [(wc, _cj)], [(d, _cj, MXU)],
                   [(wc.shape[0], d, _cj)], t_rows, dnh, chunk=conv_chunk if sec == 2 else None)

    (dp_q, gcw_q), (dp_k, gcw_k), (dp_v, gcw_v) = dn_prep_bwd(0, "q", dq), dn_prep_bwd(1, "k", dk), dn_prep_bwd(2, "v", dv)
    gw_in("q", dp_q), gw_in("k", dp_k), gw_in("v", dp_v)
    g_dn_conv = jnp.concatenate([gcw_q, gcw_k, gcw_v], axis=1)

    zpad = jnp.zeros((t_rows, LANE - 2 * dnh - m2h), F32)
    dg1 = jnp.concatenate([head_cols(dbeta_r), head_cols(dgdec_r), head_cols(ddt_r), zpad], axis=1)
    dg2 = jnp.concatenate([jnp.zeros((t_rows, 2 * dnh), F32), head_cols(dam_r), zpad], axis=1)

    def gates_bwd(rows, j, sm, pa, pb, pc, pd, d1, d2):
        _, vjp = jax.vjp(lambda *a: gates(rows, *a), sm, pa, pb, pc, pd)
        return vjp((d1, d2))

    dp_sm, g_pa, g_pb, g_pc, g_pd = _rw(
        "gates_bwd", gates_bwd, gate_ins + [("r", dg1, LANE, _c0), ("r", dg2, LANE, _c0)],
        [("r", LANE, LANE, _c0, MXU)] + [("p", 1, LANE, LANE, _c0)] * 4, t_rows, tm_rw)

    dseg = {"q": dp_q, "k": dp_k, "v": dp_v, "z": d_z, "m2z": d_m2z, "xbc": d_pxbc, "sm": dp_sm}
    gw_in("sm", dp_sm)
    gsm = gw_seg["sm"]
    gw_in_full = jnp.concatenate([gw_seg["q"], gw_seg["k"], gw_seg["v"], gw_seg["z"], gsm[:, :2 * dnh], gw_seg["m2z"],
                                  gw_seg["xbc"], gsm[:, 2 * dnh:2 * dnh + m2h]], axis=1)
    x_in = _exchange_start("grad_in_start", [], [_to_shards(gw_in_full, 1).astype(WIRE)], gw_in_full)
    d_hn1 = None
    for s in dseg:
        d_hn1 = _mm("d_hn1_" + s, dseg[s], w_all_t, b_rows=seg_cols[s], tk=2048, add=d_hn1, dep=x_in[3])
    dh0, _, g_norm_mix = norm_bwd("norm_mix_bwd", h0, norm_mix_w, d_hn1, dh1)

    g_ffn_conv = jnp.concatenate([g_fc_g, g_fc_v], axis=1)
    small_parts = [_to_shards(dh0[PADR:CH], 1), _to_shards(g_dn_conv, 1), _to_shards(g_m2_conv, 1), _to_shards(g_ffn_conv, 1)]
    small_scatter = jnp.stack([_pack([p[k] for p in small_parts]) for k in range(NDEV)])

    rep_names = ["norm_mix_w", "dn_a_log", "dn_dt_bias", "dn_norm_w", "m2_conv_b", "m2_a_log", "m2_dt_bias", "m2_d",
                 "m2_norm_w", "norm_ffn_w", "norm_final_w"]
    rep_grads = [g_norm_mix, g_pa[:, dnh:2 * dnh], g_pb[:, dnh:2 * dnh], g_dn_norm, g_m2_conv_b, g_pc[:, 2 * dnh:2 * dnh + m2h],
                 g_pd[:, 2 * dnh:2 * dnh + m2h], g_m2_d[:, :m2h], g_m2_norm, g_norm_ffn, d_wf.reshape(d)]

    weights = dict(meta_tokens=meta_tokens, norm_mix_w=norm_mix_w, w_in=w_in, dn_conv_w=dn_conv_w, dn_a_log=dn_a_log,
                   dn_dt_bias=dn_dt_bias, dn_norm_w=dn_norm_w, m2_conv_w=m2_conv_w, m2_conv_b=m2_conv_b, m2_a_log=m2_a_log,
                   m2_dt_bias=m2_dt_bias, m2_d=m2_d, m2_norm_w=m2_norm_w, w_out=w_out, norm_ffn_w=norm_ffn_w, ffn_up=ffn_up,
                   ffn_conv_w=ffn_conv_w, ffn_down=ffn_down, norm_final_w=norm_final_w)
    mom1 = dict(meta_tokens=m_meta_tokens, norm_mix_w=m_norm_mix_w, w_in=m_w_in, dn_conv_w=m_dn_conv_w, dn_a_log=m_dn_a_log,
                dn_dt_bias=m_dn_dt_bias, dn_norm_w=m_dn_norm_w, m2_conv_w=m_m2_conv_w, m2_conv_b=m_m2_conv_b,
                m2_a_log=m_m2_a_log, m2_dt_bias=m_m2_dt_bias, m2_d=m_m2_d, m2_norm_w=m_m2_norm_w, w_out=m_w_out,
                norm_ffn_w=m_norm_ffn_w, ffn_up=m_ffn_up, ffn_conv_w=m_ffn_conv_w, ffn_down=m_ffn_down,
                norm_final_w=m_norm_final_w)
    mom2 = dict(meta_tokens=v_meta_tokens, norm_mix_w=v_norm_mix_w, w_in=v_w_in, dn_conv_w=v_dn_conv_w, dn_a_log=v_dn_a_log,
                dn_dt_bias=v_dn_dt_bias, dn_norm_w=v_dn_norm_w, m2_conv_w=v_m2_conv_w, m2_conv_b=v_m2_conv_b,
                m2_a_log=v_m2_a_log, m2_dt_bias=v_m2_dt_bias, m2_d=v_m2_d, m2_norm_w=v_m2_norm_w, w_out=v_w_out,
                norm_ffn_w=v_norm_ffn_w, ffn_up=v_ffn_up, ffn_conv_w=v_ffn_conv_w, ffn_down=v_ffn_down,
                norm_final_w=v_norm_final_w)
    res = {}

    def adam_big(name, started, after):
        staged, = _exchange_wait("grad_" + name + "_wait", started, after)
        outs = _adamw("adamw_" + name, staged, weights[name][0], mom1[name][0], mom2[name][0])
        res[name] = tuple(o[None] for o in outs)
        return outs[1]

    done = adam_big("ffn_down", x_down, dh0)
    done = adam_big("ffn_up", x_up, done)
    done = adam_big("w_out", x_out, done)
    done = adam_big("w_in", x_in, done)
    st_rep, st_small = _exchange("exchange_small_grads", [_pack(rep_grads)], [small_scatter], done)

    def adam_packed(label, staged, names):
        shapes = [weights[nm].shape for nm in names]
        outs = _adamw(label, staged, *[_pack([src[nm] for nm in names]) for src in (weights, mom1, mom2)])
        unpacked = [_unpack(o, shapes) for o in outs]
        for i, nm in enumerate(names):
            res[nm] = tuple(u[i] for u in unpacked)

    adam_packed("adamw_small_sharded", st_small, ["meta_tokens", "dn_conv_w", "m2_conv_w", "ffn_conv_w"])
    adam_packed("adamw_replicated", st_rep, rep_names)

    order = list(weights)
    grad_x = dh0[CH:][None]
    return (loss, grad_x, *[res[nm][0] for nm in order], *[res[nm][1] for nm in order], *[res[nm][2] for nm in order],
            *[res[nm][3] for nm in order])
```

```python
import functools
import math

import jax
import jax.numpy as jnp
from jax import lax
from jax.experimental import pallas as pl
from jax.experimental.pallas import tpu as pltpu

F32 = jnp.float32
MXU = jnp.bfloat16
WIRE = jnp.bfloat16
HI = lax.Precision.HIGH

NDEV = 8
CH = 64
NMETA = 16
PADR = CH - NMETA
EPS = 1e-6
HD = 128
M2P = 64
M2G = 4
SSD_GPS_F, SSD_GPS_B = 4, 1
NST = 128
LANE = 128

ADAM_LR, ADAM_B1, ADAM_B2, ADAM_EPS, ADAM_WD, ADAM_STEP = 0.001, 0.9, 0.999, 1e-08, 0.01, 10

MESH_AXES = ("x", "y", "c")


def _pick(n, target, mult=16):
    best = None
    for t in range(mult, min(n, target) + 1, mult):
        if n % t == 0:
            best = t
    return best if best is not None else n


def _dg(a, b, ca, cb):
    return lax.dot_general(a.astype(MXU), b.astype(MXU), (((ca,), (cb,)), ((), ())), preferred_element_type=F32)


def _dgh(a, b, ca, cb):
    return lax.dot_general(a, b, (((ca,), (cb,)), ((), ())), precision=HI, preferred_element_type=F32)


def _silu(x):
    return x * jax.nn.sigmoid(x)


def _softplus(x):
    return jnp.maximum(x, 0.0) + jnp.log1p(jnp.exp(-jnp.abs(x)))


def _rms(x, w):
    return x * lax.rsqrt(jnp.mean(x * x, axis=-1, keepdims=True) + EPS) * w


def _cparams(sem, vmem_mb):
    return pltpu.CompilerParams(dimension_semantics=sem, vmem_limit_bytes=vmem_mb << 20)


def _mm(name, a, b, *, ta=False, tb=False, add=None, out_dtype=F32, tm=1040, tn=1024, tk=2048, dep=None,
        b_rows=None, b_cols=None):
    m = a.shape[1] if ta else a.shape[0]
    (rs, rw), (cs, cw) = b_rows or (0, b.shape[0]), b_cols or (0, b.shape[1])
    (n, ns), (kdim, ks) = ((rw, rs), (cw, cs)) if tb else ((cw, cs), (rw, rs))
    assert kdim == (a.shape[0] if ta else a.shape[1])
    tm = _pick(m, tm, 128 if ta else 16)
    tn = _pick(math.gcd(n, ns), tn, 128)
    tk = _pick(math.gcd(kdim, ks), tk, 16 if (ta and not tb) else 128)
    nk = kdim // tk
    bj0, bk0 = ns // tn, ks // tk
    ca, cb = (0 if ta else 1), (1 if tb else 0)

    def body(*refs):
        a_ref, b_ref = refs[0], refs[1]
        add_ref = refs[2] if add is not None else None
        if nk == 1:
            r = _dg(a_ref[...], b_ref[...], ca, cb)
            if add_ref is not None:
                r = r + add_ref[...].astype(F32)
            refs[-1][...] = r.astype(refs[-1].dtype)
            return
        o_ref, acc = refs[-2], refs[-1]
        k = pl.program_id(2)

        @pl.when(k == 0)
        def _():
            acc[...] = jnp.zeros_like(acc)

        acc[...] += _dg(a_ref[...], b_ref[...], ca, cb)

        @pl.when(k == nk - 1)
        def _():
            r = acc[...]
            if add_ref is not None:
                r = r + add_ref[...].astype(F32)
            o_ref[...] = r.astype(o_ref.dtype)

    a_spec = pl.BlockSpec((tk, tm), lambda i, j, k: (k, i)) if ta else pl.BlockSpec((tm, tk), lambda i, j, k: (i, k))
    b_spec = (pl.BlockSpec((tn, tk), lambda i, j, k: (j + bj0, k + bk0)) if tb
              else pl.BlockSpec((tk, tn), lambda i, j, k: (k + bk0, j + bj0)))
    in_specs, ops = [a_spec, b_spec], [a, b]
    if add is not None:
        in_specs.append(pl.BlockSpec((tm, tn), lambda i, j, k: (i, j)))
        ops.append(add)
    if dep is not None:
        in_specs.append(pl.BlockSpec((8, LANE), lambda i, j, k: (0, 0)))
        ops.append(dep)
    return pl.pallas_call(
        body, name=name, grid=(m // tm, n // tn, nk), in_specs=in_specs,
        out_specs=pl.BlockSpec((tm, tn), lambda i, j, k: (i, j)),
        out_shape=jax.ShapeDtypeStruct((m, n), out_dtype),
        scratch_shapes=[pltpu.VMEM((tm, tn), F32)] if nk > 1 else [],
        compiler_params=_cparams(("parallel", "parallel", "arbitrary"), 48),
    )(*ops)


def _rw(name, fn, ins, outs, nrows, tm, ncol=1, vmem_mb=48):
    nrow = nrows // tm
    sub = tm
    in_specs, ops = [], []
    for spec in ins:
        kind, arr, bw, cj = spec[:4]
        ops.append(arr)
        if kind == "r":
            ri = spec[4] if len(spec) > 4 else (lambda i: i)
            in_specs.append(pl.BlockSpec((tm, bw), lambda j, i, cj=cj, ri=ri: (ri(i), cj(j))))
        else:
            in_specs.append(pl.BlockSpec((arr.shape[0], bw), lambda j, i, cj=cj: (0, cj(j))))
    out_shape, out_specs = [], []
    for o in outs:
        if o[0] == "r":
            _, width, bw, cj, dt = o
            out_shape.append(jax.ShapeDtypeStruct((nrows, width), dt))
            out_specs.append(pl.BlockSpec((tm, bw), lambda j, i, cj=cj: (i, cj(j))))
        else:
            _, rows, width, bw, cj = o
            out_shape.append(jax.ShapeDtypeStruct((rows, width), F32))
            out_specs.append(pl.BlockSpec((rows, bw), lambda j, i, cj=cj: (0, cj(j))))
    n_in = len(ins)

    def body(*refs):
        j, i = pl.program_id(0), pl.program_id(1)
        in_refs, out_refs = refs[:n_in], refs[n_in:]
        pars = [ref[...] if spec[0] == "p" else None for spec, ref in zip(ins, in_refs)]

        def one(r0, nr):
            rows = i * tm + r0 + lax.broadcasted_iota(jnp.int32, (nr, 1), 0)
            vals = [par if spec[0] == "p" else ref[pl.ds(r0, nr), :] for spec, ref, par in zip(ins, in_refs, pars)]
            parts = []
            for o, val, ref in zip(outs, fn(rows, j, *vals), out_refs):
                if o[0] == "r":
                    ref[pl.ds(r0, nr), :] = val.astype(ref.dtype)
                else:
                    parts.append(val)
            return parts

        if sub >= tm:
            parts = one(0, tm)
        else:
            zero = [jnp.zeros((1, o[3]), F32) for o in outs if o[0] == "p"]
            parts = lax.fori_loop(
                0, tm // sub, lambda s, acc: [a + b for a, b in zip(acc, one(pl.multiple_of(s * sub, sub), sub))], zero)
        for ref, val in zip([r for o, r in zip(outs, out_refs) if o[0] == "p"], parts):
            @pl.when(i == 0)
            def _(ref=ref):
                ref[...] = jnp.zeros_like(ref)

            ref[...] += val

    return pl.pallas_call(
        body, name=name, grid=(ncol, nrow), in_specs=in_specs, out_specs=out_specs, out_shape=out_shape,
        compiler_params=_cparams(("parallel", "arbitrary"), vmem_mb),
    )(*ops)


def _c0(j):
    return 0


def _cj(j):
    return j


def _shift(x, s):
    if s == 0:
        return x
    return pltpu.roll(x, s % x.shape[0], 0)


def _conv(x, w):
    k = len(w)
    return functools.reduce(lambda a, b: a + b, [w[j] * _shift(x, k - 1 - j) for j in range(k)])


def _conv_t(dy, w, rows, t_end):
    k = len(w)
    terms = []
    for j in range(k):
        s = k - 1 - j
        v = _shift(dy, -s)
        if t_end is not None and s > 0:
            v = jnp.where(rows + s < t_end, v, 0.0)
        terms.append(w[j] * v)
    return functools.reduce(lambda a, b: a + b, terms)


def _conv_w(dy, x, k):
    return [jnp.sum(dy * _shift(x, k - 1 - j), axis=0, keepdims=True) for j in range(k)]


CONV_CHUNK = 320
HALO = 8


def _cv(name, fn, row_ins, par_ins, row_outs, par_outs, nrows, ncol, chunk=None):
    whole = chunk is None
    chunk = nrows if whole else chunk
    n_chunks = nrows // chunk
    assert nrows % chunk == 0 and (whole or n_chunks >= 3)
    n_ri, n_pi, n_ro = len(row_ins), len(par_ins), len(row_outs)

    def body(*refs):
        rin, pin = refs[:n_ri], refs[n_ri:n_ri + n_pi]
        rout, pout = refs[n_ri + n_pi:n_ri + n_pi + n_ro], refs[n_ri + n_pi + n_ro:]
        pars = [[p[pl.ds(r, 1), :] for r in range(p.shape[0])] for p in pin]

        def run(r0, top, bot, last):
            wlen = top + chunk + bot
            w0 = r0 - top if isinstance(r0, int) else pl.multiple_of(r0 - top, HALO)
            local = lax.broadcasted_iota(jnp.int32, (wlen, 1), 0)
            own = jnp.logical_and(local >= top, local < top + chunk)
            outs, parts = fn(w0 + local, own, last, [ref[pl.ds(w0, wlen), :] for ref in rin], pars)
            for ref, val in zip(rout, outs):
                ref[pl.ds(r0, chunk), :] = val[top:top + chunk].astype(ref.dtype)
            return parts

        def add(acc, parts):
            return [[a + b for a, b in zip(ra, rb)] for ra, rb in zip(acc, parts)]

        if whole:
            acc = run(0, 0, 0, True)
        else:
            acc = run(0, 0, HALO, False)
            acc = lax.fori_loop(1, n_chunks - 1,
                                lambda i, a: add(a, run(pl.multiple_of(i * chunk, chunk), HALO, HALO, False)), acc)
            acc = add(acc, run(nrows - chunk, HALO, 0, True))
        for ref, prow in zip(pout, acc):
            for r, v in enumerate(prow):
                ref[pl.ds(r, 1), :] = v

    in_specs = [pl.BlockSpec((nrows, LANE), lambda j, cj=cj: (0, cj(j))) for _, cj in row_ins]
    in_specs += [pl.BlockSpec((a.shape[0], LANE), lambda j, cj=cj: (0, cj(j))) for a, cj in par_ins]
    out_specs = [pl.BlockSpec((nrows, LANE), lambda j, cj=cj: (0, cj(j))) for _, cj, _ in row_outs]
    out_specs += [pl.BlockSpec((k, LANE), lambda j, cj=cj: (0, cj(j))) for k, _, cj in par_outs]
    out_shape = [jax.ShapeDtypeStruct((nrows, width), dt) for width, _, dt in row_outs]
    out_shape += [jax.ShapeDtypeStruct((k, width), F32) for k, width, _ in par_outs]
    return pl.pallas_call(
        body, name=name, grid=(ncol,), in_specs=in_specs, out_specs=out_specs, out_shape=out_shape,
        compiler_params=_cparams(("parallel",), 48),
    )(*[a for a, _ in row_ins], *[a for a, _ in par_ins])


def _tri():
    r = lax.broadcasted_iota(jnp.int32, (CH, CH), 0)
    c = lax.broadcasted_iota(jnp.int32, (CH, CH), 1)
    return r, c


def _col(row):
    r, c = _tri()
    return jnp.sum(jnp.where(r == c, row, 0.0), axis=1, keepdims=True)


def _cumsum_rc(g_r):
    r, c = _tri()
    g_c = _col(g_r)
    cs_r = jnp.sum(jnp.where(r <= c, g_c, 0.0), axis=0, keepdims=True)
    cs_c = jnp.sum(jnp.where(c <= r, g_r, 0.0), axis=1, keepdims=True)
    return cs_r, cs_c


def _decay(cs_r, cs_c):
    r, c = _tri()
    return jnp.exp(jnp.where(c <= r, cs_c - cs_r, -jnp.inf))


def _gdn_a(ks, betas, gs):
    r, c = _tri()
    cs = [_cumsum_rc(g) for g in gs]
    kk = [_dg(k, k, 1, 1) for k in ks]
    return [jnp.where(c < r, _col(b) * kki * _decay(*csi), 0.0) for b, kki, csi in zip(betas, kk, cs)]


def _neumann(a_list):
    r, c = _tri()
    xs = [jnp.where(r == c, 1.0, 0.0) - a for a in a_list]
    ps = list(a_list)
    n = 2
    while n < CH:
        ps = [_dgh(p, p, 1, 0) for p in ps]
        xs = [x + _dgh(x, p, 1, 0) for x, p in zip(xs, ps)]
        n *= 2
    return xs


def _gdn_rest(ss, qs, ks, vs, betas, gs, ts):
    n = range(len(ss))
    cs = [_cumsum_rc(g) for g in gs]
    dm = [_decay(*csi) for csi in cs]
    ecs = [jnp.exp(csi[1]) for csi in cs]
    bc = [_col(b) for b in betas]
    u = [_dgh(ts[i], vs[i] * bc[i], 1, 0) for i in n]
    w = [_dgh(ts[i], ks[i] * (bc[i] * ecs[i]), 1, 0) for i in n]
    ws = [_dg(w[i], ss[i], 1, 0) for i in n]
    v_new = [u[i] - ws[i] for i in n]
    qk = [_dg(qs[i], ks[i], 1, 1) * dm[i] for i in n]
    o_in = [_dg(qs[i] * ecs[i], ss[i], 1, 0) for i in n]
    o = [o_in[i] + _dg(qk[i], v_new[i], 1, 0) for i in n]
    g_last = [jnp.sum(g, axis=1, keepdims=True) for g in gs]
    s_new = [ss[i] * jnp.exp(g_last[i]) + _dg(ks[i] * jnp.exp(g_last[i] - cs[i][1]), v_new[i], 0, 0) for i in n]
    return s_new, o


def _gdn_fwd(q, k, v, beta, g, hb):
    t_rows, d = q.shape
    nc, ng, w = t_rows // CH, d // (HD * hb), HD * hb
    sls = [slice(h * HD, (h + 1) * HD) for h in range(hb)]

    def body(q_ref, k_ref, v_ref, b_ref, g_ref, o_ref, ss_ref, ts_ref, s_scr):
        c = pl.program_id(1)

        @pl.when(c == 0)
        def _():
            s_scr[...] = jnp.zeros_like(s_scr)

        qs, ks, vs = ([ref[:, sl] for sl in sls] for ref in (q_ref, k_ref, v_ref))
        br = [b_ref[0, 0, pl.ds(h, 1), :] for h in range(hb)]
        gr = [g_ref[0, 0, pl.ds(h, 1), :] for h in range(hb)]
        s0 = [s_scr[h] for h in range(hb)]
        tm = _neumann(_gdn_a(ks, br, gr))
        s1, o = _gdn_rest(s0, qs, ks, vs, br, gr, tm)
        for h in range(hb):
            ss_ref[0, 0, h] = s0[h]
            ts_ref[0, 0, h] = tm[h]
            o_ref[:, sls[h]] = o[h]
            s_scr[h] = s1[h]

    blk = pl.BlockSpec((CH, w), lambda n, c: (c, n))
    row = pl.BlockSpec((1, 1, hb, CH), lambda n, c: (n, c, 0, 0))
    return pl.pallas_call(
        body, name="gdn_fwd", grid=(ng, nc), in_specs=[blk, blk, blk, row, row],
        out_specs=[blk, pl.BlockSpec((1, 1, hb, HD, HD), lambda n, c: (n, c, 0, 0, 0)),
                   pl.BlockSpec((1, 1, hb, CH, CH), lambda n, c: (n, c, 0, 0, 0))],
        out_shape=[jax.ShapeDtypeStruct((t_rows, d), F32), jax.ShapeDtypeStruct((ng, nc, hb, HD, HD), F32),
                   jax.ShapeDtypeStruct((ng, nc, hb, CH, CH), F32)],
        scratch_shapes=[pltpu.VMEM((hb, HD, HD), F32)],
        compiler_params=_cparams(("parallel", "arbitrary"), 32),
    )(q, k, v, beta, g)


def _gdn_bwd(q, k, v, beta, g, ss, ts, do, hb):
    t_rows, d = q.shape
    nc, ng, w = t_rows // CH, d // (HD * hb), HD * hb
    per_f = ss.shape[2] // hb
    sls = [slice(h * HD, (h + 1) * HD) for h in range(hb)]

    def body(q_ref, k_ref, v_ref, b_ref, g_ref, ss_ref, ts_ref, do_ref, dq_ref, dk_ref, dv_ref, db_ref, dg_ref, ds_scr):
        cr = pl.program_id(1)

        @pl.when(cr == 0)
        def _():
            ds_scr[...] = jnp.zeros_like(ds_scr)

        first = cr == nc - 1
        rowi = lax.broadcasted_iota(jnp.int32, (CH, 1), 0)
        lani = lax.broadcasted_iota(jnp.int32, (1, CH), 1)
        keep_c = jnp.logical_or(jnp.logical_not(first), rowi >= PADR)
        keep_r = jnp.logical_or(jnp.logical_not(first), lani >= PADR)
        hs = range(hb)
        qs, ks, vs, dos = ([ref[:, sl] for sl in sls] for ref in (q_ref, k_ref, v_ref, do_ref))
        br = [b_ref[0, 0, pl.ds(h, 1), :] for h in hs]
        gr = [g_ref[0, 0, pl.ds(h, 1), :] for h in hs]
        tm = [ts_ref[0, 0, h] for h in hs]
        _, vjp_rest = jax.vjp(_gdn_rest, [ss_ref[0, 0, h] for h in hs], qs, ks, vs, br, gr, tm)
        ds0, dq, dk, dv, db, dg, dt = vjp_rest(([ds_scr[h] for h in hs], dos))
        dtt = [_dgh(dt[h], tm[h], 1, 1) for h in hs]
        da = [-_dgh(tm[h], dtt[h], 0, 0) for h in hs]
        _, vjp_a = jax.vjp(_gdn_a, ks, br, gr)
        dk2, db2, dg2 = vjp_a(da)
        for h in hs:
            ds_scr[h] = ds0[h]
            dq_ref[:, sls[h]] = jnp.where(keep_c, dq[h], 0.0)
            dk_ref[:, sls[h]] = jnp.where(keep_c, dk[h] + dk2[h], 0.0)
            dv_ref[:, sls[h]] = jnp.where(keep_c, dv[h], 0.0)
            db_ref[0, 0, pl.ds(h, 1), :] = jnp.where(keep_r, db[h] + db2[h], 0.0)
            dg_ref[0, 0, pl.ds(h, 1), :] = jnp.where(keep_r, dg[h] + dg2[h], 0.0)

    blk = pl.BlockSpec((CH, w), lambda n, c: (nc - 1 - c, n))
    row = pl.BlockSpec((1, 1, hb, CH), lambda n, c: (n, nc - 1 - c, 0, 0))
    return pl.pallas_call(
        body, name="gdn_bwd", grid=(ng, nc),
        in_specs=[blk, blk, blk, row, row,
                  pl.BlockSpec((1, 1, hb, HD, HD), lambda n, c: (n // per_f, nc - 1 - c, n % per_f, 0, 0)),
                  pl.BlockSpec((1, 1, hb, CH, CH), lambda n, c: (n // per_f, nc - 1 - c, n % per_f, 0, 0)), blk],
        out_specs=[blk, blk, blk, row, row],
        out_shape=[jax.ShapeDtypeStruct((t_rows, d), F32)] * 3 + [jax.ShapeDtypeStruct((ng, nc, hb, CH), F32)] * 2,
        scratch_shapes=[pltpu.VMEM((hb, HD, HD), F32)],
        compiler_params=_cparams(("parallel", "arbitrary"), 32),
    )(q, k, v, beta, g, ss, ts, do)


def _ssd_group(s, xs, bm, cm, dt_r, a_r):
    prs = range(len(s))
    ngrp = bm.shape[1] // NST
    grp = [p // (len(s) // ngrp) for p in prs]
    bms = [bm[:, g * NST:(g + 1) * NST] for g in range(ngrp)]
    cms = [cm[:, g * NST:(g + 1) * NST] for g in range(ngrp)]
    first = lax.broadcasted_iota(jnp.int32, (1, 2 * M2P), 1) < M2P

    def pick(vals, p):
        return jnp.where(first, vals[2 * p], vals[2 * p + 1])

    cs = [_cumsum_rc(a) for a in a_r]
    lm = [_decay(*csi) for csi in cs]
    ecs = [jnp.exp(csi[1]) for csi in cs]
    alast = [jnp.sum(a, axis=1, keepdims=True) for a in a_r]
    ealast = [jnp.exp(al) for al in alast]
    wt = [jnp.exp(al - csi[1]) for al, csi in zip(alast, cs)]
    dtc = [_col(t) for t in dt_r]
    xdt = [xs[:, p * LANE:(p + 1) * LANE] * pick(dtc, p) for p in prs]
    cb = [_dg(cms[g], bms[g], 1, 1) for g in range(ngrp)]
    y0 = [_dg(cb[grp[p]] * lm[2 * p], xdt[p], 1, 0) for p in prs]
    y1 = [_dg(cb[grp[p]] * lm[2 * p + 1], xdt[p], 1, 0) for p in prs]
    yo = [_dg(cms[grp[p]], s[p], 1, 0) for p in prs]
    y = [jnp.where(first, y0[p], y1[p]) + yo[p] * pick(ecs, p) for p in prs]
    s_new = [s[p] * pick(ealast, p) + _dg(bms[grp[p]], xdt[p] * pick(wt, p), 0, 0) for p in prs]
    return s_new, jnp.concatenate(y, axis=1)


def _ssd_specs(nc, d, rev, gps):
    assert M2G % gps == 0 and (d // LANE) % gps == 0
    hps = (d // M2P) // M2G * gps
    cc = (lambda c: nc - 1 - c) if rev else (lambda c: c)
    xs = pl.BlockSpec((CH, hps * M2P), lambda g, c: (cc(c), g))
    bm = pl.BlockSpec((CH, NST * gps), lambda g, c: (cc(c), (d // LANE) // gps + g))
    cm = pl.BlockSpec((CH, NST * gps), lambda g, c: (cc(c), (d // LANE + M2G) // gps + g))
    row = pl.BlockSpec((1, 1, hps, CH), lambda g, c: (g, cc(c), 0, 0))
    return xs, bm, cm, row, hps


def _ssd_fwd(xbc, dt, a, d, gps):
    t_rows = xbc.shape[0]
    nc = t_rows // CH
    xs, bm, cm, row, hpg = _ssd_specs(nc, d, False, gps)
    ppg = hpg // 2
    st = pl.BlockSpec((1, 1, ppg, NST, LANE), lambda g, c: (g, c, 0, 0, 0))

    def body(xs_ref, b_ref, c_ref, dt_ref, a_ref, y_ref, ss_ref, s_scr):
        c = pl.program_id(1)

        @pl.when(c == 0)
        def _():
            s_scr[...] = jnp.zeros_like(s_scr)

        s0 = [s_scr[p] for p in range(ppg)]
        for p in range(ppg):
            ss_ref[0, 0, p] = s0[p]
        dt_r = [dt_ref[0, 0, pl.ds(h, 1), :] for h in range(hpg)]
        a_r = [a_ref[0, 0, pl.ds(h, 1), :] for h in range(hpg)]
        s1, y = _ssd_group(s0, xs_ref[...], b_ref[...], c_ref[...], dt_r, a_r)
        y_ref[...] = y
        for p in range(ppg):
            s_scr[p] = s1[p]

    return pl.pallas_call(
        body, name="ssd_fwd", grid=(M2G // gps, nc), in_specs=[xs, bm, cm, row, row], out_specs=[xs, st],
        out_shape=[jax.ShapeDtypeStruct((t_rows, d), F32), jax.ShapeDtypeStruct((M2G // gps, nc, ppg, NST, LANE), F32)],
        scratch_shapes=[pltpu.VMEM((ppg, NST, LANE), F32)],
        compiler_params=_cparams(("parallel", "arbitrary"), 32),
    )(xbc, xbc, xbc, dt, a)


def _ssd_bwd(xbc, dt, a, ss, dy, d, gps):
    t_rows = xbc.shape[0]
    nc = t_rows // CH
    xs, bm, cm, row, hpg = _ssd_specs(nc, d, True, gps)
    ppg = hpg // 2
    per_f = ss.shape[2] // ppg
    st = pl.BlockSpec((1, 1, ppg, NST, LANE), lambda g, c: (g // per_f, nc - 1 - c, g % per_f, 0, 0))

    def body(xs_ref, b_ref, c_ref, dt_ref, a_ref, ss_ref, dy_ref, dxs_ref, db_ref, dc_ref, ddt_ref, da_ref, ds_scr):
        cr = pl.program_id(1)

        @pl.when(cr == 0)
        def _():
            ds_scr[...] = jnp.zeros_like(ds_scr)

        first = cr == nc - 1
        keep_c = jnp.logical_or(jnp.logical_not(first), lax.broadcasted_iota(jnp.int32, (CH, 1), 0) >= PADR)
        keep_r = jnp.logical_or(jnp.logical_not(first), lax.broadcasted_iota(jnp.int32, (1, CH), 1) >= PADR)
        dt_r = [dt_ref[0, 0, pl.ds(h, 1), :] for h in range(hpg)]
        a_r = [a_ref[0, 0, pl.ds(h, 1), :] for h in range(hpg)]
        s0 = [ss_ref[0, 0, p] for p in range(ppg)]
        _, vjp = jax.vjp(_ssd_group, s0, xs_ref[...], b_ref[...], c_ref[...], dt_r, a_r)
        ds0, dxs, db, dc, ddt, da = vjp(([ds_scr[p] for p in range(ppg)], dy_ref[...]))
        for p in range(ppg):
            ds_scr[p] = ds0[p]
        dxs_ref[...] = jnp.where(keep_c, dxs, 0.0)
        db_ref[...] = jnp.where(keep_c, db, 0.0)
        dc_ref[...] = jnp.where(keep_c, dc, 0.0)
        for h in range(hpg):
            ddt_ref[0, 0, pl.ds(h, 1), :] = jnp.where(keep_r, ddt[h], 0.0)
            da_ref[0, 0, pl.ds(h, 1), :] = jnp.where(keep_r, da[h], 0.0)

    grp = pl.BlockSpec((CH, NST * gps), lambda g, c: (nc - 1 - c, g))
    return pl.pallas_call(
        body, name="ssd_bwd", grid=(M2G // gps, nc), in_specs=[xs, bm, cm, row, row, st, xs],
        out_specs=[xs, grp, grp, row, row],
        out_shape=[jax.ShapeDtypeStruct((t_rows, d), F32)] + [jax.ShapeDtypeStruct((t_rows, M2G * NST), F32)] * 2
        + [jax.ShapeDtypeStruct((M2G // gps, nc, hpg, CH), F32)] * 2,
        scratch_shapes=[pltpu.VMEM((ppg, NST, LANE), F32)],
        compiler_params=_cparams(("parallel", "arbitrary"), 32),
    )(xbc, xbc, xbc, dt, a, ss, dy)


def _exchange(name, gathers, scatters, after):
    arrays = list(gathers) + list(scatters)
    n_g, n = len(gathers), len(arrays)

    def body(*refs):
        ins, outs = refs[:n], refs[n + 1:2 * n + 1]
        send_sems, recv_sems, local_sems = refs[2 * n + 1:]
        x, y, c = lax.axis_index("x"), lax.axis_index("y"), lax.axis_index("c")
        me = 4 * x + 2 * y + c

        def src(a, slot):
            return ins[a] if a < n_g else ins[a].at[slot]

        local = [pltpu.make_async_copy(src(a, me), outs[a].at[me], local_sems.at[a]) for a in range(n)]
        for cp in local:
            cp.start()
        copies = []
        for rel in range(1, NDEV):
            px, py, pc = x ^ (rel >> 2), y ^ ((rel >> 1) & 1), c ^ (rel & 1)
            peer = 4 * px + 2 * py + pc
            for a in range(n):
                copies.append(pltpu.make_async_remote_copy(
                    src_ref=src(a, peer), dst_ref=outs[a].at[me], send_sem=send_sems.at[a, rel - 1],
                    recv_sem=recv_sems.at[a, rel - 1], device_id=(px, py, pc), device_id_type=pl.DeviceIdType.MESH))
        for cp in copies:
            cp.start()
        for cp in copies:
            cp.wait_recv()
        for cp in copies:
            cp.wait_send()
        for cp in local:
            cp.wait()

    any_spec = pl.BlockSpec(memory_space=pl.ANY)
    out_shape = [jax.ShapeDtypeStruct((NDEV,) + a.shape, a.dtype) for a in gathers]
    out_shape += [jax.ShapeDtypeStruct(a.shape, a.dtype) for a in scatters]
    return pl.pallas_call(
        body, name=name, in_specs=[any_spec] * (n + 1), out_specs=[any_spec] * n, out_shape=out_shape,
        scratch_shapes=[pltpu.SemaphoreType.DMA((n, NDEV - 1)), pltpu.SemaphoreType.DMA((n, NDEV - 1)),
                        pltpu.SemaphoreType.DMA((n,))],
        compiler_params=pltpu.CompilerParams(has_side_effects=True),
    )(*arrays, after)


def _gather_two_level(name, arrays):
    n = len(arrays)

    def body(*refs):
        ins, outs = refs[:n], refs[n:2 * n]
        send_sems, recv_sems, local_sems = refs[2 * n:]
        x, y, c = lax.axis_index("x"), lax.axis_index("y"), lax.axis_index("c")
        me, sibling = (x, y, c), (x, y, 1 - c)
        chips = [(1 - x, y), (x, 1 - y), (1 - x, 1 - y)]

        def copy(a, k, block, to, src=None):
            dst = outs[a].at[4 * block[0] + 2 * block[1] + block[2]]
            return pltpu.make_async_remote_copy(
                src_ref=dst if src is None else src, dst_ref=dst, send_sem=send_sems.at[a, k], recv_sem=recv_sems.at[a, k],
                device_id=to, device_id_type=pl.DeviceIdType.MESH)

        mine = [pltpu.make_async_copy(ins[a], outs[a].at[4 * x + 2 * y + c], local_sems.at[a]) for a in range(n)]
        for cp in mine:
            cp.start()
        first = []
        for a in range(n):
            first.append(copy(a, 0, me, sibling, src=ins[a]))
            first += [copy(a, 1 + j, me, (*chip, c), src=ins[a]) for j, chip in enumerate(chips)]
        for cp in first:
            cp.start()
        passed = [[copy(a, 4 + j, (*chip, c), sibling) for j, chip in enumerate(chips)] for a in range(n)]
        for j, chip in enumerate(chips):
            for a in range(n):
                copy(a, 1 + j, (*chip, c), me).wait_recv()
                passed[a][j].start()
        for a in range(n):
            copy(a, 0, sibling, me).wait_recv()
            for j, chip in enumerate(chips):
                copy(a, 4 + j, (*chip, 1 - c), me).wait_recv()
        for cp in first + [cp for row in passed for cp in row]:
            cp.wait_send()
        for cp in mine:
            cp.wait()

    any_spec = pl.BlockSpec(memory_space=pl.ANY)
    return pl.pallas_call(
        body, name=name, in_specs=[any_spec] * n, out_specs=[any_spec] * n,
        out_shape=[jax.ShapeDtypeStruct((NDEV,) + a.shape, a.dtype) for a in arrays],
        scratch_shapes=[pltpu.SemaphoreType.DMA((n, NDEV - 1)), pltpu.SemaphoreType.DMA((n, NDEV - 1)),
                        pltpu.SemaphoreType.DMA((n,))],
        compiler_params=pltpu.CompilerParams(has_side_effects=True),
    )(*arrays)


_HBM = pl.BlockSpec(memory_space=pltpu.HBM)
_SEM = pl.BlockSpec(memory_space=pltpu.SEMAPHORE)
_EFFECT = pltpu.SideEffectType.DATAFLOW_SIDE_EFFECTING


def _split_copies(srcs, lands, send_sems, recv_sems, n_g):
    x, y, c = lax.axis_index("x"), lax.axis_index("y"), lax.axis_index("c")
    me = 4 * x + 2 * y + c
    copies = []
    for rel in range(1, NDEV):
        px, py, pc = x ^ (rel >> 2), y ^ ((rel >> 1) & 1), c ^ (rel & 1)
        peer = 4 * px + 2 * py + pc
        for a in range(len(srcs)):
            copies.append(pltpu.make_async_remote_copy(
                src_ref=srcs[a] if a < n_g else srcs[a].at[peer], dst_ref=lands[a].at[me],
                send_sem=send_sems.at[a * (NDEV - 1) + rel - 1], recv_sem=recv_sems.at[a * (NDEV - 1) + rel - 1],
                device_id=(px, py, pc), device_id_type=pl.DeviceIdType.MESH))
    return copies


def _exchange_start(name, gathers, scatters, after):
    arrays = list(gathers) + list(scatters)
    n_g, n = len(gathers), len(arrays)
    lands = [lax.empty((NDEV,) + a.shape, a.dtype) for a in gathers] + [lax.empty(a.shape, a.dtype) for a in scatters]

    def body(*refs):
        send_sems, recv_sems = refs[2 * n + 1], refs[2 * n + 2]
        for cp in _split_copies(refs[:n], refs[n:2 * n], send_sems, recv_sems, n_g):
            cp.start()
        refs[-1][...] = jnp.zeros_like(refs[-1])

    sems = pltpu.SemaphoreType.DMA((n * (NDEV - 1),))
    out = pl.pallas_call(
        body, name=name, in_specs=[_HBM] * (2 * n) + [pl.BlockSpec(memory_space=pl.ANY)],
        out_specs=(_SEM, _SEM, *[_HBM] * (2 * n), pl.BlockSpec(memory_space=pltpu.VMEM)),
        out_shape=(sems, sems, *[pltpu.HBM(a.shape, a.dtype) for a in arrays + lands], jax.ShapeDtypeStruct((8, LANE), F32)),
        input_output_aliases={i: 2 + i for i in range(2 * n)},
        compiler_params=pltpu.CompilerParams(has_side_effects=_EFFECT),
    )(*[pltpu.with_memory_space_constraint(a, pltpu.HBM) for a in arrays + lands], after)
    return out[0], out[1], list(out[2:2 + 2 * n]), out[-1], n_g


def _exchange_wait(name, started, after):
    send_sems, recv_sems, thru, _, n_g = started
    n = len(thru) // 2

    def body(*refs):
        for cp in _split_copies(refs[:n], refs[n:2 * n], refs[2 * n], refs[2 * n + 1], n_g):
            cp.wait_send()
            cp.wait_recv()

    out = pl.pallas_call(
        body, name=name, in_specs=[_HBM] * (2 * n) + [_SEM, _SEM, pl.BlockSpec(memory_space=pl.ANY)],
        out_specs=[_HBM] * (2 * n), out_shape=[pltpu.HBM(a.shape, a.dtype) for a in thru],
        input_output_aliases={i: i for i in range(2 * n)},
        compiler_params=pltpu.CompilerParams(has_side_effects=_EFFECT),
    )(*thru, send_sems, recv_sems, after)
    me = 4 * lax.axis_index("x") + 2 * lax.axis_index("y") + lax.axis_index("c")
    full = []
    for a in range(n):
        own = out[a][None] if a < n_g else lax.dynamic_index_in_dim(out[a], me, 0, keepdims=True)
        full.append(lax.dynamic_update_index_in_dim(out[n + a], own, me, 0))
    return full


def _adamw(name, staged, w, m, v):
    lead = w.ndim == 3
    r, c = w.shape[-2:]
    tr = _pick(r, 256, 8)

    def body(st_ref, w_ref, m_ref, v_ref, g_ref, d_ref, nm_ref, nv_ref):
        at = (lambda ref: ref.at[0]) if lead else (lambda ref: ref)
        g = st_ref[0].astype(F32)
        for k in range(1, NDEV):
            g = g + st_ref[k].astype(F32)
        m_new = ADAM_B1 * at(m_ref)[...] + (1.0 - ADAM_B1) * g
        v_new = ADAM_B2 * at(v_ref)[...] + (1.0 - ADAM_B2) * jnp.square(g)
        m_hat = m_new / (1.0 - ADAM_B1 ** ADAM_STEP)
        v_hat = v_new / (1.0 - ADAM_B2 ** ADAM_STEP)
        at(g_ref)[...] = g
        at(d_ref)[...] = -ADAM_LR * (m_hat / (jnp.sqrt(v_hat) + ADAM_EPS) + ADAM_WD * at(w_ref)[...])
        at(nm_ref)[...] = m_new
        at(nv_ref)[...] = v_new

    blk = pl.BlockSpec((1, tr, c), lambda i: (0, i, 0)) if lead else pl.BlockSpec((tr, c), lambda i: (i, 0))
    return pl.pallas_call(
        body, name=name, grid=(r // tr,), in_specs=[pl.BlockSpec((NDEV, tr, c), lambda i: (0, i, 0)), blk, blk, blk],
        out_specs=[blk] * 4, out_shape=[jax.ShapeDtypeStruct(w.shape, F32)] * 4,
        compiler_params=_cparams(("parallel",), 48),
    )(staged, w, m, v)


def _pack(parts):
    flat = jnp.concatenate([p.reshape(-1).astype(F32) for p in parts])
    pad = (-flat.shape[0]) % (8 * LANE)
    return jnp.pad(flat, (0, pad)).reshape(-1, LANE)


def _unpack(slab, shapes):
    flat, out, off = slab.reshape(-1), [], 0
    for s in shapes:
        n = 1
        for dim in s:
            n *= dim
        out.append(flat[off:off + n].reshape(s))
        off += n
    return out


def _to_shards(full, axis):
    shp = full.shape
    t = full.reshape(shp[:axis] + (NDEV, shp[axis] // NDEV) + shp[axis + 1:])
    return jnp.moveaxis(t, axis, 0)


def _from_shards(g, axis):
    t = jnp.moveaxis(g, 0, axis)
    shp = t.shape
    return t.reshape(shp[:axis] + (shp[axis] * shp[axis + 1],) + shp[axis + 2:])


def kernel(x, meta_tokens, norm_mix_w, w_in, dn_conv_w, dn_a_log, dn_dt_bias, dn_norm_w, m2_conv_w, m2_conv_b, m2_a_log, m2_dt_bias, m2_d, m2_norm_w, w_out, norm_ffn_w, ffn_up, ffn_conv_w, ffn_down, norm_final_w, loss_target, m_meta_tokens, m_norm_mix_w, m_w_in, m_dn_conv_w, m_dn_a_log, m_dn_dt_bias, m_dn_norm_w, m_m2_conv_w, m_m2_conv_b, m_m2_a_log, m_m2_dt_bias, m_m2_d, m_m2_norm_w, m_w_out, m_norm_ffn_w, m_ffn_up, m_ffn_conv_w, m_ffn_down, m_norm_final_w, v_meta_tokens, v_norm_mix_w, v_w_in, v_dn_conv_w, v_dn_a_log, v_dn_dt_bias, v_dn_norm_w, v_m2_conv_w, v_m2_conv_b, v_m2_a_log, v_m2_dt_bias, v_m2_d, v_m2_norm_w, v_w_out, v_norm_ffn_w, v_ffn_up, v_ffn_conv_w, v_ffn_down, v_norm_final_w):
    seq, d = x.shape[1], x.shape[2]
    t_rows = seq + CH
    nc = t_rows // CH
    dnh, m2h = d // HD, d // M2P
    dff = ffn_down.shape[1] * NDEV
    xbc_w = d + 2 * M2G * NST
    assert seq % CH == 0 and d % (2 * M2P * M2G) == 0 and 2 * dnh + m2h <= LANE
    hb_f = max(h for h in (16, 8, 4, 2, 1) if dnh % h == 0)
    hb_b = max(h for h in (8, 4, 2, 1) if dnh % h == 0)
    tm_rw = _pick(t_rows, 208, 16)
    conv_chunk = _pick(t_rows, min(CONV_CHUNK, t_rows // 3), 16)

    small_sharded = [meta_tokens, dn_conv_w[0], m2_conv_w[0], ffn_conv_w[0]]
    small_shapes = [p.shape for p in small_sharded]
    g_win, g_small = _gather_two_level("gather_w_in", [w_in[0].astype(WIRE).T, _pack(small_sharded)])
    rest = _exchange_start("gather_rest_start", [w_out[0].astype(WIRE), ffn_up[0].astype(WIRE), ffn_down[0].astype(WIRE)], [],
                           g_small)
    win_t = g_win.reshape(-1, d)
    small_full = [_unpack(g_small[k], small_shapes) for k in range(NDEV)]
    meta_f, dnconv_f, m2conv_f, ffnconv_f = [jnp.concatenate([small_full[k][i] for k in range(NDEV)], axis=-1) for i in range(4)]

    o_z, o_b, o_a = 3 * d, 4 * d, 4 * d + dnh
    o_m2z = 4 * d + 2 * dnh
    o_xbc, o_dt = o_m2z + d, o_m2z + d + xbc_w
    w_all_t = jnp.concatenate([win_t[:o_b], win_t[o_m2z:o_dt], win_t[o_b:o_m2z], win_t[o_dt:],
                               jnp.zeros((LANE - 2 * dnh - m2h, d), WIRE)], axis=0)
    seg_cols = {"q": (0, d), "k": (d, d), "v": (2 * d, d), "z": (3 * d, d), "m2z": (4 * d, d), "xbc": (5 * d, xbc_w),
                "sm": (5 * d + xbc_w, LANE)}

    h0 = jnp.concatenate([jnp.zeros((PADR, d), F32), meta_f, x[0]], axis=0)
    valid = lambda rows: rows >= PADR

    def norm_fwd(name, h, w):
        return _rw(name, lambda rows, j, hv, wv: (_rms(hv, wv),), [("r", h, d, _c0), ("p", w, d, _c0)],
                   [("r", d, d, _c0, MXU)], t_rows, tm_rw)[0]

    hn1 = norm_fwd("norm_mix", h0, norm_mix_w)
    proj = {s: _mm("proj_" + s, hn1, w_all_t, tb=True, b_rows=seg_cols[s], dep=rest[3]) for s in seg_cols}

    def dn_post(sec, cv):
        s = _silu(cv)
        if sec < 2:
            s = s * lax.rsqrt(jnp.sum(s * s, axis=-1, keepdims=True) + EPS)
        if sec == 0:
            s = s * (HD ** -0.5)
        return s

    def dn_prep(sec, name):
        def fn(rows, own, last, wins, pars):
            return [jnp.where(valid(rows), dn_post(sec, _conv(wins[0], pars[0])), 0.0)], []
        wc = dnconv_f[:, sec * d:(sec + 1) * d]
        return _cv("dn_prep_" + name, fn, [(proj[name], _cj)], [(wc, _cj)], [(d, _cj, F32)], [], t_rows, dnh)[0]

    q_act, k_act, v_act = dn_prep(0, "q"), dn_prep(1, "k"), dn_prep(2, "v")

    lane = lambda: lax.broadcasted_iota(jnp.int32, (1, LANE), 1)

    def lanes_of(vec, off):
        return jnp.pad(vec.astype(F32), ((0, 0), (off, LANE - off - vec.shape[1])))

    gate_params = [lanes_of(dn_a_log, dnh), lanes_of(dn_dt_bias, dnh), lanes_of(m2_a_log, 2 * dnh), lanes_of(m2_dt_bias, 2 * dnh)]

    def gates(rows, sm, p_alog, p_dtb, p_malog, p_mdtb):
        ln = lane()
        is_b, is_g = ln < dnh, jnp.logical_and(ln >= dnh, ln < 2 * dnh)
        is_d = jnp.logical_and(ln >= 2 * dnh, ln < 2 * dnh + m2h)
        beta = jax.nn.sigmoid(sm)
        gdec = -jnp.exp(p_alog) * _softplus(sm + p_dtb)
        dt = _softplus(sm + p_mdtb)
        am = dt * (-jnp.exp(p_malog))
        ok = valid(rows)
        g1 = jnp.where(ok, jnp.where(is_b, beta, jnp.where(is_g, gdec, jnp.where(is_d, dt, 0.0))), 0.0)
        g2 = jnp.where(jnp.logical_and(ok, is_d), am, 0.0)
        return g1, g2

    gate_ins = [("r", proj["sm"], LANE, _c0)] + [("p", p, LANE, _c0) for p in gate_params]
    g1, g2 = _rw("gates", lambda rows, j, *a: gates(rows, *a), gate_ins,
                 [("r", LANE, LANE, _c0, F32), ("r", LANE, LANE, _c0, F32)], t_rows, tm_rw)

    def head_rows(cols, per):
        n = cols.shape[1]
        return cols.reshape(nc, CH, n // per, per).transpose(2, 0, 3, 1)

    def head_cols(rows_):
        ngrp, _, per, _ = rows_.shape
        return rows_.transpose(1, 3, 0, 2).reshape(t_rows, ngrp * per)

    beta_r, gdec_r = head_rows(g1[:, :dnh], hb_f), head_rows(g1[:, dnh:2 * dnh], hb_f)
    hpg = m2h // M2G
    m2_rows = lambda gps: (head_rows(g1[:, 2 * dnh:2 * dnh + m2h], hpg * gps), head_rows(g2[:, 2 * dnh:2 * dnh + m2h], hpg * gps))

    o_dn, dn_states, dn_tinv = _gdn_fwd(q_act, k_act, v_act, beta_r, gdec_r, hb_f)

    def dn_out(o, z, w):
        outs = []
        for h in range(dnh):
            sl = slice(h * HD, (h + 1) * HD)
            outs.append(_rms(o[:, sl], w) * _silu(z[:, sl]))
        return jnp.concatenate(outs, axis=1)

    mixed_dn = _rw("dn_out", lambda rows, j, o, z, w: (dn_out(o, z, w),),
                   [("r", o_dn, d, _c0), ("r", proj["z"], d, _c0), ("p", dn_norm_w, HD, _c0)], [("r", d, d, _c0, MXU)],
                   t_rows, tm_rw)[0]

    def m2_prep(rows, own, last, wins, pars):
        return [jnp.where(valid(rows), _silu(_conv(wins[0], pars[0]) + pars[1][0]), 0.0)], []

    xbc_act = _cv("m2_prep", m2_prep, [(proj["xbc"], _cj)], [(m2conv_f, _cj), (m2_conv_b, _cj)], [(xbc_w, _cj, F32)], [],
                  t_rows, xbc_w // LANE)[0]
    y_ssd, m2_states = _ssd_fwd(xbc_act, *m2_rows(SSD_GPS_F), d, SSD_GPS_F)

    d_lanes = jnp.repeat(m2_d.astype(F32), M2P, axis=1)
    gw = d // M2G

    def m2_out(ys, xs, z, dl, nw):
        yv = (ys + dl * xs) * _silu(z)
        outs = []
        for gi in range(M2G):
            sl = slice(gi * gw, (gi + 1) * gw)
            outs.append(_rms(yv[:, sl], nw[:, sl]))
        return jnp.concatenate(outs, axis=1)

    m2_out_ins = [("r", y_ssd, d, _c0), ("r", xbc_act, d, _c0), ("r", proj["m2z"], d, _c0), ("p", d_lanes, d, _c0),
                  ("p", m2_norm_w, d, _c0)]
    mixed_m2 = _rw("m2_out", lambda rows, j, *a: (m2_out(*a),), m2_out_ins, [("r", d, d, _c0, MXU)], t_rows, tm_rw)[0]

    mixed = jnp.concatenate([mixed_dn, mixed_m2], axis=1)
    g_wout, g_wup, g_wdown = _exchange_wait("gather_rest_wait", rest, mixed)
    wout = _from_shards(g_wout, 0)
    wup = _from_shards(g_wup, 1)
    wdown = _from_shards(g_wdown, 0)
    up_g, up_v = (0, dff), (dff, dff)
    h1 = _mm("out_proj", mixed, wout, add=h0)
    hn2 = norm_fwd("norm_ffn", h1, norm_ffn_w)
    u_g, u_v = _mm("ffn_up_g", hn2, wup, b_cols=up_g), _mm("ffn_up_v", hn2, wup, b_cols=up_v)
    fc_g, fc_v = ffnconv_f[:, :dff], ffnconv_f[:, dff:]

    def ffn_act(rows, own, last, wins, pars):
        return [jnp.where(valid(rows), _silu(_conv(wins[0], pars[0])) * _conv(wins[1], pars[1]), 0.0)], []

    act = _cv("ffn_act", ffn_act, [(u_g, _cj), (u_v, _cj)], [(fc_g, _cj), (fc_v, _cj)], [(dff, _cj, MXU)], [],
              t_rows, dff // LANE)[0]
    h2 = _mm("ffn_down", act, wdown, add=h1, tk=1408)

    def loss_fn(hv, wf, tgt, rows):
        err = jnp.where(rows >= CH, _rms(hv, wf) - tgt, 0.0)
        return 0.5 * jnp.sum(jnp.mean(err * err, axis=-1, keepdims=True), axis=0, keepdims=True)

    def final(rows, j, hv, wf, tgt):
        loss, vjp = jax.vjp(lambda a, b: loss_fn(a, b, tgt, rows), hv, wf)
        dh, dw = vjp(jnp.ones((1, 1), F32))
        return dh, dh, dw, jnp.broadcast_to(loss, (1, LANE))

    wf2 = norm_final_w.reshape(1, d)
    dh2, dh2_m, d_wf, loss_part = _rw(
        "loss_head", final, [("r", h2, d, _c0), ("p", wf2, d, _c0), ("r", loss_target[0], d, _c0, lambda i: jnp.maximum(i - 1, 0))],
        [("r", d, d, _c0, F32), ("r", d, d, _c0, MXU), ("p", 1, d, d, _c0), ("p", 1, LANE, LANE, _c0)], t_rows, CH)
    loss = lax.psum(loss_part[0, 0], MESH_AXES)

    d_act = _mm("d_act", dh2_m, wdown, tb=True)
    gw_down = _mm("gw_down", act, dh2_m, ta=True, tm=1408, tn=1024, tk=2080, out_dtype=WIRE)
    x_down = _exchange_start("grad_down_start", [], [_to_shards(gw_down, 0).astype(WIRE)], gw_down)

    def t_end(last):
        return t_rows if last else None

    def ffn_act_bwd(rows, own, last, wins, pars):
        (ug, uv, da), (wg, wv) = wins, pars
        cg, cv = _conv(ug, wg), _conv(uv, wv)
        _, vjp = jax.vjp(lambda a, b: _silu(a) * b, cg, cv)
        dcg, dcv = vjp(jnp.where(valid(rows), da, 0.0))
        return ([_conv_t(dcg, wg, rows, t_end(last)), _conv_t(dcv, wv, rows, t_end(last))],
                [_conv_w(jnp.where(own, dcg, 0.0), ug, len(wg)), _conv_w(jnp.where(own, dcv, 0.0), uv, len(wv))])

    kf = fc_g.shape[0]
    du_g, du_v, g_fc_g, g_fc_v = _cv(
        "ffn_act_bwd", ffn_act_bwd, [(u_g, _cj), (u_v, _cj), (d_act, _cj)], [(fc_g, _cj), (fc_v, _cj)],
        [(dff, _cj, MXU), (dff, _cj, MXU)], [(kf, dff, _cj), (kf, dff, _cj)], t_rows, dff // LANE, chunk=conv_chunk)
    gw_up_g = _mm("gw_up_g", hn2, du_g, ta=True, tm=1024, tn=1408, tk=2080, out_dtype=WIRE, dep=x_down[3])
    gw_up_v = _mm("gw_up_v", hn2, du_v, ta=True, tm=1024, tn=1408, tk=2080, out_dtype=WIRE)
    gw_up_full = jnp.concatenate([gw_up_g, gw_up_v], axis=1)
    x_up = _exchange_start("grad_up_start", [], [_to_shards(gw_up_full, 1).astype(WIRE)], gw_up_full)
    d_hn2 = _mm("d_hn2_v", du_v, wup, b_cols=up_v, tb=True, tk=1408, dep=x_up[3],
                add=_mm("d_hn2_g", du_g, wup, b_cols=up_g, tb=True, tk=1408, tn=2048, dep=x_up[3]))

    def norm_bwd(name, h, w, dy, dres):
        def fn(rows, j, hv, wv, dyv, dr):
            _, vjp = jax.vjp(_rms, hv, wv)
            dh, dw = vjp(dyv)
            dh = dh + dr
            return dh, dh, dw
        return _rw(name, fn, [("r", h, d, _c0), ("p", w, d, _c0), ("r", dy, d, _c0), ("r", dres, d, _c0)],
                   [("r", d, d, _c0, F32), ("r", d, d, _c0, MXU), ("p", 1, d, d, _c0)], t_rows, tm_rw)

    dh1, dh1_m, g_norm_ffn = norm_bwd("norm_ffn_bwd", h1, norm_ffn_w, d_hn2, dh2)

    gw_out = _mm("gw_out", mixed, dh1_m, ta=True, tm=1024, tn=1024, tk=2080, out_dtype=WIRE)
    x_out = _exchange_start("grad_out_start", [], [_to_shards(gw_out, 0).astype(WIRE)], gw_out)
    d_mixed = _mm("d_mixed", dh1_m, wout, tb=True, dep=x_out[3])

    gw_seg = {}

    def gw_in(seg, dseg_arr):
        gw_seg[seg] = _mm("gw_in_" + seg, hn1, dseg_arr, ta=True, tm=1024, tn=1024, tk=2080, out_dtype=WIRE)

    def m2_out_bwd(rows, j, ys, xs, z, dl, nw, dy):
        _, vjp = jax.vjp(m2_out, ys, xs, z, dl, nw)
        return vjp(dy)

    dy_ssd, dxs_skip, d_m2z, g_d_lanes, g_m2_norm = _rw(
        "m2_out_bwd", m2_out_bwd, m2_out_ins + [("r", d_mixed, d, lambda j: 1)],
        [("r", d, d, _c0, F32), ("r", d, d, _c0, F32), ("r", d, d, _c0, MXU), ("p", 1, d, d, _c0), ("p", 1, d, d, _c0)],
        t_rows, _pick(t_rows, 208, 16))
    gw_in("m2z", d_m2z)

    def fold_heads(vec_ref, out_ref):
        r = lax.broadcasted_iota(jnp.int32, (d, LANE), 0)
        c = lax.broadcasted_iota(jnp.int32, (d, LANE), 1)
        out_ref[...] = _dgh(vec_ref[...], jnp.where(jnp.logical_and(r >= c * M2P, r < (c + 1) * M2P), 1.0, 0.0), 1, 0)

    g_m2_d = pl.pallas_call(fold_heads, name="fold_m2_d", out_shape=jax.ShapeDtypeStruct((1, LANE), F32))(g_d_lanes)

    dxs, db_ssd, dc_ssd, ddt_r, dam_r = _ssd_bwd(xbc_act, *m2_rows(SSD_GPS_B), m2_states, dy_ssd, d, SSD_GPS_B)


    def m2_prep_bwd(rows, own, last, wins, pars):
        (p, *ds), (w, b) = wins, pars
        _, vjp = jax.vjp(_silu, _conv(p, w) + b[0])
        dpre, = vjp(jnp.where(valid(rows), functools.reduce(lambda a_, b_: a_ + b_, ds), 0.0))
        dpre_own = jnp.where(own, dpre, 0.0)
        return [_conv_t(dpre, w, rows, t_end(last))], [_conv_w(dpre_own, p, len(w)), [jnp.sum(dpre_own, axis=0, keepdims=True)]]

    def m2_prep_bwd_call(name, off, width, d_arrs):
        at = lambda j, blk0=off // LANE: blk0 + j
        return _cv(name, m2_prep_bwd, [(proj["xbc"], at)] + [(a, _cj) for a in d_arrs], [(m2conv_f, at), (m2_conv_b, at)],
                   [(width, _cj, MXU)], [(m2conv_f.shape[0], width, _cj), (1, width, _cj)], t_rows, width // LANE, chunk=conv_chunk)

    dp_xs, gcw_xs, gcb_xs = m2_prep_bwd_call("m2_prep_bwd_x", 0, d, [dxs, dxs_skip])
    dp_b, gcw_b, gcb_b = m2_prep_bwd_call("m2_prep_bwd_b", d, M2G * NST, [db_ssd])
    dp_c, gcw_c, gcb_c = m2_prep_bwd_call("m2_prep_bwd_c", d + M2G * NST, M2G * NST, [dc_ssd])
    d_pxbc = jnp.concatenate([dp_xs, dp_b, dp_c], axis=1)
    gw_in("xbc", d_pxbc)
    g_m2_conv = jnp.concatenate([gcw_xs, gcw_b, gcw_c], axis=1)
    g_m2_conv_b = jnp.concatenate([gcb_xs, gcb_b, gcb_c], axis=1)

    def dn_out_bwd(rows, j, o, z, w, dy):
        _, vjp = jax.vjp(dn_out, o, z, w)
        return vjp(dy)

    d_o, d_z, g_dn_norm = _rw(
        "dn_out_bwd", dn_out_bwd,
        [("r", o_dn, d, _c0), ("r", proj["z"], d, _c0), ("p", dn_norm_w, HD, _c0), ("r", d_mixed, d, _c0)],
        [("r", d, d, _c0, F32), ("r", d, d, _c0, MXU), ("p", 1, HD, HD, _c0)], t_rows, _pick(t_rows, 208, 16))
    gw_in("z", d_z)

    dq, dk, dv, dbeta_r, dgdec_r = _gdn_bwd(q_act, k_act, v_act, head_rows(g1[:, :dnh], hb_b),
                                            head_rows(g1[:, dnh:2 * dnh], hb_b), dn_states, dn_tinv, d_o, hb_b)

    def dn_prep_bwd(sec, name, dact):
        def fn(rows, own, last, wins, pars):
            (p, da), (w,) = wins, pars
            _, vjp = jax.vjp(functools.partial(dn_post, sec), _conv(p, w))
            dcv, = vjp(jnp.where(valid(rows), da, 0.0))
            return [_conv_t(dcv, w, rows, t_end(last))], [_conv_w(jnp.where(own, dcv, 0.0), p, len(w))]
        wc = dnconv_f[:, sec * d:(sec + 1) * d]
        return _cv("dn_prep_bwd_" + name, fn, [(proj[name], _cj), (dact, _cj)], [(wc, _cj)], [(d, _cj, MXU)],
                   [(wc.shape[0], d, _cj)], t_rows, dnh, chunk=conv_chunk if sec == 2 else None)

    (dp_q, gcw_q), (dp_k, gcw_k), (dp_v, gcw_v) = dn_prep_bwd(0, "q", dq), dn_prep_bwd(1, "k", dk), dn_prep_bwd(2, "v", dv)
    gw_in("q", dp_q), gw_in("k", dp_k), gw_in("v", dp_v)
    g_dn_conv = jnp.concatenate([gcw_q, gcw_k, gcw_v], axis=1)

    zpad = jnp.zeros((t_rows, LANE - 2 * dnh - m2h), F32)
    dg1 = jnp.concatenate([head_cols(dbeta_r), head_cols(dgdec_r), head_cols(ddt_r), zpad], axis=1)
    dg2 = jnp.concatenate([jnp.zeros((t_rows, 2 * dnh), F32), head_cols(dam_r), zpad], axis=1)

    def gates_bwd(rows, j, sm, pa, pb, pc, pd, d1, d2):
        _, vjp = jax.vjp(lambda *a: gates(rows, *a), sm, pa, pb, pc, pd)
        return vjp((d1, d2))

    dp_sm, g_pa, g_pb, g_pc, g_pd = _rw(
        "gates_bwd", gates_bwd, gate_ins + [("r", dg1, LANE, _c0), ("r", dg2, LANE, _c0)],
        [("r", LANE, LANE, _c0, MXU)] + [("p", 1, LANE, LANE, _c0)] * 4, t_rows, tm_rw)

    dseg = {"q": dp_q, "k": dp_k, "v": dp_v, "z": d_z, "m2z": d_m2z, "xbc": d_pxbc, "sm": dp_sm}
    gw_in("sm", dp_sm)
    gsm = gw_seg["sm"]
    gw_in_full = jnp.concatenate([gw_seg["q"], gw_seg["k"], gw_seg["v"], gw_seg["z"], gsm[:, :2 * dnh], gw_seg["m2z"],
                                  gw_seg["xbc"], gsm[:, 2 * dnh:2 * dnh + m2h]], axis=1)
    x_in = _exchange_start("grad_in_start", [], [_to_shards(gw_in_full, 1).astype(WIRE)], gw_in_full)
    d_hn1 = None
    for s in dseg:
        d_hn1 = _mm("d_hn1_" + s, dseg[s], w_all_t, b_rows=seg_cols[s], tk=2048, add=d_hn1, dep=x_in[3])
    dh0, _, g_norm_mix = norm_bwd("norm_mix_bwd", h0, norm_mix_w, d_hn1, dh1)

    g_ffn_conv = jnp.concatenate([g_fc_g, g_fc_v], axis=1)
    small_parts = [_to_shards(dh0[PADR:CH], 1), _to_shards(g_dn_conv, 1), _to_shards(g_m2_conv, 1), _to_shards(g_ffn_conv, 1)]
    small_scatter = jnp.stack([_pack([p[k] for p in small_parts]) for k in range(NDEV)])

    rep_names = ["norm_mix_w", "dn_a_log", "dn_dt_bias", "dn_norm_w", "m2_conv_b", "m2_a_log", "m2_dt_bias", "m2_d",
                 "m2_norm_w", "norm_ffn_w", "norm_final_w"]
    rep_grads = [g_norm_mix, g_pa[:, dnh:2 * dnh], g_pb[:, dnh:2 * dnh], g_dn_norm, g_m2_conv_b, g_pc[:, 2 * dnh:2 * dnh + m2h],
                 g_pd[:, 2 * dnh:2 * dnh + m2h], g_m2_d[:, :m2h], g_m2_norm, g_norm_ffn, d_wf.reshape(d)]

    weights = dict(meta_tokens=meta_tokens, norm_mix_w=norm_mix_w, w_in=w_in, dn_conv_w=dn_conv_w, dn_a_log=dn_a_log,
                   dn_dt_bias=dn_dt_bias, dn_norm_w=dn_norm_w, m2_conv_w=m2_conv_w, m2_conv_b=m2_conv_b, m2_a_log=m2_a_log,
                   m2_dt_bias=m2_dt_bias, m2_d=m2_d, m2_norm_w=m2_norm_w, w_out=w_out, norm_ffn_w=norm_ffn_w, ffn_up=ffn_up,
                   ffn_conv_w=ffn_conv_w, ffn_down=ffn_down, norm_final_w=norm_final_w)
    mom1 = dict(meta_tokens=m_meta_tokens, norm_mix_w=m_norm_mix_w, w_in=m_w_in, dn_conv_w=m_dn_conv_w, dn_a_log=m_dn_a_log,
                dn_dt_bias=m_dn_dt_bias, dn_norm_w=m_dn_norm_w, m2_conv_w=m_m2_conv_w, m2_conv_b=m_m2_conv_b,
                m2_a_log=m_m2_a_log, m2_dt_bias=m_m2_dt_bias, m2_d=m_m2_d, m2_norm_w=m_m2_norm_w, w_out=m_w_out,
                norm_ffn_w=m_norm_ffn_w, ffn_up=m_ffn_up, ffn_conv_w=m_ffn_conv_w, ffn_down=m_ffn_down,
                norm_final_w=m_norm_final_w)
    mom2 = dict(meta_tokens=v_meta_tokens, norm_mix_w=v_norm_mix_w, w_in=v_w_in, dn_conv_w=v_dn_conv_w, dn_a_log=v_dn_a_log,
                dn_dt_bias=v_dn_dt_bias, dn_norm_w=v_dn_norm_w, m2_conv_w=v_m2_conv_w, m2_conv_b=v_m2_conv_b,
                m2_a_log=v_m2_a_log, m2_dt_bias=v_m2_dt_bias, m2_d=v_m2_d, m2_norm_w=v_m2_norm_w, w_out=v_w_out,
                norm_ffn_w=v_norm_ffn_w, ffn_up=v_ffn_up, ffn_conv_w=v_ffn_conv_w, ffn_down=v_ffn_down,
                norm_final_w=v_norm_final_w)
    res = {}

    def adam_big(name, started, after):
        staged, = _exchange_wait("grad_" + name + "_wait", started, after)
        outs = _adamw("adamw_" + name, staged, weights[name], mom1[name], mom2[name])
        res[name] = tuple(outs)
        return outs[1]

    done = adam_big("ffn_down", x_down, dh0)
    done = adam_big("ffn_up", x_up, done)
    done = adam_big("w_out", x_out, done)
    st_rep, st_small = _exchange("exchange_small_grads", [_pack(rep_grads)], [small_scatter], done)
    adam_big("w_in", x_in, st_small)

    def adam_packed(label, staged, names):
        shapes = [weights[nm].shape for nm in names]
        outs = _adamw(label, staged, *[_pack([src[nm] for nm in names]) for src in (weights, mom1, mom2)])
        unpacked = [_unpack(o, shapes) for o in outs]
        for i, nm in enumerate(names):
            res[nm] = tuple(u[i] for u in unpacked)

    adam_packed("adamw_small_sharded", st_small, ["meta_tokens", "dn_conv_w", "m2_conv_w", "ffn_conv_w"])
    adam_packed("adamw_replicated", st_rep, rep_names)

    order = list(weights)
    grad_x = dh0[CH:][None]
    return (loss, grad_x, *[res[nm][0] for nm in order], *[res[nm][1] for nm in order], *[res[nm][2] for nm in order],
            *[res[nm][3] for nm in order])
```

```python
import functools
import math

import jax
import jax.numpy as jnp
from jax import lax
from jax.experimental import pallas as pl
from jax.experimental.pallas import tpu as pltpu

F32 = jnp.float32
MXU = jnp.bfloat16
WIRE = jnp.bfloat16
HI = lax.Precision.HIGH

NDEV = 8
CH = 64
NMETA = 16
PADR = CH - NMETA
EPS = 1e-6
HD = 128
M2P = 64
M2G = 4
SSD_GPS_F, SSD_GPS_B = 4, 1
NST = 128
LANE = 128

ADAM_LR, ADAM_B1, ADAM_B2, ADAM_EPS, ADAM_WD, ADAM_STEP = 0.001, 0.9, 0.999, 1e-08, 0.01, 10

MESH_AXES = ("x", "y", "c")


def _pick(n, target, mult=16):
    best = None
    for t in range(mult, min(n, target) + 1, mult):
        if n % t == 0:
            best = t
    return best if best is not None else n


def _dg(a, b, ca, cb):
    return lax.dot_general(a.astype(MXU), b.astype(MXU), (((ca,), (cb,)), ((), ())), preferred_element_type=F32)


def _dgh(a, b, ca, cb):
    return lax.dot_general(a, b, (((ca,), (cb,)), ((), ())), precision=HI, preferred_element_type=F32)


def _silu(x):
    return x * jax.nn.sigmoid(x)


def _softplus(x):
    return jnp.maximum(x, 0.0) + jnp.log1p(jnp.exp(-jnp.abs(x)))


def _rms(x, w):
    return x * lax.rsqrt(jnp.mean(x * x, axis=-1, keepdims=True) + EPS) * w


def _cparams(sem, vmem_mb):
    return pltpu.CompilerParams(dimension_semantics=sem, vmem_limit_bytes=vmem_mb << 20)


def _mm(name, a, b, *, ta=False, tb=False, add=None, out_dtype=F32, tm=1040, tn=1024, tk=2048, dep=None,
        b_rows=None, b_cols=None):
    m = a.shape[1] if ta else a.shape[0]
    (rs, rw), (cs, cw) = b_rows or (0, b.shape[0]), b_cols or (0, b.shape[1])
    (n, ns), (kdim, ks) = ((rw, rs), (cw, cs)) if tb else ((cw, cs), (rw, rs))
    assert kdim == (a.shape[0] if ta else a.shape[1])
    tm = _pick(m, tm, 128 if ta else 16)
    tn = _pick(math.gcd(n, ns), tn, 128)
    tk = _pick(math.gcd(kdim, ks), tk, 16 if (ta and not tb) else 128)
    nk = kdim // tk
    bj0, bk0 = ns // tn, ks // tk
    ca, cb = (0 if ta else 1), (1 if tb else 0)

    def body(*refs):
        a_ref, b_ref = refs[0], refs[1]
        add_ref = refs[2] if add is not None else None
        if nk == 1:
            r = _dg(a_ref[...], b_ref[...], ca, cb)
            if add_ref is not None:
                r = r + add_ref[...].astype(F32)
            refs[-1][...] = r.astype(refs[-1].dtype)
            return
        o_ref, acc = refs[-2], refs[-1]
        k = pl.program_id(2)

        @pl.when(k == 0)
        def _():
            acc[...] = jnp.zeros_like(acc)

        acc[...] += _dg(a_ref[...], b_ref[...], ca, cb)

        @pl.when(k == nk - 1)
        def _():
            r = acc[...]
            if add_ref is not None:
                r = r + add_ref[...].astype(F32)
            o_ref[...] = r.astype(o_ref.dtype)

    a_spec = pl.BlockSpec((tk, tm), lambda i, j, k: (k, i)) if ta else pl.BlockSpec((tm, tk), lambda i, j, k: (i, k))
    b_spec = (pl.BlockSpec((tn, tk), lambda i, j, k: (j + bj0, k + bk0)) if tb
              else pl.BlockSpec((tk, tn), lambda i, j, k: (k + bk0, j + bj0)))
    in_specs, ops = [a_spec, b_spec], [a, b]
    if add is not None:
        in_specs.append(pl.BlockSpec((tm, tn), lambda i, j, k: (i, j)))
        ops.append(add)
    if dep is not None:
        in_specs.append(pl.BlockSpec((8, LANE), lambda i, j, k: (0, 0)))
        ops.append(dep)
    return pl.pallas_call(
        body, name=name, grid=(m // tm, n // tn, nk), in_specs=in_specs,
        out_specs=pl.BlockSpec((tm, tn), lambda i, j, k: (i, j)),
        out_shape=jax.ShapeDtypeStruct((m, n), out_dtype),
        scratch_shapes=[pltpu.VMEM((tm, tn), F32)] if nk > 1 else [],
        compiler_params=_cparams(("parallel", "parallel", "arbitrary"), 48),
    )(*ops)


def _rw(name, fn, ins, outs, nrows, tm, ncol=1, vmem_mb=48):
    nrow = nrows // tm
    sub = tm
    in_specs, ops = [], []
    for spec in ins:
        kind, arr, bw, cj = spec[:4]
        ops.append(arr)
        if kind == "r":
            ri = spec[4] if len(spec) > 4 else (lambda i: i)
            in_specs.append(pl.BlockSpec((tm, bw), lambda j, i, cj=cj, ri=ri: (ri(i), cj(j))))
        else:
            in_specs.append(pl.BlockSpec((arr.shape[0], bw), lambda j, i, cj=cj: (0, cj(j))))
    out_shape, out_specs = [], []
    for o in outs:
        if o[0] == "r":
            _, width, bw, cj, dt = o
            out_shape.append(jax.ShapeDtypeStruct((nrows, width), dt))
            out_specs.append(pl.BlockSpec((tm, bw), lambda j, i, cj=cj: (i, cj(j))))
        else:
            _, rows, width, bw, cj = o
            out_shape.append(jax.ShapeDtypeStruct((rows, width), F32))
            out_specs.append(pl.BlockSpec((rows, bw), lambda j, i, cj=cj: (0, cj(j))))
    n_in = len(ins)

    def body(*refs):
        j, i = pl.program_id(0), pl.program_id(1)
        in_refs, out_refs = refs[:n_in], refs[n_in:]
        pars = [ref[...] if spec[0] == "p" else None for spec, ref in zip(ins, in_refs)]

        def one(r0, nr):
            rows = i * tm + r0 + lax.broadcasted_iota(jnp.int32, (nr, 1), 0)
            vals = [par if spec[0] == "p" else ref[pl.ds(r0, nr), :] for spec, ref, par in zip(ins, in_refs, pars)]
            parts = []
            for o, val, ref in zip(outs, fn(rows, j, *vals), out_refs):
                if o[0] == "r":
                    ref[pl.ds(r0, nr), :] = val.astype(ref.dtype)
                else:
                    parts.append(val)
            return parts

        if sub >= tm:
            parts = one(0, tm)
        else:
            zero = [jnp.zeros((1, o[3]), F32) for o in outs if o[0] == "p"]
            parts = lax.fori_loop(
                0, tm // sub, lambda s, acc: [a + b for a, b in zip(acc, one(pl.multiple_of(s * sub, sub), sub))], zero)
        for ref, val in zip([r for o, r in zip(outs, out_refs) if o[0] == "p"], parts):
            @pl.when(i == 0)
            def _(ref=ref):
                ref[...] = jnp.zeros_like(ref)

            ref[...] += val

    return pl.pallas_call(
        body, name=name, grid=(ncol, nrow), in_specs=in_specs, out_specs=out_specs, out_shape=out_shape,
        compiler_params=_cparams(("parallel", "arbitrary"), vmem_mb),
    )(*ops)


def _c0(j):
    return 0


def _cj(j):
    return j


def _shift(x, s):
    if s == 0:
        return x
    return pltpu.roll(x, s % x.shape[0], 0)


def _conv(x, w):
    k = len(w)
    return functools.reduce(lambda a, b: a + b, [w[j] * _shift(x, k - 1 - j) for j in range(k)])


def _conv_t(dy, w, rows, t_end):
    k = len(w)
    terms = []
    for j in range(k):
        s = k - 1 - j
        v = _shift(dy, -s)
        if t_end is not None and s > 0:
            v = jnp.where(rows + s < t_end, v, 0.0)
        terms.append(w[j] * v)
    return functools.reduce(lambda a, b: a + b, terms)


def _conv_w(dy, x, k):
    return [jnp.sum(dy * _shift(x, k - 1 - j), axis=0, keepdims=True) for j in range(k)]


CONV_CHUNK = 320
HALO = 8


def _cv(name, fn, row_ins, par_ins, row_outs, par_outs, nrows, ncol, chunk=None):
    whole = chunk is None
    chunk = nrows if whole else chunk
    n_chunks = nrows // chunk
    assert nrows % chunk == 0 and (whole or n_chunks >= 3)
    n_ri, n_pi, n_ro = len(row_ins), len(par_ins), len(row_outs)

    def body(*refs):
        rin, pin = refs[:n_ri], refs[n_ri:n_ri + n_pi]
        rout, pout = refs[n_ri + n_pi:n_ri + n_pi + n_ro], refs[n_ri + n_pi + n_ro:]
        pars = [[p[pl.ds(r, 1), :] for r in range(p.shape[0])] for p in pin]

        def run(r0, top, bot, last):
            wlen = top + chunk + bot
            w0 = r0 - top if isinstance(r0, int) else pl.multiple_of(r0 - top, HALO)
            local = lax.broadcasted_iota(jnp.int32, (wlen, 1), 0)
            own = jnp.logical_and(local >= top, local < top + chunk)
            outs, parts = fn(w0 + local, own, last, [ref[pl.ds(w0, wlen), :] for ref in rin], pars)
            for ref, val in zip(rout, outs):
                ref[pl.ds(r0, chunk), :] = val[top:top + chunk].astype(ref.dtype)
            return parts

        def add(acc, parts):
            return [[a + b for a, b in zip(ra, rb)] for ra, rb in zip(acc, parts)]

        if whole:
            acc = run(0, 0, 0, True)
        else:
            acc = run(0, 0, HALO, False)
            acc = lax.fori_loop(1, n_chunks - 1,
                                lambda i, a: add(a, run(pl.multiple_of(i * chunk, chunk), HALO, HALO, False)), acc)
            acc = add(acc, run(nrows - chunk, HALO, 0, True))
        for ref, prow in zip(pout, acc):
            for r, v in enumerate(prow):
                ref[pl.ds(r, 1), :] = v

    in_specs = [pl.BlockSpec((nrows, LANE), lambda j, cj=cj: (0, cj(j))) for _, cj in row_ins]
    in_specs += [pl.BlockSpec((a.shape[0], LANE), lambda j, cj=cj: (0, cj(j))) for a, cj in par_ins]
    out_specs = [pl.BlockSpec((nrows, LANE), lambda j, cj=cj: (0, cj(j))) for _, cj, _ in row_outs]
    out_specs += [pl.BlockSpec((k, LANE), lambda j, cj=cj: (0, cj(j))) for k, _, cj in par_outs]
    out_shape = [jax.ShapeDtypeStruct((nrows, width), dt) for width, _, dt in row_outs]
    out_shape += [jax.ShapeDtypeStruct((k, width), F32) for k, width, _ in par_outs]
    return pl.pallas_call(
        body, name=name, grid=(ncol,), in_specs=in_specs, out_specs=out_specs, out_shape=out_shape,
        compiler_params=_cparams(("parallel",), 48),
    )(*[a for a, _ in row_ins], *[a for a, _ in par_ins])


def _tri():
    r = lax.broadcasted_iota(jnp.int32, (CH, CH), 0)
    c = lax.broadcasted_iota(jnp.int32, (CH, CH), 1)
    return r, c


def _col(row):
    r, c = _tri()
    return jnp.sum(jnp.where(r == c, row, 0.0), axis=1, keepdims=True)


def _cumsum_rc(g_r):
    r, c = _tri()
    g_c = _col(g_r)
    cs_r = jnp.sum(jnp.where(r <= c, g_c, 0.0), axis=0, keepdims=True)
    cs_c = jnp.sum(jnp.where(c <= r, g_r, 0.0), axis=1, keepdims=True)
    return cs_r, cs_c


def _decay(cs_r, cs_c):
    r, c = _tri()
    return jnp.exp(jnp.where(c <= r, cs_c - cs_r, -jnp.inf))


def _gdn_a(ks, betas, gs):
    r, c = _tri()
    cs = [_cumsum_rc(g) for g in gs]
    kk = [_dg(k, k, 1, 1) for k in ks]
    return [jnp.where(c < r, _col(b) * kki * _decay(*csi), 0.0) for b, kki, csi in zip(betas, kk, cs)]


def _neumann(a_list):
    r, c = _tri()
    xs = [jnp.where(r == c, 1.0, 0.0) - a for a in a_list]
    ps = list(a_list)
    n = 2
    while n < CH:
        ps = [_dgh(p, p, 1, 0) for p in ps]
        xs = [x + _dgh(x, p, 1, 0) for x, p in zip(xs, ps)]
        n *= 2
    return xs


def _gdn_rest(ss, qs, ks, vs, betas, gs, ts):
    n = range(len(ss))
    cs = [_cumsum_rc(g) for g in gs]
    dm = [_decay(*csi) for csi in cs]
    ecs = [jnp.exp(csi[1]) for csi in cs]
    bc = [_col(b) for b in betas]
    u = [_dgh(ts[i], vs[i] * bc[i], 1, 0) for i in n]
    w = [_dgh(ts[i], ks[i] * (bc[i] * ecs[i]), 1, 0) for i in n]
    ws = [_dg(w[i], ss[i], 1, 0) for i in n]
    v_new = [u[i] - ws[i] for i in n]
    qk = [_dg(qs[i], ks[i], 1, 1) * dm[i] for i in n]
    o_in = [_dg(qs[i] * ecs[i], ss[i], 1, 0) for i in n]
    o = [o_in[i] + _dg(qk[i], v_new[i], 1, 0) for i in n]
    g_last = [jnp.sum(g, axis=1, keepdims=True) for g in gs]
    s_new = [ss[i] * jnp.exp(g_last[i]) + _dg(ks[i] * jnp.exp(g_last[i] - cs[i][1]), v_new[i], 0, 0) for i in n]
    return s_new, o


def _gdn_fwd(q, k, v, beta, g, hb):
    t_rows, d = q.shape
    nc, ng, w = t_rows // CH, d // (HD * hb), HD * hb
    sls = [slice(h * HD, (h + 1) * HD) for h in range(hb)]

    def body(q_ref, k_ref, v_ref, b_ref, g_ref, o_ref, ss_ref, ts_ref, s_scr):
        c = pl.program_id(1)

        @pl.when(c == 0)
        def _():
            s_scr[...] = jnp.zeros_like(s_scr)

        qs, ks, vs = ([ref[:, sl] for sl in sls] for ref in (q_ref, k_ref, v_ref))
        br = [b_ref[0, 0, pl.ds(h, 1), :] for h in range(hb)]
        gr = [g_ref[0, 0, pl.ds(h, 1), :] for h in range(hb)]
        s0 = [s_scr[h] for h in range(hb)]
        tm = _neumann(_gdn_a(ks, br, gr))
        s1, o = _gdn_rest(s0, qs, ks, vs, br, gr, tm)
        for h in range(hb):
            ss_ref[0, 0, h] = s0[h]
            ts_ref[0, 0, h] = tm[h]
            o_ref[:, sls[h]] = o[h]
            s_scr[h] = s1[h]

    blk = pl.BlockSpec((CH, w), lambda n, c: (c, n))
    row = pl.BlockSpec((1, 1, hb, CH), lambda n, c: (n, c, 0, 0))
    return pl.pallas_call(
        body, name="gdn_fwd", grid=(ng, nc), in_specs=[blk, blk, blk, row, row],
        out_specs=[blk, pl.BlockSpec((1, 1, hb, HD, HD), lambda n, c: (n, c, 0, 0, 0)),
                   pl.BlockSpec((1, 1, hb, CH, CH), lambda n, c: (n, c, 0, 0, 0))],
        out_shape=[jax.ShapeDtypeStruct((t_rows, d), F32), jax.ShapeDtypeStruct((ng, nc, hb, HD, HD), F32),
                   jax.ShapeDtypeStruct((ng, nc, hb, CH, CH), F32)],
        scratch_shapes=[pltpu.VMEM((hb, HD, HD), F32)],
        compiler_params=_cparams(("parallel", "arbitrary"), 32),
    )(q, k, v, beta, g)


def _gdn_bwd(q, k, v, beta, g, ss, ts, do, hb):
    t_rows, d = q.shape
    nc, ng, w = t_rows // CH, d // (HD * hb), HD * hb
    per_f = ss.shape[2] // hb
    sls = [slice(h * HD, (h + 1) * HD) for h in range(hb)]

    def body(q_ref, k_ref, v_ref, b_ref, g_ref, ss_ref, ts_ref, do_ref, dq_ref, dk_ref, dv_ref, db_ref, dg_ref, ds_scr):
        cr = pl.program_id(1)

        @pl.when(cr == 0)
        def _():
            ds_scr[...] = jnp.zeros_like(ds_scr)

        first = cr == nc - 1
        rowi = lax.broadcasted_iota(jnp.int32, (CH, 1), 0)
        lani = lax.broadcasted_iota(jnp.int32, (1, CH), 1)
        keep_c = jnp.logical_or(jnp.logical_not(first), rowi >= PADR)
        keep_r = jnp.logical_or(jnp.logical_not(first), lani >= PADR)
        hs = range(hb)
        qs, ks, vs, dos = ([ref[:, sl] for sl in sls] for ref in (q_ref, k_ref, v_ref, do_ref))
        br = [b_ref[0, 0, pl.ds(h, 1), :] for h in hs]
        gr = [g_ref[0, 0, pl.ds(h, 1), :] for h in hs]
        tm = [ts_ref[0, 0, h] for h in hs]
        _, vjp_rest = jax.vjp(_gdn_rest, [ss_ref[0, 0, h] for h in hs], qs, ks, vs, br, gr, tm)
        ds0, dq, dk, dv, db, dg, dt = vjp_rest(([ds_scr[h] for h in hs], dos))
        dtt = [_dgh(dt[h], tm[h], 1, 1) for h in hs]
        da = [-_dgh(tm[h], dtt[h], 0, 0) for h in hs]
        _, vjp_a = jax.vjp(_gdn_a, ks, br, gr)
        dk2, db2, dg2 = vjp_a(da)
        for h in hs:
            ds_scr[h] = ds0[h]
            dq_ref[:, sls[h]] = jnp.where(keep_c, dq[h], 0.0)
            dk_ref[:, sls[h]] = jnp.where(keep_c, dk[h] + dk2[h], 0.0)
            dv_ref[:, sls[h]] = jnp.where(keep_c, dv[h], 0.0)
            db_ref[0, 0, pl.ds(h, 1), :] = jnp.where(keep_r, db[h] + db2[h], 0.0)
            dg_ref[0, 0, pl.ds(h, 1), :] = jnp.where(keep_r, dg[h] + dg2[h], 0.0)

    blk = pl.BlockSpec((CH, w), lambda n, c: (nc - 1 - c, n))
    row = pl.BlockSpec((1, 1, hb, CH), lambda n, c: (n, nc - 1 - c, 0, 0))
    return pl.pallas_call(
        body, name="gdn_bwd", grid=(ng, nc),
        in_specs=[blk, blk, blk, row, row,
                  pl.BlockSpec((1, 1, hb, HD, HD), lambda n, c: (n // per_f, nc - 1 - c, n % per_f, 0, 0)),
                  pl.BlockSpec((1, 1, hb, CH, CH), lambda n, c: (n // per_f, nc - 1 - c, n % per_f, 0, 0)), blk],
        out_specs=[blk, blk, blk, row, row],
        out_shape=[jax.ShapeDtypeStruct((t_rows, d), F32)] * 3 + [jax.ShapeDtypeStruct((ng, nc, hb, CH), F32)] * 2,
        scratch_shapes=[pltpu.VMEM((hb, HD, HD), F32)],
        compiler_params=_cparams(("parallel", "arbitrary"), 32),
    )(q, k, v, beta, g, ss, ts, do)


def _ssd_group(s, xs, bm, cm, dt_r, a_r):
    prs = range(len(s))
    ngrp = bm.shape[1] // NST
    grp = [p // (len(s) // ngrp) for p in prs]
    bms = [bm[:, g * NST:(g + 1) * NST] for g in range(ngrp)]
    cms = [cm[:, g * NST:(g + 1) * NST] for g in range(ngrp)]
    first = lax.broadcasted_iota(jnp.int32, (1, 2 * M2P), 1) < M2P

    def pick(vals, p):
        return jnp.where(first, vals[2 * p], vals[2 * p + 1])

    cs = [_cumsum_rc(a) for a in a_r]
    lm = [_decay(*csi) for csi in cs]
    ecs = [jnp.exp(csi[1]) for csi in cs]
    alast = [jnp.sum(a, axis=1, keepdims=True) for a in a_r]
    ealast = [jnp.exp(al) for al in alast]
    wt = [jnp.exp(al - csi[1]) for al, csi in zip(alast, cs)]
    dtc = [_col(t) for t in dt_r]
    xdt = [xs[:, p * LANE:(p + 1) * LANE] * pick(dtc, p) for p in prs]
    cb = [_dg(cms[g], bms[g], 1, 1) for g in range(ngrp)]
    y0 = [_dg(cb[grp[p]] * lm[2 * p], xdt[p], 1, 0) for p in prs]
    y1 = [_dg(cb[grp[p]] * lm[2 * p + 1], xdt[p], 1, 0) for p in prs]
    yo = [_dg(cms[grp[p]], s[p], 1, 0) for p in prs]
    y = [jnp.where(first, y0[p], y1[p]) + yo[p] * pick(ecs, p) for p in prs]
    s_new = [s[p] * pick(ealast, p) + _dg(bms[grp[p]], xdt[p] * pick(wt, p), 0, 0) for p in prs]
    return s_new, jnp.concatenate(y, axis=1)


def _ssd_specs(nc, d, rev, gps):
    assert M2G % gps == 0 and (d // LANE) % gps == 0
    hps = (d // M2P) // M2G * gps
    cc = (lambda c: nc - 1 - c) if rev else (lambda c: c)
    xs = pl.BlockSpec((CH, hps * M2P), lambda g, c: (cc(c), g))
    bm = pl.BlockSpec((CH, NST * gps), lambda g, c: (cc(c), (d // LANE) // gps + g))
    cm = pl.BlockSpec((CH, NST * gps), lambda g, c: (cc(c), (d // LANE + M2G) // gps + g))
    row = pl.BlockSpec((1, 1, hps, CH), lambda g, c: (g, cc(c), 0, 0))
    return xs, bm, cm, row, hps


def _ssd_fwd(xbc, dt, a, d, gps):
    t_rows = xbc.shape[0]
    nc = t_rows // CH
    xs, bm, cm, row, hpg = _ssd_specs(nc, d, False, gps)
    ppg = hpg // 2
    st = pl.BlockSpec((1, 1, ppg, NST, LANE), lambda g, c: (g, c, 0, 0, 0))

    def body(xs_ref, b_ref, c_ref, dt_ref, a_ref, y_ref, ss_ref, s_scr):
        c = pl.program_id(1)

        @pl.when(c == 0)
        def _():
            s_scr[...] = jnp.zeros_like(s_scr)

        s0 = [s_scr[p] for p in range(ppg)]
        for p in range(ppg):
            ss_ref[0, 0, p] = s0[p]
        dt_r = [dt_ref[0, 0, pl.ds(h, 1), :] for h in range(hpg)]
        a_r = [a_ref[0, 0, pl.ds(h, 1), :] for h in range(hpg)]
        s1, y = _ssd_group(s0, xs_ref[...], b_ref[...], c_ref[...], dt_r, a_r)
        y_ref[...] = y
        for p in range(ppg):
            s_scr[p] = s1[p]

    return pl.pallas_call(
        body, name="ssd_fwd", grid=(M2G // gps, nc), in_specs=[xs, bm, cm, row, row], out_specs=[xs, st],
        out_shape=[jax.ShapeDtypeStruct((t_rows, d), F32), jax.ShapeDtypeStruct((M2G // gps, nc, ppg, NST, LANE), F32)],
        scratch_shapes=[pltpu.VMEM((ppg, NST, LANE), F32)],
        compiler_params=_cparams(("parallel", "arbitrary"), 32),
    )(xbc, xbc, xbc, dt, a)


def _ssd_bwd(xbc, dt, a, ss, dy, d, gps):
    t_rows = xbc.shape[0]
    nc = t_rows // CH
    xs, bm, cm, row, hpg = _ssd_specs(nc, d, True, gps)
    ppg = hpg // 2
    per_f = ss.shape[2] // ppg
    st = pl.BlockSpec((1, 1, ppg, NST, LANE), lambda g, c: (g // per_f, nc - 1 - c, g % per_f, 0, 0))

    def body(xs_ref, b_ref, c_ref, dt_ref, a_ref, ss_ref, dy_ref, dxs_ref, db_ref, dc_ref, ddt_ref, da_ref, ds_scr):
        cr = pl.program_id(1)

        @pl.when(cr == 0)
        def _():
            ds_scr[...] = jnp.zeros_like(ds_scr)

        first = cr == nc - 1
        keep_c = jnp.logical_or(jnp.logical_not(first), lax.broadcasted_iota(jnp.int32, (CH, 1), 0) >= PADR)
        keep_r = jnp.logical_or(jnp.logical_not(first), lax.broadcasted_iota(jnp.int32, (1, CH), 1) >= PADR)
        dt_r = [dt_ref[0, 0, pl.ds(h, 1), :] for h in range(hpg)]
        a_r = [a_ref[0, 0, pl.ds(h, 1), :] for h in range(hpg)]
        s0 = [ss_ref[0, 0, p] for p in range(ppg)]
        _, vjp = jax.vjp(_ssd_group, s0, xs_ref[...], b_ref[...], c_ref[...], dt_r, a_r)
        ds0, dxs, db, dc, ddt, da = vjp(([ds_scr[p] for p in range(ppg)], dy_ref[...]))
        for p in range(ppg):
            ds_scr[p] = ds0[p]
        dxs_ref[...] = jnp.where(keep_c, dxs, 0.0)
        db_ref[...] = jnp.where(keep_c, db, 0.0)
        dc_ref[...] = jnp.where(keep_c, dc, 0.0)
        for h in range(hpg):
            ddt_ref[0, 0, pl.ds(h, 1), :] = jnp.where(keep_r, ddt[h], 0.0)
            da_ref[0, 0, pl.ds(h, 1), :] = jnp.where(keep_r, da[h], 0.0)

    grp = pl.BlockSpec((CH, NST * gps), lambda g, c: (nc - 1 - c, g))
    return pl.pallas_call(
        body, name="ssd_bwd", grid=(M2G // gps, nc), in_specs=[xs, bm, cm, row, row, st, xs],
        out_specs=[xs, grp, grp, row, row],
        out_shape=[jax.ShapeDtypeStruct((t_rows, d), F32)] + [jax.ShapeDtypeStruct((t_rows, M2G * NST), F32)] * 2
        + [jax.ShapeDtypeStruct((M2G // gps, nc, hpg, CH), F32)] * 2,
        scratch_shapes=[pltpu.VMEM((ppg, NST, LANE), F32)],
        compiler_params=_cparams(("parallel", "arbitrary"), 32),
    )(xbc, xbc, xbc, dt, a, ss, dy)


def _exchange(name, gathers, scatters, after):
    arrays = list(gathers) + list(scatters)
    n_g, n = len(gathers), len(arrays)

    def body(*refs):
        ins, outs = refs[:n], refs[n + 1:2 * n + 1]
        send_sems, recv_sems, local_sems = refs[2 * n + 1:]
        x, y, c = lax.axis_index("x"), lax.axis_index("y"), lax.axis_index("c")
        me = 4 * x + 2 * y + c

        def src(a, slot):
            return ins[a] if a < n_g else ins[a].at[slot]

        local = [pltpu.make_async_copy(src(a, me), outs[a].at[me], local_sems.at[a]) for a in range(n)]
        for cp in local:
            cp.start()
        copies = []
        for rel in range(1, NDEV):
            px, py, pc = x ^ (rel >> 2), y ^ ((rel >> 1) & 1), c ^ (rel & 1)
            peer = 4 * px + 2 * py + pc
            for a in range(n):
                copies.append(pltpu.make_async_remote_copy(
                    src_ref=src(a, peer), dst_ref=outs[a].at[me], send_sem=send_sems.at[a, rel - 1],
                    recv_sem=recv_sems.at[a, rel - 1], device_id=(px, py, pc), device_id_type=pl.DeviceIdType.MESH))
        for cp in copies:
            cp.start()
        for cp in copies:
            cp.wait_recv()
        for cp in copies:
            cp.wait_send()
        for cp in local:
            cp.wait()

    any_spec = pl.BlockSpec(memory_space=pl.ANY)
    out_shape = [jax.ShapeDtypeStruct((NDEV,) + a.shape, a.dtype) for a in gathers]
    out_shape += [jax.ShapeDtypeStruct(a.shape, a.dtype) for a in scatters]
    return pl.pallas_call(
        body, name=name, in_specs=[any_spec] * (n + 1), out_specs=[any_spec] * n, out_shape=out_shape,
        scratch_shapes=[pltpu.SemaphoreType.DMA((n, NDEV - 1)), pltpu.SemaphoreType.DMA((n, NDEV - 1)),
                        pltpu.SemaphoreType.DMA((n,))],
        compiler_params=pltpu.CompilerParams(has_side_effects=True),
    )(*arrays, after)


def _gather_two_level(name, arrays):
    n = len(arrays)

    def body(*refs):
        ins, outs = refs[:n], refs[n:2 * n]
        send_sems, recv_sems, local_sems = refs[2 * n:]
        x, y, c = lax.axis_index("x"), lax.axis_index("y"), lax.axis_index("c")
        me, sibling = (x, y, c), (x, y, 1 - c)
        chips = [(1 - x, y), (x, 1 - y), (1 - x, 1 - y)]

        def copy(a, k, block, to, src=None):
            dst = outs[a].at[4 * block[0] + 2 * block[1] + block[2]]
            return pltpu.make_async_remote_copy(
                src_ref=dst if src is None else src, dst_ref=dst, send_sem=send_sems.at[a, k], recv_sem=recv_sems.at[a, k],
                device_id=to, device_id_type=pl.DeviceIdType.MESH)

        mine = [pltpu.make_async_copy(ins[a], outs[a].at[4 * x + 2 * y + c], local_sems.at[a]) for a in range(n)]
        for cp in mine:
            cp.start()
        first = []
        for a in range(n):
            first.append(copy(a, 0, me, sibling, src=ins[a]))
            first += [copy(a, 1 + j, me, (*chip, c), src=ins[a]) for j, chip in enumerate(chips)]
        for cp in first:
            cp.start()
        passed = [[copy(a, 4 + j, (*chip, c), sibling) for j, chip in enumerate(chips)] for a in range(n)]
        for j, chip in enumerate(chips):
            for a in range(n):
                copy(a, 1 + j, (*chip, c), me).wait_recv()
                passed[a][j].start()
        for a in range(n):
            copy(a, 0, sibling, me).wait_recv()
            for j, chip in enumerate(chips):
                copy(a, 4 + j, (*chip, 1 - c), me).wait_recv()
        for cp in first + [cp for row in passed for cp in row]:
            cp.wait_send()
        for cp in mine:
            cp.wait()

    any_spec = pl.BlockSpec(memory_space=pl.ANY)
    return pl.pallas_call(
        body, name=name, in_specs=[any_spec] * n, out_specs=[any_spec] * n,
        out_shape=[jax.ShapeDtypeStruct((NDEV,) + a.shape, a.dtype) for a in arrays],
        scratch_shapes=[pltpu.SemaphoreType.DMA((n, NDEV - 1)), pltpu.SemaphoreType.DMA((n, NDEV - 1)),
                        pltpu.SemaphoreType.DMA((n,))],
        compiler_params=pltpu.CompilerParams(has_side_effects=True),
    )(*arrays)


_HBM = pl.BlockSpec(memory_space=pltpu.HBM)
_SEM = pl.BlockSpec(memory_space=pltpu.SEMAPHORE)
_EFFECT = pltpu.SideEffectType.DATAFLOW_SIDE_EFFECTING


def _split_copies(srcs, lands, send_sems, recv_sems, n_g):
    x, y, c = lax.axis_index("x"), lax.axis_index("y"), lax.axis_index("c")
    me = 4 * x + 2 * y + c
    copies = []
    for rel in range(1, NDEV):
        px, py, pc = x ^ (rel >> 2), y ^ ((rel >> 1) & 1), c ^ (rel & 1)
        peer = 4 * px + 2 * py + pc
        for a in range(len(srcs)):
            copies.append(pltpu.make_async_remote_copy(
                src_ref=srcs[a] if a < n_g else srcs[a].at[peer], dst_ref=lands[a].at[me],
                send_sem=send_sems.at[a * (NDEV - 1) + rel - 1], recv_sem=recv_sems.at[a * (NDEV - 1) + rel - 1],
                device_id=(px, py, pc), device_id_type=pl.DeviceIdType.MESH))
    return copies


def _exchange_start(name, gathers, scatters, after):
    arrays = list(gathers) + list(scatters)
    n_g, n = len(gathers), len(arrays)
    lands = [lax.empty((NDEV,) + a.shape, a.dtype) for a in gathers] + [lax.empty(a.shape, a.dtype) for a in scatters]

    def body(*refs):
        send_sems, recv_sems = refs[2 * n + 1], refs[2 * n + 2]
        for cp in _split_copies(refs[:n], refs[n:2 * n], send_sems, recv_sems, n_g):
            cp.start()
        refs[-1][...] = jnp.zeros_like(refs[-1])

    sems = pltpu.SemaphoreType.DMA((n * (NDEV - 1),))
    out = pl.pallas_call(
        body, name=name, in_specs=[_HBM] * (2 * n) + [pl.BlockSpec(memory_space=pl.ANY)],
        out_specs=(_SEM, _SEM, *[_HBM] * (2 * n), pl.BlockSpec(memory_space=pltpu.VMEM)),
        out_shape=(sems, sems, *[pltpu.HBM(a.shape, a.dtype) for a in arrays + lands], jax.ShapeDtypeStruct((8, LANE), F32)),
        input_output_aliases={i: 2 + i for i in range(2 * n)},
        compiler_params=pltpu.CompilerParams(has_side_effects=_EFFECT),
    )(*[pltpu.with_memory_space_constraint(a, pltpu.HBM) for a in arrays + lands], after)
    return out[0], out[1], list(out[2:2 + 2 * n]), out[-1], n_g


def _exchange_wait(name, started, after):
    send_sems, recv_sems, thru, _, n_g = started
    n = len(thru) // 2

    def body(*refs):
        for cp in _split_copies(refs[:n], refs[n:2 * n], refs[2 * n], refs[2 * n + 1], n_g):
            cp.wait_send()
            cp.wait_recv()

    out = pl.pallas_call(
        body, name=name, in_specs=[_HBM] * (2 * n) + [_SEM, _SEM, pl.BlockSpec(memory_space=pl.ANY)],
        out_specs=[_HBM] * (2 * n), out_shape=[pltpu.HBM(a.shape, a.dtype) for a in thru],
        input_output_aliases={i: i for i in range(2 * n)},
        compiler_params=pltpu.CompilerParams(has_side_effects=_EFFECT),
    )(*thru, send_sems, recv_sems, after)
    me = 4 * lax.axis_index("x") + 2 * lax.axis_index("y") + lax.axis_index("c")
    full = []
    for a in range(n):
        own = out[a][None] if a < n_g else lax.dynamic_index_in_dim(out[a], me, 0, keepdims=True)
        full.append(lax.dynamic_update_index_in_dim(out[n + a], own, me, 0))
    return full


def _adamw(name, staged, w, m, v):
    r, c = w.shape
    tr = _pick(r, 256, 8)

    def body(st_ref, w_ref, m_ref, v_ref, g_ref, d_ref, nm_ref, nv_ref):
        g = st_ref[0].astype(F32)
        for k in range(1, NDEV):
            g = g + st_ref[k].astype(F32)
        m_new = ADAM_B1 * m_ref[...] + (1.0 - ADAM_B1) * g
        v_new = ADAM_B2 * v_ref[...] + (1.0 - ADAM_B2) * jnp.square(g)
        m_hat = m_new / (1.0 - ADAM_B1 ** ADAM_STEP)
        v_hat = v_new / (1.0 - ADAM_B2 ** ADAM_STEP)
        g_ref[...] = g
        d_ref[...] = -ADAM_LR * (m_hat / (jnp.sqrt(v_hat) + ADAM_EPS) + ADAM_WD * w_ref[...])
        nm_ref[...] = m_new
        nv_ref[...] = v_new

    blk = pl.BlockSpec((tr, c), lambda i: (i, 0))
    return pl.pallas_call(
        body, name=name, grid=(r // tr,), in_specs=[pl.BlockSpec((NDEV, tr, c), lambda i: (0, i, 0)), blk, blk, blk],
        out_specs=[blk] * 4, out_shape=[jax.ShapeDtypeStruct((r, c), F32)] * 4,
        compiler_params=_cparams(("parallel",), 48),
    )(staged, w, m, v)


def _pack(parts):
    flat = jnp.concatenate([p.reshape(-1).astype(F32) for p in parts])
    pad = (-flat.shape[0]) % (8 * LANE)
    return jnp.pad(flat, (0, pad)).reshape(-1, LANE)


def _unpack(slab, shapes):
    flat, out, off = slab.reshape(-1), [], 0
    for s in shapes:
        n = 1
        for dim in s:
            n *= dim
        out.append(flat[off:off + n].reshape(s))
        off += n
    return out


def _to_shards(full, axis):
    shp = full.shape
    t = full.reshape(shp[:axis] + (NDEV, shp[axis] // NDEV) + shp[axis + 1:])
    return jnp.moveaxis(t, axis, 0)


def _from_shards(g, axis):
    t = jnp.moveaxis(g, 0, axis)
    shp = t.shape
    return t.reshape(shp[:axis] + (shp[axis] * shp[axis + 1],) + shp[axis + 2:])


def kernel(x, meta_tokens, norm_mix_w, w_in, dn_conv_w, dn_a_log, dn_dt_bias, dn_norm_w, m2_conv_w, m2_conv_b, m2_a_log, m2_dt_bias, m2_d, m2_norm_w, w_out, norm_ffn_w, ffn_up, ffn_conv_w, ffn_down, norm_final_w, loss_target, m_meta_tokens, m_norm_mix_w, m_w_in, m_dn_conv_w, m_dn_a_log, m_dn_dt_bias, m_dn_norm_w, m_m2_conv_w, m_m2_conv_b, m_m2_a_log, m_m2_dt_bias, m_m2_d, m_m2_norm_w, m_w_out, m_norm_ffn_w, m_ffn_up, m_ffn_conv_w, m_ffn_down, m_norm_final_w, v_meta_tokens, v_norm_mix_w, v_w_in, v_dn_conv_w, v_dn_a_log, v_dn_dt_bias, v_dn_norm_w, v_m2_conv_w, v_m2_conv_b, v_m2_a_log, v_m2_dt_bias, v_m2_d, v_m2_norm_w, v_w_out, v_norm_ffn_w, v_ffn_up, v_ffn_conv_w, v_ffn_down, v_norm_final_w):
    seq, d = x.shape[1], x.shape[2]
    t_rows = seq + CH
    nc = t_rows // CH
    dnh, m2h = d // HD, d // M2P
    dff = ffn_down.shape[1] * NDEV
    xbc_w = d + 2 * M2G * NST
    assert seq % CH == 0 and d % (2 * M2P * M2G) == 0 and 2 * dnh + m2h <= LANE
    hb_f = max(h for h in (16, 8, 4, 2, 1) if dnh % h == 0)
    hb_b = max(h for h in (8, 4, 2, 1) if dnh % h == 0)
    tm_rw = _pick(t_rows, 208, 16)
    conv_chunk = _pick(t_rows, min(CONV_CHUNK, t_rows // 3), 16)

    small_sharded = [meta_tokens, dn_conv_w[0], m2_conv_w[0], ffn_conv_w[0]]
    small_shapes = [p.shape for p in small_sharded]
    g_win, g_small = _gather_two_level("gather_w_in", [w_in[0].astype(WIRE).T, _pack(small_sharded)])
    rest = _exchange_start("gather_rest_start", [w_out[0].astype(WIRE), ffn_up[0].astype(WIRE), ffn_down[0].astype(WIRE)], [],
                           g_small)
    win_t = g_win.reshape(-1, d)
    small_full = [_unpack(g_small[k], small_shapes) for k in range(NDEV)]
    meta_f, dnconv_f, m2conv_f, ffnconv_f = [jnp.concatenate([small_full[k][i] for k in range(NDEV)], axis=-1) for i in range(4)]

    o_z, o_b, o_a = 3 * d, 4 * d, 4 * d + dnh
    o_m2z = 4 * d + 2 * dnh
    o_xbc, o_dt = o_m2z + d, o_m2z + d + xbc_w
    w_all_t = jnp.concatenate([win_t[:o_b], win_t[o_m2z:o_dt], win_t[o_b:o_m2z], win_t[o_dt:],
                               jnp.zeros((LANE - 2 * dnh - m2h, d), WIRE)], axis=0)
    seg_cols = {"q": (0, d), "k": (d, d), "v": (2 * d, d), "z": (3 * d, d), "m2z": (4 * d, d), "xbc": (5 * d, xbc_w),
                "sm": (5 * d + xbc_w, LANE)}

    head_rows_ = jnp.concatenate([jnp.zeros((PADR, d), F32), meta_f], axis=0)
    valid = lambda rows: rows >= PADR

    def norm_fwd(name, h, w):
        return _rw(name, lambda rows, j, hv, wv: (_rms(hv, wv),), [("r", h, d, _c0), ("p", w, d, _c0)],
                   [("r", d, d, _c0, MXU)], t_rows, tm_rw)[0]

    def embed_norm(rows, j, xv, hv, wv):
        h = jnp.where(rows < CH, hv, xv)
        return h, _rms(h, wv)

    h0, hn1 = _rw("norm_mix", embed_norm,
                  [("r", x[0], d, _c0, lambda i: jnp.maximum(i - 1, 0)), ("r", head_rows_, d, _c0, lambda i: 0),
                   ("p", norm_mix_w, d, _c0)], [("r", d, d, _c0, F32), ("r", d, d, _c0, MXU)], t_rows, CH)
    proj = {s: _mm("proj_" + s, hn1, w_all_t, tb=True, b_rows=seg_cols[s], dep=rest[3]) for s in seg_cols}

    def dn_post(sec, cv):
        s = _silu(cv)
        if sec < 2:
            s = s * lax.rsqrt(jnp.sum(s * s, axis=-1, keepdims=True) + EPS)
        if sec == 0:
            s = s * (HD ** -0.5)
        return s

    def dn_prep(sec, name):
        def fn(rows, own, last, wins, pars):
            return [jnp.where(valid(rows), dn_post(sec, _conv(wins[0], pars[0])), 0.0)], []
        wc = dnconv_f[:, sec * d:(sec + 1) * d]
        return _cv("dn_prep_" + name, fn, [(proj[name], _cj)], [(wc, _cj)], [(d, _cj, F32)], [], t_rows, dnh)[0]

    q_act, k_act, v_act = dn_prep(0, "q"), dn_prep(1, "k"), dn_prep(2, "v")

    lane = lambda: lax.broadcasted_iota(jnp.int32, (1, LANE), 1)

    def lanes_of(vec, off):
        return jnp.pad(vec.astype(F32), ((0, 0), (off, LANE - off - vec.shape[1])))

    gate_params = [lanes_of(dn_a_log, dnh), lanes_of(dn_dt_bias, dnh), lanes_of(m2_a_log, 2 * dnh), lanes_of(m2_dt_bias, 2 * dnh)]

    def gates(rows, sm, p_alog, p_dtb, p_malog, p_mdtb):
        ln = lane()
        is_b, is_g = ln < dnh, jnp.logical_and(ln >= dnh, ln < 2 * dnh)
        is_d = jnp.logical_and(ln >= 2 * dnh, ln < 2 * dnh + m2h)
        beta = jax.nn.sigmoid(sm)
        gdec = -jnp.exp(p_alog) * _softplus(sm + p_dtb)
        dt = _softplus(sm + p_mdtb)
        am = dt * (-jnp.exp(p_malog))
        ok = valid(rows)
        g1 = jnp.where(ok, jnp.where(is_b, beta, jnp.where(is_g, gdec, jnp.where(is_d, dt, 0.0))), 0.0)
        g2 = jnp.where(jnp.logical_and(ok, is_d), am, 0.0)
        return g1, g2

    gate_ins = [("r", proj["sm"], LANE, _c0)] + [("p", p, LANE, _c0) for p in gate_params]
    g1, g2 = _rw("gates", lambda rows, j, *a: gates(rows, *a), gate_ins,
                 [("r", LANE, LANE, _c0, F32), ("r", LANE, LANE, _c0, F32)], t_rows, tm_rw)

    def head_rows(cols, per):
        n = cols.shape[1]
        return cols.reshape(nc, CH, n // per, per).transpose(2, 0, 3, 1)

    def head_cols(rows_):
        ngrp, _, per, _ = rows_.shape
        return rows_.transpose(1, 3, 0, 2).reshape(t_rows, ngrp * per)

    beta_r, gdec_r = head_rows(g1[:, :dnh], hb_f), head_rows(g1[:, dnh:2 * dnh], hb_f)
    hpg = m2h // M2G
    m2_rows = lambda gps: (head_rows(g1[:, 2 * dnh:2 * dnh + m2h], hpg * gps), head_rows(g2[:, 2 * dnh:2 * dnh + m2h], hpg * gps))

    o_dn, dn_states, dn_tinv = _gdn_fwd(q_act, k_act, v_act, beta_r, gdec_r, hb_f)

    def dn_out(o, z, w):
        outs = []
        for h in range(dnh):
            sl = slice(h * HD, (h + 1) * HD)
            outs.append(_rms(o[:, sl], w) * _silu(z[:, sl]))
        return jnp.concatenate(outs, axis=1)

    mixed_dn = _rw("dn_out", lambda rows, j, o, z, w: (dn_out(o, z, w),),
                   [("r", o_dn, d, _c0), ("r", proj["z"], d, _c0), ("p", dn_norm_w, HD, _c0)], [("r", d, d, _c0, MXU)],
                   t_rows, tm_rw)[0]

    def m2_prep(rows, own, last, wins, pars):
        return [jnp.where(valid(rows), _silu(_conv(wins[0], pars[0]) + pars[1][0]), 0.0)], []

    xbc_act = _cv("m2_prep", m2_prep, [(proj["xbc"], _cj)], [(m2conv_f, _cj), (m2_conv_b, _cj)], [(xbc_w, _cj, F32)], [],
                  t_rows, xbc_w // LANE)[0]
    y_ssd, m2_states = _ssd_fwd(xbc_act, *m2_rows(SSD_GPS_F), d, SSD_GPS_F)

    d_lanes = jnp.repeat(m2_d.astype(F32), M2P, axis=1)
    gw = d // M2G

    def m2_out(ys, xs, z, dl, nw):
        yv = (ys + dl * xs) * _silu(z)
        outs = []
        for gi in range(M2G):
            sl = slice(gi * gw, (gi + 1) * gw)
            outs.append(_rms(yv[:, sl], nw[:, sl]))
        return jnp.concatenate(outs, axis=1)

    m2_out_ins = [("r", y_ssd, d, _c0), ("r", xbc_act, d, _c0), ("r", proj["m2z"], d, _c0), ("p", d_lanes, d, _c0),
                  ("p", m2_norm_w, d, _c0)]
    mixed_m2 = _rw("m2_out", lambda rows, j, *a: (m2_out(*a),), m2_out_ins, [("r", d, d, _c0, MXU)], t_rows, tm_rw)[0]

    mixed = jnp.concatenate([mixed_dn, mixed_m2], axis=1)
    g_wout, g_wup, g_wdown = _exchange_wait("gather_rest_wait", rest, mixed)
    wout = _from_shards(g_wout, 0)
    wup = _from_shards(g_wup, 1)
    wdown = _from_shards(g_wdown, 0)
    up_g, up_v = (0, dff), (dff, dff)
    h1 = _mm("out_proj", mixed, wout, add=h0)
    hn2 = norm_fwd("norm_ffn", h1, norm_ffn_w)
    u_g, u_v = _mm("ffn_up_g", hn2, wup, b_cols=up_g), _mm("ffn_up_v", hn2, wup, b_cols=up_v)
    fc_g, fc_v = ffnconv_f[:, :dff], ffnconv_f[:, dff:]

    def ffn_act(rows, own, last, wins, pars):
        return [jnp.where(valid(rows), _silu(_conv(wins[0], pars[0])) * _conv(wins[1], pars[1]), 0.0)], []

    act = _cv("ffn_act", ffn_act, [(u_g, _cj), (u_v, _cj)], [(fc_g, _cj), (fc_v, _cj)], [(dff, _cj, MXU)], [],
              t_rows, dff // LANE)[0]
    h2 = _mm("ffn_down", act, wdown, add=h1, tk=1408)

    def loss_fn(hv, wf, tgt, rows):
        err = jnp.where(rows >= CH, _rms(hv, wf) - tgt, 0.0)
        return 0.5 * jnp.sum(jnp.mean(err * err, axis=-1, keepdims=True), axis=0, keepdims=True)

    def final(rows, j, hv, wf, tgt):
        loss, vjp = jax.vjp(lambda a, b: loss_fn(a, b, tgt, rows), hv, wf)
        dh, dw = vjp(jnp.ones((1, 1), F32))
        return dh, dh, dw, jnp.broadcast_to(loss, (1, LANE))

    wf2 = norm_final_w.reshape(1, d)
    dh2, dh2_m, d_wf, loss_part = _rw(
        "loss_head", final, [("r", h2, d, _c0), ("p", wf2, d, _c0), ("r", loss_target[0], d, _c0, lambda i: jnp.maximum(i - 1, 0))],
        [("r", d, d, _c0, F32), ("r", d, d, _c0, MXU), ("p", 1, d, d, _c0), ("p", 1, LANE, LANE, _c0)], t_rows, CH)
    loss = lax.psum(loss_part[0, 0], MESH_AXES)

    d_act = _mm("d_act", dh2_m, wdown, tb=True)
    gw_down = _mm("gw_down", act, dh2_m, ta=True, tm=1408, tn=1024, tk=2080, out_dtype=WIRE)
    x_down = _exchange_start("grad_down_start", [], [_to_shards(gw_down, 0).astype(WIRE)], gw_down)

    def t_end(last):
        return t_rows if last else None

    def ffn_act_bwd(rows, own, last, wins, pars):
        (ug, uv, da), (wg, wv) = wins, pars
        cg, cv = _conv(ug, wg), _conv(uv, wv)
        _, vjp = jax.vjp(lambda a, b: _silu(a) * b, cg, cv)
        dcg, dcv = vjp(jnp.where(valid(rows), da, 0.0))
        return ([_conv_t(dcg, wg, rows, t_end(last)), _conv_t(dcv, wv, rows, t_end(last))],
                [_conv_w(jnp.where(own, dcg, 0.0), ug, len(wg)), _conv_w(jnp.where(own, dcv, 0.0), uv, len(wv))])

    kf = fc_g.shape[0]
    du_g, du_v, g_fc_g, g_fc_v = _cv(
        "ffn_act_bwd", ffn_act_bwd, [(u_g, _cj), (u_v, _cj), (d_act, _cj)], [(fc_g, _cj), (fc_v, _cj)],
        [(dff, _cj, MXU), (dff, _cj, MXU)], [(kf, dff, _cj), (kf, dff, _cj)], t_rows, dff // LANE, chunk=conv_chunk)
    gw_up_g = _mm("gw_up_g", hn2, du_g, ta=True, tm=1024, tn=1408, tk=2080, out_dtype=WIRE, dep=x_down[3])
    gw_up_v = _mm("gw_up_v", hn2, du_v, ta=True, tm=1024, tn=1408, tk=2080, out_dtype=WIRE)
    gw_up_full = jnp.concatenate([gw_up_g, gw_up_v], axis=1)
    x_up = _exchange_start("grad_up_start", [], [_to_shards(gw_up_full, 1).astype(WIRE)], gw_up_full)
    d_hn2 = _mm("d_hn2_v", du_v, wup, b_cols=up_v, tb=True, tk=1408, dep=x_up[3],
                add=_mm("d_hn2_g", du_g, wup, b_cols=up_g, tb=True, tk=1408, tn=2048, dep=x_up[3]))

    def norm_bwd(name, h, w, dy, dres):
        def fn(rows, j, hv, wv, dyv, dr):
            _, vjp = jax.vjp(_rms, hv, wv)
            dh, dw = vjp(dyv)
            dh = dh + dr
            return dh, dh, dw
        return _rw(name, fn, [("r", h, d, _c0), ("p", w, d, _c0), ("r", dy, d, _c0), ("r", dres, d, _c0)],
                   [("r", d, d, _c0, F32), ("r", d, d, _c0, MXU), ("p", 1, d, d, _c0)], t_rows, tm_rw)

    dh1, dh1_m, g_norm_ffn = norm_bwd("norm_ffn_bwd", h1, norm_ffn_w, d_hn2, dh2)

    gw_out = _mm("gw_out", mixed, dh1_m, ta=True, tm=1024, tn=1024, tk=2080, out_dtype=WIRE)
    x_out = _exchange_start("grad_out_start", [], [_to_shards(gw_out, 0).astype(WIRE)], gw_out)
    d_mixed = _mm("d_mixed", dh1_m, wout, tb=True, dep=x_out[3])

    gw_seg = {}

    def gw_in(seg, dseg_arr):
        gw_seg[seg] = _mm("gw_in_" + seg, hn1, dseg_arr, ta=True, tm=1024, tn=1024, tk=2080, out_dtype=WIRE)

    def m2_out_bwd(rows, j, ys, xs, z, dl, nw, dy):
        _, vjp = jax.vjp(m2_out, ys, xs, z, dl, nw)
        return vjp(dy)

    dy_ssd, dxs_skip, d_m2z, g_d_lanes, g_m2_norm = _rw(
        "m2_out_bwd", m2_out_bwd, m2_out_ins + [("r", d_mixed, d, lambda j: 1)],
        [("r", d, d, _c0, F32), ("r", d, d, _c0, F32), ("r", d, d, _c0, MXU), ("p", 1, d, d, _c0), ("p", 1, d, d, _c0)],
        t_rows, _pick(t_rows, 208, 16))
    gw_in("m2z", d_m2z)

    def fold_heads(vec_ref, out_ref):
        r = lax.broadcasted_iota(jnp.int32, (d, LANE), 0)
        c = lax.broadcasted_iota(jnp.int32, (d, LANE), 1)
        out_ref[...] = _dgh(vec_ref[...], jnp.where(jnp.logical_and(r >= c * M2P, r < (c + 1) * M2P), 1.0, 0.0), 1, 0)

    g_m2_d = pl.pallas_call(fold_heads, name="fold_m2_d", out_shape=jax.ShapeDtypeStruct((1, LANE), F32))(g_d_lanes)

    dxs, db_ssd, dc_ssd, ddt_r, dam_r = _ssd_bwd(xbc_act, *m2_rows(SSD_GPS_B), m2_states, dy_ssd, d, SSD_GPS_B)


    def m2_prep_bwd(rows, own, last, wins, pars):
        (p, *ds), (w, b) = wins, pars
        _, vjp = jax.vjp(_silu, _conv(p, w) + b[0])
        dpre, = vjp(jnp.where(valid(rows), functools.reduce(lambda a_, b_: a_ + b_, ds), 0.0))
        dpre_own = jnp.where(own, dpre, 0.0)
        return [_conv_t(dpre, w, rows, t_end(last))], [_conv_w(dpre_own, p, len(w)), [jnp.sum(dpre_own, axis=0, keepdims=True)]]

    def m2_prep_bwd_call(name, off, width, d_arrs):
        at = lambda j, blk0=off // LANE: blk0 + j
        return _cv(name, m2_prep_bwd, [(proj["xbc"], at)] + [(a, _cj) for a in d_arrs], [(m2conv_f, at), (m2_conv_b, at)],
                   [(width, _cj, MXU)], [(m2conv_f.shape[0], width, _cj), (1, width, _cj)], t_rows, width // LANE, chunk=conv_chunk)

    dp_xs, gcw_xs, gcb_xs = m2_prep_bwd_call("m2_prep_bwd_x", 0, d, [dxs, dxs_skip])
    dp_b, gcw_b, gcb_b = m2_prep_bwd_call("m2_prep_bwd_b", d, M2G * NST, [db_ssd])
    dp_c, gcw_c, gcb_c = m2_prep_bwd_call("m2_prep_bwd_c", d + M2G * NST, M2G * NST, [dc_ssd])
    d_pxbc = jnp.concatenate([dp_xs, dp_b, dp_c], axis=1)
    gw_in("xbc", d_pxbc)
    g_m2_conv = jnp.concatenate([gcw_xs, gcw_b, gcw_c], axis=1)
    g_m2_conv_b = jnp.concatenate([gcb_xs, gcb_b, gcb_c], axis=1)

    def dn_out_bwd(rows, j, o, z, w, dy):
        _, vjp = jax.vjp(dn_out, o, z, w)
        return vjp(dy)

    d_o, d_z, g_dn_norm = _rw(
        "dn_out_bwd", dn_out_bwd,
        [("r", o_dn, d, _c0), ("r", proj["z"], d, _c0), ("p", dn_norm_w, HD, _c0), ("r", d_mixed, d, _c0)],
        [("r", d, d, _c0, F32), ("r", d, d, _c0, MXU), ("p", 1, HD, HD, _c0)], t_rows, _pick(t_rows, 208, 16))
    gw_in("z", d_z)

    dq, dk, dv, dbeta_r, dgdec_r = _gdn_bwd(q_act, k_act, v_act, head_rows(g1[:, :dnh], hb_b),
                                            head_rows(g1[:, dnh:2 * dnh], hb_b), dn_states, dn_tinv, d_o, hb_b)

    def dn_prep_bwd(sec, name, dact):
        def fn(rows, own, last, wins, pars):
            (p, da), (w,) = wins, pars
            _, vjp = jax.vjp(functools.partial(dn_post, sec), _conv(p, w))
            dcv, = vjp(jnp.where(valid(rows), da, 0.0))
            return [_conv_t(dcv, w, rows, t_end(last))], [_conv_w(jnp.where(own, dcv, 0.0), p, len(w))]
        wc = dnconv_f[:, sec * d:(sec + 1) * d]
        return _cv("dn_prep_bwd_" + name, fn, [(proj[name], _cj), (dact, _cj)], [(wc, _cj)], [(d, _cj, MXU)],
                   [(wc.shape[0], d, _cj)], t_rows, dnh, chunk=conv_chunk if sec == 2 else None)

    (dp_q, gcw_q), (dp_k, gcw_k), (dp_v, gcw_v) = dn_prep_bwd(0, "q", dq), dn_prep_bwd(1, "k", dk), dn_prep_bwd(2, "v", dv)
    gw_in("q", dp_q), gw_in("k", dp_k), gw_in("v", dp_v)
    g_dn_conv = jnp.concatenate([gcw_q, gcw_k, gcw_v], axis=1)

    zpad = jnp.zeros((t_rows, LANE - 2 * dnh - m2h), F32)
    dg1 = jnp.concatenate([head_cols(dbeta_r), head_cols(dgdec_r), head_cols(ddt_r), zpad], axis=1)
    dg2 = jnp.concatenate([jnp.zeros((t_rows, 2 * dnh), F32), head_cols(dam_r), zpad], axis=1)

    def gates_bwd(rows, j, sm, pa, pb, pc, pd, d1, d2):
        _, vjp = jax.vjp(lambda *a: gates(rows, *a), sm, pa, pb, pc, pd)
        return vjp((d1, d2))

    dp_sm, g_pa, g_pb, g_pc, g_pd = _rw(
        "gates_bwd", gates_bwd, gate_ins + [("r", dg1, LANE, _c0), ("r", dg2, LANE, _c0)],
        [("r", LANE, LANE, _c0, MXU)] + [("p", 1, LANE, LANE, _c0)] * 4, t_rows, tm_rw)

    dseg = {"q": dp_q, "k": dp_k, "v": dp_v, "z": d_z, "m2z": d_m2z, "xbc": d_pxbc, "sm": dp_sm}
    gw_in("sm", dp_sm)
    gsm = gw_seg["sm"]
    gw_in_full = jnp.concatenate([gw_seg["q"], gw_seg["k"], gw_seg["v"], gw_seg["z"], gsm[:, :2 * dnh], gw_seg["m2z"],
                                  gw_seg["xbc"], gsm[:, 2 * dnh:2 * dnh + m2h]], axis=1)
    x_in = _exchange_start("grad_in_start", [], [_to_shards(gw_in_full, 1).astype(WIRE)], gw_in_full)
    d_hn1 = None
    for s in dseg:
        d_hn1 = _mm("d_hn1_" + s, dseg[s], w_all_t, b_rows=seg_cols[s], tk=2048, add=d_hn1, dep=x_in[3])
    dh0, _, g_norm_mix = norm_bwd("norm_mix_bwd", h0, norm_mix_w, d_hn1, dh1)

    g_ffn_conv = jnp.concatenate([g_fc_g, g_fc_v], axis=1)
    small_parts = [_to_shards(dh0[PADR:CH], 1), _to_shards(g_dn_conv, 1), _to_shards(g_m2_conv, 1), _to_shards(g_ffn_conv, 1)]
    small_scatter = jnp.stack([_pack([p[k] for p in small_parts]) for k in range(NDEV)])

    rep_names = ["norm_mix_w", "dn_a_log", "dn_dt_bias", "dn_norm_w", "m2_conv_b", "m2_a_log", "m2_dt_bias", "m2_d",
                 "m2_norm_w", "norm_ffn_w", "norm_final_w"]
    rep_grads = [g_norm_mix, g_pa[:, dnh:2 * dnh], g_pb[:, dnh:2 * dnh], g_dn_norm, g_m2_conv_b, g_pc[:, 2 * dnh:2 * dnh + m2h],
                 g_pd[:, 2 * dnh:2 * dnh + m2h], g_m2_d[:, :m2h], g_m2_norm, g_norm_ffn, d_wf.reshape(d)]

    weights = dict(meta_tokens=meta_tokens, norm_mix_w=norm_mix_w, w_in=w_in, dn_conv_w=dn_conv_w, dn_a_log=dn_a_log,
                   dn_dt_bias=dn_dt_bias, dn_norm_w=dn_norm_w, m2_conv_w=m2_conv_w, m2_conv_b=m2_conv_b, m2_a_log=m2_a_log,
                   m2_dt_bias=m2_dt_bias, m2_d=m2_d, m2_norm_w=m2_norm_w, w_out=w_out, norm_ffn_w=norm_ffn_w, ffn_up=ffn_up,
                   ffn_conv_w=ffn_conv_w, ffn_down=ffn_down, norm_final_w=norm_final_w)
    mom1 = dict(meta_tokens=m_meta_tokens, norm_mix_w=m_norm_mix_w, w_in=m_w_in, dn_conv_w=m_dn_conv_w, dn_a_log=m_dn_a_log,
                dn_dt_bias=m_dn_dt_bias, dn_norm_w=m_dn_norm_w, m2_conv_w=m_m2_conv_w, m2_conv_b=m_m2_conv_b,
                m2_a_log=m_m2_a_log, m2_dt_bias=m_m2_dt_bias, m2_d=m_m2_d, m2_norm_w=m_m2_norm_w, w_out=m_w_out,
                norm_ffn_w=m_norm_ffn_w, ffn_up=m_ffn_up, ffn_conv_w=m_ffn_conv_w, ffn_down=m_ffn_down,
                norm_final_w=m_norm_final_w)
    mom2 = dict(meta_tokens=v_meta_tokens, norm_mix_w=v_norm_mix_w, w_in=v_w_in, dn_conv_w=v_dn_conv_w, dn_a_log=v_dn_a_log,
                dn_dt_bias=v_dn_dt_bias, dn_norm_w=v_dn_norm_w, m2_conv_w=v_m2_conv_w, m2_conv_b=v_m2_conv_b,
                m2_a_log=v_m2_a_log, m2_dt_bias=v_m2_dt_bias, m2_d=v_m2_d, m2_norm_w=v_m2_norm_w, w_out=v_w_out,
                norm_ffn_w=v_norm_ffn_w, ffn_up=v_ffn_up, ffn_conv_w=v_ffn_conv_w, ffn_down=v_ffn_down,
                norm_final_w=v_norm_final_w)
    res = {}

    def adam_big(name, started, after):
        staged, = _exchange_wait("grad_" + name + "_wait", started, after)
        outs = _adamw("adamw_" + name, staged, weights[name][0], mom1[name][0], mom2[name][0])
        res[name] = tuple(o[None] for o in outs)
        return outs[1]

    done = adam_big("ffn_down", x_down, dh0)
    done = adam_big("ffn_up", x_up, done)
    done = adam_big("w_out", x_out, done)
    st_rep, st_small = _exchange("exchange_small_grads", [_pack(rep_grads)], [small_scatter], done)
    adam_big("w_in", x_in, st_small)

    def adam_packed(label, staged, names):
        shapes = [weights[nm].shape for nm in names]
        outs = _adamw(label, staged, *[_pack([src[nm] for nm in names]) for src in (weights, mom1, mom2)])
        unpacked = [_unpack(o, shapes) for o in outs]
        for i, nm in enumerate(names):
            res[nm] = tuple(u[i] for u in unpacked)

    adam_packed("adamw_small_sharded", st_small, ["meta_tokens", "dn_conv_w", "m2_conv_w", "ffn_conv_w"])
    adam_packed("adamw_replicated", st_rep, rep_names)

    order = list(weights)
    grad_x = dh0[CH:][None]
    return (loss, grad_x, *[res[nm][0] for nm in order], *[res[nm][1] for nm in order], *[res[nm][2] for nm in order],
            *[res[nm][3] for nm in order])
```

```python
import functools
import math

import jax
import jax.numpy as jnp
from jax import lax
from jax.experimental import pallas as pl
from jax.experimental.pallas import tpu as pltpu

F32 = jnp.float32
MXU = jnp.bfloat16
WIRE = jnp.bfloat16
HI = lax.Precision.HIGH

NDEV = 8
CH = 64
NMETA = 16
PADR = CH - NMETA
EPS = 1e-6
HD = 128
M2P = 64
M2G = 4
SSD_GPS_F, SSD_GPS_B = 4, 1
NST = 128
LANE = 128

ADAM_LR, ADAM_B1, ADAM_B2, ADAM_EPS, ADAM_WD, ADAM_STEP = 0.001, 0.9, 0.999, 1e-08, 0.01, 10

MESH_AXES = ("x", "y", "c")


def _pick(n, target, mult=16):
    best = None
    for t in range(mult, min(n, target) + 1, mult):
        if n % t == 0:
            best = t
    return best if best is not None else n


def _dg(a, b, ca, cb):
    return lax.dot_general(a.astype(MXU), b.astype(MXU), (((ca,), (cb,)), ((), ())), preferred_element_type=F32)


def _dgh(a, b, ca, cb):
    return lax.dot_general(a, b, (((ca,), (cb,)), ((), ())), precision=HI, preferred_element_type=F32)


def _silu(x):
    return x * jax.nn.sigmoid(x)


def _softplus(x):
    return jnp.maximum(x, 0.0) + jnp.log1p(jnp.exp(-jnp.abs(x)))


def _rms(x, w):
    return x * lax.rsqrt(jnp.mean(x * x, axis=-1, keepdims=True) + EPS) * w


def _cparams(sem, vmem_mb):
    return pltpu.CompilerParams(dimension_semantics=sem, vmem_limit_bytes=vmem_mb << 20)


def _mm(name, a, b, *, ta=False, tb=False, add=None, out_dtype=F32, tm=1040, tn=1024, tk=2048, dep=None,
        b_rows=None, b_cols=None):
    m = a.shape[1] if ta else a.shape[0]
    (rs, rw), (cs, cw) = b_rows or (0, b.shape[0]), b_cols or (0, b.shape[1])
    (n, ns), (kdim, ks) = ((rw, rs), (cw, cs)) if tb else ((cw, cs), (rw, rs))
    assert kdim == (a.shape[0] if ta else a.shape[1])
    tm = _pick(m, tm, 128 if ta else 16)
    tn = _pick(math.gcd(n, ns), tn, 128)
    tk = _pick(math.gcd(kdim, ks), tk, 16 if (ta and not tb) else 128)
    nk = kdim // tk
    bj0, bk0 = ns // tn, ks // tk
    ca, cb = (0 if ta else 1), (1 if tb else 0)

    def body(*refs):
        a_ref, b_ref = refs[0], refs[1]
        add_ref = refs[2] if add is not None else None
        if nk == 1:
            r = _dg(a_ref[...], b_ref[...], ca, cb)
            if add_ref is not None:
                r = r + add_ref[...].astype(F32)
            refs[-1][...] = r.astype(refs[-1].dtype)
            return
        o_ref, acc = refs[-2], refs[-1]
        k = pl.program_id(2)

        @pl.when(k == 0)
        def _():
            acc[...] = jnp.zeros_like(acc)

        acc[...] += _dg(a_ref[...], b_ref[...], ca, cb)

        @pl.when(k == nk - 1)
        def _():
            r = acc[...]
            if add_ref is not None:
                r = r + add_ref[...].astype(F32)
            o_ref[...] = r.astype(o_ref.dtype)

    a_spec = pl.BlockSpec((tk, tm), lambda i, j, k: (k, i)) if ta else pl.BlockSpec((tm, tk), lambda i, j, k: (i, k))
    b_spec = (pl.BlockSpec((tn, tk), lambda i, j, k: (j + bj0, k + bk0)) if tb
              else pl.BlockSpec((tk, tn), lambda i, j, k: (k + bk0, j + bj0)))
    in_specs, ops = [a_spec, b_spec], [a, b]
    if add is not None:
        in_specs.append(pl.BlockSpec((tm, tn), lambda i, j, k: (i, j)))
        ops.append(add)
    if dep is not None:
        in_specs.append(pl.BlockSpec((8, LANE), lambda i, j, k: (0, 0)))
        ops.append(dep)
    return pl.pallas_call(
        body, name=name, grid=(m // tm, n // tn, nk), in_specs=in_specs,
        out_specs=pl.BlockSpec((tm, tn), lambda i, j, k: (i, j)),
        out_shape=jax.ShapeDtypeStruct((m, n), out_dtype),
        scratch_shapes=[pltpu.VMEM((tm, tn), F32)] if nk > 1 else [],
        compiler_params=_cparams(("parallel", "parallel", "arbitrary"), 48),
    )(*ops)


def _rw(name, fn, ins, outs, nrows, tm, ncol=1, vmem_mb=48):
    nrow = nrows // tm
    sub = tm
    in_specs, ops = [], []
    for spec in ins:
        kind, arr, bw, cj = spec[:4]
        ops.append(arr)
        if kind == "r":
            ri = spec[4] if len(spec) > 4 else (lambda i: i)
            in_specs.append(pl.BlockSpec((tm, bw), lambda j, i, cj=cj, ri=ri: (ri(i), cj(j))))
        else:
            in_specs.append(pl.BlockSpec((arr.shape[0], bw), lambda j, i, cj=cj: (0, cj(j))))
    out_shape, out_specs = [], []
    for o in outs:
        if o[0] == "r":
            _, width, bw, cj, dt = o
            out_shape.append(jax.ShapeDtypeStruct((nrows, width), dt))
            out_specs.append(pl.BlockSpec((tm, bw), lambda j, i, cj=cj: (i, cj(j))))
        else:
            _, rows, width, bw, cj = o
            out_shape.append(jax.ShapeDtypeStruct((rows, width), F32))
            out_specs.append(pl.BlockSpec((rows, bw), lambda j, i, cj=cj: (0, cj(j))))
    n_in = len(ins)

    def body(*refs):
        j, i = pl.program_id(0), pl.program_id(1)
        in_refs, out_refs = refs[:n_in], refs[n_in:]
        pars = [ref[...] if spec[0] == "p" else None for spec, ref in zip(ins, in_refs)]

        def one(r0, nr):
            rows = i * tm + r0 + lax.broadcasted_iota(jnp.int32, (nr, 1), 0)
            vals = [par if spec[0] == "p" else ref[pl.ds(r0, nr), :] for spec, ref, par in zip(ins, in_refs, pars)]
            parts = []
            for o, val, ref in zip(outs, fn(rows, j, *vals), out_refs):
                if o[0] == "r":
                    ref[pl.ds(r0, nr), :] = val.astype(ref.dtype)
                else:
                    parts.append(val)
            return parts

        if sub >= tm:
            parts = one(0, tm)
        else:
            zero = [jnp.zeros((1, o[3]), F32) for o in outs if o[0] == "p"]
            parts = lax.fori_loop(
                0, tm // sub, lambda s, acc: [a + b for a, b in zip(acc, one(pl.multiple_of(s * sub, sub), sub))], zero)
        for ref, val in zip([r for o, r in zip(outs, out_refs) if o[0] == "p"], parts):
            @pl.when(i == 0)
            def _(ref=ref):
                ref[...] = jnp.zeros_like(ref)

            ref[...] += val

    return pl.pallas_call(
        body, name=name, grid=(ncol, nrow), in_specs=in_specs, out_specs=out_specs, out_shape=out_shape,
        compiler_params=_cparams(("parallel", "arbitrary"), vmem_mb),
    )(*ops)


def _c0(j):
    return 0


def _cj(j):
    return j


def _shift(x, s):
    if s == 0:
        return x
    return pltpu.roll(x, s % x.shape[0], 0)


def _conv(x, w):
    k = len(w)
    return functools.reduce(lambda a, b: a + b, [w[j] * _shift(x, k - 1 - j) for j in range(k)])


def _conv_t(dy, w, rows, t_end):
    k = len(w)
    terms = []
    for j in range(k):
        s = k - 1 - j
        v = _shift(dy, -s)
        if t_end is not None and s > 0:
            v = jnp.where(rows + s < t_end, v, 0.0)
        terms.append(w[j] * v)
    return functools.reduce(lambda a, b: a + b, terms)


def _conv_w(dy, x, k):
    return [jnp.sum(dy * _shift(x, k - 1 - j), axis=0, keepdims=True) for j in range(k)]


CONV_CHUNK = 320
HALO = 8


def _cv(name, fn, row_ins, par_ins, row_outs, par_outs, nrows, ncol, chunk=None):
    whole = chunk is None
    chunk = nrows if whole else chunk
    n_chunks = nrows // chunk
    assert nrows % chunk == 0 and (whole or n_chunks >= 3)
    n_ri, n_pi, n_ro = len(row_ins), len(par_ins), len(row_outs)

    def body(*refs):
        rin, pin = refs[:n_ri], refs[n_ri:n_ri + n_pi]
        rout, pout = refs[n_ri + n_pi:n_ri + n_pi + n_ro], refs[n_ri + n_pi + n_ro:]
        pars = [[p[pl.ds(r, 1), :] for r in range(p.shape[0])] for p in pin]

        def run(r0, top, bot, last):
            wlen = top + chunk + bot
            w0 = r0 - top if isinstance(r0, int) else pl.multiple_of(r0 - top, HALO)
            local = lax.broadcasted_iota(jnp.int32, (wlen, 1), 0)
            own = jnp.logical_and(local >= top, local < top + chunk)
            outs, parts = fn(w0 + local, own, last, [ref[pl.ds(w0, wlen), :] for ref in rin], pars)
            for ref, val in zip(rout, outs):
                ref[pl.ds(r0, chunk), :] = val[top:top + chunk].astype(ref.dtype)
            return parts

        def add(acc, parts):
            return [[a + b for a, b in zip(ra, rb)] for ra, rb in zip(acc, parts)]

        if whole:
            acc = run(0, 0, 0, True)
        else:
            acc = run(0, 0, HALO, False)
            acc = lax.fori_loop(1, n_chunks - 1,
                                lambda i, a: add(a, run(pl.multiple_of(i * chunk, chunk), HALO, HALO, False)), acc)
            acc = add(acc, run(nrows - chunk, HALO, 0, True))
        for ref, prow in zip(pout, acc):
            for r, v in enumerate(prow):
                ref[pl.ds(r, 1), :] = v

    in_specs = [pl.BlockSpec((nrows, LANE), lambda j, cj=cj: (0, cj(j))) for _, cj in row_ins]
    in_specs += [pl.BlockSpec((a.shape[0], LANE), lambda j, cj=cj: (0, cj(j))) for a, cj in par_ins]
    out_specs = [pl.BlockSpec((nrows, LANE), lambda j, cj=cj: (0, cj(j))) for _, cj, _ in row_outs]
    out_specs += [pl.BlockSpec((k, LANE), lambda j, cj=cj: (0, cj(j))) for k, _, cj in par_outs]
    out_shape = [jax.ShapeDtypeStruct((nrows, width), dt) for width, _, dt in row_outs]
    out_shape += [jax.ShapeDtypeStruct((k, width), F32) for k, width, _ in par_outs]
    return pl.pallas_call(
        body, name=name, grid=(ncol,), in_specs=in_specs, out_specs=out_specs, out_shape=out_shape,
        compiler_params=_cparams(("parallel",), 48),
    )(*[a for a, _ in row_ins], *[a for a, _ in par_ins])


def _tri():
    r = lax.broadcasted_iota(jnp.int32, (CH, CH), 0)
    c = lax.broadcasted_iota(jnp.int32, (CH, CH), 1)
    return r, c


def _col(row):
    r, c = _tri()
    return jnp.sum(jnp.where(r == c, row, 0.0), axis=1, keepdims=True)


def _cumsum_rc(g_r):
    r, c = _tri()
    g_c = _col(g_r)
    cs_r = jnp.sum(jnp.where(r <= c, g_c, 0.0), axis=0, keepdims=True)
    cs_c = jnp.sum(jnp.where(c <= r, g_r, 0.0), axis=1, keepdims=True)
    return cs_r, cs_c


def _decay(cs_r, cs_c):
    r, c = _tri()
    return jnp.exp(jnp.where(c <= r, cs_c - cs_r, -jnp.inf))


def _gdn_a(ks, betas, gs):
    r, c = _tri()
    cs = [_cumsum_rc(g) for g in gs]
    kk = [_dg(k, k, 1, 1) for k in ks]
    return [jnp.where(c < r, _col(b) * kki * _decay(*csi), 0.0) for b, kki, csi in zip(betas, kk, cs)]


def _neumann(a_list):
    r, c = _tri()
    xs = [jnp.where(r == c, 1.0, 0.0) - a for a in a_list]
    ps = list(a_list)
    n = 2
    while n < CH:
        ps = [_dgh(p, p, 1, 0) for p in ps]
        xs = [x + _dgh(x, p, 1, 0) for x, p in zip(xs, ps)]
        n *= 2
    return xs


def _gdn_rest(ss, qs, ks, vs, betas, gs, ts):
    n = range(len(ss))
    cs = [_cumsum_rc(g) for g in gs]
    dm = [_decay(*csi) for csi in cs]
    ecs = [jnp.exp(csi[1]) for csi in cs]
    bc = [_col(b) for b in betas]
    sb = [s.astype(MXU) for s in ss]
    kb = [k.astype(MXU) for k in ks]
    u = [_dgh(ts[i], vs[i] * bc[i], 1, 0) for i in n]
    w = [_dgh(ts[i], ks[i] * (bc[i] * ecs[i]), 1, 0) for i in n]
    ws = [_dg(w[i], sb[i], 1, 0) for i in n]
    v_new = [u[i] - ws[i] for i in n]
    qk = [_dg(qs[i], kb[i], 1, 1) * dm[i] for i in n]
    o_in = [_dg(qs[i] * ecs[i], sb[i], 1, 0) for i in n]
    o = [o_in[i] + _dg(qk[i], v_new[i], 1, 0) for i in n]
    g_last = [jnp.sum(g, axis=1, keepdims=True) for g in gs]
    s_new = [ss[i] * jnp.exp(g_last[i]) + _dg(ks[i] * jnp.exp(g_last[i] - cs[i][1]), v_new[i], 0, 0) for i in n]
    return s_new, o


def _gdn_fwd(q, k, v, beta, g, hb):
    t_rows, d = q.shape
    nc, ng, w = t_rows // CH, d // (HD * hb), HD * hb
    sls = [slice(h * HD, (h + 1) * HD) for h in range(hb)]

    def body(q_ref, k_ref, v_ref, b_ref, g_ref, o_ref, ss_ref, ts_ref, s_scr):
        c = pl.program_id(1)

        @pl.when(c == 0)
        def _():
            s_scr[...] = jnp.zeros_like(s_scr)

        qs, ks, vs = ([ref[:, sl] for sl in sls] for ref in (q_ref, k_ref, v_ref))
        br = [b_ref[0, 0, pl.ds(h, 1), :] for h in range(hb)]
        gr = [g_ref[0, 0, pl.ds(h, 1), :] for h in range(hb)]
        s0 = [s_scr[h] for h in range(hb)]
        tm = _neumann(_gdn_a(ks, br, gr))
        s1, o = _gdn_rest(s0, qs, ks, vs, br, gr, tm)
        for h in range(hb):
            ss_ref[0, 0, h] = s0[h]
            ts_ref[0, 0, h] = tm[h]
            o_ref[:, sls[h]] = o[h]
            s_scr[h] = s1[h]

    blk = pl.BlockSpec((CH, w), lambda n, c: (c, n))
    row = pl.BlockSpec((1, 1, hb, CH), lambda n, c: (n, c, 0, 0))
    return pl.pallas_call(
        body, name="gdn_fwd", grid=(ng, nc), in_specs=[blk, blk, blk, row, row],
        out_specs=[blk, pl.BlockSpec((1, 1, hb, HD, HD), lambda n, c: (n, c, 0, 0, 0)),
                   pl.BlockSpec((1, 1, hb, CH, CH), lambda n, c: (n, c, 0, 0, 0))],
        out_shape=[jax.ShapeDtypeStruct((t_rows, d), F32), jax.ShapeDtypeStruct((ng, nc, hb, HD, HD), F32),
                   jax.ShapeDtypeStruct((ng, nc, hb, CH, CH), F32)],
        scratch_shapes=[pltpu.VMEM((hb, HD, HD), F32)],
        compiler_params=_cparams(("parallel", "arbitrary"), 32),
    )(q, k, v, beta, g)


def _gdn_bwd(q, k, v, beta, g, ss, ts, do, hb):
    t_rows, d = q.shape
    nc, ng, w = t_rows // CH, d // (HD * hb), HD * hb
    per_f = ss.shape[2] // hb
    sls = [slice(h * HD, (h + 1) * HD) for h in range(hb)]

    def body(q_ref, k_ref, v_ref, b_ref, g_ref, ss_ref, ts_ref, do_ref, dq_ref, dk_ref, dv_ref, db_ref, dg_ref, ds_scr):
        cr = pl.program_id(1)

        @pl.when(cr == 0)
        def _():
            ds_scr[...] = jnp.zeros_like(ds_scr)

        first = cr == nc - 1
        rowi = lax.broadcasted_iota(jnp.int32, (CH, 1), 0)
        lani = lax.broadcasted_iota(jnp.int32, (1, CH), 1)
        keep_c = jnp.logical_or(jnp.logical_not(first), rowi >= PADR)
        keep_r = jnp.logical_or(jnp.logical_not(first), lani >= PADR)
        hs = range(hb)
        qs, ks, vs, dos = ([ref[:, sl] for sl in sls] for ref in (q_ref, k_ref, v_ref, do_ref))
        br = [b_ref[0, 0, pl.ds(h, 1), :] for h in hs]
        gr = [g_ref[0, 0, pl.ds(h, 1), :] for h in hs]
        tm = [ts_ref[0, 0, h] for h in hs]
        _, vjp_rest = jax.vjp(_gdn_rest, [ss_ref[0, 0, h] for h in hs], qs, ks, vs, br, gr, tm)
        ds0, dq, dk, dv, db, dg, dt = vjp_rest(([ds_scr[h] for h in hs], dos))
        dtt = [_dgh(dt[h], tm[h], 1, 1) for h in hs]
        da = [-_dgh(tm[h], dtt[h], 0, 0) for h in hs]
        _, vjp_a = jax.vjp(_gdn_a, ks, br, gr)
        dk2, db2, dg2 = vjp_a(da)
        for h in hs:
            ds_scr[h] = ds0[h]
            dq_ref[:, sls[h]] = jnp.where(keep_c, dq[h], 0.0)
            dk_ref[:, sls[h]] = jnp.where(keep_c, dk[h] + dk2[h], 0.0)
            dv_ref[:, sls[h]] = jnp.where(keep_c, dv[h], 0.0)
            db_ref[0, 0, pl.ds(h, 1), :] = jnp.where(keep_r, db[h] + db2[h], 0.0)
            dg_ref[0, 0, pl.ds(h, 1), :] = jnp.where(keep_r, dg[h] + dg2[h], 0.0)

    blk = pl.BlockSpec((CH, w), lambda n, c: (nc - 1 - c, n))
    row = pl.BlockSpec((1, 1, hb, CH), lambda n, c: (n, nc - 1 - c, 0, 0))
    return pl.pallas_call(
        body, name="gdn_bwd", grid=(ng, nc),
        in_specs=[blk, blk, blk, row, row,
                  pl.BlockSpec((1, 1, hb, HD, HD), lambda n, c: (n // per_f, nc - 1 - c, n % per_f, 0, 0)),
                  pl.BlockSpec((1, 1, hb, CH, CH), lambda n, c: (n // per_f, nc - 1 - c, n % per_f, 0, 0)), blk],
        out_specs=[blk, blk, blk, row, row],
        out_shape=[jax.ShapeDtypeStruct((t_rows, d), F32)] * 3 + [jax.ShapeDtypeStruct((ng, nc, hb, CH), F32)] * 2,
        scratch_shapes=[pltpu.VMEM((hb, HD, HD), F32)],
        compiler_params=_cparams(("parallel", "arbitrary"), 32),
    )(q, k, v, beta, g, ss, ts, do)


def _ssd_group(s, xs, bm, cm, dt_r, a_r):
    prs = range(len(s))
    ngrp = bm.shape[1] // NST
    grp = [p // (len(s) // ngrp) for p in prs]
    bms = [bm[:, g * NST:(g + 1) * NST].astype(MXU) for g in range(ngrp)]
    cms = [cm[:, g * NST:(g + 1) * NST].astype(MXU) for g in range(ngrp)]
    first = lax.broadcasted_iota(jnp.int32, (1, 2 * M2P), 1) < M2P

    def pick(vals, p):
        return jnp.where(first, vals[2 * p], vals[2 * p + 1])

    cs = [_cumsum_rc(a) for a in a_r]
    lm = [_decay(*csi) for csi in cs]
    ecs = [jnp.exp(csi[1]) for csi in cs]
    alast = [jnp.sum(a, axis=1, keepdims=True) for a in a_r]
    ealast = [jnp.exp(al) for al in alast]
    wt = [jnp.exp(al - csi[1]) for al, csi in zip(alast, cs)]
    dtc = [_col(t) for t in dt_r]
    xdt = [xs[:, p * LANE:(p + 1) * LANE] * pick(dtc, p) for p in prs]
    cb = [_dg(cms[g], bms[g], 1, 1) for g in range(ngrp)]
    y0 = [_dg(cb[grp[p]] * lm[2 * p], xdt[p], 1, 0) for p in prs]
    y1 = [_dg(cb[grp[p]] * lm[2 * p + 1], xdt[p], 1, 0) for p in prs]
    yo = [_dg(cms[grp[p]], s[p], 1, 0) for p in prs]
    y = [jnp.where(first, y0[p], y1[p]) + yo[p] * pick(ecs, p) for p in prs]
    s_new = [s[p] * pick(ealast, p) + _dg(bms[grp[p]], xdt[p] * pick(wt, p), 0, 0) for p in prs]
    return s_new, jnp.concatenate(y, axis=1)


def _ssd_specs(nc, d, rev, gps):
    assert M2G % gps == 0 and (d // LANE) % gps == 0
    hps = (d // M2P) // M2G * gps
    cc = (lambda c: nc - 1 - c) if rev else (lambda c: c)
    xs = pl.BlockSpec((CH, hps * M2P), lambda g, c: (cc(c), g))
    bm = pl.BlockSpec((CH, NST * gps), lambda g, c: (cc(c), (d // LANE) // gps + g))
    cm = pl.BlockSpec((CH, NST * gps), lambda g, c: (cc(c), (d // LANE + M2G) // gps + g))
    row = pl.BlockSpec((1, 1, hps, CH), lambda g, c: (g, cc(c), 0, 0))
    return xs, bm, cm, row, hps


def _ssd_fwd(xbc, dt, a, d, gps):
    t_rows = xbc.shape[0]
    nc = t_rows // CH
    xs, bm, cm, row, hpg = _ssd_specs(nc, d, False, gps)
    ppg = hpg // 2
    st = pl.BlockSpec((1, 1, ppg, NST, LANE), lambda g, c: (g, c, 0, 0, 0))

    def body(xs_ref, b_ref, c_ref, dt_ref, a_ref, y_ref, ss_ref, s_scr):
        c = pl.program_id(1)

        @pl.when(c == 0)
        def _():
            s_scr[...] = jnp.zeros_like(s_scr)

        s0 = [s_scr[p] for p in range(ppg)]
        for p in range(ppg):
            ss_ref[0, 0, p] = s0[p]
        dt_r = [dt_ref[0, 0, pl.ds(h, 1), :] for h in range(hpg)]
        a_r = [a_ref[0, 0, pl.ds(h, 1), :] for h in range(hpg)]
        s1, y = _ssd_group(s0, xs_ref[...], b_ref[...], c_ref[...], dt_r, a_r)
        y_ref[...] = y
        for p in range(ppg):
            s_scr[p] = s1[p]

    return pl.pallas_call(
        body, name="ssd_fwd", grid=(M2G // gps, nc), in_specs=[xs, bm, cm, row, row], out_specs=[xs, st],
        out_shape=[jax.ShapeDtypeStruct((t_rows, d), F32), jax.ShapeDtypeStruct((M2G // gps, nc, ppg, NST, LANE), F32)],
        scratch_shapes=[pltpu.VMEM((ppg, NST, LANE), F32)],
        compiler_params=_cparams(("parallel", "arbitrary"), 32),
    )(xbc, xbc, xbc, dt, a)


def _ssd_bwd(xbc, dt, a, ss, dy, d, gps):
    t_rows = xbc.shape[0]
    nc = t_rows // CH
    xs, bm, cm, row, hpg = _ssd_specs(nc, d, True, gps)
    ppg = hpg // 2
    per_f = ss.shape[2] // ppg
    st = pl.BlockSpec((1, 1, ppg, NST, LANE), lambda g, c: (g // per_f, nc - 1 - c, g % per_f, 0, 0))

    def body(xs_ref, b_ref, c_ref, dt_ref, a_ref, ss_ref, dy_ref, dxs_ref, db_ref, dc_ref, ddt_ref, da_ref, ds_scr):
        cr = pl.program_id(1)

        @pl.when(cr == 0)
        def _():
            ds_scr[...] = jnp.zeros_like(ds_scr)

        first = cr == nc - 1
        keep_c = jnp.logical_or(jnp.logical_not(first), lax.broadcasted_iota(jnp.int32, (CH, 1), 0) >= PADR)
        keep_r = jnp.logical_or(jnp.logical_not(first), lax.broadcasted_iota(jnp.int32, (1, CH), 1) >= PADR)
        dt_r = [dt_ref[0, 0, pl.ds(h, 1), :] for h in range(hpg)]
        a_r = [a_ref[0, 0, pl.ds(h, 1), :] for h in range(hpg)]
        s0 = [ss_ref[0, 0, p] for p in range(ppg)]
        _, vjp = jax.vjp(_ssd_group, s0, xs_ref[...], b_ref[...], c_ref[...], dt_r, a_r)
        ds0, dxs, db, dc, ddt, da = vjp(([ds_scr[p] for p in range(ppg)], dy_ref[...]))
        for p in range(ppg):
            ds_scr[p] = ds0[p]
        dxs_ref[...] = jnp.where(keep_c, dxs, 0.0)
        db_ref[...] = jnp.where(keep_c, db, 0.0)
        dc_ref[...] = jnp.where(keep_c, dc, 0.0)
        for h in range(hpg):
            ddt_ref[0, 0, pl.ds(h, 1), :] = jnp.where(keep_r, ddt[h], 0.0)
            da_ref[0, 0, pl.ds(h, 1), :] = jnp.where(keep_r, da[h], 0.0)

    grp = pl.BlockSpec((CH, NST * gps), lambda g, c: (nc - 1 - c, g))
    return pl.pallas_call(
        body, name="ssd_bwd", grid=(M2G // gps, nc), in_specs=[xs, bm, cm, row, row, st, xs],
        out_specs=[xs, grp, grp, row, row],
        out_shape=[jax.ShapeDtypeStruct((t_rows, d), F32)] + [jax.ShapeDtypeStruct((t_rows, M2G * NST), F32)] * 2
        + [jax.ShapeDtypeStruct((M2G // gps, nc, hpg, CH), F32)] * 2,
        scratch_shapes=[pltpu.VMEM((ppg, NST, LANE), F32)],
        compiler_params=_cparams(("parallel", "arbitrary"), 32),
    )(xbc, xbc, xbc, dt, a, ss, dy)


def _exchange(name, gathers, scatters, after):
    arrays = list(gathers) + list(scatters)
    n_g, n = len(gathers), len(arrays)

    def body(*refs):
        ins, outs = refs[:n], refs[n + 1:2 * n + 1]
        send_sems, recv_sems, local_sems = refs[2 * n + 1:]
        x, y, c = lax.axis_index("x"), lax.axis_index("y"), lax.axis_index("c")
        me = 4 * x + 2 * y + c

        def src(a, slot):
            return ins[a] if a < n_g else ins[a].at[slot]

        local = [pltpu.make_async_copy(src(a, me), outs[a].at[me], local_sems.at[a]) for a in range(n)]
        for cp in local:
            cp.start()
        copies = []
        for rel in range(1, NDEV):
            px, py, pc = x ^ (rel >> 2), y ^ ((rel >> 1) & 1), c ^ (rel & 1)
            peer = 4 * px + 2 * py + pc
            for a in range(n):
                copies.append(pltpu.make_async_remote_copy(
                    src_ref=src(a, peer), dst_ref=outs[a].at[me], send_sem=send_sems.at[a, rel - 1],
                    recv_sem=recv_sems.at[a, rel - 1], device_id=(px, py, pc), device_id_type=pl.DeviceIdType.MESH))
        for cp in copies:
            cp.start()
        for cp in copies:
            cp.wait_recv()
        for cp in copies:
            cp.wait_send()
        for cp in local:
            cp.wait()

    any_spec = pl.BlockSpec(memory_space=pl.ANY)
    out_shape = [jax.ShapeDtypeStruct((NDEV,) + a.shape, a.dtype) for a in gathers]
    out_shape += [jax.ShapeDtypeStruct(a.shape, a.dtype) for a in scatters]
    return pl.pallas_call(
        body, name=name, in_specs=[any_spec] * (n + 1), out_specs=[any_spec] * n, out_shape=out_shape,
        scratch_shapes=[pltpu.SemaphoreType.DMA((n, NDEV - 1)), pltpu.SemaphoreType.DMA((n, NDEV - 1)),
                        pltpu.SemaphoreType.DMA((n,))],
        compiler_params=pltpu.CompilerParams(has_side_effects=True),
    )(*arrays, after)


def _gather_two_level(name, arrays):
    n = len(arrays)

    def body(*refs):
        ins, outs = refs[:n], refs[n:2 * n]
        send_sems, recv_sems, local_sems = refs[2 * n:]
        x, y, c = lax.axis_index("x"), lax.axis_index("y"), lax.axis_index("c")
        me, sibling = (x, y, c), (x, y, 1 - c)
        chips = [(1 - x, y), (x, 1 - y), (1 - x, 1 - y)]

        def copy(a, k, block, to, src=None):
            dst = outs[a].at[4 * block[0] + 2 * block[1] + block[2]]
            return pltpu.make_async_remote_copy(
                src_ref=dst if src is None else src, dst_ref=dst, send_sem=send_sems.at[a, k], recv_sem=recv_sems.at[a, k],
                device_id=to, device_id_type=pl.DeviceIdType.MESH)

        mine = [pltpu.make_async_copy(ins[a], outs[a].at[4 * x + 2 * y + c], local_sems.at[a]) for a in range(n)]
        for cp in mine:
            cp.start()
        first = []
        for a in range(n):
            first.append(copy(a, 0, me, sibling, src=ins[a]))
            first += [copy(a, 1 + j, me, (*chip, c), src=ins[a]) for j, chip in enumerate(chips)]
        for cp in first:
            cp.start()
        passed = [[copy(a, 4 + j, (*chip, c), sibling) for j, chip in enumerate(chips)] for a in range(n)]
        for j, chip in enumerate(chips):
            for a in range(n):
                copy(a, 1 + j, (*chip, c), me).wait_recv()
                passed[a][j].start()
        for a in range(n):
            copy(a, 0, sibling, me).wait_recv()
            for j, chip in enumerate(chips):
                copy(a, 4 + j, (*chip, 1 - c), me).wait_recv()
        for cp in first + [cp for row in passed for cp in row]:
            cp.wait_send()
        for cp in mine:
            cp.wait()

    any_spec = pl.BlockSpec(memory_space=pl.ANY)
    return pl.pallas_call(
        body, name=name, in_specs=[any_spec] * n, out_specs=[any_spec] * n,
        out_shape=[jax.ShapeDtypeStruct((NDEV,) + a.shape, a.dtype) for a in arrays],
        scratch_shapes=[pltpu.SemaphoreType.DMA((n, NDEV - 1)), pltpu.SemaphoreType.DMA((n, NDEV - 1)),
                        pltpu.SemaphoreType.DMA((n,))],
        compiler_params=pltpu.CompilerParams(has_side_effects=True),
    )(*arrays)


_HBM = pl.BlockSpec(memory_space=pltpu.HBM)
_SEM = pl.BlockSpec(memory_space=pltpu.SEMAPHORE)
_EFFECT = pltpu.SideEffectType.DATAFLOW_SIDE_EFFECTING


def _split_copies(srcs, lands, send_sems, recv_sems, n_g):
    x, y, c = lax.axis_index("x"), lax.axis_index("y"), lax.axis_index("c")
    me = 4 * x + 2 * y + c
    copies = []
    for rel in range(1, NDEV):
        px, py, pc = x ^ (rel >> 2), y ^ ((rel >> 1) & 1), c ^ (rel & 1)
        peer = 4 * px + 2 * py + pc
        for a in range(len(srcs)):
            copies.append(pltpu.make_async_remote_copy(
                src_ref=srcs[a] if a < n_g else srcs[a].at[peer], dst_ref=lands[a].at[me],
                send_sem=send_sems.at[a * (NDEV - 1) + rel - 1], recv_sem=recv_sems.at[a * (NDEV - 1) + rel - 1],
                device_id=(px, py, pc), device_id_type=pl.DeviceIdType.MESH))
    return copies


def _exchange_start(name, gathers, scatters, after):
    arrays = list(gathers) + list(scatters)
    n_g, n = len(gathers), len(arrays)
    lands = [lax.empty((NDEV,) + a.shape, a.dtype) for a in gathers] + [lax.empty(a.shape, a.dtype) for a in scatters]

    def body(*refs):
        send_sems, recv_sems = refs[2 * n + 1], refs[2 * n + 2]
        for cp in _split_copies(refs[:n], refs[n:2 * n], send_sems, recv_sems, n_g):
            cp.start()
        refs[-1][...] = jnp.zeros_like(refs[-1])

    sems = pltpu.SemaphoreType.DMA((n * (NDEV - 1),))
    out = pl.pallas_call(
        body, name=name, in_specs=[_HBM] * (2 * n) + [pl.BlockSpec(memory_space=pl.ANY)],
        out_specs=(_SEM, _SEM, *[_HBM] * (2 * n), pl.BlockSpec(memory_space=pltpu.VMEM)),
        out_shape=(sems, sems, *[pltpu.HBM(a.shape, a.dtype) for a in arrays + lands], jax.ShapeDtypeStruct((8, LANE), F32)),
        input_output_aliases={i: 2 + i for i in range(2 * n)},
        compiler_params=pltpu.CompilerParams(has_side_effects=_EFFECT),
    )(*[pltpu.with_memory_space_constraint(a, pltpu.HBM) for a in arrays + lands], after)
    return out[0], out[1], list(out[2:2 + 2 * n]), out[-1], n_g


def _exchange_wait(name, started, after):
    send_sems, recv_sems, thru, _, n_g = started
    n = len(thru) // 2

    def body(*refs):
        for cp in _split_copies(refs[:n], refs[n:2 * n], refs[2 * n], refs[2 * n + 1], n_g):
            cp.wait_send()
            cp.wait_recv()

    out = pl.pallas_call(
        body, name=name, in_specs=[_HBM] * (2 * n) + [_SEM, _SEM, pl.BlockSpec(memory_space=pl.ANY)],
        out_specs=[_HBM] * (2 * n), out_shape=[pltpu.HBM(a.shape, a.dtype) for a in thru],
        input_output_aliases={i: i for i in range(2 * n)},
        compiler_params=pltpu.CompilerParams(has_side_effects=_EFFECT),
    )(*thru, send_sems, recv_sems, after)
    me = 4 * lax.axis_index("x") + 2 * lax.axis_index("y") + lax.axis_index("c")
    full = []
    for a in range(n):
        own = out[a][None] if a < n_g else lax.dynamic_index_in_dim(out[a], me, 0, keepdims=True)
        full.append(lax.dynamic_update_index_in_dim(out[n + a], own, me, 0))
    return full


def _adamw(name, staged, w, m, v):
    r, c = w.shape
    tr = _pick(r, 256, 8)

    def body(st_ref, w_ref, m_ref, v_ref, g_ref, d_ref, nm_ref, nv_ref):
        g = st_ref[0].astype(F32)
        for k in range(1, NDEV):
            g = g + st_ref[k].astype(F32)
        m_new = ADAM_B1 * m_ref[...] + (1.0 - ADAM_B1) * g
        v_new = ADAM_B2 * v_ref[...] + (1.0 - ADAM_B2) * jnp.square(g)
        m_hat = m_new / (1.0 - ADAM_B1 ** ADAM_STEP)
        v_hat = v_new / (1.0 - ADAM_B2 ** ADAM_STEP)
        g_ref[...] = g
        d_ref[...] = -ADAM_LR * (m_hat / (jnp.sqrt(v_hat) + ADAM_EPS) + ADAM_WD * w_ref[...])
        nm_ref[...] = m_new
        nv_ref[...] = v_new

    blk = pl.BlockSpec((tr, c), lambda i: (i, 0))
    return pl.pallas_call(
        body, name=name, grid=(r // tr,), in_specs=[pl.BlockSpec((NDEV, tr, c), lambda i: (0, i, 0)), blk, blk, blk],
        out_specs=[blk] * 4, out_shape=[jax.ShapeDtypeStruct((r, c), F32)] * 4,
        compiler_params=_cparams(("parallel",), 48),
    )(staged, w, m, v)


def _pack(parts):
    flat = jnp.concatenate([p.reshape(-1).astype(F32) for p in parts])
    pad = (-flat.shape[0]) % (8 * LANE)
    return jnp.pad(flat, (0, pad)).reshape(-1, LANE)


def _unpack(slab, shapes):
    flat, out, off = slab.reshape(-1), [], 0
    for s in shapes:
        n = 1
        for dim in s:
            n *= dim
        out.append(flat[off:off + n].reshape(s))
        off += n
    return out


def _to_shards(full, axis):
    shp = full.shape
    t = full.reshape(shp[:axis] + (NDEV, shp[axis] // NDEV) + shp[axis + 1:])
    return jnp.moveaxis(t, axis, 0)


def _from_shards(g, axis):
    t = jnp.moveaxis(g, 0, axis)
    shp = t.shape
    return t.reshape(shp[:axis] + (shp[axis] * shp[axis + 1],) + shp[axis + 2:])


def kernel(x, meta_tokens, norm_mix_w, w_in, dn_conv_w, dn_a_log, dn_dt_bias, dn_norm_w, m2_conv_w, m2_conv_b, m2_a_log, m2_dt_bias, m2_d, m2_norm_w, w_out, norm_ffn_w, ffn_up, ffn_conv_w, ffn_down, norm_final_w, loss_target, m_meta_tokens, m_norm_mix_w, m_w_in, m_dn_conv_w, m_dn_a_log, m_dn_dt_bias, m_dn_norm_w, m_m2_conv_w, m_m2_conv_b, m_m2_a_log, m_m2_dt_bias, m_m2_d, m_m2_norm_w, m_w_out, m_norm_ffn_w, m_ffn_up, m_ffn_conv_w, m_ffn_down, m_norm_final_w, v_meta_tokens, v_norm_mix_w, v_w_in, v_dn_conv_w, v_dn_a_log, v_dn_dt_bias, v_dn_norm_w, v_m2_conv_w, v_m2_conv_b, v_m2_a_log, v_m2_dt_bias, v_m2_d, v_m2_norm_w, v_w_out, v_norm_ffn_w, v_ffn_up, v_ffn_conv_w, v_ffn_down, v_norm_final_w):
    seq, d = x.shape[1], x.shape[2]
    t_rows = seq + CH
    nc = t_rows // CH
    dnh, m2h = d // HD, d // M2P
    dff = ffn_down.shape[1] * NDEV
    xbc_w = d + 2 * M2G * NST
    assert seq % CH == 0 and d % (2 * M2P * M2G) == 0 and 2 * dnh + m2h <= LANE
    hb_f = max(h for h in (16, 8, 4, 2, 1) if dnh % h == 0)
    hb_b = max(h for h in (8, 4, 2, 1) if dnh % h == 0)
    tm_rw = _pick(t_rows, 208, 16)
    conv_chunk = _pick(t_rows, min(CONV_CHUNK, t_rows // 3), 16)

    small_sharded = [meta_tokens, dn_conv_w[0], m2_conv_w[0], ffn_conv_w[0]]
    small_shapes = [p.shape for p in small_sharded]
    g_win, g_small = _gather_two_level("gather_w_in", [w_in[0].astype(WIRE).T, _pack(small_sharded)])
    rest = _exchange_start("gather_rest_start", [w_out[0].astype(WIRE), ffn_up[0].astype(WIRE), ffn_down[0].astype(WIRE)], [],
                           g_small)
    win_t = g_win.reshape(-1, d)
    small_full = [_unpack(g_small[k], small_shapes) for k in range(NDEV)]
    meta_f, dnconv_f, m2conv_f, ffnconv_f = [jnp.concatenate([small_full[k][i] for k in range(NDEV)], axis=-1) for i in range(4)]

    o_z, o_b, o_a = 3 * d, 4 * d, 4 * d + dnh
    o_m2z = 4 * d + 2 * dnh
    o_xbc, o_dt = o_m2z + d, o_m2z + d + xbc_w
    w_all_t = jnp.concatenate([win_t[:o_b], win_t[o_m2z:o_dt], win_t[o_b:o_m2z], win_t[o_dt:],
                               jnp.zeros((LANE - 2 * dnh - m2h, d), WIRE)], axis=0)
    seg_cols = {"q": (0, d), "k": (d, d), "v": (2 * d, d), "z": (3 * d, d), "m2z": (4 * d, d), "xbc": (5 * d, xbc_w),
                "sm": (5 * d + xbc_w, LANE)}

    h0 = jnp.concatenate([jnp.zeros((PADR, d), F32), meta_f, x[0]], axis=0)
    valid = lambda rows: rows >= PADR

    def norm_fwd(name, h, w):
        return _rw(name, lambda rows, j, hv, wv: (_rms(hv, wv),), [("r", h, d, _c0), ("p", w, d, _c0)],
                   [("r", d, d, _c0, MXU)], t_rows, tm_rw)[0]

    hn1 = norm_fwd("norm_mix", h0, norm_mix_w)
    proj = {s: _mm("proj_" + s, hn1, w_all_t, tb=True, b_rows=seg_cols[s], dep=rest[3]) for s in seg_cols}

    def dn_post(sec, cv):
        s = _silu(cv)
        if sec < 2:
            s = s * lax.rsqrt(jnp.sum(s * s, axis=-1, keepdims=True) + EPS)
        if sec == 0:
            s = s * (HD ** -0.5)
        return s

    def dn_prep(sec, name):
        def fn(rows, own, last, wins, pars):
            return [jnp.where(valid(rows), dn_post(sec, _conv(wins[0], pars[0])), 0.0)], []
        wc = dnconv_f[:, sec * d:(sec + 1) * d]
        return _cv("dn_prep_" + name, fn, [(proj[name], _cj)], [(wc, _cj)], [(d, _cj, F32)], [], t_rows, dnh)[0]

    q_act, k_act, v_act = dn_prep(0, "q"), dn_prep(1, "k"), dn_prep(2, "v")

    lane = lambda: lax.broadcasted_iota(jnp.int32, (1, LANE), 1)

    def lanes_of(vec, off):
        return jnp.pad(vec.astype(F32), ((0, 0), (off, LANE - off - vec.shape[1])))

    gate_params = [lanes_of(dn_a_log, dnh), lanes_of(dn_dt_bias, dnh), lanes_of(m2_a_log, 2 * dnh), lanes_of(m2_dt_bias, 2 * dnh)]

    def gates(rows, sm, p_alog, p_dtb, p_malog, p_mdtb):
        ln = lane()
        is_b, is_g = ln < dnh, jnp.logical_and(ln >= dnh, ln < 2 * dnh)
        is_d = jnp.logical_and(ln >= 2 * dnh, ln < 2 * dnh + m2h)
        beta = jax.nn.sigmoid(sm)
        gdec = -jnp.exp(p_alog) * _softplus(sm + p_dtb)
        dt = _softplus(sm + p_mdtb)
        am = dt * (-jnp.exp(p_malog))
        ok = valid(rows)
        g1 = jnp.where(ok, jnp.where(is_b, beta, jnp.where(is_g, gdec, jnp.where(is_d, dt, 0.0))), 0.0)
        g2 = jnp.where(jnp.logical_and(ok, is_d), am, 0.0)
        return g1, g2

    gate_ins = [("r", proj["sm"], LANE, _c0)] + [("p", p, LANE, _c0) for p in gate_params]
    g1, g2 = _rw("gates", lambda rows, j, *a: gates(rows, *a), gate_ins,
                 [("r", LANE, LANE, _c0, F32), ("r", LANE, LANE, _c0, F32)], t_rows, tm_rw)

    def head_rows(cols, per):
        n = cols.shape[1]
        return cols.reshape(nc, CH, n // per, per).transpose(2, 0, 3, 1)

    def head_cols(rows_):
        ngrp, _, per, _ = rows_.shape
        return rows_.transpose(1, 3, 0, 2).reshape(t_rows, ngrp * per)

    beta_r, gdec_r = head_rows(g1[:, :dnh], hb_f), head_rows(g1[:, dnh:2 * dnh], hb_f)
    hpg = m2h // M2G
    m2_rows = lambda gps: (head_rows(g1[:, 2 * dnh:2 * dnh + m2h], hpg * gps), head_rows(g2[:, 2 * dnh:2 * dnh + m2h], hpg * gps))

    o_dn, dn_states, dn_tinv = _gdn_fwd(q_act, k_act, v_act, beta_r, gdec_r, hb_f)

    def dn_out(o, z, w):
        outs = []
        for h in range(dnh):
            sl = slice(h * HD, (h + 1) * HD)
            outs.append(_rms(o[:, sl], w) * _silu(z[:, sl]))
        return jnp.concatenate(outs, axis=1)

    mixed_dn = _rw("dn_out", lambda rows, j, o, z, w: (dn_out(o, z, w),),
                   [("r", o_dn, d, _c0), ("r", proj["z"], d, _c0), ("p", dn_norm_w, HD, _c0)], [("r", d, d, _c0, MXU)],
                   t_rows, tm_rw)[0]

    def m2_prep(rows, own, last, wins, pars):
        return [jnp.where(valid(rows), _silu(_conv(wins[0], pars[0]) + pars[1][0]), 0.0)], []

    xbc_act = _cv("m2_prep", m2_prep, [(proj["xbc"], _cj)], [(m2conv_f, _cj), (m2_conv_b, _cj)], [(xbc_w, _cj, F32)], [],
                  t_rows, xbc_w // LANE)[0]
    y_ssd, m2_states = _ssd_fwd(xbc_act, *m2_rows(SSD_GPS_F), d, SSD_GPS_F)

    d_lanes = jnp.repeat(m2_d.astype(F32), M2P, axis=1)
    gw = d // M2G

    def m2_out(ys, xs, z, dl, nw):
        yv = (ys + dl * xs) * _silu(z)
        outs = []
        for gi in range(M2G):
            sl = slice(gi * gw, (gi + 1) * gw)
            outs.append(_rms(yv[:, sl], nw[:, sl]))
        return jnp.concatenate(outs, axis=1)

    m2_out_ins = [("r", y_ssd, d, _c0), ("r", xbc_act, d, _c0), ("r", proj["m2z"], d, _c0), ("p", d_lanes, d, _c0),
                  ("p", m2_norm_w, d, _c0)]
    mixed_m2 = _rw("m2_out", lambda rows, j, *a: (m2_out(*a),), m2_out_ins, [("r", d, d, _c0, MXU)], t_rows, tm_rw)[0]

    mixed = jnp.concatenate([mixed_dn, mixed_m2], axis=1)
    g_wout, g_wup, g_wdown = _exchange_wait("gather_rest_wait", rest, mixed)
    wout = _from_shards(g_wout, 0)
    wup = _from_shards(g_wup, 1)
    wdown = _from_shards(g_wdown, 0)
    up_g, up_v = (0, dff), (dff, dff)
    h1 = _mm("out_proj", mixed, wout, add=h0)
    hn2 = norm_fwd("norm_ffn", h1, norm_ffn_w)
    u_g, u_v = _mm("ffn_up_g", hn2, wup, b_cols=up_g), _mm("ffn_up_v", hn2, wup, b_cols=up_v)
    fc_g, fc_v = ffnconv_f[:, :dff], ffnconv_f[:, dff:]

    def ffn_act(rows, own, last, wins, pars):
        return [jnp.where(valid(rows), _silu(_conv(wins[0], pars[0])) * _conv(wins[1], pars[1]), 0.0)], []

    act = _cv("ffn_act", ffn_act, [(u_g, _cj), (u_v, _cj)], [(fc_g, _cj), (fc_v, _cj)], [(dff, _cj, MXU)], [],
              t_rows, dff // LANE)[0]
    h2 = _mm("ffn_down", act, wdown, add=h1, tk=1408)

    def loss_fn(hv, wf, tgt, rows):
        err = jnp.where(rows >= CH, _rms(hv, wf) - tgt, 0.0)
        return 0.5 * jnp.sum(jnp.mean(err * err, axis=-1, keepdims=True), axis=0, keepdims=True)

    def final(rows, j, hv, wf, tgt):
        loss, vjp = jax.vjp(lambda a, b: loss_fn(a, b, tgt, rows), hv, wf)
        dh, dw = vjp(jnp.ones((1, 1), F32))
        return dh, dh, dw, jnp.broadcast_to(loss, (1, LANE))

    wf2 = norm_final_w.reshape(1, d)
    dh2, dh2_m, d_wf, loss_part = _rw(
        "loss_head", final, [("r", h2, d, _c0), ("p", wf2, d, _c0), ("r", loss_target[0], d, _c0, lambda i: jnp.maximum(i - 1, 0))],
        [("r", d, d, _c0, F32), ("r", d, d, _c0, MXU), ("p", 1, d, d, _c0), ("p", 1, LANE, LANE, _c0)], t_rows, CH)
    loss = lax.psum(loss_part[0, 0], MESH_AXES)

    d_act = _mm("d_act", dh2_m, wdown, tb=True)
    gw_down = _mm("gw_down", act, dh2_m, ta=True, tm=1408, tn=1024, tk=2080, out_dtype=WIRE)
    x_down = _exchange_start("grad_down_start", [], [_to_shards(gw_down, 0).astype(WIRE)], gw_down)

    def t_end(last):
        return t_rows if last else None

    def ffn_act_bwd(rows, own, last, wins, pars):
        (ug, uv, da), (wg, wv) = wins, pars
        cg, cv = _conv(ug, wg), _conv(uv, wv)
        _, vjp = jax.vjp(lambda a, b: _silu(a) * b, cg, cv)
        dcg, dcv = vjp(jnp.where(valid(rows), da, 0.0))
        return ([_conv_t(dcg, wg, rows, t_end(last)), _conv_t(dcv, wv, rows, t_end(last))],
                [_conv_w(jnp.where(own, dcg, 0.0), ug, len(wg)), _conv_w(jnp.where(own, dcv, 0.0), uv, len(wv))])

    kf = fc_g.shape[0]
    du_g, du_v, g_fc_g, g_fc_v = _cv(
        "ffn_act_bwd", ffn_act_bwd, [(u_g, _cj), (u_v, _cj), (d_act, _cj)], [(fc_g, _cj), (fc_v, _cj)],
        [(dff, _cj, MXU), (dff, _cj, MXU)], [(kf, dff, _cj), (kf, dff, _cj)], t_rows, dff // LANE, chunk=conv_chunk)
    gw_up_g = _mm("gw_up_g", hn2, du_g, ta=True, tm=1024, tn=1408, tk=2080, out_dtype=WIRE, dep=x_down[3])
    gw_up_v = _mm("gw_up_v", hn2, du_v, ta=True, tm=1024, tn=1408, tk=2080, out_dtype=WIRE)
    gw_up_full = jnp.concatenate([gw_up_g, gw_up_v], axis=1)
    x_up = _exchange_start("grad_up_start", [], [_to_shards(gw_up_full, 1).astype(WIRE)], gw_up_full)
    d_hn2 = _mm("d_hn2_v", du_v, wup, b_cols=up_v, tb=True, tk=1408, dep=x_up[3],
                add=_mm("d_hn2_g", du_g, wup, b_cols=up_g, tb=True, tk=1408, tn=2048, dep=x_up[3]))

    def norm_bwd(name, h, w, dy, dres):
        def fn(rows, j, hv, wv, dyv, dr):
            _, vjp = jax.vjp(_rms, hv, wv)
            dh, dw = vjp(dyv)
            dh = dh + dr
            return dh, dh, dw
        return _rw(name, fn, [("r", h, d, _c0), ("p", w, d, _c0), ("r", dy, d, _c0), ("r", dres, d, _c0)],
                   [("r", d, d, _c0, F32), ("r", d, d, _c0, MXU), ("p", 1, d, d, _c0)], t_rows, tm_rw)

    dh1, dh1_m, g_norm_ffn = norm_bwd("norm_ffn_bwd", h1, norm_ffn_w, d_hn2, dh2)

    gw_out = _mm("gw_out", mixed, dh1_m, ta=True, tm=1024, tn=1024, tk=2080, out_dtype=WIRE)
    x_out = _exchange_start("grad_out_start", [], [_to_shards(gw_out, 0).astype(WIRE)], gw_out)
    d_mixed = _mm("d_mixed", dh1_m, wout, tb=True, dep=x_out[3])

    gw_seg = {}

    def gw_in(seg, dseg_arr):
        gw_seg[seg] = _mm("gw_in_" + seg, hn1, dseg_arr, ta=True, tm=1024, tn=1024, tk=2080, out_dtype=WIRE)

    def m2_out_bwd(rows, j, ys, xs, z, dl, nw, dy):
        _, vjp = jax.vjp(m2_out, ys, xs, z, dl, nw)
        return vjp(dy)

    dy_ssd, dxs_skip, d_m2z, g_d_lanes, g_m2_norm = _rw(
        "m2_out_bwd", m2_out_bwd, m2_out_ins + [("r", d_mixed, d, lambda j: 1)],
        [("r", d, d, _c0, F32), ("r", d, d, _c0, F32), ("r", d, d, _c0, MXU), ("p", 1, d, d, _c0), ("p", 1, d, d, _c0)],
        t_rows, _pick(t_rows, 208, 16))
    gw_in("m2z", d_m2z)

    def fold_heads(vec_ref, out_ref):
        r = lax.broadcasted_iota(jnp.int32, (d, LANE), 0)
        c = lax.broadcasted_iota(jnp.int32, (d, LANE), 1)
        out_ref[...] = _dgh(vec_ref[...], jnp.where(jnp.logical_and(r >= c * M2P, r < (c + 1) * M2P), 1.0, 0.0), 1, 0)

    g_m2_d = pl.pallas_call(fold_heads, name="fold_m2_d", out_shape=jax.ShapeDtypeStruct((1, LANE), F32))(g_d_lanes)

    dxs, db_ssd, dc_ssd, ddt_r, dam_r = _ssd_bwd(xbc_act, *m2_rows(SSD_GPS_B), m2_states, dy_ssd, d, SSD_GPS_B)


    def m2_prep_bwd(rows, own, last, wins, pars):
        (p, *ds), (w, b) = wins, pars
        _, vjp = jax.vjp(_silu, _conv(p, w) + b[0])
        dpre, = vjp(jnp.where(valid(rows), functools.reduce(lambda a_, b_: a_ + b_, ds), 0.0))
        dpre_own = jnp.where(own, dpre, 0.0)
        return [_conv_t(dpre, w, rows, t_end(last))], [_conv_w(dpre_own, p, len(w)), [jnp.sum(dpre_own, axis=0, keepdims=True)]]

    def m2_prep_bwd_call(name, off, width, d_arrs):
        at = lambda j, blk0=off // LANE: blk0 + j
        return _cv(name, m2_prep_bwd, [(proj["xbc"], at)] + [(a, _cj) for a in d_arrs], [(m2conv_f, at), (m2_conv_b, at)],
                   [(width, _cj, MXU)], [(m2conv_f.shape[0], width, _cj), (1, width, _cj)], t_rows, width // LANE, chunk=conv_chunk)

    dp_xs, gcw_xs, gcb_xs = m2_prep_bwd_call("m2_prep_bwd_x", 0, d, [dxs, dxs_skip])
    dp_b, gcw_b, gcb_b = m2_prep_bwd_call("m2_prep_bwd_b", d, M2G * NST, [db_ssd])
    dp_c, gcw_c, gcb_c = m2_prep_bwd_call("m2_prep_bwd_c", d + M2G * NST, M2G * NST, [dc_ssd])
    d_pxbc = jnp.concatenate([dp_xs, dp_b, dp_c], axis=1)
    gw_in("xbc", d_pxbc)
    g_m2_conv = jnp.concatenate([gcw_xs, gcw_b, gcw_c], axis=1)
    g_m2_conv_b = jnp.concatenate([gcb_xs, gcb_b, gcb_c], axis=1)

    def dn_out_bwd(rows, j, o, z, w, dy):
        _, vjp = jax.vjp(dn_out, o, z, w)
        return vjp(dy)

    d_o, d_z, g_dn_norm = _rw(
        "dn_out_bwd", dn_out_bwd,
        [("r", o_dn, d, _c0), ("r", proj["z"], d, _c0), ("p", dn_norm_w, HD, _c0), ("r", d_mixed, d, _c0)],
        [("r", d, d, _c0, F32), ("r", d, d, _c0, MXU), ("p", 1, HD, HD, _c0)], t_rows, _pick(t_rows, 208, 16))
    gw_in("z", d_z)

    dq, dk, dv, dbeta_r, dgdec_r = _gdn_bwd(q_act, k_act, v_act, head_rows(g1[:, :dnh], hb_b),
                                            head_rows(g1[:, dnh:2 * dnh], hb_b), dn_states, dn_tinv, d_o, hb_b)

    def dn_prep_bwd(sec, name, dact):
        def fn(rows, own, last, wins, pars):
            (p, da), (w,) = wins, pars
            _, vjp = jax.vjp(functools.partial(dn_post, sec), _conv(p, w))
            dcv, = vjp(jnp.where(valid(rows), da, 0.0))
            return [_conv_t(dcv, w, rows, t_end(last))], [_conv_w(jnp.where(own, dcv, 0.0), p, len(w))]
        wc = dnconv_f[:, sec * d:(sec + 1) * d]
        return _cv("dn_prep_bwd_" + name, fn, [(proj[name], _cj), (dact, _cj)], [(wc, _cj)], [(d, _cj, MXU)],
                   [(wc.shape[0], d, _cj)], t_rows, dnh, chunk=conv_chunk if sec == 2 else None)

    (dp_q, gcw_q), (dp_k, gcw_k), (dp_v, gcw_v) = dn_prep_bwd(0, "q", dq), dn_prep_bwd(1, "k", dk), dn_prep_bwd(2, "v", dv)
    gw_in("q", dp_q), gw_in("k", dp_k), gw_in("v", dp_v)
    g_dn_conv = jnp.concatenate([gcw_q, gcw_k, gcw_v], axis=1)

    zpad = jnp.zeros((t_rows, LANE - 2 * dnh - m2h), F32)
    dg1 = jnp.concatenate([head_cols(dbeta_r), head_cols(dgdec_r), head_cols(ddt_r), zpad], axis=1)
    dg2 = jnp.concatenate([jnp.zeros((t_rows, 2 * dnh), F32), head_cols(dam_r), zpad], axis=1)

    def gates_bwd(rows, j, sm, pa, pb, pc, pd, d1, d2):
        _, vjp = jax.vjp(lambda *a: gates(rows, *a), sm, pa, pb, pc, pd)
        return vjp((d1, d2))

    dp_sm, g_pa, g_pb, g_pc, g_pd = _rw(
        "gates_bwd", gates_bwd, gate_ins + [("r", dg1, LANE, _c0), ("r", dg2, LANE, _c0)],
        [("r", LANE, LANE, _c0, MXU)] + [("p", 1, LANE, LANE, _c0)] * 4, t_rows, tm_rw)

    dseg = {"q": dp_q, "k": dp_k, "v": dp_v, "z": d_z, "m2z": d_m2z, "xbc": d_pxbc, "sm": dp_sm}
    gw_in("sm", dp_sm)
    gsm = gw_seg["sm"]
    gw_in_full = jnp.concatenate([gw_seg["q"], gw_seg["k"], gw_seg["v"], gw_seg["z"], gsm[:, :2 * dnh], gw_seg["m2z"],
                                  gw_seg["xbc"], gsm[:, 2 * dnh:2 * dnh + m2h]], axis=1)
    x_in = _exchange_start("grad_in_start", [], [_to_shards(gw_in_full, 1).astype(WIRE)], gw_in_full)
    d_hn1 = None
    for s in dseg:
        d_hn1 = _mm("d_hn1_" + s, dseg[s], w_all_t, b_rows=seg_cols[s], tk=2048, add=d_hn1, dep=x_in[3])
    dh0, _, g_norm_mix = norm_bwd("norm_mix_bwd", h0, norm_mix_w, d_hn1, dh1)

    g_ffn_conv = jnp.concatenate([g_fc_g, g_fc_v], axis=1)
    small_parts = [_to_shards(dh0[PADR:CH], 1), _to_shards(g_dn_conv, 1), _to_shards(g_m2_conv, 1), _to_shards(g_ffn_conv, 1)]
    small_scatter = jnp.stack([_pack([p[k] for p in small_parts]) for k in range(NDEV)])

    rep_names = ["norm_mix_w", "dn_a_log", "dn_dt_bias", "dn_norm_w", "m2_conv_b", "m2_a_log", "m2_dt_bias", "m2_d",
                 "m2_norm_w", "norm_ffn_w", "norm_final_w"]
    rep_grads = [g_norm_mix, g_pa[:, dnh:2 * dnh], g_pb[:, dnh:2 * dnh], g_dn_norm, g_m2_conv_b, g_pc[:, 2 * dnh:2 * dnh + m2h],
                 g_pd[:, 2 * dnh:2 * dnh + m2h], g_m2_d[:, :m2h], g_m2_norm, g_norm_ffn, d_wf.reshape(d)]

    weights = dict(meta_tokens=meta_tokens, norm_mix_w=norm_mix_w, w_in=w_in, dn_conv_w=dn_conv_w, dn_a_log=dn_a_log,
                   dn_dt_bias=dn_dt_bias, dn_norm_w=dn_norm_w, m2_conv_w=m2_conv_w, m2_conv_b=m2_conv_b, m2_a_log=m2_a_log,
                   m2_dt_bias=m2_dt_bias, m2_d=m2_d, m2_norm_w=m2_norm_w, w_out=w_out, norm_ffn_w=norm_ffn_w, ffn_up=ffn_up,
                   ffn_conv_w=ffn_conv_w, ffn_down=ffn_down, norm_final_w=norm_final_w)
    mom1 = dict(meta_tokens=m_meta_tokens, norm_mix_w=m_norm_mix_w, w_in=m_w_in, dn_conv_w=m_dn_conv_w, dn_a_log=m_dn_a_log,
                dn_dt_bias=m_dn_dt_bias, dn_norm_w=m_dn_norm_w, m2_conv_w=m_m2_conv_w, m2_conv_b=m_m2_conv_b,
                m2_a_log=m_m2_a_log, m2_dt_bias=m_m2_dt_bias, m2_d=m_m2_d, m2_norm_w=m_m2_norm_w, w_out=m_w_out,
                norm_ffn_w=m_norm_ffn_w, ffn_up=m_ffn_up, ffn_conv_w=m_ffn_conv_w, ffn_down=m_ffn_down,
                norm_final_w=m_norm_final_w)
    mom2 = dict(meta_tokens=v_meta_tokens, norm_mix_w=v_norm_mix_w, w_in=v_w_in, dn_conv_w=v_dn_conv_w, dn_a_log=v_dn_a_log,
                dn_dt_bias=v_dn_dt_bias, dn_norm_w=v_dn_norm_w, m2_conv_w=v_m2_conv_w, m2_conv_b=v_m2_conv_b,
                m2_a_log=v_m2_a_log, m2_dt_bias=v_m2_dt_bias, m2_d=v_m2_d, m2_norm_w=v_m2_norm_w, w_out=v_w_out,
                norm_ffn_w=v_norm_ffn_w, ffn_up=v_ffn_up, ffn_conv_w=v_ffn_conv_w, ffn_down=v_ffn_down,
                norm_final_w=v_norm_final_w)
    res = {}

    def adam_big(name, started, after):
        staged, = _exchange_wait("grad_" + name + "_wait", started, after)
        outs = _adamw("adamw_" + name, staged, weights[name][0], mom1[name][0], mom2[name][0])
        res[name] = tuple(o[None] for o in outs)
        return outs[1]

    done = adam_big("ffn_down", x_down, dh0)
    done = adam_big("ffn_up", x_up, done)
    done = adam_big("w_out", x_out, done)
    st_rep, st_small = _exchange("exchange_small_grads", [_pack(rep_grads)], [small_scatter], done)
    adam_big("w_in", x_in, st_small)

    def adam_packed(label, staged, names):
        shapes = [weights[nm].shape for nm in names]
        outs = _adamw(label, staged, *[_pack([src[nm] for nm in names]) for src in (weights, mom1, mom2)])
        unpacked = [_unpack(o, shapes) for o in outs]
        for i, nm in enumerate(names):
            res[nm] = tuple(u[i] for u in unpacked)

    adam_packed("adamw_small_sharded", st_small, ["meta_tokens", "dn_conv_w", "m2_conv_w", "ffn_conv_w"])
    adam_packed("adamw_replicated", st_rep, rep_names)

    order = list(weights)
    grad_x = dh0[CH:][None]
    return (loss, grad_x, *[res[nm][0] for nm in order], *[res[nm][1] for nm in order], *[res[nm][2] for nm in order],
            *[res[nm][3] for nm in order])
```
